```python
import math
import jax, jax.numpy as jnp
from jax import lax
import numpy as np

D_MODEL = 1024
BATCH = 8
SEQ = 8192
DEPTH = 4

N_A_LAYERS = DEPTH // 2
N_B_LAYERS = DEPTH - N_A_LAYERS
HEAD_DIM = 128
GDN_HEADS = 6
GDN_WIDTH = GDN_HEADS * HEAD_DIM
CONV_WIDTH = 4
CHUNK = 64
FOX_HEADS = 6
FOX_WIDTH = FOX_HEADS * HEAD_DIM
Q_BLOCK = 128
MEM_LEN = 256
MEM_HEADS = 4
MEM_HEAD_DIM = 64
MEM_WIDTH = MEM_HEADS * MEM_HEAD_DIM
MIX_WIDTH = GDN_WIDTH + MEM_WIDTH
A_IN_WIDTH = 4 * GDN_WIDTH + 2 * GDN_HEADS + MEM_WIDTH
B_IN_WIDTH = FOX_WIDTH + MEM_WIDTH
KV_WIDTH = 2 * FOX_WIDTH + FOX_HEADS
FFN_HIDDEN = 2816
EPS = 1e-6
NEG_INF = -1e30

kernel_name = "yoco_gdn_fox_macaron_memory"


def rmsnorm(x, gain):
    x32 = x.astype(jnp.float32)
    y = x32 * lax.rsqrt(jnp.mean(x32 * x32, axis=-1, keepdims=True) + EPS)
    return (y * gain.astype(jnp.float32)).astype(x.dtype)


def l2norm(x):
    x32 = x.astype(jnp.float32)
    return x32 * lax.rsqrt(jnp.sum(x32 * x32, axis=-1, keepdims=True) + EPS)


def swiglu(x, w_gate_up, w_down):
    gate, up = jnp.split(x @ w_gate_up, 2, axis=-1)
    return (jax.nn.silu(gate) * up) @ w_down


def causal_depthwise_conv(x, w):
    c = x.shape[-1]
    return lax.conv_general_dilated(
        x, w[:, None, :].astype(x.dtype), window_strides=(1,),
        padding=((CONV_WIDTH - 1, 0),), dimension_numbers=("NWC", "WIO", "NWC"),
        feature_group_count=c)


def chunk_gated_delta_rule(q, k, v, g, beta):
    bsz, s, h, dk = q.shape
    dv = v.shape[-1]
    n = s // CHUNK

    def chunks(t):
        return t.reshape(bsz, n, CHUNK, h, -1).transpose(0, 3, 1, 2, 4)

    q, k, v = chunks(q), chunks(k), chunks(v)
    g = g.reshape(bsz, n, CHUNK, h).transpose(0, 3, 1, 2)
    beta = beta.reshape(bsz, n, CHUNK, h).transpose(0, 3, 1, 2)
    g = jnp.cumsum(g, axis=-1)
    causal = jnp.tril(jnp.ones((CHUNK, CHUNK), dtype=bool))
    strict = jnp.tril(jnp.ones((CHUNK, CHUNK), dtype=bool), k=-1)
    gdiff = g[..., :, None] - g[..., None, :]
    decay = jnp.where(causal, jnp.exp(jnp.where(causal, gdiff, 0.0)), 0.0)
    k_beta = k * beta[..., None]
    lower = jnp.where(strict, jnp.einsum("bhnid,bhnjd->bhnij", k_beta, k) * decay, 0.0)
    system = lower + jnp.eye(CHUNK, dtype=jnp.float32)
    rhs = jnp.concatenate([v * beta[..., None], k_beta * jnp.exp(g)[..., None]], axis=-1)
    sol = lax.linalg.triangular_solve(system, rhs, left_side=True, lower=True,
                                      unit_diagonal=True)
    u_vals, w_keys = sol[..., :dv], sol[..., dv:]
    qk = jnp.where(causal, jnp.einsum("bhnid,bhnjd->bhnij", q, k) * decay, 0.0)
    g_last = g[..., -1]
    k_tail = k * jnp.exp(g_last[..., None] - g)[..., None]
    q_head = q * jnp.exp(g)[..., None]
    xs = tuple(jnp.moveaxis(t, 2, 0) for t in (q_head, qk, u_vals, w_keys, k_tail, g_last))

    def step(state, inp):
        q_c, qk_c, u_c, w_c, kt_c, gl_c = inp
        v_new = u_c - jnp.einsum("bhck,bhkv->bhcv", w_c, state)
        out = (jnp.einsum("bhck,bhkv->bhcv", q_c, state)
               + jnp.einsum("bhij,bhjv->bhiv", qk_c, v_new))
        state = (state * jnp.exp(gl_c)[..., None, None]
                 + jnp.einsum("bhck,bhcv->bhkv", kt_c, v_new))
        return state, out

    state0 = jnp.zeros((bsz, h, dk, dv), jnp.float32)
    _, out = lax.scan(step, state0, xs)
    return out.transpose(1, 0, 3, 2, 4).reshape(bsz, s, h, dv)


def gdn_mixer(u, w_in, conv_w, A_log, dt_bias, out_norm):
    bsz, s, _ = u.shape
    proj = u @ w_in
    o0 = 3 * GDN_WIDTH
    o1 = 4 * GDN_WIDTH
    qkv = proj[..., :o0]
    z = proj[..., o0:o1]
    a = proj[..., o1:o1 + GDN_HEADS]
    b = proj[..., o1 + GDN_HEADS:o1 + 2 * GDN_HEADS]
    q_mem = proj[..., o1 + 2 * GDN_HEADS:]
    qkv = jax.nn.silu(causal_depthwise_conv(qkv, conv_w))
    qkv = qkv.reshape(bsz, s, 3, GDN_HEADS, HEAD_DIM)
    q = l2norm(qkv[:, :, 0]) * (HEAD_DIM ** -0.5)
    k = l2norm(qkv[:, :, 1])
    v = qkv[:, :, 2].astype(jnp.float32)
    beta = jax.nn.sigmoid(b.astype(jnp.float32))
    g = -jnp.exp(A_log.astype(jnp.float32)) * jax.nn.softplus(
        a.astype(jnp.float32) + dt_bias.astype(jnp.float32))
    o = chunk_gated_delta_rule(q, k, v, g, beta)
    z = z.reshape(bsz, s, GDN_HEADS, HEAD_DIM).astype(jnp.float32)
    o = rmsnorm(o, out_norm) * jax.nn.silu(z)
    return o.reshape(bsz, s, GDN_WIDTH).astype(u.dtype), q_mem


def shared_fox_kv(h, kv_norm, kv_w, kv_b_f):
    bsz, s, _ = h.shape
    p = rmsnorm(h, kv_norm) @ kv_w
    k = p[..., :FOX_WIDTH].reshape(bsz, s, FOX_HEADS, HEAD_DIM)
    v = p[..., FOX_WIDTH:2 * FOX_WIDTH].reshape(bsz, s, FOX_HEADS, HEAD_DIM)
    log_f = jax.nn.log_sigmoid(p[..., 2 * FOX_WIDTH:].astype(jnp.float32)
                               + kv_b_f.astype(jnp.float32))
    c = jnp.cumsum(log_f, axis=1).transpose(0, 2, 1)
    return k, v, c


def forgetting_attention(q, k, v, c):
    bsz, s, h, d = q.shape
    nb = s // Q_BLOCK
    scale = d ** -0.5
    q_blocks = q.reshape(bsz, nb, Q_BLOCK, h, d).transpose(1, 0, 3, 2, 4)
    c_blocks = c.reshape(bsz, h, nb, Q_BLOCK).transpose(2, 0, 1, 3)
    key_pos = jnp.arange(s)

    def attend(args):
        q_i, c_i, i = args
        logits = jnp.einsum("bhqd,bshd->bhqs", q_i, k).astype(jnp.float32) * scale
        logits = logits + c_i[..., None] - c[:, :, None, :]
        q_pos = i * Q_BLOCK + jnp.arange(Q_BLOCK)
        causal = key_pos[None, :] <= q_pos[:, None]
        p = jax.nn.softmax(jnp.where(causal, logits, NEG_INF), axis=-1)
        return jnp.einsum("bhqs,bshd->bqhd", p.astype(v.dtype), v)

    out = lax.map(attend, (q_blocks, c_blocks, jnp.arange(nb)))
    return out.transpose(1, 0, 2, 3, 4).reshape(bsz, s, h * d).astype(q.dtype)


def memory_attention(q, mem_n, w_kv):
    bsz, s, _ = q.shape
    m = mem_n.shape[1]
    kv = mem_n @ w_kv
    k = kv[..., :MEM_WIDTH].reshape(bsz, m, MEM_HEADS, MEM_HEAD_DIM)
    v = kv[..., MEM_WIDTH:].reshape(bsz, m, MEM_HEADS, MEM_HEAD_DIM)
    q = q.reshape(bsz, s, MEM_HEADS, MEM_HEAD_DIM)
    logits = jnp.einsum("bqhd,bmhd->bhqm", q, k).astype(jnp.float32) * (MEM_HEAD_DIM ** -0.5)
    p = jax.nn.softmax(logits, axis=-1)
    out = jnp.einsum("bhqm,bmhd->bqhd", p.astype(v.dtype), v)
    return out.reshape(bsz, s, MEM_WIDTH)


def _fwd_setup_inputs(seed: int = 0) -> dict:
    key = jax.random.key(seed)
    ks = iter(jax.random.split(key, 32))

    def nrm(shape, scale):
        return scale * jax.random.normal(next(ks), shape, jnp.float32)

    def gain(shape):
        return 1.0 + nrm(shape, 0.02)

    x = nrm((BATCH, SEQ, D_MODEL), 1.0)
    mem = nrm((BATCH, MEM_LEN, D_MODEL), 1.0)
    ffn1_norm = gain((DEPTH, D_MODEL))
    ffn1_w_gate_up = nrm((DEPTH, D_MODEL, 2 * FFN_HIDDEN), D_MODEL ** -0.5)
    ffn1_w_down = nrm((DEPTH, FFN_HIDDEN, D_MODEL), FFN_HIDDEN ** -0.5)
    mix_norm = gain((DEPTH, D_MODEL))
    ffn2_norm = gain((DEPTH, D_MODEL))
    ffn2_w_gate_up = nrm((DEPTH, D_MODEL, 2 * FFN_HIDDEN), D_MODEL ** -0.5)
    ffn2_w_down = nrm((DEPTH, FFN_HIDDEN, D_MODEL), FFN_HIDDEN ** -0.5)
    gdn_w_in = nrm((N_A_LAYERS, D_MODEL, A_IN_WIDTH), D_MODEL ** -0.5)
    gdn_conv = nrm((N_A_LAYERS, CONV_WIDTH, 3 * GDN_WIDTH), CONV_WIDTH ** -0.5)
    gdn_A_log = jnp.log(jax.random.uniform(next(ks), (N_A_LAYERS, GDN_HEADS),
                                           jnp.float32, 1.0, 16.0))
    dt = jnp.exp(jax.random.uniform(next(ks), (N_A_LAYERS, GDN_HEADS), jnp.float32,
                                    math.log(1e-3), math.log(1e-1)))
    gdn_dt_bias = dt + jnp.log(-jnp.expm1(-dt))
    gdn_out_norm = gain((N_A_LAYERS, HEAD_DIM))
    fox_w_in = nrm((N_B_LAYERS, D_MODEL, B_IN_WIDTH), D_MODEL ** -0.5)
    w_out = nrm((DEPTH, MIX_WIDTH, D_MODEL), MIX_WIDTH ** -0.5)
    mem_norm = gain((D_MODEL,))
    mem_w_kv = nrm((DEPTH, D_MODEL, 2 * MEM_WIDTH), D_MODEL ** -0.5)
    kv_norm = gain((D_MODEL,))
    kv_w = nrm((D_MODEL, KV_WIDTH), D_MODEL ** -0.5)
    kv_b_f = 2.0 + nrm((FOX_HEADS,), 0.1)
    final_norm = gain((D_MODEL,))
    return {"x": x, "mem": mem,
            "ffn1_norm": ffn1_norm, "ffn1_w_gate_up": ffn1_w_gate_up, "ffn1_w_down": ffn1_w_down,
            "mix_norm": mix_norm,
            "ffn2_norm": ffn2_norm, "ffn2_w_gate_up": ffn2_w_gate_up, "ffn2_w_down": ffn2_w_down,
            "gdn_w_in": gdn_w_in, "gdn_conv": gdn_conv, "gdn_A_log": gdn_A_log,
            "gdn_dt_bias": gdn_dt_bias, "gdn_out_norm": gdn_out_norm,
            "fox_w_in": fox_w_in, "w_out": w_out,
            "mem_norm": mem_norm, "mem_w_kv": mem_w_kv,
            "kv_norm": kv_norm, "kv_w": kv_w, "kv_b_f": kv_b_f,
            "final_norm": final_norm}


def _fwd_reference(x, mem, ffn1_norm, ffn1_w_gate_up, ffn1_w_down, mix_norm,
              ffn2_norm, ffn2_w_gate_up, ffn2_w_down,
              gdn_w_in, gdn_conv, gdn_A_log, gdn_dt_bias, gdn_out_norm,
              fox_w_in, w_out, mem_norm, mem_w_kv, kv_norm, kv_w, kv_b_f, final_norm):
    bsz, s, _ = x.shape
    mem_n = rmsnorm(mem, mem_norm)
    h = x
    shared_k = shared_v = shared_c = None
    for l in range(DEPTH):
        h = h + 0.5 * swiglu(rmsnorm(h, ffn1_norm[l]), ffn1_w_gate_up[l], ffn1_w_down[l])
        u = rmsnorm(h, mix_norm[l])
        if l < N_A_LAYERS:
            main, q_mem = gdn_mixer(u, gdn_w_in[l], gdn_conv[l], gdn_A_log[l],
                                    gdn_dt_bias[l], gdn_out_norm[l])
        else:
            proj = u @ fox_w_in[l - N_A_LAYERS]
            q_fox = proj[..., :FOX_WIDTH].reshape(bsz, s, FOX_HEADS, HEAD_DIM)
            q_mem = proj[..., FOX_WIDTH:]
            main = forgetting_attention(q_fox, shared_k, shared_v, shared_c)
        mem_out = memory_attention(q_mem, mem_n, mem_w_kv[l])
        h = h + jnp.concatenate([main, mem_out], axis=-1) @ w_out[l]
        h = h + 0.5 * swiglu(rmsnorm(h, ffn2_norm[l]), ffn2_w_gate_up[l], ffn2_w_down[l])
        if l == N_A_LAYERS - 1:
            shared_k, shared_v, shared_c = shared_fox_kv(h, kv_norm, kv_w, kv_b_f)
    return rmsnorm(h, final_norm)


import jax as _jax
import jax.numpy as _jnp

TWIN_FORMAT = 'train_step'
FWD_PARAMS = ['x', 'mem', 'ffn1_norm', 'ffn1_w_gate_up', 'ffn1_w_down', 'mix_norm', 'ffn2_norm', 'ffn2_w_gate_up', 'ffn2_w_down', 'gdn_w_in', 'gdn_conv', 'gdn_A_log', 'gdn_dt_bias', 'gdn_out_norm', 'fox_w_in', 'w_out', 'mem_norm', 'mem_w_kv', 'kv_norm', 'kv_w', 'kv_b_f', 'final_norm']
TWIN_WEIGHTS = ['ffn1_norm', 'ffn1_w_gate_up', 'ffn1_w_down', 'mix_norm', 'ffn2_norm', 'ffn2_w_gate_up', 'ffn2_w_down', 'gdn_w_in', 'gdn_conv', 'gdn_A_log', 'gdn_dt_bias', 'gdn_out_norm', 'fox_w_in', 'w_out', 'mem_norm', 'mem_w_kv', 'kv_norm', 'kv_w', 'kv_b_f', 'final_norm']
TWIN_DIFF_INPUT = 'x'
TWIN_INPUTS = ['x', 'mem', 'ffn1_norm', 'ffn1_w_gate_up', 'ffn1_w_down', 'mix_norm', 'ffn2_norm', 'ffn2_w_gate_up', 'ffn2_w_down', 'gdn_w_in', 'gdn_conv', 'gdn_A_log', 'gdn_dt_bias', 'gdn_out_norm', 'fox_w_in', 'w_out', 'mem_norm', 'mem_w_kv', 'kv_norm', 'kv_w', 'kv_b_f', 'final_norm', 'loss_target', 'm_ffn1_norm', 'm_ffn1_w_gate_up', 'm_ffn1_w_down', 'm_mix_norm', 'm_ffn2_norm', 'm_ffn2_w_gate_up', 'm_ffn2_w_down', 'm_gdn_w_in', 'm_gdn_conv', 'm_gdn_A_log', 'm_gdn_dt_bias', 'm_gdn_out_norm', 'm_fox_w_in', 'm_w_out', 'm_mem_norm', 'm_mem_w_kv', 'm_kv_norm', 'm_kv_w', 'm_kv_b_f', 'm_final_norm', 'v_ffn1_norm', 'v_ffn1_w_gate_up', 'v_ffn1_w_down', 'v_mix_norm', 'v_ffn2_norm', 'v_ffn2_w_gate_up', 'v_ffn2_w_down', 'v_gdn_w_in', 'v_gdn_conv', 'v_gdn_A_log', 'v_gdn_dt_bias', 'v_gdn_out_norm', 'v_fox_w_in', 'v_w_out', 'v_mem_norm', 'v_mem_w_kv', 'v_kv_norm', 'v_kv_w', 'v_kv_b_f', 'v_final_norm']
TWIN_OUTPUTS = ['loss', 'grad_x', 'grad_ffn1_norm', 'grad_ffn1_w_gate_up', 'grad_ffn1_w_down', 'grad_mix_norm', 'grad_ffn2_norm', 'grad_ffn2_w_gate_up', 'grad_ffn2_w_down', 'grad_gdn_w_in', 'grad_gdn_conv', 'grad_gdn_A_log', 'grad_gdn_dt_bias', 'grad_gdn_out_norm', 'grad_fox_w_in', 'grad_w_out', 'grad_mem_norm', 'grad_mem_w_kv', 'grad_kv_norm', 'grad_kv_w', 'grad_kv_b_f', 'grad_final_norm', 'delta_ffn1_norm', 'delta_ffn1_w_gate_up', 'delta_ffn1_w_down', 'delta_mix_norm', 'delta_ffn2_norm', 'delta_ffn2_w_gate_up', 'delta_ffn2_w_down', 'delta_gdn_w_in', 'delta_gdn_conv', 'delta_gdn_A_log', 'delta_gdn_dt_bias', 'delta_gdn_out_norm', 'delta_fox_w_in', 'delta_w_out', 'delta_mem_norm', 'delta_mem_w_kv', 'delta_kv_norm', 'delta_kv_w', 'delta_kv_b_f', 'delta_final_norm', 'new_m_ffn1_norm', 'new_m_ffn1_w_gate_up', 'new_m_ffn1_w_down', 'new_m_mix_norm', 'new_m_ffn2_norm', 'new_m_ffn2_w_gate_up', 'new_m_ffn2_w_down', 'new_m_gdn_w_in', 'new_m_gdn_conv', 'new_m_gdn_A_log', 'new_m_gdn_dt_bias', 'new_m_gdn_out_norm', 'new_m_fox_w_in', 'new_m_w_out', 'new_m_mem_norm', 'new_m_mem_w_kv', 'new_m_kv_norm', 'new_m_kv_w', 'new_m_kv_b_f', 'new_m_final_norm', 'new_v_ffn1_norm', 'new_v_ffn1_w_gate_up', 'new_v_ffn1_w_down', 'new_v_mix_norm', 'new_v_ffn2_norm', 'new_v_ffn2_w_gate_up', 'new_v_ffn2_w_down', 'new_v_gdn_w_in', 'new_v_gdn_conv', 'new_v_gdn_A_log', 'new_v_gdn_dt_bias', 'new_v_gdn_out_norm', 'new_v_fox_w_in', 'new_v_w_out', 'new_v_mem_norm', 'new_v_mem_w_kv', 'new_v_kv_norm', 'new_v_kv_w', 'new_v_kv_b_f', 'new_v_final_norm']
TWIN_LEAF_KINDS = {'loss': 'loss', 'grad_x': 'grad_x', 'grad_ffn1_norm': 'grad_w', 'grad_ffn1_w_gate_up': 'grad_w', 'grad_ffn1_w_down': 'grad_w', 'grad_mix_norm': 'grad_w', 'grad_ffn2_norm': 'grad_w', 'grad_ffn2_w_gate_up': 'grad_w', 'grad_ffn2_w_down': 'grad_w', 'grad_gdn_w_in': 'grad_w', 'grad_gdn_conv': 'grad_w', 'grad_gdn_A_log': 'grad_w', 'grad_gdn_dt_bias': 'grad_w', 'grad_gdn_out_norm': 'grad_w', 'grad_fox_w_in': 'grad_w', 'grad_w_out': 'grad_w', 'grad_mem_norm': 'grad_w', 'grad_mem_w_kv': 'grad_w', 'grad_kv_norm': 'grad_w', 'grad_kv_w': 'grad_w', 'grad_kv_b_f': 'grad_w', 'grad_final_norm': 'grad_w', 'delta_ffn1_norm': 'delta_w', 'delta_ffn1_w_gate_up': 'delta_w', 'delta_ffn1_w_down': 'delta_w', 'delta_mix_norm': 'delta_w', 'delta_ffn2_norm': 'delta_w', 'delta_ffn2_w_gate_up': 'delta_w', 'delta_ffn2_w_down': 'delta_w', 'delta_gdn_w_in': 'delta_w', 'delta_gdn_conv': 'delta_w', 'delta_gdn_A_log': 'delta_w', 'delta_gdn_dt_bias': 'delta_w', 'delta_gdn_out_norm': 'delta_w', 'delta_fox_w_in': 'delta_w', 'delta_w_out': 'delta_w', 'delta_mem_norm': 'delta_w', 'delta_mem_w_kv': 'delta_w', 'delta_kv_norm': 'delta_w', 'delta_kv_w': 'delta_w', 'delta_kv_b_f': 'delta_w', 'delta_final_norm': 'delta_w', 'new_m_ffn1_norm': 'new_m', 'new_m_ffn1_w_gate_up': 'new_m', 'new_m_ffn1_w_down': 'new_m', 'new_m_mix_norm': 'new_m', 'new_m_ffn2_norm': 'new_m', 'new_m_ffn2_w_gate_up': 'new_m', 'new_m_ffn2_w_down': 'new_m', 'new_m_gdn_w_in': 'new_m', 'new_m_gdn_conv': 'new_m', 'new_m_gdn_A_log': 'new_m', 'new_m_gdn_dt_bias': 'new_m', 'new_m_gdn_out_norm': 'new_m', 'new_m_fox_w_in': 'new_m', 'new_m_w_out': 'new_m', 'new_m_mem_norm': 'new_m', 'new_m_mem_w_kv': 'new_m', 'new_m_kv_norm': 'new_m', 'new_m_kv_w': 'new_m', 'new_m_kv_b_f': 'new_m', 'new_m_final_norm': 'new_m', 'new_v_ffn1_norm': 'new_v', 'new_v_ffn1_w_gate_up': 'new_v', 'new_v_ffn1_w_down': 'new_v', 'new_v_mix_norm': 'new_v', 'new_v_ffn2_norm': 'new_v', 'new_v_ffn2_w_gate_up': 'new_v', 'new_v_ffn2_w_down': 'new_v', 'new_v_gdn_w_in': 'new_v', 'new_v_gdn_conv': 'new_v', 'new_v_gdn_A_log': 'new_v', 'new_v_gdn_dt_bias': 'new_v', 'new_v_gdn_out_norm': 'new_v', 'new_v_fox_w_in': 'new_v', 'new_v_w_out': 'new_v', 'new_v_mem_norm': 'new_v', 'new_v_mem_w_kv': 'new_v', 'new_v_kv_norm': 'new_v', 'new_v_kv_w': 'new_v', 'new_v_kv_b_f': 'new_v', 'new_v_final_norm': 'new_v'}


def _forward(args):
    return _fwd_reference(*[args[k] for k in FWD_PARAMS])


def _output_shape():
    def fwd():
        inp = _fwd_setup_inputs(0)
        return _fwd_reference(*[inp[k] for k in FWD_PARAMS])
    out = _jax.eval_shape(fwd)
    return out.shape, out.dtype

N_MICROBATCH = 1
ADAM_LR = 0.001
ADAM_B1 = 0.9
ADAM_B2 = 0.999
ADAM_EPS = 1e-08
ADAM_WD = 0.01
ADAM_STEP = 10
PER_EXAMPLE_BATCH_AXIS = {'x': 0, 'mem': 0, 'loss_target': 0}
SHARED_INPUTS = []
_WEIGHT_DTYPES = {'ffn1_norm': _jnp.float32, 'ffn1_w_gate_up': _jnp.float32, 'ffn1_w_down': _jnp.float32, 'mix_norm': _jnp.float32, 'ffn2_norm': _jnp.float32, 'ffn2_w_gate_up': _jnp.float32, 'ffn2_w_down': _jnp.float32, 'gdn_w_in': _jnp.float32, 'gdn_conv': _jnp.float32, 'gdn_A_log': _jnp.float32, 'gdn_dt_bias': _jnp.float32, 'gdn_out_norm': _jnp.float32, 'fox_w_in': _jnp.float32, 'w_out': _jnp.float32, 'mem_norm': _jnp.float32, 'mem_w_kv': _jnp.float32, 'kv_norm': _jnp.float32, 'kv_w': _jnp.float32, 'kv_b_f': _jnp.float32, 'final_norm': _jnp.float32}
MOMENT_SCALE = {'ffn1_norm': 1.171245e-01, 'ffn1_w_gate_up': 4.899262e-02, 'ffn1_w_down': 8.007253e-02, 'mix_norm': 1.620564e-01, 'ffn2_norm': 9.152919e-02, 'ffn2_w_gate_up': 3.897075e-02, 'ffn2_w_down': 6.368408e-02, 'gdn_w_in': 1.245829e-01, 'gdn_conv': 1.170142e-01, 'gdn_A_log': 4.699765e-01, 'gdn_dt_bias': 4.628609e-01, 'gdn_out_norm': 3.785960e-01, 'fox_w_in': 4.708819e-02, 'w_out': 1.079282e-01, 'mem_norm': 3.641120e-02, 'mem_w_kv': 2.453863e-02, 'kv_norm': 1.154507e-01, 'kv_w': 9.239880e-02, 'kv_b_f': 6.753457e-01, 'final_norm': 6.406261e+01}


def _to_microbatches(a, axis):
    t = _jnp.moveaxis(a, axis, 0)
    t = t.reshape((N_MICROBATCH, t.shape[0] // N_MICROBATCH) + t.shape[1:])
    return _jnp.moveaxis(t, 1, axis + 1)


def setup_inputs(seed: int = 0) -> dict:
    inp = _fwd_setup_inputs(seed)
    key = _jax.random.fold_in(_jax.random.key(seed), 7919)
    shape, _ = _output_shape()
    out = dict(inp)
    out["loss_target"] = _jax.random.normal(_jax.random.fold_in(key, 0), shape, _jnp.float32)
    for i, name in enumerate(TWIN_WEIGHTS):
        w = inp[name].astype(_jnp.float32)
        if MOMENT_SCALE is None:
            s = _jnp.sqrt(_jnp.mean(_jnp.square(w)) + 1e-30)
        else:
            s = MOMENT_SCALE[name]
        km, kv = _jax.random.split(_jax.random.fold_in(key, i + 1))
        out[name] = w
        out["m_" + name] = s * _jax.random.normal(km, w.shape, _jnp.float32)
        out["v_" + name] = (s * s) * _jax.random.uniform(kv, w.shape, _jnp.float32, 0.5, 1.5)
    if N_MICROBATCH > 1:
        for name, axis in PER_EXAMPLE_BATCH_AXIS.items():
            out[name] = _to_microbatches(out[name], axis)
    return {'x': out['x'], 'mem': out['mem'], 'ffn1_norm': out['ffn1_norm'], 'ffn1_w_gate_up': out['ffn1_w_gate_up'], 'ffn1_w_down': out['ffn1_w_down'], 'mix_norm': out['mix_norm'], 'ffn2_norm': out['ffn2_norm'], 'ffn2_w_gate_up': out['ffn2_w_gate_up'], 'ffn2_w_down': out['ffn2_w_down'], 'gdn_w_in': out['gdn_w_in'], 'gdn_conv': out['gdn_conv'], 'gdn_A_log': out['gdn_A_log'], 'gdn_dt_bias': out['gdn_dt_bias'], 'gdn_out_norm': out['gdn_out_norm'], 'fox_w_in': out['fox_w_in'], 'w_out': out['w_out'], 'mem_norm': out['mem_norm'], 'mem_w_kv': out['mem_w_kv'], 'kv_norm': out['kv_norm'], 'kv_w': out['kv_w'], 'kv_b_f': out['kv_b_f'], 'final_norm': out['final_norm'], 'loss_target': out['loss_target'], 'm_ffn1_norm': out['m_ffn1_norm'], 'm_ffn1_w_gate_up': out['m_ffn1_w_gate_up'], 'm_ffn1_w_down': out['m_ffn1_w_down'], 'm_mix_norm': out['m_mix_norm'], 'm_ffn2_norm': out['m_ffn2_norm'], 'm_ffn2_w_gate_up': out['m_ffn2_w_gate_up'], 'm_ffn2_w_down': out['m_ffn2_w_down'], 'm_gdn_w_in': out['m_gdn_w_in'], 'm_gdn_conv': out['m_gdn_conv'], 'm_gdn_A_log': out['m_gdn_A_log'], 'm_gdn_dt_bias': out['m_gdn_dt_bias'], 'm_gdn_out_norm': out['m_gdn_out_norm'], 'm_fox_w_in': out['m_fox_w_in'], 'm_w_out': out['m_w_out'], 'm_mem_norm': out['m_mem_norm'], 'm_mem_w_kv': out['m_mem_w_kv'], 'm_kv_norm': out['m_kv_norm'], 'm_kv_w': out['m_kv_w'], 'm_kv_b_f': out['m_kv_b_f'], 'm_final_norm': out['m_final_norm'], 'v_ffn1_norm': out['v_ffn1_norm'], 'v_ffn1_w_gate_up': out['v_ffn1_w_gate_up'], 'v_ffn1_w_down': out['v_ffn1_w_down'], 'v_mix_norm': out['v_mix_norm'], 'v_ffn2_norm': out['v_ffn2_norm'], 'v_ffn2_w_gate_up': out['v_ffn2_w_gate_up'], 'v_ffn2_w_down': out['v_ffn2_w_down'], 'v_gdn_w_in': out['v_gdn_w_in'], 'v_gdn_conv': out['v_gdn_conv'], 'v_gdn_A_log': out['v_gdn_A_log'], 'v_gdn_dt_bias': out['v_gdn_dt_bias'], 'v_gdn_out_norm': out['v_gdn_out_norm'], 'v_fox_w_in': out['v_fox_w_in'], 'v_w_out': out['v_w_out'], 'v_mem_norm': out['v_mem_norm'], 'v_mem_w_kv': out['v_mem_w_kv'], 'v_kv_norm': out['v_kv_norm'], 'v_kv_w': out['v_kv_w'], 'v_kv_b_f': out['v_kv_b_f'], 'v_final_norm': out['v_final_norm']}


def _loss(weights, diff, rest, loss_target):
    with _jax.named_scope("forward"):
        args = {**rest, TWIN_DIFF_INPUT: diff, **{k: w.astype(_WEIGHT_DTYPES[k]) for k, w in weights.items()}}
        y = _forward(args)
    with _jax.named_scope("loss_head"):
        err = _jnp.square(y.astype(_jnp.float32) - loss_target)
        return 0.5 * _jnp.sum(_jnp.mean(err, axis=-1)) if err.ndim else 0.5 * err


def _adamw(w, g, m, v):
    m = ADAM_B1 * m + (1.0 - ADAM_B1) * g
    v = ADAM_B2 * v + (1.0 - ADAM_B2) * _jnp.square(g)
    m_hat = m / (1.0 - ADAM_B1 ** ADAM_STEP)
    v_hat = v / (1.0 - ADAM_B2 ** ADAM_STEP)
    delta = -ADAM_LR * (m_hat / (_jnp.sqrt(v_hat) + ADAM_EPS) + ADAM_WD * w)
    return delta, m, v


def reference(x, mem, ffn1_norm, ffn1_w_gate_up, ffn1_w_down, mix_norm, ffn2_norm, ffn2_w_gate_up, ffn2_w_down, gdn_w_in, gdn_conv, gdn_A_log, gdn_dt_bias, gdn_out_norm, fox_w_in, w_out, mem_norm, mem_w_kv, kv_norm, kv_w, kv_b_f, final_norm, loss_target, m_ffn1_norm, m_ffn1_w_gate_up, m_ffn1_w_down, m_mix_norm, m_ffn2_norm, m_ffn2_w_gate_up, m_ffn2_w_down, m_gdn_w_in, m_gdn_conv, m_gdn_A_log, m_gdn_dt_bias, m_gdn_out_norm, m_fox_w_in, m_w_out, m_mem_norm, m_mem_w_kv, m_kv_norm, m_kv_w, m_kv_b_f, m_final_norm, v_ffn1_norm, v_ffn1_w_gate_up, v_ffn1_w_down, v_mix_norm, v_ffn2_norm, v_ffn2_w_gate_up, v_ffn2_w_down, v_gdn_w_in, v_gdn_conv, v_gdn_A_log, v_gdn_dt_bias, v_gdn_out_norm, v_fox_w_in, v_w_out, v_mem_norm, v_mem_w_kv, v_kv_norm, v_kv_w, v_kv_b_f, v_final_norm):
    given = dict(x=x, mem=mem, ffn1_norm=ffn1_norm, ffn1_w_gate_up=ffn1_w_gate_up, ffn1_w_down=ffn1_w_down, mix_norm=mix_norm, ffn2_norm=ffn2_norm, ffn2_w_gate_up=ffn2_w_gate_up, ffn2_w_down=ffn2_w_down, gdn_w_in=gdn_w_in, gdn_conv=gdn_conv, gdn_A_log=gdn_A_log, gdn_dt_bias=gdn_dt_bias, gdn_out_norm=gdn_out_norm, fox_w_in=fox_w_in, w_out=w_out, mem_norm=mem_norm, mem_w_kv=mem_w_kv, kv_norm=kv_norm, kv_w=kv_w, kv_b_f=kv_b_f, final_norm=final_norm, loss_target=loss_target, m_ffn1_norm=m_ffn1_norm, m_ffn1_w_gate_up=m_ffn1_w_gate_up, m_ffn1_w_down=m_ffn1_w_down, m_mix_norm=m_mix_norm, m_ffn2_norm=m_ffn2_norm, m_ffn2_w_gate_up=m_ffn2_w_gate_up, m_ffn2_w_down=m_ffn2_w_down, m_gdn_w_in=m_gdn_w_in, m_gdn_conv=m_gdn_conv, m_gdn_A_log=m_gdn_A_log, m_gdn_dt_bias=m_gdn_dt_bias, m_gdn_out_norm=m_gdn_out_norm, m_fox_w_in=m_fox_w_in, m_w_out=m_w_out, m_mem_norm=m_mem_norm, m_mem_w_kv=m_mem_w_kv, m_kv_norm=m_kv_norm, m_kv_w=m_kv_w, m_kv_b_f=m_kv_b_f, m_final_norm=m_final_norm, v_ffn1_norm=v_ffn1_norm, v_ffn1_w_gate_up=v_ffn1_w_gate_up, v_ffn1_w_down=v_ffn1_w_down, v_mix_norm=v_mix_norm, v_ffn2_norm=v_ffn2_norm, v_ffn2_w_gate_up=v_ffn2_w_gate_up, v_ffn2_w_down=v_ffn2_w_down, v_gdn_w_in=v_gdn_w_in, v_gdn_conv=v_gdn_conv, v_gdn_A_log=v_gdn_A_log, v_gdn_dt_bias=v_gdn_dt_bias, v_gdn_out_norm=v_gdn_out_norm, v_fox_w_in=v_fox_w_in, v_w_out=v_w_out, v_mem_norm=v_mem_norm, v_mem_w_kv=v_mem_w_kv, v_kv_norm=v_kv_norm, v_kv_w=v_kv_w, v_kv_b_f=v_kv_b_f, v_final_norm=v_final_norm)
    weights = {n: given[n] for n in TWIN_WEIGHTS}
    shared = {n: given[n] for n in SHARED_INPUTS}
    per_example = {n: given[n] for n in ['x', 'mem']}
    grad_fn = _jax.value_and_grad(_loss, argnums=(0, 1))

    def one_microbatch(ex, loss_target):
        ex = dict(ex)
        diff = ex.pop(TWIN_DIFF_INPUT)
        return grad_fn(weights, diff, {**shared, **ex}, loss_target)

    if N_MICROBATCH == 1:
        loss, (grad_w, grad_x) = one_microbatch(per_example, given["loss_target"])
    else:
        def body(carry, xs):
            loss_sum, grad_sum = carry
            l_k, (gw_k, gx_k) = one_microbatch(xs[0], xs[1])
            with _jax.named_scope("update"):
                return (loss_sum + l_k, _jax.tree.map(_jnp.add, grad_sum, gw_k)), gx_k

        init = (_jnp.zeros((), _jnp.float32), _jax.tree.map(_jnp.zeros_like, weights))
        (loss, grad_w), grad_x = _jax.lax.scan(body, init, (per_example, given["loss_target"]))
    with _jax.named_scope("update"):
        delta_w, new_m, new_v = {}, {}, {}
        for n in TWIN_WEIGHTS:
            delta_w[n], new_m[n], new_v[n] = _adamw(weights[n], grad_w[n], given["m_" + n], given["v_" + n])
    return (loss, grad_x, *[grad_w[n] for n in TWIN_WEIGHTS], *[delta_w[n] for n in TWIN_WEIGHTS],
            *[new_m[n] for n in TWIN_WEIGHTS], *[new_v[n] for n in TWIN_WEIGHTS])
```

```python
import functools
import math

import jax
import jax.numpy as jnp
from jax import lax
from jax.experimental import pallas as pl
from jax.experimental.pallas import tpu as pltpu

f32 = jnp.float32
bf16 = jnp.bfloat16
SDS = jax.ShapeDtypeStruct
HIGHEST = lax.Precision.HIGHEST

N_DEV = 8
MEM_HEADS = 4
CHUNK = 64
LANES = 128
EPS = 1e-6
NEG_INF = -1e30
ADAM_LR = 0.001
ADAM_B1 = 0.9
ADAM_B2 = 0.999
ADAM_EPS = 1e-08
ADAM_WD = 0.01
ADAM_STEP = 10

ROW_TILE = 256
MM_TILE = 512
FFN_FWD_TILE = 512
FFN_BWD_TILE = 256
ATT_TILE = 512
CUMSUM_TILE = 256
VMEM_LIMIT = 56 * 1024 * 1024

MESH = pl.DeviceIdType.MESH


def _cp(sem=None):
    return pltpu.CompilerParams(dimension_semantics=sem, vmem_limit_bytes=VMEM_LIMIT)


_DIMS = {"nn": (((1,), (0,)), ((), ())), "nt": (((1,), (1,)), ((), ())), "tn": (((0,), (0,)), ((), ()))}


def _dg(a, b, mode):
    return lax.dot_general(a.astype(bf16), b.astype(bf16), _DIMS[mode], preferred_element_type=f32)


@functools.partial(jax.custom_vjp, nondiff_argnums=(2,))
def _mm(a, b, mode):
    return _dg(a, b, mode)


def _mm_fwd(a, b, mode):
    return _dg(a, b, mode), (a, b)


def _mm_bwd(mode, res, ct):
    a, b = res
    if mode == "nn":
        da, db = _dg(ct, b, "nt"), _dg(a, ct, "tn")
    elif mode == "nt":
        da, db = _dg(ct, b, "nn"), _dg(ct, a, "tn")
    else:
        da, db = _dg(b, ct, "nt"), _dg(a, ct, "nn")
    return da.astype(a.dtype), db.astype(b.dtype)


_mm.defvjp(_mm_fwd, _mm_bwd)


def _dgh(a, b, mode="nn"):
    return lax.dot_general(a, b, _DIMS[mode], precision=HIGHEST, preferred_element_type=f32)


@jax.custom_vjp
def _unit_lower_inverse(low):
    c = low.shape[0]
    ri = lax.broadcasted_iota(jnp.int32, (c, c), 0)
    ci = lax.broadcasted_iota(jnp.int32, (c, c), 1)
    x = -low
    r = jnp.where(ri == ci, 1.0, 0.0) + x
    for _ in range(int(math.log2(c)) - 1):
        x = _dgh(x, x)
        r = r + _dgh(r, x)
    return r


def _uli_fwd(low):
    t = _unit_lower_inverse(low)
    return t, t


def _uli_bwd(t, ct):
    return (-_dgh(t, _dgh(ct, t, "nt"), "tn"),)


_unit_lower_inverse.defvjp(_uli_fwd, _uli_bwd)


@functools.partial(jax.custom_vjp, nondiff_argnums=(1,))
def _split_lanes(x, width):
    return tuple(x[:, i * width:(i + 1) * width] for i in range(x.shape[1] // width))


def _split_fwd(x, width):
    return _split_lanes(x, width), None


def _split_bwd(width, _, cts):
    return (jnp.concatenate(list(cts), axis=1),)


_split_lanes.defvjp(_split_fwd, _split_bwd)


def _sigmoid(x):
    return 1.0 / (1.0 + jnp.exp(-x))


def _silu(x):
    return x * _sigmoid(x)


def _softplus(x):
    return jnp.maximum(x, 0.0) + jnp.log1p(jnp.exp(-jnp.abs(x)))


def _rms(x, gain):
    return x * lax.rsqrt(jnp.mean(x * x, axis=-1, keepdims=True) + EPS) * gain


class V:
    def __init__(self, arr, lead=None, cb=0, w=None):
        self.arr, self.lead, self.cb = arr, lead, cb
        self.w = arr.shape[-1] if w is None else w

    @property
    def rows(self):
        return self.arr.shape[-2]

    def spec(self, tile, order=None):
        lead, cb, w = self.lead, self.cb, self.w
        order = order or (lambda i: i)
        if lead is None:
            return pl.BlockSpec((tile, w), lambda i: (order(i), cb))
        return pl.BlockSpec((None, tile, w), lambda i: (lead, order(i), cb))

    def const_spec(self):
        lead, cb, w, r = self.lead, self.cb, self.w, self.rows
        if lead is None:
            return pl.BlockSpec((r, w), lambda i: (0, cb))
        return pl.BlockSpec((None, r, w), lambda i: (lead, 0, cb))


def _v(a):
    return a if isinstance(a, V) else V(a)


def _rowwise(name, fn, rows, consts, outs, tile=None):
    rows = [_v(r) for r in rows]
    consts = [_v(c) for c in consts]
    s = rows[0].rows
    tile = min(tile or ROW_TILE, s)
    nr, nc = len(rows), len(consts)

    def body(*refs):
        vals = [r[...].astype(f32) for r in refs[:nr + nc]]
        res = fn(*vals)
        for o, val in zip(refs[nr + nc:], res):
            o[...] = val.astype(o.dtype)

    return pl.pallas_call(
        body, name=name, grid=(s // tile,),
        in_specs=[r.spec(tile) for r in rows] + [c.const_spec() for c in consts],
        out_specs=[pl.BlockSpec((tile, w), lambda i: (i, 0)) for w, _ in outs],
        out_shape=[SDS((s, w), dt) for w, dt in outs],
        compiler_params=_cp(("parallel",)),
    )(*[r.arr for r in rows], *[c.arr for c in consts])


def _rowwise_vjp(name, fn, rows, consts, cts, d_rows, add=None, tile=None):
    rows = [_v(r) for r in rows]
    consts = [_v(c) for c in consts]
    cts = [[_v(c) for c in group] for group in cts]
    flat_cts = [c for group in cts for c in group]
    s = rows[0].rows
    tile = min(tile or ROW_TILE, s)
    nr, nc, nt = len(rows), len(consts), len(flat_cts)
    want = [k for k, dt in enumerate(d_rows) if dt is not None]
    has_add = add is not None
    add_v = [_v(add)] if has_add else []

    def body(*refs):
        vals = [r[...].astype(f32) for r in refs[:nr + nc]]
        ct_refs = refs[nr + nc:nr + nc + nt]
        pos = nr + nc + nt
        add_ref = refs[pos] if has_add else None
        pos += 1 if has_add else 0
        drow_refs = refs[pos:pos + len(want)]
        dconst_refs = refs[pos + len(want):]
        ctv, at = [], 0
        for group in cts:
            acc = ct_refs[at][...].astype(f32)
            for r in ct_refs[at + 1:at + len(group)]:
                acc = acc + r[...].astype(f32)
            at += len(group)
            ctv.append(acc)
        _, vjp = jax.vjp(fn, *vals)
        grads = vjp(tuple(ctv))
        for o, k in zip(drow_refs, want):
            g = grads[k]
            if has_add and k == want[0]:
                g = g + add_ref[...].astype(f32)
            o[...] = g.astype(o.dtype)

        @pl.when(pl.program_id(0) == 0)
        def _():
            for o in dconst_refs:
                o[...] = jnp.zeros_like(o)

        for o, g in zip(dconst_refs, grads[nr:]):
            o[...] += g

    outs = pl.pallas_call(
        body, name=name, grid=(s // tile,),
        in_specs=[r.spec(tile) for r in rows] + [c.const_spec() for c in consts]
        + [c.spec(tile) for c in flat_cts] + [a.spec(tile) for a in add_v],
        out_specs=[pl.BlockSpec((tile, rows[k].w), lambda i: (i, 0)) for k in want]
        + [pl.BlockSpec((c.rows, c.w), lambda i: (0, 0)) for c in consts],
        out_shape=[SDS((s, rows[k].w), d_rows[k]) for k in want] + [SDS((c.rows, c.w), f32) for c in consts],
        compiler_params=_cp(("arbitrary",)),
    )(*[r.arr for r in rows], *[c.arr for c in consts], *[c.arr for c in flat_cts], *[a.arr for a in add_v])
    return list(outs[:len(want)]), list(outs[len(want):])


def _pick(n, cap):
    best = None
    for t in range(LANES, min(n, cap) + 1, LANES):
        if n % t == 0:
            best = t
    return best or n


def _matmul(name, a, b, mode, out_dtype, add=None, tn_cap=1280):
    has_add = add is not None
    if mode in ("nn", "nt"):
        m, k = a.shape
        n = b.shape[1] if mode == "nn" else b.shape[0]
        tm = min(MM_TILE, m)
        tn = _pick(n, tn_cap) if k * n * 2 > (8 << 20) else n

        def body(*refs):
            a_ref, b_ref = refs[0], refs[1]
            o_ref = refs[-1]
            acc = _dg(a_ref[...], b_ref[...], mode)
            if has_add:
                acc = acc + refs[2][...].astype(f32)
            o_ref[...] = acc.astype(o_ref.dtype)

        b_spec = pl.BlockSpec((k, tn), lambda i, j: (0, j)) if mode == "nn" else pl.BlockSpec((tn, k), lambda i, j: (j, 0))
        in_specs = [pl.BlockSpec((tm, k), lambda i, j: (i, 0)), b_spec]
        args = [a, b]
        if has_add:
            in_specs.append(pl.BlockSpec((tm, tn), lambda i, j: (i, j)))
            args.append(add)
        return pl.pallas_call(
            body, name=name, grid=(m // tm, n // tn), in_specs=in_specs,
            out_specs=pl.BlockSpec((tm, tn), lambda i, j: (i, j)),
            out_shape=SDS((m, n), out_dtype), compiler_params=_cp(("parallel", "parallel")),
        )(*args)
    kk, m = a.shape
    n = b.shape[1]
    tk = min(MM_TILE, kk)
    tn = _pick(n, tn_cap) if m * n * 4 > (6 << 20) else n

    def body_tn(a_ref, b_ref, o_ref):
        @pl.when(pl.program_id(1) == 0)
        def _():
            o_ref[...] = jnp.zeros_like(o_ref)

        o_ref[...] += _dg(a_ref[...], b_ref[...], "tn")

    return pl.pallas_call(
        body_tn, name=name, grid=(n // tn, kk // tk),
        in_specs=[pl.BlockSpec((tk, m), lambda j, k: (k, 0)), pl.BlockSpec((tk, tn), lambda j, k: (k, j))],
        out_specs=pl.BlockSpec((m, tn), lambda j, k: (0, j)),
        out_shape=SDS((m, n), f32), compiler_params=_cp(("parallel", "arbitrary")),
    )(a, b)


def _ffn_fwd(name, h, gain, wgu, wd, layer):
    s, d = h.shape
    hs = wgu.shape[3]
    nh = wgu.shape[0] // 2
    tm = min(FFN_FWD_TILE, s)

    def body(h_ref, g_ref, wg_ref, wu_ref, wd_ref, o_ref, n_scr, acc_scr):
        t = pl.program_id(1)

        @pl.when(t == 0)
        def _():
            n_scr[...] = _rms(h_ref[...], g_ref[...]).astype(bf16)
            acc_scr[...] = jnp.zeros_like(acc_scr)

        n = n_scr[...]
        gate = _dg(n, wg_ref[...], "nn")
        up = _dg(n, wu_ref[...], "nn")
        acc_scr[...] += _dg(_silu(gate) * up, wd_ref[...], "nn")

        @pl.when(t == nh - 1)
        def _():
            o_ref[...] = h_ref[...] + 0.5 * acc_scr[...]

    return pl.pallas_call(
        body, name=name, grid=(s // tm, nh),
        in_specs=[
            pl.BlockSpec((tm, d), lambda i, t: (i, 0)),
            pl.BlockSpec((1, d), lambda i, t: (0, 0)),
            pl.BlockSpec((None, None, d, hs), lambda i, t: (t, layer, 0, 0)),
            pl.BlockSpec((None, None, d, hs), lambda i, t: (t + nh, layer, 0, 0)),
            pl.BlockSpec((None, hs, d), lambda i, t: (layer, t, 0)),
        ],
        out_specs=pl.BlockSpec((tm, d), lambda i, t: (i, 0)),
        out_shape=SDS((s, d), f32),
        scratch_shapes=[pltpu.VMEM((tm, d), bf16), pltpu.VMEM((tm, d), f32)],
        compiler_params=_cp(("parallel", "arbitrary")),
    )(h, gain, wgu, wgu, wd)


def _ffn_bwd(name, h, dout, gain, wgu, wd, layer):
    s, d = h.shape
    hs = wgu.shape[3]
    nh = wgu.shape[0] // 2
    tm = min(FFN_BWD_TILE, s)

    def body(h_ref, do_ref, g_ref, wg_ref, wu_ref, wd_ref, dn_ref, dwg_ref, dwu_ref, dwd_ref):
        @pl.when(pl.program_id(1) == 0)
        def _():
            dwg_ref[...] = jnp.zeros_like(dwg_ref)
            dwu_ref[...] = jnp.zeros_like(dwu_ref)
            dwd_ref[...] = jnp.zeros_like(dwd_ref)

        n = _rms(h_ref[...], g_ref[...]).astype(bf16)
        wg, wu, wdn = wg_ref[...], wu_ref[...], wd_ref[...]
        gate = _dg(n, wg, "nn")
        up = _dg(n, wu, "nn")
        sg = _sigmoid(gate)
        act = gate * sg
        dy = (0.5 * do_ref[...]).astype(bf16)
        da = _dg(dy, wdn, "nt")
        dup = (da * act).astype(bf16)
        dgate = (da * up * (sg * (1.0 + gate * (1.0 - sg)))).astype(bf16)
        dwd_ref[...] += _dg(act * up, dy, "tn")
        dwg_ref[...] += _dg(n, dgate, "tn")
        dwu_ref[...] += _dg(n, dup, "tn")
        dn_ref[...] = _dg(dgate, wg, "nt") + _dg(dup, wu, "nt")

    return pl.pallas_call(
        body, name=name, grid=(nh, s // tm),
        in_specs=[
            pl.BlockSpec((tm, d), lambda t, i: (i, 0)),
            pl.BlockSpec((tm, d), lambda t, i: (i, 0)),
            pl.BlockSpec((1, d), lambda t, i: (0, 0)),
            pl.BlockSpec((None, None, d, hs), lambda t, i: (t, layer, 0, 0)),
            pl.BlockSpec((None, None, d, hs), lambda t, i: (t + nh, layer, 0, 0)),
            pl.BlockSpec((None, hs, d), lambda t, i: (layer, t, 0)),
        ],
        out_specs=[
            pl.BlockSpec((None, tm, d), lambda t, i: (t, i, 0)),
            pl.BlockSpec((None, d, hs), lambda t, i: (t, 0, 0)),
            pl.BlockSpec((None, d, hs), lambda t, i: (t, 0, 0)),
            pl.BlockSpec((hs, d), lambda t, i: (t, 0)),
        ],
        out_shape=[SDS((nh, s, d), f32), SDS((nh, d, hs), f32), SDS((nh, d, hs), f32), SDS((nh * hs, d), f32)],
        compiler_params=_cp(("parallel", "arbitrary")),
    )(h, dout, gain, wgu, wgu, wd)


def _conv_fwd(name, x, w):
    x = _v(x)
    s, c = x.rows, x.w
    cw = w.shape[0]
    tile = min(ROW_TILE, s)
    cb = x.cb

    def body(x_ref, halo_ref, w_ref, o_ref, buf):
        first = pl.program_id(0) == 0
        buf[0:8, :] = jnp.where(first, 0.0, halo_ref[...])
        buf[8:8 + tile, :] = x_ref[...]
        acc = w_ref[0:1, :] * buf[pl.ds(8 - cw + 1, tile), :]
        for j in range(1, cw):
            acc = acc + w_ref[j:j + 1, :] * buf[pl.ds(8 - cw + 1 + j, tile), :]
        o_ref[...] = acc

    return pl.pallas_call(
        body, name=name, grid=(s // tile,),
        in_specs=[
            pl.BlockSpec((tile, c), lambda i: (i, cb)),
            pl.BlockSpec((8, c), lambda i: (jnp.maximum(i * (tile // 8) - 1, 0), cb)),
            pl.BlockSpec((cw, c), lambda i: (0, 0)),
        ],
        out_specs=pl.BlockSpec((tile, c), lambda i: (i, 0)),
        out_shape=SDS((s, c), f32),
        scratch_shapes=[pltpu.VMEM((tile + 8, c), f32)],
        compiler_params=_cp(("parallel",)),
    )(x.arr, x.arr, w)


def _conv_bwd(name, x, dy, w):
    x = _v(x)
    s, c = x.rows, x.w
    cw = w.shape[0]
    tile = min(ROW_TILE, s)
    n_tiles = s // tile
    cb = x.cb

    def body(x_ref, xh_ref, dy_ref, dyh_ref, w_ref, dx_ref, dw_ref, xbuf, dbuf):
        i = pl.program_id(0)
        xbuf[0:8, :] = jnp.where(i == 0, 0.0, xh_ref[...])
        xbuf[8:8 + tile, :] = x_ref[...]
        dyv = dy_ref[...]
        dbuf[0:tile, :] = dyv
        dbuf[tile:tile + 8, :] = jnp.where(i == n_tiles - 1, 0.0, dyh_ref[...])

        @pl.when(i == 0)
        def _():
            dw_ref[...] = jnp.zeros_like(dw_ref)

        acc = w_ref[0:1, :] * dbuf[pl.ds(cw - 1, tile), :]
        for j in range(1, cw):
            acc = acc + w_ref[j:j + 1, :] * dbuf[pl.ds(cw - 1 - j, tile), :]
        dx_ref[...] = acc.astype(dx_ref.dtype)
        for j in range(cw):
            dw_ref[j:j + 1, :] += jnp.sum(xbuf[pl.ds(8 - cw + 1 + j, tile), :] * dyv, axis=0, keepdims=True)

    return pl.pallas_call(
        body, name=name, grid=(n_tiles,),
        in_specs=[
            pl.BlockSpec((tile, c), lambda i: (i, cb)),
            pl.BlockSpec((8, c), lambda i: (jnp.maximum(i * (tile // 8) - 1, 0), cb)),
            pl.BlockSpec((tile, c), lambda i: (i, 0)),
            pl.BlockSpec((8, c), lambda i: (jnp.minimum((i + 1) * (tile // 8), s // 8 - 1), 0)),
            pl.BlockSpec((cw, c), lambda i: (0, 0)),
        ],
        out_specs=[pl.BlockSpec((tile, c), lambda i: (i, 0)), pl.BlockSpec((cw, c), lambda i: (0, 0))],
        out_shape=[SDS((s, c), bf16), SDS((cw, c), f32)],
        scratch_shapes=[pltpu.VMEM((tile + 8, c), f32), pltpu.VMEM((tile + 8, c), f32)],
        compiler_params=_cp(("arbitrary",)),
    )(x.arr, x.arr, dy, dy, w)


def _gdn_pre(n_heads, head_dim, yc, ab, a_log, dt_bias):
    gw = n_heads * head_dim
    act = _silu(yc)
    parts = _split_lanes(act, head_dim)
    qs = [p * lax.rsqrt(jnp.sum(p * p, axis=-1, keepdims=True) + EPS) * (head_dim ** -0.5) for p in parts[:n_heads]]
    ks = [p * lax.rsqrt(jnp.sum(p * p, axis=-1, keepdims=True) + EPS) for p in parts[n_heads:2 * n_heads]]
    lane = lax.broadcasted_iota(jnp.int32, ab.shape, 1)
    g = -jnp.exp(a_log) * _softplus(ab + dt_bias)
    gb = jnp.where(lane < n_heads, g, jnp.where(lane < 2 * n_heads, _sigmoid(ab), 0.0))
    del gw
    return (jnp.concatenate(qs, axis=1), jnp.concatenate(ks, axis=1),
            jnp.concatenate(list(parts[2 * n_heads:]), axis=1), gb)


def _gdn_post(n_heads, head_dim, o, z, out_norm):
    parts = _split_lanes(o, head_dim)
    normed = jnp.concatenate([_rms(p, out_norm) for p in parts], axis=1)
    return (normed * _silu(z),)


def _gdn_chunk(n_heads, head_dim, q, k, v, gb, *states):
    c = q.shape[0]
    ri = lax.broadcasted_iota(jnp.int32, (c, c), 0)
    ci = lax.broadcasted_iota(jnp.int32, (c, c), 1)
    incl, strict, diag = ri >= ci, ri > ci, ri == ci
    lane = lax.broadcasted_iota(jnp.int32, gb.shape, 1)
    qs, ks, vs = _split_lanes(q, head_dim), _split_lanes(k, head_dim), _split_lanes(v, head_dim)
    outs, new_states = [], []
    for h in range(n_heads):
        qh, kh, vh, st = qs[h], ks[h], vs[h], states[h]
        g = jnp.sum(jnp.where(lane == h, gb, 0.0), axis=1, keepdims=True)
        beta = jnp.sum(jnp.where(lane == n_heads + h, gb, 0.0), axis=1, keepdims=True)
        g_row = jnp.sum(jnp.where(diag, g, 0.0), axis=0, keepdims=True)
        cg_col = jnp.sum(jnp.where(incl, g_row, 0.0), axis=1, keepdims=True)
        cg_row = jnp.sum(jnp.where(ri <= ci, g, 0.0), axis=0, keepdims=True)
        g_last = jnp.sum(g, axis=0, keepdims=True)
        decay = jnp.where(incl, jnp.exp(jnp.where(incl, cg_col - cg_row, 0.0)), 0.0)
        kb = kh * beta
        lower = jnp.where(strict, _mm(kb, kh, "nt") * decay, 0.0)
        off_diag = _unit_lower_inverse(lower) - jnp.where(diag, 1.0, 0.0)
        vb, kbg = vh * beta, kb * jnp.exp(cg_col)
        u = vb + _mm(off_diag, vb, "nn")
        w = kbg + _mm(off_diag, kbg, "nn")
        qk = jnp.where(incl, _mm(qh, kh, "nt") * decay, 0.0)
        v_new = u - _mm(w, st, "nn")
        outs.append(_mm(qh * jnp.exp(cg_col), st, "nn") + _mm(qk, v_new, "nn"))
        new_states.append(st * jnp.exp(g_last) + _mm(kh * jnp.exp(g_last - cg_col), v_new, "tn"))
    return (jnp.concatenate(outs, axis=1), *new_states)


def _gdn_chunk_fwd(name, q, k, v, gb, n_heads, head_dim):
    s, gw = q.shape
    n = s // CHUNK
    fn = functools.partial(_gdn_chunk, n_heads, head_dim)

    def body(q_ref, k_ref, v_ref, gb_ref, o_ref, st_ref, st_scr):
        @pl.when(pl.program_id(0) == 0)
        def _():
            st_scr[...] = jnp.zeros_like(st_scr)

        st_ref[...] = st_scr[...]
        res = fn(q_ref[...], k_ref[...], v_ref[...], gb_ref[...], *[st_scr[h] for h in range(n_heads)])
        o_ref[...] = res[0]
        for h in range(n_heads):
            st_scr[h] = res[1 + h]

    row = lambda w: pl.BlockSpec((CHUNK, w), lambda i: (i, 0))
    return pl.pallas_call(
        body, name=name, grid=(n,),
        in_specs=[row(gw), row(gw), row(gw), row(LANES)],
        out_specs=[row(gw), pl.BlockSpec((None, n_heads, head_dim, head_dim), lambda i: (i, 0, 0, 0))],
        out_shape=[SDS((s, gw), f32), SDS((n, n_heads, head_dim, head_dim), f32)],
        scratch_shapes=[pltpu.VMEM((n_heads, head_dim, head_dim), f32)],
        compiler_params=_cp(("arbitrary",)),
    )(q, k, v, gb)


def _gdn_chunk_bwd(name, q, k, v, gb, states, d_out, n_heads, head_dim):
    s, gw = q.shape
    n = s // CHUNK
    fn = functools.partial(_gdn_chunk, n_heads, head_dim)

    def body(q_ref, k_ref, v_ref, gb_ref, st_ref, do_ref, dq_ref, dk_ref, dv_ref, dgb_ref, dst_scr):
        @pl.when(pl.program_id(0) == 0)
        def _():
            dst_scr[...] = jnp.zeros_like(dst_scr)

        _, vjp = jax.vjp(fn, q_ref[...], k_ref[...], v_ref[...], gb_ref[...], *[st_ref[h] for h in range(n_heads)])
        grads = vjp((do_ref[...].astype(f32), *[dst_scr[h] for h in range(n_heads)]))
        dq_ref[...] = grads[0]
        dk_ref[...] = grads[1]
        dv_ref[...] = grads[2]
        dgb_ref[...] = grads[3]
        for h in range(n_heads):
            dst_scr[h] = grads[4 + h]

    row = lambda w: pl.BlockSpec((CHUNK, w), lambda i: (n - 1 - i, 0))
    return pl.pallas_call(
        body, name=name, grid=(n,),
        in_specs=[row(gw), row(gw), row(gw), row(LANES),
                  pl.BlockSpec((None, n_heads, head_dim, head_dim), lambda i: (n - 1 - i, 0, 0, 0)), row(gw)],
        out_specs=[row(gw), row(gw), row(gw), row(LANES)],
        out_shape=[SDS((s, gw), f32), SDS((s, gw), f32), SDS((s, gw), f32), SDS((s, LANES), f32)],
        scratch_shapes=[pltpu.VMEM((n_heads, head_dim, head_dim), f32)],
        compiler_params=_cp(("arbitrary",)),
    )(q, k, v, gb, states, d_out)


def _mem_attn(qm, km, vm):
    width = qm.shape[1]
    hd = width // MEM_HEADS
    lane = lax.broadcasted_iota(jnp.int32, (1, width), 1)
    out = jnp.zeros_like(qm)
    for h in range(MEM_HEADS):
        msk = jnp.where((lane >= h * hd) & (lane < (h + 1) * hd), 1.0, 0.0)
        logits = _mm(qm * msk, km, "nt") * (hd ** -0.5)
        p = jnp.exp(logits - jnp.max(logits, axis=-1, keepdims=True))
        p = p / jnp.sum(p, axis=-1, keepdims=True)
        out = out + _mm(p, vm, "nn") * msk
    return (out,)


def _kv_post(n_heads, pk, pv, pf, b_f):
    lane = lax.broadcasted_iota(jnp.int32, pf.shape, 1)
    log_f = jnp.where(lane < n_heads, -_softplus(-(pf + b_f)), 0.0)
    return pk, pv, log_f


def _cumsum(name, xs, reverse):
    s, w = xs[0].shape
    tile = min(CUMSUM_TILE, s)
    n = s // tile

    def body(*refs):
        x_refs, o_ref, carry = refs[:-2], refs[-2], refs[-1]

        @pl.when(pl.program_id(0) == 0)
        def _():
            carry[...] = jnp.zeros_like(carry)

        xv = x_refs[0][...]
        for r in x_refs[1:]:
            xv = xv + r[...]
        ri = lax.broadcasted_iota(jnp.int32, (tile, tile), 0)
        ci = lax.broadcasted_iota(jnp.int32, (tile, tile), 1)
        tri = jnp.where((ri <= ci) if reverse else (ri >= ci), 1.0, 0.0).astype(bf16)
        x1 = xv.astype(bf16)
        r1 = xv - x1.astype(f32)
        x2 = r1.astype(bf16)
        x3 = (r1 - x2.astype(f32)).astype(bf16)
        acc = carry[...] + _dg(tri, x1, "nn") + _dg(tri, x2, "nn") + _dg(tri, x3, "nn")
        o_ref[...] = acc
        carry[...] += jnp.sum(xv, axis=0, keepdims=True)

    order = (lambda i: (n - 1 - i, 0)) if reverse else (lambda i: (i, 0))
    return pl.pallas_call(
        body, name=name, grid=(n,),
        in_specs=[pl.BlockSpec((tile, w), order)] * len(xs), out_specs=pl.BlockSpec((tile, w), order),
        out_shape=SDS((s, w), f32), scratch_shapes=[pltpu.VMEM((1, w), f32)],
        compiler_params=_cp(("arbitrary",)),
    )(*xs)


def _fox_scores(q, k, ccol, crow, i, j, tq, tk, scale):
    s = _dg(q, k, "nt") * scale + ccol - crow
    rpos = i * tq + lax.broadcasted_iota(jnp.int32, (tq, tk), 0)
    cpos = j * tk + lax.broadcasted_iota(jnp.int32, (tq, tk), 1)
    return s, cpos <= rpos


def _fox_fwd(name, q, k, v, ccol, crow, n_heads, head_dim):
    s = k.shape[0]
    tq = tk = min(ATT_TILE, s)
    nq, nk = s // tq, s // tk
    scale = head_dim ** -0.5

    def body(q_ref, k_ref, v_ref, cc_ref, cr_ref, o_ref, lse_ref, m_scr, l_scr, acc_scr):
        i, j = pl.program_id(1), pl.program_id(2)

        @pl.when(j == 0)
        def _():
            m_scr[...] = jnp.full_like(m_scr, NEG_INF)
            l_scr[...] = jnp.zeros_like(l_scr)
            acc_scr[...] = jnp.zeros_like(acc_scr)

        @pl.when(j <= i)
        def _():
            sc, ok = _fox_scores(q_ref[...], k_ref[...], cc_ref[...], cr_ref[...], i, j, tq, tk, scale)
            sc = jnp.where(ok, sc, NEG_INF)
            m_new = jnp.maximum(m_scr[...], jnp.max(sc, axis=-1, keepdims=True))
            p = jnp.where(ok, jnp.exp(sc - m_new), 0.0)
            alpha = jnp.exp(m_scr[...] - m_new)
            l_scr[...] = alpha * l_scr[...] + jnp.sum(p, axis=-1, keepdims=True)
            acc_scr[...] = alpha * acc_scr[...] + _dg(p, v_ref[...], "nn")
            m_scr[...] = m_new

        @pl.when(j == nk - 1)
        def _():
            o_ref[...] = (acc_scr[...] / l_scr[...]).astype(o_ref.dtype)
            lse_ref[...] = m_scr[...] + jnp.log(l_scr[...])

    return pl.pallas_call(
        body, name=name, grid=(n_heads, nq, nk),
        in_specs=[
            pl.BlockSpec((tq, head_dim), lambda h, i, j: (i, h)),
            pl.BlockSpec((tk, head_dim), lambda h, i, j: (jnp.minimum(j, i), h)),
            pl.BlockSpec((tk, head_dim), lambda h, i, j: (jnp.minimum(j, i), h)),
            pl.BlockSpec((None, tq, 1), lambda h, i, j: (h, i, 0)),
            pl.BlockSpec((None, 1, tk), lambda h, i, j: (h, 0, jnp.minimum(j, i))),
        ],
        out_specs=[pl.BlockSpec((tq, head_dim), lambda h, i, j: (i, h)),
                   pl.BlockSpec((None, tq, 1), lambda h, i, j: (h, i, 0))],
        out_shape=[SDS((s, n_heads * head_dim), bf16), SDS((n_heads, s, 1), f32)],
        scratch_shapes=[pltpu.VMEM((tq, 1), f32), pltpu.VMEM((tq, 1), f32), pltpu.VMEM((tq, head_dim), f32)],
        compiler_params=_cp(("parallel", "parallel", "arbitrary")),
    )(q, k, v, ccol, crow)


def _fox_probs(q_ref, k_ref, v_ref, cc_ref, cr_ref, lse_ref, do, i, j, tq, tk, scale):
    sc, ok = _fox_scores(q_ref[...], k_ref[...], cc_ref[...], cr_ref[...], i, j, tq, tk, scale)
    p = jnp.where(ok, jnp.exp(jnp.where(ok, sc - lse_ref[...], 0.0)), 0.0)
    return p, _dg(do, v_ref[...], "nt")


def _fox_bwd_dq(name, q, k, v, ccol, crow, lse, do, n_heads, head_dim):
    s = k.shape[0]
    tq = tk = min(ATT_TILE, s)
    nq, nk = s // tq, s // tk
    scale = head_dim ** -0.5

    def body(q_ref, k_ref, v_ref, cc_ref, cr_ref, lse_ref, do_ref, dq_ref, delta_ref, dcc_ref, acc_scr):
        i, sweep, j = pl.program_id(1), pl.program_id(2), pl.program_id(3)

        @pl.when((sweep == 0) & (j == 0))
        def _():
            delta_ref[...] = jnp.zeros_like(delta_ref)
            dcc_ref[...] = jnp.zeros_like(dcc_ref)
            acc_scr[...] = jnp.zeros_like(acc_scr)

        @pl.when((sweep == 0) & (j <= i))
        def _():
            p, dp = _fox_probs(q_ref, k_ref, v_ref, cc_ref, cr_ref, lse_ref, do_ref[...], i, j, tq, tk, scale)
            delta_ref[...] += jnp.sum(p * dp, axis=-1, keepdims=True)

        @pl.when((sweep == 1) & (j <= i))
        def _():
            p, dp = _fox_probs(q_ref, k_ref, v_ref, cc_ref, cr_ref, lse_ref, do_ref[...], i, j, tq, tk, scale)
            ds = p * (dp - delta_ref[...])
            dcc_ref[...] += jnp.sum(ds, axis=-1, keepdims=True)
            acc_scr[...] += _dg(ds, k_ref[...], "nn")

        @pl.when((sweep == 1) & (j == nk - 1))
        def _():
            dq_ref[...] = (acc_scr[...] * scale).astype(dq_ref.dtype)

    qspec = pl.BlockSpec((tq, head_dim), lambda h, i, w, j: (i, h))
    kspec = pl.BlockSpec((tk, head_dim), lambda h, i, w, j: (jnp.minimum(j, i), h))
    cspec = pl.BlockSpec((None, tq, 1), lambda h, i, w, j: (h, i, 0))
    return pl.pallas_call(
        body, name=name, grid=(n_heads, nq, 2, nk),
        in_specs=[qspec, kspec, kspec, cspec,
                  pl.BlockSpec((None, 1, tk), lambda h, i, w, j: (h, 0, jnp.minimum(j, i))), cspec, qspec],
        out_specs=[qspec, cspec, cspec],
        out_shape=[SDS((s, n_heads * head_dim), bf16), SDS((n_heads, s, 1), f32), SDS((n_heads, s, 1), f32)],
        scratch_shapes=[pltpu.VMEM((tq, head_dim), f32)],
        compiler_params=_cp(("parallel", "parallel", "arbitrary", "arbitrary")),
    )(q, k, v, ccol, crow, lse, do)


def _fox_bwd_dkv(name, q, k, v, ccol, crow, lse, delta, do, n_heads, head_dim):
    s = k.shape[0]
    tq = tk = min(ATT_TILE, s)
    nq, nk = s // tq, s // tk
    scale = head_dim ** -0.5

    def body(q_ref, k_ref, v_ref, cc_ref, cr_ref, lse_ref, delta_ref, do_ref, dk_ref, dv_ref, dc_ref):
        j, i = pl.program_id(1), pl.program_id(2)

        @pl.when(i == 0)
        def _():
            dk_ref[...] = jnp.zeros_like(dk_ref)
            dv_ref[...] = jnp.zeros_like(dv_ref)
            dc_ref[...] = jnp.zeros_like(dc_ref)

        @pl.when(i >= j)
        def _():
            dov = do_ref[...]
            p, dp = _fox_probs(q_ref, k_ref, v_ref, cc_ref, cr_ref, lse_ref, dov, i, j, tq, tk, scale)
            ds = p * (dp - delta_ref[...])
            dv_ref[...] += _dg(p, dov, "tn")
            dk_ref[...] += _dg(ds, q_ref[...], "tn") * scale
            dc_ref[...] -= jnp.sum(ds, axis=0, keepdims=True)

    qspec = pl.BlockSpec((tq, head_dim), lambda h, j, i: (jnp.maximum(i, j), h))
    kspec = pl.BlockSpec((tk, head_dim), lambda h, j, i: (j, h))
    cspec = pl.BlockSpec((None, tq, 1), lambda h, j, i: (h, jnp.maximum(i, j), 0))
    rspec = pl.BlockSpec((None, 1, tk), lambda h, j, i: (h, 0, j))
    return pl.pallas_call(
        body, name=name, grid=(n_heads, nk, nq),
        in_specs=[qspec, kspec, kspec, cspec, rspec, cspec, cspec, qspec],
        out_specs=[kspec, kspec, rspec],
        out_shape=[SDS((s, n_heads * head_dim), f32), SDS((s, n_heads * head_dim), f32), SDS((n_heads, 1, s), f32)],
        compiler_params=_cp(("parallel", "parallel", "arbitrary")),
    )(q, k, v, ccol, crow, lse, delta, do)


def _final_loss(name, h, target, gain):
    s, d = h.shape
    tile = min(ROW_TILE, s)

    def body(h_ref, t_ref, g_ref, loss_ref, dh_ref, dg_ref):
        @pl.when(pl.program_id(0) == 0)
        def _():
            loss_ref[...] = jnp.zeros_like(loss_ref)
            dg_ref[...] = jnp.zeros_like(dg_ref)

        x, g = h_ref[...], g_ref[...]
        rstd = lax.rsqrt(jnp.mean(x * x, axis=-1, keepdims=True) + EPS)
        xhat = x * rstd
        err = xhat * g - t_ref[...]
        row = jnp.sum(err * err, axis=-1, keepdims=True) * (0.5 / d)
        loss_ref[...] += jnp.sum(row, axis=0, keepdims=True)
        dy = err * (1.0 / d)
        dg_ref[...] += jnp.sum(dy * xhat, axis=0, keepdims=True)
        dxhat = dy * g
        dh_ref[...] = rstd * (dxhat - xhat * jnp.mean(dxhat * xhat, axis=-1, keepdims=True))

    return pl.pallas_call(
        body, name=name, grid=(s // tile,),
        in_specs=[pl.BlockSpec((tile, d), lambda i: (i, 0)), pl.BlockSpec((tile, d), lambda i: (i, 0)),
                  pl.BlockSpec((1, d), lambda i: (0, 0))],
        out_specs=[pl.BlockSpec((1, LANES), lambda i: (0, 0)), pl.BlockSpec((tile, d), lambda i: (i, 0)),
                   pl.BlockSpec((1, d), lambda i: (0, 0))],
        out_shape=[SDS((1, LANES), f32), SDS((s, d), f32), SDS((1, d), f32)],
        compiler_params=_cp(("arbitrary",)),
    )(h, target, gain)


def _adamw(name, parts, w, m, v):
    r, c = w.shape
    rb = r
    for cand in (256, 128, 64, 32, 16):
        if r % cand == 0 and cand * c * 4 <= (2 << 20):
            rb = cand
            break
    n_parts = parts.shape[0]

    def body(p_ref, w_ref, m_ref, v_ref, g_out, d_out, m_out, v_out):
        g = p_ref[0].astype(f32)
        for k in range(1, n_parts):
            g = g + p_ref[k].astype(f32)
        m_new = ADAM_B1 * m_ref[...] + (1.0 - ADAM_B1) * g
        v_new = ADAM_B2 * v_ref[...] + (1.0 - ADAM_B2) * (g * g)
        m_hat = m_new / (1.0 - ADAM_B1 ** ADAM_STEP)
        v_hat = v_new / (1.0 - ADAM_B2 ** ADAM_STEP)
        g_out[...] = g
        d_out[...] = -ADAM_LR * (m_hat / (jnp.sqrt(v_hat) + ADAM_EPS) + ADAM_WD * w_ref[...])
        m_out[...] = m_new
        v_out[...] = v_new

    blk = pl.BlockSpec((rb, c), lambda i: (i, 0))
    return pl.pallas_call(
        body, name=name, grid=(r // rb,),
        in_specs=[pl.BlockSpec((n_parts, rb, c), lambda i: (0, i, 0)), blk, blk, blk],
        out_specs=[blk, blk, blk, blk],
        out_shape=[SDS((r, c), f32)] * 4,
        compiler_params=_cp(("parallel",)),
    )(parts, w, m, v)


def _position():
    x, y, c = lax.axis_index("x"), lax.axis_index("y"), lax.axis_index("c")
    return x, y, c


def _all_gather(name, shards):
    n = len(shards)

    def body(*refs):
        ins, outs = refs[:n], refs[n:2 * n]
        send_sems, recv_sems, local_sems = refs[2 * n:]
        x, y, c = _position()
        me, sibling = (x, y, c), (x, y, 1 - c)
        chips = [(1 - x, y), (x, 1 - y), (1 - x, 1 - y)]

        def slot(a, block):
            px, py, pc = block
            return outs[a].at[4 * px + 2 * py + pc]

        def copy(a, k, block, to, src=None):
            return pltpu.make_async_remote_copy(
                src_ref=slot(a, block) if src is None else src, dst_ref=slot(a, block),
                send_sem=send_sems.at[a, k], recv_sem=recv_sems.at[a, k], device_id=to, device_id_type=MESH)

        local = [pltpu.make_async_copy(ins[a], slot(a, me), local_sems.at[a]) for a in range(n)]
        for cp in local:
            cp.start()
        started = []
        for a in range(n):
            first = [copy(a, 0, me, sibling, src=ins[a])]
            first += [copy(a, 1 + j, me, (*chip, c), src=ins[a]) for j, chip in enumerate(chips)]
            for cp in first:
                cp.start()
            started += first
        for a in range(n):
            for j, chip in enumerate(chips):
                copy(a, 1 + j, (*chip, c), me).wait_recv()
                passed = copy(a, 4 + j, (*chip, c), sibling)
                passed.start()
                started.append(passed)
        for a in range(n):
            copy(a, 0, sibling, me).wait_recv()
            for j, chip in enumerate(chips):
                copy(a, 4 + j, (*chip, 1 - c), me).wait_recv()
        for cp in started:
            cp.wait_send()
        for cp in local:
            cp.wait()

    any_spec = pl.BlockSpec(memory_space=pl.ANY)
    outs = pl.pallas_call(
        body, name=name,
        in_specs=[any_spec] * n, out_specs=[any_spec] * n,
        out_shape=[SDS((N_DEV, *a.shape), a.dtype) for a in shards],
        scratch_shapes=[pltpu.SemaphoreType.DMA((n, 7)), pltpu.SemaphoreType.DMA((n, 7)), pltpu.SemaphoreType.DMA((n,))],
    )(*shards)
    return list(outs)


def _exchange(name, stacks):
    n = len(stacks)

    def body(*refs):
        ins, outs = refs[:n], refs[n:2 * n]
        send_sems, recv_sems, local_sems = refs[2 * n:]
        x, y, c = _position()
        me = 4 * x + 2 * y + c
        local = [pltpu.make_async_copy(ins[a].at[me], outs[a].at[me], local_sems.at[a]) for a in range(n)]
        for cp in local:
            cp.start()
        sends, recvs = [], []
        for a in range(n):
            for k in range(1, N_DEV):
                fx, fy, fc = (k >> 2) & 1, (k >> 1) & 1, k & 1
                px = (1 - x) if fx else x
                py = (1 - y) if fy else y
                pc = (1 - c) if fc else c
                peer = 4 * px + 2 * py + pc
                send = pltpu.make_async_remote_copy(
                    src_ref=ins[a].at[peer], dst_ref=outs[a].at[me],
                    send_sem=send_sems.at[a, k - 1], recv_sem=recv_sems.at[a, k - 1],
                    device_id=(px, py, pc), device_id_type=MESH)
                send.start()
                sends.append(send)
                recvs.append(pltpu.make_async_remote_copy(
                    src_ref=ins[a].at[peer], dst_ref=outs[a].at[peer],
                    send_sem=send_sems.at[a, k - 1], recv_sem=recv_sems.at[a, k - 1],
                    device_id=(px, py, pc), device_id_type=MESH))
        for cp in recvs:
            cp.wait_recv()
        for cp in sends:
            cp.wait_send()
        for cp in local:
            cp.wait()

    any_spec = pl.BlockSpec(memory_space=pl.ANY)
    outs = pl.pallas_call(
        body, name=name,
        in_specs=[any_spec] * n, out_specs=[any_spec] * n,
        out_shape=[SDS(a.shape, a.dtype) for a in stacks],
        scratch_shapes=[pltpu.SemaphoreType.DMA((n, 7)), pltpu.SemaphoreType.DMA((n, 7)), pltpu.SemaphoreType.DMA((n,))],
    )(*stacks)
    return list(outs)


def _rows_from_shards(g):
    n, l, r, c = g.shape
    return g.transpose(1, 0, 2, 3).reshape(l, n * r, c)


def _rows_to_shards(w):
    l, rows, c = w.shape
    return w.reshape(l, N_DEV, rows // N_DEV, c).transpose(1, 0, 2, 3)


def _pad_lanes(a, width):
    return jnp.pad(a, [(0, 0)] * (a.ndim - 1) + [(0, width - a.shape[-1])])


def _row(vec, width=None):
    vec = vec.reshape(1, -1)
    return vec if width is None else _pad_lanes(vec, width)


class _SmallPack:
    def __init__(self, shapes):
        self.shapes, self.offsets, at = shapes, {}, 0
        for name, shape in shapes.items():
            last = shape[-1]
            lead = int(math.prod(shape[:-1]))
            rows = lead * (last // LANES) if last >= LANES else lead
            self.offsets[name] = (at, rows)
            at += rows
        self.rows = -(-at // 8) * 8

    def pack(self, values):
        pieces = []
        for name, shape in self.shapes.items():
            val = values[name].astype(f32)
            if shape[-1] >= LANES:
                pieces.append(val.reshape(-1, LANES))
            else:
                pieces.append(_pad_lanes(val.reshape(-1, shape[-1]), LANES))
        used = sum(p.shape[0] for p in pieces)
        if used < self.rows:
            pieces.append(jnp.zeros((self.rows - used, LANES), f32))
        return jnp.concatenate(pieces, axis=0)

    def unpack(self, packed):
        out = {}
        for name, shape in self.shapes.items():
            at, rows = self.offsets[name]
            blk = packed[at:at + rows]
            out[name] = blk.reshape(shape) if shape[-1] >= LANES else blk[:, :shape[-1]].reshape(shape)
        return out


def kernel(x, mem, ffn1_norm, ffn1_w_gate_up, ffn1_w_down, mix_norm, ffn2_norm, ffn2_w_gate_up, ffn2_w_down, gdn_w_in, gdn_conv, gdn_A_log, gdn_dt_bias, gdn_out_norm, fox_w_in, w_out, mem_norm, mem_w_kv, kv_norm, kv_w, kv_b_f, final_norm, loss_target, m_ffn1_norm, m_ffn1_w_gate_up, m_ffn1_w_down, m_mix_norm, m_ffn2_norm, m_ffn2_w_gate_up, m_ffn2_w_down, m_gdn_w_in, m_gdn_conv, m_gdn_A_log, m_gdn_dt_bias, m_gdn_out_norm, m_fox_w_in, m_w_out, m_mem_norm, m_mem_w_kv, m_kv_norm, m_kv_w, m_kv_b_f, m_final_norm, v_ffn1_norm, v_ffn1_w_gate_up, v_ffn1_w_down, v_mix_norm, v_ffn2_norm, v_ffn2_w_gate_up, v_ffn2_w_down, v_gdn_w_in, v_gdn_conv, v_gdn_A_log, v_gdn_dt_bias, v_gdn_out_norm, v_fox_w_in, v_w_out, v_mem_norm, v_mem_w_kv, v_kv_norm, v_kv_w, v_kv_b_f, v_final_norm):
    weights = dict(ffn1_norm=ffn1_norm, ffn1_w_gate_up=ffn1_w_gate_up, ffn1_w_down=ffn1_w_down, mix_norm=mix_norm,
                   ffn2_norm=ffn2_norm, ffn2_w_gate_up=ffn2_w_gate_up, ffn2_w_down=ffn2_w_down, gdn_w_in=gdn_w_in,
                   gdn_conv=gdn_conv, gdn_A_log=gdn_A_log, gdn_dt_bias=gdn_dt_bias, gdn_out_norm=gdn_out_norm,
                   fox_w_in=fox_w_in, w_out=w_out, mem_norm=mem_norm, mem_w_kv=mem_w_kv, kv_norm=kv_norm, kv_w=kv_w,
                   kv_b_f=kv_b_f, final_norm=final_norm)
    mom_m = dict(ffn1_norm=m_ffn1_norm, ffn1_w_gate_up=m_ffn1_w_gate_up, ffn1_w_down=m_ffn1_w_down, mix_norm=m_mix_norm,
                 ffn2_norm=m_ffn2_norm, ffn2_w_gate_up=m_ffn2_w_gate_up, ffn2_w_down=m_ffn2_w_down, gdn_w_in=m_gdn_w_in,
                 gdn_conv=m_gdn_conv, gdn_A_log=m_gdn_A_log, gdn_dt_bias=m_gdn_dt_bias, gdn_out_norm=m_gdn_out_norm,
                 fox_w_in=m_fox_w_in, w_out=m_w_out, mem_norm=m_mem_norm, mem_w_kv=m_mem_w_kv, kv_norm=m_kv_norm,
                 kv_w=m_kv_w, kv_b_f=m_kv_b_f, final_norm=m_final_norm)
    mom_v = dict(ffn1_norm=v_ffn1_norm, ffn1_w_gate_up=v_ffn1_w_gate_up, ffn1_w_down=v_ffn1_w_down, mix_norm=v_mix_norm,
                 ffn2_norm=v_ffn2_norm, ffn2_w_gate_up=v_ffn2_w_gate_up, ffn2_w_down=v_ffn2_w_down, gdn_w_in=v_gdn_w_in,
                 gdn_conv=v_gdn_conv, gdn_A_log=v_gdn_A_log, gdn_dt_bias=v_gdn_dt_bias, gdn_out_norm=v_gdn_out_norm,
                 fox_w_in=v_fox_w_in, w_out=v_w_out, mem_norm=v_mem_norm, mem_w_kv=v_mem_w_kv, kv_norm=v_kv_norm,
                 kv_w=v_kv_w, kv_b_f=v_kv_b_f, final_norm=v_final_norm)
    names = list(weights)
    small_names = [n for n in names if weights[n].shape == mom_m[n].shape and n in (
        "ffn1_norm", "mix_norm", "ffn2_norm", "gdn_A_log", "gdn_dt_bias", "gdn_out_norm", "mem_norm", "kv_norm",
        "kv_b_f", "final_norm")]
    big_names = [n for n in names if n not in small_names]

    h = x[0]
    target = loss_target[0]
    mem_tokens = mem[0]
    s, d = h.shape
    depth = ffn1_norm.shape[0]
    n_a = gdn_w_in.shape[0]
    n_heads, head_dim = gdn_A_log.shape[1], gdn_out_norm.shape[1]
    gw = n_heads * head_dim
    a_in = gdn_w_in.shape[2]
    mem_w = a_in - 4 * gw - 2 * n_heads
    a_in_pad = 4 * gw + mem_w + LANES
    kv_width = kv_w.shape[1]
    kv_pad = 2 * gw + LANES
    fh = ffn1_w_down.shape[1] * N_DEV

    def permute_in(w):
        ab = w[..., 4 * gw:4 * gw + 2 * n_heads]
        return jnp.concatenate([w[..., :4 * gw], w[..., 4 * gw + 2 * n_heads:], _pad_lanes(ab, LANES)], axis=-1)

    def unpermute_in(w):
        return jnp.concatenate([w[..., :4 * gw], w[..., 4 * gw + mem_w:4 * gw + mem_w + 2 * n_heads],
                                w[..., 4 * gw:4 * gw + mem_w]], axis=-1)

    gathered = _all_gather("gather_weights", [
        ffn1_w_gate_up.astype(bf16), ffn1_w_down.astype(bf16), ffn2_w_gate_up.astype(bf16), ffn2_w_down.astype(bf16),
        permute_in(gdn_w_in).astype(bf16), fox_w_in.astype(bf16), w_out.astype(bf16), mem_w_kv.astype(bf16),
        _pad_lanes(kv_w, kv_pad).astype(bf16)[None], gdn_conv])
    wgu1, wd1_s, wgu2, wd2_s, win_s, wfox_s, wout_s, wmem_s, wkv_s, conv_s = gathered
    wd1, wd2 = _rows_from_shards(wd1_s), _rows_from_shards(wd2_s)
    win, wfox, wout = _rows_from_shards(win_s), _rows_from_shards(wfox_s), _rows_from_shards(wout_s)
    wmem = _rows_from_shards(wmem_s)
    wmem_cat = wmem.transpose(1, 0, 2).reshape(d, depth * 2 * mem_w)
    wkv = _rows_from_shards(wkv_s)[0]
    conv_w = conv_s.transpose(1, 2, 0, 3).reshape(n_a, gdn_conv.shape[1], 3 * gw)

    a_log_rows = [_row(gdn_A_log[l], LANES) for l in range(n_a)]
    dt_rows = [_row(gdn_dt_bias[l], LANES) for l in range(n_a)]
    onorm_rows = [_row(gdn_out_norm[l]) for l in range(n_a)]
    b_f_row = _row(kv_b_f, LANES)

    (mem_n,) = _rowwise("mem_norm", lambda t, g: (_rms(t, g),), [mem_tokens], [_row(mem_norm)], [(d, bf16)])
    mem_kv = _matmul("mem_kv", mem_n, wmem_cat, "nn", f32)

    saved = []
    shared = None
    for l in range(depth):
        rec = {"h0": h}
        h1 = _ffn_fwd(f"ffn1_fwd_{l}", h, _row(ffn1_norm[l]), wgu1, wd1, l)
        (u,) = _rowwise(f"mix_norm_{l}", lambda t, g: (_rms(t, g),), [h1], [_row(mix_norm[l])], [(d, bf16)])
        rec.update(h1=h1, u=u)
        if l < n_a:
            proj = _matmul(f"gdn_in_{l}", u, win[l], "nn", f32)
            yc = _conv_fwd(f"conv_fwd_{l}", V(proj, cb=0, w=3 * gw), conv_w[l])
            ab_view = V(proj, cb=(4 * gw + mem_w) // LANES, w=LANES)
            q, k, v, gb = _rowwise(f"gdn_pre_{l}", functools.partial(_gdn_pre, n_heads, head_dim),
                                   [yc, ab_view], [a_log_rows[l], dt_rows[l]],
                                   [(gw, f32), (gw, f32), (gw, f32), (LANES, f32)])
            o, states = _gdn_chunk_fwd(f"gdn_chunk_fwd_{l}", q, k, v, gb, n_heads, head_dim)
            z_view = V(proj, cb=3, w=gw)
            (main,) = _rowwise(f"gdn_post_{l}", functools.partial(_gdn_post, n_heads, head_dim),
                               [o, z_view], [onorm_rows[l]], [(gw, bf16)])
            qmem_view = V(proj, cb=4 * gw // mem_w, w=mem_w)
            rec.update(proj=proj, yc=yc, q=q, k=k, v=v, gb=gb, o=o, states=states)
        else:
            proj = _matmul(f"fox_in_{l}", u, wfox[l - n_a], "nn", bf16)
            sk, sv, ccol, crow = shared["k"], shared["v"], shared["ccol"], shared["crow"]
            main, lse = _fox_fwd(f"fox_fwd_{l}", proj, sk, sv, ccol, crow, n_heads, head_dim)
            qmem_view = V(proj, cb=gw // mem_w, w=mem_w)
            rec.update(proj=proj, lse=lse)
        km = V(mem_kv, cb=2 * l, w=mem_w)
        vm = V(mem_kv, cb=2 * l + 1, w=mem_w)
        (mem_out,) = _rowwise(f"mem_attn_{l}", _mem_attn, [qmem_view], [km, vm], [(mem_w, bf16)])
        cat = jnp.concatenate([main, mem_out], axis=1)
        h2 = _matmul(f"out_proj_{l}", cat, wout[l], "nn", f32, add=h1)
        h3 = _ffn_fwd(f"ffn2_fwd_{l}", h2, _row(ffn2_norm[l]), wgu2, wd2, l)
        rec.update(cat=cat, h2=h2, qmem=qmem_view)
        saved.append(rec)
        h = h3
        if l == n_a - 1:
            (hn,) = _rowwise("kv_norm", lambda t, g: (_rms(t, g),), [h], [_row(kv_norm)], [(d, bf16)])
            p = _matmul("kv_proj", hn, wkv, "nn", f32)
            pk, pv, pf = V(p, cb=0, w=gw), V(p, cb=1, w=gw), V(p, cb=2 * gw // LANES, w=LANES)
            sk, sv, log_f = _rowwise("kv_post", functools.partial(_kv_post, n_heads), [pk, pv, pf], [b_f_row],
                                     [(gw, bf16), (gw, bf16), (LANES, f32)])
            cum = _cumsum("forget_cumsum", [log_f], reverse=False)
            c_heads = cum[:, :n_heads].T
            shared = dict(k=sk, v=sv, ccol=c_heads.reshape(n_heads, s, 1), crow=c_heads.reshape(n_heads, 1, s),
                          h=h, hn=hn, p=p, views=(pk, pv, pf))

    loss_part, dh, d_final = _final_loss("final_loss", h, target, _row(final_norm))
    loss = lax.psum(loss_part[0, 0], ("x", "y", "c"))

    grads = {}
    per_layer = {n: [None] * depth for n in ("ffn1_norm", "mix_norm", "ffn2_norm", "ffn1_gu", "ffn1_d", "ffn2_gu",
                                             "ffn2_d", "w_out")}
    per_a = {n: [None] * n_a for n in ("gdn_w_in", "gdn_conv", "gdn_A_log", "gdn_dt_bias", "gdn_out_norm")}
    per_b = {"fox_w_in": [None] * (depth - n_a)}
    d_mem_kv = [None] * depth
    fox_grads = []

    def ffn_backward(tag, l, h_in, d_out, gain, wgu, wd):
        parts, dwg, dwu, dwd = _ffn_bwd(f"{tag}_bwd_{l}", h_in, d_out, _row(gain), wgu, wd, l)
        nh = parts.shape[0]
        (d_in,), (d_gain,) = _rowwise_vjp(
            f"{tag}_norm_bwd_{l}", lambda t, g: (_rms(t, g),), [h_in], [_row(gain)],
            [[V(parts, lead=t) for t in range(nh)]], [f32], add=d_out)
        return d_in, d_gain, jnp.concatenate([dwg, dwu], axis=0), dwd

    for l in reversed(range(depth)):
        rec = saved[l]
        if l == n_a - 1:
            dk_list = [V(g["dk"]) for g in fox_grads]
            dv_list = [V(g["dv"]) for g in fox_grads]
            dc_parts = [_pad_lanes(part.reshape(n_heads, s).T, LANES) for g in fox_grads for part in g["dc"]]
            d_log_f = _cumsum("forget_cumsum_bwd", dc_parts, reverse=True)
            pk, pv, pf = shared["views"]
            (dpk, dpv, dpf), (d_bf,) = _rowwise_vjp(
                "kv_post_bwd", functools.partial(_kv_post, n_heads), [pk, pv, pf], [b_f_row],
                [dk_list, dv_list, [d_log_f]], [bf16, bf16, bf16])
            dp = jnp.concatenate([dpk, dpv, dpf], axis=1)
            d_hn = _matmul("kv_proj_dx", dp, wkv, "nt", f32)
            grads["kv_w"] = _matmul("kv_proj_dw", shared["hn"], dp, "tn", f32)[:, :kv_width]
            (dh,), (d_kvn,) = _rowwise_vjp("kv_norm_bwd", lambda t, g: (_rms(t, g),), [shared["h"]], [_row(kv_norm)],
                                           [[d_hn]], [f32], add=dh)
            grads["kv_norm"] = d_kvn.reshape(-1)
            grads["kv_b_f"] = d_bf[0, :n_heads]

        dh2, per_layer["ffn2_norm"][l], per_layer["ffn2_gu"][l], per_layer["ffn2_d"][l] = ffn_backward(
            "ffn2", l, rec["h2"], dh, ffn2_norm[l], wgu2, wd2)
        d_cat = _matmul(f"out_proj_dx_{l}", dh2, wout[l], "nt", f32)
        per_layer["w_out"][l] = _matmul(f"out_proj_dw_{l}", rec["cat"], dh2, "tn", f32)
        d_main = V(d_cat, cb=0, w=gw)
        d_memo = V(d_cat, cb=gw // mem_w, w=mem_w)
        km, vm = V(mem_kv, cb=2 * l, w=mem_w), V(mem_kv, cb=2 * l + 1, w=mem_w)
        (dqmem,), (dkm, dvm) = _rowwise_vjp(f"mem_attn_bwd_{l}", _mem_attn, [rec["qmem"]], [km, vm], [[d_memo]], [bf16])
        d_mem_kv[l] = jnp.concatenate([dkm, dvm], axis=1)
        if l < n_a:
            proj = rec["proj"]
            z_view = V(proj, cb=3, w=gw)
            (d_o, d_z), (d_onorm,) = _rowwise_vjp(
                f"gdn_post_bwd_{l}", functools.partial(_gdn_post, n_heads, head_dim), [rec["o"], z_view],
                [onorm_rows[l]], [[d_main]], [f32, bf16])
            dq, dk, dv, dgb = _gdn_chunk_bwd(f"gdn_chunk_bwd_{l}", rec["q"], rec["k"], rec["v"], rec["gb"],
                                             rec["states"], d_o, n_heads, head_dim)
            ab_view = V(proj, cb=(4 * gw + mem_w) // LANES, w=LANES)
            (d_yc, d_ab), (d_alog, d_dt) = _rowwise_vjp(
                f"gdn_pre_bwd_{l}", functools.partial(_gdn_pre, n_heads, head_dim), [rec["yc"], ab_view],
                [a_log_rows[l], dt_rows[l]], [[dq], [dk], [dv], [dgb]], [f32, bf16])
            d_qkv, d_conv = _conv_bwd(f"conv_bwd_{l}", V(proj, cb=0, w=3 * gw), d_yc, conv_w[l])
            d_proj = jnp.concatenate([d_qkv, d_z, dqmem, d_ab], axis=1)
            du = _matmul(f"gdn_in_dx_{l}", d_proj, win[l], "nt", f32)
            per_a["gdn_w_in"][l] = unpermute_in(_matmul(f"gdn_in_dw_{l}", rec["u"], d_proj, "tn", f32))
            per_a["gdn_conv"][l] = d_conv
            per_a["gdn_A_log"][l] = d_alog[0, :n_heads]
            per_a["gdn_dt_bias"][l] = d_dt[0, :n_heads]
            per_a["gdn_out_norm"][l] = d_onorm[0]
        else:
            proj = rec["proj"]
            sk, sv, ccol, crow = shared["k"], shared["v"], shared["ccol"], shared["crow"]
            dq, delta, dc_col = _fox_bwd_dq(f"fox_dq_{l}", proj, sk, sv, ccol, crow, rec["lse"], d_cat, n_heads, head_dim)
            dk, dv, dc_row = _fox_bwd_dkv(f"fox_dkv_{l}", proj, sk, sv, ccol, crow, rec["lse"], delta, d_cat,
                                          n_heads, head_dim)
            fox_grads.append(dict(dk=dk, dv=dv, dc=(dc_row, dc_col)))
            d_proj = jnp.concatenate([dq, dqmem], axis=1)
            du = _matmul(f"fox_in_dx_{l}", d_proj, wfox[l - n_a], "nt", f32)
            per_b["fox_w_in"][l - n_a] = _matmul(f"fox_in_dw_{l}", rec["u"], d_proj, "tn", f32)
        (dh1,), (d_mix,) = _rowwise_vjp(f"mix_norm_bwd_{l}", lambda t, g: (_rms(t, g),), [rec["h1"]],
                                        [_row(mix_norm[l])], [[du]], [f32], add=dh2)
        per_layer["mix_norm"][l] = d_mix
        dh, per_layer["ffn1_norm"][l], per_layer["ffn1_gu"][l], per_layer["ffn1_d"][l] = ffn_backward(
            "ffn1", l, rec["h0"], dh1, ffn1_norm[l], wgu1, wd1)

    grad_x = dh[None]

    d_mem_kv_cat = jnp.concatenate(d_mem_kv, axis=1)
    d_wmem_cat = _matmul("mem_kv_dw", mem_n, d_mem_kv_cat, "tn", f32)
    d_mem_n = _matmul("mem_kv_dx", d_mem_kv_cat, wmem_cat, "nt", f32)
    _, (d_memnorm,) = _rowwise_vjp("mem_norm_bwd", lambda t, g: (_rms(t, g),), [mem_tokens], [_row(mem_norm)],
                                   [[d_mem_n]], [None])

    def gu_stack(per):
        return jnp.stack(per, axis=1).astype(bf16)

    stacks = dict(
        ffn1_w_gate_up=gu_stack(per_layer["ffn1_gu"]),
        ffn1_w_down=_rows_to_shards(jnp.stack(per_layer["ffn1_d"])).astype(bf16),
        ffn2_w_gate_up=gu_stack(per_layer["ffn2_gu"]),
        ffn2_w_down=_rows_to_shards(jnp.stack(per_layer["ffn2_d"])).astype(bf16),
        gdn_w_in=_rows_to_shards(jnp.stack(per_a["gdn_w_in"])).astype(bf16),
        gdn_conv=jnp.stack(per_a["gdn_conv"]).reshape(n_a, -1, N_DEV, 3 * gw // N_DEV).transpose(2, 0, 1, 3),
        fox_w_in=_rows_to_shards(jnp.stack(per_b["fox_w_in"])).astype(bf16),
        w_out=_rows_to_shards(jnp.stack(per_layer["w_out"])).astype(bf16),
        mem_w_kv=_rows_to_shards(d_wmem_cat.reshape(d, depth, 2 * mem_w).transpose(1, 0, 2)).astype(bf16),
        kv_w=_rows_to_shards(grads["kv_w"][None])[:, 0].astype(bf16),
    )
    received = dict(zip(big_names, _exchange("exchange_grads", [stacks[n] for n in big_names])))

    small_shapes = {n: weights[n].shape for n in small_names}
    pack = _SmallPack(small_shapes)
    small_grads = dict(
        ffn1_norm=jnp.concatenate(per_layer["ffn1_norm"], axis=0), mix_norm=jnp.concatenate(per_layer["mix_norm"], axis=0),
        ffn2_norm=jnp.concatenate(per_layer["ffn2_norm"], axis=0), gdn_A_log=jnp.stack(per_a["gdn_A_log"]),
        gdn_dt_bias=jnp.stack(per_a["gdn_dt_bias"]), gdn_out_norm=jnp.stack(per_a["gdn_out_norm"]),
        mem_norm=d_memnorm.reshape(-1), kv_norm=grads["kv_norm"], kv_b_f=grads["kv_b_f"], final_norm=d_final.reshape(-1))
    (small_parts,) = _all_gather("gather_small_grads", [pack.pack(small_grads)])

    out_g, out_d, out_m, out_v = {}, {}, {}, {}
    for n in big_names:
        shape = weights[n].shape
        c = shape[-1]
        parts = received[n].reshape(N_DEV, -1, c)
        res = _adamw(f"adamw_{n}", parts, weights[n].reshape(-1, c), mom_m[n].reshape(-1, c), mom_v[n].reshape(-1, c))
        out_g[n], out_d[n], out_m[n], out_v[n] = [r.reshape(shape) for r in res]
    res = _adamw("adamw_small", small_parts, pack.pack({n: weights[n] for n in small_names}),
                 pack.pack({n: mom_m[n] for n in small_names}), pack.pack({n: mom_v[n] for n in small_names}))
    for dst, packed in zip((out_g, out_d, out_m, out_v), res):
        dst.update(pack.unpack(packed))

    return (loss, grad_x, *[out_g[n] for n in names], *[out_d[n] for n in names],
            *[out_m[n] for n in names], *[out_v[n] for n in names])
```

```python
import functools
import math

import jax
import jax.numpy as jnp
from jax import lax
from jax.experimental import pallas as pl
from jax.experimental.pallas import tpu as pltpu

f32 = jnp.float32
bf16 = jnp.bfloat16
SDS = jax.ShapeDtypeStruct
HIGHEST = lax.Precision.HIGHEST

N_DEV = 8
MEM_HEADS = 4
CHUNK = 64
LANES = 128
EPS = 1e-6
NEG_INF = -1e30
ADAM_LR = 0.001
ADAM_B1 = 0.9
ADAM_B2 = 0.999
ADAM_EPS = 1e-08
ADAM_WD = 0.01
ADAM_STEP = 10

ROW_TILE = 256
MM_TILE = 512
FFN_FWD_TILE = 512
FFN_BWD_TILE = 256
ATT_TILE = 512
ATT_HEADS = 3
CUMSUM_TILE = 256
VMEM_LIMIT = 56 * 1024 * 1024

MESH = pl.DeviceIdType.MESH


def _cp(sem=None):
    return pltpu.CompilerParams(dimension_semantics=sem, vmem_limit_bytes=VMEM_LIMIT)


_DIMS = {"nn": (((1,), (0,)), ((), ())), "nt": (((1,), (1,)), ((), ())), "tn": (((0,), (0,)), ((), ()))}


def _dg(a, b, mode):
    return lax.dot_general(a.astype(bf16), b.astype(bf16), _DIMS[mode], preferred_element_type=f32)


@functools.partial(jax.custom_vjp, nondiff_argnums=(2,))
def _mm(a, b, mode):
    return _dg(a, b, mode)


def _mm_fwd(a, b, mode):
    return _dg(a, b, mode), (a, b)


def _mm_bwd(mode, res, ct):
    a, b = res
    if mode == "nn":
        da, db = _dg(ct, b, "nt"), _dg(a, ct, "tn")
    elif mode == "nt":
        da, db = _dg(ct, b, "nn"), _dg(ct, a, "tn")
    else:
        da, db = _dg(b, ct, "nt"), _dg(a, ct, "nn")
    return da.astype(a.dtype), db.astype(b.dtype)


_mm.defvjp(_mm_fwd, _mm_bwd)


def _dgh(a, b, mode="nn"):
    return lax.dot_general(a, b, _DIMS[mode], precision=HIGHEST, preferred_element_type=f32)


@jax.custom_vjp
def _unit_lower_inverse(low):
    c = low.shape[0]
    ri = lax.broadcasted_iota(jnp.int32, (c, c), 0)
    ci = lax.broadcasted_iota(jnp.int32, (c, c), 1)
    x = -low
    r = jnp.where(ri == ci, 1.0, 0.0) + x
    for _ in range(int(math.log2(c)) - 1):
        x = _dgh(x, x)
        r = r + _dgh(r, x)
    return r


def _uli_fwd(low):
    t = _unit_lower_inverse(low)
    return t, t


def _uli_bwd(t, ct):
    return (-_dgh(t, _dgh(ct, t, "nt"), "tn"),)


_unit_lower_inverse.defvjp(_uli_fwd, _uli_bwd)


@functools.partial(jax.custom_vjp, nondiff_argnums=(1,))
def _split_lanes(x, width):
    return tuple(x[:, i * width:(i + 1) * width] for i in range(x.shape[1] // width))


def _split_fwd(x, width):
    return _split_lanes(x, width), None


def _split_bwd(width, _, cts):
    return (jnp.concatenate(list(cts), axis=1),)


_split_lanes.defvjp(_split_fwd, _split_bwd)


def _sigmoid(x):
    return 1.0 / (1.0 + jnp.exp(-x))


def _silu(x):
    return x * _sigmoid(x)


def _softplus(x):
    return jnp.maximum(x, 0.0) + jnp.log1p(jnp.exp(-jnp.abs(x)))


def _rms(x, gain):
    return x * lax.rsqrt(jnp.mean(x * x, axis=-1, keepdims=True) + EPS) * gain


class V:
    def __init__(self, arr, lead=None, cb=0, w=None):
        self.arr, self.lead, self.cb = arr, lead, cb
        self.w = arr.shape[-1] if w is None else w

    @property
    def rows(self):
        return self.arr.shape[-2]

    def spec(self, tile, order=None):
        lead, cb, w = self.lead, self.cb, self.w
        order = order or (lambda i: i)
        if lead is None:
            return pl.BlockSpec((tile, w), lambda i: (order(i), cb))
        return pl.BlockSpec((None, tile, w), lambda i: (lead, order(i), cb))

    def const_spec(self):
        lead, cb, w, r = self.lead, self.cb, self.w, self.rows
        if lead is None:
            return pl.BlockSpec((r, w), lambda i: (0, cb))
        return pl.BlockSpec((None, r, w), lambda i: (lead, 0, cb))


def _v(a):
    return a if isinstance(a, V) else V(a)


def _rowwise(name, fn, rows, consts, outs, tile=None):
    rows = [_v(r) for r in rows]
    consts = [_v(c) for c in consts]
    s = rows[0].rows
    tile = min(tile or ROW_TILE, s)
    nr, nc = len(rows), len(consts)

    def body(*refs):
        vals = [r[...].astype(f32) for r in refs[:nr + nc]]
        res = fn(*vals)
        for o, val in zip(refs[nr + nc:], res):
            o[...] = val.astype(o.dtype)

    return pl.pallas_call(
        body, name=name, grid=(s // tile,),
        in_specs=[r.spec(tile) for r in rows] + [c.const_spec() for c in consts],
        out_specs=[pl.BlockSpec((tile, w), lambda i: (i, 0)) for w, _ in outs],
        out_shape=[SDS((s, w), dt) for w, dt in outs],
        compiler_params=_cp(("parallel",)),
    )(*[r.arr for r in rows], *[c.arr for c in consts])


def _rowwise_vjp(name, fn, rows, consts, cts, d_rows, add=None, tile=None):
    rows = [_v(r) for r in rows]
    consts = [_v(c) for c in consts]
    cts = [[_v(c) for c in group] for group in cts]
    flat_cts = [c for group in cts for c in group]
    s = rows[0].rows
    tile = min(tile or ROW_TILE, s)
    nr, nc, nt = len(rows), len(consts), len(flat_cts)
    want = [k for k, dt in enumerate(d_rows) if dt is not None]
    has_add = add is not None
    add_v = [_v(add)] if has_add else []

    def body(*refs):
        vals = [r[...].astype(f32) for r in refs[:nr + nc]]
        ct_refs = refs[nr + nc:nr + nc + nt]
        pos = nr + nc + nt
        add_ref = refs[pos] if has_add else None
        pos += 1 if has_add else 0
        drow_refs = refs[pos:pos + len(want)]
        dconst_refs = refs[pos + len(want):]
        ctv, at = [], 0
        for group in cts:
            acc = ct_refs[at][...].astype(f32)
            for r in ct_refs[at + 1:at + len(group)]:
                acc = acc + r[...].astype(f32)
            at += len(group)
            ctv.append(acc)
        _, vjp = jax.vjp(fn, *vals)
        grads = vjp(tuple(ctv))
        for o, k in zip(drow_refs, want):
            g = grads[k]
            if has_add and k == want[0]:
                g = g + add_ref[...].astype(f32)
            o[...] = g.astype(o.dtype)

        @pl.when(pl.program_id(0) == 0)
        def _():
            for o in dconst_refs:
                o[...] = jnp.zeros_like(o)

        for o, g in zip(dconst_refs, grads[nr:]):
            o[...] += g

    outs = pl.pallas_call(
        body, name=name, grid=(s // tile,),
        in_specs=[r.spec(tile) for r in rows] + [c.const_spec() for c in consts]
        + [c.spec(tile) for c in flat_cts] + [a.spec(tile) for a in add_v],
        out_specs=[pl.BlockSpec((tile, rows[k].w), lambda i: (i, 0)) for k in want]
        + [pl.BlockSpec((c.rows, c.w), lambda i: (0, 0)) for c in consts],
        out_shape=[SDS((s, rows[k].w), d_rows[k]) for k in want] + [SDS((c.rows, c.w), f32) for c in consts],
        compiler_params=_cp(("arbitrary",)),
    )(*[r.arr for r in rows], *[c.arr for c in consts], *[c.arr for c in flat_cts], *[a.arr for a in add_v])
    return list(outs[:len(want)]), list(outs[len(want):])


def _pick(n, cap):
    best = None
    for t in range(LANES, min(n, cap) + 1, LANES):
        if n % t == 0:
            best = t
    return best or n


def _matmul(name, a, b, mode, out_dtype, add=None, tn_cap=1280):
    has_add = add is not None
    if mode in ("nn", "nt"):
        m, k = a.shape
        n = b.shape[1] if mode == "nn" else b.shape[0]
        tm = min(MM_TILE, m)
        tn = _pick(n, tn_cap) if k * n * 2 > (8 << 20) else n

        def body(*refs):
            a_ref, b_ref = refs[0], refs[1]
            o_ref = refs[-1]
            acc = _dg(a_ref[...], b_ref[...], mode)
            if has_add:
                acc = acc + refs[2][...].astype(f32)
            o_ref[...] = acc.astype(o_ref.dtype)

        b_spec = pl.BlockSpec((k, tn), lambda i, j: (0, j)) if mode == "nn" else pl.BlockSpec((tn, k), lambda i, j: (j, 0))
        in_specs = [pl.BlockSpec((tm, k), lambda i, j: (i, 0)), b_spec]
        args = [a, b]
        if has_add:
            in_specs.append(pl.BlockSpec((tm, tn), lambda i, j: (i, j)))
            args.append(add)
        return pl.pallas_call(
            body, name=name, grid=(m // tm, n // tn), in_specs=in_specs,
            out_specs=pl.BlockSpec((tm, tn), lambda i, j: (i, j)),
            out_shape=SDS((m, n), out_dtype), compiler_params=_cp(("parallel", "parallel")),
        )(*args)
    kk, m = a.shape
    n = b.shape[1]
    tk = min(MM_TILE, kk)
    tn = _pick(n, tn_cap) if m * n * 4 > (6 << 20) else n

    def body_tn(a_ref, b_ref, o_ref):
        @pl.when(pl.program_id(1) == 0)
        def _():
            o_ref[...] = jnp.zeros_like(o_ref)

        o_ref[...] += _dg(a_ref[...], b_ref[...], "tn")

    return pl.pallas_call(
        body_tn, name=name, grid=(n // tn, kk // tk),
        in_specs=[pl.BlockSpec((tk, m), lambda j, k: (k, 0)), pl.BlockSpec((tk, tn), lambda j, k: (k, j))],
        out_specs=pl.BlockSpec((m, tn), lambda j, k: (0, j)),
        out_shape=SDS((m, n), f32), compiler_params=_cp(("parallel", "arbitrary")),
    )(a, b)


def _ffn_fwd(name, h, gain, wgu, wd, layer):
    s, d = h.shape
    hs = wgu.shape[3]
    nh = wgu.shape[0] // 2
    tm = min(FFN_FWD_TILE, s)

    def body(h_ref, g_ref, wg_ref, wu_ref, wd_ref, o_ref, n_scr, acc_scr):
        t = pl.program_id(1)

        @pl.when(t == 0)
        def _():
            n_scr[...] = _rms(h_ref[...], g_ref[...]).astype(bf16)
            acc_scr[...] = jnp.zeros_like(acc_scr)

        n = n_scr[...]
        gate = _dg(n, wg_ref[...], "nn")
        up = _dg(n, wu_ref[...], "nn")
        acc_scr[...] += _dg(_silu(gate) * up, wd_ref[...], "nn")

        @pl.when(t == nh - 1)
        def _():
            o_ref[...] = h_ref[...] + 0.5 * acc_scr[...]

    return pl.pallas_call(
        body, name=name, grid=(s // tm, nh),
        in_specs=[
            pl.BlockSpec((tm, d), lambda i, t: (i, 0)),
            pl.BlockSpec((1, d), lambda i, t: (0, 0)),
            pl.BlockSpec((None, None, d, hs), lambda i, t: (t, layer, 0, 0)),
            pl.BlockSpec((None, None, d, hs), lambda i, t: (t + nh, layer, 0, 0)),
            pl.BlockSpec((None, hs, d), lambda i, t: (layer, t, 0)),
        ],
        out_specs=pl.BlockSpec((tm, d), lambda i, t: (i, 0)),
        out_shape=SDS((s, d), f32),
        scratch_shapes=[pltpu.VMEM((tm, d), bf16), pltpu.VMEM((tm, d), f32)],
        compiler_params=_cp(("parallel", "arbitrary")),
    )(h, gain, wgu, wgu, wd)


def _ffn_bwd(name, h, dout, gain, wgu, wd, layer):
    s, d = h.shape
    hs = wgu.shape[3]
    nh = wgu.shape[0] // 2
    tm = min(FFN_BWD_TILE, s)

    def body(h_ref, do_ref, g_ref, wg_ref, wu_ref, wd_ref, dn_ref, dwg_ref, dwu_ref, dwd_ref):
        @pl.when(pl.program_id(1) == 0)
        def _():
            dwg_ref[...] = jnp.zeros_like(dwg_ref)
            dwu_ref[...] = jnp.zeros_like(dwu_ref)
            dwd_ref[...] = jnp.zeros_like(dwd_ref)

        n = _rms(h_ref[...], g_ref[...]).astype(bf16)
        wg, wu, wdn = wg_ref[...], wu_ref[...], wd_ref[...]
        gate = _dg(n, wg, "nn")
        up = _dg(n, wu, "nn")
        sg = _sigmoid(gate)
        act = gate * sg
        dy = (0.5 * do_ref[...]).astype(bf16)
        da = _dg(dy, wdn, "nt")
        dup = (da * act).astype(bf16)
        dgate = (da * up * (sg * (1.0 + gate * (1.0 - sg)))).astype(bf16)
        dwd_ref[...] += _dg(act * up, dy, "tn")
        dwg_ref[...] += _dg(n, dgate, "tn")
        dwu_ref[...] += _dg(n, dup, "tn")
        dn_ref[...] = _dg(dgate, wg, "nt") + _dg(dup, wu, "nt")

    return pl.pallas_call(
        body, name=name, grid=(nh, s // tm),
        in_specs=[
            pl.BlockSpec((tm, d), lambda t, i: (i, 0)),
            pl.BlockSpec((tm, d), lambda t, i: (i, 0)),
            pl.BlockSpec((1, d), lambda t, i: (0, 0)),
            pl.BlockSpec((None, None, d, hs), lambda t, i: (t, layer, 0, 0)),
            pl.BlockSpec((None, None, d, hs), lambda t, i: (t + nh, layer, 0, 0)),
            pl.BlockSpec((None, hs, d), lambda t, i: (layer, t, 0)),
        ],
        out_specs=[
            pl.BlockSpec((None, tm, d), lambda t, i: (t, i, 0)),
            pl.BlockSpec((None, d, hs), lambda t, i: (t, 0, 0)),
            pl.BlockSpec((None, d, hs), lambda t, i: (t, 0, 0)),
            pl.BlockSpec((hs, d), lambda t, i: (t, 0)),
        ],
        out_shape=[SDS((nh, s, d), f32), SDS((nh, d, hs), f32), SDS((nh, d, hs), f32), SDS((nh * hs, d), f32)],
        compiler_params=_cp(("parallel", "arbitrary")),
    )(h, dout, gain, wgu, wgu, wd)


def _conv_fwd(name, x, w):
    x = _v(x)
    s, c = x.rows, x.w
    cw = w.shape[0]
    tile = min(ROW_TILE, s)
    cb = x.cb

    def body(x_ref, halo_ref, w_ref, o_ref, buf):
        first = pl.program_id(0) == 0
        buf[0:8, :] = jnp.where(first, 0.0, halo_ref[...])
        buf[8:8 + tile, :] = x_ref[...]
        acc = w_ref[0:1, :] * buf[pl.ds(8 - cw + 1, tile), :]
        for j in range(1, cw):
            acc = acc + w_ref[j:j + 1, :] * buf[pl.ds(8 - cw + 1 + j, tile), :]
        o_ref[...] = acc

    return pl.pallas_call(
        body, name=name, grid=(s // tile,),
        in_specs=[
            pl.BlockSpec((tile, c), lambda i: (i, cb)),
            pl.BlockSpec((8, c), lambda i: (jnp.maximum(i * (tile // 8) - 1, 0), cb)),
            pl.BlockSpec((cw, c), lambda i: (0, 0)),
        ],
        out_specs=pl.BlockSpec((tile, c), lambda i: (i, 0)),
        out_shape=SDS((s, c), f32),
        scratch_shapes=[pltpu.VMEM((tile + 8, c), f32)],
        compiler_params=_cp(("parallel",)),
    )(x.arr, x.arr, w)


def _conv_bwd(name, x, dy, w):
    x = _v(x)
    s, c = x.rows, x.w
    cw = w.shape[0]
    tile = min(ROW_TILE, s)
    n_tiles = s // tile
    cb = x.cb

    def body(x_ref, xh_ref, dy_ref, dyh_ref, w_ref, dx_ref, dw_ref, xbuf, dbuf):
        i = pl.program_id(0)
        xbuf[0:8, :] = jnp.where(i == 0, 0.0, xh_ref[...])
        xbuf[8:8 + tile, :] = x_ref[...]
        dyv = dy_ref[...]
        dbuf[0:tile, :] = dyv
        dbuf[tile:tile + 8, :] = jnp.where(i == n_tiles - 1, 0.0, dyh_ref[...])

        @pl.when(i == 0)
        def _():
            dw_ref[...] = jnp.zeros_like(dw_ref)

        acc = w_ref[0:1, :] * dbuf[pl.ds(cw - 1, tile), :]
        for j in range(1, cw):
            acc = acc + w_ref[j:j + 1, :] * dbuf[pl.ds(cw - 1 - j, tile), :]
        dx_ref[...] = acc.astype(dx_ref.dtype)
        for j in range(cw):
            dw_ref[j:j + 1, :] += jnp.sum(xbuf[pl.ds(8 - cw + 1 + j, tile), :] * dyv, axis=0, keepdims=True)

    return pl.pallas_call(
        body, name=name, grid=(n_tiles,),
        in_specs=[
            pl.BlockSpec((tile, c), lambda i: (i, cb)),
            pl.BlockSpec((8, c), lambda i: (jnp.maximum(i * (tile // 8) - 1, 0), cb)),
            pl.BlockSpec((tile, c), lambda i: (i, 0)),
            pl.BlockSpec((8, c), lambda i: (jnp.minimum((i + 1) * (tile // 8), s // 8 - 1), 0)),
            pl.BlockSpec((cw, c), lambda i: (0, 0)),
        ],
        out_specs=[pl.BlockSpec((tile, c), lambda i: (i, 0)), pl.BlockSpec((cw, c), lambda i: (0, 0))],
        out_shape=[SDS((s, c), bf16), SDS((cw, c), f32)],
        scratch_shapes=[pltpu.VMEM((tile + 8, c), f32), pltpu.VMEM((tile + 8, c), f32)],
        compiler_params=_cp(("arbitrary",)),
    )(x.arr, x.arr, dy, dy, w)


def _gdn_pre(n_heads, head_dim, yc, ab, a_log, dt_bias):
    gw = n_heads * head_dim
    act = _silu(yc)
    parts = _split_lanes(act, head_dim)
    qs = [p * lax.rsqrt(jnp.sum(p * p, axis=-1, keepdims=True) + EPS) * (head_dim ** -0.5) for p in parts[:n_heads]]
    ks = [p * lax.rsqrt(jnp.sum(p * p, axis=-1, keepdims=True) + EPS) for p in parts[n_heads:2 * n_heads]]
    lane = lax.broadcasted_iota(jnp.int32, ab.shape, 1)
    g = -jnp.exp(a_log) * _softplus(ab + dt_bias)
    gb = jnp.where(lane < n_heads, g, jnp.where(lane < 2 * n_heads, _sigmoid(ab), 0.0))
    del gw
    return (jnp.concatenate(qs, axis=1), jnp.concatenate(ks, axis=1),
            jnp.concatenate(list(parts[2 * n_heads:]), axis=1), gb)


def _gdn_post(n_heads, head_dim, o, z, out_norm):
    parts = _split_lanes(o, head_dim)
    normed = jnp.concatenate([_rms(p, out_norm) for p in parts], axis=1)
    return (normed * _silu(z),)


def _gdn_chunk(n_heads, head_dim, q, k, v, gb, *states):
    c = q.shape[0]
    ri = lax.broadcasted_iota(jnp.int32, (c, c), 0)
    ci = lax.broadcasted_iota(jnp.int32, (c, c), 1)
    incl, strict, diag = ri >= ci, ri > ci, ri == ci
    lane = lax.broadcasted_iota(jnp.int32, gb.shape, 1)
    qs, ks, vs = _split_lanes(q, head_dim), _split_lanes(k, head_dim), _split_lanes(v, head_dim)
    outs, new_states = [], []
    for h in range(n_heads):
        qh, kh, vh, st = qs[h], ks[h], vs[h], states[h]
        g = jnp.sum(jnp.where(lane == h, gb, 0.0), axis=1, keepdims=True)
        beta = jnp.sum(jnp.where(lane == n_heads + h, gb, 0.0), axis=1, keepdims=True)
        g_row = jnp.sum(jnp.where(diag, g, 0.0), axis=0, keepdims=True)
        cg_col = jnp.sum(jnp.where(incl, g_row, 0.0), axis=1, keepdims=True)
        cg_row = jnp.sum(jnp.where(ri <= ci, g, 0.0), axis=0, keepdims=True)
        g_last = jnp.sum(g, axis=0, keepdims=True)
        decay = jnp.where(incl, jnp.exp(jnp.where(incl, cg_col - cg_row, 0.0)), 0.0)
        kb = kh * beta
        lower = jnp.where(strict, _mm(kb, kh, "nt") * decay, 0.0)
        off_diag = _unit_lower_inverse(lower) - jnp.where(diag, 1.0, 0.0)
        vb, kbg = vh * beta, kb * jnp.exp(cg_col)
        u = vb + _mm(off_diag, vb, "nn")
        w = kbg + _mm(off_diag, kbg, "nn")
        qk = jnp.where(incl, _mm(qh, kh, "nt") * decay, 0.0)
        v_new = u - _mm(w, st, "nn")
        outs.append(_mm(qh * jnp.exp(cg_col), st, "nn") + _mm(qk, v_new, "nn"))
        new_states.append(st * jnp.exp(g_last) + _mm(kh * jnp.exp(g_last - cg_col), v_new, "tn"))
    return (jnp.concatenate(outs, axis=1), *new_states)


def _gdn_chunk_fwd(name, q, k, v, gb, n_heads, head_dim):
    s, gw = q.shape
    n = s // CHUNK
    fn = functools.partial(_gdn_chunk, n_heads, head_dim)

    def body(q_ref, k_ref, v_ref, gb_ref, o_ref, st_ref, st_scr):
        @pl.when(pl.program_id(0) == 0)
        def _():
            st_scr[...] = jnp.zeros_like(st_scr)

        st_ref[...] = st_scr[...]
        res = fn(q_ref[...], k_ref[...], v_ref[...], gb_ref[...], *[st_scr[h] for h in range(n_heads)])
        o_ref[...] = res[0]
        for h in range(n_heads):
            st_scr[h] = res[1 + h]

    row = lambda w: pl.BlockSpec((CHUNK, w), lambda i: (i, 0))
    return pl.pallas_call(
        body, name=name, grid=(n,),
        in_specs=[row(gw), row(gw), row(gw), row(LANES)],
        out_specs=[row(gw), pl.BlockSpec((None, n_heads, head_dim, head_dim), lambda i: (i, 0, 0, 0))],
        out_shape=[SDS((s, gw), f32), SDS((n, n_heads, head_dim, head_dim), f32)],
        scratch_shapes=[pltpu.VMEM((n_heads, head_dim, head_dim), f32)],
        compiler_params=_cp(("arbitrary",)),
    )(q, k, v, gb)


def _gdn_chunk_bwd(name, q, k, v, gb, states, d_out, n_heads, head_dim):
    s, gw = q.shape
    n = s // CHUNK
    fn = functools.partial(_gdn_chunk, n_heads, head_dim)

    def body(q_ref, k_ref, v_ref, gb_ref, st_ref, do_ref, dq_ref, dk_ref, dv_ref, dgb_ref, dst_scr):
        @pl.when(pl.program_id(0) == 0)
        def _():
            dst_scr[...] = jnp.zeros_like(dst_scr)

        _, vjp = jax.vjp(fn, q_ref[...], k_ref[...], v_ref[...], gb_ref[...], *[st_ref[h] for h in range(n_heads)])
        grads = vjp((do_ref[...].astype(f32), *[dst_scr[h] for h in range(n_heads)]))
        dq_ref[...] = grads[0]
        dk_ref[...] = grads[1]
        dv_ref[...] = grads[2]
        dgb_ref[...] = grads[3]
        for h in range(n_heads):
            dst_scr[h] = grads[4 + h]

    row = lambda w: pl.BlockSpec((CHUNK, w), lambda i: (n - 1 - i, 0))
    return pl.pallas_call(
        body, name=name, grid=(n,),
        in_specs=[row(gw), row(gw), row(gw), row(LANES),
                  pl.BlockSpec((None, n_heads, head_dim, head_dim), lambda i: (n - 1 - i, 0, 0, 0)), row(gw)],
        out_specs=[row(gw), row(gw), row(gw), row(LANES)],
        out_shape=[SDS((s, gw), f32), SDS((s, gw), f32), SDS((s, gw), f32), SDS((s, LANES), f32)],
        scratch_shapes=[pltpu.VMEM((n_heads, head_dim, head_dim), f32)],
        compiler_params=_cp(("arbitrary",)),
    )(q, k, v, gb, states, d_out)


def _mem_attn(qm, km, vm):
    width = qm.shape[1]
    hd = width // MEM_HEADS
    lane = lax.broadcasted_iota(jnp.int32, (1, width), 1)
    out = jnp.zeros_like(qm)
    for h in range(MEM_HEADS):
        msk = jnp.where((lane >= h * hd) & (lane < (h + 1) * hd), 1.0, 0.0)
        logits = _mm(qm * msk, km, "nt") * (hd ** -0.5)
        p = jnp.exp(logits - jnp.max(logits, axis=-1, keepdims=True))
        p = p / jnp.sum(p, axis=-1, keepdims=True)
        out = out + _mm(p, vm, "nn") * msk
    return (out,)


def _kv_post(n_heads, pk, pv, pf, b_f):
    lane = lax.broadcasted_iota(jnp.int32, pf.shape, 1)
    log_f = jnp.where(lane < n_heads, -_softplus(-(pf + b_f)), 0.0)
    return pk, pv, log_f


def _cumsum(name, xs, reverse):
    s, w = xs[0].shape
    tile = min(CUMSUM_TILE, s)
    n = s // tile

    def body(*refs):
        x_refs, o_ref, carry = refs[:-2], refs[-2], refs[-1]

        @pl.when(pl.program_id(0) == 0)
        def _():
            carry[...] = jnp.zeros_like(carry)

        xv = x_refs[0][...]
        for r in x_refs[1:]:
            xv = xv + r[...]
        ri = lax.broadcasted_iota(jnp.int32, (tile, tile), 0)
        ci = lax.broadcasted_iota(jnp.int32, (tile, tile), 1)
        tri = jnp.where((ri <= ci) if reverse else (ri >= ci), 1.0, 0.0).astype(bf16)
        x1 = xv.astype(bf16)
        r1 = xv - x1.astype(f32)
        x2 = r1.astype(bf16)
        x3 = (r1 - x2.astype(f32)).astype(bf16)
        acc = carry[...] + _dg(tri, x1, "nn") + _dg(tri, x2, "nn") + _dg(tri, x3, "nn")
        o_ref[...] = acc
        carry[...] += jnp.sum(xv, axis=0, keepdims=True)

    order = (lambda i: (n - 1 - i, 0)) if reverse else (lambda i: (i, 0))
    return pl.pallas_call(
        body, name=name, grid=(n,),
        in_specs=[pl.BlockSpec((tile, w), order)] * len(xs), out_specs=pl.BlockSpec((tile, w), order),
        out_shape=SDS((s, w), f32), scratch_shapes=[pltpu.VMEM((1, w), f32)],
        compiler_params=_cp(("arbitrary",)),
    )(*xs)


def _fox_logits(q_ref, k_ref, cr_ref, hh, head_dim, scale, diagonal):
    sl = slice(hh * head_dim, (hh + 1) * head_dim)
    s = _dg(q_ref[:, sl], k_ref[:, sl], "nt") * scale - cr_ref[hh]
    if diagonal:
        tq, tk = s.shape
        ok = lax.broadcasted_iota(jnp.int32, (tq, tk), 1) <= lax.broadcasted_iota(jnp.int32, (tq, tk), 0)
        s = jnp.where(ok, s, NEG_INF)
    return s, sl


def _on_causal_tiles(i, j, fn):
    @pl.when(j < i)
    def _():
        fn(False)

    @pl.when(j == i)
    def _():
        fn(True)


def _fox_fwd(name, q, k, v, crow, n_heads, head_dim):
    s = k.shape[0]
    tq = tk = min(ATT_TILE, s)
    nq, nk = s // tq, s // tk
    scale = head_dim ** -0.5
    hpb = ATT_HEADS
    wb = hpb * head_dim

    def body(q_ref, k_ref, v_ref, cr_ref, o_ref, lse_ref, m_scr, l_scr, acc_scr):
        i, j = pl.program_id(1), pl.program_id(2)

        @pl.when(j == 0)
        def _():
            m_scr[...] = jnp.full_like(m_scr, NEG_INF)
            l_scr[...] = jnp.zeros_like(l_scr)
            acc_scr[...] = jnp.zeros_like(acc_scr)

        def step(diagonal):
            for hh in range(hpb):
                sc, sl = _fox_logits(q_ref, k_ref, cr_ref, hh, head_dim, scale, diagonal)
                m_old = m_scr[hh]
                m_new = jnp.maximum(m_old, jnp.max(sc, axis=-1, keepdims=True))
                p = jnp.exp(sc - m_new)
                alpha = jnp.exp(m_old - m_new)
                l_scr[hh] = alpha * l_scr[hh] + jnp.sum(p, axis=-1, keepdims=True)
                acc_scr[:, sl] = alpha * acc_scr[:, sl] + _dg(p, v_ref[:, sl], "nn")
                m_scr[hh] = m_new

        _on_causal_tiles(i, j, step)

        @pl.when(j == nk - 1)
        def _():
            for hh in range(hpb):
                sl = slice(hh * head_dim, (hh + 1) * head_dim)
                o_ref[:, sl] = (acc_scr[:, sl] / l_scr[hh]).astype(o_ref.dtype)
                lse_ref[hh] = m_scr[hh] + jnp.log(l_scr[hh])

    return pl.pallas_call(
        body, name=name, grid=(n_heads // hpb, nq, nk),
        in_specs=[
            pl.BlockSpec((tq, wb), lambda g, i, j: (i, g)),
            pl.BlockSpec((tk, wb), lambda g, i, j: (jnp.minimum(j, i), g)),
            pl.BlockSpec((tk, wb), lambda g, i, j: (jnp.minimum(j, i), g)),
            pl.BlockSpec((hpb, 1, tk), lambda g, i, j: (g, 0, jnp.minimum(j, i))),
        ],
        out_specs=[pl.BlockSpec((tq, wb), lambda g, i, j: (i, g)),
                   pl.BlockSpec((hpb, tq, 1), lambda g, i, j: (g, i, 0))],
        out_shape=[SDS((s, n_heads * head_dim), bf16), SDS((n_heads, s, 1), f32)],
        scratch_shapes=[pltpu.VMEM((hpb, tq, 1), f32), pltpu.VMEM((hpb, tq, 1), f32), pltpu.VMEM((tq, wb), f32)],
        compiler_params=_cp(("parallel", "parallel", "arbitrary")),
    )(q, k, v, crow)


def _fox_probs(q_ref, k_ref, v_ref, cr_ref, lse_ref, do_ref, hh, head_dim, scale, diagonal):
    sc, sl = _fox_logits(q_ref, k_ref, cr_ref, hh, head_dim, scale, diagonal)
    return jnp.exp(sc - lse_ref[hh]), _dg(do_ref[:, sl], v_ref[:, sl], "nt"), sl


def _fox_bwd_dq(name, q, k, v, crow, lse, do, n_heads, head_dim):
    s = k.shape[0]
    tq = tk = min(ATT_TILE, s)
    nq, nk = s // tq, s // tk
    scale = head_dim ** -0.5
    hpb = ATT_HEADS
    wb = hpb * head_dim

    def body(q_ref, k_ref, v_ref, cr_ref, lse_ref, do_ref, dq_ref, delta_ref, dcc_ref, acc_scr):
        i, sweep, j = pl.program_id(1), pl.program_id(2), pl.program_id(3)

        @pl.when((sweep == 0) & (j == 0))
        def _():
            delta_ref[...] = jnp.zeros_like(delta_ref)
            dcc_ref[...] = jnp.zeros_like(dcc_ref)
            acc_scr[...] = jnp.zeros_like(acc_scr)

        def step(diagonal):
            @pl.when(sweep == 0)
            def _():
                for hh in range(hpb):
                    p, dp, _ = _fox_probs(q_ref, k_ref, v_ref, cr_ref, lse_ref, do_ref, hh, head_dim, scale, diagonal)
                    delta_ref[hh] += jnp.sum(p * dp, axis=-1, keepdims=True)

            @pl.when(sweep == 1)
            def _():
                for hh in range(hpb):
                    p, dp, sl = _fox_probs(q_ref, k_ref, v_ref, cr_ref, lse_ref, do_ref, hh, head_dim, scale, diagonal)
                    ds = p * (dp - delta_ref[hh])
                    dcc_ref[hh] += jnp.sum(ds, axis=-1, keepdims=True)
                    acc_scr[:, sl] += _dg(ds, k_ref[:, sl], "nn")

        _on_causal_tiles(i, j, step)

        @pl.when((sweep == 1) & (j == nk - 1))
        def _():
            dq_ref[...] = (acc_scr[...] * scale).astype(dq_ref.dtype)

    qspec = pl.BlockSpec((tq, wb), lambda g, i, w, j: (i, g))
    kspec = pl.BlockSpec((tk, wb), lambda g, i, w, j: (jnp.minimum(j, i), g))
    cspec = pl.BlockSpec((hpb, tq, 1), lambda g, i, w, j: (g, i, 0))
    return pl.pallas_call(
        body, name=name, grid=(n_heads // hpb, nq, 2, nk),
        in_specs=[qspec, kspec, kspec,
                  pl.BlockSpec((hpb, 1, tk), lambda g, i, w, j: (g, 0, jnp.minimum(j, i))), cspec, qspec],
        out_specs=[qspec, cspec, cspec],
        out_shape=[SDS((s, n_heads * head_dim), bf16), SDS((n_heads, s, 1), f32), SDS((n_heads, s, 1), f32)],
        scratch_shapes=[pltpu.VMEM((tq, wb), f32)],
        compiler_params=_cp(("parallel", "parallel", "arbitrary", "arbitrary")),
    )(q, k, v, crow, lse, do)


def _fox_bwd_dkv(name, q, k, v, crow, lse, delta, do, n_heads, head_dim):
    s = k.shape[0]
    tq = tk = min(ATT_TILE, s)
    nq, nk = s // tq, s // tk
    scale = head_dim ** -0.5
    hpb = ATT_HEADS
    wb = hpb * head_dim

    def body(q_ref, k_ref, v_ref, cr_ref, lse_ref, delta_ref, do_ref, dk_ref, dv_ref, dc_ref):
        j, i = pl.program_id(1), pl.program_id(2)

        @pl.when(i == 0)
        def _():
            dk_ref[...] = jnp.zeros_like(dk_ref)
            dv_ref[...] = jnp.zeros_like(dv_ref)
            dc_ref[...] = jnp.zeros_like(dc_ref)

        def step(diagonal):
            for hh in range(hpb):
                p, dp, sl = _fox_probs(q_ref, k_ref, v_ref, cr_ref, lse_ref, do_ref, hh, head_dim, scale, diagonal)
                ds = p * (dp - delta_ref[hh])
                dv_ref[:, sl] += _dg(p, do_ref[:, sl], "tn")
                dk_ref[:, sl] += _dg(ds, q_ref[:, sl], "tn") * scale
                dc_ref[hh] -= jnp.sum(ds, axis=0, keepdims=True)

        _on_causal_tiles(i, j, step)

    qspec = pl.BlockSpec((tq, wb), lambda g, j, i: (jnp.maximum(i, j), g))
    kspec = pl.BlockSpec((tk, wb), lambda g, j, i: (j, g))
    cspec = pl.BlockSpec((hpb, tq, 1), lambda g, j, i: (g, jnp.maximum(i, j), 0))
    rspec = pl.BlockSpec((hpb, 1, tk), lambda g, j, i: (g, 0, j))
    return pl.pallas_call(
        body, name=name, grid=(n_heads // hpb, nk, nq),
        in_specs=[qspec, kspec, kspec, rspec, cspec, cspec, qspec],
        out_specs=[kspec, kspec, rspec],
        out_shape=[SDS((s, n_heads * head_dim), f32), SDS((s, n_heads * head_dim), f32), SDS((n_heads, 1, s), f32)],
        compiler_params=_cp(("parallel", "parallel", "arbitrary")),
    )(q, k, v, crow, lse, delta, do)


def _final_loss(name, h, target, gain):
    s, d = h.shape
    tile = min(ROW_TILE, s)

    def body(h_ref, t_ref, g_ref, loss_ref, dh_ref, dg_ref):
        @pl.when(pl.program_id(0) == 0)
        def _():
            loss_ref[...] = jnp.zeros_like(loss_ref)
            dg_ref[...] = jnp.zeros_like(dg_ref)

        x, g = h_ref[...], g_ref[...]
        rstd = lax.rsqrt(jnp.mean(x * x, axis=-1, keepdims=True) + EPS)
        xhat = x * rstd
        err = xhat * g - t_ref[...]
        row = jnp.sum(err * err, axis=-1, keepdims=True) * (0.5 / d)
        loss_ref[...] += jnp.sum(row, axis=0, keepdims=True)
        dy = err * (1.0 / d)
        dg_ref[...] += jnp.sum(dy * xhat, axis=0, keepdims=True)
        dxhat = dy * g
        dh_ref[...] = rstd * (dxhat - xhat * jnp.mean(dxhat * xhat, axis=-1, keepdims=True))

    return pl.pallas_call(
        body, name=name, grid=(s // tile,),
        in_specs=[pl.BlockSpec((tile, d), lambda i: (i, 0)), pl.BlockSpec((tile, d), lambda i: (i, 0)),
                  pl.BlockSpec((1, d), lambda i: (0, 0))],
        out_specs=[pl.BlockSpec((1, LANES), lambda i: (0, 0)), pl.BlockSpec((tile, d), lambda i: (i, 0)),
                   pl.BlockSpec((1, d), lambda i: (0, 0))],
        out_shape=[SDS((1, LANES), f32), SDS((s, d), f32), SDS((1, d), f32)],
        compiler_params=_cp(("arbitrary",)),
    )(h, target, gain)


def _adamw(name, parts, w, m, v):
    r, c = w.shape
    rb = r
    for cand in (256, 128, 64, 32, 16):
        if r % cand == 0 and cand * c * 4 <= (2 << 20):
            rb = cand
            break
    n_parts = parts.shape[0]

    def body(p_ref, w_ref, m_ref, v_ref, g_out, d_out, m_out, v_out):
        g = p_ref[0].astype(f32)
        for k in range(1, n_parts):
            g = g + p_ref[k].astype(f32)
        m_new = ADAM_B1 * m_ref[...] + (1.0 - ADAM_B1) * g
        v_new = ADAM_B2 * v_ref[...] + (1.0 - ADAM_B2) * (g * g)
        m_hat = m_new / (1.0 - ADAM_B1 ** ADAM_STEP)
        v_hat = v_new / (1.0 - ADAM_B2 ** ADAM_STEP)
        g_out[...] = g
        d_out[...] = -ADAM_LR * (m_hat / (jnp.sqrt(v_hat) + ADAM_EPS) + ADAM_WD * w_ref[...])
        m_out[...] = m_new
        v_out[...] = v_new

    blk = pl.BlockSpec((rb, c), lambda i: (i, 0))
    return pl.pallas_call(
        body, name=name, grid=(r // rb,),
        in_specs=[pl.BlockSpec((n_parts, rb, c), lambda i: (0, i, 0)), blk, blk, blk],
        out_specs=[blk, blk, blk, blk],
        out_shape=[SDS((r, c), f32)] * 4,
        compiler_params=_cp(("parallel",)),
    )(parts, w, m, v)


def _position():
    x, y, c = lax.axis_index("x"), lax.axis_index("y"), lax.axis_index("c")
    return x, y, c


def _all_gather(name, shards):
    n = len(shards)

    def body(*refs):
        ins, outs = refs[:n], refs[n:2 * n]
        send_sems, recv_sems, local_sems = refs[2 * n:]
        x, y, c = _position()
        me, sibling = (x, y, c), (x, y, 1 - c)
        chips = [(1 - x, y), (x, 1 - y), (1 - x, 1 - y)]

        def slot(a, block):
            px, py, pc = block
            return outs[a].at[4 * px + 2 * py + pc]

        def copy(a, k, block, to, src=None):
            return pltpu.make_async_remote_copy(
                src_ref=slot(a, block) if src is None else src, dst_ref=slot(a, block),
                send_sem=send_sems.at[a, k], recv_sem=recv_sems.at[a, k], device_id=to, device_id_type=MESH)

        local = [pltpu.make_async_copy(ins[a], slot(a, me), local_sems.at[a]) for a in range(n)]
        for cp in local:
            cp.start()
        started = []
        for a in range(n):
            first = [copy(a, 0, me, sibling, src=ins[a])]
            first += [copy(a, 1 + j, me, (*chip, c), src=ins[a]) for j, chip in enumerate(chips)]
            for cp in first:
                cp.start()
            started += first
        for a in range(n):
            for j, chip in enumerate(chips):
                copy(a, 1 + j, (*chip, c), me).wait_recv()
                passed = copy(a, 4 + j, (*chip, c), sibling)
                passed.start()
                started.append(passed)
        for a in range(n):
            copy(a, 0, sibling, me).wait_recv()
            for j, chip in enumerate(chips):
                copy(a, 4 + j, (*chip, 1 - c), me).wait_recv()
        for cp in started:
            cp.wait_send()
        for cp in local:
            cp.wait()

    any_spec = pl.BlockSpec(memory_space=pl.ANY)
    outs = pl.pallas_call(
        body, name=name,
        in_specs=[any_spec] * n, out_specs=[any_spec] * n,
        out_shape=[SDS((N_DEV, *a.shape), a.dtype) for a in shards],
        scratch_shapes=[pltpu.SemaphoreType.DMA((n, 7)), pltpu.SemaphoreType.DMA((n, 7)), pltpu.SemaphoreType.DMA((n,))],
    )(*shards)
    return list(outs)


def _exchange(name, stacks):
    n = len(stacks)

    def body(*refs):
        ins, outs = refs[:n], refs[n:2 * n]
        send_sems, recv_sems, local_sems = refs[2 * n:]
        x, y, c = _position()
        me = 4 * x + 2 * y + c
        local = [pltpu.make_async_copy(ins[a].at[me], outs[a].at[me], local_sems.at[a]) for a in range(n)]
        for cp in local:
            cp.start()
        sends, recvs = [], []
        for a in range(n):
            for k in range(1, N_DEV):
                fx, fy, fc = (k >> 2) & 1, (k >> 1) & 1, k & 1
                px = (1 - x) if fx else x
                py = (1 - y) if fy else y
                pc = (1 - c) if fc else c
                peer = 4 * px + 2 * py + pc
                send = pltpu.make_async_remote_copy(
                    src_ref=ins[a].at[peer], dst_ref=outs[a].at[me],
                    send_sem=send_sems.at[a, k - 1], recv_sem=recv_sems.at[a, k - 1],
                    device_id=(px, py, pc), device_id_type=MESH)
                send.start()
                sends.append(send)
                recvs.append(pltpu.make_async_remote_copy(
                    src_ref=ins[a].at[peer], dst_ref=outs[a].at[peer],
                    send_sem=send_sems.at[a, k - 1], recv_sem=recv_sems.at[a, k - 1],
                    device_id=(px, py, pc), device_id_type=MESH))
        for cp in recvs:
            cp.wait_recv()
        for cp in sends:
            cp.wait_send()
        for cp in local:
            cp.wait()

    any_spec = pl.BlockSpec(memory_space=pl.ANY)
    outs = pl.pallas_call(
        body, name=name,
        in_specs=[any_spec] * n, out_specs=[any_spec] * n,
        out_shape=[SDS(a.shape, a.dtype) for a in stacks],
        scratch_shapes=[pltpu.SemaphoreType.DMA((n, 7)), pltpu.SemaphoreType.DMA((n, 7)), pltpu.SemaphoreType.DMA((n,))],
    )(*stacks)
    return list(outs)


def _rows_from_shards(g):
    n, l, r, c = g.shape
    return g.transpose(1, 0, 2, 3).reshape(l, n * r, c)


def _rows_to_shards(w):
    l, rows, c = w.shape
    return w.reshape(l, N_DEV, rows // N_DEV, c).transpose(1, 0, 2, 3)


def _pad_lanes(a, width):
    return jnp.pad(a, [(0, 0)] * (a.ndim - 1) + [(0, width - a.shape[-1])])


def _row(vec, width=None):
    vec = vec.reshape(1, -1)
    return vec if width is None else _pad_lanes(vec, width)


class _SmallPack:
    def __init__(self, shapes):
        self.shapes, self.offsets, at = shapes, {}, 0
        for name, shape in shapes.items():
            last = shape[-1]
            lead = int(math.prod(shape[:-1]))
            rows = lead * (last // LANES) if last >= LANES else lead
            self.offsets[name] = (at, rows)
            at += rows
        self.rows = -(-at // 8) * 8

    def pack(self, values):
        pieces = []
        for name, shape in self.shapes.items():
            val = values[name].astype(f32)
            if shape[-1] >= LANES:
                pieces.append(val.reshape(-1, LANES))
            else:
                pieces.append(_pad_lanes(val.reshape(-1, shape[-1]), LANES))
        used = sum(p.shape[0] for p in pieces)
        if used < self.rows:
            pieces.append(jnp.zeros((self.rows - used, LANES), f32))
        return jnp.concatenate(pieces, axis=0)

    def unpack(self, packed):
        out = {}
        for name, shape in self.shapes.items():
            at, rows = self.offsets[name]
            blk = packed[at:at + rows]
            out[name] = blk.reshape(shape) if shape[-1] >= LANES else blk[:, :shape[-1]].reshape(shape)
        return out


def kernel(x, mem, ffn1_norm, ffn1_w_gate_up, ffn1_w_down, mix_norm, ffn2_norm, ffn2_w_gate_up, ffn2_w_down, gdn_w_in, gdn_conv, gdn_A_log, gdn_dt_bias, gdn_out_norm, fox_w_in, w_out, mem_norm, mem_w_kv, kv_norm, kv_w, kv_b_f, final_norm, loss_target, m_ffn1_norm, m_ffn1_w_gate_up, m_ffn1_w_down, m_mix_norm, m_ffn2_norm, m_ffn2_w_gate_up, m_ffn2_w_down, m_gdn_w_in, m_gdn_conv, m_gdn_A_log, m_gdn_dt_bias, m_gdn_out_norm, m_fox_w_in, m_w_out, m_mem_norm, m_mem_w_kv, m_kv_norm, m_kv_w, m_kv_b_f, m_final_norm, v_ffn1_norm, v_ffn1_w_gate_up, v_ffn1_w_down, v_mix_norm, v_ffn2_norm, v_ffn2_w_gate_up, v_ffn2_w_down, v_gdn_w_in, v_gdn_conv, v_gdn_A_log, v_gdn_dt_bias, v_gdn_out_norm, v_fox_w_in, v_w_out, v_mem_norm, v_mem_w_kv, v_kv_norm, v_kv_w, v_kv_b_f, v_final_norm):
    weights = dict(ffn1_norm=ffn1_norm, ffn1_w_gate_up=ffn1_w_gate_up, ffn1_w_down=ffn1_w_down, mix_norm=mix_norm,
                   ffn2_norm=ffn2_norm, ffn2_w_gate_up=ffn2_w_gate_up, ffn2_w_down=ffn2_w_down, gdn_w_in=gdn_w_in,
                   gdn_conv=gdn_conv, gdn_A_log=gdn_A_log, gdn_dt_bias=gdn_dt_bias, gdn_out_norm=gdn_out_norm,
                   fox_w_in=fox_w_in, w_out=w_out, mem_norm=mem_norm, mem_w_kv=mem_w_kv, kv_norm=kv_norm, kv_w=kv_w,
                   kv_b_f=kv_b_f, final_norm=final_norm)
    mom_m = dict(ffn1_norm=m_ffn1_norm, ffn1_w_gate_up=m_ffn1_w_gate_up, ffn1_w_down=m_ffn1_w_down, mix_norm=m_mix_norm,
                 ffn2_norm=m_ffn2_norm, ffn2_w_gate_up=m_ffn2_w_gate_up, ffn2_w_down=m_ffn2_w_down, gdn_w_in=m_gdn_w_in,
                 gdn_conv=m_gdn_conv, gdn_A_log=m_gdn_A_log, gdn_dt_bias=m_gdn_dt_bias, gdn_out_norm=m_gdn_out_norm,
                 fox_w_in=m_fox_w_in, w_out=m_w_out, mem_norm=m_mem_norm, mem_w_kv=m_mem_w_kv, kv_norm=m_kv_norm,
                 kv_w=m_kv_w, kv_b_f=m_kv_b_f, final_norm=m_final_norm)
    mom_v = dict(ffn1_norm=v_ffn1_norm, ffn1_w_gate_up=v_ffn1_w_gate_up, ffn1_w_down=v_ffn1_w_down, mix_norm=v_mix_norm,
                 ffn2_norm=v_ffn2_norm, ffn2_w_gate_up=v_ffn2_w_gate_up, ffn2_w_down=v_ffn2_w_down, gdn_w_in=v_gdn_w_in,
                 gdn_conv=v_gdn_conv, gdn_A_log=v_gdn_A_log, gdn_dt_bias=v_gdn_dt_bias, gdn_out_norm=v_gdn_out_norm,
                 fox_w_in=v_fox_w_in, w_out=v_w_out, mem_norm=v_mem_norm, mem_w_kv=v_mem_w_kv, kv_norm=v_kv_norm,
                 kv_w=v_kv_w, kv_b_f=v_kv_b_f, final_norm=v_final_norm)
    names = list(weights)
    small_names = [n for n in names if weights[n].shape == mom_m[n].shape and n in (
        "ffn1_norm", "mix_norm", "ffn2_norm", "gdn_A_log", "gdn_dt_bias", "gdn_out_norm", "mem_norm", "kv_norm",
        "kv_b_f", "final_norm")]
    big_names = [n for n in names if n not in small_names]

    h = x[0]
    target = loss_target[0]
    mem_tokens = mem[0]
    s, d = h.shape
    depth = ffn1_norm.shape[0]
    n_a = gdn_w_in.shape[0]
    n_heads, head_dim = gdn_A_log.shape[1], gdn_out_norm.shape[1]
    gw = n_heads * head_dim
    a_in = gdn_w_in.shape[2]
    mem_w = a_in - 4 * gw - 2 * n_heads
    a_in_pad = 4 * gw + mem_w + LANES
    kv_width = kv_w.shape[1]
    kv_pad = 2 * gw + LANES
    fh = ffn1_w_down.shape[1] * N_DEV

    def permute_in(w):
        ab = w[..., 4 * gw:4 * gw + 2 * n_heads]
        return jnp.concatenate([w[..., :4 * gw], w[..., 4 * gw + 2 * n_heads:], _pad_lanes(ab, LANES)], axis=-1)

    def unpermute_in(w):
        return jnp.concatenate([w[..., :4 * gw], w[..., 4 * gw + mem_w:4 * gw + mem_w + 2 * n_heads],
                                w[..., 4 * gw:4 * gw + mem_w]], axis=-1)

    gathered = _all_gather("gather_weights", [
        ffn1_w_gate_up.astype(bf16), ffn1_w_down.astype(bf16), ffn2_w_gate_up.astype(bf16), ffn2_w_down.astype(bf16),
        permute_in(gdn_w_in).astype(bf16), fox_w_in.astype(bf16), w_out.astype(bf16), mem_w_kv.astype(bf16),
        _pad_lanes(kv_w, kv_pad).astype(bf16)[None], gdn_conv])
    wgu1, wd1_s, wgu2, wd2_s, win_s, wfox_s, wout_s, wmem_s, wkv_s, conv_s = gathered
    wd1, wd2 = _rows_from_shards(wd1_s), _rows_from_shards(wd2_s)
    win, wfox, wout = _rows_from_shards(win_s), _rows_from_shards(wfox_s), _rows_from_shards(wout_s)
    wmem = _rows_from_shards(wmem_s)
    wmem_cat = wmem.transpose(1, 0, 2).reshape(d, depth * 2 * mem_w)
    wkv = _rows_from_shards(wkv_s)[0]
    conv_w = conv_s.transpose(1, 2, 0, 3).reshape(n_a, gdn_conv.shape[1], 3 * gw)

    a_log_rows = [_row(gdn_A_log[l], LANES) for l in range(n_a)]
    dt_rows = [_row(gdn_dt_bias[l], LANES) for l in range(n_a)]
    onorm_rows = [_row(gdn_out_norm[l]) for l in range(n_a)]
    b_f_row = _row(kv_b_f, LANES)

    (mem_n,) = _rowwise("mem_norm", lambda t, g: (_rms(t, g),), [mem_tokens], [_row(mem_norm)], [(d, bf16)])
    mem_kv = _matmul("mem_kv", mem_n, wmem_cat, "nn", f32)

    saved = []
    shared = None
    for l in range(depth):
        rec = {"h0": h}
        h1 = _ffn_fwd(f"ffn1_fwd_{l}", h, _row(ffn1_norm[l]), wgu1, wd1, l)
        (u,) = _rowwise(f"mix_norm_{l}", lambda t, g: (_rms(t, g),), [h1], [_row(mix_norm[l])], [(d, bf16)])
        rec.update(h1=h1, u=u)
        if l < n_a:
            proj = _matmul(f"gdn_in_{l}", u, win[l], "nn", f32)
            yc = _conv_fwd(f"conv_fwd_{l}", V(proj, cb=0, w=3 * gw), conv_w[l])
            ab_view = V(proj, cb=(4 * gw + mem_w) // LANES, w=LANES)
            q, k, v, gb = _rowwise(f"gdn_pre_{l}", functools.partial(_gdn_pre, n_heads, head_dim),
                                   [yc, ab_view], [a_log_rows[l], dt_rows[l]],
                                   [(gw, f32), (gw, f32), (gw, f32), (LANES, f32)])
            o, states = _gdn_chunk_fwd(f"gdn_chunk_fwd_{l}", q, k, v, gb, n_heads, head_dim)
            z_view = V(proj, cb=3, w=gw)
            (main,) = _rowwise(f"gdn_post_{l}", functools.partial(_gdn_post, n_heads, head_dim),
                               [o, z_view], [onorm_rows[l]], [(gw, bf16)])
            qmem_view = V(proj, cb=4 * gw // mem_w, w=mem_w)
            rec.update(proj=proj, yc=yc, q=q, k=k, v=v, gb=gb, o=o, states=states)
        else:
            proj = _matmul(f"fox_in_{l}", u, wfox[l - n_a], "nn", bf16)
            sk, sv, crow = shared["k"], shared["v"], shared["crow"]
            main, lse = _fox_fwd(f"fox_fwd_{l}", proj, sk, sv, crow, n_heads, head_dim)
            qmem_view = V(proj, cb=gw // mem_w, w=mem_w)
            rec.update(proj=proj, lse=lse)
        km = V(mem_kv, cb=2 * l, w=mem_w)
        vm = V(mem_kv, cb=2 * l + 1, w=mem_w)
        (mem_out,) = _rowwise(f"mem_attn_{l}", _mem_attn, [qmem_view], [km, vm], [(mem_w, bf16)])
        cat = jnp.concatenate([main, mem_out], axis=1)
        h2 = _matmul(f"out_proj_{l}", cat, wout[l], "nn", f32, add=h1)
        h3 = _ffn_fwd(f"ffn2_fwd_{l}", h2, _row(ffn2_norm[l]), wgu2, wd2, l)
        rec.update(cat=cat, h2=h2, qmem=qmem_view)
        saved.append(rec)
        h = h3
        if l == n_a - 1:
            (hn,) = _rowwise("kv_norm", lambda t, g: (_rms(t, g),), [h], [_row(kv_norm)], [(d, bf16)])
            p = _matmul("kv_proj", hn, wkv, "nn", f32)
            pk, pv, pf = V(p, cb=0, w=gw), V(p, cb=1, w=gw), V(p, cb=2 * gw // LANES, w=LANES)
            sk, sv, log_f = _rowwise("kv_post", functools.partial(_kv_post, n_heads), [pk, pv, pf], [b_f_row],
                                     [(gw, bf16), (gw, bf16), (LANES, f32)])
            cum = _cumsum("forget_cumsum", [log_f], reverse=False)
            c_heads = cum[:, :n_heads].T
            shared = dict(k=sk, v=sv, crow=c_heads.reshape(n_heads, 1, s), h=h, hn=hn, p=p, views=(pk, pv, pf))

    loss_part, dh, d_final = _final_loss("final_loss", h, target, _row(final_norm))
    loss = lax.psum(loss_part[0, 0], ("x", "y", "c"))

    grads = {}
    per_layer = {n: [None] * depth for n in ("ffn1_norm", "mix_norm", "ffn2_norm", "ffn1_gu", "ffn1_d", "ffn2_gu",
                                             "ffn2_d", "w_out")}
    per_a = {n: [None] * n_a for n in ("gdn_w_in", "gdn_conv", "gdn_A_log", "gdn_dt_bias", "gdn_out_norm")}
    per_b = {"fox_w_in": [None] * (depth - n_a)}
    d_mem_kv = [None] * depth
    fox_grads = []

    def ffn_backward(tag, l, h_in, d_out, gain, wgu, wd):
        parts, dwg, dwu, dwd = _ffn_bwd(f"{tag}_bwd_{l}", h_in, d_out, _row(gain), wgu, wd, l)
        nh = parts.shape[0]
        (d_in,), (d_gain,) = _rowwise_vjp(
            f"{tag}_norm_bwd_{l}", lambda t, g: (_rms(t, g),), [h_in], [_row(gain)],
            [[V(parts, lead=t) for t in range(nh)]], [f32], add=d_out)
        return d_in, d_gain, jnp.concatenate([dwg, dwu], axis=0), dwd

    for l in reversed(range(depth)):
        rec = saved[l]
        if l == n_a - 1:
            dk_list = [V(g["dk"]) for g in fox_grads]
            dv_list = [V(g["dv"]) for g in fox_grads]
            dc_parts = [_pad_lanes(part.reshape(n_heads, s).T, LANES) for g in fox_grads for part in g["dc"]]
            d_log_f = _cumsum("forget_cumsum_bwd", dc_parts, reverse=True)
            pk, pv, pf = shared["views"]
            (dpk, dpv, dpf), (d_bf,) = _rowwise_vjp(
                "kv_post_bwd", functools.partial(_kv_post, n_heads), [pk, pv, pf], [b_f_row],
                [dk_list, dv_list, [d_log_f]], [bf16, bf16, bf16])
            dp = jnp.concatenate([dpk, dpv, dpf], axis=1)
            d_hn = _matmul("kv_proj_dx", dp, wkv, "nt", f32)
            grads["kv_w"] = _matmul("kv_proj_dw", shared["hn"], dp, "tn", f32)[:, :kv_width]
            (dh,), (d_kvn,) = _rowwise_vjp("kv_norm_bwd", lambda t, g: (_rms(t, g),), [shared["h"]], [_row(kv_norm)],
                                           [[d_hn]], [f32], add=dh)
            grads["kv_norm"] = d_kvn.reshape(-1)
            grads["kv_b_f"] = d_bf[0, :n_heads]

        dh2, per_layer["ffn2_norm"][l], per_layer["ffn2_gu"][l], per_layer["ffn2_d"][l] = ffn_backward(
            "ffn2", l, rec["h2"], dh, ffn2_norm[l], wgu2, wd2)
        d_cat = _matmul(f"out_proj_dx_{l}", dh2, wout[l], "nt", f32)
        per_layer["w_out"][l] = _matmul(f"out_proj_dw_{l}", rec["cat"], dh2, "tn", f32)
        d_main = V(d_cat, cb=0, w=gw)
        d_memo = V(d_cat, cb=gw // mem_w, w=mem_w)
        km, vm = V(mem_kv, cb=2 * l, w=mem_w), V(mem_kv, cb=2 * l + 1, w=mem_w)
        (dqmem,), (dkm, dvm) = _rowwise_vjp(f"mem_attn_bwd_{l}", _mem_attn, [rec["qmem"]], [km, vm], [[d_memo]], [bf16])
        d_mem_kv[l] = jnp.concatenate([dkm, dvm], axis=1)
        if l < n_a:
            proj = rec["proj"]
            z_view = V(proj, cb=3, w=gw)
            (d_o, d_z), (d_onorm,) = _rowwise_vjp(
                f"gdn_post_bwd_{l}", functools.partial(_gdn_post, n_heads, head_dim), [rec["o"], z_view],
                [onorm_rows[l]], [[d_main]], [f32, bf16])
            dq, dk, dv, dgb = _gdn_chunk_bwd(f"gdn_chunk_bwd_{l}", rec["q"], rec["k"], rec["v"], rec["gb"],
                                             rec["states"], d_o, n_heads, head_dim)
            ab_view = V(proj, cb=(4 * gw + mem_w) // LANES, w=LANES)
            (d_yc, d_ab), (d_alog, d_dt) = _rowwise_vjp(
                f"gdn_pre_bwd_{l}", functools.partial(_gdn_pre, n_heads, head_dim), [rec["yc"], ab_view],
                [a_log_rows[l], dt_rows[l]], [[dq], [dk], [dv], [dgb]], [f32, bf16])
            d_qkv, d_conv = _conv_bwd(f"conv_bwd_{l}", V(proj, cb=0, w=3 * gw), d_yc, conv_w[l])
            d_proj = jnp.concatenate([d_qkv, d_z, dqmem, d_ab], axis=1)
            du = _matmul(f"gdn_in_dx_{l}", d_proj, win[l], "nt", f32)
            per_a["gdn_w_in"][l] = unpermute_in(_matmul(f"gdn_in_dw_{l}", rec["u"], d_proj, "tn", f32))
            per_a["gdn_conv"][l] = d_conv
            per_a["gdn_A_log"][l] = d_alog[0, :n_heads]
            per_a["gdn_dt_bias"][l] = d_dt[0, :n_heads]
            per_a["gdn_out_norm"][l] = d_onorm[0]
        else:
            proj = rec["proj"]
            sk, sv, crow = shared["k"], shared["v"], shared["crow"]
            dq, delta, dc_col = _fox_bwd_dq(f"fox_dq_{l}", proj, sk, sv, crow, rec["lse"], d_cat, n_heads, head_dim)
            dk, dv, dc_row = _fox_bwd_dkv(f"fox_dkv_{l}", proj, sk, sv, crow, rec["lse"], delta, d_cat,
                                          n_heads, head_dim)
            fox_grads.append(dict(dk=dk, dv=dv, dc=(dc_row, dc_col)))
            d_proj = jnp.concatenate([dq, dqmem], axis=1)
            du = _matmul(f"fox_in_dx_{l}", d_proj, wfox[l - n_a], "nt", f32)
            per_b["fox_w_in"][l - n_a] = _matmul(f"fox_in_dw_{l}", rec["u"], d_proj, "tn", f32)
        (dh1,), (d_mix,) = _rowwise_vjp(f"mix_norm_bwd_{l}", lambda t, g: (_rms(t, g),), [rec["h1"]],
                                        [_row(mix_norm[l])], [[du]], [f32], add=dh2)
        per_layer["mix_norm"][l] = d_mix
        dh, per_layer["ffn1_norm"][l], per_layer["ffn1_gu"][l], per_layer["ffn1_d"][l] = ffn_backward(
            "ffn1", l, rec["h0"], dh1, ffn1_norm[l], wgu1, wd1)

    grad_x = dh[None]

    d_mem_kv_cat = jnp.concatenate(d_mem_kv, axis=1)
    d_wmem_cat = _matmul("mem_kv_dw", mem_n, d_mem_kv_cat, "tn", f32)
    d_mem_n = _matmul("mem_kv_dx", d_mem_kv_cat, wmem_cat, "nt", f32)
    _, (d_memnorm,) = _rowwise_vjp("mem_norm_bwd", lambda t, g: (_rms(t, g),), [mem_tokens], [_row(mem_norm)],
                                   [[d_mem_n]], [None])

    def gu_stack(per):
        return jnp.stack(per, axis=1).astype(bf16)

    stacks = dict(
        ffn1_w_gate_up=gu_stack(per_layer["ffn1_gu"]),
        ffn1_w_down=_rows_to_shards(jnp.stack(per_layer["ffn1_d"])).astype(bf16),
        ffn2_w_gate_up=gu_stack(per_layer["ffn2_gu"]),
        ffn2_w_down=_rows_to_shards(jnp.stack(per_layer["ffn2_d"])).astype(bf16),
        gdn_w_in=_rows_to_shards(jnp.stack(per_a["gdn_w_in"])).astype(bf16),
        gdn_conv=jnp.stack(per_a["gdn_conv"]).reshape(n_a, -1, N_DEV, 3 * gw // N_DEV).transpose(2, 0, 1, 3),
        fox_w_in=_rows_to_shards(jnp.stack(per_b["fox_w_in"])).astype(bf16),
        w_out=_rows_to_shards(jnp.stack(per_layer["w_out"])).astype(bf16),
        mem_w_kv=_rows_to_shards(d_wmem_cat.reshape(d, depth, 2 * mem_w).transpose(1, 0, 2)).astype(bf16),
        kv_w=_rows_to_shards(grads["kv_w"][None])[:, 0].astype(bf16),
    )
    received = dict(zip(big_names, _exchange("exchange_grads", [stacks[n] for n in big_names])))

    small_shapes = {n: weights[n].shape for n in small_names}
    pack = _SmallPack(small_shapes)
    small_grads = dict(
        ffn1_norm=jnp.concatenate(per_layer["ffn1_norm"], axis=0), mix_norm=jnp.concatenate(per_layer["mix_norm"], axis=0),
        ffn2_norm=jnp.concatenate(per_layer["ffn2_norm"], axis=0), gdn_A_log=jnp.stack(per_a["gdn_A_log"]),
        gdn_dt_bias=jnp.stack(per_a["gdn_dt_bias"]), gdn_out_norm=jnp.stack(per_a["gdn_out_norm"]),
        mem_norm=d_memnorm.reshape(-1), kv_norm=grads["kv_norm"], kv_b_f=grads["kv_b_f"], final_norm=d_final.reshape(-1))
    (small_parts,) = _all_gather("gather_small_grads", [pack.pack(small_grads)])

    out_g, out_d, out_m, out_v = {}, {}, {}, {}
    for n in big_names:
        shape = weights[n].shape
        c = shape[-1]
        parts = received[n].reshape(N_DEV, -1, c)
        res = _adamw(f"adamw_{n}", parts, weights[n].reshape(-1, c), mom_m[n].reshape(-1, c), mom_v[n].reshape(-1, c))
        out_g[n], out_d[n], out_m[n], out_v[n] = [r.reshape(shape) for r in res]
    res = _adamw("adamw_small", small_parts, pack.pack({n: weights[n] for n in small_names}),
                 pack.pack({n: mom_m[n] for n in small_names}), pack.pack({n: mom_v[n] for n in small_names}))
    for dst, packed in zip((out_g, out_d, out_m, out_v), res):
        dst.update(pack.unpack(packed))

    return (loss, grad_x, *[out_g[n] for n in names], *[out_d[n] for n in names],
            *[out_m[n] for n in names], *[out_v[n] for n in names])
```

```python
import functools
import math

import jax
import jax.numpy as jnp
from jax import lax
from jax.experimental import pallas as pl
from jax.experimental.pallas import tpu as pltpu

f32 = jnp.float32
bf16 = jnp.bfloat16
SDS = jax.ShapeDtypeStruct
HIGHEST = lax.Precision.HIGHEST

N_DEV = 8
MEM_HEADS = 4
CHUNK = 64
LANES = 128
EPS = 1e-6
NEG_INF = -1e30
ADAM_LR = 0.001
ADAM_B1 = 0.9
ADAM_B2 = 0.999
ADAM_EPS = 1e-08
ADAM_WD = 0.01
ADAM_STEP = 10

ROW_TILE = 256
MM_TILE = 512
FFN_FWD_TILE = 512
FFN_BWD_TILE = 256
ATT_TILE = 512
ATT_HEADS = 3
CUMSUM_TILE = 256
VMEM_LIMIT = 56 * 1024 * 1024

MESH = pl.DeviceIdType.MESH


def _cp(sem=None):
    return pltpu.CompilerParams(dimension_semantics=sem, vmem_limit_bytes=VMEM_LIMIT)


_DIMS = {"nn": (((1,), (0,)), ((), ())), "nt": (((1,), (1,)), ((), ())), "tn": (((0,), (0,)), ((), ()))}


def _dg(a, b, mode):
    return lax.dot_general(a.astype(bf16), b.astype(bf16), _DIMS[mode], preferred_element_type=f32)


@functools.partial(jax.custom_vjp, nondiff_argnums=(2,))
def _mm(a, b, mode):
    return _dg(a, b, mode)


def _mm_fwd(a, b, mode):
    return _dg(a, b, mode), (a, b)


def _mm_bwd(mode, res, ct):
    a, b = res
    if mode == "nn":
        da, db = _dg(ct, b, "nt"), _dg(a, ct, "tn")
    elif mode == "nt":
        da, db = _dg(ct, b, "nn"), _dg(ct, a, "tn")
    else:
        da, db = _dg(b, ct, "nt"), _dg(a, ct, "nn")
    return da.astype(a.dtype), db.astype(b.dtype)


_mm.defvjp(_mm_fwd, _mm_bwd)


def _dgh(a, b, mode="nn"):
    return lax.dot_general(a, b, _DIMS[mode], precision=HIGHEST, preferred_element_type=f32)


@jax.custom_vjp
def _unit_lower_inverses(lows):
    c = lows[0].shape[0]
    ri = lax.broadcasted_iota(jnp.int32, (c, c), 0)
    ci = lax.broadcasted_iota(jnp.int32, (c, c), 1)
    eye = jnp.where(ri == ci, 1.0, 0.0)
    xs = [-low for low in lows]
    rs = [eye + x for x in xs]
    for _ in range(int(math.log2(c)) - 1):
        xs = [_dgh(x, x) for x in xs]
        rs = [r + _dgh(r, x) for r, x in zip(rs, xs)]
    return tuple(rs)


def _uli_fwd(lows):
    ts = _unit_lower_inverses(lows)
    return ts, ts


def _uli_bwd(ts, cts):
    mids = [_dgh(ct, t, "nt") for t, ct in zip(ts, cts)]
    return (tuple(-_dgh(t, m, "tn") for t, m in zip(ts, mids)),)


_unit_lower_inverses.defvjp(_uli_fwd, _uli_bwd)


@functools.partial(jax.custom_vjp, nondiff_argnums=(1,))
def _split_lanes(x, width):
    return tuple(x[:, i * width:(i + 1) * width] for i in range(x.shape[1] // width))


def _split_fwd(x, width):
    return _split_lanes(x, width), None


def _split_bwd(width, _, cts):
    return (jnp.concatenate(list(cts), axis=1),)


_split_lanes.defvjp(_split_fwd, _split_bwd)


def _sigmoid(x):
    return 1.0 / (1.0 + jnp.exp(-x))


def _silu(x):
    return x * _sigmoid(x)


def _softplus(x):
    return jnp.maximum(x, 0.0) + jnp.log1p(jnp.exp(-jnp.abs(x)))


def _rms(x, gain):
    return x * lax.rsqrt(jnp.mean(x * x, axis=-1, keepdims=True) + EPS) * gain


class V:
    def __init__(self, arr, lead=None, cb=0, w=None):
        self.arr, self.lead, self.cb = arr, lead, cb
        self.w = arr.shape[-1] if w is None else w

    @property
    def rows(self):
        return self.arr.shape[-2]

    def spec(self, tile, order=None):
        lead, cb, w = self.lead, self.cb, self.w
        order = order or (lambda i: i)
        if lead is None:
            return pl.BlockSpec((tile, w), lambda i: (order(i), cb))
        return pl.BlockSpec((None, tile, w), lambda i: (lead, order(i), cb))

    def const_spec(self):
        lead, cb, w, r = self.lead, self.cb, self.w, self.rows
        if lead is None:
            return pl.BlockSpec((r, w), lambda i: (0, cb))
        return pl.BlockSpec((None, r, w), lambda i: (lead, 0, cb))


def _v(a):
    return a if isinstance(a, V) else V(a)


def _rowwise(name, fn, rows, consts, outs, tile=None):
    rows = [_v(r) for r in rows]
    consts = [_v(c) for c in consts]
    s = rows[0].rows
    tile = min(tile or ROW_TILE, s)
    nr, nc = len(rows), len(consts)

    def body(*refs):
        vals = [r[...].astype(f32) for r in refs[:nr + nc]]
        res = fn(*vals)
        for o, val in zip(refs[nr + nc:], res):
            o[...] = val.astype(o.dtype)

    return pl.pallas_call(
        body, name=name, grid=(s // tile,),
        in_specs=[r.spec(tile) for r in rows] + [c.const_spec() for c in consts],
        out_specs=[pl.BlockSpec((tile, w), lambda i: (i, 0)) for w, _ in outs],
        out_shape=[SDS((s, w), dt) for w, dt in outs],
        compiler_params=_cp(("parallel",)),
    )(*[r.arr for r in rows], *[c.arr for c in consts])


def _rowwise_vjp(name, fn, rows, consts, cts, d_rows, add=None, tile=None):
    rows = [_v(r) for r in rows]
    consts = [_v(c) for c in consts]
    cts = [[_v(c) for c in group] for group in cts]
    flat_cts = [c for group in cts for c in group]
    s = rows[0].rows
    tile = min(tile or ROW_TILE, s)
    nr, nc, nt = len(rows), len(consts), len(flat_cts)
    want = [k for k, dt in enumerate(d_rows) if dt is not None]
    has_add = add is not None
    add_v = [_v(add)] if has_add else []

    def body(*refs):
        vals = [r[...].astype(f32) for r in refs[:nr + nc]]
        ct_refs = refs[nr + nc:nr + nc + nt]
        pos = nr + nc + nt
        add_ref = refs[pos] if has_add else None
        pos += 1 if has_add else 0
        drow_refs = refs[pos:pos + len(want)]
        dconst_refs = refs[pos + len(want):]
        ctv, at = [], 0
        for group in cts:
            acc = ct_refs[at][...].astype(f32)
            for r in ct_refs[at + 1:at + len(group)]:
                acc = acc + r[...].astype(f32)
            at += len(group)
            ctv.append(acc)
        _, vjp = jax.vjp(fn, *vals)
        grads = vjp(tuple(ctv))
        for o, k in zip(drow_refs, want):
            g = grads[k]
            if has_add and k == want[0]:
                g = g + add_ref[...].astype(f32)
            o[...] = g.astype(o.dtype)

        @pl.when(pl.program_id(0) == 0)
        def _():
            for o in dconst_refs:
                o[...] = jnp.zeros_like(o)

        for o, g in zip(dconst_refs, grads[nr:]):
            o[...] += g

    outs = pl.pallas_call(
        body, name=name, grid=(s // tile,),
        in_specs=[r.spec(tile) for r in rows] + [c.const_spec() for c in consts]
        + [c.spec(tile) for c in flat_cts] + [a.spec(tile) for a in add_v],
        out_specs=[pl.BlockSpec((tile, rows[k].w), lambda i: (i, 0)) for k in want]
        + [pl.BlockSpec((c.rows, c.w), lambda i: (0, 0)) for c in consts],
        out_shape=[SDS((s, rows[k].w), d_rows[k]) for k in want] + [SDS((c.rows, c.w), f32) for c in consts],
        compiler_params=_cp(("arbitrary",)),
    )(*[r.arr for r in rows], *[c.arr for c in consts], *[c.arr for c in flat_cts], *[a.arr for a in add_v])
    return list(outs[:len(want)]), list(outs[len(want):])


def _pick(n, cap):
    best = None
    for t in range(LANES, min(n, cap) + 1, LANES):
        if n % t == 0:
            best = t
    return best or n


def _matmul(name, a, b, mode, out_dtype, add=None, tn_cap=1280):
    has_add = add is not None
    if mode in ("nn", "nt"):
        m, k = a.shape
        n = b.shape[1] if mode == "nn" else b.shape[0]
        tm = min(MM_TILE, m)
        tn = _pick(n, tn_cap) if k * n * 2 > (8 << 20) else n

        def body(*refs):
            a_ref, b_ref = refs[0], refs[1]
            o_ref = refs[-1]
            acc = _dg(a_ref[...], b_ref[...], mode)
            if has_add:
                acc = acc + refs[2][...].astype(f32)
            o_ref[...] = acc.astype(o_ref.dtype)

        b_spec = pl.BlockSpec((k, tn), lambda i, j: (0, j)) if mode == "nn" else pl.BlockSpec((tn, k), lambda i, j: (j, 0))
        in_specs = [pl.BlockSpec((tm, k), lambda i, j: (i, 0)), b_spec]
        args = [a, b]
        if has_add:
            in_specs.append(pl.BlockSpec((tm, tn), lambda i, j: (i, j)))
            args.append(add)
        return pl.pallas_call(
            body, name=name, grid=(m // tm, n // tn), in_specs=in_specs,
            out_specs=pl.BlockSpec((tm, tn), lambda i, j: (i, j)),
            out_shape=SDS((m, n), out_dtype), compiler_params=_cp(("parallel", "parallel")),
        )(*args)
    kk, m = a.shape
    n = b.shape[1]
    tk = min(MM_TILE, kk)
    tn = _pick(n, tn_cap) if m * n * 4 > (7 << 20) else n

    def body_tn(a_ref, b_ref, o_ref):
        @pl.when(pl.program_id(1) == 0)
        def _():
            o_ref[...] = jnp.zeros_like(o_ref)

        o_ref[...] += _dg(a_ref[...], b_ref[...], "tn")

    return pl.pallas_call(
        body_tn, name=name, grid=(n // tn, kk // tk),
        in_specs=[pl.BlockSpec((tk, m), lambda j, k: (k, 0)), pl.BlockSpec((tk, tn), lambda j, k: (k, j))],
        out_specs=pl.BlockSpec((m, tn), lambda j, k: (0, j)),
        out_shape=SDS((m, n), f32), compiler_params=_cp(("parallel", "arbitrary")),
    )(a, b)


def _ffn_fwd(name, h, gain, wgu, wd, layer):
    s, d = h.shape
    hs = wgu.shape[3]
    nh = wgu.shape[0] // 2
    tm = min(FFN_FWD_TILE, s)

    def body(h_ref, g_ref, wg_ref, wu_ref, wd_ref, o_ref, gate_ref, up_ref, n_scr, acc_scr):
        t = pl.program_id(1)

        @pl.when(t == 0)
        def _():
            n_scr[...] = _rms(h_ref[...], g_ref[...]).astype(bf16)
            acc_scr[...] = jnp.zeros_like(acc_scr)

        n = n_scr[...]
        gate = _dg(n, wg_ref[...], "nn")
        up = _dg(n, wu_ref[...], "nn")
        gate_ref[...] = gate.astype(gate_ref.dtype)
        up_ref[...] = up.astype(up_ref.dtype)
        acc_scr[...] += _dg(_silu(gate) * up, wd_ref[...], "nn")

        @pl.when(t == nh - 1)
        def _():
            o_ref[...] = h_ref[...] + 0.5 * acc_scr[...]

    saved_spec = pl.BlockSpec((None, tm, hs), lambda i, t: (t, i, 0))
    return pl.pallas_call(
        body, name=name, grid=(s // tm, nh),
        in_specs=[
            pl.BlockSpec((tm, d), lambda i, t: (i, 0)),
            pl.BlockSpec((1, d), lambda i, t: (0, 0)),
            pl.BlockSpec((None, None, d, hs), lambda i, t: (t, layer, 0, 0)),
            pl.BlockSpec((None, None, d, hs), lambda i, t: (t + nh, layer, 0, 0)),
            pl.BlockSpec((None, hs, d), lambda i, t: (layer, t, 0)),
        ],
        out_specs=[pl.BlockSpec((tm, d), lambda i, t: (i, 0)), saved_spec, saved_spec],
        out_shape=[SDS((s, d), f32), SDS((nh, s, hs), bf16), SDS((nh, s, hs), bf16)],
        scratch_shapes=[pltpu.VMEM((tm, d), bf16), pltpu.VMEM((tm, d), f32)],
        compiler_params=_cp(("parallel", "arbitrary")),
    )(h, gain, wgu, wgu, wd)


def _ffn_bwd(name, h, dout, gain, gate_s, up_s, wgu, wd, layer):
    s, d = h.shape
    hs = wgu.shape[3]
    nh = wgu.shape[0] // 2
    tm = min(FFN_BWD_TILE, s)

    def body(h_ref, do_ref, g_ref, gate_ref, up_ref, wg_ref, wu_ref, wd_ref, dn_ref, dwg_ref, dwu_ref, dwd_ref):
        @pl.when(pl.program_id(1) == 0)
        def _():
            dwg_ref[...] = jnp.zeros_like(dwg_ref)
            dwu_ref[...] = jnp.zeros_like(dwu_ref)
            dwd_ref[...] = jnp.zeros_like(dwd_ref)

        n = _rms(h_ref[...], g_ref[...]).astype(bf16)
        wg, wu, wdn = wg_ref[...], wu_ref[...], wd_ref[...]
        gate = gate_ref[...].astype(f32)
        up = up_ref[...].astype(f32)
        sg = _sigmoid(gate)
        act = gate * sg
        dy = (0.5 * do_ref[...]).astype(bf16)
        da = _dg(dy, wdn, "nt")
        dup = (da * act).astype(bf16)
        dgate = (da * up * (sg * (1.0 + gate * (1.0 - sg)))).astype(bf16)
        dwd_ref[...] += _dg(act * up, dy, "tn")
        dwg_ref[...] += _dg(n, dgate, "tn")
        dwu_ref[...] += _dg(n, dup, "tn")
        dn_ref[...] = _dg(dgate, wg, "nt") + _dg(dup, wu, "nt")

    return pl.pallas_call(
        body, name=name, grid=(nh, s // tm),
        in_specs=[
            pl.BlockSpec((tm, d), lambda t, i: (i, 0)),
            pl.BlockSpec((tm, d), lambda t, i: (i, 0)),
            pl.BlockSpec((1, d), lambda t, i: (0, 0)),
            pl.BlockSpec((None, tm, hs), lambda t, i: (t, i, 0)),
            pl.BlockSpec((None, tm, hs), lambda t, i: (t, i, 0)),
            pl.BlockSpec((None, None, d, hs), lambda t, i: (t, layer, 0, 0)),
            pl.BlockSpec((None, None, d, hs), lambda t, i: (t + nh, layer, 0, 0)),
            pl.BlockSpec((None, hs, d), lambda t, i: (layer, t, 0)),
        ],
        out_specs=[
            pl.BlockSpec((None, tm, d), lambda t, i: (t, i, 0)),
            pl.BlockSpec((None, d, hs), lambda t, i: (t, 0, 0)),
            pl.BlockSpec((None, d, hs), lambda t, i: (t, 0, 0)),
            pl.BlockSpec((hs, d), lambda t, i: (t, 0)),
        ],
        out_shape=[SDS((nh, s, d), f32), SDS((nh, d, hs), f32), SDS((nh, d, hs), f32), SDS((nh * hs, d), f32)],
        compiler_params=_cp(("parallel", "arbitrary")),
    )(h, dout, gain, gate_s, up_s, wgu, wgu, wd)


def _conv_fwd(name, x, w):
    x = _v(x)
    s, c = x.rows, x.w
    cw = w.shape[0]
    tile = min(ROW_TILE, s)
    cb = x.cb

    def body(x_ref, halo_ref, w_ref, o_ref, buf):
        first = pl.program_id(0) == 0
        buf[0:8, :] = jnp.where(first, 0.0, halo_ref[...])
        buf[8:8 + tile, :] = x_ref[...]
        acc = w_ref[0:1, :] * buf[pl.ds(8 - cw + 1, tile), :]
        for j in range(1, cw):
            acc = acc + w_ref[j:j + 1, :] * buf[pl.ds(8 - cw + 1 + j, tile), :]
        o_ref[...] = acc

    return pl.pallas_call(
        body, name=name, grid=(s // tile,),
        in_specs=[
            pl.BlockSpec((tile, c), lambda i: (i, cb)),
            pl.BlockSpec((8, c), lambda i: (jnp.maximum(i * (tile // 8) - 1, 0), cb)),
            pl.BlockSpec((cw, c), lambda i: (0, 0)),
        ],
        out_specs=pl.BlockSpec((tile, c), lambda i: (i, 0)),
        out_shape=SDS((s, c), f32),
        scratch_shapes=[pltpu.VMEM((tile + 8, c), f32)],
        compiler_params=_cp(("parallel",)),
    )(x.arr, x.arr, w)


def _conv_bwd(name, x, dy, w):
    x = _v(x)
    s, c = x.rows, x.w
    cw = w.shape[0]
    tile = min(ROW_TILE, s)
    n_tiles = s // tile
    cb = x.cb

    def body(x_ref, xh_ref, dy_ref, dyh_ref, w_ref, dx_ref, dw_ref, xbuf, dbuf):
        i = pl.program_id(0)
        xbuf[0:8, :] = jnp.where(i == 0, 0.0, xh_ref[...])
        xbuf[8:8 + tile, :] = x_ref[...]
        dyv = dy_ref[...]
        dbuf[0:tile, :] = dyv
        dbuf[tile:tile + 8, :] = jnp.where(i == n_tiles - 1, 0.0, dyh_ref[...])

        @pl.when(i == 0)
        def _():
            dw_ref[...] = jnp.zeros_like(dw_ref)

        acc = w_ref[0:1, :] * dbuf[pl.ds(cw - 1, tile), :]
        for j in range(1, cw):
            acc = acc + w_ref[j:j + 1, :] * dbuf[pl.ds(cw - 1 - j, tile), :]
        dx_ref[...] = acc.astype(dx_ref.dtype)
        for j in range(cw):
            dw_ref[j:j + 1, :] += jnp.sum(xbuf[pl.ds(8 - cw + 1 + j, tile), :] * dyv, axis=0, keepdims=True)

    return pl.pallas_call(
        body, name=name, grid=(n_tiles,),
        in_specs=[
            pl.BlockSpec((tile, c), lambda i: (i, cb)),
            pl.BlockSpec((8, c), lambda i: (jnp.maximum(i * (tile // 8) - 1, 0), cb)),
            pl.BlockSpec((tile, c), lambda i: (i, 0)),
            pl.BlockSpec((8, c), lambda i: (jnp.minimum((i + 1) * (tile // 8), s // 8 - 1), 0)),
            pl.BlockSpec((cw, c), lambda i: (0, 0)),
        ],
        out_specs=[pl.BlockSpec((tile, c), lambda i: (i, 0)), pl.BlockSpec((cw, c), lambda i: (0, 0))],
        out_shape=[SDS((s, c), bf16), SDS((cw, c), f32)],
        scratch_shapes=[pltpu.VMEM((tile + 8, c), f32), pltpu.VMEM((tile + 8, c), f32)],
        compiler_params=_cp(("arbitrary",)),
    )(x.arr, x.arr, dy, dy, w)


def _gdn_pre(n_heads, head_dim, yc, ab, a_log, dt_bias):
    gw = n_heads * head_dim
    act = _silu(yc)
    parts = _split_lanes(act, head_dim)
    qs = [p * lax.rsqrt(jnp.sum(p * p, axis=-1, keepdims=True) + EPS) * (head_dim ** -0.5) for p in parts[:n_heads]]
    ks = [p * lax.rsqrt(jnp.sum(p * p, axis=-1, keepdims=True) + EPS) for p in parts[n_heads:2 * n_heads]]
    lane = lax.broadcasted_iota(jnp.int32, ab.shape, 1)
    g = -jnp.exp(a_log) * _softplus(ab + dt_bias)
    gb = jnp.where(lane < n_heads, g, jnp.where(lane < 2 * n_heads, _sigmoid(ab), 0.0))
    del gw
    return (jnp.concatenate(qs, axis=1), jnp.concatenate(ks, axis=1),
            jnp.concatenate(list(parts[2 * n_heads:]), axis=1), gb)


def _gdn_post(n_heads, head_dim, o, z, out_norm):
    parts = _split_lanes(o, head_dim)
    normed = jnp.concatenate([_rms(p, out_norm) for p in parts], axis=1)
    return (normed * _silu(z),)


def _gdn_chunk(n_heads, head_dim, q, k, v, gb, *states):
    c = q.shape[0]
    ri = lax.broadcasted_iota(jnp.int32, (c, c), 0)
    ci = lax.broadcasted_iota(jnp.int32, (c, c), 1)
    incl, strict, diag = ri >= ci, ri > ci, ri == ci
    lane = lax.broadcasted_iota(jnp.int32, gb.shape, 1)
    qs, ks, vs = _split_lanes(q, head_dim), _split_lanes(k, head_dim), _split_lanes(v, head_dim)
    heads = range(n_heads)
    g = [jnp.sum(jnp.where(lane == h, gb, 0.0), axis=1, keepdims=True) for h in heads]
    beta = [jnp.sum(jnp.where(lane == n_heads + h, gb, 0.0), axis=1, keepdims=True) for h in heads]
    g_row = [jnp.sum(jnp.where(diag, g[h], 0.0), axis=0, keepdims=True) for h in heads]
    cg_col = [jnp.sum(jnp.where(incl, g_row[h], 0.0), axis=1, keepdims=True) for h in heads]
    cg_row = [jnp.sum(jnp.where(ri <= ci, g[h], 0.0), axis=0, keepdims=True) for h in heads]
    g_last = [jnp.sum(g[h], axis=0, keepdims=True) for h in heads]
    decay = [jnp.where(incl, jnp.exp(jnp.where(incl, cg_col[h] - cg_row[h], 0.0)), 0.0) for h in heads]
    kb = [ks[h] * beta[h] for h in heads]
    lower = [jnp.where(strict, _mm(kb[h], ks[h], "nt") * decay[h], 0.0) for h in heads]
    eye = jnp.where(diag, 1.0, 0.0)
    off_diag = [t - eye for t in _unit_lower_inverses(tuple(lower))]
    e_col = [jnp.exp(cg_col[h]) for h in heads]
    vb = [vs[h] * beta[h] for h in heads]
    kbg = [kb[h] * e_col[h] for h in heads]
    u = [vb[h] + _mm(off_diag[h], vb[h], "nn") for h in heads]
    w = [kbg[h] + _mm(off_diag[h], kbg[h], "nn") for h in heads]
    qk = [jnp.where(incl, _mm(qs[h], ks[h], "nt") * decay[h], 0.0) for h in heads]
    v_new = [u[h] - _mm(w[h], states[h], "nn") for h in heads]
    inter = [_mm(qs[h] * e_col[h], states[h], "nn") for h in heads]
    outs = [inter[h] + _mm(qk[h], v_new[h], "nn") for h in heads]
    k_tail = [ks[h] * jnp.exp(g_last[h] - cg_col[h]) for h in heads]
    new_states = [states[h] * jnp.exp(g_last[h]) + _mm(k_tail[h], v_new[h], "tn") for h in heads]
    return (jnp.concatenate(outs, axis=1), *new_states)


def _gdn_chunk_fwd(name, q, k, v, gb, n_heads, head_dim):
    s, gw = q.shape
    n = s // CHUNK
    fn = functools.partial(_gdn_chunk, n_heads, head_dim)

    def body(q_ref, k_ref, v_ref, gb_ref, o_ref, st_ref, st_scr):
        @pl.when(pl.program_id(0) == 0)
        def _():
            st_scr[...] = jnp.zeros_like(st_scr)

        st_ref[...] = st_scr[...]
        res = fn(q_ref[...], k_ref[...], v_ref[...], gb_ref[...], *[st_scr[h] for h in range(n_heads)])
        o_ref[...] = res[0]
        for h in range(n_heads):
            st_scr[h] = res[1 + h]

    row = lambda w: pl.BlockSpec((CHUNK, w), lambda i: (i, 0))
    return pl.pallas_call(
        body, name=name, grid=(n,),
        in_specs=[row(gw), row(gw), row(gw), row(LANES)],
        out_specs=[row(gw), pl.BlockSpec((None, n_heads, head_dim, head_dim), lambda i: (i, 0, 0, 0))],
        out_shape=[SDS((s, gw), f32), SDS((n, n_heads, head_dim, head_dim), f32)],
        scratch_shapes=[pltpu.VMEM((n_heads, head_dim, head_dim), f32)],
        compiler_params=_cp(("arbitrary",)),
    )(q, k, v, gb)


def _gdn_chunk_bwd(name, q, k, v, gb, states, d_out, n_heads, head_dim):
    s, gw = q.shape
    n = s // CHUNK
    fn = functools.partial(_gdn_chunk, n_heads, head_dim)

    def body(q_ref, k_ref, v_ref, gb_ref, st_ref, do_ref, dq_ref, dk_ref, dv_ref, dgb_ref, dst_scr):
        @pl.when(pl.program_id(0) == 0)
        def _():
            dst_scr[...] = jnp.zeros_like(dst_scr)

        _, vjp = jax.vjp(fn, q_ref[...], k_ref[...], v_ref[...], gb_ref[...], *[st_ref[h] for h in range(n_heads)])
        grads = vjp((do_ref[...].astype(f32), *[dst_scr[h] for h in range(n_heads)]))
        dq_ref[...] = grads[0]
        dk_ref[...] = grads[1]
        dv_ref[...] = grads[2]
        dgb_ref[...] = grads[3]
        for h in range(n_heads):
            dst_scr[h] = grads[4 + h]

    row = lambda w: pl.BlockSpec((CHUNK, w), lambda i: (n - 1 - i, 0))
    return pl.pallas_call(
        body, name=name, grid=(n,),
        in_specs=[row(gw), row(gw), row(gw), row(LANES),
                  pl.BlockSpec((None, n_heads, head_dim, head_dim), lambda i: (n - 1 - i, 0, 0, 0)), row(gw)],
        out_specs=[row(gw), row(gw), row(gw), row(LANES)],
        out_shape=[SDS((s, gw), f32), SDS((s, gw), f32), SDS((s, gw), f32), SDS((s, LANES), f32)],
        scratch_shapes=[pltpu.VMEM((n_heads, head_dim, head_dim), f32)],
        compiler_params=_cp(("arbitrary",)),
    )(q, k, v, gb, states, d_out)


def _mem_attn(qm, km, vm):
    width = qm.shape[1]
    hd = width // MEM_HEADS
    lane = lax.broadcasted_iota(jnp.int32, (1, width), 1)
    out = jnp.zeros_like(qm)
    for h in range(MEM_HEADS):
        msk = jnp.where((lane >= h * hd) & (lane < (h + 1) * hd), 1.0, 0.0)
        logits = _mm(qm * msk, km, "nt") * (hd ** -0.5)
        p = jnp.exp(logits - jnp.max(logits, axis=-1, keepdims=True))
        p = p / jnp.sum(p, axis=-1, keepdims=True)
        out = out + _mm(p, vm, "nn") * msk
    return (out,)


def _kv_post(n_heads, pk, pv, pf, b_f):
    lane = lax.broadcasted_iota(jnp.int32, pf.shape, 1)
    log_f = jnp.where(lane < n_heads, -_softplus(-(pf + b_f)), 0.0)
    return pk, pv, log_f


def _cumsum(name, xs, reverse):
    s, w = xs[0].shape
    tile = min(CUMSUM_TILE, s)
    n = s // tile

    def body(*refs):
        x_refs, o_ref, carry = refs[:-2], refs[-2], refs[-1]

        @pl.when(pl.program_id(0) == 0)
        def _():
            carry[...] = jnp.zeros_like(carry)

        xv = x_refs[0][...]
        for r in x_refs[1:]:
            xv = xv + r[...]
        ri = lax.broadcasted_iota(jnp.int32, (tile, tile), 0)
        ci = lax.broadcasted_iota(jnp.int32, (tile, tile), 1)
        tri = jnp.where((ri <= ci) if reverse else (ri >= ci), 1.0, 0.0).astype(bf16)
        x1 = xv.astype(bf16)
        r1 = xv - x1.astype(f32)
        x2 = r1.astype(bf16)
        x3 = (r1 - x2.astype(f32)).astype(bf16)
        acc = carry[...] + _dg(tri, x1, "nn") + _dg(tri, x2, "nn") + _dg(tri, x3, "nn")
        o_ref[...] = acc
        carry[...] += jnp.sum(xv, axis=0, keepdims=True)

    order = (lambda i: (n - 1 - i, 0)) if reverse else (lambda i: (i, 0))
    return pl.pallas_call(
        body, name=name, grid=(n,),
        in_specs=[pl.BlockSpec((tile, w), order)] * len(xs), out_specs=pl.BlockSpec((tile, w), order),
        out_shape=SDS((s, w), f32), scratch_shapes=[pltpu.VMEM((1, w), f32)],
        compiler_params=_cp(("arbitrary",)),
    )(*xs)


def _fox_logits(q_ref, k_ref, cr_ref, hh, head_dim, scale, diagonal):
    sl = slice(hh * head_dim, (hh + 1) * head_dim)
    s = _dg(q_ref[:, sl], k_ref[:, sl], "nt") * scale - cr_ref[hh]
    if diagonal:
        tq, tk = s.shape
        ok = lax.broadcasted_iota(jnp.int32, (tq, tk), 1) <= lax.broadcasted_iota(jnp.int32, (tq, tk), 0)
        s = jnp.where(ok, s, NEG_INF)
    return s, sl


def _on_causal_tiles(i, j, fn):
    @pl.when(j < i)
    def _():
        fn(False)

    @pl.when(j == i)
    def _():
        fn(True)


def _fox_fwd(name, q, k, v, crow, n_heads, head_dim):
    s = k.shape[0]
    tq = tk = min(ATT_TILE, s)
    nq, nk = s // tq, s // tk
    scale = head_dim ** -0.5
    hpb = ATT_HEADS
    wb = hpb * head_dim

    def body(q_ref, k_ref, v_ref, cr_ref, o_ref, lse_ref, m_scr, l_scr, acc_scr):
        i, j = pl.program_id(1), pl.program_id(2)

        @pl.when(j == 0)
        def _():
            m_scr[...] = jnp.full_like(m_scr, NEG_INF)
            l_scr[...] = jnp.zeros_like(l_scr)
            acc_scr[...] = jnp.zeros_like(acc_scr)

        def step(diagonal):
            heads = range(hpb)
            sl = [slice(hh * head_dim, (hh + 1) * head_dim) for hh in heads]
            sc = [_fox_logits(q_ref, k_ref, cr_ref, hh, head_dim, scale, diagonal)[0] for hh in heads]
            m_old = [m_scr[hh] for hh in heads]
            m_new = [jnp.maximum(m_old[hh], jnp.max(sc[hh], axis=-1, keepdims=True)) for hh in heads]
            p = [jnp.exp(sc[hh] - m_new[hh]) for hh in heads]
            alpha = [jnp.exp(m_old[hh] - m_new[hh]) for hh in heads]
            pv = [_dg(p[hh], v_ref[:, sl[hh]], "nn") for hh in heads]
            for hh in heads:
                l_scr[hh] = alpha[hh] * l_scr[hh] + jnp.sum(p[hh], axis=-1, keepdims=True)
                acc_scr[:, sl[hh]] = alpha[hh] * acc_scr[:, sl[hh]] + pv[hh]
                m_scr[hh] = m_new[hh]

        _on_causal_tiles(i, j, step)

        @pl.when(j == nk - 1)
        def _():
            for hh in range(hpb):
                sl = slice(hh * head_dim, (hh + 1) * head_dim)
                o_ref[:, sl] = (acc_scr[:, sl] / l_scr[hh]).astype(o_ref.dtype)
                lse_ref[hh] = m_scr[hh] + jnp.log(l_scr[hh])

    return pl.pallas_call(
        body, name=name, grid=(n_heads // hpb, nq, nk),
        in_specs=[
            pl.BlockSpec((tq, wb), lambda g, i, j: (i, g)),
            pl.BlockSpec((tk, wb), lambda g, i, j: (jnp.minimum(j, i), g)),
            pl.BlockSpec((tk, wb), lambda g, i, j: (jnp.minimum(j, i), g)),
            pl.BlockSpec((hpb, 1, tk), lambda g, i, j: (g, 0, jnp.minimum(j, i))),
        ],
        out_specs=[pl.BlockSpec((tq, wb), lambda g, i, j: (i, g)),
                   pl.BlockSpec((hpb, tq, 1), lambda g, i, j: (g, i, 0))],
        out_shape=[SDS((s, n_heads * head_dim), bf16), SDS((n_heads, s, 1), f32)],
        scratch_shapes=[pltpu.VMEM((hpb, tq, 1), f32), pltpu.VMEM((hpb, tq, 1), f32), pltpu.VMEM((tq, wb), f32)],
        compiler_params=_cp(("parallel", "parallel", "arbitrary")),
    )(q, k, v, crow)


def _fox_probs(q_ref, k_ref, v_ref, cr_ref, lse_ref, do_ref, hh, head_dim, scale, diagonal):
    sc, sl = _fox_logits(q_ref, k_ref, cr_ref, hh, head_dim, scale, diagonal)
    return jnp.exp(sc - lse_ref[hh]), _dg(do_ref[:, sl], v_ref[:, sl], "nt"), sl


def _fox_bwd_dq(name, q, k, v, crow, lse, do, n_heads, head_dim):
    s = k.shape[0]
    tq = tk = min(ATT_TILE, s)
    nq, nk = s // tq, s // tk
    scale = head_dim ** -0.5
    hpb = ATT_HEADS
    wb = hpb * head_dim

    def body(q_ref, k_ref, v_ref, cr_ref, lse_ref, do_ref, dq_ref, delta_ref, dcc_ref, acc_scr):
        i, sweep, j = pl.program_id(1), pl.program_id(2), pl.program_id(3)

        @pl.when((sweep == 0) & (j == 0))
        def _():
            delta_ref[...] = jnp.zeros_like(delta_ref)
            dcc_ref[...] = jnp.zeros_like(dcc_ref)
            acc_scr[...] = jnp.zeros_like(acc_scr)

        def step(diagonal):
            heads = range(hpb)

            @pl.when(sweep == 0)
            def _():
                pd = [_fox_probs(q_ref, k_ref, v_ref, cr_ref, lse_ref, do_ref, hh, head_dim, scale, diagonal) for hh in heads]
                for hh in heads:
                    delta_ref[hh] += jnp.sum(pd[hh][0] * pd[hh][1], axis=-1, keepdims=True)

            @pl.when(sweep == 1)
            def _():
                pd = [_fox_probs(q_ref, k_ref, v_ref, cr_ref, lse_ref, do_ref, hh, head_dim, scale, diagonal) for hh in heads]
                ds = [pd[hh][0] * (pd[hh][1] - delta_ref[hh]) for hh in heads]
                dqs = [_dg(ds[hh], k_ref[:, pd[hh][2]], "nn") for hh in heads]
                for hh in heads:
                    dcc_ref[hh] += jnp.sum(ds[hh], axis=-1, keepdims=True)
                    acc_scr[:, pd[hh][2]] += dqs[hh]

        _on_causal_tiles(i, j, step)

        @pl.when((sweep == 1) & (j == nk - 1))
        def _():
            dq_ref[...] = (acc_scr[...] * scale).astype(dq_ref.dtype)

    qspec = pl.BlockSpec((tq, wb), lambda g, i, w, j: (i, g))
    kspec = pl.BlockSpec((tk, wb), lambda g, i, w, j: (jnp.minimum(j, i), g))
    cspec = pl.BlockSpec((hpb, tq, 1), lambda g, i, w, j: (g, i, 0))
    return pl.pallas_call(
        body, name=name, grid=(n_heads // hpb, nq, 2, nk),
        in_specs=[qspec, kspec, kspec,
                  pl.BlockSpec((hpb, 1, tk), lambda g, i, w, j: (g, 0, jnp.minimum(j, i))), cspec, qspec],
        out_specs=[qspec, cspec, cspec],
        out_shape=[SDS((s, n_heads * head_dim), bf16), SDS((n_heads, s, 1), f32), SDS((n_heads, s, 1), f32)],
        scratch_shapes=[pltpu.VMEM((tq, wb), f32)],
        compiler_params=_cp(("parallel", "parallel", "arbitrary", "arbitrary")),
    )(q, k, v, crow, lse, do)


def _fox_bwd_dkv(name, q, k, v, crow, lse, delta, do, n_heads, head_dim):
    s = k.shape[0]
    tq = tk = min(ATT_TILE, s)
    nq, nk = s // tq, s // tk
    scale = head_dim ** -0.5
    hpb = ATT_HEADS
    wb = hpb * head_dim

    def body(q_ref, k_ref, v_ref, cr_ref, lse_ref, delta_ref, do_ref, dk_ref, dv_ref, dc_ref):
        j, i = pl.program_id(1), pl.program_id(2)

        @pl.when(i == 0)
        def _():
            dk_ref[...] = jnp.zeros_like(dk_ref)
            dv_ref[...] = jnp.zeros_like(dv_ref)
            dc_ref[...] = jnp.zeros_like(dc_ref)

        def step(diagonal):
            heads = range(hpb)
            pd = [_fox_probs(q_ref, k_ref, v_ref, cr_ref, lse_ref, do_ref, hh, head_dim, scale, diagonal) for hh in heads]
            ds = [pd[hh][0] * (pd[hh][1] - delta_ref[hh]) for hh in heads]
            dvs = [_dg(pd[hh][0], do_ref[:, pd[hh][2]], "tn") for hh in heads]
            dks = [_dg(ds[hh], q_ref[:, pd[hh][2]], "tn") for hh in heads]
            for hh in heads:
                sl = pd[hh][2]
                dv_ref[:, sl] += dvs[hh]
                dk_ref[:, sl] += dks[hh] * scale
                dc_ref[hh] -= jnp.sum(ds[hh], axis=0, keepdims=True)

        _on_causal_tiles(i, j, step)

    qspec = pl.BlockSpec((tq, wb), lambda g, j, i: (jnp.maximum(i, j), g))
    kspec = pl.BlockSpec((tk, wb), lambda g, j, i: (j, g))
    cspec = pl.BlockSpec((hpb, tq, 1), lambda g, j, i: (g, jnp.maximum(i, j), 0))
    rspec = pl.BlockSpec((hpb, 1, tk), lambda g, j, i: (g, 0, j))
    return pl.pallas_call(
        body, name=name, grid=(n_heads // hpb, nk, nq),
        in_specs=[qspec, kspec, kspec, rspec, cspec, cspec, qspec],
        out_specs=[kspec, kspec, rspec],
        out_shape=[SDS((s, n_heads * head_dim), f32), SDS((s, n_heads * head_dim), f32), SDS((n_heads, 1, s), f32)],
        compiler_params=_cp(("parallel", "parallel", "arbitrary")),
    )(q, k, v, crow, lse, delta, do)


def _final_loss(name, h, target, gain):
    s, d = h.shape
    tile = min(ROW_TILE, s)

    def body(h_ref, t_ref, g_ref, loss_ref, dh_ref, dg_ref):
        @pl.when(pl.program_id(0) == 0)
        def _():
            loss_ref[...] = jnp.zeros_like(loss_ref)
            dg_ref[...] = jnp.zeros_like(dg_ref)

        x, g = h_ref[...], g_ref[...]
        rstd = lax.rsqrt(jnp.mean(x * x, axis=-1, keepdims=True) + EPS)
        xhat = x * rstd
        err = xhat * g - t_ref[...]
        row = jnp.sum(err * err, axis=-1, keepdims=True) * (0.5 / d)
        loss_ref[...] += jnp.sum(row, axis=0, keepdims=True)
        dy = err * (1.0 / d)
        dg_ref[...] += jnp.sum(dy * xhat, axis=0, keepdims=True)
        dxhat = dy * g
        dh_ref[...] = rstd * (dxhat - xhat * jnp.mean(dxhat * xhat, axis=-1, keepdims=True))

    return pl.pallas_call(
        body, name=name, grid=(s // tile,),
        in_specs=[pl.BlockSpec((tile, d), lambda i: (i, 0)), pl.BlockSpec((tile, d), lambda i: (i, 0)),
                  pl.BlockSpec((1, d), lambda i: (0, 0))],
        out_specs=[pl.BlockSpec((1, LANES), lambda i: (0, 0)), pl.BlockSpec((tile, d), lambda i: (i, 0)),
                   pl.BlockSpec((1, d), lambda i: (0, 0))],
        out_shape=[SDS((1, LANES), f32), SDS((s, d), f32), SDS((1, d), f32)],
        compiler_params=_cp(("arbitrary",)),
    )(h, target, gain)


def _adamw(name, parts, w, m, v):
    r, c = w.shape
    rb = r
    for cand in (256, 128, 64, 32, 16):
        if r % cand == 0 and cand * c * 4 <= (2 << 20):
            rb = cand
            break
    n_parts = parts.shape[0]

    def body(p_ref, w_ref, m_ref, v_ref, g_out, d_out, m_out, v_out):
        g = p_ref[0].astype(f32)
        for k in range(1, n_parts):
            g = g + p_ref[k].astype(f32)
        m_new = ADAM_B1 * m_ref[...] + (1.0 - ADAM_B1) * g
        v_new = ADAM_B2 * v_ref[...] + (1.0 - ADAM_B2) * (g * g)
        m_hat = m_new / (1.0 - ADAM_B1 ** ADAM_STEP)
        v_hat = v_new / (1.0 - ADAM_B2 ** ADAM_STEP)
        g_out[...] = g
        d_out[...] = -ADAM_LR * (m_hat / (jnp.sqrt(v_hat) + ADAM_EPS) + ADAM_WD * w_ref[...])
        m_out[...] = m_new
        v_out[...] = v_new

    blk = pl.BlockSpec((rb, c), lambda i: (i, 0))
    return pl.pallas_call(
        body, name=name, grid=(r // rb,),
        in_specs=[pl.BlockSpec((n_parts, rb, c), lambda i: (0, i, 0)), blk, blk, blk],
        out_specs=[blk, blk, blk, blk],
        out_shape=[SDS((r, c), f32)] * 4,
        compiler_params=_cp(("parallel",)),
    )(parts, w, m, v)


def _position():
    x, y, c = lax.axis_index("x"), lax.axis_index("y"), lax.axis_index("c")
    return x, y, c


def _all_gather(name, shards):
    n = len(shards)

    def body(*refs):
        ins, outs = refs[:n], refs[n:2 * n]
        send_sems, recv_sems, local_sems = refs[2 * n:]
        x, y, c = _position()
        me, sibling = (x, y, c), (x, y, 1 - c)
        chips = [(1 - x, y), (x, 1 - y), (1 - x, 1 - y)]

        def slot(a, block):
            px, py, pc = block
            return outs[a].at[4 * px + 2 * py + pc]

        def copy(a, k, block, to, src=None):
            return pltpu.make_async_remote_copy(
                src_ref=slot(a, block) if src is None else src, dst_ref=slot(a, block),
                send_sem=send_sems.at[a, k], recv_sem=recv_sems.at[a, k], device_id=to, device_id_type=MESH)

        local = [pltpu.make_async_copy(ins[a], slot(a, me), local_sems.at[a]) for a in range(n)]
        for cp in local:
            cp.start()
        started = []
        for a in range(n):
            first = [copy(a, 0, me, sibling, src=ins[a])]
            first += [copy(a, 1 + j, me, (*chip, c), src=ins[a]) for j, chip in enumerate(chips)]
            for cp in first:
                cp.start()
            started += first
        for a in range(n):
            for j, chip in enumerate(chips):
                copy(a, 1 + j, (*chip, c), me).wait_recv()
                passed = copy(a, 4 + j, (*chip, c), sibling)
                passed.start()
                started.append(passed)
        for a in range(n):
            copy(a, 0, sibling, me).wait_recv()
            for j, chip in enumerate(chips):
                copy(a, 4 + j, (*chip, 1 - c), me).wait_recv()
        for cp in started:
            cp.wait_send()
        for cp in local:
            cp.wait()

    any_spec = pl.BlockSpec(memory_space=pl.ANY)
    outs = pl.pallas_call(
        body, name=name,
        in_specs=[any_spec] * n, out_specs=[any_spec] * n,
        out_shape=[SDS((N_DEV, *a.shape), a.dtype) for a in shards],
        scratch_shapes=[pltpu.SemaphoreType.DMA((n, 7)), pltpu.SemaphoreType.DMA((n, 7)), pltpu.SemaphoreType.DMA((n,))],
    )(*shards)
    return list(outs)


def _exchange(name, stacks):
    n = len(stacks)

    def body(*refs):
        ins, outs = refs[:n], refs[n:2 * n]
        send_sems, recv_sems, local_sems = refs[2 * n:]
        x, y, c = _position()
        me = 4 * x + 2 * y + c
        local = [pltpu.make_async_copy(ins[a].at[me], outs[a].at[me], local_sems.at[a]) for a in range(n)]
        for cp in local:
            cp.start()
        sends, recvs = [], []
        for a in range(n):
            for k in range(1, N_DEV):
                fx, fy, fc = (k >> 2) & 1, (k >> 1) & 1, k & 1
                px = (1 - x) if fx else x
                py = (1 - y) if fy else y
                pc = (1 - c) if fc else c
                peer = 4 * px + 2 * py + pc
                send = pltpu.make_async_remote_copy(
                    src_ref=ins[a].at[peer], dst_ref=outs[a].at[me],
                    send_sem=send_sems.at[a, k - 1], recv_sem=recv_sems.at[a, k - 1],
                    device_id=(px, py, pc), device_id_type=MESH)
                send.start()
                sends.append(send)
                recvs.append(pltpu.make_async_remote_copy(
                    src_ref=ins[a].at[peer], dst_ref=outs[a].at[peer],
                    send_sem=send_sems.at[a, k - 1], recv_sem=recv_sems.at[a, k - 1],
                    device_id=(px, py, pc), device_id_type=MESH))
        for cp in recvs:
            cp.wait_recv()
        for cp in sends:
            cp.wait_send()
        for cp in local:
            cp.wait()

    any_spec = pl.BlockSpec(memory_space=pl.ANY)
    outs = pl.pallas_call(
        body, name=name,
        in_specs=[any_spec] * n, out_specs=[any_spec] * n,
        out_shape=[SDS(a.shape, a.dtype) for a in stacks],
        scratch_shapes=[pltpu.SemaphoreType.DMA((n, 7)), pltpu.SemaphoreType.DMA((n, 7)), pltpu.SemaphoreType.DMA((n,))],
    )(*stacks)
    return list(outs)


def _rows_from_shards(g):
    n, l, r, c = g.shape
    return g.transpose(1, 0, 2, 3).reshape(l, n * r, c)


def _rows_to_shards(w):
    l, rows, c = w.shape
    return w.reshape(l, N_DEV, rows // N_DEV, c).transpose(1, 0, 2, 3)


def _pad_lanes(a, width):
    return jnp.pad(a, [(0, 0)] * (a.ndim - 1) + [(0, width - a.shape[-1])])


def _row(vec, width=None):
    vec = vec.reshape(1, -1)
    return vec if width is None else _pad_lanes(vec, width)


class _SmallPack:
    def __init__(self, shapes):
        self.shapes, self.offsets, at = shapes, {}, 0
        for name, shape in shapes.items():
            last = shape[-1]
            lead = int(math.prod(shape[:-1]))
            rows = lead * (last // LANES) if last >= LANES else lead
            self.offsets[name] = (at, rows)
            at += rows
        self.rows = -(-at // 8) * 8

    def pack(self, values):
        pieces = []
        for name, shape in self.shapes.items():
            val = values[name].astype(f32)
            if shape[-1] >= LANES:
                pieces.append(val.reshape(-1, LANES))
            else:
                pieces.append(_pad_lanes(val.reshape(-1, shape[-1]), LANES))
        used = sum(p.shape[0] for p in pieces)
        if used < self.rows:
            pieces.append(jnp.zeros((self.rows - used, LANES), f32))
        return jnp.concatenate(pieces, axis=0)

    def unpack(self, packed):
        out = {}
        for name, shape in self.shapes.items():
            at, rows = self.offsets[name]
            blk = packed[at:at + rows]
            out[name] = blk.reshape(shape) if shape[-1] >= LANES else blk[:, :shape[-1]].reshape(shape)
        return out


def kernel(x, mem, ffn1_norm, ffn1_w_gate_up, ffn1_w_down, mix_norm, ffn2_norm, ffn2_w_gate_up, ffn2_w_down, gdn_w_in, gdn_conv, gdn_A_log, gdn_dt_bias, gdn_out_norm, fox_w_in, w_out, mem_norm, mem_w_kv, kv_norm, kv_w, kv_b_f, final_norm, loss_target, m_ffn1_norm, m_ffn1_w_gate_up, m_ffn1_w_down, m_mix_norm, m_ffn2_norm, m_ffn2_w_gate_up, m_ffn2_w_down, m_gdn_w_in, m_gdn_conv, m_gdn_A_log, m_gdn_dt_bias, m_gdn_out_norm, m_fox_w_in, m_w_out, m_mem_norm, m_mem_w_kv, m_kv_norm, m_kv_w, m_kv_b_f, m_final_norm, v_ffn1_norm, v_ffn1_w_gate_up, v_ffn1_w_down, v_mix_norm, v_ffn2_norm, v_ffn2_w_gate_up, v_ffn2_w_down, v_gdn_w_in, v_gdn_conv, v_gdn_A_log, v_gdn_dt_bias, v_gdn_out_norm, v_fox_w_in, v_w_out, v_mem_norm, v_mem_w_kv, v_kv_norm, v_kv_w, v_kv_b_f, v_final_norm):
    weights = dict(ffn1_norm=ffn1_norm, ffn1_w_gate_up=ffn1_w_gate_up, ffn1_w_down=ffn1_w_down, mix_norm=mix_norm,
                   ffn2_norm=ffn2_norm, ffn2_w_gate_up=ffn2_w_gate_up, ffn2_w_down=ffn2_w_down, gdn_w_in=gdn_w_in,
                   gdn_conv=gdn_conv, gdn_A_log=gdn_A_log, gdn_dt_bias=gdn_dt_bias, gdn_out_norm=gdn_out_norm,
                   fox_w_in=fox_w_in, w_out=w_out, mem_norm=mem_norm, mem_w_kv=mem_w_kv, kv_norm=kv_norm, kv_w=kv_w,
                   kv_b_f=kv_b_f, final_norm=final_norm)
    mom_m = dict(ffn1_norm=m_ffn1_norm, ffn1_w_gate_up=m_ffn1_w_gate_up, ffn1_w_down=m_ffn1_w_down, mix_norm=m_mix_norm,
                 ffn2_norm=m_ffn2_norm, ffn2_w_gate_up=m_ffn2_w_gate_up, ffn2_w_down=m_ffn2_w_down, gdn_w_in=m_gdn_w_in,
                 gdn_conv=m_gdn_conv, gdn_A_log=m_gdn_A_log, gdn_dt_bias=m_gdn_dt_bias, gdn_out_norm=m_gdn_out_norm,
                 fox_w_in=m_fox_w_in, w_out=m_w_out, mem_norm=m_mem_norm, mem_w_kv=m_mem_w_kv, kv_norm=m_kv_norm,
                 kv_w=m_kv_w, kv_b_f=m_kv_b_f, final_norm=m_final_norm)
    mom_v = dict(ffn1_norm=v_ffn1_norm, ffn1_w_gate_up=v_ffn1_w_gate_up, ffn1_w_down=v_ffn1_w_down, mix_norm=v_mix_norm,
                 ffn2_norm=v_ffn2_norm, ffn2_w_gate_up=v_ffn2_w_gate_up, ffn2_w_down=v_ffn2_w_down, gdn_w_in=v_gdn_w_in,
                 gdn_conv=v_gdn_conv, gdn_A_log=v_gdn_A_log, gdn_dt_bias=v_gdn_dt_bias, gdn_out_norm=v_gdn_out_norm,
                 fox_w_in=v_fox_w_in, w_out=v_w_out, mem_norm=v_mem_norm, mem_w_kv=v_mem_w_kv, kv_norm=v_kv_norm,
                 kv_w=v_kv_w, kv_b_f=v_kv_b_f, final_norm=v_final_norm)
    names = list(weights)
    small_names = [n for n in names if weights[n].shape == mom_m[n].shape and n in (
        "ffn1_norm", "mix_norm", "ffn2_norm", "gdn_A_log", "gdn_dt_bias", "gdn_out_norm", "mem_norm", "kv_norm",
        "kv_b_f", "final_norm")]
    big_names = [n for n in names if n not in small_names]

    h = x[0]
    target = loss_target[0]
    mem_tokens = mem[0]
    s, d = h.shape
    depth = ffn1_norm.shape[0]
    n_a = gdn_w_in.shape[0]
    n_heads, head_dim = gdn_A_log.shape[1], gdn_out_norm.shape[1]
    gw = n_heads * head_dim
    a_in = gdn_w_in.shape[2]
    mem_w = a_in - 4 * gw - 2 * n_heads
    a_in_pad = 4 * gw + mem_w + LANES
    kv_width = kv_w.shape[1]
    kv_pad = 2 * gw + LANES
    fh = ffn1_w_down.shape[1] * N_DEV

    def permute_in(w):
        ab = w[..., 4 * gw:4 * gw + 2 * n_heads]
        return jnp.concatenate([w[..., :4 * gw], w[..., 4 * gw + 2 * n_heads:], _pad_lanes(ab, LANES)], axis=-1)

    def unpermute_in(w):
        return jnp.concatenate([w[..., :4 * gw], w[..., 4 * gw + mem_w:4 * gw + mem_w + 2 * n_heads],
                                w[..., 4 * gw:4 * gw + mem_w]], axis=-1)

    gathered = _all_gather("gather_weights", [
        ffn1_w_gate_up.astype(bf16), ffn1_w_down.astype(bf16), ffn2_w_gate_up.astype(bf16), ffn2_w_down.astype(bf16),
        permute_in(gdn_w_in).astype(bf16), fox_w_in.astype(bf16), w_out.astype(bf16), mem_w_kv.astype(bf16),
        _pad_lanes(kv_w, kv_pad).astype(bf16)[None], gdn_conv])
    wgu1, wd1_s, wgu2, wd2_s, win_s, wfox_s, wout_s, wmem_s, wkv_s, conv_s = gathered
    wd1, wd2 = _rows_from_shards(wd1_s), _rows_from_shards(wd2_s)
    win, wfox, wout = _rows_from_shards(win_s), _rows_from_shards(wfox_s), _rows_from_shards(wout_s)
    wmem = _rows_from_shards(wmem_s)
    wmem_cat = wmem.transpose(1, 0, 2).reshape(d, depth * 2 * mem_w)
    wkv = _rows_from_shards(wkv_s)[0]
    conv_w = conv_s.transpose(1, 2, 0, 3).reshape(n_a, gdn_conv.shape[1], 3 * gw)

    a_log_rows = [_row(gdn_A_log[l], LANES) for l in range(n_a)]
    dt_rows = [_row(gdn_dt_bias[l], LANES) for l in range(n_a)]
    onorm_rows = [_row(gdn_out_norm[l]) for l in range(n_a)]
    b_f_row = _row(kv_b_f, LANES)

    (mem_n,) = _rowwise("mem_norm", lambda t, g: (_rms(t, g),), [mem_tokens], [_row(mem_norm)], [(d, bf16)])
    mem_kv = _matmul("mem_kv", mem_n, wmem_cat, "nn", f32)

    saved = []
    shared = None
    for l in range(depth):
        rec = {"h0": h}
        h1, gate1, up1 = _ffn_fwd(f"ffn1_fwd_{l}", h, _row(ffn1_norm[l]), wgu1, wd1, l)
        (u,) = _rowwise(f"mix_norm_{l}", lambda t, g: (_rms(t, g),), [h1], [_row(mix_norm[l])], [(d, bf16)])
        rec.update(h1=h1, u=u, ffn1=(gate1, up1))
        if l < n_a:
            proj = _matmul(f"gdn_in_{l}", u, win[l], "nn", f32)
            yc = _conv_fwd(f"conv_fwd_{l}", V(proj, cb=0, w=3 * gw), conv_w[l])
            ab_view = V(proj, cb=(4 * gw + mem_w) // LANES, w=LANES)
            q, k, v, gb = _rowwise(f"gdn_pre_{l}", functools.partial(_gdn_pre, n_heads, head_dim),
                                   [yc, ab_view], [a_log_rows[l], dt_rows[l]],
                                   [(gw, f32), (gw, f32), (gw, f32), (LANES, f32)])
            o, states = _gdn_chunk_fwd(f"gdn_chunk_fwd_{l}", q, k, v, gb, n_heads, head_dim)
            z_view = V(proj, cb=3, w=gw)
            (main,) = _rowwise(f"gdn_post_{l}", functools.partial(_gdn_post, n_heads, head_dim),
                               [o, z_view], [onorm_rows[l]], [(gw, bf16)])
            qmem_view = V(proj, cb=4 * gw // mem_w, w=mem_w)
            rec.update(proj=proj, yc=yc, q=q, k=k, v=v, gb=gb, o=o, states=states)
        else:
            proj = _matmul(f"fox_in_{l}", u, wfox[l - n_a], "nn", bf16)
            sk, sv, crow = shared["k"], shared["v"], shared["crow"]
            main, lse = _fox_fwd(f"fox_fwd_{l}", proj, sk, sv, crow, n_heads, head_dim)
            qmem_view = V(proj, cb=gw // mem_w, w=mem_w)
            rec.update(proj=proj, lse=lse)
        km = V(mem_kv, cb=2 * l, w=mem_w)
        vm = V(mem_kv, cb=2 * l + 1, w=mem_w)
        (mem_out,) = _rowwise(f"mem_attn_{l}", _mem_attn, [qmem_view], [km, vm], [(mem_w, bf16)])
        cat = jnp.concatenate([main, mem_out], axis=1)
        h2 = _matmul(f"out_proj_{l}", cat, wout[l], "nn", f32, add=h1)
        h3, gate2, up2 = _ffn_fwd(f"ffn2_fwd_{l}", h2, _row(ffn2_norm[l]), wgu2, wd2, l)
        rec.update(cat=cat, h2=h2, qmem=qmem_view, ffn2=(gate2, up2))
        saved.append(rec)
        h = h3
        if l == n_a - 1:
            (hn,) = _rowwise("kv_norm", lambda t, g: (_rms(t, g),), [h], [_row(kv_norm)], [(d, bf16)])
            p = _matmul("kv_proj", hn, wkv, "nn", f32)
            pk, pv, pf = V(p, cb=0, w=gw), V(p, cb=1, w=gw), V(p, cb=2 * gw // LANES, w=LANES)
            sk, sv, log_f = _rowwise("kv_post", functools.partial(_kv_post, n_heads), [pk, pv, pf], [b_f_row],
                                     [(gw, bf16), (gw, bf16), (LANES, f32)])
            cum = _cumsum("forget_cumsum", [log_f], reverse=False)
            c_heads = cum[:, :n_heads].T
            shared = dict(k=sk, v=sv, crow=c_heads.reshape(n_heads, 1, s), h=h, hn=hn, p=p, views=(pk, pv, pf))

    loss_part, dh, d_final = _final_loss("final_loss", h, target, _row(final_norm))
    loss = lax.psum(loss_part[0, 0], ("x", "y", "c"))

    grads = {}
    per_layer = {n: [None] * depth for n in ("ffn1_norm", "mix_norm", "ffn2_norm", "ffn1_gu", "ffn1_d", "ffn2_gu",
                                             "ffn2_d", "w_out")}
    per_a = {n: [None] * n_a for n in ("gdn_w_in", "gdn_conv", "gdn_A_log", "gdn_dt_bias", "gdn_out_norm")}
    per_b = {"fox_w_in": [None] * (depth - n_a)}
    d_mem_kv = [None] * depth
    fox_grads = []

    def ffn_backward(tag, l, h_in, d_out, gain, kept, wgu, wd):
        parts, dwg, dwu, dwd = _ffn_bwd(f"{tag}_bwd_{l}", h_in, d_out, _row(gain), kept[0], kept[1], wgu, wd, l)
        nh = parts.shape[0]
        (d_in,), (d_gain,) = _rowwise_vjp(
            f"{tag}_norm_bwd_{l}", lambda t, g: (_rms(t, g),), [h_in], [_row(gain)],
            [[V(parts, lead=t) for t in range(nh)]], [f32], add=d_out)
        return d_in, d_gain, jnp.concatenate([dwg, dwu], axis=0), dwd

    for l in reversed(range(depth)):
        rec = saved[l]
        if l == n_a - 1:
            dk_list = [V(g["dk"]) for g in fox_grads]
            dv_list = [V(g["dv"]) for g in fox_grads]
            dc_parts = [_pad_lanes(part.reshape(n_heads, s).T, LANES) for g in fox_grads for part in g["dc"]]
            d_log_f = _cumsum("forget_cumsum_bwd", dc_parts, reverse=True)
            pk, pv, pf = shared["views"]
            (dpk, dpv, dpf), (d_bf,) = _rowwise_vjp(
                "kv_post_bwd", functools.partial(_kv_post, n_heads), [pk, pv, pf], [b_f_row],
                [dk_list, dv_list, [d_log_f]], [bf16, bf16, bf16])
            dp = jnp.concatenate([dpk, dpv, dpf], axis=1)
            d_hn = _matmul("kv_proj_dx", dp, wkv, "nt", f32)
            grads["kv_w"] = _matmul("kv_proj_dw", shared["hn"], dp, "tn", f32)[:, :kv_width]
            (dh,), (d_kvn,) = _rowwise_vjp("kv_norm_bwd", lambda t, g: (_rms(t, g),), [shared["h"]], [_row(kv_norm)],
                                           [[d_hn]], [f32], add=dh)
            grads["kv_norm"] = d_kvn.reshape(-1)
            grads["kv_b_f"] = d_bf[0, :n_heads]

        dh2, per_layer["ffn2_norm"][l], per_layer["ffn2_gu"][l], per_layer["ffn2_d"][l] = ffn_backward(
            "ffn2", l, rec["h2"], dh, ffn2_norm[l], rec["ffn2"], wgu2, wd2)
        d_cat = _matmul(f"out_proj_dx_{l}", dh2, wout[l], "nt", f32)
        per_layer["w_out"][l] = _matmul(f"out_proj_dw_{l}", rec["cat"], dh2, "tn", f32)
        d_main = V(d_cat, cb=0, w=gw)
        d_memo = V(d_cat, cb=gw // mem_w, w=mem_w)
        km, vm = V(mem_kv, cb=2 * l, w=mem_w), V(mem_kv, cb=2 * l + 1, w=mem_w)
        (dqmem,), (dkm, dvm) = _rowwise_vjp(f"mem_attn_bwd_{l}", _mem_attn, [rec["qmem"]], [km, vm], [[d_memo]], [bf16])
        d_mem_kv[l] = jnp.concatenate([dkm, dvm], axis=1)
        if l < n_a:
            proj = rec["proj"]
            z_view = V(proj, cb=3, w=gw)
            (d_o, d_z), (d_onorm,) = _rowwise_vjp(
                f"gdn_post_bwd_{l}", functools.partial(_gdn_post, n_heads, head_dim), [rec["o"], z_view],
                [onorm_rows[l]], [[d_main]], [f32, bf16])
            dq, dk, dv, dgb = _gdn_chunk_bwd(f"gdn_chunk_bwd_{l}", rec["q"], rec["k"], rec["v"], rec["gb"],
                                             rec["states"], d_o, n_heads, head_dim)
            ab_view = V(proj, cb=(4 * gw + mem_w) // LANES, w=LANES)
            (d_yc, d_ab), (d_alog, d_dt) = _rowwise_vjp(
                f"gdn_pre_bwd_{l}", functools.partial(_gdn_pre, n_heads, head_dim), [rec["yc"], ab_view],
                [a_log_rows[l], dt_rows[l]], [[dq], [dk], [dv], [dgb]], [f32, bf16])
            d_qkv, d_conv = _conv_bwd(f"conv_bwd_{l}", V(proj, cb=0, w=3 * gw), d_yc, conv_w[l])
            d_proj = jnp.concatenate([d_qkv, d_z, dqmem, d_ab], axis=1)
            du = _matmul(f"gdn_in_dx_{l}", d_proj, win[l], "nt", f32)
            per_a["gdn_w_in"][l] = unpermute_in(_matmul(f"gdn_in_dw_{l}", rec["u"], d_proj, "tn", f32))
            per_a["gdn_conv"][l] = d_conv
            per_a["gdn_A_log"][l] = d_alog[0, :n_heads]
            per_a["gdn_dt_bias"][l] = d_dt[0, :n_heads]
            per_a["gdn_out_norm"][l] = d_onorm[0]
        else:
            proj = rec["proj"]
            sk, sv, crow = shared["k"], shared["v"], shared["crow"]
            dq, delta, dc_col = _fox_bwd_dq(f"fox_dq_{l}", proj, sk, sv, crow, rec["lse"], d_cat, n_heads, head_dim)
            dk, dv, dc_row = _fox_bwd_dkv(f"fox_dkv_{l}", proj, sk, sv, crow, rec["lse"], delta, d_cat,
                                          n_heads, head_dim)
            fox_grads.append(dict(dk=dk, dv=dv, dc=(dc_row, dc_col)))
            d_proj = jnp.concatenate([dq, dqmem], axis=1)
            du = _matmul(f"fox_in_dx_{l}", d_proj, wfox[l - n_a], "nt", f32)
            per_b["fox_w_in"][l - n_a] = _matmul(f"fox_in_dw_{l}", rec["u"], d_proj, "tn", f32)
        (dh1,), (d_mix,) = _rowwise_vjp(f"mix_norm_bwd_{l}", lambda t, g: (_rms(t, g),), [rec["h1"]],
                                        [_row(mix_norm[l])], [[du]], [f32], add=dh2)
        per_layer["mix_norm"][l] = d_mix
        dh, per_layer["ffn1_norm"][l], per_layer["ffn1_gu"][l], per_layer["ffn1_d"][l] = ffn_backward(
            "ffn1", l, rec["h0"], dh1, ffn1_norm[l], rec["ffn1"], wgu1, wd1)

    grad_x = dh[None]

    d_mem_kv_cat = jnp.concatenate(d_mem_kv, axis=1)
    d_wmem_cat = _matmul("mem_kv_dw", mem_n, d_mem_kv_cat, "tn", f32)
    d_mem_n = _matmul("mem_kv_dx", d_mem_kv_cat, wmem_cat, "nt", f32)
    _, (d_memnorm,) = _rowwise_vjp("mem_norm_bwd", lambda t, g: (_rms(t, g),), [mem_tokens], [_row(mem_norm)],
                                   [[d_mem_n]], [None])

    def gu_stack(per):
        return jnp.stack(per, axis=1).astype(bf16)

    stacks = dict(
        ffn1_w_gate_up=gu_stack(per_layer["ffn1_gu"]),
        ffn1_w_down=_rows_to_shards(jnp.stack(per_layer["ffn1_d"])).astype(bf16),
        ffn2_w_gate_up=gu_stack(per_layer["ffn2_gu"]),
        ffn2_w_down=_rows_to_shards(jnp.stack(per_layer["ffn2_d"])).astype(bf16),
        gdn_w_in=_rows_to_shards(jnp.stack(per_a["gdn_w_in"])).astype(bf16),
        gdn_conv=jnp.stack(per_a["gdn_conv"]).reshape(n_a, -1, N_DEV, 3 * gw // N_DEV).transpose(2, 0, 1, 3),
        fox_w_in=_rows_to_shards(jnp.stack(per_b["fox_w_in"])).astype(bf16),
        w_out=_rows_to_shards(jnp.stack(per_layer["w_out"])).astype(bf16),
        mem_w_kv=_rows_to_shards(d_wmem_cat.reshape(d, depth, 2 * mem_w).transpose(1, 0, 2)).astype(bf16),
        kv_w=_rows_to_shards(grads["kv_w"][None])[:, 0].astype(bf16),
    )
    received = dict(zip(big_names, _exchange("exchange_grads", [stacks[n] for n in big_names])))

    small_shapes = {n: weights[n].shape for n in small_names}
    pack = _SmallPack(small_shapes)
    small_grads = dict(
        ffn1_norm=jnp.concatenate(per_layer["ffn1_norm"], axis=0), mix_norm=jnp.concatenate(per_layer["mix_norm"], axis=0),
        ffn2_norm=jnp.concatenate(per_layer["ffn2_norm"], axis=0), gdn_A_log=jnp.stack(per_a["gdn_A_log"]),
        gdn_dt_bias=jnp.stack(per_a["gdn_dt_bias"]), gdn_out_norm=jnp.stack(per_a["gdn_out_norm"]),
        mem_norm=d_memnorm.reshape(-1), kv_norm=grads["kv_norm"], kv_b_f=grads["kv_b_f"], final_norm=d_final.reshape(-1))
    (small_parts,) = _all_gather("gather_small_grads", [pack.pack(small_grads)])

    out_g, out_d, out_m, out_v = {}, {}, {}, {}
    for n in big_names:
        shape = weights[n].shape
        c = shape[-1]
        parts = received[n].reshape(N_DEV, -1, c)
        res = _adamw(f"adamw_{n}", parts, weights[n].reshape(-1, c), mom_m[n].reshape(-1, c), mom_v[n].reshape(-1, c))
        out_g[n], out_d[n], out_m[n], out_v[n] = [r.reshape(shape) for r in res]
    res = _adamw("adamw_small", small_parts, pack.pack({n: weights[n] for n in small_names}),
                 pack.pack({n: mom_m[n] for n in small_names}), pack.pack({n: mom_v[n] for n in small_names}))
    for dst, packed in zip((out_g, out_d, out_m, out_v), res):
        dst.update(pack.unpack(packed))

    return (loss, grad_x, *[out_g[n] for n in names], *[out_d[n] for n in names],
            *[out_m[n] for n in names], *[out_v[n] for n in names])
```

```python
import functools
import math

import jax
import jax.numpy as jnp
from jax import lax
from jax.experimental import pallas as pl
from jax.experimental.pallas import tpu as pltpu

f32 = jnp.float32
bf16 = jnp.bfloat16
SDS = jax.ShapeDtypeStruct
HIGHEST = lax.Precision.HIGHEST

N_DEV = 8
MEM_HEADS = 4
CHUNK = 64
LANES = 128
EPS = 1e-6
NEG_INF = -1e30
ADAM_LR = 0.001
ADAM_B1 = 0.9
ADAM_B2 = 0.999
ADAM_EPS = 1e-08
ADAM_WD = 0.01
ADAM_STEP = 10

ROW_TILE = 256
MM_TILE = 512
FFN_FWD_TILE = 512
FFN_BWD_TILE = 256
ATT_TILE = 512
ATT_HEADS = 3
CUMSUM_TILE = 256
VMEM_LIMIT = 56 * 1024 * 1024

MESH = pl.DeviceIdType.MESH


def _cp(sem=None):
    return pltpu.CompilerParams(dimension_semantics=sem, vmem_limit_bytes=VMEM_LIMIT)


_DIMS = {"nn": (((1,), (0,)), ((), ())), "nt": (((1,), (1,)), ((), ())), "tn": (((0,), (0,)), ((), ()))}


def _dg(a, b, mode):
    return lax.dot_general(a.astype(bf16), b.astype(bf16), _DIMS[mode], preferred_element_type=f32)


@functools.partial(jax.custom_vjp, nondiff_argnums=(2,))
def _mm(a, b, mode):
    return _dg(a, b, mode)


def _mm_fwd(a, b, mode):
    return _dg(a, b, mode), (a, b)


def _mm_bwd(mode, res, ct):
    a, b = res
    if mode == "nn":
        da, db = _dg(ct, b, "nt"), _dg(a, ct, "tn")
    elif mode == "nt":
        da, db = _dg(ct, b, "nn"), _dg(ct, a, "tn")
    else:
        da, db = _dg(b, ct, "nt"), _dg(a, ct, "nn")
    return da.astype(a.dtype), db.astype(b.dtype)


_mm.defvjp(_mm_fwd, _mm_bwd)


def _dgh(a, b, mode="nn"):
    return lax.dot_general(a, b, _DIMS[mode], precision=HIGHEST, preferred_element_type=f32)


@jax.custom_vjp
def _unit_lower_inverses(lows):
    c = lows[0].shape[0]
    ri = lax.broadcasted_iota(jnp.int32, (c, c), 0)
    ci = lax.broadcasted_iota(jnp.int32, (c, c), 1)
    eye = jnp.where(ri == ci, 1.0, 0.0)
    xs = [-low for low in lows]
    rs = [eye + x for x in xs]
    for _ in range(int(math.log2(c)) - 1):
        xs = [_dgh(x, x) for x in xs]
        rs = [r + _dgh(r, x) for r, x in zip(rs, xs)]
    return tuple(rs)


def _uli_fwd(lows):
    ts = _unit_lower_inverses(lows)
    return ts, ts


def _uli_bwd(ts, cts):
    mids = [_dgh(ct, t, "nt") for t, ct in zip(ts, cts)]
    return (tuple(-_dgh(t, m, "tn") for t, m in zip(ts, mids)),)


_unit_lower_inverses.defvjp(_uli_fwd, _uli_bwd)


@functools.partial(jax.custom_vjp, nondiff_argnums=(1,))
def _split_lanes(x, width):
    return tuple(x[:, i * width:(i + 1) * width] for i in range(x.shape[1] // width))


def _split_fwd(x, width):
    return _split_lanes(x, width), None


def _split_bwd(width, _, cts):
    return (jnp.concatenate(list(cts), axis=1),)


_split_lanes.defvjp(_split_fwd, _split_bwd)


def _sigmoid(x):
    return 1.0 / (1.0 + jnp.exp(-x))


def _silu(x):
    return x * _sigmoid(x)


def _softplus(x):
    return jnp.maximum(x, 0.0) + jnp.log1p(jnp.exp(-jnp.abs(x)))


def _rms(x, gain):
    return x * lax.rsqrt(jnp.mean(x * x, axis=-1, keepdims=True) + EPS) * gain


class V:
    def __init__(self, arr, lead=None, cb=0, w=None):
        self.arr, self.lead, self.cb = arr, lead, cb
        self.w = arr.shape[-1] if w is None else w

    @property
    def rows(self):
        return self.arr.shape[-2]

    def spec(self, tile, order=None):
        lead, cb, w = self.lead, self.cb, self.w
        order = order or (lambda i: i)
        if lead is None:
            return pl.BlockSpec((tile, w), lambda i: (order(i), cb))
        return pl.BlockSpec((None, tile, w), lambda i: (lead, order(i), cb))

    def const_spec(self):
        lead, cb, w, r = self.lead, self.cb, self.w, self.rows
        if lead is None:
            return pl.BlockSpec((r, w), lambda i: (0, cb))
        return pl.BlockSpec((None, r, w), lambda i: (lead, 0, cb))


def _v(a):
    return a if isinstance(a, V) else V(a)


def _rowwise(name, fn, rows, consts, outs, tile=None):
    rows = [_v(r) for r in rows]
    consts = [_v(c) for c in consts]
    s = rows[0].rows
    tile = min(tile or ROW_TILE, s)
    nr, nc = len(rows), len(consts)

    def body(*refs):
        vals = [r[...].astype(f32) for r in refs[:nr + nc]]
        res = fn(*vals)
        for o, val in zip(refs[nr + nc:], res):
            o[...] = val.astype(o.dtype)

    return pl.pallas_call(
        body, name=name, grid=(s // tile,),
        in_specs=[r.spec(tile) for r in rows] + [c.const_spec() for c in consts],
        out_specs=[pl.BlockSpec((tile, w), lambda i: (i, 0)) for w, _ in outs],
        out_shape=[SDS((s, w), dt) for w, dt in outs],
        compiler_params=_cp(("parallel",)),
    )(*[r.arr for r in rows], *[c.arr for c in consts])


def _rowwise_vjp(name, fn, rows, consts, cts, d_rows, add=None, tile=None):
    rows = [_v(r) for r in rows]
    consts = [_v(c) for c in consts]
    cts = [[_v(c) for c in group] for group in cts]
    flat_cts = [c for group in cts for c in group]
    s = rows[0].rows
    tile = min(tile or ROW_TILE, s)
    nr, nc, nt = len(rows), len(consts), len(flat_cts)
    want = [k for k, dt in enumerate(d_rows) if dt is not None]
    has_add = add is not None
    add_v = [_v(add)] if has_add else []

    def body(*refs):
        vals = [r[...].astype(f32) for r in refs[:nr + nc]]
        ct_refs = refs[nr + nc:nr + nc + nt]
        pos = nr + nc + nt
        add_ref = refs[pos] if has_add else None
        pos += 1 if has_add else 0
        drow_refs = refs[pos:pos + len(want)]
        dconst_refs = refs[pos + len(want):]
        ctv, at = [], 0
        for group in cts:
            acc = ct_refs[at][...].astype(f32)
            for r in ct_refs[at + 1:at + len(group)]:
                acc = acc + r[...].astype(f32)
            at += len(group)
            ctv.append(acc)
        _, vjp = jax.vjp(fn, *vals)
        grads = vjp(tuple(ctv))
        for o, k in zip(drow_refs, want):
            g = grads[k]
            if has_add and k == want[0]:
                g = g + add_ref[...].astype(f32)
            o[...] = g.astype(o.dtype)

        @pl.when(pl.program_id(0) == 0)
        def _():
            for o in dconst_refs:
                o[...] = jnp.zeros_like(o)

        for o, g in zip(dconst_refs, grads[nr:]):
            o[...] += g

    outs = pl.pallas_call(
        body, name=name, grid=(s // tile,),
        in_specs=[r.spec(tile) for r in rows] + [c.const_spec() for c in consts]
        + [c.spec(tile) for c in flat_cts] + [a.spec(tile) for a in add_v],
        out_specs=[pl.BlockSpec((tile, rows[k].w), lambda i: (i, 0)) for k in want]
        + [pl.BlockSpec((c.rows, c.w), lambda i: (0, 0)) for c in consts],
        out_shape=[SDS((s, rows[k].w), d_rows[k]) for k in want] + [SDS((c.rows, c.w), f32) for c in consts],
        compiler_params=_cp(("arbitrary",)),
    )(*[r.arr for r in rows], *[c.arr for c in consts], *[c.arr for c in flat_cts], *[a.arr for a in add_v])
    return list(outs[:len(want)]), list(outs[len(want):])


def _pick(n, cap):
    best = None
    for t in range(LANES, min(n, cap) + 1, LANES):
        if n % t == 0:
            best = t
    return best or n


def _matmul(name, a, b, mode, out_dtype, add=None, tn_cap=1280):
    has_add = add is not None
    if mode in ("nn", "nt"):
        m, k = a.shape
        n = b.shape[1] if mode == "nn" else b.shape[0]
        tm = min(MM_TILE, m)
        tn = _pick(n, tn_cap) if k * n * 2 > (8 << 20) else n

        def body(*refs):
            a_ref, b_ref = refs[0], refs[1]
            o_ref = refs[-1]
            acc = _dg(a_ref[...], b_ref[...], mode)
            if has_add:
                acc = acc + refs[2][...].astype(f32)
            o_ref[...] = acc.astype(o_ref.dtype)

        b_spec = pl.BlockSpec((k, tn), lambda i, j: (0, j)) if mode == "nn" else pl.BlockSpec((tn, k), lambda i, j: (j, 0))
        in_specs = [pl.BlockSpec((tm, k), lambda i, j: (i, 0)), b_spec]
        args = [a, b]
        if has_add:
            in_specs.append(pl.BlockSpec((tm, tn), lambda i, j: (i, j)))
            args.append(add)
        return pl.pallas_call(
            body, name=name, grid=(m // tm, n // tn), in_specs=in_specs,
            out_specs=pl.BlockSpec((tm, tn), lambda i, j: (i, j)),
            out_shape=SDS((m, n), out_dtype), compiler_params=_cp(("parallel", "parallel")),
        )(*args)
    kk, m = a.shape
    n = b.shape[1]
    tk = min(MM_TILE, kk)
    tn = _pick(n, tn_cap) if m * n * 4 > (7 << 20) else n

    def body_tn(a_ref, b_ref, o_ref):
        @pl.when(pl.program_id(1) == 0)
        def _():
            o_ref[...] = jnp.zeros_like(o_ref)

        o_ref[...] += _dg(a_ref[...], b_ref[...], "tn")

    return pl.pallas_call(
        body_tn, name=name, grid=(n // tn, kk // tk),
        in_specs=[pl.BlockSpec((tk, m), lambda j, k: (k, 0)), pl.BlockSpec((tk, tn), lambda j, k: (k, j))],
        out_specs=pl.BlockSpec((m, tn), lambda j, k: (0, j)),
        out_shape=SDS((m, n), f32), compiler_params=_cp(("parallel", "arbitrary")),
    )(a, b)


def _ffn_fwd(name, h, gain, wgu, wd, layer):
    s, d = h.shape
    hs = wgu.shape[3]
    nh = wgu.shape[0] // 2
    tm = min(FFN_FWD_TILE, s)

    def body(h_ref, g_ref, wg_ref, wu_ref, wd_ref, o_ref, gate_ref, up_ref, n_scr, acc_scr):
        t = pl.program_id(1)

        @pl.when(t == 0)
        def _():
            n_scr[...] = _rms(h_ref[...], g_ref[...]).astype(bf16)
            acc_scr[...] = jnp.zeros_like(acc_scr)

        n = n_scr[...]
        gate = _dg(n, wg_ref[...], "nn")
        up = _dg(n, wu_ref[...], "nn")
        gate_ref[...] = gate.astype(gate_ref.dtype)
        up_ref[...] = up.astype(up_ref.dtype)
        acc_scr[...] += _dg(_silu(gate) * up, wd_ref[...], "nn")

        @pl.when(t == nh - 1)
        def _():
            o_ref[...] = h_ref[...] + 0.5 * acc_scr[...]

    saved_spec = pl.BlockSpec((None, tm, hs), lambda i, t: (t, i, 0))
    return pl.pallas_call(
        body, name=name, grid=(s // tm, nh),
        in_specs=[
            pl.BlockSpec((tm, d), lambda i, t: (i, 0)),
            pl.BlockSpec((1, d), lambda i, t: (0, 0)),
            pl.BlockSpec((None, None, d, hs), lambda i, t: (t, layer, 0, 0)),
            pl.BlockSpec((None, None, d, hs), lambda i, t: (t + nh, layer, 0, 0)),
            pl.BlockSpec((None, hs, d), lambda i, t: (layer, t, 0)),
        ],
        out_specs=[pl.BlockSpec((tm, d), lambda i, t: (i, 0)), saved_spec, saved_spec],
        out_shape=[SDS((s, d), f32), SDS((nh, s, hs), bf16), SDS((nh, s, hs), bf16)],
        scratch_shapes=[pltpu.VMEM((tm, d), bf16), pltpu.VMEM((tm, d), f32)],
        compiler_params=_cp(("parallel", "arbitrary")),
    )(h, gain, wgu, wgu, wd)


def _ffn_bwd(name, h, dout, gain, gate_s, up_s, wgu, wd, layer):
    s, d = h.shape
    hs = wgu.shape[3]
    nh = wgu.shape[0] // 2
    tm = min(FFN_BWD_TILE, s)

    def body(h_ref, do_ref, g_ref, gate_ref, up_ref, wg_ref, wu_ref, wd_ref, dn_ref, dwg_ref, dwu_ref, dwd_ref):
        @pl.when(pl.program_id(1) == 0)
        def _():
            dwg_ref[...] = jnp.zeros_like(dwg_ref)
            dwu_ref[...] = jnp.zeros_like(dwu_ref)
            dwd_ref[...] = jnp.zeros_like(dwd_ref)

        n = _rms(h_ref[...], g_ref[...]).astype(bf16)
        wg, wu, wdn = wg_ref[...], wu_ref[...], wd_ref[...]
        gate = gate_ref[...].astype(f32)
        up = up_ref[...].astype(f32)
        sg = _sigmoid(gate)
        act = gate * sg
        dy = (0.5 * do_ref[...]).astype(bf16)
        da = _dg(dy, wdn, "nt")
        dup = (da * act).astype(bf16)
        dgate = (da * up * (sg * (1.0 + gate * (1.0 - sg)))).astype(bf16)
        dwd_ref[...] += _dg(act * up, dy, "tn")
        dwg_ref[...] += _dg(n, dgate, "tn")
        dwu_ref[...] += _dg(n, dup, "tn")
        dn_ref[...] = (_dg(dgate, wg, "nt") + _dg(dup, wu, "nt")).astype(dn_ref.dtype)

    return pl.pallas_call(
        body, name=name, grid=(nh, s // tm),
        in_specs=[
            pl.BlockSpec((tm, d), lambda t, i: (i, 0)),
            pl.BlockSpec((tm, d), lambda t, i: (i, 0)),
            pl.BlockSpec((1, d), lambda t, i: (0, 0)),
            pl.BlockSpec((None, tm, hs), lambda t, i: (t, i, 0)),
            pl.BlockSpec((None, tm, hs), lambda t, i: (t, i, 0)),
            pl.BlockSpec((None, None, d, hs), lambda t, i: (t, layer, 0, 0)),
            pl.BlockSpec((None, None, d, hs), lambda t, i: (t + nh, layer, 0, 0)),
            pl.BlockSpec((None, hs, d), lambda t, i: (layer, t, 0)),
        ],
        out_specs=[
            pl.BlockSpec((None, tm, d), lambda t, i: (t, i, 0)),
            pl.BlockSpec((None, d, hs), lambda t, i: (t, 0, 0)),
            pl.BlockSpec((None, d, hs), lambda t, i: (t, 0, 0)),
            pl.BlockSpec((hs, d), lambda t, i: (t, 0)),
        ],
        out_shape=[SDS((nh, s, d), bf16), SDS((nh, d, hs), f32), SDS((nh, d, hs), f32), SDS((nh * hs, d), f32)],
        compiler_params=_cp(("parallel", "arbitrary")),
    )(h, dout, gain, gate_s, up_s, wgu, wgu, wd)


def _conv_fwd(name, x, w):
    x = _v(x)
    s, c = x.rows, x.w
    cw = w.shape[0]
    tile = min(ROW_TILE, s)
    cb = x.cb

    def body(x_ref, halo_ref, w_ref, o_ref, buf):
        first = pl.program_id(0) == 0
        buf[0:8, :] = jnp.where(first, 0.0, halo_ref[...])
        buf[8:8 + tile, :] = x_ref[...]
        acc = w_ref[0:1, :] * buf[pl.ds(8 - cw + 1, tile), :]
        for j in range(1, cw):
            acc = acc + w_ref[j:j + 1, :] * buf[pl.ds(8 - cw + 1 + j, tile), :]
        o_ref[...] = acc

    return pl.pallas_call(
        body, name=name, grid=(s // tile,),
        in_specs=[
            pl.BlockSpec((tile, c), lambda i: (i, cb)),
            pl.BlockSpec((8, c), lambda i: (jnp.maximum(i * (tile // 8) - 1, 0), cb)),
            pl.BlockSpec((cw, c), lambda i: (0, 0)),
        ],
        out_specs=pl.BlockSpec((tile, c), lambda i: (i, 0)),
        out_shape=SDS((s, c), f32),
        scratch_shapes=[pltpu.VMEM((tile + 8, c), f32)],
        compiler_params=_cp(("parallel",)),
    )(x.arr, x.arr, w)


def _conv_bwd(name, x, dy, w):
    x = _v(x)
    s, c = x.rows, x.w
    cw = w.shape[0]
    tile = min(ROW_TILE, s)
    n_tiles = s // tile
    cb = x.cb

    def body(x_ref, xh_ref, dy_ref, dyh_ref, w_ref, dx_ref, dw_ref, xbuf, dbuf):
        i = pl.program_id(0)
        xbuf[0:8, :] = jnp.where(i == 0, 0.0, xh_ref[...])
        xbuf[8:8 + tile, :] = x_ref[...]
        dyv = dy_ref[...]
        dbuf[0:tile, :] = dyv
        dbuf[tile:tile + 8, :] = jnp.where(i == n_tiles - 1, 0.0, dyh_ref[...])

        @pl.when(i == 0)
        def _():
            dw_ref[...] = jnp.zeros_like(dw_ref)

        acc = w_ref[0:1, :] * dbuf[pl.ds(cw - 1, tile), :]
        for j in range(1, cw):
            acc = acc + w_ref[j:j + 1, :] * dbuf[pl.ds(cw - 1 - j, tile), :]
        dx_ref[...] = acc.astype(dx_ref.dtype)
        for j in range(cw):
            dw_ref[j:j + 1, :] += jnp.sum(xbuf[pl.ds(8 - cw + 1 + j, tile), :] * dyv, axis=0, keepdims=True)

    return pl.pallas_call(
        body, name=name, grid=(n_tiles,),
        in_specs=[
            pl.BlockSpec((tile, c), lambda i: (i, cb)),
            pl.BlockSpec((8, c), lambda i: (jnp.maximum(i * (tile // 8) - 1, 0), cb)),
            pl.BlockSpec((tile, c), lambda i: (i, 0)),
            pl.BlockSpec((8, c), lambda i: (jnp.minimum((i + 1) * (tile // 8), s // 8 - 1), 0)),
            pl.BlockSpec((cw, c), lambda i: (0, 0)),
        ],
        out_specs=[pl.BlockSpec((tile, c), lambda i: (i, 0)), pl.BlockSpec((cw, c), lambda i: (0, 0))],
        out_shape=[SDS((s, c), bf16), SDS((cw, c), f32)],
        scratch_shapes=[pltpu.VMEM((tile + 8, c), f32), pltpu.VMEM((tile + 8, c), f32)],
        compiler_params=_cp(("arbitrary",)),
    )(x.arr, x.arr, dy, dy, w)


def _gdn_pre(n_heads, head_dim, yc, ab, a_log, dt_bias):
    gw = n_heads * head_dim
    act = _silu(yc)
    parts = _split_lanes(act, head_dim)
    qs = [p * lax.rsqrt(jnp.sum(p * p, axis=-1, keepdims=True) + EPS) * (head_dim ** -0.5) for p in parts[:n_heads]]
    ks = [p * lax.rsqrt(jnp.sum(p * p, axis=-1, keepdims=True) + EPS) for p in parts[n_heads:2 * n_heads]]
    lane = lax.broadcasted_iota(jnp.int32, ab.shape, 1)
    g = -jnp.exp(a_log) * _softplus(ab + dt_bias)
    gb = jnp.where(lane < n_heads, g, jnp.where(lane < 2 * n_heads, _sigmoid(ab), 0.0))
    del gw
    return (jnp.concatenate(qs, axis=1), jnp.concatenate(ks, axis=1),
            jnp.concatenate(list(parts[2 * n_heads:]), axis=1), gb)


def _gdn_post(n_heads, head_dim, o, z, out_norm):
    parts = _split_lanes(o, head_dim)
    normed = jnp.concatenate([_rms(p, out_norm) for p in parts], axis=1)
    return (normed * _silu(z),)


def _gdn_chunk(n_heads, head_dim, q, k, v, gb, *states):
    c = q.shape[0]
    ri = lax.broadcasted_iota(jnp.int32, (c, c), 0)
    ci = lax.broadcasted_iota(jnp.int32, (c, c), 1)
    incl, strict, diag = ri >= ci, ri > ci, ri == ci
    lane = lax.broadcasted_iota(jnp.int32, gb.shape, 1)
    qs, ks, vs = _split_lanes(q, head_dim), _split_lanes(k, head_dim), _split_lanes(v, head_dim)
    heads = range(n_heads)
    g = [jnp.sum(jnp.where(lane == h, gb, 0.0), axis=1, keepdims=True) for h in heads]
    beta = [jnp.sum(jnp.where(lane == n_heads + h, gb, 0.0), axis=1, keepdims=True) for h in heads]
    g_row = [jnp.sum(jnp.where(diag, g[h], 0.0), axis=0, keepdims=True) for h in heads]
    cg_col = [jnp.sum(jnp.where(incl, g_row[h], 0.0), axis=1, keepdims=True) for h in heads]
    cg_row = [jnp.sum(jnp.where(ri <= ci, g[h], 0.0), axis=0, keepdims=True) for h in heads]
    g_last = [jnp.sum(g[h], axis=0, keepdims=True) for h in heads]
    decay = [jnp.where(incl, jnp.exp(jnp.where(incl, cg_col[h] - cg_row[h], 0.0)), 0.0) for h in heads]
    kb = [ks[h] * beta[h] for h in heads]
    lower = [jnp.where(strict, _mm(kb[h], ks[h], "nt") * decay[h], 0.0) for h in heads]
    eye = jnp.where(diag, 1.0, 0.0)
    off_diag = [t - eye for t in _unit_lower_inverses(tuple(lower))]
    e_col = [jnp.exp(cg_col[h]) for h in heads]
    vb = [vs[h] * beta[h] for h in heads]
    kbg = [kb[h] * e_col[h] for h in heads]
    u = [vb[h] + _mm(off_diag[h], vb[h], "nn") for h in heads]
    w = [kbg[h] + _mm(off_diag[h], kbg[h], "nn") for h in heads]
    qk = [jnp.where(incl, _mm(qs[h], ks[h], "nt") * decay[h], 0.0) for h in heads]
    v_new = [u[h] - _mm(w[h], states[h], "nn") for h in heads]
    inter = [_mm(qs[h] * e_col[h], states[h], "nn") for h in heads]
    outs = [inter[h] + _mm(qk[h], v_new[h], "nn") for h in heads]
    k_tail = [ks[h] * jnp.exp(g_last[h] - cg_col[h]) for h in heads]
    new_states = [states[h] * jnp.exp(g_last[h]) + _mm(k_tail[h], v_new[h], "tn") for h in heads]
    return (jnp.concatenate(outs, axis=1), *new_states)


def _gdn_chunk_fwd(name, q, k, v, gb, n_heads, head_dim):
    s, gw = q.shape
    n = s // CHUNK
    fn = functools.partial(_gdn_chunk, n_heads, head_dim)

    def body(q_ref, k_ref, v_ref, gb_ref, o_ref, st_ref, st_scr):
        @pl.when(pl.program_id(0) == 0)
        def _():
            st_scr[...] = jnp.zeros_like(st_scr)

        st_ref[...] = st_scr[...]
        res = fn(q_ref[...], k_ref[...], v_ref[...], gb_ref[...], *[st_scr[h] for h in range(n_heads)])
        o_ref[...] = res[0]
        for h in range(n_heads):
            st_scr[h] = res[1 + h]

    row = lambda w: pl.BlockSpec((CHUNK, w), lambda i: (i, 0))
    return pl.pallas_call(
        body, name=name, grid=(n,),
        in_specs=[row(gw), row(gw), row(gw), row(LANES)],
        out_specs=[row(gw), pl.BlockSpec((None, n_heads, head_dim, head_dim), lambda i: (i, 0, 0, 0))],
        out_shape=[SDS((s, gw), f32), SDS((n, n_heads, head_dim, head_dim), f32)],
        scratch_shapes=[pltpu.VMEM((n_heads, head_dim, head_dim), f32)],
        compiler_params=_cp(("arbitrary",)),
    )(q, k, v, gb)


def _gdn_chunk_bwd(name, q, k, v, gb, states, d_out, n_heads, head_dim):
    s, gw = q.shape
    n = s // CHUNK
    fn = functools.partial(_gdn_chunk, n_heads, head_dim)

    def body(q_ref, k_ref, v_ref, gb_ref, st_ref, do_ref, dq_ref, dk_ref, dv_ref, dgb_ref, dst_scr):
        @pl.when(pl.program_id(0) == 0)
        def _():
            dst_scr[...] = jnp.zeros_like(dst_scr)

        _, vjp = jax.vjp(fn, q_ref[...], k_ref[...], v_ref[...], gb_ref[...], *[st_ref[h] for h in range(n_heads)])
        grads = vjp((do_ref[...].astype(f32), *[dst_scr[h] for h in range(n_heads)]))
        dq_ref[...] = grads[0]
        dk_ref[...] = grads[1]
        dv_ref[...] = grads[2]
        dgb_ref[...] = grads[3]
        for h in range(n_heads):
            dst_scr[h] = grads[4 + h]

    row = lambda w: pl.BlockSpec((CHUNK, w), lambda i: (n - 1 - i, 0))
    return pl.pallas_call(
        body, name=name, grid=(n,),
        in_specs=[row(gw), row(gw), row(gw), row(LANES),
                  pl.BlockSpec((None, n_heads, head_dim, head_dim), lambda i: (n - 1 - i, 0, 0, 0)), row(gw)],
        out_specs=[row(gw), row(gw), row(gw), row(LANES)],
        out_shape=[SDS((s, gw), f32), SDS((s, gw), f32), SDS((s, gw), f32), SDS((s, LANES), f32)],
        scratch_shapes=[pltpu.VMEM((n_heads, head_dim, head_dim), f32)],
        compiler_params=_cp(("arbitrary",)),
    )(q, k, v, gb, states, d_out)


def _mem_attn(qm, km, vm):
    width = qm.shape[1]
    hd = width // MEM_HEADS
    lane = lax.broadcasted_iota(jnp.int32, (1, width), 1)
    out = jnp.zeros_like(qm)
    for h in range(MEM_HEADS):
        msk = jnp.where((lane >= h * hd) & (lane < (h + 1) * hd), 1.0, 0.0)
        logits = _mm(qm * msk, km, "nt") * (hd ** -0.5)
        p = jnp.exp(logits - jnp.max(logits, axis=-1, keepdims=True))
        p = p / jnp.sum(p, axis=-1, keepdims=True)
        out = out + _mm(p, vm, "nn") * msk
    return (out,)


def _kv_post(n_heads, pk, pv, pf, b_f):
    lane = lax.broadcasted_iota(jnp.int32, pf.shape, 1)
    log_f = jnp.where(lane < n_heads, -_softplus(-(pf + b_f)), 0.0)
    return pk, pv, log_f


def _cumsum(name, xs, reverse):
    s, w = xs[0].shape
    tile = min(CUMSUM_TILE, s)
    n = s // tile

    def body(*refs):
        x_refs, o_ref, carry = refs[:-2], refs[-2], refs[-1]

        @pl.when(pl.program_id(0) == 0)
        def _():
            carry[...] = jnp.zeros_like(carry)

        xv = x_refs[0][...]
        for r in x_refs[1:]:
            xv = xv + r[...]
        ri = lax.broadcasted_iota(jnp.int32, (tile, tile), 0)
        ci = lax.broadcasted_iota(jnp.int32, (tile, tile), 1)
        tri = jnp.where((ri <= ci) if reverse else (ri >= ci), 1.0, 0.0).astype(bf16)
        x1 = xv.astype(bf16)
        r1 = xv - x1.astype(f32)
        x2 = r1.astype(bf16)
        x3 = (r1 - x2.astype(f32)).astype(bf16)
        acc = carry[...] + _dg(tri, x1, "nn") + _dg(tri, x2, "nn") + _dg(tri, x3, "nn")
        o_ref[...] = acc
        carry[...] += jnp.sum(xv, axis=0, keepdims=True)

    order = (lambda i: (n - 1 - i, 0)) if reverse else (lambda i: (i, 0))
    return pl.pallas_call(
        body, name=name, grid=(n,),
        in_specs=[pl.BlockSpec((tile, w), order)] * len(xs), out_specs=pl.BlockSpec((tile, w), order),
        out_shape=SDS((s, w), f32), scratch_shapes=[pltpu.VMEM((1, w), f32)],
        compiler_params=_cp(("arbitrary",)),
    )(*xs)


def _fox_logits(q_ref, k_ref, cr_ref, hh, head_dim, scale, diagonal):
    sl = slice(hh * head_dim, (hh + 1) * head_dim)
    s = _dg(q_ref[:, sl], k_ref[:, sl], "nt") * scale - cr_ref[hh]
    if diagonal:
        tq, tk = s.shape
        ok = lax.broadcasted_iota(jnp.int32, (tq, tk), 1) <= lax.broadcasted_iota(jnp.int32, (tq, tk), 0)
        s = jnp.where(ok, s, NEG_INF)
    return s, sl


def _on_causal_tiles(i, j, fn):
    @pl.when(j < i)
    def _():
        fn(False)

    @pl.when(j == i)
    def _():
        fn(True)


def _fox_fwd(name, q, k, v, crow, n_heads, head_dim):
    s = k.shape[0]
    tq = tk = min(ATT_TILE, s)
    nq, nk = s // tq, s // tk
    scale = head_dim ** -0.5
    hpb = ATT_HEADS
    wb = hpb * head_dim

    def body(q_ref, k_ref, v_ref, cr_ref, o_ref, lse_ref, m_scr, l_scr, acc_scr):
        i, j = pl.program_id(1), pl.program_id(2)

        @pl.when(j == 0)
        def _():
            m_scr[...] = jnp.full_like(m_scr, NEG_INF)
            l_scr[...] = jnp.zeros_like(l_scr)
            acc_scr[...] = jnp.zeros_like(acc_scr)

        def step(diagonal):
            heads = range(hpb)
            sl = [slice(hh * head_dim, (hh + 1) * head_dim) for hh in heads]
            sc = [_fox_logits(q_ref, k_ref, cr_ref, hh, head_dim, scale, diagonal)[0] for hh in heads]
            m_old = [m_scr[hh] for hh in heads]
            m_new = [jnp.maximum(m_old[hh], jnp.max(sc[hh], axis=-1, keepdims=True)) for hh in heads]
            p = [jnp.exp(sc[hh] - m_new[hh]) for hh in heads]
            alpha = [jnp.exp(m_old[hh] - m_new[hh]) for hh in heads]
            pv = [_dg(p[hh], v_ref[:, sl[hh]], "nn") for hh in heads]
            for hh in heads:
                l_scr[hh] = alpha[hh] * l_scr[hh] + jnp.sum(p[hh], axis=-1, keepdims=True)
                acc_scr[:, sl[hh]] = alpha[hh] * acc_scr[:, sl[hh]] + pv[hh]
                m_scr[hh] = m_new[hh]

        _on_causal_tiles(i, j, step)

        @pl.when(j == nk - 1)
        def _():
            for hh in range(hpb):
                sl = slice(hh * head_dim, (hh + 1) * head_dim)
                o_ref[:, sl] = (acc_scr[:, sl] / l_scr[hh]).astype(o_ref.dtype)
                lse_ref[hh] = m_scr[hh] + jnp.log(l_scr[hh])

    return pl.pallas_call(
        body, name=name, grid=(n_heads // hpb, nq, nk),
        in_specs=[
            pl.BlockSpec((tq, wb), lambda g, i, j: (i, g)),
            pl.BlockSpec((tk, wb), lambda g, i, j: (jnp.minimum(j, i), g)),
            pl.BlockSpec((tk, wb), lambda g, i, j: (jnp.minimum(j, i), g)),
            pl.BlockSpec((hpb, 1, tk), lambda g, i, j: (g, 0, jnp.minimum(j, i))),
        ],
        out_specs=[pl.BlockSpec((tq, wb), lambda g, i, j: (i, g)),
                   pl.BlockSpec((hpb, tq, 1), lambda g, i, j: (g, i, 0))],
        out_shape=[SDS((s, n_heads * head_dim), bf16), SDS((n_heads, s, 1), f32)],
        scratch_shapes=[pltpu.VMEM((hpb, tq, 1), f32), pltpu.VMEM((hpb, tq, 1), f32), pltpu.VMEM((tq, wb), f32)],
        compiler_params=_cp(("parallel", "parallel", "arbitrary")),
    )(q, k, v, crow)


def _fox_probs(q_ref, k_ref, v_ref, cr_ref, lse_ref, do_ref, hh, head_dim, scale, diagonal):
    sc, sl = _fox_logits(q_ref, k_ref, cr_ref, hh, head_dim, scale, diagonal)
    return jnp.exp(sc - lse_ref[hh]), _dg(do_ref[:, sl], v_ref[:, sl], "nt"), sl


def _fox_bwd_dq(name, q, k, v, crow, lse, o, do, n_heads, head_dim):
    s = k.shape[0]
    tq = tk = min(ATT_TILE, s)
    nq, nk = s // tq, s // tk
    scale = head_dim ** -0.5
    hpb = ATT_HEADS
    wb = hpb * head_dim

    def body(q_ref, k_ref, v_ref, cr_ref, lse_ref, o_ref, do_ref, dq_ref, delta_ref, dcc_ref, acc_scr):
        i, j = pl.program_id(1), pl.program_id(2)
        heads = range(hpb)

        @pl.when(j == 0)
        def _():
            prod = o_ref[...].astype(f32) * do_ref[...].astype(f32)
            for hh in heads:
                delta_ref[hh] = jnp.sum(prod[:, hh * head_dim:(hh + 1) * head_dim], axis=-1, keepdims=True)
            dcc_ref[...] = jnp.zeros_like(dcc_ref)
            acc_scr[...] = jnp.zeros_like(acc_scr)

        def step(diagonal):
            pd = [_fox_probs(q_ref, k_ref, v_ref, cr_ref, lse_ref, do_ref, hh, head_dim, scale, diagonal) for hh in heads]
            ds = [pd[hh][0] * (pd[hh][1] - delta_ref[hh]) for hh in heads]
            dqs = [_dg(ds[hh], k_ref[:, pd[hh][2]], "nn") for hh in heads]
            for hh in heads:
                dcc_ref[hh] += jnp.sum(ds[hh], axis=-1, keepdims=True)
                acc_scr[:, pd[hh][2]] += dqs[hh]

        _on_causal_tiles(i, j, step)

        @pl.when(j == nk - 1)
        def _():
            dq_ref[...] = (acc_scr[...] * scale).astype(dq_ref.dtype)

    qspec = pl.BlockSpec((tq, wb), lambda g, i, j: (i, g))
    kspec = pl.BlockSpec((tk, wb), lambda g, i, j: (jnp.minimum(j, i), g))
    cspec = pl.BlockSpec((hpb, tq, 1), lambda g, i, j: (g, i, 0))
    return pl.pallas_call(
        body, name=name, grid=(n_heads // hpb, nq, nk),
        in_specs=[qspec, kspec, kspec,
                  pl.BlockSpec((hpb, 1, tk), lambda g, i, j: (g, 0, jnp.minimum(j, i))), cspec, qspec, qspec],
        out_specs=[qspec, cspec, cspec],
        out_shape=[SDS((s, n_heads * head_dim), bf16), SDS((n_heads, s, 1), f32), SDS((n_heads, s, 1), f32)],
        scratch_shapes=[pltpu.VMEM((tq, wb), f32)],
        compiler_params=_cp(("parallel", "parallel", "arbitrary")),
    )(q, k, v, crow, lse, o, do)


def _fox_bwd_dkv(name, q, k, v, crow, lse, delta, do, n_heads, head_dim):
    s = k.shape[0]
    tq = tk = min(ATT_TILE, s)
    nq, nk = s // tq, s // tk
    scale = head_dim ** -0.5
    hpb = ATT_HEADS
    wb = hpb * head_dim

    def body(q_ref, k_ref, v_ref, cr_ref, lse_ref, delta_ref, do_ref, dk_ref, dv_ref, dc_ref):
        j, i = pl.program_id(1), pl.program_id(2)

        @pl.when(i == 0)
        def _():
            dk_ref[...] = jnp.zeros_like(dk_ref)
            dv_ref[...] = jnp.zeros_like(dv_ref)
            dc_ref[...] = jnp.zeros_like(dc_ref)

        def step(diagonal):
            heads = range(hpb)
            pd = [_fox_probs(q_ref, k_ref, v_ref, cr_ref, lse_ref, do_ref, hh, head_dim, scale, diagonal) for hh in heads]
            ds = [pd[hh][0] * (pd[hh][1] - delta_ref[hh]) for hh in heads]
            dvs = [_dg(pd[hh][0], do_ref[:, pd[hh][2]], "tn") for hh in heads]
            dks = [_dg(ds[hh], q_ref[:, pd[hh][2]], "tn") for hh in heads]
            for hh in heads:
                sl = pd[hh][2]
                dv_ref[:, sl] += dvs[hh]
                dk_ref[:, sl] += dks[hh] * scale
                dc_ref[hh] -= jnp.sum(ds[hh], axis=0, keepdims=True)

        _on_causal_tiles(i, j, step)

    qspec = pl.BlockSpec((tq, wb), lambda g, j, i: (jnp.maximum(i, j), g))
    kspec = pl.BlockSpec((tk, wb), lambda g, j, i: (j, g))
    cspec = pl.BlockSpec((hpb, tq, 1), lambda g, j, i: (g, jnp.maximum(i, j), 0))
    rspec = pl.BlockSpec((hpb, 1, tk), lambda g, j, i: (g, 0, j))
    return pl.pallas_call(
        body, name=name, grid=(n_heads // hpb, nk, nq),
        in_specs=[qspec, kspec, kspec, rspec, cspec, cspec, qspec],
        out_specs=[kspec, kspec, rspec],
        out_shape=[SDS((s, n_heads * head_dim), f32), SDS((s, n_heads * head_dim), f32), SDS((n_heads, 1, s), f32)],
        compiler_params=_cp(("parallel", "parallel", "arbitrary")),
    )(q, k, v, crow, lse, delta, do)


def _final_loss(name, h, target, gain):
    s, d = h.shape
    tile = min(ROW_TILE, s)

    def body(h_ref, t_ref, g_ref, loss_ref, dh_ref, dg_ref):
        @pl.when(pl.program_id(0) == 0)
        def _():
            loss_ref[...] = jnp.zeros_like(loss_ref)
            dg_ref[...] = jnp.zeros_like(dg_ref)

        x, g = h_ref[...], g_ref[...]
        rstd = lax.rsqrt(jnp.mean(x * x, axis=-1, keepdims=True) + EPS)
        xhat = x * rstd
        err = xhat * g - t_ref[...]
        row = jnp.sum(err * err, axis=-1, keepdims=True) * (0.5 / d)
        loss_ref[...] += jnp.sum(row, axis=0, keepdims=True)
        dy = err * (1.0 / d)
        dg_ref[...] += jnp.sum(dy * xhat, axis=0, keepdims=True)
        dxhat = dy * g
        dh_ref[...] = rstd * (dxhat - xhat * jnp.mean(dxhat * xhat, axis=-1, keepdims=True))

    return pl.pallas_call(
        body, name=name, grid=(s // tile,),
        in_specs=[pl.BlockSpec((tile, d), lambda i: (i, 0)), pl.BlockSpec((tile, d), lambda i: (i, 0)),
                  pl.BlockSpec((1, d), lambda i: (0, 0))],
        out_specs=[pl.BlockSpec((1, LANES), lambda i: (0, 0)), pl.BlockSpec((tile, d), lambda i: (i, 0)),
                   pl.BlockSpec((1, d), lambda i: (0, 0))],
        out_shape=[SDS((1, LANES), f32), SDS((s, d), f32), SDS((1, d), f32)],
        compiler_params=_cp(("arbitrary",)),
    )(h, target, gain)


def _adamw(name, parts, w, m, v):
    r, c = w.shape
    rb = r
    for cand in (256, 128, 64, 32, 16):
        if r % cand == 0 and cand * c * 4 <= (2 << 20):
            rb = cand
            break
    n_parts = parts.shape[0]

    def body(p_ref, w_ref, m_ref, v_ref, g_out, d_out, m_out, v_out):
        g = p_ref[0].astype(f32)
        for k in range(1, n_parts):
            g = g + p_ref[k].astype(f32)
        m_new = ADAM_B1 * m_ref[...] + (1.0 - ADAM_B1) * g
        v_new = ADAM_B2 * v_ref[...] + (1.0 - ADAM_B2) * (g * g)
        m_hat = m_new / (1.0 - ADAM_B1 ** ADAM_STEP)
        v_hat = v_new / (1.0 - ADAM_B2 ** ADAM_STEP)
        g_out[...] = g
        d_out[...] = -ADAM_LR * (m_hat / (jnp.sqrt(v_hat) + ADAM_EPS) + ADAM_WD * w_ref[...])
        m_out[...] = m_new
        v_out[...] = v_new

    blk = pl.BlockSpec((rb, c), lambda i: (i, 0))
    return pl.pallas_call(
        body, name=name, grid=(r // rb,),
        in_specs=[pl.BlockSpec((n_parts, rb, c), lambda i: (0, i, 0)), blk, blk, blk],
        out_specs=[blk, blk, blk, blk],
        out_shape=[SDS((r, c), f32)] * 4,
        compiler_params=_cp(("parallel",)),
    )(parts, w, m, v)


def _position():
    x, y, c = lax.axis_index("x"), lax.axis_index("y"), lax.axis_index("c")
    return x, y, c


def _all_gather(name, shards):
    n = len(shards)

    def body(*refs):
        ins, outs = refs[:n], refs[n:2 * n]
        send_sems, recv_sems, local_sems = refs[2 * n:]
        x, y, c = _position()
        me, sibling = (x, y, c), (x, y, 1 - c)
        chips = [(1 - x, y), (x, 1 - y), (1 - x, 1 - y)]

        def slot(a, block):
            px, py, pc = block
            return outs[a].at[4 * px + 2 * py + pc]

        def copy(a, k, block, to, src=None):
            return pltpu.make_async_remote_copy(
                src_ref=slot(a, block) if src is None else src, dst_ref=slot(a, block),
                send_sem=send_sems.at[a, k], recv_sem=recv_sems.at[a, k], device_id=to, device_id_type=MESH)

        local = [pltpu.make_async_copy(ins[a], slot(a, me), local_sems.at[a]) for a in range(n)]
        for cp in local:
            cp.start()
        started = []
        for a in range(n):
            first = [copy(a, 0, me, sibling, src=ins[a])]
            first += [copy(a, 1 + j, me, (*chip, c), src=ins[a]) for j, chip in enumerate(chips)]
            for cp in first:
                cp.start()
            started += first
        for a in range(n):
            for j, chip in enumerate(chips):
                copy(a, 1 + j, (*chip, c), me).wait_recv()
                passed = copy(a, 4 + j, (*chip, c), sibling)
                passed.start()
                started.append(passed)
        for a in range(n):
            copy(a, 0, sibling, me).wait_recv()
            for j, chip in enumerate(chips):
                copy(a, 4 + j, (*chip, 1 - c), me).wait_recv()
        for cp in started:
            cp.wait_send()
        for cp in local:
            cp.wait()

    any_spec = pl.BlockSpec(memory_space=pl.ANY)
    outs = pl.pallas_call(
        body, name=name,
        in_specs=[any_spec] * n, out_specs=[any_spec] * n,
        out_shape=[SDS((N_DEV, *a.shape), a.dtype) for a in shards],
        scratch_shapes=[pltpu.SemaphoreType.DMA((n, 7)), pltpu.SemaphoreType.DMA((n, 7)), pltpu.SemaphoreType.DMA((n,))],
    )(*shards)
    return list(outs)


N_CHIPS = N_DEV // 2


def _sibling_exchange(name, stacks):
    n = len(stacks)

    def body(*refs):
        ins, mine, theirs = refs[:n], refs[n:2 * n], refs[2 * n:3 * n]
        send_sems, recv_sems, local_sems = refs[3 * n:]
        x, y, c = _position()
        copies = []
        for a in range(n):
            for q in range(N_CHIPS):
                local = pltpu.make_async_copy(ins[a].at[2 * q + c], mine[a].at[q], local_sems.at[a, q])
                remote = pltpu.make_async_remote_copy(
                    src_ref=ins[a].at[2 * q + 1 - c], dst_ref=theirs[a].at[q],
                    send_sem=send_sems.at[a, q], recv_sem=recv_sems.at[a, q],
                    device_id=(x, y, 1 - c), device_id_type=MESH)
                local.start()
                remote.start()
                copies.append((local, remote))
        for local, remote in copies:
            remote.wait_recv()
        for local, remote in copies:
            remote.wait_send()
            local.wait()

    any_spec = pl.BlockSpec(memory_space=pl.ANY)
    half = [SDS((N_CHIPS, *a.shape[1:]), a.dtype) for a in stacks]
    outs = pl.pallas_call(
        body, name=name,
        in_specs=[any_spec] * n, out_specs=[any_spec] * (2 * n), out_shape=half + half,
        scratch_shapes=[pltpu.SemaphoreType.DMA((n, N_CHIPS)), pltpu.SemaphoreType.DMA((n, N_CHIPS)),
                        pltpu.SemaphoreType.DMA((n, N_CHIPS))],
    )(*stacks)
    return list(outs[:n]), list(outs[n:])


def _chip_exchange(name, parts):
    n = len(parts)

    def body(*refs):
        ins, outs = refs[:n], refs[n:2 * n]
        send_sems, recv_sems, local_sems = refs[2 * n:]
        x, y, c = _position()
        my_chip = 2 * x + y
        local = [pltpu.make_async_copy(ins[a].at[my_chip], outs[a].at[my_chip], local_sems.at[a]) for a in range(n)]
        for cp in local:
            cp.start()
        sends, recvs = [], []
        for a in range(n):
            for k in range(1, N_CHIPS):
                px = (1 - x) if (k >> 1) & 1 else x
                py = (1 - y) if k & 1 else y
                peer_chip = 2 * px + py
                send = pltpu.make_async_remote_copy(
                    src_ref=ins[a].at[peer_chip], dst_ref=outs[a].at[my_chip],
                    send_sem=send_sems.at[a, k - 1], recv_sem=recv_sems.at[a, k - 1],
                    device_id=(px, py, c), device_id_type=MESH)
                send.start()
                sends.append(send)
                recvs.append(pltpu.make_async_remote_copy(
                    src_ref=ins[a].at[peer_chip], dst_ref=outs[a].at[peer_chip],
                    send_sem=send_sems.at[a, k - 1], recv_sem=recv_sems.at[a, k - 1],
                    device_id=(px, py, c), device_id_type=MESH))
        for cp in recvs:
            cp.wait_recv()
        for cp in sends:
            cp.wait_send()
        for cp in local:
            cp.wait()

    any_spec = pl.BlockSpec(memory_space=pl.ANY)
    outs = pl.pallas_call(
        body, name=name,
        in_specs=[any_spec] * n, out_specs=[any_spec] * n,
        out_shape=[SDS(a.shape, a.dtype) for a in parts],
        scratch_shapes=[pltpu.SemaphoreType.DMA((n, N_CHIPS - 1)), pltpu.SemaphoreType.DMA((n, N_CHIPS - 1)),
                        pltpu.SemaphoreType.DMA((n,))],
    )(*parts)
    return list(outs)


def _reduce_exchange(stacks):
    mine, theirs = _sibling_exchange("exchange_sibling", stacks)
    summed = []
    for a, (m, t) in enumerate(zip(mine, theirs)):
        cols = m.shape[-1]
        (both,) = _rowwise(f"exchange_add_{a}", lambda p, q: (p + q,), [m.reshape(-1, cols), t.reshape(-1, cols)], [],
                           [(cols, m.dtype)])
        summed.append(both.reshape(m.shape))
    return _chip_exchange("exchange_chips", summed)


def _rows_from_shards(g):
    n, l, r, c = g.shape
    return g.transpose(1, 0, 2, 3).reshape(l, n * r, c)


def _rows_to_shards(w):
    l, rows, c = w.shape
    return w.reshape(l, N_DEV, rows // N_DEV, c).transpose(1, 0, 2, 3)


def _pad_lanes(a, width):
    return jnp.pad(a, [(0, 0)] * (a.ndim - 1) + [(0, width - a.shape[-1])])


def _row(vec, width=None):
    vec = vec.reshape(1, -1)
    return vec if width is None else _pad_lanes(vec, width)


class _SmallPack:
    def __init__(self, shapes):
        self.shapes, self.offsets, at = shapes, {}, 0
        for name, shape in shapes.items():
            last = shape[-1]
            lead = int(math.prod(shape[:-1]))
            rows = lead * (last // LANES) if last >= LANES else lead
            self.offsets[name] = (at, rows)
            at += rows
        self.rows = -(-at // 8) * 8

    def pack(self, values):
        pieces = []
        for name, shape in self.shapes.items():
            val = values[name].astype(f32)
            if shape[-1] >= LANES:
                pieces.append(val.reshape(-1, LANES))
            else:
                pieces.append(_pad_lanes(val.reshape(-1, shape[-1]), LANES))
        used = sum(p.shape[0] for p in pieces)
        if used < self.rows:
            pieces.append(jnp.zeros((self.rows - used, LANES), f32))
        return jnp.concatenate(pieces, axis=0)

    def unpack(self, packed):
        out = {}
        for name, shape in self.shapes.items():
            at, rows = self.offsets[name]
            blk = packed[at:at + rows]
            out[name] = blk.reshape(shape) if shape[-1] >= LANES else blk[:, :shape[-1]].reshape(shape)
        return out


def kernel(x, mem, ffn1_norm, ffn1_w_gate_up, ffn1_w_down, mix_norm, ffn2_norm, ffn2_w_gate_up, ffn2_w_down, gdn_w_in, gdn_conv, gdn_A_log, gdn_dt_bias, gdn_out_norm, fox_w_in, w_out, mem_norm, mem_w_kv, kv_norm, kv_w, kv_b_f, final_norm, loss_target, m_ffn1_norm, m_ffn1_w_gate_up, m_ffn1_w_down, m_mix_norm, m_ffn2_norm, m_ffn2_w_gate_up, m_ffn2_w_down, m_gdn_w_in, m_gdn_conv, m_gdn_A_log, m_gdn_dt_bias, m_gdn_out_norm, m_fox_w_in, m_w_out, m_mem_norm, m_mem_w_kv, m_kv_norm, m_kv_w, m_kv_b_f, m_final_norm, v_ffn1_norm, v_ffn1_w_gate_up, v_ffn1_w_down, v_mix_norm, v_ffn2_norm, v_ffn2_w_gate_up, v_ffn2_w_down, v_gdn_w_in, v_gdn_conv, v_gdn_A_log, v_gdn_dt_bias, v_gdn_out_norm, v_fox_w_in, v_w_out, v_mem_norm, v_mem_w_kv, v_kv_norm, v_kv_w, v_kv_b_f, v_final_norm):
    weights = dict(ffn1_norm=ffn1_norm, ffn1_w_gate_up=ffn1_w_gate_up, ffn1_w_down=ffn1_w_down, mix_norm=mix_norm,
                   ffn2_norm=ffn2_norm, ffn2_w_gate_up=ffn2_w_gate_up, ffn2_w_down=ffn2_w_down, gdn_w_in=gdn_w_in,
                   gdn_conv=gdn_conv, gdn_A_log=gdn_A_log, gdn_dt_bias=gdn_dt_bias, gdn_out_norm=gdn_out_norm,
                   fox_w_in=fox_w_in, w_out=w_out, mem_norm=mem_norm, mem_w_kv=mem_w_kv, kv_norm=kv_norm, kv_w=kv_w,
                   kv_b_f=kv_b_f, final_norm=final_norm)
    mom_m = dict(ffn1_norm=m_ffn1_norm, ffn1_w_gate_up=m_ffn1_w_gate_up, ffn1_w_down=m_ffn1_w_down, mix_norm=m_mix_norm,
                 ffn2_norm=m_ffn2_norm, ffn2_w_gate_up=m_ffn2_w_gate_up, ffn2_w_down=m_ffn2_w_down, gdn_w_in=m_gdn_w_in,
                 gdn_conv=m_gdn_conv, gdn_A_log=m_gdn_A_log, gdn_dt_bias=m_gdn_dt_bias, gdn_out_norm=m_gdn_out_norm,
                 fox_w_in=m_fox_w_in, w_out=m_w_out, mem_norm=m_mem_norm, mem_w_kv=m_mem_w_kv, kv_norm=m_kv_norm,
                 kv_w=m_kv_w, kv_b_f=m_kv_b_f, final_norm=m_final_norm)
    mom_v = dict(ffn1_norm=v_ffn1_norm, ffn1_w_gate_up=v_ffn1_w_gate_up, ffn1_w_down=v_ffn1_w_down, mix_norm=v_mix_norm,
                 ffn2_norm=v_ffn2_norm, ffn2_w_gate_up=v_ffn2_w_gate_up, ffn2_w_down=v_ffn2_w_down, gdn_w_in=v_gdn_w_in,
                 gdn_conv=v_gdn_conv, gdn_A_log=v_gdn_A_log, gdn_dt_bias=v_gdn_dt_bias, gdn_out_norm=v_gdn_out_norm,
                 fox_w_in=v_fox_w_in, w_out=v_w_out, mem_norm=v_mem_norm, mem_w_kv=v_mem_w_kv, kv_norm=v_kv_norm,
                 kv_w=v_kv_w, kv_b_f=v_kv_b_f, final_norm=v_final_norm)
    names = list(weights)
    small_names = [n for n in names if weights[n].shape == mom_m[n].shape and n in (
        "ffn1_norm", "mix_norm", "ffn2_norm", "gdn_A_log", "gdn_dt_bias", "gdn_out_norm", "mem_norm", "kv_norm",
        "kv_b_f", "final_norm")]
    big_names = [n for n in names if n not in small_names]

    h = x[0]
    target = loss_target[0]
    mem_tokens = mem[0]
    s, d = h.shape
    depth = ffn1_norm.shape[0]
    n_a = gdn_w_in.shape[0]
    n_heads, head_dim = gdn_A_log.shape[1], gdn_out_norm.shape[1]
    gw = n_heads * head_dim
    a_in = gdn_w_in.shape[2]
    mem_w = a_in - 4 * gw - 2 * n_heads
    a_in_pad = 4 * gw + mem_w + LANES
    kv_width = kv_w.shape[1]
    kv_pad = 2 * gw + LANES
    fh = ffn1_w_down.shape[1] * N_DEV

    def permute_in(w):
        ab = w[..., 4 * gw:4 * gw + 2 * n_heads]
        return jnp.concatenate([w[..., :4 * gw], w[..., 4 * gw + 2 * n_heads:], _pad_lanes(ab, LANES)], axis=-1)

    def unpermute_in(w):
        return jnp.concatenate([w[..., :4 * gw], w[..., 4 * gw + mem_w:4 * gw + mem_w + 2 * n_heads],
                                w[..., 4 * gw:4 * gw + mem_w]], axis=-1)

    gathered = _all_gather("gather_weights", [
        ffn1_w_gate_up.astype(bf16), ffn1_w_down.astype(bf16), ffn2_w_gate_up.astype(bf16), ffn2_w_down.astype(bf16),
        permute_in(gdn_w_in).astype(bf16), fox_w_in.astype(bf16), w_out.astype(bf16), mem_w_kv.astype(bf16),
        _pad_lanes(kv_w, kv_pad).astype(bf16)[None], gdn_conv])
    wgu1, wd1_s, wgu2, wd2_s, win_s, wfox_s, wout_s, wmem_s, wkv_s, conv_s = gathered
    wd1, wd2 = _rows_from_shards(wd1_s), _rows_from_shards(wd2_s)
    win, wfox, wout = _rows_from_shards(win_s), _rows_from_shards(wfox_s), _rows_from_shards(wout_s)
    wmem = _rows_from_shards(wmem_s)
    wmem_cat = wmem.transpose(1, 0, 2).reshape(d, depth * 2 * mem_w)
    wkv = _rows_from_shards(wkv_s)[0]
    conv_w = conv_s.transpose(1, 2, 0, 3).reshape(n_a, gdn_conv.shape[1], 3 * gw)

    a_log_rows = [_row(gdn_A_log[l], LANES) for l in range(n_a)]
    dt_rows = [_row(gdn_dt_bias[l], LANES) for l in range(n_a)]
    onorm_rows = [_row(gdn_out_norm[l]) for l in range(n_a)]
    b_f_row = _row(kv_b_f, LANES)

    (mem_n,) = _rowwise("mem_norm", lambda t, g: (_rms(t, g),), [mem_tokens], [_row(mem_norm)], [(d, bf16)])
    mem_kv = _matmul("mem_kv", mem_n, wmem_cat, "nn", f32)

    saved = []
    shared = None
    for l in range(depth):
        rec = {"h0": h}
        h1, gate1, up1 = _ffn_fwd(f"ffn1_fwd_{l}", h, _row(ffn1_norm[l]), wgu1, wd1, l)
        (u,) = _rowwise(f"mix_norm_{l}", lambda t, g: (_rms(t, g),), [h1], [_row(mix_norm[l])], [(d, bf16)])
        rec.update(h1=h1, u=u, ffn1=(gate1, up1))
        if l < n_a:
            proj = _matmul(f"gdn_in_{l}", u, win[l], "nn", f32)
            yc = _conv_fwd(f"conv_fwd_{l}", V(proj, cb=0, w=3 * gw), conv_w[l])
            ab_view = V(proj, cb=(4 * gw + mem_w) // LANES, w=LANES)
            q, k, v, gb = _rowwise(f"gdn_pre_{l}", functools.partial(_gdn_pre, n_heads, head_dim),
                                   [yc, ab_view], [a_log_rows[l], dt_rows[l]],
                                   [(gw, f32), (gw, f32), (gw, f32), (LANES, f32)])
            o, states = _gdn_chunk_fwd(f"gdn_chunk_fwd_{l}", q, k, v, gb, n_heads, head_dim)
            z_view = V(proj, cb=3, w=gw)
            (main,) = _rowwise(f"gdn_post_{l}", functools.partial(_gdn_post, n_heads, head_dim),
                               [o, z_view], [onorm_rows[l]], [(gw, bf16)])
            qmem_view = V(proj, cb=4 * gw // mem_w, w=mem_w)
            rec.update(proj=proj, yc=yc, q=q, k=k, v=v, gb=gb, o=o, states=states)
        else:
            proj = _matmul(f"fox_in_{l}", u, wfox[l - n_a], "nn", bf16)
            sk, sv, crow = shared["k"], shared["v"], shared["crow"]
            main, lse = _fox_fwd(f"fox_fwd_{l}", proj, sk, sv, crow, n_heads, head_dim)
            qmem_view = V(proj, cb=gw // mem_w, w=mem_w)
            rec.update(proj=proj, lse=lse)
        km = V(mem_kv, cb=2 * l, w=mem_w)
        vm = V(mem_kv, cb=2 * l + 1, w=mem_w)
        (mem_out,) = _rowwise(f"mem_attn_{l}", _mem_attn, [qmem_view], [km, vm], [(mem_w, bf16)])
        cat = jnp.concatenate([main, mem_out], axis=1)
        h2 = _matmul(f"out_proj_{l}", cat, wout[l], "nn", f32, add=h1)
        h3, gate2, up2 = _ffn_fwd(f"ffn2_fwd_{l}", h2, _row(ffn2_norm[l]), wgu2, wd2, l)
        rec.update(cat=cat, h2=h2, qmem=qmem_view, ffn2=(gate2, up2))
        saved.append(rec)
        h = h3
        if l == n_a - 1:
            (hn,) = _rowwise("kv_norm", lambda t, g: (_rms(t, g),), [h], [_row(kv_norm)], [(d, bf16)])
            p = _matmul("kv_proj", hn, wkv, "nn", f32)
            pk, pv, pf = V(p, cb=0, w=gw), V(p, cb=1, w=gw), V(p, cb=2 * gw // LANES, w=LANES)
            sk, sv, log_f = _rowwise("kv_post", functools.partial(_kv_post, n_heads), [pk, pv, pf], [b_f_row],
                                     [(gw, bf16), (gw, bf16), (LANES, f32)])
            cum = _cumsum("forget_cumsum", [log_f], reverse=False)
            c_heads = cum[:, :n_heads].T
            shared = dict(k=sk, v=sv, crow=c_heads.reshape(n_heads, 1, s), h=h, hn=hn, p=p, views=(pk, pv, pf))

    loss_part, dh, d_final = _final_loss("final_loss", h, target, _row(final_norm))
    loss = lax.psum(loss_part[0, 0], ("x", "y", "c"))

    grads = {}
    per_layer = {n: [None] * depth for n in ("ffn1_norm", "mix_norm", "ffn2_norm", "ffn1_gu", "ffn1_d", "ffn2_gu",
                                             "ffn2_d", "w_out")}
    per_a = {n: [None] * n_a for n in ("gdn_w_in", "gdn_conv", "gdn_A_log", "gdn_dt_bias", "gdn_out_norm")}
    per_b = {"fox_w_in": [None] * (depth - n_a)}
    d_mem_kv = [None] * depth
    fox_grads = []

    def ffn_backward(tag, l, h_in, d_out, gain, kept, wgu, wd):
        parts, dwg, dwu, dwd = _ffn_bwd(f"{tag}_bwd_{l}", h_in, d_out, _row(gain), kept[0], kept[1], wgu, wd, l)
        nh = parts.shape[0]
        (d_in,), (d_gain,) = _rowwise_vjp(
            f"{tag}_norm_bwd_{l}", lambda t, g: (_rms(t, g),), [h_in], [_row(gain)],
            [[V(parts, lead=t) for t in range(nh)]], [f32], add=d_out)
        return d_in, d_gain, jnp.concatenate([dwg, dwu], axis=0), dwd

    for l in reversed(range(depth)):
        rec = saved[l]
        if l == n_a - 1:
            dk_list = [V(g["dk"]) for g in fox_grads]
            dv_list = [V(g["dv"]) for g in fox_grads]
            dc_parts = [_pad_lanes(part.reshape(n_heads, s).T, LANES) for g in fox_grads for part in g["dc"]]
            d_log_f = _cumsum("forget_cumsum_bwd", dc_parts, reverse=True)
            pk, pv, pf = shared["views"]
            (dpk, dpv, dpf), (d_bf,) = _rowwise_vjp(
                "kv_post_bwd", functools.partial(_kv_post, n_heads), [pk, pv, pf], [b_f_row],
                [dk_list, dv_list, [d_log_f]], [bf16, bf16, bf16])
            dp = jnp.concatenate([dpk, dpv, dpf], axis=1)
            d_hn = _matmul("kv_proj_dx", dp, wkv, "nt", f32)
            grads["kv_w"] = _matmul("kv_proj_dw", shared["hn"], dp, "tn", f32)[:, :kv_width]
            (dh,), (d_kvn,) = _rowwise_vjp("kv_norm_bwd", lambda t, g: (_rms(t, g),), [shared["h"]], [_row(kv_norm)],
                                           [[d_hn]], [f32], add=dh)
            grads["kv_norm"] = d_kvn.reshape(-1)
            grads["kv_b_f"] = d_bf[0, :n_heads]

        dh2, per_layer["ffn2_norm"][l], per_layer["ffn2_gu"][l], per_layer["ffn2_d"][l] = ffn_backward(
            "ffn2", l, rec["h2"], dh, ffn2_norm[l], rec["ffn2"], wgu2, wd2)
        d_cat = _matmul(f"out_proj_dx_{l}", dh2, wout[l], "nt", f32)
        per_layer["w_out"][l] = _matmul(f"out_proj_dw_{l}", rec["cat"], dh2, "tn", f32)
        d_main = V(d_cat, cb=0, w=gw)
        d_memo = V(d_cat, cb=gw // mem_w, w=mem_w)
        km, vm = V(mem_kv, cb=2 * l, w=mem_w), V(mem_kv, cb=2 * l + 1, w=mem_w)
        (dqmem,), (dkm, dvm) = _rowwise_vjp(f"mem_attn_bwd_{l}", _mem_attn, [rec["qmem"]], [km, vm], [[d_memo]], [bf16])
        d_mem_kv[l] = jnp.concatenate([dkm, dvm], axis=1)
        if l < n_a:
            proj = rec["proj"]
            z_view = V(proj, cb=3, w=gw)
            (d_o, d_z), (d_onorm,) = _rowwise_vjp(
                f"gdn_post_bwd_{l}", functools.partial(_gdn_post, n_heads, head_dim), [rec["o"], z_view],
                [onorm_rows[l]], [[d_main]], [f32, bf16])
            dq, dk, dv, dgb = _gdn_chunk_bwd(f"gdn_chunk_bwd_{l}", rec["q"], rec["k"], rec["v"], rec["gb"],
                                             rec["states"], d_o, n_heads, head_dim)
            ab_view = V(proj, cb=(4 * gw + mem_w) // LANES, w=LANES)
            (d_yc, d_ab), (d_alog, d_dt) = _rowwise_vjp(
                f"gdn_pre_bwd_{l}", functools.partial(_gdn_pre, n_heads, head_dim), [rec["yc"], ab_view],
                [a_log_rows[l], dt_rows[l]], [[dq], [dk], [dv], [dgb]], [f32, bf16])
            d_qkv, d_conv = _conv_bwd(f"conv_bwd_{l}", V(proj, cb=0, w=3 * gw), d_yc, conv_w[l])
            d_proj = jnp.concatenate([d_qkv, d_z, dqmem, d_ab], axis=1)
            du = _matmul(f"gdn_in_dx_{l}", d_proj, win[l], "nt", f32)
            per_a["gdn_w_in"][l] = unpermute_in(_matmul(f"gdn_in_dw_{l}", rec["u"], d_proj, "tn", f32))
            per_a["gdn_conv"][l] = d_conv
            per_a["gdn_A_log"][l] = d_alog[0, :n_heads]
            per_a["gdn_dt_bias"][l] = d_dt[0, :n_heads]
            per_a["gdn_out_norm"][l] = d_onorm[0]
        else:
            proj = rec["proj"]
            sk, sv, crow = shared["k"], shared["v"], shared["crow"]
            dq, delta, dc_col = _fox_bwd_dq(f"fox_dq_{l}", proj, sk, sv, crow, rec["lse"], rec["cat"], d_cat,
                                            n_heads, head_dim)
            dk, dv, dc_row = _fox_bwd_dkv(f"fox_dkv_{l}", proj, sk, sv, crow, rec["lse"], delta, d_cat,
                                          n_heads, head_dim)
            fox_grads.append(dict(dk=dk, dv=dv, dc=(dc_row, dc_col)))
            d_proj = jnp.concatenate([dq, dqmem], axis=1)
            du = _matmul(f"fox_in_dx_{l}", d_proj, wfox[l - n_a], "nt", f32)
            per_b["fox_w_in"][l - n_a] = _matmul(f"fox_in_dw_{l}", rec["u"], d_proj, "tn", f32)
        (dh1,), (d_mix,) = _rowwise_vjp(f"mix_norm_bwd_{l}", lambda t, g: (_rms(t, g),), [rec["h1"]],
                                        [_row(mix_norm[l])], [[du]], [f32], add=dh2)
        per_layer["mix_norm"][l] = d_mix
        dh, per_layer["ffn1_norm"][l], per_layer["ffn1_gu"][l], per_layer["ffn1_d"][l] = ffn_backward(
            "ffn1", l, rec["h0"], dh1, ffn1_norm[l], rec["ffn1"], wgu1, wd1)

    grad_x = dh[None]

    d_mem_kv_cat = jnp.concatenate(d_mem_kv, axis=1)
    d_wmem_cat = _matmul("mem_kv_dw", mem_n, d_mem_kv_cat, "tn", f32)
    d_mem_n = _matmul("mem_kv_dx", d_mem_kv_cat, wmem_cat, "nt", f32)
    _, (d_memnorm,) = _rowwise_vjp("mem_norm_bwd", lambda t, g: (_rms(t, g),), [mem_tokens], [_row(mem_norm)],
                                   [[d_mem_n]], [None])

    def gu_stack(per):
        return jnp.stack(per, axis=1).astype(bf16)

    stacks = dict(
        ffn1_w_gate_up=gu_stack(per_layer["ffn1_gu"]),
        ffn1_w_down=_rows_to_shards(jnp.stack(per_layer["ffn1_d"])).astype(bf16),
        ffn2_w_gate_up=gu_stack(per_layer["ffn2_gu"]),
        ffn2_w_down=_rows_to_shards(jnp.stack(per_layer["ffn2_d"])).astype(bf16),
        gdn_w_in=_rows_to_shards(jnp.stack(per_a["gdn_w_in"])).astype(bf16),
        gdn_conv=jnp.stack(per_a["gdn_conv"]).reshape(n_a, -1, N_DEV, 3 * gw // N_DEV).transpose(2, 0, 1, 3),
        fox_w_in=_rows_to_shards(jnp.stack(per_b["fox_w_in"])).astype(bf16),
        w_out=_rows_to_shards(jnp.stack(per_layer["w_out"])).astype(bf16),
        mem_w_kv=_rows_to_shards(d_wmem_cat.reshape(d, depth, 2 * mem_w).transpose(1, 0, 2)).astype(bf16),
        kv_w=_rows_to_shards(grads["kv_w"][None])[:, 0].astype(bf16),
    )
    received = dict(zip(big_names, _reduce_exchange([stacks[n] for n in big_names])))

    small_shapes = {n: weights[n].shape for n in small_names}
    pack = _SmallPack(small_shapes)
    small_grads = dict(
        ffn1_norm=jnp.concatenate(per_layer["ffn1_norm"], axis=0), mix_norm=jnp.concatenate(per_layer["mix_norm"], axis=0),
        ffn2_norm=jnp.concatenate(per_layer["ffn2_norm"], axis=0), gdn_A_log=jnp.stack(per_a["gdn_A_log"]),
        gdn_dt_bias=jnp.stack(per_a["gdn_dt_bias"]), gdn_out_norm=jnp.stack(per_a["gdn_out_norm"]),
        mem_norm=d_memnorm.reshape(-1), kv_norm=grads["kv_norm"], kv_b_f=grads["kv_b_f"], final_norm=d_final.reshape(-1))
    (small_parts,) = _all_gather("gather_small_grads", [pack.pack(small_grads)])

    out_g, out_d, out_m, out_v = {}, {}, {}, {}
    for n in big_names:
        shape = weights[n].shape
        c = shape[-1]
        parts = received[n].reshape(N_CHIPS, -1, c)
        res = _adamw(f"adamw_{n}", parts, weights[n].reshape(-1, c), mom_m[n].reshape(-1, c), mom_v[n].reshape(-1, c))
        out_g[n], out_d[n], out_m[n], out_v[n] = [r.reshape(shape) for r in res]
    res = _adamw("adamw_small", small_parts, pack.pack({n: weights[n] for n in small_names}),
                 pack.pack({n: mom_m[n] for n in small_names}), pack.pack({n: mom_v[n] for n in small_names}))
    for dst, packed in zip((out_g, out_d, out_m, out_v), res):
        dst.update(pack.unpack(packed))

    return (loss, grad_x, *[out_g[n] for n in names], *[out_d[n] for n in names],
            *[out_m[n] for n in names], *[out_v[n] for n in names])
```

```python
import functools
import math

import jax
import jax.numpy as jnp
from jax import lax
from jax.experimental import pallas as pl
from jax.experimental.pallas import tpu as pltpu

f32 = jnp.float32
bf16 = jnp.bfloat16
SDS = jax.ShapeDtypeStruct
HIGHEST = lax.Precision.HIGHEST

N_DEV = 8
MEM_HEADS = 4
CHUNK = 64
LANES = 128
EPS = 1e-6
NEG_INF = -1e30
ADAM_LR = 0.001
ADAM_B1 = 0.9
ADAM_B2 = 0.999
ADAM_EPS = 1e-08
ADAM_WD = 0.01
ADAM_STEP = 10

ROW_TILE = 256
MM_TILE = 512
FFN_FWD_TILE = 512
FFN_BWD_TILE = 256
ATT_TILE = 512
ATT_HEADS = 3
CUMSUM_TILE = 256
VMEM_LIMIT = 56 * 1024 * 1024

MESH = pl.DeviceIdType.MESH


def _cp(sem=None):
    return pltpu.CompilerParams(dimension_semantics=sem, vmem_limit_bytes=VMEM_LIMIT)


_DIMS = {"nn": (((1,), (0,)), ((), ())), "nt": (((1,), (1,)), ((), ())), "tn": (((0,), (0,)), ((), ()))}


def _dg(a, b, mode):
    return lax.dot_general(a.astype(bf16), b.astype(bf16), _DIMS[mode], preferred_element_type=f32)


@functools.partial(jax.custom_vjp, nondiff_argnums=(2,))
def _mm(a, b, mode):
    return _dg(a, b, mode)


def _mm_fwd(a, b, mode):
    return _dg(a, b, mode), (a, b)


def _mm_bwd(mode, res, ct):
    a, b = res
    if mode == "nn":
        da, db = _dg(ct, b, "nt"), _dg(a, ct, "tn")
    elif mode == "nt":
        da, db = _dg(ct, b, "nn"), _dg(ct, a, "tn")
    else:
        da, db = _dg(b, ct, "nt"), _dg(a, ct, "nn")
    return da.astype(a.dtype), db.astype(b.dtype)


_mm.defvjp(_mm_fwd, _mm_bwd)


def _split_bf16(x):
    hi = x.astype(bf16)
    return hi, (x - hi.astype(f32)).astype(bf16)


def _dgh(a, b, mode="nn"):
    a_hi, a_lo = _split_bf16(a)
    b_hi, b_lo = _split_bf16(b)
    dims = _DIMS[mode]
    return (lax.dot_general(a_hi, b_hi, dims, preferred_element_type=f32)
            + lax.dot_general(a_hi, b_lo, dims, preferred_element_type=f32)
            + lax.dot_general(a_lo, b_hi, dims, preferred_element_type=f32))


@jax.custom_vjp
def _unit_lower_inverses(lows):
    c = lows[0].shape[0]
    ri = lax.broadcasted_iota(jnp.int32, (c, c), 0)
    ci = lax.broadcasted_iota(jnp.int32, (c, c), 1)
    eye = jnp.where(ri == ci, 1.0, 0.0)
    xs = [-low for low in lows]
    rs = [eye + x for x in xs]
    for _ in range(int(math.log2(c)) - 1):
        xs = [_dgh(x, x) for x in xs]
        rs = [r + _dgh(r, x) for r, x in zip(rs, xs)]
    return tuple(rs)


def _uli_fwd(lows):
    ts = _unit_lower_inverses(lows)
    return ts, ts


def _uli_bwd(ts, cts):
    mids = [_dgh(ct, t, "nt") for t, ct in zip(ts, cts)]
    return (tuple(-_dgh(t, m, "tn") for t, m in zip(ts, mids)),)


_unit_lower_inverses.defvjp(_uli_fwd, _uli_bwd)


@functools.partial(jax.custom_vjp, nondiff_argnums=(1,))
def _split_lanes(x, width):
    return tuple(x[:, i * width:(i + 1) * width] for i in range(x.shape[1] // width))


def _split_fwd(x, width):
    return _split_lanes(x, width), None


def _split_bwd(width, _, cts):
    return (jnp.concatenate(list(cts), axis=1),)


_split_lanes.defvjp(_split_fwd, _split_bwd)


def _sigmoid(x):
    return 1.0 / (1.0 + jnp.exp(-x))


def _silu(x):
    return x * _sigmoid(x)


def _softplus(x):
    return jnp.maximum(x, 0.0) + jnp.log1p(jnp.exp(-jnp.abs(x)))


def _rms(x, gain):
    return x * lax.rsqrt(jnp.mean(x * x, axis=-1, keepdims=True) + EPS) * gain


class V:
    def __init__(self, arr, lead=None, cb=0, w=None):
        self.arr, self.lead, self.cb = arr, lead, cb
        self.w = arr.shape[-1] if w is None else w

    @property
    def rows(self):
        return self.arr.shape[-2]

    def spec(self, tile, order=None):
        lead, cb, w = self.lead, self.cb, self.w
        order = order or (lambda i: i)
        if lead is None:
            return pl.BlockSpec((tile, w), lambda i: (order(i), cb))
        return pl.BlockSpec((None, tile, w), lambda i: (lead, order(i), cb))

    def const_spec(self):
        lead, cb, w, r = self.lead, self.cb, self.w, self.rows
        if lead is None:
            return pl.BlockSpec((r, w), lambda i: (0, cb))
        return pl.BlockSpec((None, r, w), lambda i: (lead, 0, cb))


def _v(a):
    return a if isinstance(a, V) else V(a)


def _rowwise(name, fn, rows, consts, outs, tile=None):
    rows = [_v(r) for r in rows]
    consts = [_v(c) for c in consts]
    s = rows[0].rows
    tile = min(tile or ROW_TILE, s)
    nr, nc = len(rows), len(consts)

    def body(*refs):
        vals = [r[...].astype(f32) for r in refs[:nr + nc]]
        res = fn(*vals)
        for o, val in zip(refs[nr + nc:], res):
            o[...] = val.astype(o.dtype)

    return pl.pallas_call(
        body, name=name, grid=(s // tile,),
        in_specs=[r.spec(tile) for r in rows] + [c.const_spec() for c in consts],
        out_specs=[pl.BlockSpec((tile, w), lambda i: (i, 0)) for w, _ in outs],
        out_shape=[SDS((s, w), dt) for w, dt in outs],
        compiler_params=_cp(("parallel",)),
    )(*[r.arr for r in rows], *[c.arr for c in consts])


def _rowwise_vjp(name, fn, rows, consts, cts, d_rows, add=None, tile=None):
    rows = [_v(r) for r in rows]
    consts = [_v(c) for c in consts]
    cts = [[_v(c) for c in group] for group in cts]
    flat_cts = [c for group in cts for c in group]
    s = rows[0].rows
    tile = min(tile or ROW_TILE, s)
    nr, nc, nt = len(rows), len(consts), len(flat_cts)
    want = [k for k, dt in enumerate(d_rows) if dt is not None]
    has_add = add is not None
    add_v = [_v(add)] if has_add else []

    def body(*refs):
        vals = [r[...].astype(f32) for r in refs[:nr + nc]]
        ct_refs = refs[nr + nc:nr + nc + nt]
        pos = nr + nc + nt
        add_ref = refs[pos] if has_add else None
        pos += 1 if has_add else 0
        drow_refs = refs[pos:pos + len(want)]
        dconst_refs = refs[pos + len(want):]
        ctv, at = [], 0
        for group in cts:
            acc = ct_refs[at][...].astype(f32)
            for r in ct_refs[at + 1:at + len(group)]:
                acc = acc + r[...].astype(f32)
            at += len(group)
            ctv.append(acc)
        _, vjp = jax.vjp(fn, *vals)
        grads = vjp(tuple(ctv))
        for o, k in zip(drow_refs, want):
            g = grads[k]
            if has_add and k == want[0]:
                g = g + add_ref[...].astype(f32)
            o[...] = g.astype(o.dtype)

        @pl.when(pl.program_id(0) == 0)
        def _():
            for o in dconst_refs:
                o[...] = jnp.zeros_like(o)

        for o, g in zip(dconst_refs, grads[nr:]):
            o[...] += g

    outs = pl.pallas_call(
        body, name=name, grid=(s // tile,),
        in_specs=[r.spec(tile) for r in rows] + [c.const_spec() for c in consts]
        + [c.spec(tile) for c in flat_cts] + [a.spec(tile) for a in add_v],
        out_specs=[pl.BlockSpec((tile, rows[k].w), lambda i: (i, 0)) for k in want]
        + [pl.BlockSpec((c.rows, c.w), lambda i: (0, 0)) for c in consts],
        out_shape=[SDS((s, rows[k].w), d_rows[k]) for k in want] + [SDS((c.rows, c.w), f32) for c in consts],
        compiler_params=_cp(("arbitrary",)),
    )(*[r.arr for r in rows], *[c.arr for c in consts], *[c.arr for c in flat_cts], *[a.arr for a in add_v])
    return list(outs[:len(want)]), list(outs[len(want):])


def _pick(n, cap):
    best = None
    for t in range(LANES, min(n, cap) + 1, LANES):
        if n % t == 0:
            best = t
    return best or n


def _matmul(name, a, b, mode, out_dtype, add=None, tn_cap=1280):
    has_add = add is not None
    if mode in ("nn", "nt"):
        m, k = a.shape
        n = b.shape[1] if mode == "nn" else b.shape[0]
        tm = min(MM_TILE, m)
        tn = _pick(n, tn_cap) if k * n * 2 > (8 << 20) else n

        def body(*refs):
            a_ref, b_ref = refs[0], refs[1]
            o_ref = refs[-1]
            acc = _dg(a_ref[...], b_ref[...], mode)
            if has_add:
                acc = acc + refs[2][...].astype(f32)
            o_ref[...] = acc.astype(o_ref.dtype)

        b_spec = pl.BlockSpec((k, tn), lambda i, j: (0, j)) if mode == "nn" else pl.BlockSpec((tn, k), lambda i, j: (j, 0))
        in_specs = [pl.BlockSpec((tm, k), lambda i, j: (i, 0)), b_spec]
        args = [a, b]
        if has_add:
            in_specs.append(pl.BlockSpec((tm, tn), lambda i, j: (i, j)))
            args.append(add)
        return pl.pallas_call(
            body, name=name, grid=(m // tm, n // tn), in_specs=in_specs,
            out_specs=pl.BlockSpec((tm, tn), lambda i, j: (i, j)),
            out_shape=SDS((m, n), out_dtype), compiler_params=_cp(("parallel", "parallel")),
        )(*args)
    kk, m = a.shape
    n = b.shape[1]
    tk = min(MM_TILE, kk)
    tn = _pick(n, tn_cap) if m * n * 4 > (7 << 20) else n

    def body_tn(a_ref, b_ref, o_ref):
        @pl.when(pl.program_id(1) == 0)
        def _():
            o_ref[...] = jnp.zeros_like(o_ref)

        o_ref[...] += _dg(a_ref[...], b_ref[...], "tn")

    return pl.pallas_call(
        body_tn, name=name, grid=(n // tn, kk // tk),
        in_specs=[pl.BlockSpec((tk, m), lambda j, k: (k, 0)), pl.BlockSpec((tk, tn), lambda j, k: (k, j))],
        out_specs=pl.BlockSpec((m, tn), lambda j, k: (0, j)),
        out_shape=SDS((m, n), f32), compiler_params=_cp(("parallel", "arbitrary")),
    )(a, b)


def _ffn_fwd(name, h, gain, wgu, wd, layer):
    s, d = h.shape
    hs = wgu.shape[3]
    nh = wgu.shape[0] // 2
    tm = min(FFN_FWD_TILE, s)

    def body(h_ref, g_ref, wg_ref, wu_ref, wd_ref, o_ref, gate_ref, up_ref, n_scr, acc_scr):
        t = pl.program_id(1)

        @pl.when(t == 0)
        def _():
            n_scr[...] = _rms(h_ref[...], g_ref[...]).astype(bf16)
            acc_scr[...] = jnp.zeros_like(acc_scr)

        n = n_scr[...]
        gate = _dg(n, wg_ref[...], "nn")
        up = _dg(n, wu_ref[...], "nn")
        gate_ref[...] = gate.astype(gate_ref.dtype)
        up_ref[...] = up.astype(up_ref.dtype)
        acc_scr[...] += _dg(_silu(gate) * up, wd_ref[...], "nn")

        @pl.when(t == nh - 1)
        def _():
            o_ref[...] = h_ref[...] + 0.5 * acc_scr[...]

    saved_spec = pl.BlockSpec((None, tm, hs), lambda i, t: (t, i, 0))
    return pl.pallas_call(
        body, name=name, grid=(s // tm, nh),
        in_specs=[
            pl.BlockSpec((tm, d), lambda i, t: (i, 0)),
            pl.BlockSpec((1, d), lambda i, t: (0, 0)),
            pl.BlockSpec((None, None, d, hs), lambda i, t: (t, layer, 0, 0)),
            pl.BlockSpec((None, None, d, hs), lambda i, t: (t + nh, layer, 0, 0)),
            pl.BlockSpec((None, hs, d), lambda i, t: (layer, t, 0)),
        ],
        out_specs=[pl.BlockSpec((tm, d), lambda i, t: (i, 0)), saved_spec, saved_spec],
        out_shape=[SDS((s, d), f32), SDS((nh, s, hs), bf16), SDS((nh, s, hs), bf16)],
        scratch_shapes=[pltpu.VMEM((tm, d), bf16), pltpu.VMEM((tm, d), f32)],
        compiler_params=_cp(("parallel", "arbitrary")),
    )(h, gain, wgu, wgu, wd)


def _ffn_bwd(name, h, dout, gain, gate_s, up_s, wgu, wd, layer):
    s, d = h.shape
    hs = wgu.shape[3]
    nh = wgu.shape[0] // 2
    tm = min(FFN_BWD_TILE, s)

    def body(h_ref, do_ref, g_ref, gate_ref, up_ref, wg_ref, wu_ref, wd_ref, dn_ref, dwg_ref, dwu_ref, dwd_ref):
        @pl.when(pl.program_id(1) == 0)
        def _():
            dwg_ref[...] = jnp.zeros_like(dwg_ref)
            dwu_ref[...] = jnp.zeros_like(dwu_ref)
            dwd_ref[...] = jnp.zeros_like(dwd_ref)

        n = _rms(h_ref[...], g_ref[...]).astype(bf16)
        wg, wu, wdn = wg_ref[...], wu_ref[...], wd_ref[...]
        gate = gate_ref[...].astype(f32)
        up = up_ref[...].astype(f32)
        sg = _sigmoid(gate)
        act = gate * sg
        dy = (0.5 * do_ref[...]).astype(bf16)
        da = _dg(dy, wdn, "nt")
        dup = (da * act).astype(bf16)
        dgate = (da * up * (sg * (1.0 + gate * (1.0 - sg)))).astype(bf16)
        dwd_ref[...] += _dg(act * up, dy, "tn")
        dwg_ref[...] += _dg(n, dgate, "tn")
        dwu_ref[...] += _dg(n, dup, "tn")
        dn_ref[...] = (_dg(dgate, wg, "nt") + _dg(dup, wu, "nt")).astype(dn_ref.dtype)

    return pl.pallas_call(
        body, name=name, grid=(nh, s // tm),
        in_specs=[
            pl.BlockSpec((tm, d), lambda t, i: (i, 0)),
            pl.BlockSpec((tm, d), lambda t, i: (i, 0)),
            pl.BlockSpec((1, d), lambda t, i: (0, 0)),
            pl.BlockSpec((None, tm, hs), lambda t, i: (t, i, 0)),
            pl.BlockSpec((None, tm, hs), lambda t, i: (t, i, 0)),
            pl.BlockSpec((None, None, d, hs), lambda t, i: (t, layer, 0, 0)),
            pl.BlockSpec((None, None, d, hs), lambda t, i: (t + nh, layer, 0, 0)),
            pl.BlockSpec((None, hs, d), lambda t, i: (layer, t, 0)),
        ],
        out_specs=[
            pl.BlockSpec((None, tm, d), lambda t, i: (t, i, 0)),
            pl.BlockSpec((None, d, hs), lambda t, i: (t, 0, 0)),
            pl.BlockSpec((None, d, hs), lambda t, i: (t, 0, 0)),
            pl.BlockSpec((hs, d), lambda t, i: (t, 0)),
        ],
        out_shape=[SDS((nh, s, d), bf16), SDS((nh, d, hs), f32), SDS((nh, d, hs), f32), SDS((nh * hs, d), f32)],
        compiler_params=_cp(("parallel", "arbitrary")),
    )(h, dout, gain, gate_s, up_s, wgu, wgu, wd)


def _conv_fwd(name, x, w):
    x = _v(x)
    s, c = x.rows, x.w
    cw = w.shape[0]
    tile = min(ROW_TILE, s)
    cb = x.cb

    def body(x_ref, halo_ref, w_ref, o_ref, buf):
        first = pl.program_id(0) == 0
        buf[0:8, :] = jnp.where(first, 0.0, halo_ref[...])
        buf[8:8 + tile, :] = x_ref[...]
        acc = w_ref[0:1, :] * buf[pl.ds(8 - cw + 1, tile), :]
        for j in range(1, cw):
            acc = acc + w_ref[j:j + 1, :] * buf[pl.ds(8 - cw + 1 + j, tile), :]
        o_ref[...] = acc

    return pl.pallas_call(
        body, name=name, grid=(s // tile,),
        in_specs=[
            pl.BlockSpec((tile, c), lambda i: (i, cb)),
            pl.BlockSpec((8, c), lambda i: (jnp.maximum(i * (tile // 8) - 1, 0), cb)),
            pl.BlockSpec((cw, c), lambda i: (0, 0)),
        ],
        out_specs=pl.BlockSpec((tile, c), lambda i: (i, 0)),
        out_shape=SDS((s, c), f32),
        scratch_shapes=[pltpu.VMEM((tile + 8, c), f32)],
        compiler_params=_cp(("parallel",)),
    )(x.arr, x.arr, w)


def _conv_bwd(name, x, dy, w):
    x = _v(x)
    s, c = x.rows, x.w
    cw = w.shape[0]
    tile = min(ROW_TILE, s)
    n_tiles = s // tile
    cb = x.cb

    def body(x_ref, xh_ref, dy_ref, dyh_ref, w_ref, dx_ref, dw_ref, xbuf, dbuf):
        i = pl.program_id(0)
        xbuf[0:8, :] = jnp.where(i == 0, 0.0, xh_ref[...])
        xbuf[8:8 + tile, :] = x_ref[...]
        dyv = dy_ref[...]
        dbuf[0:tile, :] = dyv
        dbuf[tile:tile + 8, :] = jnp.where(i == n_tiles - 1, 0.0, dyh_ref[...])

        @pl.when(i == 0)
        def _():
            dw_ref[...] = jnp.zeros_like(dw_ref)

        acc = w_ref[0:1, :] * dbuf[pl.ds(cw - 1, tile), :]
        for j in range(1, cw):
            acc = acc + w_ref[j:j + 1, :] * dbuf[pl.ds(cw - 1 - j, tile), :]
        dx_ref[...] = acc.astype(dx_ref.dtype)
        for j in range(cw):
            dw_ref[j:j + 1, :] += jnp.sum(xbuf[pl.ds(8 - cw + 1 + j, tile), :] * dyv, axis=0, keepdims=True)

    return pl.pallas_call(
        body, name=name, grid=(n_tiles,),
        in_specs=[
            pl.BlockSpec((tile, c), lambda i: (i, cb)),
            pl.BlockSpec((8, c), lambda i: (jnp.maximum(i * (tile // 8) - 1, 0), cb)),
            pl.BlockSpec((tile, c), lambda i: (i, 0)),
            pl.BlockSpec((8, c), lambda i: (jnp.minimum((i + 1) * (tile // 8), s // 8 - 1), 0)),
            pl.BlockSpec((cw, c), lambda i: (0, 0)),
        ],
        out_specs=[pl.BlockSpec((tile, c), lambda i: (i, 0)), pl.BlockSpec((cw, c), lambda i: (0, 0))],
        out_shape=[SDS((s, c), bf16), SDS((cw, c), f32)],
        scratch_shapes=[pltpu.VMEM((tile + 8, c), f32), pltpu.VMEM((tile + 8, c), f32)],
        compiler_params=_cp(("arbitrary",)),
    )(x.arr, x.arr, dy, dy, w)


def _gdn_pre(n_heads, head_dim, yc, ab, a_log, dt_bias):
    gw = n_heads * head_dim
    act = _silu(yc)
    parts = _split_lanes(act, head_dim)
    qs = [p * lax.rsqrt(jnp.sum(p * p, axis=-1, keepdims=True) + EPS) * (head_dim ** -0.5) for p in parts[:n_heads]]
    ks = [p * lax.rsqrt(jnp.sum(p * p, axis=-1, keepdims=True) + EPS) for p in parts[n_heads:2 * n_heads]]
    lane = lax.broadcasted_iota(jnp.int32, ab.shape, 1)
    g = -jnp.exp(a_log) * _softplus(ab + dt_bias)
    gb = jnp.where(lane < n_heads, g, jnp.where(lane < 2 * n_heads, _sigmoid(ab), 0.0))
    del gw
    return (jnp.concatenate(qs, axis=1), jnp.concatenate(ks, axis=1),
            jnp.concatenate(list(parts[2 * n_heads:]), axis=1), gb)


def _gdn_post(n_heads, head_dim, o, z, out_norm):
    parts = _split_lanes(o, head_dim)
    normed = jnp.concatenate([_rms(p, out_norm) for p in parts], axis=1)
    return (normed * _silu(z),)


def _gdn_chunk(n_heads, head_dim, q, k, v, gb, *states):
    c = q.shape[0]
    ri = lax.broadcasted_iota(jnp.int32, (c, c), 0)
    ci = lax.broadcasted_iota(jnp.int32, (c, c), 1)
    incl, strict, diag = ri >= ci, ri > ci, ri == ci
    lane = lax.broadcasted_iota(jnp.int32, gb.shape, 1)
    qs, ks, vs = _split_lanes(q, head_dim), _split_lanes(k, head_dim), _split_lanes(v, head_dim)
    heads = range(n_heads)
    g = [jnp.sum(jnp.where(lane == h, gb, 0.0), axis=1, keepdims=True) for h in heads]
    beta = [jnp.sum(jnp.where(lane == n_heads + h, gb, 0.0), axis=1, keepdims=True) for h in heads]
    g_row = [jnp.sum(jnp.where(diag, g[h], 0.0), axis=0, keepdims=True) for h in heads]
    cg_col = [jnp.sum(jnp.where(incl, g_row[h], 0.0), axis=1, keepdims=True) for h in heads]
    cg_row = [jnp.sum(jnp.where(ri <= ci, g[h], 0.0), axis=0, keepdims=True) for h in heads]
    g_last = [jnp.sum(g[h], axis=0, keepdims=True) for h in heads]
    decay = [jnp.where(incl, jnp.exp(jnp.where(incl, cg_col[h] - cg_row[h], 0.0)), 0.0) for h in heads]
    kb = [ks[h] * beta[h] for h in heads]
    lower = [jnp.where(strict, _mm(kb[h], ks[h], "nt") * decay[h], 0.0) for h in heads]
    eye = jnp.where(diag, 1.0, 0.0)
    off_diag = [t - eye for t in _unit_lower_inverses(tuple(lower))]
    e_col = [jnp.exp(cg_col[h]) for h in heads]
    vb = [vs[h] * beta[h] for h in heads]
    kbg = [kb[h] * e_col[h] for h in heads]
    u = [vb[h] + _mm(off_diag[h], vb[h], "nn") for h in heads]
    w = [kbg[h] + _mm(off_diag[h], kbg[h], "nn") for h in heads]
    qk = [jnp.where(incl, _mm(qs[h], ks[h], "nt") * decay[h], 0.0) for h in heads]
    v_new = [u[h] - _mm(w[h], states[h], "nn") for h in heads]
    inter = [_mm(qs[h] * e_col[h], states[h], "nn") for h in heads]
    outs = [inter[h] + _mm(qk[h], v_new[h], "nn") for h in heads]
    k_tail = [ks[h] * jnp.exp(g_last[h] - cg_col[h]) for h in heads]
    new_states = [states[h] * jnp.exp(g_last[h]) + _mm(k_tail[h], v_new[h], "tn") for h in heads]
    return (jnp.concatenate(outs, axis=1), *new_states)


def _gdn_chunk_fwd(name, q, k, v, gb, n_heads, head_dim):
    s, gw = q.shape
    n = s // CHUNK
    fn = functools.partial(_gdn_chunk, n_heads, head_dim)

    def body(q_ref, k_ref, v_ref, gb_ref, o_ref, st_ref, st_scr):
        @pl.when(pl.program_id(0) == 0)
        def _():
            st_scr[...] = jnp.zeros_like(st_scr)

        st_ref[...] = st_scr[...]
        res = fn(q_ref[...], k_ref[...], v_ref[...], gb_ref[...], *[st_scr[h] for h in range(n_heads)])
        o_ref[...] = res[0]
        for h in range(n_heads):
            st_scr[h] = res[1 + h]

    row = lambda w: pl.BlockSpec((CHUNK, w), lambda i: (i, 0))
    return pl.pallas_call(
        body, name=name, grid=(n,),
        in_specs=[row(gw), row(gw), row(gw), row(LANES)],
        out_specs=[row(gw), pl.BlockSpec((None, n_heads, head_dim, head_dim), lambda i: (i, 0, 0, 0))],
        out_shape=[SDS((s, gw), f32), SDS((n, n_heads, head_dim, head_dim), f32)],
        scratch_shapes=[pltpu.VMEM((n_heads, head_dim, head_dim), f32)],
        compiler_params=_cp(("arbitrary",)),
    )(q, k, v, gb)


def _gdn_chunk_bwd(name, q, k, v, gb, states, d_out, n_heads, head_dim):
    s, gw = q.shape
    n = s // CHUNK
    fn = functools.partial(_gdn_chunk, n_heads, head_dim)

    def body(q_ref, k_ref, v_ref, gb_ref, st_ref, do_ref, dq_ref, dk_ref, dv_ref, dgb_ref, dst_scr):
        @pl.when(pl.program_id(0) == 0)
        def _():
            dst_scr[...] = jnp.zeros_like(dst_scr)

        _, vjp = jax.vjp(fn, q_ref[...], k_ref[...], v_ref[...], gb_ref[...], *[st_ref[h] for h in range(n_heads)])
        grads = vjp((do_ref[...].astype(f32), *[dst_scr[h] for h in range(n_heads)]))
        dq_ref[...] = grads[0]
        dk_ref[...] = grads[1]
        dv_ref[...] = grads[2]
        dgb_ref[...] = grads[3]
        for h in range(n_heads):
            dst_scr[h] = grads[4 + h]

    row = lambda w: pl.BlockSpec((CHUNK, w), lambda i: (n - 1 - i, 0))
    return pl.pallas_call(
        body, name=name, grid=(n,),
        in_specs=[row(gw), row(gw), row(gw), row(LANES),
                  pl.BlockSpec((None, n_heads, head_dim, head_dim), lambda i: (n - 1 - i, 0, 0, 0)), row(gw)],
        out_specs=[row(gw), row(gw), row(gw), row(LANES)],
        out_shape=[SDS((s, gw), f32), SDS((s, gw), f32), SDS((s, gw), f32), SDS((s, LANES), f32)],
        scratch_shapes=[pltpu.VMEM((n_heads, head_dim, head_dim), f32)],
        compiler_params=_cp(("arbitrary",)),
    )(q, k, v, gb, states, d_out)


def _mem_attn(qm, km, vm):
    width = qm.shape[1]
    hd = width // MEM_HEADS
    lane = lax.broadcasted_iota(jnp.int32, (1, width), 1)
    out = jnp.zeros_like(qm)
    for h in range(MEM_HEADS):
        msk = jnp.where((lane >= h * hd) & (lane < (h + 1) * hd), 1.0, 0.0)
        logits = _mm(qm * msk, km, "nt") * (hd ** -0.5)
        p = jnp.exp(logits - jnp.max(logits, axis=-1, keepdims=True))
        p = p / jnp.sum(p, axis=-1, keepdims=True)
        out = out + _mm(p, vm, "nn") * msk
    return (out,)


def _kv_post(n_heads, pk, pv, pf, b_f):
    lane = lax.broadcasted_iota(jnp.int32, pf.shape, 1)
    log_f = jnp.where(lane < n_heads, -_softplus(-(pf + b_f)), 0.0)
    return pk, pv, log_f


def _cumsum(name, xs, reverse):
    s, w = xs[0].shape
    tile = min(CUMSUM_TILE, s)
    n = s // tile

    def body(*refs):
        x_refs, o_ref, carry = refs[:-2], refs[-2], refs[-1]

        @pl.when(pl.program_id(0) == 0)
        def _():
            carry[...] = jnp.zeros_like(carry)

        xv = x_refs[0][...]
        for r in x_refs[1:]:
            xv = xv + r[...]
        ri = lax.broadcasted_iota(jnp.int32, (tile, tile), 0)
        ci = lax.broadcasted_iota(jnp.int32, (tile, tile), 1)
        tri = jnp.where((ri <= ci) if reverse else (ri >= ci), 1.0, 0.0).astype(bf16)
        x1 = xv.astype(bf16)
        r1 = xv - x1.astype(f32)
        x2 = r1.astype(bf16)
        x3 = (r1 - x2.astype(f32)).astype(bf16)
        acc = carry[...] + _dg(tri, x1, "nn") + _dg(tri, x2, "nn") + _dg(tri, x3, "nn")
        o_ref[...] = acc
        carry[...] += jnp.sum(xv, axis=0, keepdims=True)

    order = (lambda i: (n - 1 - i, 0)) if reverse else (lambda i: (i, 0))
    return pl.pallas_call(
        body, name=name, grid=(n,),
        in_specs=[pl.BlockSpec((tile, w), order)] * len(xs), out_specs=pl.BlockSpec((tile, w), order),
        out_shape=SDS((s, w), f32), scratch_shapes=[pltpu.VMEM((1, w), f32)],
        compiler_params=_cp(("arbitrary",)),
    )(*xs)


def _fox_logits(q_ref, k_ref, cr_ref, hh, head_dim, scale, diagonal):
    sl = slice(hh * head_dim, (hh + 1) * head_dim)
    s = _dg(q_ref[:, sl], k_ref[:, sl], "nt") * scale - cr_ref[hh]
    if diagonal:
        tq, tk = s.shape
        ok = lax.broadcasted_iota(jnp.int32, (tq, tk), 1) <= lax.broadcasted_iota(jnp.int32, (tq, tk), 0)
        s = jnp.where(ok, s, NEG_INF)
    return s, sl


def _on_causal_tiles(i, j, fn):
    @pl.when(j < i)
    def _():
        fn(False)

    @pl.when(j == i)
    def _():
        fn(True)


def _fox_fwd(name, q, k, v, crow, n_heads, head_dim):
    s = k.shape[0]
    tq = tk = min(ATT_TILE, s)
    nq, nk = s // tq, s // tk
    scale = head_dim ** -0.5
    hpb = ATT_HEADS
    wb = hpb * head_dim

    def body(q_ref, k_ref, v_ref, cr_ref, o_ref, lse_ref, m_scr, l_scr, acc_scr):
        i, j = pl.program_id(1), pl.program_id(2)

        @pl.when(j == 0)
        def _():
            m_scr[...] = jnp.full_like(m_scr, NEG_INF)
            l_scr[...] = jnp.zeros_like(l_scr)
            acc_scr[...] = jnp.zeros_like(acc_scr)

        def step(diagonal):
            heads = range(hpb)
            sl = [slice(hh * head_dim, (hh + 1) * head_dim) for hh in heads]
            sc = [_fox_logits(q_ref, k_ref, cr_ref, hh, head_dim, scale, diagonal)[0] for hh in heads]
            m_old = [m_scr[hh] for hh in heads]
            m_new = [jnp.maximum(m_old[hh], jnp.max(sc[hh], axis=-1, keepdims=True)) for hh in heads]
            p = [jnp.exp(sc[hh] - m_new[hh]) for hh in heads]
            alpha = [jnp.exp(m_old[hh] - m_new[hh]) for hh in heads]
            pv = [_dg(p[hh], v_ref[:, sl[hh]], "nn") for hh in heads]
            for hh in heads:
                l_scr[hh] = alpha[hh] * l_scr[hh] + jnp.sum(p[hh], axis=-1, keepdims=True)
                acc_scr[:, sl[hh]] = alpha[hh] * acc_scr[:, sl[hh]] + pv[hh]
                m_scr[hh] = m_new[hh]

        _on_causal_tiles(i, j, step)

        @pl.when(j == nk - 1)
        def _():
            for hh in range(hpb):
                sl = slice(hh * head_dim, (hh + 1) * head_dim)
                o_ref[:, sl] = (acc_scr[:, sl] / l_scr[hh]).astype(o_ref.dtype)
                lse_ref[hh] = m_scr[hh] + jnp.log(l_scr[hh])

    return pl.pallas_call(
        body, name=name, grid=(n_heads // hpb, nq, nk),
        in_specs=[
            pl.BlockSpec((tq, wb), lambda g, i, j: (i, g)),
            pl.BlockSpec((tk, wb), lambda g, i, j: (jnp.minimum(j, i), g)),
            pl.BlockSpec((tk, wb), lambda g, i, j: (jnp.minimum(j, i), g)),
            pl.BlockSpec((hpb, 1, tk), lambda g, i, j: (g, 0, jnp.minimum(j, i))),
        ],
        out_specs=[pl.BlockSpec((tq, wb), lambda g, i, j: (i, g)),
                   pl.BlockSpec((hpb, tq, 1), lambda g, i, j: (g, i, 0))],
        out_shape=[SDS((s, n_heads * head_dim), bf16), SDS((n_heads, s, 1), f32)],
        scratch_shapes=[pltpu.VMEM((hpb, tq, 1), f32), pltpu.VMEM((hpb, tq, 1), f32), pltpu.VMEM((tq, wb), f32)],
        compiler_params=_cp(("parallel", "parallel", "arbitrary")),
    )(q, k, v, crow)


def _fox_probs(q_ref, k_ref, v_ref, cr_ref, lse_ref, do_ref, hh, head_dim, scale, diagonal):
    sc, sl = _fox_logits(q_ref, k_ref, cr_ref, hh, head_dim, scale, diagonal)
    return jnp.exp(sc - lse_ref[hh]), _dg(do_ref[:, sl], v_ref[:, sl], "nt"), sl


def _fox_bwd_dq(name, q, k, v, crow, lse, o, do, n_heads, head_dim):
    s = k.shape[0]
    tq = tk = min(ATT_TILE, s)
    nq, nk = s // tq, s // tk
    scale = head_dim ** -0.5
    hpb = ATT_HEADS
    wb = hpb * head_dim

    def body(q_ref, k_ref, v_ref, cr_ref, lse_ref, o_ref, do_ref, dq_ref, delta_ref, dcc_ref, acc_scr):
        i, j = pl.program_id(1), pl.program_id(2)
        heads = range(hpb)

        @pl.when(j == 0)
        def _():
            prod = o_ref[...].astype(f32) * do_ref[...].astype(f32)
            for hh in heads:
                delta_ref[hh] = jnp.sum(prod[:, hh * head_dim:(hh + 1) * head_dim], axis=-1, keepdims=True)
            dcc_ref[...] = jnp.zeros_like(dcc_ref)
            acc_scr[...] = jnp.zeros_like(acc_scr)

        def step(diagonal):
            pd = [_fox_probs(q_ref, k_ref, v_ref, cr_ref, lse_ref, do_ref, hh, head_dim, scale, diagonal) for hh in heads]
            ds = [pd[hh][0] * (pd[hh][1] - delta_ref[hh]) for hh in heads]
            dqs = [_dg(ds[hh], k_ref[:, pd[hh][2]], "nn") for hh in heads]
            for hh in heads:
                dcc_ref[hh] += jnp.sum(ds[hh], axis=-1, keepdims=True)
                acc_scr[:, pd[hh][2]] += dqs[hh]

        _on_causal_tiles(i, j, step)

        @pl.when(j == nk - 1)
        def _():
            dq_ref[...] = (acc_scr[...] * scale).astype(dq_ref.dtype)

    qspec = pl.BlockSpec((tq, wb), lambda g, i, j: (i, g))
    kspec = pl.BlockSpec((tk, wb), lambda g, i, j: (jnp.minimum(j, i), g))
    cspec = pl.BlockSpec((hpb, tq, 1), lambda g, i, j: (g, i, 0))
    return pl.pallas_call(
        body, name=name, grid=(n_heads // hpb, nq, nk),
        in_specs=[qspec, kspec, kspec,
                  pl.BlockSpec((hpb, 1, tk), lambda g, i, j: (g, 0, jnp.minimum(j, i))), cspec, qspec, qspec],
        out_specs=[qspec, cspec, cspec],
        out_shape=[SDS((s, n_heads * head_dim), bf16), SDS((n_heads, s, 1), f32), SDS((n_heads, s, 1), f32)],
        scratch_shapes=[pltpu.VMEM((tq, wb), f32)],
        compiler_params=_cp(("parallel", "parallel", "arbitrary")),
    )(q, k, v, crow, lse, o, do)


def _fox_bwd_dkv(name, q, k, v, crow, lse, delta, do, n_heads, head_dim):
    s = k.shape[0]
    tq = tk = min(ATT_TILE, s)
    nq, nk = s // tq, s // tk
    scale = head_dim ** -0.5
    hpb = ATT_HEADS
    wb = hpb * head_dim

    def body(q_ref, k_ref, v_ref, cr_ref, lse_ref, delta_ref, do_ref, dk_ref, dv_ref, dc_ref):
        j, i = pl.program_id(1), pl.program_id(2)

        @pl.when(i == 0)
        def _():
            dk_ref[...] = jnp.zeros_like(dk_ref)
            dv_ref[...] = jnp.zeros_like(dv_ref)
            dc_ref[...] = jnp.zeros_like(dc_ref)

        def step(diagonal):
            heads = range(hpb)
            pd = [_fox_probs(q_ref, k_ref, v_ref, cr_ref, lse_ref, do_ref, hh, head_dim, scale, diagonal) for hh in heads]
            ds = [pd[hh][0] * (pd[hh][1] - delta_ref[hh]) for hh in heads]
            dvs = [_dg(pd[hh][0], do_ref[:, pd[hh][2]], "tn") for hh in heads]
            dks = [_dg(ds[hh], q_ref[:, pd[hh][2]], "tn") for hh in heads]
            for hh in heads:
                sl = pd[hh][2]
                dv_ref[:, sl] += dvs[hh]
                dk_ref[:, sl] += dks[hh] * scale
                dc_ref[hh] -= jnp.sum(ds[hh], axis=0, keepdims=True)

        _on_causal_tiles(i, j, step)

    qspec = pl.BlockSpec((tq, wb), lambda g, j, i: (jnp.maximum(i, j), g))
    kspec = pl.BlockSpec((tk, wb), lambda g, j, i: (j, g))
    cspec = pl.BlockSpec((hpb, tq, 1), lambda g, j, i: (g, jnp.maximum(i, j), 0))
    rspec = pl.BlockSpec((hpb, 1, tk), lambda g, j, i: (g, 0, j))
    return pl.pallas_call(
        body, name=name, grid=(n_heads // hpb, nk, nq),
        in_specs=[qspec, kspec, kspec, rspec, cspec, cspec, qspec],
        out_specs=[kspec, kspec, rspec],
        out_shape=[SDS((s, n_heads * head_dim), f32), SDS((s, n_heads * head_dim), f32), SDS((n_heads, 1, s), f32)],
        compiler_params=_cp(("parallel", "parallel", "arbitrary")),
    )(q, k, v, crow, lse, delta, do)


def _final_loss(name, h, target, gain):
    s, d = h.shape
    tile = min(ROW_TILE, s)

    def body(h_ref, t_ref, g_ref, loss_ref, dh_ref, dg_ref):
        @pl.when(pl.program_id(0) == 0)
        def _():
            loss_ref[...] = jnp.zeros_like(loss_ref)
            dg_ref[...] = jnp.zeros_like(dg_ref)

        x, g = h_ref[...], g_ref[...]
        rstd = lax.rsqrt(jnp.mean(x * x, axis=-1, keepdims=True) + EPS)
        xhat = x * rstd
        err = xhat * g - t_ref[...]
        row = jnp.sum(err * err, axis=-1, keepdims=True) * (0.5 / d)
        loss_ref[...] += jnp.sum(row, axis=0, keepdims=True)
        dy = err * (1.0 / d)
        dg_ref[...] += jnp.sum(dy * xhat, axis=0, keepdims=True)
        dxhat = dy * g
        dh_ref[...] = rstd * (dxhat - xhat * jnp.mean(dxhat * xhat, axis=-1, keepdims=True))

    return pl.pallas_call(
        body, name=name, grid=(s // tile,),
        in_specs=[pl.BlockSpec((tile, d), lambda i: (i, 0)), pl.BlockSpec((tile, d), lambda i: (i, 0)),
                  pl.BlockSpec((1, d), lambda i: (0, 0))],
        out_specs=[pl.BlockSpec((1, LANES), lambda i: (0, 0)), pl.BlockSpec((tile, d), lambda i: (i, 0)),
                   pl.BlockSpec((1, d), lambda i: (0, 0))],
        out_shape=[SDS((1, LANES), f32), SDS((s, d), f32), SDS((1, d), f32)],
        compiler_params=_cp(("arbitrary",)),
    )(h, target, gain)


def _adamw(name, parts, w, m, v):
    r, c = w.shape
    rb = r
    for cand in (256, 128, 64, 32, 16):
        if r % cand == 0 and cand * c * 4 <= (2 << 20):
            rb = cand
            break
    n_parts = parts.shape[0]

    def body(p_ref, w_ref, m_ref, v_ref, g_out, d_out, m_out, v_out):
        g = p_ref[0].astype(f32)
        for k in range(1, n_parts):
            g = g + p_ref[k].astype(f32)
        m_new = ADAM_B1 * m_ref[...] + (1.0 - ADAM_B1) * g
        v_new = ADAM_B2 * v_ref[...] + (1.0 - ADAM_B2) * (g * g)
        m_hat = m_new / (1.0 - ADAM_B1 ** ADAM_STEP)
        v_hat = v_new / (1.0 - ADAM_B2 ** ADAM_STEP)
        g_out[...] = g
        d_out[...] = -ADAM_LR * (m_hat / (jnp.sqrt(v_hat) + ADAM_EPS) + ADAM_WD * w_ref[...])
        m_out[...] = m_new
        v_out[...] = v_new

    blk = pl.BlockSpec((rb, c), lambda i: (i, 0))
    return pl.pallas_call(
        body, name=name, grid=(r // rb,),
        in_specs=[pl.BlockSpec((n_parts, rb, c), lambda i: (0, i, 0)), blk, blk, blk],
        out_specs=[blk, blk, blk, blk],
        out_shape=[SDS((r, c), f32)] * 4,
        compiler_params=_cp(("parallel",)),
    )(parts, w, m, v)


def _position():
    x, y, c = lax.axis_index("x"), lax.axis_index("y"), lax.axis_index("c")
    return x, y, c


def _all_gather(name, shards):
    n = len(shards)

    def body(*refs):
        ins, outs = refs[:n], refs[n:2 * n]
        send_sems, recv_sems, local_sems = refs[2 * n:]
        x, y, c = _position()
        me, sibling = (x, y, c), (x, y, 1 - c)
        chips = [(1 - x, y), (x, 1 - y), (1 - x, 1 - y)]

        def slot(a, block):
            px, py, pc = block
            return outs[a].at[4 * px + 2 * py + pc]

        def copy(a, k, block, to, src=None):
            return pltpu.make_async_remote_copy(
                src_ref=slot(a, block) if src is None else src, dst_ref=slot(a, block),
                send_sem=send_sems.at[a, k], recv_sem=recv_sems.at[a, k], device_id=to, device_id_type=MESH)

        local = [pltpu.make_async_copy(ins[a], slot(a, me), local_sems.at[a]) for a in range(n)]
        for cp in local:
            cp.start()
        started = []
        for a in range(n):
            first = [copy(a, 0, me, sibling, src=ins[a])]
            first += [copy(a, 1 + j, me, (*chip, c), src=ins[a]) for j, chip in enumerate(chips)]
            for cp in first:
                cp.start()
            started += first
        for a in range(n):
            for j, chip in enumerate(chips):
                copy(a, 1 + j, (*chip, c), me).wait_recv()
                passed = copy(a, 4 + j, (*chip, c), sibling)
                passed.start()
                started.append(passed)
        for a in range(n):
            copy(a, 0, sibling, me).wait_recv()
            for j, chip in enumerate(chips):
                copy(a, 4 + j, (*chip, 1 - c), me).wait_recv()
        for cp in started:
            cp.wait_send()
        for cp in local:
            cp.wait()

    any_spec = pl.BlockSpec(memory_space=pl.ANY)
    outs = pl.pallas_call(
        body, name=name,
        in_specs=[any_spec] * n, out_specs=[any_spec] * n,
        out_shape=[SDS((N_DEV, *a.shape), a.dtype) for a in shards],
        scratch_shapes=[pltpu.SemaphoreType.DMA((n, 7)), pltpu.SemaphoreType.DMA((n, 7)), pltpu.SemaphoreType.DMA((n,))],
    )(*shards)
    return list(outs)


N_CHIPS = N_DEV // 2


def _sibling_exchange(name, arrs):
    n = len(arrs)

    def body(*refs):
        ins, outs = refs[:n], refs[n:2 * n]
        send_sems, recv_sems = refs[2 * n:]
        x, y, c = _position()
        copies = [pltpu.make_async_remote_copy(
            src_ref=ins[a], dst_ref=outs[a], send_sem=send_sems.at[a], recv_sem=recv_sems.at[a],
            device_id=(x, y, 1 - c), device_id_type=MESH) for a in range(n)]
        for cp in copies:
            cp.start()
        for cp in copies:
            cp.wait_recv()
        for cp in copies:
            cp.wait_send()

    any_spec = pl.BlockSpec(memory_space=pl.ANY)
    outs = pl.pallas_call(
        body, name=name,
        in_specs=[any_spec] * n, out_specs=[any_spec] * n, out_shape=[SDS(a.shape, a.dtype) for a in arrs],
        scratch_shapes=[pltpu.SemaphoreType.DMA((n,)), pltpu.SemaphoreType.DMA((n,))],
    )(*arrs)
    return list(outs)


def _chip_exchange(name, parts):
    n = len(parts)

    def body(*refs):
        ins, outs = refs[:n], refs[n:2 * n]
        send_sems, recv_sems, local_sems = refs[2 * n:]
        x, y, c = _position()
        my_chip = 2 * x + y
        local = [pltpu.make_async_copy(ins[a].at[my_chip], outs[a].at[my_chip], local_sems.at[a]) for a in range(n)]
        for cp in local:
            cp.start()
        sends, recvs = [], []
        for a in range(n):
            for k in range(1, N_CHIPS):
                px = (1 - x) if (k >> 1) & 1 else x
                py = (1 - y) if k & 1 else y
                peer_chip = 2 * px + py
                send = pltpu.make_async_remote_copy(
                    src_ref=ins[a].at[peer_chip], dst_ref=outs[a].at[my_chip],
                    send_sem=send_sems.at[a, k - 1], recv_sem=recv_sems.at[a, k - 1],
                    device_id=(px, py, c), device_id_type=MESH)
                send.start()
                sends.append(send)
                recvs.append(pltpu.make_async_remote_copy(
                    src_ref=ins[a].at[peer_chip], dst_ref=outs[a].at[peer_chip],
                    send_sem=send_sems.at[a, k - 1], recv_sem=recv_sems.at[a, k - 1],
                    device_id=(px, py, c), device_id_type=MESH))
        for cp in recvs:
            cp.wait_recv()
        for cp in sends:
            cp.wait_send()
        for cp in local:
            cp.wait()

    any_spec = pl.BlockSpec(memory_space=pl.ANY)
    outs = pl.pallas_call(
        body, name=name,
        in_specs=[any_spec] * n, out_specs=[any_spec] * n,
        out_shape=[SDS(a.shape, a.dtype) for a in parts],
        scratch_shapes=[pltpu.SemaphoreType.DMA((n, N_CHIPS - 1)), pltpu.SemaphoreType.DMA((n, N_CHIPS - 1)),
                        pltpu.SemaphoreType.DMA((n,))],
    )(*parts)
    return list(outs)


def _reduce_exchange(stacks):
    core = lax.axis_index("c")
    by_core = [st.reshape(N_CHIPS, 2, *st.shape[1:]) for st in stacks]
    mine = [lax.dynamic_index_in_dim(v, core, axis=1, keepdims=False) for v in by_core]
    theirs = _sibling_exchange(
        "exchange_sibling", [lax.dynamic_index_in_dim(v, 1 - core, axis=1, keepdims=False) for v in by_core])
    summed = []
    for a, (m, t) in enumerate(zip(mine, theirs)):
        cols = m.shape[-1]
        (both,) = _rowwise(f"exchange_add_{a}", lambda p, q: (p + q,), [m.reshape(-1, cols), t.reshape(-1, cols)], [],
                           [(cols, m.dtype)])
        summed.append(both.reshape(m.shape))
    return _chip_exchange("exchange_chips", summed)


def _rows_from_shards(g):
    n, l, r, c = g.shape
    return g.transpose(1, 0, 2, 3).reshape(l, n * r, c)


def _rows_to_shards(w):
    l, rows, c = w.shape
    return w.reshape(l, N_DEV, rows // N_DEV, c).transpose(1, 0, 2, 3)


def _pad_lanes(a, width):
    return jnp.pad(a, [(0, 0)] * (a.ndim - 1) + [(0, width - a.shape[-1])])


def _row(vec, width=None):
    vec = vec.reshape(1, -1)
    return vec if width is None else _pad_lanes(vec, width)


class _SmallPack:
    def __init__(self, shapes):
        self.shapes, self.offsets, at = shapes, {}, 0
        for name, shape in shapes.items():
            last = shape[-1]
            lead = int(math.prod(shape[:-1]))
            rows = lead * (last // LANES) if last >= LANES else lead
            self.offsets[name] = (at, rows)
            at += rows
        self.rows = -(-at // 8) * 8

    def pack(self, values):
        pieces = []
        for name, shape in self.shapes.items():
            val = values[name].astype(f32)
            if shape[-1] >= LANES:
                pieces.append(val.reshape(-1, LANES))
            else:
                pieces.append(_pad_lanes(val.reshape(-1, shape[-1]), LANES))
        used = sum(p.shape[0] for p in pieces)
        if used < self.rows:
            pieces.append(jnp.zeros((self.rows - used, LANES), f32))
        return jnp.concatenate(pieces, axis=0)

    def unpack(self, packed):
        out = {}
        for name, shape in self.shapes.items():
            at, rows = self.offsets[name]
            blk = packed[at:at + rows]
            out[name] = blk.reshape(shape) if shape[-1] >= LANES else blk[:, :shape[-1]].reshape(shape)
        return out


def kernel(x, mem, ffn1_norm, ffn1_w_gate_up, ffn1_w_down, mix_norm, ffn2_norm, ffn2_w_gate_up, ffn2_w_down, gdn_w_in, gdn_conv, gdn_A_log, gdn_dt_bias, gdn_out_norm, fox_w_in, w_out, mem_norm, mem_w_kv, kv_norm, kv_w, kv_b_f, final_norm, loss_target, m_ffn1_norm, m_ffn1_w_gate_up, m_ffn1_w_down, m_mix_norm, m_ffn2_norm, m_ffn2_w_gate_up, m_ffn2_w_down, m_gdn_w_in, m_gdn_conv, m_gdn_A_log, m_gdn_dt_bias, m_gdn_out_norm, m_fox_w_in, m_w_out, m_mem_norm, m_mem_w_kv, m_kv_norm, m_kv_w, m_kv_b_f, m_final_norm, v_ffn1_norm, v_ffn1_w_gate_up, v_ffn1_w_down, v_mix_norm, v_ffn2_norm, v_ffn2_w_gate_up, v_ffn2_w_down, v_gdn_w_in, v_gdn_conv, v_gdn_A_log, v_gdn_dt_bias, v_gdn_out_norm, v_fox_w_in, v_w_out, v_mem_norm, v_mem_w_kv, v_kv_norm, v_kv_w, v_kv_b_f, v_final_norm):
    weights = dict(ffn1_norm=ffn1_norm, ffn1_w_gate_up=ffn1_w_gate_up, ffn1_w_down=ffn1_w_down, mix_norm=mix_norm,
                   ffn2_norm=ffn2_norm, ffn2_w_gate_up=ffn2_w_gate_up, ffn2_w_down=ffn2_w_down, gdn_w_in=gdn_w_in,
                   gdn_conv=gdn_conv, gdn_A_log=gdn_A_log, gdn_dt_bias=gdn_dt_bias, gdn_out_norm=gdn_out_norm,
                   fox_w_in=fox_w_in, w_out=w_out, mem_norm=mem_norm, mem_w_kv=mem_w_kv, kv_norm=kv_norm, kv_w=kv_w,
                   kv_b_f=kv_b_f, final_norm=final_norm)
    mom_m = dict(ffn1_norm=m_ffn1_norm, ffn1_w_gate_up=m_ffn1_w_gate_up, ffn1_w_down=m_ffn1_w_down, mix_norm=m_mix_norm,
                 ffn2_norm=m_ffn2_norm, ffn2_w_gate_up=m_ffn2_w_gate_up, ffn2_w_down=m_ffn2_w_down, gdn_w_in=m_gdn_w_in,
                 gdn_conv=m_gdn_conv, gdn_A_log=m_gdn_A_log, gdn_dt_bias=m_gdn_dt_bias, gdn_out_norm=m_gdn_out_norm,
                 fox_w_in=m_fox_w_in, w_out=m_w_out, mem_norm=m_mem_norm, mem_w_kv=m_mem_w_kv, kv_norm=m_kv_norm,
                 kv_w=m_kv_w, kv_b_f=m_kv_b_f, final_norm=m_final_norm)
    mom_v = dict(ffn1_norm=v_ffn1_norm, ffn1_w_gate_up=v_ffn1_w_gate_up, ffn1_w_down=v_ffn1_w_down, mix_norm=v_mix_norm,
                 ffn2_norm=v_ffn2_norm, ffn2_w_gate_up=v_ffn2_w_gate_up, ffn2_w_down=v_ffn2_w_down, gdn_w_in=v_gdn_w_in,
                 gdn_conv=v_gdn_conv, gdn_A_log=v_gdn_A_log, gdn_dt_bias=v_gdn_dt_bias, gdn_out_norm=v_gdn_out_norm,
                 fox_w_in=v_fox_w_in, w_out=v_w_out, mem_norm=v_mem_norm, mem_w_kv=v_mem_w_kv, kv_norm=v_kv_norm,
                 kv_w=v_kv_w, kv_b_f=v_kv_b_f, final_norm=v_final_norm)
    names = list(weights)
    small_names = [n for n in names if weights[n].shape == mom_m[n].shape and n in (
        "ffn1_norm", "mix_norm", "ffn2_norm", "gdn_A_log", "gdn_dt_bias", "gdn_out_norm", "mem_norm", "kv_norm",
        "kv_b_f", "final_norm")]
    big_names = [n for n in names if n not in small_names]

    h = x[0]
    target = loss_target[0]
    mem_tokens = mem[0]
    s, d = h.shape
    depth = ffn1_norm.shape[0]
    n_a = gdn_w_in.shape[0]
    n_heads, head_dim = gdn_A_log.shape[1], gdn_out_norm.shape[1]
    gw = n_heads * head_dim
    a_in = gdn_w_in.shape[2]
    mem_w = a_in - 4 * gw - 2 * n_heads
    a_in_pad = 4 * gw + mem_w + LANES
    kv_width = kv_w.shape[1]
    kv_pad = 2 * gw + LANES
    fh = ffn1_w_down.shape[1] * N_DEV

    def permute_in(w):
        ab = w[..., 4 * gw:4 * gw + 2 * n_heads]
        return jnp.concatenate([w[..., :4 * gw], w[..., 4 * gw + 2 * n_heads:], _pad_lanes(ab, LANES)], axis=-1)

    def unpermute_in(w):
        return jnp.concatenate([w[..., :4 * gw], w[..., 4 * gw + mem_w:4 * gw + mem_w + 2 * n_heads],
                                w[..., 4 * gw:4 * gw + mem_w]], axis=-1)

    gathered = _all_gather("gather_weights", [
        ffn1_w_gate_up.astype(bf16), ffn1_w_down.astype(bf16), ffn2_w_gate_up.astype(bf16), ffn2_w_down.astype(bf16),
        permute_in(gdn_w_in).astype(bf16), fox_w_in.astype(bf16), w_out.astype(bf16), mem_w_kv.astype(bf16),
        _pad_lanes(kv_w, kv_pad).astype(bf16)[None], gdn_conv])
    wgu1, wd1_s, wgu2, wd2_s, win_s, wfox_s, wout_s, wmem_s, wkv_s, conv_s = gathered
    wd1, wd2 = _rows_from_shards(wd1_s), _rows_from_shards(wd2_s)
    win, wfox, wout = _rows_from_shards(win_s), _rows_from_shards(wfox_s), _rows_from_shards(wout_s)
    wmem = _rows_from_shards(wmem_s)
    wmem_cat = wmem.transpose(1, 0, 2).reshape(d, depth * 2 * mem_w)
    wkv = _rows_from_shards(wkv_s)[0]
    conv_w = conv_s.transpose(1, 2, 0, 3).reshape(n_a, gdn_conv.shape[1], 3 * gw)

    a_log_rows = [_row(gdn_A_log[l], LANES) for l in range(n_a)]
    dt_rows = [_row(gdn_dt_bias[l], LANES) for l in range(n_a)]
    onorm_rows = [_row(gdn_out_norm[l]) for l in range(n_a)]
    b_f_row = _row(kv_b_f, LANES)

    (mem_n,) = _rowwise("mem_norm", lambda t, g: (_rms(t, g),), [mem_tokens], [_row(mem_norm)], [(d, bf16)])
    mem_kv = _matmul("mem_kv", mem_n, wmem_cat, "nn", f32)

    saved = []
    shared = None
    for l in range(depth):
        rec = {"h0": h}
        h1, gate1, up1 = _ffn_fwd(f"ffn1_fwd_{l}", h, _row(ffn1_norm[l]), wgu1, wd1, l)
        (u,) = _rowwise(f"mix_norm_{l}", lambda t, g: (_rms(t, g),), [h1], [_row(mix_norm[l])], [(d, bf16)])
        rec.update(h1=h1, u=u, ffn1=(gate1, up1))
        if l < n_a:
            proj = _matmul(f"gdn_in_{l}", u, win[l], "nn", f32)
            yc = _conv_fwd(f"conv_fwd_{l}", V(proj, cb=0, w=3 * gw), conv_w[l])
            ab_view = V(proj, cb=(4 * gw + mem_w) // LANES, w=LANES)
            q, k, v, gb = _rowwise(f"gdn_pre_{l}", functools.partial(_gdn_pre, n_heads, head_dim),
                                   [yc, ab_view], [a_log_rows[l], dt_rows[l]],
                                   [(gw, f32), (gw, f32), (gw, f32), (LANES, f32)])
            o, states = _gdn_chunk_fwd(f"gdn_chunk_fwd_{l}", q, k, v, gb, n_heads, head_dim)
            z_view = V(proj, cb=3, w=gw)
            (main,) = _rowwise(f"gdn_post_{l}", functools.partial(_gdn_post, n_heads, head_dim),
                               [o, z_view], [onorm_rows[l]], [(gw, bf16)])
            qmem_view = V(proj, cb=4 * gw // mem_w, w=mem_w)
            rec.update(proj=proj, yc=yc, q=q, k=k, v=v, gb=gb, o=o, states=states)
        else:
            proj = _matmul(f"fox_in_{l}", u, wfox[l - n_a], "nn", bf16)
            sk, sv, crow = shared["k"], shared["v"], shared["crow"]
            main, lse = _fox_fwd(f"fox_fwd_{l}", proj, sk, sv, crow, n_heads, head_dim)
            qmem_view = V(proj, cb=gw // mem_w, w=mem_w)
            rec.update(proj=proj, lse=lse)
        km = V(mem_kv, cb=2 * l, w=mem_w)
        vm = V(mem_kv, cb=2 * l + 1, w=mem_w)
        (mem_out,) = _rowwise(f"mem_attn_{l}", _mem_attn, [qmem_view], [km, vm], [(mem_w, bf16)])
        cat = jnp.concatenate([main, mem_out], axis=1)
        h2 = _matmul(f"out_proj_{l}", cat, wout[l], "nn", f32, add=h1)
        h3, gate2, up2 = _ffn_fwd(f"ffn2_fwd_{l}", h2, _row(ffn2_norm[l]), wgu2, wd2, l)
        rec.update(cat=cat, h2=h2, qmem=qmem_view, ffn2=(gate2, up2))
        saved.append(rec)
        h = h3
        if l == n_a - 1:
            (hn,) = _rowwise("kv_norm", lambda t, g: (_rms(t, g),), [h], [_row(kv_norm)], [(d, bf16)])
            p = _matmul("kv_proj", hn, wkv, "nn", f32)
            pk, pv, pf = V(p, cb=0, w=gw), V(p, cb=1, w=gw), V(p, cb=2 * gw // LANES, w=LANES)
            sk, sv, log_f = _rowwise("kv_post", functools.partial(_kv_post, n_heads), [pk, pv, pf], [b_f_row],
                                     [(gw, bf16), (gw, bf16), (LANES, f32)])
            cum = _cumsum("forget_cumsum", [log_f], reverse=False)
            c_heads = cum[:, :n_heads].T
            shared = dict(k=sk, v=sv, crow=c_heads.reshape(n_heads, 1, s), h=h, hn=hn, p=p, views=(pk, pv, pf))

    loss_part, dh, d_final = _final_loss("final_loss", h, target, _row(final_norm))
    loss = lax.psum(loss_part[0, 0], ("x", "y", "c"))

    grads = {}
    per_layer = {n: [None] * depth for n in ("ffn1_norm", "mix_norm", "ffn2_norm", "ffn1_gu", "ffn1_d", "ffn2_gu",
                                             "ffn2_d", "w_out")}
    per_a = {n: [None] * n_a for n in ("gdn_w_in", "gdn_conv", "gdn_A_log", "gdn_dt_bias", "gdn_out_norm")}
    per_b = {"fox_w_in": [None] * (depth - n_a)}
    d_mem_kv = [None] * depth
    fox_grads = []

    def ffn_backward(tag, l, h_in, d_out, gain, kept, wgu, wd):
        parts, dwg, dwu, dwd = _ffn_bwd(f"{tag}_bwd_{l}", h_in, d_out, _row(gain), kept[0], kept[1], wgu, wd, l)
        nh = parts.shape[0]
        (d_in,), (d_gain,) = _rowwise_vjp(
            f"{tag}_norm_bwd_{l}", lambda t, g: (_rms(t, g),), [h_in], [_row(gain)],
            [[V(parts, lead=t) for t in range(nh)]], [f32], add=d_out)
        return d_in, d_gain, jnp.concatenate([dwg, dwu], axis=0), dwd

    for l in reversed(range(depth)):
        rec = saved[l]
        if l == n_a - 1:
            dk_list = [V(g["dk"]) for g in fox_grads]
            dv_list = [V(g["dv"]) for g in fox_grads]
            dc_parts = [_pad_lanes(part.reshape(n_heads, s).T, LANES) for g in fox_grads for part in g["dc"]]
            d_log_f = _cumsum("forget_cumsum_bwd", dc_parts, reverse=True)
            pk, pv, pf = shared["views"]
            (dpk, dpv, dpf), (d_bf,) = _rowwise_vjp(
                "kv_post_bwd", functools.partial(_kv_post, n_heads), [pk, pv, pf], [b_f_row],
                [dk_list, dv_list, [d_log_f]], [bf16, bf16, bf16])
            dp = jnp.concatenate([dpk, dpv, dpf], axis=1)
            d_hn = _matmul("kv_proj_dx", dp, wkv, "nt", f32)
            grads["kv_w"] = _matmul("kv_proj_dw", shared["hn"], dp, "tn", f32)[:, :kv_width]
            (dh,), (d_kvn,) = _rowwise_vjp("kv_norm_bwd", lambda t, g: (_rms(t, g),), [shared["h"]], [_row(kv_norm)],
                                           [[d_hn]], [f32], add=dh)
            grads["kv_norm"] = d_kvn.reshape(-1)
            grads["kv_b_f"] = d_bf[0, :n_heads]

        dh2, per_layer["ffn2_norm"][l], per_layer["ffn2_gu"][l], per_layer["ffn2_d"][l] = ffn_backward(
            "ffn2", l, rec["h2"], dh, ffn2_norm[l], rec["ffn2"], wgu2, wd2)
        d_cat = _matmul(f"out_proj_dx_{l}", dh2, wout[l], "nt", f32)
        per_layer["w_out"][l] = _matmul(f"out_proj_dw_{l}", rec["cat"], dh2, "tn", f32)
        d_main = V(d_cat, cb=0, w=gw)
        d_memo = V(d_cat, cb=gw // mem_w, w=mem_w)
        km, vm = V(mem_kv, cb=2 * l, w=mem_w), V(mem_kv, cb=2 * l + 1, w=mem_w)
        (dqmem,), (dkm, dvm) = _rowwise_vjp(f"mem_attn_bwd_{l}", _mem_attn, [rec["qmem"]], [km, vm], [[d_memo]], [bf16])
        d_mem_kv[l] = jnp.concatenate([dkm, dvm], axis=1)
        if l < n_a:
            proj = rec["proj"]
            z_view = V(proj, cb=3, w=gw)
            (d_o, d_z), (d_onorm,) = _rowwise_vjp(
                f"gdn_post_bwd_{l}", functools.partial(_gdn_post, n_heads, head_dim), [rec["o"], z_view],
                [onorm_rows[l]], [[d_main]], [f32, bf16])
            dq, dk, dv, dgb = _gdn_chunk_bwd(f"gdn_chunk_bwd_{l}", rec["q"], rec["k"], rec["v"], rec["gb"],
                                             rec["states"], d_o, n_heads, head_dim)
            ab_view = V(proj, cb=(4 * gw + mem_w) // LANES, w=LANES)
            (d_yc, d_ab), (d_alog, d_dt) = _rowwise_vjp(
                f"gdn_pre_bwd_{l}", functools.partial(_gdn_pre, n_heads, head_dim), [rec["yc"], ab_view],
                [a_log_rows[l], dt_rows[l]], [[dq], [dk], [dv], [dgb]], [f32, bf16])
            d_qkv, d_conv = _conv_bwd(f"conv_bwd_{l}", V(proj, cb=0, w=3 * gw), d_yc, conv_w[l])
            d_proj = jnp.concatenate([d_qkv, d_z, dqmem, d_ab], axis=1)
            du = _matmul(f"gdn_in_dx_{l}", d_proj, win[l], "nt", f32)
            per_a["gdn_w_in"][l] = unpermute_in(_matmul(f"gdn_in_dw_{l}", rec["u"], d_proj, "tn", f32))
            per_a["gdn_conv"][l] = d_conv
            per_a["gdn_A_log"][l] = d_alog[0, :n_heads]
            per_a["gdn_dt_bias"][l] = d_dt[0, :n_heads]
            per_a["gdn_out_norm"][l] = d_onorm[0]
        else:
            proj = rec["proj"]
            sk, sv, crow = shared["k"], shared["v"], shared["crow"]
            dq, delta, dc_col = _fox_bwd_dq(f"fox_dq_{l}", proj, sk, sv, crow, rec["lse"], rec["cat"], d_cat,
                                            n_heads, head_dim)
            dk, dv, dc_row = _fox_bwd_dkv(f"fox_dkv_{l}", proj, sk, sv, crow, rec["lse"], delta, d_cat,
                                          n_heads, head_dim)
            fox_grads.append(dict(dk=dk, dv=dv, dc=(dc_row, dc_col)))
            d_proj = jnp.concatenate([dq, dqmem], axis=1)
            du = _matmul(f"fox_in_dx_{l}", d_proj, wfox[l - n_a], "nt", f32)
            per_b["fox_w_in"][l - n_a] = _matmul(f"fox_in_dw_{l}", rec["u"], d_proj, "tn", f32)
        (dh1,), (d_mix,) = _rowwise_vjp(f"mix_norm_bwd_{l}", lambda t, g: (_rms(t, g),), [rec["h1"]],
                                        [_row(mix_norm[l])], [[du]], [f32], add=dh2)
        per_layer["mix_norm"][l] = d_mix
        dh, per_layer["ffn1_norm"][l], per_layer["ffn1_gu"][l], per_layer["ffn1_d"][l] = ffn_backward(
            "ffn1", l, rec["h0"], dh1, ffn1_norm[l], rec["ffn1"], wgu1, wd1)

    grad_x = dh[None]

    d_mem_kv_cat = jnp.concatenate(d_mem_kv, axis=1)
    d_wmem_cat = _matmul("mem_kv_dw", mem_n, d_mem_kv_cat, "tn", f32)
    d_mem_n = _matmul("mem_kv_dx", d_mem_kv_cat, wmem_cat, "nt", f32)
    _, (d_memnorm,) = _rowwise_vjp("mem_norm_bwd", lambda t, g: (_rms(t, g),), [mem_tokens], [_row(mem_norm)],
                                   [[d_mem_n]], [None])

    def gu_stack(per):
        return jnp.stack(per, axis=1).astype(bf16)

    stacks = dict(
        ffn1_w_gate_up=gu_stack(per_layer["ffn1_gu"]),
        ffn1_w_down=_rows_to_shards(jnp.stack(per_layer["ffn1_d"])).astype(bf16),
        ffn2_w_gate_up=gu_stack(per_layer["ffn2_gu"]),
        ffn2_w_down=_rows_to_shards(jnp.stack(per_layer["ffn2_d"])).astype(bf16),
        gdn_w_in=_rows_to_shards(jnp.stack(per_a["gdn_w_in"])).astype(bf16),
        gdn_conv=jnp.stack(per_a["gdn_conv"]).reshape(n_a, -1, N_DEV, 3 * gw // N_DEV).transpose(2, 0, 1, 3),
        fox_w_in=_rows_to_shards(jnp.stack(per_b["fox_w_in"])).astype(bf16),
        w_out=_rows_to_shards(jnp.stack(per_layer["w_out"])).astype(bf16),
        mem_w_kv=_rows_to_shards(d_wmem_cat.reshape(d, depth, 2 * mem_w).transpose(1, 0, 2)).astype(bf16),
        kv_w=_rows_to_shards(grads["kv_w"][None])[:, 0].astype(bf16),
    )
    received = dict(zip(big_names, _reduce_exchange([stacks[n] for n in big_names])))

    small_shapes = {n: weights[n].shape for n in small_names}
    pack = _SmallPack(small_shapes)
    small_grads = dict(
        ffn1_norm=jnp.concatenate(per_layer["ffn1_norm"], axis=0), mix_norm=jnp.concatenate(per_layer["mix_norm"], axis=0),
        ffn2_norm=jnp.concatenate(per_layer["ffn2_norm"], axis=0), gdn_A_log=jnp.stack(per_a["gdn_A_log"]),
        gdn_dt_bias=jnp.stack(per_a["gdn_dt_bias"]), gdn_out_norm=jnp.stack(per_a["gdn_out_norm"]),
        mem_norm=d_memnorm.reshape(-1), kv_norm=grads["kv_norm"], kv_b_f=grads["kv_b_f"], final_norm=d_final.reshape(-1))
    (small_parts,) = _all_gather("gather_small_grads", [pack.pack(small_grads)])

    out_g, out_d, out_m, out_v = {}, {}, {}, {}
    for n in big_names:
        shape = weights[n].shape
        c = shape[-1]
        parts = received[n].reshape(N_CHIPS, -1, c)
        res = _adamw(f"adamw_{n}", parts, weights[n].reshape(-1, c), mom_m[n].reshape(-1, c), mom_v[n].reshape(-1, c))
        out_g[n], out_d[n], out_m[n], out_v[n] = [r.reshape(shape) for r in res]
    res = _adamw("adamw_small", small_parts, pack.pack({n: weights[n] for n in small_names}),
                 pack.pack({n: mom_m[n] for n in small_names}), pack.pack({n: mom_v[n] for n in small_names}))
    for dst, packed in zip((out_g, out_d, out_m, out_v), res):
        dst.update(pack.unpack(packed))

    return (loss, grad_x, *[out_g[n] for n in names], *[out_d[n] for n in names],
            *[out_m[n] for n in names], *[out_v[n] for n in names])
```

```python
import functools
import math

import jax
import jax.numpy as jnp
from jax import lax
from jax.experimental import pallas as pl
from jax.experimental.pallas import tpu as pltpu

f32 = jnp.float32
bf16 = jnp.bfloat16
SDS = jax.ShapeDtypeStruct
HIGHEST = lax.Precision.HIGHEST

N_DEV = 8
MEM_HEADS = 4
CHUNK = 64
LANES = 128
EPS = 1e-6
NEG_INF = -1e30
ADAM_LR = 0.001
ADAM_B1 = 0.9
ADAM_B2 = 0.999
ADAM_EPS = 1e-08
ADAM_WD = 0.01
ADAM_STEP = 10

ROW_TILE = 256
MM_TILE = 512
FFN_FWD_TILE = 1024
FFN_BWD_TILE = 512
ATT_TILE = 512
ATT_HEADS = 6
CUMSUM_TILE = 256
VMEM_LIMIT = 56 * 1024 * 1024

MESH = pl.DeviceIdType.MESH


def _cp(sem=None):
    return pltpu.CompilerParams(dimension_semantics=sem, vmem_limit_bytes=VMEM_LIMIT)


_DIMS = {"nn": (((1,), (0,)), ((), ())), "nt": (((1,), (1,)), ((), ())), "tn": (((0,), (0,)), ((), ()))}


def _dg(a, b, mode):
    return lax.dot_general(a.astype(bf16), b.astype(bf16), _DIMS[mode], preferred_element_type=f32)


@functools.partial(jax.custom_vjp, nondiff_argnums=(2,))
def _mm(a, b, mode):
    return _dg(a, b, mode)


def _mm_fwd(a, b, mode):
    return _dg(a, b, mode), (a, b)


def _mm_bwd(mode, res, ct):
    a, b = res
    if mode == "nn":
        da, db = _dg(ct, b, "nt"), _dg(a, ct, "tn")
    elif mode == "nt":
        da, db = _dg(ct, b, "nn"), _dg(ct, a, "tn")
    else:
        da, db = _dg(b, ct, "nt"), _dg(a, ct, "nn")
    return da.astype(a.dtype), db.astype(b.dtype)


_mm.defvjp(_mm_fwd, _mm_bwd)


def _split_bf16(x):
    hi = x.astype(bf16)
    return hi, (x - hi.astype(f32)).astype(bf16)


def _dgh(a, b, mode="nn"):
    a_hi, a_lo = _split_bf16(a)
    b_hi, b_lo = _split_bf16(b)
    dims = _DIMS[mode]
    return (lax.dot_general(a_hi, b_hi, dims, preferred_element_type=f32)
            + lax.dot_general(a_hi, b_lo, dims, preferred_element_type=f32)
            + lax.dot_general(a_lo, b_hi, dims, preferred_element_type=f32))


@jax.custom_vjp
def _unit_lower_inverses(lows):
    c = lows[0].shape[0]
    ri = lax.broadcasted_iota(jnp.int32, (c, c), 0)
    ci = lax.broadcasted_iota(jnp.int32, (c, c), 1)
    eye = jnp.where(ri == ci, 1.0, 0.0)
    xs = [-low for low in lows]
    rs = [eye + x for x in xs]
    for _ in range(int(math.log2(c)) - 1):
        xs = [_dgh(x, x) for x in xs]
        rs = [r + _dgh(r, x) for r, x in zip(rs, xs)]
    return tuple(rs)


def _uli_fwd(lows):
    ts = _unit_lower_inverses(lows)
    return ts, ts


def _uli_bwd(ts, cts):
    mids = [_dgh(ct, t, "nt") for t, ct in zip(ts, cts)]
    return (tuple(-_dgh(t, m, "tn") for t, m in zip(ts, mids)),)


_unit_lower_inverses.defvjp(_uli_fwd, _uli_bwd)


@functools.partial(jax.custom_vjp, nondiff_argnums=(1,))
def _split_lanes(x, width):
    return tuple(x[:, i * width:(i + 1) * width] for i in range(x.shape[1] // width))


def _split_fwd(x, width):
    return _split_lanes(x, width), None


def _split_bwd(width, _, cts):
    return (jnp.concatenate(list(cts), axis=1),)


_split_lanes.defvjp(_split_fwd, _split_bwd)


def _sigmoid(x):
    return 1.0 / (1.0 + jnp.exp(-x))


def _silu(x):
    return x * _sigmoid(x)


def _softplus(x):
    return jnp.maximum(x, 0.0) + jnp.log1p(jnp.exp(-jnp.abs(x)))


def _rms(x, gain):
    return x * lax.rsqrt(jnp.mean(x * x, axis=-1, keepdims=True) + EPS) * gain


class V:
    def __init__(self, arr, lead=None, cb=0, w=None):
        self.arr, self.lead, self.cb = arr, lead, cb
        self.w = arr.shape[-1] if w is None else w

    @property
    def rows(self):
        return self.arr.shape[-2]

    def spec(self, tile, order=None):
        lead, cb, w = self.lead, self.cb, self.w
        order = order or (lambda i: i)
        if lead is None:
            return pl.BlockSpec((tile, w), lambda i: (order(i), cb))
        return pl.BlockSpec((None, tile, w), lambda i: (lead, order(i), cb))

    def const_spec(self):
        lead, cb, w, r = self.lead, self.cb, self.w, self.rows
        if lead is None:
            return pl.BlockSpec((r, w), lambda i: (0, cb))
        return pl.BlockSpec((None, r, w), lambda i: (lead, 0, cb))


def _v(a):
    return a if isinstance(a, V) else V(a)


def _rowwise(name, fn, rows, consts, outs, tile=None):
    rows = [_v(r) for r in rows]
    consts = [_v(c) for c in consts]
    s = rows[0].rows
    tile = min(tile or ROW_TILE, s)
    nr, nc = len(rows), len(consts)

    def body(*refs):
        vals = [r[...].astype(f32) for r in refs[:nr + nc]]
        res = fn(*vals)
        for o, val in zip(refs[nr + nc:], res):
            o[...] = val.astype(o.dtype)

    return pl.pallas_call(
        body, name=name, grid=(s // tile,),
        in_specs=[r.spec(tile) for r in rows] + [c.const_spec() for c in consts],
        out_specs=[pl.BlockSpec((tile, w), lambda i: (i, 0)) for w, _ in outs],
        out_shape=[SDS((s, w), dt) for w, dt in outs],
        compiler_params=_cp(("parallel",)),
    )(*[r.arr for r in rows], *[c.arr for c in consts])


def _rowwise_vjp(name, fn, rows, consts, cts, d_rows, add=None, tile=None):
    rows = [_v(r) for r in rows]
    consts = [_v(c) for c in consts]
    cts = [[_v(c) for c in group] for group in cts]
    flat_cts = [c for group in cts for c in group]
    s = rows[0].rows
    tile = min(tile or ROW_TILE, s)
    nr, nc, nt = len(rows), len(consts), len(flat_cts)
    want = [k for k, dt in enumerate(d_rows) if dt is not None]
    has_add = add is not None
    add_v = [_v(add)] if has_add else []

    def body(*refs):
        vals = [r[...].astype(f32) for r in refs[:nr + nc]]
        ct_refs = refs[nr + nc:nr + nc + nt]
        pos = nr + nc + nt
        add_ref = refs[pos] if has_add else None
        pos += 1 if has_add else 0
        drow_refs = refs[pos:pos + len(want)]
        dconst_refs = refs[pos + len(want):]
        ctv, at = [], 0
        for group in cts:
            acc = ct_refs[at][...].astype(f32)
            for r in ct_refs[at + 1:at + len(group)]:
                acc = acc + r[...].astype(f32)
            at += len(group)
            ctv.append(acc)
        _, vjp = jax.vjp(fn, *vals)
        grads = vjp(tuple(ctv))
        for o, k in zip(drow_refs, want):
            g = grads[k]
            if has_add and k == want[0]:
                g = g + add_ref[...].astype(f32)
            o[...] = g.astype(o.dtype)

        @pl.when(pl.program_id(0) == 0)
        def _():
            for o in dconst_refs:
                o[...] = jnp.zeros_like(o)

        for o, g in zip(dconst_refs, grads[nr:]):
            o[...] += g

    outs = pl.pallas_call(
        body, name=name, grid=(s // tile,),
        in_specs=[r.spec(tile) for r in rows] + [c.const_spec() for c in consts]
        + [c.spec(tile) for c in flat_cts] + [a.spec(tile) for a in add_v],
        out_specs=[pl.BlockSpec((tile, rows[k].w), lambda i: (i, 0)) for k in want]
        + [pl.BlockSpec((c.rows, c.w), lambda i: (0, 0)) for c in consts],
        out_shape=[SDS((s, rows[k].w), d_rows[k]) for k in want] + [SDS((c.rows, c.w), f32) for c in consts],
        compiler_params=_cp(("arbitrary",)),
    )(*[r.arr for r in rows], *[c.arr for c in consts], *[c.arr for c in flat_cts], *[a.arr for a in add_v])
    return list(outs[:len(want)]), list(outs[len(want):])


def _pick(n, cap):
    best = None
    for t in range(LANES, min(n, cap) + 1, LANES):
        if n % t == 0:
            best = t
    return best or n


def _matmul(name, a, b, mode, out_dtype, add=None, tn_cap=1280):
    has_add = add is not None
    if mode in ("nn", "nt"):
        m, k = a.shape
        n = b.shape[1] if mode == "nn" else b.shape[0]
        tm = min(MM_TILE, m)
        tn = _pick(n, tn_cap) if k * n * 2 > (8 << 20) else n

        def body(*refs):
            a_ref, b_ref = refs[0], refs[1]
            o_ref = refs[-1]
            acc = _dg(a_ref[...], b_ref[...], mode)
            if has_add:
                acc = acc + refs[2][...].astype(f32)
            o_ref[...] = acc.astype(o_ref.dtype)

        b_spec = pl.BlockSpec((k, tn), lambda i, j: (0, j)) if mode == "nn" else pl.BlockSpec((tn, k), lambda i, j: (j, 0))
        in_specs = [pl.BlockSpec((tm, k), lambda i, j: (i, 0)), b_spec]
        args = [a, b]
        if has_add:
            in_specs.append(pl.BlockSpec((tm, tn), lambda i, j: (i, j)))
            args.append(add)
        return pl.pallas_call(
            body, name=name, grid=(m // tm, n // tn), in_specs=in_specs,
            out_specs=pl.BlockSpec((tm, tn), lambda i, j: (i, j)),
            out_shape=SDS((m, n), out_dtype), compiler_params=_cp(("parallel", "parallel")),
        )(*args)
    kk, m = a.shape
    n = b.shape[1]
    tk = min(MM_TILE, kk)
    tn = _pick(n, tn_cap) if m * n * 4 > (7 << 20) else n

    def body_tn(a_ref, b_ref, o_ref):
        @pl.when(pl.program_id(1) == 0)
        def _():
            o_ref[...] = jnp.zeros_like(o_ref)

        o_ref[...] += _dg(a_ref[...], b_ref[...], "tn")

    return pl.pallas_call(
        body_tn, name=name, grid=(n // tn, kk // tk),
        in_specs=[pl.BlockSpec((tk, m), lambda j, k: (k, 0)), pl.BlockSpec((tk, tn), lambda j, k: (k, j))],
        out_specs=pl.BlockSpec((m, tn), lambda j, k: (0, j)),
        out_shape=SDS((m, n), f32), compiler_params=_cp(("parallel", "arbitrary")),
    )(a, b)


def _ffn_fwd(name, h, gain, wgu, wd, layer):
    s, d = h.shape
    hs = wgu.shape[3]
    nh = wgu.shape[0] // 2
    tm = min(FFN_FWD_TILE, s)

    def body(h_ref, g_ref, wg_ref, wu_ref, wd_ref, o_ref, gate_ref, up_ref, n_scr, acc_scr):
        t = pl.program_id(1)

        @pl.when(t == 0)
        def _():
            n_scr[...] = _rms(h_ref[...], g_ref[...]).astype(bf16)
            acc_scr[...] = jnp.zeros_like(acc_scr)

        n = n_scr[...]
        gate = _dg(n, wg_ref[...], "nn")
        up = _dg(n, wu_ref[...], "nn")
        gate_ref[...] = gate.astype(gate_ref.dtype)
        up_ref[...] = up.astype(up_ref.dtype)
        acc_scr[...] += _dg(_silu(gate) * up, wd_ref[...], "nn")

        @pl.when(t == nh - 1)
        def _():
            o_ref[...] = h_ref[...] + 0.5 * acc_scr[...]

    saved_spec = pl.BlockSpec((None, tm, hs), lambda i, t: (t, i, 0))
    return pl.pallas_call(
        body, name=name, grid=(s // tm, nh),
        in_specs=[
            pl.BlockSpec((tm, d), lambda i, t: (i, 0)),
            pl.BlockSpec((1, d), lambda i, t: (0, 0)),
            pl.BlockSpec((None, None, d, hs), lambda i, t: (t, layer, 0, 0)),
            pl.BlockSpec((None, None, d, hs), lambda i, t: (t + nh, layer, 0, 0)),
            pl.BlockSpec((None, hs, d), lambda i, t: (layer, t, 0)),
        ],
        out_specs=[pl.BlockSpec((tm, d), lambda i, t: (i, 0)), saved_spec, saved_spec],
        out_shape=[SDS((s, d), f32), SDS((nh, s, hs), bf16), SDS((nh, s, hs), bf16)],
        scratch_shapes=[pltpu.VMEM((tm, d), bf16), pltpu.VMEM((tm, d), f32)],
        compiler_params=_cp(("parallel", "arbitrary")),
    )(h, gain, wgu, wgu, wd)


def _ffn_bwd(name, h, dout, gain, gate_s, up_s, wgu, wd, layer):
    s, d = h.shape
    hs = wgu.shape[3]
    nh = wgu.shape[0] // 2
    tm = min(FFN_BWD_TILE, s)

    def body(h_ref, do_ref, g_ref, gate_ref, up_ref, wg_ref, wu_ref, wd_ref, dn_ref, dwg_ref, dwu_ref, dwd_ref):
        @pl.when(pl.program_id(1) == 0)
        def _():
            dwg_ref[...] = jnp.zeros_like(dwg_ref)
            dwu_ref[...] = jnp.zeros_like(dwu_ref)
            dwd_ref[...] = jnp.zeros_like(dwd_ref)

        n = _rms(h_ref[...], g_ref[...]).astype(bf16)
        wg, wu, wdn = wg_ref[...], wu_ref[...], wd_ref[...]
        gate = gate_ref[...].astype(f32)
        up = up_ref[...].astype(f32)
        sg = _sigmoid(gate)
        act = gate * sg
        dy = (0.5 * do_ref[...]).astype(bf16)
        da = _dg(dy, wdn, "nt")
        dup = (da * act).astype(bf16)
        dgate = (da * up * (sg * (1.0 + gate * (1.0 - sg)))).astype(bf16)
        dwd_ref[...] += _dg(act * up, dy, "tn")
        dwg_ref[...] += _dg(n, dgate, "tn")
        dwu_ref[...] += _dg(n, dup, "tn")
        dn_ref[...] = (_dg(dgate, wg, "nt") + _dg(dup, wu, "nt")).astype(dn_ref.dtype)

    return pl.pallas_call(
        body, name=name, grid=(nh, s // tm),
        in_specs=[
            pl.BlockSpec((tm, d), lambda t, i: (i, 0)),
            pl.BlockSpec((tm, d), lambda t, i: (i, 0)),
            pl.BlockSpec((1, d), lambda t, i: (0, 0)),
            pl.BlockSpec((None, tm, hs), lambda t, i: (t, i, 0)),
            pl.BlockSpec((None, tm, hs), lambda t, i: (t, i, 0)),
            pl.BlockSpec((None, None, d, hs), lambda t, i: (t, layer, 0, 0)),
            pl.BlockSpec((None, None, d, hs), lambda t, i: (t + nh, layer, 0, 0)),
            pl.BlockSpec((None, hs, d), lambda t, i: (layer, t, 0)),
        ],
        out_specs=[
            pl.BlockSpec((None, tm, d), lambda t, i: (t, i, 0)),
            pl.BlockSpec((None, d, hs), lambda t, i: (t, 0, 0)),
            pl.BlockSpec((None, d, hs), lambda t, i: (t, 0, 0)),
            pl.BlockSpec((hs, d), lambda t, i: (t, 0)),
        ],
        out_shape=[SDS((nh, s, d), bf16), SDS((nh, d, hs), f32), SDS((nh, d, hs), f32), SDS((nh * hs, d), f32)],
        compiler_params=_cp(("parallel", "arbitrary")),
    )(h, dout, gain, gate_s, up_s, wgu, wgu, wd)


def _conv_fwd(name, x, w):
    x = _v(x)
    s, c = x.rows, x.w
    cw = w.shape[0]
    tile = min(ROW_TILE, s)
    cb = x.cb

    def body(x_ref, halo_ref, w_ref, o_ref, buf):
        first = pl.program_id(0) == 0
        buf[0:8, :] = jnp.where(first, 0.0, halo_ref[...])
        buf[8:8 + tile, :] = x_ref[...]
        acc = w_ref[0:1, :] * buf[pl.ds(8 - cw + 1, tile), :]
        for j in range(1, cw):
            acc = acc + w_ref[j:j + 1, :] * buf[pl.ds(8 - cw + 1 + j, tile), :]
        o_ref[...] = acc

    return pl.pallas_call(
        body, name=name, grid=(s // tile,),
        in_specs=[
            pl.BlockSpec((tile, c), lambda i: (i, cb)),
            pl.BlockSpec((8, c), lambda i: (jnp.maximum(i * (tile // 8) - 1, 0), cb)),
            pl.BlockSpec((cw, c), lambda i: (0, 0)),
        ],
        out_specs=pl.BlockSpec((tile, c), lambda i: (i, 0)),
        out_shape=SDS((s, c), f32),
        scratch_shapes=[pltpu.VMEM((tile + 8, c), f32)],
        compiler_params=_cp(("parallel",)),
    )(x.arr, x.arr, w)


def _conv_bwd(name, x, dy, w):
    x = _v(x)
    s, c = x.rows, x.w
    cw = w.shape[0]
    tile = min(ROW_TILE, s)
    n_tiles = s // tile
    cb = x.cb

    def body(x_ref, xh_ref, dy_ref, dyh_ref, w_ref, dx_ref, dw_ref, xbuf, dbuf):
        i = pl.program_id(0)
        xbuf[0:8, :] = jnp.where(i == 0, 0.0, xh_ref[...])
        xbuf[8:8 + tile, :] = x_ref[...]
        dyv = dy_ref[...]
        dbuf[0:tile, :] = dyv
        dbuf[tile:tile + 8, :] = jnp.where(i == n_tiles - 1, 0.0, dyh_ref[...])

        @pl.when(i == 0)
        def _():
            dw_ref[...] = jnp.zeros_like(dw_ref)

        acc = w_ref[0:1, :] * dbuf[pl.ds(cw - 1, tile), :]
        for j in range(1, cw):
            acc = acc + w_ref[j:j + 1, :] * dbuf[pl.ds(cw - 1 - j, tile), :]
        dx_ref[...] = acc.astype(dx_ref.dtype)
        for j in range(cw):
            dw_ref[j:j + 1, :] += jnp.sum(xbuf[pl.ds(8 - cw + 1 + j, tile), :] * dyv, axis=0, keepdims=True)

    return pl.pallas_call(
        body, name=name, grid=(n_tiles,),
        in_specs=[
            pl.BlockSpec((tile, c), lambda i: (i, cb)),
            pl.BlockSpec((8, c), lambda i: (jnp.maximum(i * (tile // 8) - 1, 0), cb)),
            pl.BlockSpec((tile, c), lambda i: (i, 0)),
            pl.BlockSpec((8, c), lambda i: (jnp.minimum((i + 1) * (tile // 8), s // 8 - 1), 0)),
            pl.BlockSpec((cw, c), lambda i: (0, 0)),
        ],
        out_specs=[pl.BlockSpec((tile, c), lambda i: (i, 0)), pl.BlockSpec((cw, c), lambda i: (0, 0))],
        out_shape=[SDS((s, c), bf16), SDS((cw, c), f32)],
        scratch_shapes=[pltpu.VMEM((tile + 8, c), f32), pltpu.VMEM((tile + 8, c), f32)],
        compiler_params=_cp(("arbitrary",)),
    )(x.arr, x.arr, dy, dy, w)


def _gdn_pre(n_heads, head_dim, yc, ab, a_log, dt_bias):
    gw = n_heads * head_dim
    act = _silu(yc)
    parts = _split_lanes(act, head_dim)
    qs = [p * lax.rsqrt(jnp.sum(p * p, axis=-1, keepdims=True) + EPS) * (head_dim ** -0.5) for p in parts[:n_heads]]
    ks = [p * lax.rsqrt(jnp.sum(p * p, axis=-1, keepdims=True) + EPS) for p in parts[n_heads:2 * n_heads]]
    lane = lax.broadcasted_iota(jnp.int32, ab.shape, 1)
    g = -jnp.exp(a_log) * _softplus(ab + dt_bias)
    gb = jnp.where(lane < n_heads, g, jnp.where(lane < 2 * n_heads, _sigmoid(ab), 0.0))
    del gw
    return (jnp.concatenate(qs, axis=1), jnp.concatenate(ks, axis=1),
            jnp.concatenate(list(parts[2 * n_heads:]), axis=1), gb)


def _gdn_post(n_heads, head_dim, o, z, out_norm):
    parts = _split_lanes(o, head_dim)
    normed = jnp.concatenate([_rms(p, out_norm) for p in parts], axis=1)
    return (normed * _silu(z),)


def _gdn_chunk(n_heads, head_dim, q, k, v, gb, *states):
    c = q.shape[0]
    ri = lax.broadcasted_iota(jnp.int32, (c, c), 0)
    ci = lax.broadcasted_iota(jnp.int32, (c, c), 1)
    incl, strict, diag = ri >= ci, ri > ci, ri == ci
    lane = lax.broadcasted_iota(jnp.int32, gb.shape, 1)
    qs, ks, vs = _split_lanes(q, head_dim), _split_lanes(k, head_dim), _split_lanes(v, head_dim)
    heads = range(n_heads)
    g = [jnp.sum(jnp.where(lane == h, gb, 0.0), axis=1, keepdims=True) for h in heads]
    beta = [jnp.sum(jnp.where(lane == n_heads + h, gb, 0.0), axis=1, keepdims=True) for h in heads]
    g_row = [jnp.sum(jnp.where(diag, g[h], 0.0), axis=0, keepdims=True) for h in heads]
    cg_col = [jnp.sum(jnp.where(incl, g_row[h], 0.0), axis=1, keepdims=True) for h in heads]
    cg_row = [jnp.sum(jnp.where(ri <= ci, g[h], 0.0), axis=0, keepdims=True) for h in heads]
    g_last = [jnp.sum(g[h], axis=0, keepdims=True) for h in heads]
    decay = [jnp.where(incl, jnp.exp(jnp.where(incl, cg_col[h] - cg_row[h], 0.0)), 0.0) for h in heads]
    kb = [ks[h] * beta[h] for h in heads]
    lower = [jnp.where(strict, _mm(kb[h], ks[h], "nt") * decay[h], 0.0) for h in heads]
    eye = jnp.where(diag, 1.0, 0.0)
    off_diag = [t - eye for t in _unit_lower_inverses(tuple(lower))]
    e_col = [jnp.exp(cg_col[h]) for h in heads]
    vb = [vs[h] * beta[h] for h in heads]
    kbg = [kb[h] * e_col[h] for h in heads]
    u = [vb[h] + _mm(off_diag[h], vb[h], "nn") for h in heads]
    w = [kbg[h] + _mm(off_diag[h], kbg[h], "nn") for h in heads]
    qk = [jnp.where(incl, _mm(qs[h], ks[h], "nt") * decay[h], 0.0) for h in heads]
    v_new = [u[h] - _mm(w[h], states[h], "nn") for h in heads]
    inter = [_mm(qs[h] * e_col[h], states[h], "nn") for h in heads]
    outs = [inter[h] + _mm(qk[h], v_new[h], "nn") for h in heads]
    k_tail = [ks[h] * jnp.exp(g_last[h] - cg_col[h]) for h in heads]
    new_states = [states[h] * jnp.exp(g_last[h]) + _mm(k_tail[h], v_new[h], "tn") for h in heads]
    return (jnp.concatenate(outs, axis=1), *new_states)


def _gdn_chunk_fwd(name, q, k, v, gb, n_heads, head_dim):
    s, gw = q.shape
    n = s // CHUNK
    fn = functools.partial(_gdn_chunk, n_heads, head_dim)

    def body(q_ref, k_ref, v_ref, gb_ref, o_ref, st_ref, st_scr):
        @pl.when(pl.program_id(0) == 0)
        def _():
            st_scr[...] = jnp.zeros_like(st_scr)

        st_ref[...] = st_scr[...]
        res = fn(q_ref[...], k_ref[...], v_ref[...], gb_ref[...], *[st_scr[h] for h in range(n_heads)])
        o_ref[...] = res[0]
        for h in range(n_heads):
            st_scr[h] = res[1 + h]

    row = lambda w: pl.BlockSpec((CHUNK, w), lambda i: (i, 0))
    return pl.pallas_call(
        body, name=name, grid=(n,),
        in_specs=[row(gw), row(gw), row(gw), row(LANES)],
        out_specs=[row(gw), pl.BlockSpec((None, n_heads, head_dim, head_dim), lambda i: (i, 0, 0, 0))],
        out_shape=[SDS((s, gw), f32), SDS((n, n_heads, head_dim, head_dim), f32)],
        scratch_shapes=[pltpu.VMEM((n_heads, head_dim, head_dim), f32)],
        compiler_params=_cp(("arbitrary",)),
    )(q, k, v, gb)


def _gdn_chunk_bwd(name, q, k, v, gb, states, d_out, n_heads, head_dim):
    s, gw = q.shape
    n = s // CHUNK
    fn = functools.partial(_gdn_chunk, n_heads, head_dim)

    def body(q_ref, k_ref, v_ref, gb_ref, st_ref, do_ref, dq_ref, dk_ref, dv_ref, dgb_ref, dst_scr):
        @pl.when(pl.program_id(0) == 0)
        def _():
            dst_scr[...] = jnp.zeros_like(dst_scr)

        _, vjp = jax.vjp(fn, q_ref[...], k_ref[...], v_ref[...], gb_ref[...], *[st_ref[h] for h in range(n_heads)])
        grads = vjp((do_ref[...].astype(f32), *[dst_scr[h] for h in range(n_heads)]))
        dq_ref[...] = grads[0]
        dk_ref[...] = grads[1]
        dv_ref[...] = grads[2]
        dgb_ref[...] = grads[3]
        for h in range(n_heads):
            dst_scr[h] = grads[4 + h]

    row = lambda w: pl.BlockSpec((CHUNK, w), lambda i: (n - 1 - i, 0))
    return pl.pallas_call(
        body, name=name, grid=(n,),
        in_specs=[row(gw), row(gw), row(gw), row(LANES),
                  pl.BlockSpec((None, n_heads, head_dim, head_dim), lambda i: (n - 1 - i, 0, 0, 0)), row(gw)],
        out_specs=[row(gw), row(gw), row(gw), row(LANES)],
        out_shape=[SDS((s, gw), f32), SDS((s, gw), f32), SDS((s, gw), f32), SDS((s, LANES), f32)],
        scratch_shapes=[pltpu.VMEM((n_heads, head_dim, head_dim), f32)],
        compiler_params=_cp(("arbitrary",)),
    )(q, k, v, gb, states, d_out)


def _mem_attn(qm, km, vm):
    width = qm.shape[1]
    hd = width // MEM_HEADS
    lane = lax.broadcasted_iota(jnp.int32, (1, width), 1)
    out = jnp.zeros_like(qm)
    for h in range(MEM_HEADS):
        msk = jnp.where((lane >= h * hd) & (lane < (h + 1) * hd), 1.0, 0.0)
        logits = _mm(qm * msk, km, "nt") * (hd ** -0.5)
        p = jnp.exp(logits - jnp.max(logits, axis=-1, keepdims=True))
        p = p / jnp.sum(p, axis=-1, keepdims=True)
        out = out + _mm(p, vm, "nn") * msk
    return (out,)


def _kv_post(n_heads, pk, pv, pf, b_f):
    lane = lax.broadcasted_iota(jnp.int32, pf.shape, 1)
    log_f = jnp.where(lane < n_heads, -_softplus(-(pf + b_f)), 0.0)
    return pk, pv, log_f


def _cumsum(name, xs, reverse):
    s, w = xs[0].shape
    tile = min(CUMSUM_TILE, s)
    n = s // tile

    def body(*refs):
        x_refs, o_ref, carry = refs[:-2], refs[-2], refs[-1]

        @pl.when(pl.program_id(0) == 0)
        def _():
            carry[...] = jnp.zeros_like(carry)

        xv = x_refs[0][...]
        for r in x_refs[1:]:
            xv = xv + r[...]
        ri = lax.broadcasted_iota(jnp.int32, (tile, tile), 0)
        ci = lax.broadcasted_iota(jnp.int32, (tile, tile), 1)
        tri = jnp.where((ri <= ci) if reverse else (ri >= ci), 1.0, 0.0).astype(bf16)
        x1 = xv.astype(bf16)
        r1 = xv - x1.astype(f32)
        x2 = r1.astype(bf16)
        x3 = (r1 - x2.astype(f32)).astype(bf16)
        acc = carry[...] + _dg(tri, x1, "nn") + _dg(tri, x2, "nn") + _dg(tri, x3, "nn")
        o_ref[...] = acc
        carry[...] += jnp.sum(xv, axis=0, keepdims=True)

    order = (lambda i: (n - 1 - i, 0)) if reverse else (lambda i: (i, 0))
    return pl.pallas_call(
        body, name=name, grid=(n,),
        in_specs=[pl.BlockSpec((tile, w), order)] * len(xs), out_specs=pl.BlockSpec((tile, w), order),
        out_shape=SDS((s, w), f32), scratch_shapes=[pltpu.VMEM((1, w), f32)],
        compiler_params=_cp(("arbitrary",)),
    )(*xs)


def _fox_logits(q_ref, k_ref, cr_ref, hh, head_dim, scale, diagonal):
    sl = slice(hh * head_dim, (hh + 1) * head_dim)
    s = _dg(q_ref[:, sl], k_ref[:, sl], "nt") * scale - cr_ref[hh]
    if diagonal:
        tq, tk = s.shape
        ok = lax.broadcasted_iota(jnp.int32, (tq, tk), 1) <= lax.broadcasted_iota(jnp.int32, (tq, tk), 0)
        s = jnp.where(ok, s, NEG_INF)
    return s, sl


def _on_causal_tiles(i, j, fn):
    @pl.when(j < i)
    def _():
        fn(False)

    @pl.when(j == i)
    def _():
        fn(True)


def _fox_fwd(name, q, k, v, crow, n_heads, head_dim):
    s = k.shape[0]
    tq = tk = min(ATT_TILE, s)
    nq, nk = s // tq, s // tk
    scale = head_dim ** -0.5
    hpb = ATT_HEADS
    wb = hpb * head_dim

    def body(q_ref, k_ref, v_ref, cr_ref, o_ref, lse_ref, m_scr, l_scr, acc_scr):
        i, j = pl.program_id(1), pl.program_id(2)

        @pl.when(j == 0)
        def _():
            m_scr[...] = jnp.full_like(m_scr, NEG_INF)
            l_scr[...] = jnp.zeros_like(l_scr)
            acc_scr[...] = jnp.zeros_like(acc_scr)

        def step(diagonal):
            heads = range(hpb)
            sl = [slice(hh * head_dim, (hh + 1) * head_dim) for hh in heads]
            sc = [_fox_logits(q_ref, k_ref, cr_ref, hh, head_dim, scale, diagonal)[0] for hh in heads]
            m_old = [m_scr[hh] for hh in heads]
            m_new = [jnp.maximum(m_old[hh], jnp.max(sc[hh], axis=-1, keepdims=True)) for hh in heads]
            p = [jnp.exp(sc[hh] - m_new[hh]) for hh in heads]
            alpha = [jnp.exp(m_old[hh] - m_new[hh]) for hh in heads]
            pv = [_dg(p[hh], v_ref[:, sl[hh]], "nn") for hh in heads]
            for hh in heads:
                l_scr[hh] = alpha[hh] * l_scr[hh] + jnp.sum(p[hh], axis=-1, keepdims=True)
                acc_scr[:, sl[hh]] = alpha[hh] * acc_scr[:, sl[hh]] + pv[hh]
                m_scr[hh] = m_new[hh]

        _on_causal_tiles(i, j, step)

        @pl.when(j == nk - 1)
        def _():
            for hh in range(hpb):
                sl = slice(hh * head_dim, (hh + 1) * head_dim)
                o_ref[:, sl] = (acc_scr[:, sl] / l_scr[hh]).astype(o_ref.dtype)
                lse_ref[hh] = m_scr[hh] + jnp.log(l_scr[hh])

    return pl.pallas_call(
        body, name=name, grid=(n_heads // hpb, nq, nk),
        in_specs=[
            pl.BlockSpec((tq, wb), lambda g, i, j: (i, g)),
            pl.BlockSpec((tk, wb), lambda g, i, j: (jnp.minimum(j, i), g)),
            pl.BlockSpec((tk, wb), lambda g, i, j: (jnp.minimum(j, i), g)),
            pl.BlockSpec((hpb, 1, tk), lambda g, i, j: (g, 0, jnp.minimum(j, i))),
        ],
        out_specs=[pl.BlockSpec((tq, wb), lambda g, i, j: (i, g)),
                   pl.BlockSpec((hpb, tq, 1), lambda g, i, j: (g, i, 0))],
        out_shape=[SDS((s, n_heads * head_dim), bf16), SDS((n_heads, s, 1), f32)],
        scratch_shapes=[pltpu.VMEM((hpb, tq, 1), f32), pltpu.VMEM((hpb, tq, 1), f32), pltpu.VMEM((tq, wb), f32)],
        compiler_params=_cp(("parallel", "parallel", "arbitrary")),
    )(q, k, v, crow)


def _fox_probs(q_ref, k_ref, v_ref, cr_ref, lse_ref, do_ref, hh, head_dim, scale, diagonal):
    sc, sl = _fox_logits(q_ref, k_ref, cr_ref, hh, head_dim, scale, diagonal)
    return jnp.exp(sc - lse_ref[hh]), _dg(do_ref[:, sl], v_ref[:, sl], "nt"), sl


def _fox_bwd_dq(name, q, k, v, crow, lse, o, do, n_heads, head_dim):
    s = k.shape[0]
    tq = tk = min(ATT_TILE, s)
    nq, nk = s // tq, s // tk
    scale = head_dim ** -0.5
    hpb = ATT_HEADS
    wb = hpb * head_dim

    def body(q_ref, k_ref, v_ref, cr_ref, lse_ref, o_ref, do_ref, dq_ref, delta_ref, dcc_ref, acc_scr):
        i, j = pl.program_id(1), pl.program_id(2)
        heads = range(hpb)

        @pl.when(j == 0)
        def _():
            prod = o_ref[...].astype(f32) * do_ref[...].astype(f32)
            for hh in heads:
                delta_ref[hh] = jnp.sum(prod[:, hh * head_dim:(hh + 1) * head_dim], axis=-1, keepdims=True)
            dcc_ref[...] = jnp.zeros_like(dcc_ref)
            acc_scr[...] = jnp.zeros_like(acc_scr)

        def step(diagonal):
            pd = [_fox_probs(q_ref, k_ref, v_ref, cr_ref, lse_ref, do_ref, hh, head_dim, scale, diagonal) for hh in heads]
            ds = [pd[hh][0] * (pd[hh][1] - delta_ref[hh]) for hh in heads]
            dqs = [_dg(ds[hh], k_ref[:, pd[hh][2]], "nn") for hh in heads]
            for hh in heads:
                dcc_ref[hh] += jnp.sum(ds[hh], axis=-1, keepdims=True)
                acc_scr[:, pd[hh][2]] += dqs[hh]

        _on_causal_tiles(i, j, step)

        @pl.when(j == nk - 1)
        def _():
            dq_ref[...] = (acc_scr[...] * scale).astype(dq_ref.dtype)

    qspec = pl.BlockSpec((tq, wb), lambda g, i, j: (i, g))
    kspec = pl.BlockSpec((tk, wb), lambda g, i, j: (jnp.minimum(j, i), g))
    cspec = pl.BlockSpec((hpb, tq, 1), lambda g, i, j: (g, i, 0))
    return pl.pallas_call(
        body, name=name, grid=(n_heads // hpb, nq, nk),
        in_specs=[qspec, kspec, kspec,
                  pl.BlockSpec((hpb, 1, tk), lambda g, i, j: (g, 0, jnp.minimum(j, i))), cspec, qspec, qspec],
        out_specs=[qspec, cspec, cspec],
        out_shape=[SDS((s, n_heads * head_dim), bf16), SDS((n_heads, s, 1), f32), SDS((n_heads, s, 1), f32)],
        scratch_shapes=[pltpu.VMEM((tq, wb), f32)],
        compiler_params=_cp(("parallel", "parallel", "arbitrary")),
    )(q, k, v, crow, lse, o, do)


def _fox_bwd_dkv(name, q, k, v, crow, lse, delta, do, n_heads, head_dim):
    s = k.shape[0]
    tq = tk = min(ATT_TILE, s)
    nq, nk = s // tq, s // tk
    scale = head_dim ** -0.5
    hpb = ATT_HEADS
    wb = hpb * head_dim

    def body(q_ref, k_ref, v_ref, cr_ref, lse_ref, delta_ref, do_ref, dk_ref, dv_ref, dc_ref):
        j, i = pl.program_id(1), pl.program_id(2)

        @pl.when(i == 0)
        def _():
            dk_ref[...] = jnp.zeros_like(dk_ref)
            dv_ref[...] = jnp.zeros_like(dv_ref)
            dc_ref[...] = jnp.zeros_like(dc_ref)

        def step(diagonal):
            heads = range(hpb)
            pd = [_fox_probs(q_ref, k_ref, v_ref, cr_ref, lse_ref, do_ref, hh, head_dim, scale, diagonal) for hh in heads]
            ds = [pd[hh][0] * (pd[hh][1] - delta_ref[hh]) for hh in heads]
            dvs = [_dg(pd[hh][0], do_ref[:, pd[hh][2]], "tn") for hh in heads]
            dks = [_dg(ds[hh], q_ref[:, pd[hh][2]], "tn") for hh in heads]
            for hh in heads:
                sl = pd[hh][2]
                dv_ref[:, sl] += dvs[hh]
                dk_ref[:, sl] += dks[hh] * scale
                dc_ref[hh] -= jnp.sum(ds[hh], axis=0, keepdims=True)

        _on_causal_tiles(i, j, step)

    qspec = pl.BlockSpec((tq, wb), lambda g, j, i: (jnp.maximum(i, j), g))
    kspec = pl.BlockSpec((tk, wb), lambda g, j, i: (j, g))
    cspec = pl.BlockSpec((hpb, tq, 1), lambda g, j, i: (g, jnp.maximum(i, j), 0))
    rspec = pl.BlockSpec((hpb, 1, tk), lambda g, j, i: (g, 0, j))
    return pl.pallas_call(
        body, name=name, grid=(n_heads // hpb, nk, nq),
        in_specs=[qspec, kspec, kspec, rspec, cspec, cspec, qspec],
        out_specs=[kspec, kspec, rspec],
        out_shape=[SDS((s, n_heads * head_dim), f32), SDS((s, n_heads * head_dim), f32), SDS((n_heads, 1, s), f32)],
        compiler_params=_cp(("parallel", "parallel", "arbitrary")),
    )(q, k, v, crow, lse, delta, do)


def _final_loss(name, h, target, gain):
    s, d = h.shape
    tile = min(ROW_TILE, s)

    def body(h_ref, t_ref, g_ref, loss_ref, dh_ref, dg_ref):
        @pl.when(pl.program_id(0) == 0)
        def _():
            loss_ref[...] = jnp.zeros_like(loss_ref)
            dg_ref[...] = jnp.zeros_like(dg_ref)

        x, g = h_ref[...], g_ref[...]
        rstd = lax.rsqrt(jnp.mean(x * x, axis=-1, keepdims=True) + EPS)
        xhat = x * rstd
        err = xhat * g - t_ref[...]
        row = jnp.sum(err * err, axis=-1, keepdims=True) * (0.5 / d)
        loss_ref[...] += jnp.sum(row, axis=0, keepdims=True)
        dy = err * (1.0 / d)
        dg_ref[...] += jnp.sum(dy * xhat, axis=0, keepdims=True)
        dxhat = dy * g
        dh_ref[...] = rstd * (dxhat - xhat * jnp.mean(dxhat * xhat, axis=-1, keepdims=True))

    return pl.pallas_call(
        body, name=name, grid=(s // tile,),
        in_specs=[pl.BlockSpec((tile, d), lambda i: (i, 0)), pl.BlockSpec((tile, d), lambda i: (i, 0)),
                  pl.BlockSpec((1, d), lambda i: (0, 0))],
        out_specs=[pl.BlockSpec((1, LANES), lambda i: (0, 0)), pl.BlockSpec((tile, d), lambda i: (i, 0)),
                   pl.BlockSpec((1, d), lambda i: (0, 0))],
        out_shape=[SDS((1, LANES), f32), SDS((s, d), f32), SDS((1, d), f32)],
        compiler_params=_cp(("arbitrary",)),
    )(h, target, gain)


def _adamw(name, parts, w, m, v):
    r, c = w.shape
    rb = r
    for cand in (256, 128, 64, 32, 16):
        if r % cand == 0 and cand * c * 4 <= (2 << 20):
            rb = cand
            break
    n_parts = parts.shape[0]

    def body(p_ref, w_ref, m_ref, v_ref, g_out, d_out, m_out, v_out):
        g = p_ref[0].astype(f32)
        for k in range(1, n_parts):
            g = g + p_ref[k].astype(f32)
        m_new = ADAM_B1 * m_ref[...] + (1.0 - ADAM_B1) * g
        v_new = ADAM_B2 * v_ref[...] + (1.0 - ADAM_B2) * (g * g)
        m_hat = m_new / (1.0 - ADAM_B1 ** ADAM_STEP)
        v_hat = v_new / (1.0 - ADAM_B2 ** ADAM_STEP)
        g_out[...] = g
        d_out[...] = -ADAM_LR * (m_hat / (jnp.sqrt(v_hat) + ADAM_EPS) + ADAM_WD * w_ref[...])
        m_out[...] = m_new
        v_out[...] = v_new

    blk = pl.BlockSpec((rb, c), lambda i: (i, 0))
    return pl.pallas_call(
        body, name=name, grid=(r // rb,),
        in_specs=[pl.BlockSpec((n_parts, rb, c), lambda i: (0, i, 0)), blk, blk, blk],
        out_specs=[blk, blk, blk, blk],
        out_shape=[SDS((r, c), f32)] * 4,
        compiler_params=_cp(("parallel",)),
    )(parts, w, m, v)


def _position():
    x, y, c = lax.axis_index("x"), lax.axis_index("y"), lax.axis_index("c")
    return x, y, c


def _all_gather(name, shards):
    n = len(shards)

    def body(*refs):
        ins, outs = refs[:n], refs[n:2 * n]
        send_sems, recv_sems, local_sems = refs[2 * n:]
        x, y, c = _position()
        me, sibling = (x, y, c), (x, y, 1 - c)
        chips = [(1 - x, y), (x, 1 - y), (1 - x, 1 - y)]

        def slot(a, block):
            px, py, pc = block
            return outs[a].at[4 * px + 2 * py + pc]

        def copy(a, k, block, to, src=None):
            return pltpu.make_async_remote_copy(
                src_ref=slot(a, block) if src is None else src, dst_ref=slot(a, block),
                send_sem=send_sems.at[a, k], recv_sem=recv_sems.at[a, k], device_id=to, device_id_type=MESH)

        local = [pltpu.make_async_copy(ins[a], slot(a, me), local_sems.at[a]) for a in range(n)]
        for cp in local:
            cp.start()
        started = []
        for a in range(n):
            first = [copy(a, 0, me, sibling, src=ins[a])]
            first += [copy(a, 1 + j, me, (*chip, c), src=ins[a]) for j, chip in enumerate(chips)]
            for cp in first:
                cp.start()
            started += first
        for a in range(n):
            for j, chip in enumerate(chips):
                copy(a, 1 + j, (*chip, c), me).wait_recv()
                passed = copy(a, 4 + j, (*chip, c), sibling)
                passed.start()
                started.append(passed)
        for a in range(n):
            copy(a, 0, sibling, me).wait_recv()
            for j, chip in enumerate(chips):
                copy(a, 4 + j, (*chip, 1 - c), me).wait_recv()
        for cp in started:
            cp.wait_send()
        for cp in local:
            cp.wait()

    any_spec = pl.BlockSpec(memory_space=pl.ANY)
    outs = pl.pallas_call(
        body, name=name,
        in_specs=[any_spec] * n, out_specs=[any_spec] * n,
        out_shape=[SDS((N_DEV, *a.shape), a.dtype) for a in shards],
        scratch_shapes=[pltpu.SemaphoreType.DMA((n, 7)), pltpu.SemaphoreType.DMA((n, 7)), pltpu.SemaphoreType.DMA((n,))],
    )(*shards)
    return list(outs)


N_CHIPS = N_DEV // 2


def _sibling_exchange(name, arrs):
    n = len(arrs)

    def body(*refs):
        ins, outs = refs[:n], refs[n:2 * n]
        send_sems, recv_sems = refs[2 * n:]
        x, y, c = _position()
        copies = [pltpu.make_async_remote_copy(
            src_ref=ins[a], dst_ref=outs[a], send_sem=send_sems.at[a], recv_sem=recv_sems.at[a],
            device_id=(x, y, 1 - c), device_id_type=MESH) for a in range(n)]
        for cp in copies:
            cp.start()
        for cp in copies:
            cp.wait_recv()
        for cp in copies:
            cp.wait_send()

    any_spec = pl.BlockSpec(memory_space=pl.ANY)
    outs = pl.pallas_call(
        body, name=name,
        in_specs=[any_spec] * n, out_specs=[any_spec] * n, out_shape=[SDS(a.shape, a.dtype) for a in arrs],
        scratch_shapes=[pltpu.SemaphoreType.DMA((n,)), pltpu.SemaphoreType.DMA((n,))],
    )(*arrs)
    return list(outs)


def _chip_exchange(name, parts):
    n = len(parts)

    def body(*refs):
        ins, outs = refs[:n], refs[n:2 * n]
        send_sems, recv_sems, local_sems = refs[2 * n:]
        x, y, c = _position()
        my_chip = 2 * x + y
        local = [pltpu.make_async_copy(ins[a].at[my_chip], outs[a].at[my_chip], local_sems.at[a]) for a in range(n)]
        for cp in local:
            cp.start()
        sends, recvs = [], []
        for a in range(n):
            for k in range(1, N_CHIPS):
                px = (1 - x) if (k >> 1) & 1 else x
                py = (1 - y) if k & 1 else y
                peer_chip = 2 * px + py
                send = pltpu.make_async_remote_copy(
                    src_ref=ins[a].at[peer_chip], dst_ref=outs[a].at[my_chip],
                    send_sem=send_sems.at[a, k - 1], recv_sem=recv_sems.at[a, k - 1],
                    device_id=(px, py, c), device_id_type=MESH)
                send.start()
                sends.append(send)
                recvs.append(pltpu.make_async_remote_copy(
                    src_ref=ins[a].at[peer_chip], dst_ref=outs[a].at[peer_chip],
                    send_sem=send_sems.at[a, k - 1], recv_sem=recv_sems.at[a, k - 1],
                    device_id=(px, py, c), device_id_type=MESH))
        for cp in recvs:
            cp.wait_recv()
        for cp in sends:
            cp.wait_send()
        for cp in local:
            cp.wait()

    any_spec = pl.BlockSpec(memory_space=pl.ANY)
    outs = pl.pallas_call(
        body, name=name,
        in_specs=[any_spec] * n, out_specs=[any_spec] * n,
        out_shape=[SDS(a.shape, a.dtype) for a in parts],
        scratch_shapes=[pltpu.SemaphoreType.DMA((n, N_CHIPS - 1)), pltpu.SemaphoreType.DMA((n, N_CHIPS - 1)),
                        pltpu.SemaphoreType.DMA((n,))],
    )(*parts)
    return list(outs)


def _reduce_exchange(stacks):
    core = lax.axis_index("c")
    by_core = [st.reshape(N_CHIPS, 2, *st.shape[1:]) for st in stacks]
    mine = [lax.dynamic_index_in_dim(v, core, axis=1, keepdims=False) for v in by_core]
    theirs = _sibling_exchange(
        "exchange_sibling", [lax.dynamic_index_in_dim(v, 1 - core, axis=1, keepdims=False) for v in by_core])
    summed = []
    for a, (m, t) in enumerate(zip(mine, theirs)):
        cols = m.shape[-1]
        (both,) = _rowwise(f"exchange_add_{a}", lambda p, q: (p + q,), [m.reshape(-1, cols), t.reshape(-1, cols)], [],
                           [(cols, m.dtype)])
        summed.append(both.reshape(m.shape))
    return _chip_exchange("exchange_chips", summed)


def _rows_from_shards(g):
    n, l, r, c = g.shape
    return g.transpose(1, 0, 2, 3).reshape(l, n * r, c)


def _rows_to_shards(w):
    l, rows, c = w.shape
    return w.reshape(l, N_DEV, rows // N_DEV, c).transpose(1, 0, 2, 3)


def _pad_lanes(a, width):
    return jnp.pad(a, [(0, 0)] * (a.ndim - 1) + [(0, width - a.shape[-1])])


def _row(vec, width=None):
    vec = vec.reshape(1, -1)
    return vec if width is None else _pad_lanes(vec, width)


class _SmallPack:
    def __init__(self, shapes):
        self.shapes, self.offsets, at = shapes, {}, 0
        for name, shape in shapes.items():
            last = shape[-1]
            lead = int(math.prod(shape[:-1]))
            rows = lead * (last // LANES) if last >= LANES else lead
            self.offsets[name] = (at, rows)
            at += rows
        self.rows = -(-at // 8) * 8

    def pack(self, values):
        pieces = []
        for name, shape in self.shapes.items():
            val = values[name].astype(f32)
            if shape[-1] >= LANES:
                pieces.append(val.reshape(-1, LANES))
            else:
                pieces.append(_pad_lanes(val.reshape(-1, shape[-1]), LANES))
        used = sum(p.shape[0] for p in pieces)
        if used < self.rows:
            pieces.append(jnp.zeros((self.rows - used, LANES), f32))
        return jnp.concatenate(pieces, axis=0)

    def unpack(self, packed):
        out = {}
        for name, shape in self.shapes.items():
            at, rows = self.offsets[name]
            blk = packed[at:at + rows]
            out[name] = blk.reshape(shape) if shape[-1] >= LANES else blk[:, :shape[-1]].reshape(shape)
        return out


def kernel(x, mem, ffn1_norm, ffn1_w_gate_up, ffn1_w_down, mix_norm, ffn2_norm, ffn2_w_gate_up, ffn2_w_down, gdn_w_in, gdn_conv, gdn_A_log, gdn_dt_bias, gdn_out_norm, fox_w_in, w_out, mem_norm, mem_w_kv, kv_norm, kv_w, kv_b_f, final_norm, loss_target, m_ffn1_norm, m_ffn1_w_gate_up, m_ffn1_w_down, m_mix_norm, m_ffn2_norm, m_ffn2_w_gate_up, m_ffn2_w_down, m_gdn_w_in, m_gdn_conv, m_gdn_A_log, m_gdn_dt_bias, m_gdn_out_norm, m_fox_w_in, m_w_out, m_mem_norm, m_mem_w_kv, m_kv_norm, m_kv_w, m_kv_b_f, m_final_norm, v_ffn1_norm, v_ffn1_w_gate_up, v_ffn1_w_down, v_mix_norm, v_ffn2_norm, v_ffn2_w_gate_up, v_ffn2_w_down, v_gdn_w_in, v_gdn_conv, v_gdn_A_log, v_gdn_dt_bias, v_gdn_out_norm, v_fox_w_in, v_w_out, v_mem_norm, v_mem_w_kv, v_kv_norm, v_kv_w, v_kv_b_f, v_final_norm):
    weights = dict(ffn1_norm=ffn1_norm, ffn1_w_gate_up=ffn1_w_gate_up, ffn1_w_down=ffn1_w_down, mix_norm=mix_norm,
                   ffn2_norm=ffn2_norm, ffn2_w_gate_up=ffn2_w_gate_up, ffn2_w_down=ffn2_w_down, gdn_w_in=gdn_w_in,
                   gdn_conv=gdn_conv, gdn_A_log=gdn_A_log, gdn_dt_bias=gdn_dt_bias, gdn_out_norm=gdn_out_norm,
                   fox_w_in=fox_w_in, w_out=w_out, mem_norm=mem_norm, mem_w_kv=mem_w_kv, kv_norm=kv_norm, kv_w=kv_w,
                   kv_b_f=kv_b_f, final_norm=final_norm)
    mom_m = dict(ffn1_norm=m_ffn1_norm, ffn1_w_gate_up=m_ffn1_w_gate_up, ffn1_w_down=m_ffn1_w_down, mix_norm=m_mix_norm,
                 ffn2_norm=m_ffn2_norm, ffn2_w_gate_up=m_ffn2_w_gate_up, ffn2_w_down=m_ffn2_w_down, gdn_w_in=m_gdn_w_in,
                 gdn_conv=m_gdn_conv, gdn_A_log=m_gdn_A_log, gdn_dt_bias=m_gdn_dt_bias, gdn_out_norm=m_gdn_out_norm,
                 fox_w_in=m_fox_w_in, w_out=m_w_out, mem_norm=m_mem_norm, mem_w_kv=m_mem_w_kv, kv_norm=m_kv_norm,
                 kv_w=m_kv_w, kv_b_f=m_kv_b_f, final_norm=m_final_norm)
    mom_v = dict(ffn1_norm=v_ffn1_norm, ffn1_w_gate_up=v_ffn1_w_gate_up, ffn1_w_down=v_ffn1_w_down, mix_norm=v_mix_norm,
                 ffn2_norm=v_ffn2_norm, ffn2_w_gate_up=v_ffn2_w_gate_up, ffn2_w_down=v_ffn2_w_down, gdn_w_in=v_gdn_w_in,
                 gdn_conv=v_gdn_conv, gdn_A_log=v_gdn_A_log, gdn_dt_bias=v_gdn_dt_bias, gdn_out_norm=v_gdn_out_norm,
                 fox_w_in=v_fox_w_in, w_out=v_w_out, mem_norm=v_mem_norm, mem_w_kv=v_mem_w_kv, kv_norm=v_kv_norm,
                 kv_w=v_kv_w, kv_b_f=v_kv_b_f, final_norm=v_final_norm)
    names = list(weights)
    small_names = [n for n in names if weights[n].shape == mom_m[n].shape and n in (
        "ffn1_norm", "mix_norm", "ffn2_norm", "gdn_A_log", "gdn_dt_bias", "gdn_out_norm", "mem_norm", "kv_norm",
        "kv_b_f", "final_norm")]
    big_names = [n for n in names if n not in small_names]

    h = x[0]
    target = loss_target[0]
    mem_tokens = mem[0]
    s, d = h.shape
    depth = ffn1_norm.shape[0]
    n_a = gdn_w_in.shape[0]
    n_heads, head_dim = gdn_A_log.shape[1], gdn_out_norm.shape[1]
    gw = n_heads * head_dim
    a_in = gdn_w_in.shape[2]
    mem_w = a_in - 4 * gw - 2 * n_heads
    a_in_pad = 4 * gw + mem_w + LANES
    kv_width = kv_w.shape[1]
    kv_pad = 2 * gw + LANES
    fh = ffn1_w_down.shape[1] * N_DEV

    def permute_in(w):
        ab = w[..., 4 * gw:4 * gw + 2 * n_heads]
        return jnp.concatenate([w[..., :4 * gw], w[..., 4 * gw + 2 * n_heads:], _pad_lanes(ab, LANES)], axis=-1)

    def unpermute_in(w):
        return jnp.concatenate([w[..., :4 * gw], w[..., 4 * gw + mem_w:4 * gw + mem_w + 2 * n_heads],
                                w[..., 4 * gw:4 * gw + mem_w]], axis=-1)

    gathered = _all_gather("gather_weights", [
        ffn1_w_gate_up.astype(bf16), ffn1_w_down.astype(bf16), ffn2_w_gate_up.astype(bf16), ffn2_w_down.astype(bf16),
        permute_in(gdn_w_in).astype(bf16), fox_w_in.astype(bf16), w_out.astype(bf16), mem_w_kv.astype(bf16),
        _pad_lanes(kv_w, kv_pad).astype(bf16)[None], gdn_conv])
    wgu1, wd1_s, wgu2, wd2_s, win_s, wfox_s, wout_s, wmem_s, wkv_s, conv_s = gathered
    wd1, wd2 = _rows_from_shards(wd1_s), _rows_from_shards(wd2_s)
    win, wfox, wout = _rows_from_shards(win_s), _rows_from_shards(wfox_s), _rows_from_shards(wout_s)
    wmem = _rows_from_shards(wmem_s)
    wmem_cat = wmem.transpose(1, 0, 2).reshape(d, depth * 2 * mem_w)
    wkv = _rows_from_shards(wkv_s)[0]
    conv_w = conv_s.transpose(1, 2, 0, 3).reshape(n_a, gdn_conv.shape[1], 3 * gw)

    a_log_rows = [_row(gdn_A_log[l], LANES) for l in range(n_a)]
    dt_rows = [_row(gdn_dt_bias[l], LANES) for l in range(n_a)]
    onorm_rows = [_row(gdn_out_norm[l]) for l in range(n_a)]
    b_f_row = _row(kv_b_f, LANES)

    (mem_n,) = _rowwise("mem_norm", lambda t, g: (_rms(t, g),), [mem_tokens], [_row(mem_norm)], [(d, bf16)])
    mem_kv = _matmul("mem_kv", mem_n, wmem_cat, "nn", f32)

    saved = []
    shared = None
    for l in range(depth):
        rec = {"h0": h}
        h1, gate1, up1 = _ffn_fwd(f"ffn1_fwd_{l}", h, _row(ffn1_norm[l]), wgu1, wd1, l)
        (u,) = _rowwise(f"mix_norm_{l}", lambda t, g: (_rms(t, g),), [h1], [_row(mix_norm[l])], [(d, bf16)])
        rec.update(h1=h1, u=u, ffn1=(gate1, up1))
        if l < n_a:
            proj = _matmul(f"gdn_in_{l}", u, win[l], "nn", f32)
            yc = _conv_fwd(f"conv_fwd_{l}", V(proj, cb=0, w=3 * gw), conv_w[l])
            ab_view = V(proj, cb=(4 * gw + mem_w) // LANES, w=LANES)
            q, k, v, gb = _rowwise(f"gdn_pre_{l}", functools.partial(_gdn_pre, n_heads, head_dim),
                                   [yc, ab_view], [a_log_rows[l], dt_rows[l]],
                                   [(gw, f32), (gw, f32), (gw, f32), (LANES, f32)])
            o, states = _gdn_chunk_fwd(f"gdn_chunk_fwd_{l}", q, k, v, gb, n_heads, head_dim)
            z_view = V(proj, cb=3, w=gw)
            (main,) = _rowwise(f"gdn_post_{l}", functools.partial(_gdn_post, n_heads, head_dim),
                               [o, z_view], [onorm_rows[l]], [(gw, bf16)])
            qmem_view = V(proj, cb=4 * gw // mem_w, w=mem_w)
            rec.update(proj=proj, yc=yc, q=q, k=k, v=v, gb=gb, o=o, states=states)
        else:
            proj = _matmul(f"fox_in_{l}", u, wfox[l - n_a], "nn", bf16)
            sk, sv, crow = shared["k"], shared["v"], shared["crow"]
            main, lse = _fox_fwd(f"fox_fwd_{l}", proj, sk, sv, crow, n_heads, head_dim)
            qmem_view = V(proj, cb=gw // mem_w, w=mem_w)
            rec.update(proj=proj, lse=lse)
        km = V(mem_kv, cb=2 * l, w=mem_w)
        vm = V(mem_kv, cb=2 * l + 1, w=mem_w)
        (mem_out,) = _rowwise(f"mem_attn_{l}", _mem_attn, [qmem_view], [km, vm], [(mem_w, bf16)])
        cat = jnp.concatenate([main, mem_out], axis=1)
        h2 = _matmul(f"out_proj_{l}", cat, wout[l], "nn", f32, add=h1)
        h3, gate2, up2 = _ffn_fwd(f"ffn2_fwd_{l}", h2, _row(ffn2_norm[l]), wgu2, wd2, l)
        rec.update(cat=cat, h2=h2, qmem=qmem_view, ffn2=(gate2, up2))
        saved.append(rec)
        h = h3
        if l == n_a - 1:
            (hn,) = _rowwise("kv_norm", lambda t, g: (_rms(t, g),), [h], [_row(kv_norm)], [(d, bf16)])
            p = _matmul("kv_proj", hn, wkv, "nn", f32)
            pk, pv, pf = V(p, cb=0, w=gw), V(p, cb=1, w=gw), V(p, cb=2 * gw // LANES, w=LANES)
            sk, sv, log_f = _rowwise("kv_post", functools.partial(_kv_post, n_heads), [pk, pv, pf], [b_f_row],
                                     [(gw, bf16), (gw, bf16), (LANES, f32)])
            cum = _cumsum("forget_cumsum", [log_f], reverse=False)
            c_heads = cum[:, :n_heads].T
            shared = dict(k=sk, v=sv, crow=c_heads.reshape(n_heads, 1, s), h=h, hn=hn, p=p, views=(pk, pv, pf))

    loss_part, dh, d_final = _final_loss("final_loss", h, target, _row(final_norm))
    loss = lax.psum(loss_part[0, 0], ("x", "y", "c"))

    grads = {}
    per_layer = {n: [None] * depth for n in ("ffn1_norm", "mix_norm", "ffn2_norm", "ffn1_gu", "ffn1_d", "ffn2_gu",
                                             "ffn2_d", "w_out")}
    per_a = {n: [None] * n_a for n in ("gdn_w_in", "gdn_conv", "gdn_A_log", "gdn_dt_bias", "gdn_out_norm")}
    per_b = {"fox_w_in": [None] * (depth - n_a)}
    d_mem_kv = [None] * depth
    fox_grads = []

    def ffn_backward(tag, l, h_in, d_out, gain, kept, wgu, wd):
        parts, dwg, dwu, dwd = _ffn_bwd(f"{tag}_bwd_{l}", h_in, d_out, _row(gain), kept[0], kept[1], wgu, wd, l)
        nh = parts.shape[0]
        (d_in,), (d_gain,) = _rowwise_vjp(
            f"{tag}_norm_bwd_{l}", lambda t, g: (_rms(t, g),), [h_in], [_row(gain)],
            [[V(parts, lead=t) for t in range(nh)]], [f32], add=d_out)
        return d_in, d_gain, jnp.concatenate([dwg, dwu], axis=0), dwd

    for l in reversed(range(depth)):
        rec = saved[l]
        if l == n_a - 1:
            dk_list = [V(g["dk"]) for g in fox_grads]
            dv_list = [V(g["dv"]) for g in fox_grads]
            dc_parts = [_pad_lanes(part.reshape(n_heads, s).T, LANES) for g in fox_grads for part in g["dc"]]
            d_log_f = _cumsum("forget_cumsum_bwd", dc_parts, reverse=True)
            pk, pv, pf = shared["views"]
            (dpk, dpv, dpf), (d_bf,) = _rowwise_vjp(
                "kv_post_bwd", functools.partial(_kv_post, n_heads), [pk, pv, pf], [b_f_row],
                [dk_list, dv_list, [d_log_f]], [bf16, bf16, bf16])
            dp = jnp.concatenate([dpk, dpv, dpf], axis=1)
            d_hn = _matmul("kv_proj_dx", dp, wkv, "nt", f32)
            grads["kv_w"] = _matmul("kv_proj_dw", shared["hn"], dp, "tn", f32)[:, :kv_width]
            (dh,), (d_kvn,) = _rowwise_vjp("kv_norm_bwd", lambda t, g: (_rms(t, g),), [shared["h"]], [_row(kv_norm)],
                                           [[d_hn]], [f32], add=dh)
            grads["kv_norm"] = d_kvn.reshape(-1)
            grads["kv_b_f"] = d_bf[0, :n_heads]

        dh2, per_layer["ffn2_norm"][l], per_layer["ffn2_gu"][l], per_layer["ffn2_d"][l] = ffn_backward(
            "ffn2", l, rec["h2"], dh, ffn2_norm[l], rec["ffn2"], wgu2, wd2)
        d_cat = _matmul(f"out_proj_dx_{l}", dh2, wout[l], "nt", f32)
        per_layer["w_out"][l] = _matmul(f"out_proj_dw_{l}", rec["cat"], dh2, "tn", f32)
        d_main = V(d_cat, cb=0, w=gw)
        d_memo = V(d_cat, cb=gw // mem_w, w=mem_w)
        km, vm = V(mem_kv, cb=2 * l, w=mem_w), V(mem_kv, cb=2 * l + 1, w=mem_w)
        (dqmem,), (dkm, dvm) = _rowwise_vjp(f"mem_attn_bwd_{l}", _mem_attn, [rec["qmem"]], [km, vm], [[d_memo]], [bf16])
        d_mem_kv[l] = jnp.concatenate([dkm, dvm], axis=1)
        if l < n_a:
            proj = rec["proj"]
            z_view = V(proj, cb=3, w=gw)
            (d_o, d_z), (d_onorm,) = _rowwise_vjp(
                f"gdn_post_bwd_{l}", functools.partial(_gdn_post, n_heads, head_dim), [rec["o"], z_view],
                [onorm_rows[l]], [[d_main]], [f32, bf16])
            dq, dk, dv, dgb = _gdn_chunk_bwd(f"gdn_chunk_bwd_{l}", rec["q"], rec["k"], rec["v"], rec["gb"],
                                             rec["states"], d_o, n_heads, head_dim)
            ab_view = V(proj, cb=(4 * gw + mem_w) // LANES, w=LANES)
            (d_yc, d_ab), (d_alog, d_dt) = _rowwise_vjp(
                f"gdn_pre_bwd_{l}", functools.partial(_gdn_pre, n_heads, head_dim), [rec["yc"], ab_view],
                [a_log_rows[l], dt_rows[l]], [[dq], [dk], [dv], [dgb]], [f32, bf16])
            d_qkv, d_conv = _conv_bwd(f"conv_bwd_{l}", V(proj, cb=0, w=3 * gw), d_yc, conv_w[l])
            d_proj = jnp.concatenate([d_qkv, d_z, dqmem, d_ab], axis=1)
            du = _matmul(f"gdn_in_dx_{l}", d_proj, win[l], "nt", f32)
            per_a["gdn_w_in"][l] = unpermute_in(_matmul(f"gdn_in_dw_{l}", rec["u"], d_proj, "tn", f32))
            per_a["gdn_conv"][l] = d_conv
            per_a["gdn_A_log"][l] = d_alog[0, :n_heads]
            per_a["gdn_dt_bias"][l] = d_dt[0, :n_heads]
            per_a["gdn_out_norm"][l] = d_onorm[0]
        else:
            proj = rec["proj"]
            sk, sv, crow = shared["k"], shared["v"], shared["crow"]
            dq, delta, dc_col = _fox_bwd_dq(f"fox_dq_{l}", proj, sk, sv, crow, rec["lse"], rec["cat"], d_cat,
                                            n_heads, head_dim)
            dk, dv, dc_row = _fox_bwd_dkv(f"fox_dkv_{l}", proj, sk, sv, crow, rec["lse"], delta, d_cat,
                                          n_heads, head_dim)
            fox_grads.append(dict(dk=dk, dv=dv, dc=(dc_row, dc_col)))
            d_proj = jnp.concatenate([dq, dqmem], axis=1)
            du = _matmul(f"fox_in_dx_{l}", d_proj, wfox[l - n_a], "nt", f32)
            per_b["fox_w_in"][l - n_a] = _matmul(f"fox_in_dw_{l}", rec["u"], d_proj, "tn", f32)
        (dh1,), (d_mix,) = _rowwise_vjp(f"mix_norm_bwd_{l}", lambda t, g: (_rms(t, g),), [rec["h1"]],
                                        [_row(mix_norm[l])], [[du]], [f32], add=dh2)
        per_layer["mix_norm"][l] = d_mix
        dh, per_layer["ffn1_norm"][l], per_layer["ffn1_gu"][l], per_layer["ffn1_d"][l] = ffn_backward(
            "ffn1", l, rec["h0"], dh1, ffn1_norm[l], rec["ffn1"], wgu1, wd1)

    grad_x = dh[None]

    d_mem_kv_cat = jnp.concatenate(d_mem_kv, axis=1)
    d_wmem_cat = _matmul("mem_kv_dw", mem_n, d_mem_kv_cat, "tn", f32)
    d_mem_n = _matmul("mem_kv_dx", d_mem_kv_cat, wmem_cat, "nt", f32)
    _, (d_memnorm,) = _rowwise_vjp("mem_norm_bwd", lambda t, g: (_rms(t, g),), [mem_tokens], [_row(mem_norm)],
                                   [[d_mem_n]], [None])

    def gu_stack(per):
        return jnp.stack(per, axis=1).astype(bf16)

    stacks = dict(
        ffn1_w_gate_up=gu_stack(per_layer["ffn1_gu"]),
        ffn1_w_down=_rows_to_shards(jnp.stack(per_layer["ffn1_d"])).astype(bf16),
        ffn2_w_gate_up=gu_stack(per_layer["ffn2_gu"]),
        ffn2_w_down=_rows_to_shards(jnp.stack(per_layer["ffn2_d"])).astype(bf16),
        gdn_w_in=_rows_to_shards(jnp.stack(per_a["gdn_w_in"])).astype(bf16),
        gdn_conv=jnp.stack(per_a["gdn_conv"]).reshape(n_a, -1, N_DEV, 3 * gw // N_DEV).transpose(2, 0, 1, 3),
        fox_w_in=_rows_to_shards(jnp.stack(per_b["fox_w_in"])).astype(bf16),
        w_out=_rows_to_shards(jnp.stack(per_layer["w_out"])).astype(bf16),
        mem_w_kv=_rows_to_shards(d_wmem_cat.reshape(d, depth, 2 * mem_w).transpose(1, 0, 2)).astype(bf16),
        kv_w=_rows_to_shards(grads["kv_w"][None])[:, 0].astype(bf16),
    )
    received = dict(zip(big_names, _reduce_exchange([stacks[n] for n in big_names])))

    small_shapes = {n: weights[n].shape for n in small_names}
    pack = _SmallPack(small_shapes)
    small_grads = dict(
        ffn1_norm=jnp.concatenate(per_layer["ffn1_norm"], axis=0), mix_norm=jnp.concatenate(per_layer["mix_norm"], axis=0),
        ffn2_norm=jnp.concatenate(per_layer["ffn2_norm"], axis=0), gdn_A_log=jnp.stack(per_a["gdn_A_log"]),
        gdn_dt_bias=jnp.stack(per_a["gdn_dt_bias"]), gdn_out_norm=jnp.stack(per_a["gdn_out_norm"]),
        mem_norm=d_memnorm.reshape(-1), kv_norm=grads["kv_norm"], kv_b_f=grads["kv_b_f"], final_norm=d_final.reshape(-1))
    (small_parts,) = _all_gather("gather_small_grads", [pack.pack(small_grads)])

    out_g, out_d, out_m, out_v = {}, {}, {}, {}
    for n in big_names:
        shape = weights[n].shape
        c = shape[-1]
        parts = received[n].reshape(N_CHIPS, -1, c)
        res = _adamw(f"adamw_{n}", parts, weights[n].reshape(-1, c), mom_m[n].reshape(-1, c), mom_v[n].reshape(-1, c))
        out_g[n], out_d[n], out_m[n], out_v[n] = [r.reshape(shape) for r in res]
    res = _adamw("adamw_small", small_parts, pack.pack({n: weights[n] for n in small_names}),
                 pack.pack({n: mom_m[n] for n in small_names}), pack.pack({n: mom_v[n] for n in small_names}))
    for dst, packed in zip((out_g, out_d, out_m, out_v), res):
        dst.update(pack.unpack(packed))

    return (loss, grad_x, *[out_g[n] for n in names], *[out_d[n] for n in names],
            *[out_m[n] for n in names], *[out_v[n] for n in names])
```

```python
import functools
import math

import jax
import jax.numpy as jnp
from jax import lax
from jax.experimental import pallas as pl
from jax.experimental.pallas import tpu as pltpu

f32 = jnp.float32
bf16 = jnp.bfloat16
SDS = jax.ShapeDtypeStruct
HIGHEST = lax.Precision.HIGHEST

N_DEV = 8
MEM_HEADS = 4
CHUNK = 64
LANES = 128
EPS = 1e-6
NEG_INF = -1e30
ADAM_LR = 0.001
ADAM_B1 = 0.9
ADAM_B2 = 0.999
ADAM_EPS = 1e-08
ADAM_WD = 0.01
ADAM_STEP = 10

ROW_TILE = 256
MM_TILE = 512
FFN_FWD_TILE = 1024
FFN_BWD_TILE = 512
ATT_TILE = 1024
ATT_HEADS = 3
CUMSUM_TILE = 256
VMEM_LIMIT = 56 * 1024 * 1024

MESH = pl.DeviceIdType.MESH


def _cp(sem=None):
    return pltpu.CompilerParams(dimension_semantics=sem, vmem_limit_bytes=VMEM_LIMIT)


_DIMS = {"nn": (((1,), (0,)), ((), ())), "nt": (((1,), (1,)), ((), ())), "tn": (((0,), (0,)), ((), ()))}


def _dg(a, b, mode):
    return lax.dot_general(a.astype(bf16), b.astype(bf16), _DIMS[mode], preferred_element_type=f32)


@functools.partial(jax.custom_vjp, nondiff_argnums=(2,))
def _mm(a, b, mode):
    return _dg(a, b, mode)


def _mm_fwd(a, b, mode):
    return _dg(a, b, mode), (a, b)


def _mm_bwd(mode, res, ct):
    a, b = res
    if mode == "nn":
        da, db = _dg(ct, b, "nt"), _dg(a, ct, "tn")
    elif mode == "nt":
        da, db = _dg(ct, b, "nn"), _dg(ct, a, "tn")
    else:
        da, db = _dg(b, ct, "nt"), _dg(a, ct, "nn")
    return da.astype(a.dtype), db.astype(b.dtype)


_mm.defvjp(_mm_fwd, _mm_bwd)


def _split_bf16(x):
    hi = x.astype(bf16)
    return hi, (x - hi.astype(f32)).astype(bf16)


def _dgh(a, b, mode="nn"):
    a_hi, a_lo = _split_bf16(a)
    b_hi, b_lo = _split_bf16(b)
    dims = _DIMS[mode]
    return (lax.dot_general(a_hi, b_hi, dims, preferred_element_type=f32)
            + lax.dot_general(a_hi, b_lo, dims, preferred_element_type=f32)
            + lax.dot_general(a_lo, b_hi, dims, preferred_element_type=f32))


@jax.custom_vjp
def _unit_lower_inverses(lows):
    c = lows[0].shape[0]
    ri = lax.broadcasted_iota(jnp.int32, (c, c), 0)
    ci = lax.broadcasted_iota(jnp.int32, (c, c), 1)
    eye = jnp.where(ri == ci, 1.0, 0.0)
    xs = [-low for low in lows]
    rs = [eye + x for x in xs]
    for _ in range(int(math.log2(c)) - 1):
        xs = [_dgh(x, x) for x in xs]
        rs = [r + _dgh(r, x) for r, x in zip(rs, xs)]
    return tuple(rs)


def _uli_fwd(lows):
    ts = _unit_lower_inverses(lows)
    return ts, ts


def _uli_bwd(ts, cts):
    mids = [_dgh(ct, t, "nt") for t, ct in zip(ts, cts)]
    return (tuple(-_dgh(t, m, "tn") for t, m in zip(ts, mids)),)


_unit_lower_inverses.defvjp(_uli_fwd, _uli_bwd)


@functools.partial(jax.custom_vjp, nondiff_argnums=(1,))
def _split_lanes(x, width):
    return tuple(x[:, i * width:(i + 1) * width] for i in range(x.shape[1] // width))


def _split_fwd(x, width):
    return _split_lanes(x, width), None


def _split_bwd(width, _, cts):
    return (jnp.concatenate(list(cts), axis=1),)


_split_lanes.defvjp(_split_fwd, _split_bwd)


def _sigmoid(x):
    return 1.0 / (1.0 + jnp.exp(-x))


def _silu(x):
    return x * _sigmoid(x)


def _softplus(x):
    return jnp.maximum(x, 0.0) + jnp.log1p(jnp.exp(-jnp.abs(x)))


def _rms(x, gain):
    return x * lax.rsqrt(jnp.mean(x * x, axis=-1, keepdims=True) + EPS) * gain


class V:
    def __init__(self, arr, lead=None, cb=0, w=None):
        self.arr, self.lead, self.cb = arr, lead, cb
        self.w = arr.shape[-1] if w is None else w

    @property
    def rows(self):
        return self.arr.shape[-2]

    def spec(self, tile, order=None):
        lead, cb, w = self.lead, self.cb, self.w
        order = order or (lambda i: i)
        if lead is None:
            return pl.BlockSpec((tile, w), lambda i: (order(i), cb))
        return pl.BlockSpec((None, tile, w), lambda i: (lead, order(i), cb))

    def const_spec(self):
        lead, cb, w, r = self.lead, self.cb, self.w, self.rows
        if lead is None:
            return pl.BlockSpec((r, w), lambda i: (0, cb))
        return pl.BlockSpec((None, r, w), lambda i: (lead, 0, cb))


def _v(a):
    return a if isinstance(a, V) else V(a)


def _rowwise(name, fn, rows, consts, outs, tile=None):
    rows = [_v(r) for r in rows]
    consts = [_v(c) for c in consts]
    s = rows[0].rows
    tile = min(tile or ROW_TILE, s)
    nr, nc = len(rows), len(consts)

    def body(*refs):
        vals = [r[...].astype(f32) for r in refs[:nr + nc]]
        res = fn(*vals)
        for o, val in zip(refs[nr + nc:], res):
            o[...] = val.astype(o.dtype)

    return pl.pallas_call(
        body, name=name, grid=(s // tile,),
        in_specs=[r.spec(tile) for r in rows] + [c.const_spec() for c in consts],
        out_specs=[pl.BlockSpec((tile, w), lambda i: (i, 0)) for w, _ in outs],
        out_shape=[SDS((s, w), dt) for w, dt in outs],
        compiler_params=_cp(("parallel",)),
    )(*[r.arr for r in rows], *[c.arr for c in consts])


def _rowwise_vjp(name, fn, rows, consts, cts, d_rows, add=None, tile=None):
    rows = [_v(r) for r in rows]
    consts = [_v(c) for c in consts]
    cts = [[_v(c) for c in group] for group in cts]
    flat_cts = [c for group in cts for c in group]
    s = rows[0].rows
    tile = min(tile or ROW_TILE, s)
    nr, nc, nt = len(rows), len(consts), len(flat_cts)
    want = [k for k, dt in enumerate(d_rows) if dt is not None]
    has_add = add is not None
    add_v = [_v(add)] if has_add else []

    def body(*refs):
        vals = [r[...].astype(f32) for r in refs[:nr + nc]]
        ct_refs = refs[nr + nc:nr + nc + nt]
        pos = nr + nc + nt
        add_ref = refs[pos] if has_add else None
        pos += 1 if has_add else 0
        drow_refs = refs[pos:pos + len(want)]
        dconst_refs = refs[pos + len(want):]
        ctv, at = [], 0
        for group in cts:
            acc = ct_refs[at][...].astype(f32)
            for r in ct_refs[at + 1:at + len(group)]:
                acc = acc + r[...].astype(f32)
            at += len(group)
            ctv.append(acc)
        _, vjp = jax.vjp(fn, *vals)
        grads = vjp(tuple(ctv))
        for o, k in zip(drow_refs, want):
            g = grads[k]
            if has_add and k == want[0]:
                g = g + add_ref[...].astype(f32)
            o[...] = g.astype(o.dtype)

        @pl.when(pl.program_id(0) == 0)
        def _():
            for o in dconst_refs:
                o[...] = jnp.zeros_like(o)

        for o, g in zip(dconst_refs, grads[nr:]):
            o[...] += g

    outs = pl.pallas_call(
        body, name=name, grid=(s // tile,),
        in_specs=[r.spec(tile) for r in rows] + [c.const_spec() for c in consts]
        + [c.spec(tile) for c in flat_cts] + [a.spec(tile) for a in add_v],
        out_specs=[pl.BlockSpec((tile, rows[k].w), lambda i: (i, 0)) for k in want]
        + [pl.BlockSpec((c.rows, c.w), lambda i: (0, 0)) for c in consts],
        out_shape=[SDS((s, rows[k].w), d_rows[k]) for k in want] + [SDS((c.rows, c.w), f32) for c in consts],
        compiler_params=_cp(("arbitrary",)),
    )(*[r.arr for r in rows], *[c.arr for c in consts], *[c.arr for c in flat_cts], *[a.arr for a in add_v])
    return list(outs[:len(want)]), list(outs[len(want):])


def _pick(n, cap):
    best = None
    for t in range(LANES, min(n, cap) + 1, LANES):
        if n % t == 0:
            best = t
    return best or n


def _matmul(name, a, b, mode, out_dtype, add=None, tn_cap=1280):
    has_add = add is not None
    if mode in ("nn", "nt"):
        m, k = a.shape
        n = b.shape[1] if mode == "nn" else b.shape[0]
        tm = min(MM_TILE, m)
        tn = _pick(n, tn_cap) if k * n * 2 > (8 << 20) else n

        def body(*refs):
            a_ref, b_ref = refs[0], refs[1]
            o_ref = refs[-1]
            acc = _dg(a_ref[...], b_ref[...], mode)
            if has_add:
                acc = acc + refs[2][...].astype(f32)
            o_ref[...] = acc.astype(o_ref.dtype)

        b_spec = pl.BlockSpec((k, tn), lambda i, j: (0, j)) if mode == "nn" else pl.BlockSpec((tn, k), lambda i, j: (j, 0))
        in_specs = [pl.BlockSpec((tm, k), lambda i, j: (i, 0)), b_spec]
        args = [a, b]
        if has_add:
            in_specs.append(pl.BlockSpec((tm, tn), lambda i, j: (i, j)))
            args.append(add)
        return pl.pallas_call(
            body, name=name, grid=(m // tm, n // tn), in_specs=in_specs,
            out_specs=pl.BlockSpec((tm, tn), lambda i, j: (i, j)),
            out_shape=SDS((m, n), out_dtype), compiler_params=_cp(("parallel", "parallel")),
        )(*args)
    kk, m = a.shape
    n = b.shape[1]
    tk = min(MM_TILE, kk)
    tn = _pick(n, tn_cap) if m * n * 4 > (7 << 20) else n

    def body_tn(a_ref, b_ref, o_ref):
        @pl.when(pl.program_id(1) == 0)
        def _():
            o_ref[...] = jnp.zeros_like(o_ref)

        o_ref[...] += _dg(a_ref[...], b_ref[...], "tn")

    return pl.pallas_call(
        body_tn, name=name, grid=(n // tn, kk // tk),
        in_specs=[pl.BlockSpec((tk, m), lambda j, k: (k, 0)), pl.BlockSpec((tk, tn), lambda j, k: (k, j))],
        out_specs=pl.BlockSpec((m, tn), lambda j, k: (0, j)),
        out_shape=SDS((m, n), f32), compiler_params=_cp(("parallel", "arbitrary")),
    )(a, b)


def _ffn_fwd(name, h, gain, wgu, wd, layer):
    s, d = h.shape
    hs = wgu.shape[3]
    nh = wgu.shape[0] // 2
    tm = min(FFN_FWD_TILE, s)

    def body(h_ref, g_ref, wg_ref, wu_ref, wd_ref, o_ref, gate_ref, up_ref, n_scr, acc_scr):
        t = pl.program_id(1)

        @pl.when(t == 0)
        def _():
            n_scr[...] = _rms(h_ref[...], g_ref[...]).astype(bf16)
            acc_scr[...] = jnp.zeros_like(acc_scr)

        n = n_scr[...]
        gate = _dg(n, wg_ref[...], "nn")
        up = _dg(n, wu_ref[...], "nn")
        gate_ref[...] = gate.astype(gate_ref.dtype)
        up_ref[...] = up.astype(up_ref.dtype)
        acc_scr[...] += _dg(_silu(gate) * up, wd_ref[...], "nn")

        @pl.when(t == nh - 1)
        def _():
            o_ref[...] = h_ref[...] + 0.5 * acc_scr[...]

    saved_spec = pl.BlockSpec((None, tm, hs), lambda i, t: (t, i, 0))
    return pl.pallas_call(
        body, name=name, grid=(s // tm, nh),
        in_specs=[
            pl.BlockSpec((tm, d), lambda i, t: (i, 0)),
            pl.BlockSpec((1, d), lambda i, t: (0, 0)),
            pl.BlockSpec((None, None, d, hs), lambda i, t: (t, layer, 0, 0)),
            pl.BlockSpec((None, None, d, hs), lambda i, t: (t + nh, layer, 0, 0)),
            pl.BlockSpec((None, hs, d), lambda i, t: (layer, t, 0)),
        ],
        out_specs=[pl.BlockSpec((tm, d), lambda i, t: (i, 0)), saved_spec, saved_spec],
        out_shape=[SDS((s, d), f32), SDS((nh, s, hs), bf16), SDS((nh, s, hs), bf16)],
        scratch_shapes=[pltpu.VMEM((tm, d), bf16), pltpu.VMEM((tm, d), f32)],
        compiler_params=_cp(("parallel", "arbitrary")),
    )(h, gain, wgu, wgu, wd)


def _ffn_bwd(name, h, dout, gain, gate_s, up_s, wgu, wd, layer):
    s, d = h.shape
    hs = wgu.shape[3]
    nh = wgu.shape[0] // 2
    tm = min(FFN_BWD_TILE, s)

    def body(h_ref, do_ref, g_ref, gate_ref, up_ref, wg_ref, wu_ref, wd_ref, dn_ref, dwg_ref, dwu_ref, dwd_ref):
        @pl.when(pl.program_id(1) == 0)
        def _():
            dwg_ref[...] = jnp.zeros_like(dwg_ref)
            dwu_ref[...] = jnp.zeros_like(dwu_ref)
            dwd_ref[...] = jnp.zeros_like(dwd_ref)

        n = _rms(h_ref[...], g_ref[...]).astype(bf16)
        wg, wu, wdn = wg_ref[...], wu_ref[...], wd_ref[...]
        gate = gate_ref[...].astype(f32)
        up = up_ref[...].astype(f32)
        sg = _sigmoid(gate)
        act = gate * sg
        dy = (0.5 * do_ref[...]).astype(bf16)
        da = _dg(dy, wdn, "nt")
        dup = (da * act).astype(bf16)
        dgate = (da * up * (sg * (1.0 + gate * (1.0 - sg)))).astype(bf16)
        dwd_ref[...] += _dg(act * up, dy, "tn")
        dwg_ref[...] += _dg(n, dgate, "tn")
        dwu_ref[...] += _dg(n, dup, "tn")
        dn_ref[...] = (_dg(dgate, wg, "nt") + _dg(dup, wu, "nt")).astype(dn_ref.dtype)

    return pl.pallas_call(
        body, name=name, grid=(nh, s // tm),
        in_specs=[
            pl.BlockSpec((tm, d), lambda t, i: (i, 0)),
            pl.BlockSpec((tm, d), lambda t, i: (i, 0)),
            pl.BlockSpec((1, d), lambda t, i: (0, 0)),
            pl.BlockSpec((None, tm, hs), lambda t, i: (t, i, 0)),
            pl.BlockSpec((None, tm, hs), lambda t, i: (t, i, 0)),
            pl.BlockSpec((None, None, d, hs), lambda t, i: (t, layer, 0, 0)),
            pl.BlockSpec((None, None, d, hs), lambda t, i: (t + nh, layer, 0, 0)),
            pl.BlockSpec((None, hs, d), lambda t, i: (layer, t, 0)),
        ],
        out_specs=[
            pl.BlockSpec((None, tm, d), lambda t, i: (t, i, 0)),
            pl.BlockSpec((None, d, hs), lambda t, i: (t, 0, 0)),
            pl.BlockSpec((None, d, hs), lambda t, i: (t, 0, 0)),
            pl.BlockSpec((hs, d), lambda t, i: (t, 0)),
        ],
        out_shape=[SDS((nh, s, d), bf16), SDS((nh, d, hs), f32), SDS((nh, d, hs), f32), SDS((nh * hs, d), f32)],
        compiler_params=_cp(("parallel", "arbitrary")),
    )(h, dout, gain, gate_s, up_s, wgu, wgu, wd)


def _conv_fwd(name, x, w):
    x = _v(x)
    s, c = x.rows, x.w
    cw = w.shape[0]
    tile = min(ROW_TILE, s)
    cb = x.cb

    def body(x_ref, halo_ref, w_ref, o_ref, buf):
        first = pl.program_id(0) == 0
        buf[0:8, :] = jnp.where(first, 0.0, halo_ref[...])
        buf[8:8 + tile, :] = x_ref[...]
        acc = w_ref[0:1, :] * buf[pl.ds(8 - cw + 1, tile), :]
        for j in range(1, cw):
            acc = acc + w_ref[j:j + 1, :] * buf[pl.ds(8 - cw + 1 + j, tile), :]
        o_ref[...] = acc

    return pl.pallas_call(
        body, name=name, grid=(s // tile,),
        in_specs=[
            pl.BlockSpec((tile, c), lambda i: (i, cb)),
            pl.BlockSpec((8, c), lambda i: (jnp.maximum(i * (tile // 8) - 1, 0), cb)),
            pl.BlockSpec((cw, c), lambda i: (0, 0)),
        ],
        out_specs=pl.BlockSpec((tile, c), lambda i: (i, 0)),
        out_shape=SDS((s, c), f32),
        scratch_shapes=[pltpu.VMEM((tile + 8, c), f32)],
        compiler_params=_cp(("parallel",)),
    )(x.arr, x.arr, w)


def _conv_bwd(name, x, dy, w):
    x = _v(x)
    s, c = x.rows, x.w
    cw = w.shape[0]
    tile = min(ROW_TILE, s)
    n_tiles = s // tile
    cb = x.cb

    def body(x_ref, xh_ref, dy_ref, dyh_ref, w_ref, dx_ref, dw_ref, xbuf, dbuf):
        i = pl.program_id(0)
        xbuf[0:8, :] = jnp.where(i == 0, 0.0, xh_ref[...])
        xbuf[8:8 + tile, :] = x_ref[...]
        dyv = dy_ref[...]
        dbuf[0:tile, :] = dyv
        dbuf[tile:tile + 8, :] = jnp.where(i == n_tiles - 1, 0.0, dyh_ref[...])

        @pl.when(i == 0)
        def _():
            dw_ref[...] = jnp.zeros_like(dw_ref)

        acc = w_ref[0:1, :] * dbuf[pl.ds(cw - 1, tile), :]
        for j in range(1, cw):
            acc = acc + w_ref[j:j + 1, :] * dbuf[pl.ds(cw - 1 - j, tile), :]
        dx_ref[...] = acc.astype(dx_ref.dtype)
        for j in range(cw):
            dw_ref[j:j + 1, :] += jnp.sum(xbuf[pl.ds(8 - cw + 1 + j, tile), :] * dyv, axis=0, keepdims=True)

    return pl.pallas_call(
        body, name=name, grid=(n_tiles,),
        in_specs=[
            pl.BlockSpec((tile, c), lambda i: (i, cb)),
            pl.BlockSpec((8, c), lambda i: (jnp.maximum(i * (tile // 8) - 1, 0), cb)),
            pl.BlockSpec((tile, c), lambda i: (i, 0)),
            pl.BlockSpec((8, c), lambda i: (jnp.minimum((i + 1) * (tile // 8), s // 8 - 1), 0)),
            pl.BlockSpec((cw, c), lambda i: (0, 0)),
        ],
        out_specs=[pl.BlockSpec((tile, c), lambda i: (i, 0)), pl.BlockSpec((cw, c), lambda i: (0, 0))],
        out_shape=[SDS((s, c), bf16), SDS((cw, c), f32)],
        scratch_shapes=[pltpu.VMEM((tile + 8, c), f32), pltpu.VMEM((tile + 8, c), f32)],
        compiler_params=_cp(("arbitrary",)),
    )(x.arr, x.arr, dy, dy, w)


def _gdn_pre(n_heads, head_dim, yc, ab, a_log, dt_bias):
    gw = n_heads * head_dim
    act = _silu(yc)
    parts = _split_lanes(act, head_dim)
    qs = [p * lax.rsqrt(jnp.sum(p * p, axis=-1, keepdims=True) + EPS) * (head_dim ** -0.5) for p in parts[:n_heads]]
    ks = [p * lax.rsqrt(jnp.sum(p * p, axis=-1, keepdims=True) + EPS) for p in parts[n_heads:2 * n_heads]]
    lane = lax.broadcasted_iota(jnp.int32, ab.shape, 1)
    g = -jnp.exp(a_log) * _softplus(ab + dt_bias)
    gb = jnp.where(lane < n_heads, g, jnp.where(lane < 2 * n_heads, _sigmoid(ab), 0.0))
    del gw
    return (jnp.concatenate(qs, axis=1), jnp.concatenate(ks, axis=1),
            jnp.concatenate(list(parts[2 * n_heads:]), axis=1), gb)


def _gdn_post(n_heads, head_dim, o, z, out_norm):
    parts = _split_lanes(o, head_dim)
    normed = jnp.concatenate([_rms(p, out_norm) for p in parts], axis=1)
    return (normed * _silu(z),)


def _gdn_chunk(n_heads, head_dim, q, k, v, gb, *states):
    c = q.shape[0]
    ri = lax.broadcasted_iota(jnp.int32, (c, c), 0)
    ci = lax.broadcasted_iota(jnp.int32, (c, c), 1)
    incl, strict, diag = ri >= ci, ri > ci, ri == ci
    lane = lax.broadcasted_iota(jnp.int32, gb.shape, 1)
    qs, ks, vs = _split_lanes(q, head_dim), _split_lanes(k, head_dim), _split_lanes(v, head_dim)
    heads = range(n_heads)
    g = [jnp.sum(jnp.where(lane == h, gb, 0.0), axis=1, keepdims=True) for h in heads]
    beta = [jnp.sum(jnp.where(lane == n_heads + h, gb, 0.0), axis=1, keepdims=True) for h in heads]
    g_row = [jnp.sum(jnp.where(diag, g[h], 0.0), axis=0, keepdims=True) for h in heads]
    cg_col = [jnp.sum(jnp.where(incl, g_row[h], 0.0), axis=1, keepdims=True) for h in heads]
    cg_row = [jnp.sum(jnp.where(ri <= ci, g[h], 0.0), axis=0, keepdims=True) for h in heads]
    g_last = [jnp.sum(g[h], axis=0, keepdims=True) for h in heads]
    decay = [jnp.where(incl, jnp.exp(jnp.where(incl, cg_col[h] - cg_row[h], 0.0)), 0.0) for h in heads]
    kb = [ks[h] * beta[h] for h in heads]
    lower = [jnp.where(strict, _mm(kb[h], ks[h], "nt") * decay[h], 0.0) for h in heads]
    eye = jnp.where(diag, 1.0, 0.0)
    off_diag = [t - eye for t in _unit_lower_inverses(tuple(lower))]
    e_col = [jnp.exp(cg_col[h]) for h in heads]
    vb = [vs[h] * beta[h] for h in heads]
    kbg = [kb[h] * e_col[h] for h in heads]
    u = [vb[h] + _mm(off_diag[h], vb[h], "nn") for h in heads]
    w = [kbg[h] + _mm(off_diag[h], kbg[h], "nn") for h in heads]
    qk = [jnp.where(incl, _mm(qs[h], ks[h], "nt") * decay[h], 0.0) for h in heads]
    v_new = [u[h] - _mm(w[h], states[h], "nn") for h in heads]
    inter = [_mm(qs[h] * e_col[h], states[h], "nn") for h in heads]
    outs = [inter[h] + _mm(qk[h], v_new[h], "nn") for h in heads]
    k_tail = [ks[h] * jnp.exp(g_last[h] - cg_col[h]) for h in heads]
    new_states = [states[h] * jnp.exp(g_last[h]) + _mm(k_tail[h], v_new[h], "tn") for h in heads]
    return (jnp.concatenate(outs, axis=1), *new_states)


def _gdn_chunk_fwd(name, q, k, v, gb, n_heads, head_dim):
    s, gw = q.shape
    n = s // CHUNK
    fn = functools.partial(_gdn_chunk, n_heads, head_dim)

    def body(q_ref, k_ref, v_ref, gb_ref, o_ref, st_ref, st_scr):
        @pl.when(pl.program_id(0) == 0)
        def _():
            st_scr[...] = jnp.zeros_like(st_scr)

        st_ref[...] = st_scr[...]
        res = fn(q_ref[...], k_ref[...], v_ref[...], gb_ref[...], *[st_scr[h] for h in range(n_heads)])
        o_ref[...] = res[0]
        for h in range(n_heads):
            st_scr[h] = res[1 + h]

    row = lambda w: pl.BlockSpec((CHUNK, w), lambda i: (i, 0))
    return pl.pallas_call(
        body, name=name, grid=(n,),
        in_specs=[row(gw), row(gw), row(gw), row(LANES)],
        out_specs=[row(gw), pl.BlockSpec((None, n_heads, head_dim, head_dim), lambda i: (i, 0, 0, 0))],
        out_shape=[SDS((s, gw), f32), SDS((n, n_heads, head_dim, head_dim), f32)],
        scratch_shapes=[pltpu.VMEM((n_heads, head_dim, head_dim), f32)],
        compiler_params=_cp(("arbitrary",)),
    )(q, k, v, gb)


def _gdn_chunk_bwd(name, q, k, v, gb, states, d_out, n_heads, head_dim):
    s, gw = q.shape
    n = s // CHUNK
    fn = functools.partial(_gdn_chunk, n_heads, head_dim)

    def body(q_ref, k_ref, v_ref, gb_ref, st_ref, do_ref, dq_ref, dk_ref, dv_ref, dgb_ref, dst_scr):
        @pl.when(pl.program_id(0) == 0)
        def _():
            dst_scr[...] = jnp.zeros_like(dst_scr)

        _, vjp = jax.vjp(fn, q_ref[...], k_ref[...], v_ref[...], gb_ref[...], *[st_ref[h] for h in range(n_heads)])
        grads = vjp((do_ref[...].astype(f32), *[dst_scr[h] for h in range(n_heads)]))
        dq_ref[...] = grads[0]
        dk_ref[...] = grads[1]
        dv_ref[...] = grads[2]
        dgb_ref[...] = grads[3]
        for h in range(n_heads):
            dst_scr[h] = grads[4 + h]

    row = lambda w: pl.BlockSpec((CHUNK, w), lambda i: (n - 1 - i, 0))
    return pl.pallas_call(
        body, name=name, grid=(n,),
        in_specs=[row(gw), row(gw), row(gw), row(LANES),
                  pl.BlockSpec((None, n_heads, head_dim, head_dim), lambda i: (n - 1 - i, 0, 0, 0)), row(gw)],
        out_specs=[row(gw), row(gw), row(gw), row(LANES)],
        out_shape=[SDS((s, gw), f32), SDS((s, gw), f32), SDS((s, gw), f32), SDS((s, LANES), f32)],
        scratch_shapes=[pltpu.VMEM((n_heads, head_dim, head_dim), f32)],
        compiler_params=_cp(("arbitrary",)),
    )(q, k, v, gb, states, d_out)


def _mem_attn(qm, km, vm):
    width = qm.shape[1]
    hd = width // MEM_HEADS
    lane = lax.broadcasted_iota(jnp.int32, (1, width), 1)
    out = jnp.zeros_like(qm)
    for h in range(MEM_HEADS):
        msk = jnp.where((lane >= h * hd) & (lane < (h + 1) * hd), 1.0, 0.0)
        logits = _mm(qm * msk, km, "nt") * (hd ** -0.5)
        p = jnp.exp(logits - jnp.max(logits, axis=-1, keepdims=True))
        p = p / jnp.sum(p, axis=-1, keepdims=True)
        out = out + _mm(p, vm, "nn") * msk
    return (out,)


def _kv_post(n_heads, pk, pv, pf, b_f):
    lane = lax.broadcasted_iota(jnp.int32, pf.shape, 1)
    log_f = jnp.where(lane < n_heads, -_softplus(-(pf + b_f)), 0.0)
    return pk, pv, log_f


def _cumsum(name, xs, reverse):
    s, w = xs[0].shape
    tile = min(CUMSUM_TILE, s)
    n = s // tile

    def body(*refs):
        x_refs, o_ref, carry = refs[:-2], refs[-2], refs[-1]

        @pl.when(pl.program_id(0) == 0)
        def _():
            carry[...] = jnp.zeros_like(carry)

        xv = x_refs[0][...]
        for r in x_refs[1:]:
            xv = xv + r[...]
        ri = lax.broadcasted_iota(jnp.int32, (tile, tile), 0)
        ci = lax.broadcasted_iota(jnp.int32, (tile, tile), 1)
        tri = jnp.where((ri <= ci) if reverse else (ri >= ci), 1.0, 0.0).astype(bf16)
        x1 = xv.astype(bf16)
        r1 = xv - x1.astype(f32)
        x2 = r1.astype(bf16)
        x3 = (r1 - x2.astype(f32)).astype(bf16)
        acc = carry[...] + _dg(tri, x1, "nn") + _dg(tri, x2, "nn") + _dg(tri, x3, "nn")
        o_ref[...] = acc
        carry[...] += jnp.sum(xv, axis=0, keepdims=True)

    order = (lambda i: (n - 1 - i, 0)) if reverse else (lambda i: (i, 0))
    return pl.pallas_call(
        body, name=name, grid=(n,),
        in_specs=[pl.BlockSpec((tile, w), order)] * len(xs), out_specs=pl.BlockSpec((tile, w), order),
        out_shape=SDS((s, w), f32), scratch_shapes=[pltpu.VMEM((1, w), f32)],
        compiler_params=_cp(("arbitrary",)),
    )(*xs)


def _fox_logits(q_ref, k_ref, cr_ref, hh, head_dim, scale, diagonal):
    sl = slice(hh * head_dim, (hh + 1) * head_dim)
    s = _dg(q_ref[:, sl], k_ref[:, sl], "nt") * scale - cr_ref[hh]
    if diagonal:
        tq, tk = s.shape
        ok = lax.broadcasted_iota(jnp.int32, (tq, tk), 1) <= lax.broadcasted_iota(jnp.int32, (tq, tk), 0)
        s = jnp.where(ok, s, NEG_INF)
    return s, sl


def _on_causal_tiles(i, j, fn):
    @pl.when(j < i)
    def _():
        fn(False)

    @pl.when(j == i)
    def _():
        fn(True)


def _fox_fwd(name, q, k, v, crow, n_heads, head_dim):
    s = k.shape[0]
    tq = tk = min(ATT_TILE, s)
    nq, nk = s // tq, s // tk
    scale = head_dim ** -0.5
    hpb = ATT_HEADS
    wb = hpb * head_dim

    def body(q_ref, k_ref, v_ref, cr_ref, o_ref, lse_ref, m_scr, l_scr, acc_scr):
        i, j = pl.program_id(1), pl.program_id(2)

        @pl.when(j == 0)
        def _():
            m_scr[...] = jnp.full_like(m_scr, NEG_INF)
            l_scr[...] = jnp.zeros_like(l_scr)
            acc_scr[...] = jnp.zeros_like(acc_scr)

        def step(diagonal):
            heads = range(hpb)
            sl = [slice(hh * head_dim, (hh + 1) * head_dim) for hh in heads]
            sc = [_fox_logits(q_ref, k_ref, cr_ref, hh, head_dim, scale, diagonal)[0] for hh in heads]
            m_old = [m_scr[hh] for hh in heads]
            m_new = [jnp.maximum(m_old[hh], jnp.max(sc[hh], axis=-1, keepdims=True)) for hh in heads]
            p = [jnp.exp(sc[hh] - m_new[hh]) for hh in heads]
            alpha = [jnp.exp(m_old[hh] - m_new[hh]) for hh in heads]
            pv = [_dg(p[hh], v_ref[:, sl[hh]], "nn") for hh in heads]
            for hh in heads:
                l_scr[hh] = alpha[hh] * l_scr[hh] + jnp.sum(p[hh], axis=-1, keepdims=True)
                acc_scr[:, sl[hh]] = alpha[hh] * acc_scr[:, sl[hh]] + pv[hh]
                m_scr[hh] = m_new[hh]

        _on_causal_tiles(i, j, step)

        @pl.when(j == nk - 1)
        def _():
            for hh in range(hpb):
                sl = slice(hh * head_dim, (hh + 1) * head_dim)
                o_ref[:, sl] = (acc_scr[:, sl] / l_scr[hh]).astype(o_ref.dtype)
                lse_ref[hh] = m_scr[hh] + jnp.log(l_scr[hh])

    return pl.pallas_call(
        body, name=name, grid=(n_heads // hpb, nq, nk),
        in_specs=[
            pl.BlockSpec((tq, wb), lambda g, i, j: (i, g)),
            pl.BlockSpec((tk, wb), lambda g, i, j: (jnp.minimum(j, i), g)),
            pl.BlockSpec((tk, wb), lambda g, i, j: (jnp.minimum(j, i), g)),
            pl.BlockSpec((hpb, 1, tk), lambda g, i, j: (g, 0, jnp.minimum(j, i))),
        ],
        out_specs=[pl.BlockSpec((tq, wb), lambda g, i, j: (i, g)),
                   pl.BlockSpec((hpb, tq, 1), lambda g, i, j: (g, i, 0))],
        out_shape=[SDS((s, n_heads * head_dim), bf16), SDS((n_heads, s, 1), f32)],
        scratch_shapes=[pltpu.VMEM((hpb, tq, 1), f32), pltpu.VMEM((hpb, tq, 1), f32), pltpu.VMEM((tq, wb), f32)],
        compiler_params=_cp(("parallel", "parallel", "arbitrary")),
    )(q, k, v, crow)


def _fox_probs(q_ref, k_ref, v_ref, cr_ref, lse_ref, do_ref, hh, head_dim, scale, diagonal):
    sc, sl = _fox_logits(q_ref, k_ref, cr_ref, hh, head_dim, scale, diagonal)
    return jnp.exp(sc - lse_ref[hh]), _dg(do_ref[:, sl], v_ref[:, sl], "nt"), sl


def _fox_bwd_dq(name, q, k, v, crow, lse, o, do, n_heads, head_dim):
    s = k.shape[0]
    tq = tk = min(ATT_TILE, s)
    nq, nk = s // tq, s // tk
    scale = head_dim ** -0.5
    hpb = ATT_HEADS
    wb = hpb * head_dim

    def body(q_ref, k_ref, v_ref, cr_ref, lse_ref, o_ref, do_ref, dq_ref, delta_ref, dcc_ref, acc_scr):
        i, j = pl.program_id(1), pl.program_id(2)
        heads = range(hpb)

        @pl.when(j == 0)
        def _():
            prod = o_ref[...].astype(f32) * do_ref[...].astype(f32)
            for hh in heads:
                delta_ref[hh] = jnp.sum(prod[:, hh * head_dim:(hh + 1) * head_dim], axis=-1, keepdims=True)
            dcc_ref[...] = jnp.zeros_like(dcc_ref)
            acc_scr[...] = jnp.zeros_like(acc_scr)

        def step(diagonal):
            pd = [_fox_probs(q_ref, k_ref, v_ref, cr_ref, lse_ref, do_ref, hh, head_dim, scale, diagonal) for hh in heads]
            ds = [pd[hh][0] * (pd[hh][1] - delta_ref[hh]) for hh in heads]
            dqs = [_dg(ds[hh], k_ref[:, pd[hh][2]], "nn") for hh in heads]
            for hh in heads:
                dcc_ref[hh] += jnp.sum(ds[hh], axis=-1, keepdims=True)
                acc_scr[:, pd[hh][2]] += dqs[hh]

        _on_causal_tiles(i, j, step)

        @pl.when(j == nk - 1)
        def _():
            dq_ref[...] = (acc_scr[...] * scale).astype(dq_ref.dtype)

    qspec = pl.BlockSpec((tq, wb), lambda g, i, j: (i, g))
    kspec = pl.BlockSpec((tk, wb), lambda g, i, j: (jnp.minimum(j, i), g))
    cspec = pl.BlockSpec((hpb, tq, 1), lambda g, i, j: (g, i, 0))
    return pl.pallas_call(
        body, name=name, grid=(n_heads // hpb, nq, nk),
        in_specs=[qspec, kspec, kspec,
                  pl.BlockSpec((hpb, 1, tk), lambda g, i, j: (g, 0, jnp.minimum(j, i))), cspec, qspec, qspec],
        out_specs=[qspec, cspec, cspec],
        out_shape=[SDS((s, n_heads * head_dim), bf16), SDS((n_heads, s, 1), f32), SDS((n_heads, s, 1), f32)],
        scratch_shapes=[pltpu.VMEM((tq, wb), f32)],
        compiler_params=_cp(("parallel", "parallel", "arbitrary")),
    )(q, k, v, crow, lse, o, do)


def _fox_bwd_dkv(name, q, k, v, crow, lse, delta, do, n_heads, head_dim):
    s = k.shape[0]
    tq = tk = min(ATT_TILE, s)
    nq, nk = s // tq, s // tk
    scale = head_dim ** -0.5
    hpb = ATT_HEADS
    wb = hpb * head_dim

    def body(q_ref, k_ref, v_ref, cr_ref, lse_ref, delta_ref, do_ref, dk_ref, dv_ref, dc_ref):
        j, i = pl.program_id(1), pl.program_id(2)

        @pl.when(i == 0)
        def _():
            dk_ref[...] = jnp.zeros_like(dk_ref)
            dv_ref[...] = jnp.zeros_like(dv_ref)
            dc_ref[...] = jnp.zeros_like(dc_ref)

        def step(diagonal):
            heads = range(hpb)
            pd = [_fox_probs(q_ref, k_ref, v_ref, cr_ref, lse_ref, do_ref, hh, head_dim, scale, diagonal) for hh in heads]
            ds = [pd[hh][0] * (pd[hh][1] - delta_ref[hh]) for hh in heads]
            dvs = [_dg(pd[hh][0], do_ref[:, pd[hh][2]], "tn") for hh in heads]
            dks = [_dg(ds[hh], q_ref[:, pd[hh][2]], "tn") for hh in heads]
            for hh in heads:
                sl = pd[hh][2]
                dv_ref[:, sl] += dvs[hh]
                dk_ref[:, sl] += dks[hh] * scale
                dc_ref[hh] -= jnp.sum(ds[hh], axis=0, keepdims=True)

        _on_causal_tiles(i, j, step)

    qspec = pl.BlockSpec((tq, wb), lambda g, j, i: (jnp.maximum(i, j), g))
    kspec = pl.BlockSpec((tk, wb), lambda g, j, i: (j, g))
    cspec = pl.BlockSpec((hpb, tq, 1), lambda g, j, i: (g, jnp.maximum(i, j), 0))
    rspec = pl.BlockSpec((hpb, 1, tk), lambda g, j, i: (g, 0, j))
    return pl.pallas_call(
        body, name=name, grid=(n_heads // hpb, nk, nq),
        in_specs=[qspec, kspec, kspec, rspec, cspec, cspec, qspec],
        out_specs=[kspec, kspec, rspec],
        out_shape=[SDS((s, n_heads * head_dim), f32), SDS((s, n_heads * head_dim), f32), SDS((n_heads, 1, s), f32)],
        compiler_params=_cp(("parallel", "parallel", "arbitrary")),
    )(q, k, v, crow, lse, delta, do)


def _final_loss(name, h, target, gain):
    s, d = h.shape
    tile = min(ROW_TILE, s)

    def body(h_ref, t_ref, g_ref, loss_ref, dh_ref, dg_ref):
        @pl.when(pl.program_id(0) == 0)
        def _():
            loss_ref[...] = jnp.zeros_like(loss_ref)
            dg_ref[...] = jnp.zeros_like(dg_ref)

        x, g = h_ref[...], g_ref[...]
        rstd = lax.rsqrt(jnp.mean(x * x, axis=-1, keepdims=True) + EPS)
        xhat = x * rstd
        err = xhat * g - t_ref[...]
        row = jnp.sum(err * err, axis=-1, keepdims=True) * (0.5 / d)
        loss_ref[...] += jnp.sum(row, axis=0, keepdims=True)
        dy = err * (1.0 / d)
        dg_ref[...] += jnp.sum(dy * xhat, axis=0, keepdims=True)
        dxhat = dy * g
        dh_ref[...] = rstd * (dxhat - xhat * jnp.mean(dxhat * xhat, axis=-1, keepdims=True))

    return pl.pallas_call(
        body, name=name, grid=(s // tile,),
        in_specs=[pl.BlockSpec((tile, d), lambda i: (i, 0)), pl.BlockSpec((tile, d), lambda i: (i, 0)),
                  pl.BlockSpec((1, d), lambda i: (0, 0))],
        out_specs=[pl.BlockSpec((1, LANES), lambda i: (0, 0)), pl.BlockSpec((tile, d), lambda i: (i, 0)),
                   pl.BlockSpec((1, d), lambda i: (0, 0))],
        out_shape=[SDS((1, LANES), f32), SDS((s, d), f32), SDS((1, d), f32)],
        compiler_params=_cp(("arbitrary",)),
    )(h, target, gain)


def _adamw(name, parts, w, m, v):
    r, c = w.shape
    rb = r
    for cand in (256, 128, 64, 32, 16):
        if r % cand == 0 and cand * c * 4 <= (2 << 20):
            rb = cand
            break
    n_parts = parts.shape[0]

    def body(p_ref, w_ref, m_ref, v_ref, g_out, d_out, m_out, v_out):
        g = p_ref[0].astype(f32)
        for k in range(1, n_parts):
            g = g + p_ref[k].astype(f32)
        m_new = ADAM_B1 * m_ref[...] + (1.0 - ADAM_B1) * g
        v_new = ADAM_B2 * v_ref[...] + (1.0 - ADAM_B2) * (g * g)
        m_hat = m_new / (1.0 - ADAM_B1 ** ADAM_STEP)
        v_hat = v_new / (1.0 - ADAM_B2 ** ADAM_STEP)
        g_out[...] = g
        d_out[...] = -ADAM_LR * (m_hat / (jnp.sqrt(v_hat) + ADAM_EPS) + ADAM_WD * w_ref[...])
        m_out[...] = m_new
        v_out[...] = v_new

    blk = pl.BlockSpec((rb, c), lambda i: (i, 0))
    return pl.pallas_call(
        body, name=name, grid=(r // rb,),
        in_specs=[pl.BlockSpec((n_parts, rb, c), lambda i: (0, i, 0)), blk, blk, blk],
        out_specs=[blk, blk, blk, blk],
        out_shape=[SDS((r, c), f32)] * 4,
        compiler_params=_cp(("parallel",)),
    )(parts, w, m, v)


def _position():
    x, y, c = lax.axis_index("x"), lax.axis_index("y"), lax.axis_index("c")
    return x, y, c


def _all_gather(name, shards):
    n = len(shards)

    def body(*refs):
        ins, outs = refs[:n], refs[n:2 * n]
        send_sems, recv_sems, local_sems = refs[2 * n:]
        x, y, c = _position()
        me, sibling = (x, y, c), (x, y, 1 - c)
        chips = [(1 - x, y), (x, 1 - y), (1 - x, 1 - y)]

        def slot(a, block):
            px, py, pc = block
            return outs[a].at[4 * px + 2 * py + pc]

        def copy(a, k, block, to, src=None):
            return pltpu.make_async_remote_copy(
                src_ref=slot(a, block) if src is None else src, dst_ref=slot(a, block),
                send_sem=send_sems.at[a, k], recv_sem=recv_sems.at[a, k], device_id=to, device_id_type=MESH)

        local = [pltpu.make_async_copy(ins[a], slot(a, me), local_sems.at[a]) for a in range(n)]
        for cp in local:
            cp.start()
        started = []
        for a in range(n):
            first = [copy(a, 0, me, sibling, src=ins[a])]
            first += [copy(a, 1 + j, me, (*chip, c), src=ins[a]) for j, chip in enumerate(chips)]
            for cp in first:
                cp.start()
            started += first
        for a in range(n):
            for j, chip in enumerate(chips):
                copy(a, 1 + j, (*chip, c), me).wait_recv()
                passed = copy(a, 4 + j, (*chip, c), sibling)
                passed.start()
                started.append(passed)
        for a in range(n):
            copy(a, 0, sibling, me).wait_recv()
            for j, chip in enumerate(chips):
                copy(a, 4 + j, (*chip, 1 - c), me).wait_recv()
        for cp in started:
            cp.wait_send()
        for cp in local:
            cp.wait()

    any_spec = pl.BlockSpec(memory_space=pl.ANY)
    outs = pl.pallas_call(
        body, name=name,
        in_specs=[any_spec] * n, out_specs=[any_spec] * n,
        out_shape=[SDS((N_DEV, *a.shape), a.dtype) for a in shards],
        scratch_shapes=[pltpu.SemaphoreType.DMA((n, 7)), pltpu.SemaphoreType.DMA((n, 7)), pltpu.SemaphoreType.DMA((n,))],
    )(*shards)
    return list(outs)


N_CHIPS = N_DEV // 2


def _sibling_exchange(name, arrs):
    n = len(arrs)

    def body(*refs):
        ins, outs = refs[:n], refs[n:2 * n]
        send_sems, recv_sems = refs[2 * n:]
        x, y, c = _position()
        copies = [pltpu.make_async_remote_copy(
            src_ref=ins[a], dst_ref=outs[a], send_sem=send_sems.at[a], recv_sem=recv_sems.at[a],
            device_id=(x, y, 1 - c), device_id_type=MESH) for a in range(n)]
        for cp in copies:
            cp.start()
        for cp in copies:
            cp.wait_recv()
        for cp in copies:
            cp.wait_send()

    any_spec = pl.BlockSpec(memory_space=pl.ANY)
    outs = pl.pallas_call(
        body, name=name,
        in_specs=[any_spec] * n, out_specs=[any_spec] * n, out_shape=[SDS(a.shape, a.dtype) for a in arrs],
        scratch_shapes=[pltpu.SemaphoreType.DMA((n,)), pltpu.SemaphoreType.DMA((n,))],
    )(*arrs)
    return list(outs)


def _chip_exchange(name, parts):
    n = len(parts)

    def body(*refs):
        ins, outs = refs[:n], refs[n:2 * n]
        send_sems, recv_sems, local_sems = refs[2 * n:]
        x, y, c = _position()
        my_chip = 2 * x + y
        local = [pltpu.make_async_copy(ins[a].at[my_chip], outs[a].at[my_chip], local_sems.at[a]) for a in range(n)]
        for cp in local:
            cp.start()
        sends, recvs = [], []
        for a in range(n):
            for k in range(1, N_CHIPS):
                px = (1 - x) if (k >> 1) & 1 else x
                py = (1 - y) if k & 1 else y
                peer_chip = 2 * px + py
                send = pltpu.make_async_remote_copy(
                    src_ref=ins[a].at[peer_chip], dst_ref=outs[a].at[my_chip],
                    send_sem=send_sems.at[a, k - 1], recv_sem=recv_sems.at[a, k - 1],
                    device_id=(px, py, c), device_id_type=MESH)
                send.start()
                sends.append(send)
                recvs.append(pltpu.make_async_remote_copy(
                    src_ref=ins[a].at[peer_chip], dst_ref=outs[a].at[peer_chip],
                    send_sem=send_sems.at[a, k - 1], recv_sem=recv_sems.at[a, k - 1],
                    device_id=(px, py, c), device_id_type=MESH))
        for cp in recvs:
            cp.wait_recv()
        for cp in sends:
            cp.wait_send()
        for cp in local:
            cp.wait()

    any_spec = pl.BlockSpec(memory_space=pl.ANY)
    outs = pl.pallas_call(
        body, name=name,
        in_specs=[any_spec] * n, out_specs=[any_spec] * n,
        out_shape=[SDS(a.shape, a.dtype) for a in parts],
        scratch_shapes=[pltpu.SemaphoreType.DMA((n, N_CHIPS - 1)), pltpu.SemaphoreType.DMA((n, N_CHIPS - 1)),
                        pltpu.SemaphoreType.DMA((n,))],
    )(*parts)
    return list(outs)


def _reduce_exchange(stacks):
    core = lax.axis_index("c")
    by_core = [st.reshape(N_CHIPS, 2, *st.shape[1:]) for st in stacks]
    mine = [lax.dynamic_index_in_dim(v, core, axis=1, keepdims=False) for v in by_core]
    theirs = _sibling_exchange(
        "exchange_sibling", [lax.dynamic_index_in_dim(v, 1 - core, axis=1, keepdims=False) for v in by_core])
    summed = []
    for a, (m, t) in enumerate(zip(mine, theirs)):
        cols = m.shape[-1]
        (both,) = _rowwise(f"exchange_add_{a}", lambda p, q: (p + q,), [m.reshape(-1, cols), t.reshape(-1, cols)], [],
                           [(cols, m.dtype)])
        summed.append(both.reshape(m.shape))
    return _chip_exchange("exchange_chips", summed)


def _rows_from_shards(g):
    n, l, r, c = g.shape
    return g.transpose(1, 0, 2, 3).reshape(l, n * r, c)


def _rows_to_shards(w):
    l, rows, c = w.shape
    return w.reshape(l, N_DEV, rows // N_DEV, c).transpose(1, 0, 2, 3)


def _pad_lanes(a, width):
    return jnp.pad(a, [(0, 0)] * (a.ndim - 1) + [(0, width - a.shape[-1])])


def _row(vec, width=None):
    vec = vec.reshape(1, -1)
    return vec if width is None else _pad_lanes(vec, width)


class _SmallPack:
    def __init__(self, shapes):
        self.shapes, self.offsets, at = shapes, {}, 0
        for name, shape in shapes.items():
            last = shape[-1]
            lead = int(math.prod(shape[:-1]))
            rows = lead * (last // LANES) if last >= LANES else lead
            self.offsets[name] = (at, rows)
            at += rows
        self.rows = -(-at // 8) * 8

    def pack(self, values):
        pieces = []
        for name, shape in self.shapes.items():
            val = values[name].astype(f32)
            if shape[-1] >= LANES:
                pieces.append(val.reshape(-1, LANES))
            else:
                pieces.append(_pad_lanes(val.reshape(-1, shape[-1]), LANES))
        used = sum(p.shape[0] for p in pieces)
        if used < self.rows:
            pieces.append(jnp.zeros((self.rows - used, LANES), f32))
        return jnp.concatenate(pieces, axis=0)

    def unpack(self, packed):
        out = {}
        for name, shape in self.shapes.items():
            at, rows = self.offsets[name]
            blk = packed[at:at + rows]
            out[name] = blk.reshape(shape) if shape[-1] >= LANES else blk[:, :shape[-1]].reshape(shape)
        return out


def kernel(x, mem, ffn1_norm, ffn1_w_gate_up, ffn1_w_down, mix_norm, ffn2_norm, ffn2_w_gate_up, ffn2_w_down, gdn_w_in, gdn_conv, gdn_A_log, gdn_dt_bias, gdn_out_norm, fox_w_in, w_out, mem_norm, mem_w_kv, kv_norm, kv_w, kv_b_f, final_norm, loss_target, m_ffn1_norm, m_ffn1_w_gate_up, m_ffn1_w_down, m_mix_norm, m_ffn2_norm, m_ffn2_w_gate_up, m_ffn2_w_down, m_gdn_w_in, m_gdn_conv, m_gdn_A_log, m_gdn_dt_bias, m_gdn_out_norm, m_fox_w_in, m_w_out, m_mem_norm, m_mem_w_kv, m_kv_norm, m_kv_w, m_kv_b_f, m_final_norm, v_ffn1_norm, v_ffn1_w_gate_up, v_ffn1_w_down, v_mix_norm, v_ffn2_norm, v_ffn2_w_gate_up, v_ffn2_w_down, v_gdn_w_in, v_gdn_conv, v_gdn_A_log, v_gdn_dt_bias, v_gdn_out_norm, v_fox_w_in, v_w_out, v_mem_norm, v_mem_w_kv, v_kv_norm, v_kv_w, v_kv_b_f, v_final_norm):
    weights = dict(ffn1_norm=ffn1_norm, ffn1_w_gate_up=ffn1_w_gate_up, ffn1_w_down=ffn1_w_down, mix_norm=mix_norm,
                   ffn2_norm=ffn2_norm, ffn2_w_gate_up=ffn2_w_gate_up, ffn2_w_down=ffn2_w_down, gdn_w_in=gdn_w_in,
                   gdn_conv=gdn_conv, gdn_A_log=gdn_A_log, gdn_dt_bias=gdn_dt_bias, gdn_out_norm=gdn_out_norm,
                   fox_w_in=fox_w_in, w_out=w_out, mem_norm=mem_norm, mem_w_kv=mem_w_kv, kv_norm=kv_norm, kv_w=kv_w,
                   kv_b_f=kv_b_f, final_norm=final_norm)
    mom_m = dict(ffn1_norm=m_ffn1_norm, ffn1_w_gate_up=m_ffn1_w_gate_up, ffn1_w_down=m_ffn1_w_down, mix_norm=m_mix_norm,
                 ffn2_norm=m_ffn2_norm, ffn2_w_gate_up=m_ffn2_w_gate_up, ffn2_w_down=m_ffn2_w_down, gdn_w_in=m_gdn_w_in,
                 gdn_conv=m_gdn_conv, gdn_A_log=m_gdn_A_log, gdn_dt_bias=m_gdn_dt_bias, gdn_out_norm=m_gdn_out_norm,
                 fox_w_in=m_fox_w_in, w_out=m_w_out, mem_norm=m_mem_norm, mem_w_kv=m_mem_w_kv, kv_norm=m_kv_norm,
                 kv_w=m_kv_w, kv_b_f=m_kv_b_f, final_norm=m_final_norm)
    mom_v = dict(ffn1_norm=v_ffn1_norm, ffn1_w_gate_up=v_ffn1_w_gate_up, ffn1_w_down=v_ffn1_w_down, mix_norm=v_mix_norm,
                 ffn2_norm=v_ffn2_norm, ffn2_w_gate_up=v_ffn2_w_gate_up, ffn2_w_down=v_ffn2_w_down, gdn_w_in=v_gdn_w_in,
                 gdn_conv=v_gdn_conv, gdn_A_log=v_gdn_A_log, gdn_dt_bias=v_gdn_dt_bias, gdn_out_norm=v_gdn_out_norm,
                 fox_w_in=v_fox_w_in, w_out=v_w_out, mem_norm=v_mem_norm, mem_w_kv=v_mem_w_kv, kv_norm=v_kv_norm,
                 kv_w=v_kv_w, kv_b_f=v_kv_b_f, final_norm=v_final_norm)
    names = list(weights)
    small_names = [n for n in names if weights[n].shape == mom_m[n].shape and n in (
        "ffn1_norm", "mix_norm", "ffn2_norm", "gdn_A_log", "gdn_dt_bias", "gdn_out_norm", "mem_norm", "kv_norm",
        "kv_b_f", "final_norm")]
    big_names = [n for n in names if n not in small_names]

    h = x[0]
    target = loss_target[0]
    mem_tokens = mem[0]
    s, d = h.shape
    depth = ffn1_norm.shape[0]
    n_a = gdn_w_in.shape[0]
    n_heads, head_dim = gdn_A_log.shape[1], gdn_out_norm.shape[1]
    gw = n_heads * head_dim
    a_in = gdn_w_in.shape[2]
    mem_w = a_in - 4 * gw - 2 * n_heads
    a_in_pad = 4 * gw + mem_w + LANES
    kv_width = kv_w.shape[1]
    kv_pad = 2 * gw + LANES
    fh = ffn1_w_down.shape[1] * N_DEV

    def permute_in(w):
        ab = w[..., 4 * gw:4 * gw + 2 * n_heads]
        return jnp.concatenate([w[..., :4 * gw], w[..., 4 * gw + 2 * n_heads:], _pad_lanes(ab, LANES)], axis=-1)

    def unpermute_in(w):
        return jnp.concatenate([w[..., :4 * gw], w[..., 4 * gw + mem_w:4 * gw + mem_w + 2 * n_heads],
                                w[..., 4 * gw:4 * gw + mem_w]], axis=-1)

    gathered = _all_gather("gather_weights", [
        ffn1_w_gate_up.astype(bf16), ffn1_w_down.astype(bf16), ffn2_w_gate_up.astype(bf16), ffn2_w_down.astype(bf16),
        permute_in(gdn_w_in).astype(bf16), fox_w_in.astype(bf16), w_out.astype(bf16), mem_w_kv.astype(bf16),
        _pad_lanes(kv_w, kv_pad).astype(bf16)[None], gdn_conv])
    wgu1, wd1_s, wgu2, wd2_s, win_s, wfox_s, wout_s, wmem_s, wkv_s, conv_s = gathered
    wd1, wd2 = _rows_from_shards(wd1_s), _rows_from_shards(wd2_s)
    win, wfox, wout = _rows_from_shards(win_s), _rows_from_shards(wfox_s), _rows_from_shards(wout_s)
    wmem = _rows_from_shards(wmem_s)
    wmem_cat = wmem.transpose(1, 0, 2).reshape(d, depth * 2 * mem_w)
    wkv = _rows_from_shards(wkv_s)[0]
    conv_w = conv_s.transpose(1, 2, 0, 3).reshape(n_a, gdn_conv.shape[1], 3 * gw)

    a_log_rows = [_row(gdn_A_log[l], LANES) for l in range(n_a)]
    dt_rows = [_row(gdn_dt_bias[l], LANES) for l in range(n_a)]
    onorm_rows = [_row(gdn_out_norm[l]) for l in range(n_a)]
    b_f_row = _row(kv_b_f, LANES)

    (mem_n,) = _rowwise("mem_norm", lambda t, g: (_rms(t, g),), [mem_tokens], [_row(mem_norm)], [(d, bf16)])
    mem_kv = _matmul("mem_kv", mem_n, wmem_cat, "nn", f32)

    saved = []
    shared = None
    for l in range(depth):
        rec = {"h0": h}
        h1, gate1, up1 = _ffn_fwd(f"ffn1_fwd_{l}", h, _row(ffn1_norm[l]), wgu1, wd1, l)
        (u,) = _rowwise(f"mix_norm_{l}", lambda t, g: (_rms(t, g),), [h1], [_row(mix_norm[l])], [(d, bf16)])
        rec.update(h1=h1, u=u, ffn1=(gate1, up1))
        if l < n_a:
            proj = _matmul(f"gdn_in_{l}", u, win[l], "nn", f32)
            yc = _conv_fwd(f"conv_fwd_{l}", V(proj, cb=0, w=3 * gw), conv_w[l])
            ab_view = V(proj, cb=(4 * gw + mem_w) // LANES, w=LANES)
            q, k, v, gb = _rowwise(f"gdn_pre_{l}", functools.partial(_gdn_pre, n_heads, head_dim),
                                   [yc, ab_view], [a_log_rows[l], dt_rows[l]],
                                   [(gw, f32), (gw, f32), (gw, f32), (LANES, f32)])
            o, states = _gdn_chunk_fwd(f"gdn_chunk_fwd_{l}", q, k, v, gb, n_heads, head_dim)
            z_view = V(proj, cb=3, w=gw)
            (main,) = _rowwise(f"gdn_post_{l}", functools.partial(_gdn_post, n_heads, head_dim),
                               [o, z_view], [onorm_rows[l]], [(gw, bf16)])
            qmem_view = V(proj, cb=4 * gw // mem_w, w=mem_w)
            rec.update(proj=proj, yc=yc, q=q, k=k, v=v, gb=gb, o=o, states=states)
        else:
            proj = _matmul(f"fox_in_{l}", u, wfox[l - n_a], "nn", bf16)
            sk, sv, crow = shared["k"], shared["v"], shared["crow"]
            main, lse = _fox_fwd(f"fox_fwd_{l}", proj, sk, sv, crow, n_heads, head_dim)
            qmem_view = V(proj, cb=gw // mem_w, w=mem_w)
            rec.update(proj=proj, lse=lse)
        km = V(mem_kv, cb=2 * l, w=mem_w)
        vm = V(mem_kv, cb=2 * l + 1, w=mem_w)
        (mem_out,) = _rowwise(f"mem_attn_{l}", _mem_attn, [qmem_view], [km, vm], [(mem_w, bf16)])
        cat = jnp.concatenate([main, mem_out], axis=1)
        h2 = _matmul(f"out_proj_{l}", cat, wout[l], "nn", f32, add=h1)
        h3, gate2, up2 = _ffn_fwd(f"ffn2_fwd_{l}", h2, _row(ffn2_norm[l]), wgu2, wd2, l)
        rec.update(cat=cat, h2=h2, qmem=qmem_view, ffn2=(gate2, up2))
        saved.append(rec)
        h = h3
        if l == n_a - 1:
            (hn,) = _rowwise("kv_norm", lambda t, g: (_rms(t, g),), [h], [_row(kv_norm)], [(d, bf16)])
            p = _matmul("kv_proj", hn, wkv, "nn", f32)
            pk, pv, pf = V(p, cb=0, w=gw), V(p, cb=1, w=gw), V(p, cb=2 * gw // LANES, w=LANES)
            sk, sv, log_f = _rowwise("kv_post", functools.partial(_kv_post, n_heads), [pk, pv, pf], [b_f_row],
                                     [(gw, bf16), (gw, bf16), (LANES, f32)])
            cum = _cumsum("forget_cumsum", [log_f], reverse=False)
            c_heads = cum[:, :n_heads].T
            shared = dict(k=sk, v=sv, crow=c_heads.reshape(n_heads, 1, s), h=h, hn=hn, p=p, views=(pk, pv, pf))

    loss_part, dh, d_final = _final_loss("final_loss", h, target, _row(final_norm))
    loss = lax.psum(loss_part[0, 0], ("x", "y", "c"))

    grads = {}
    per_layer = {n: [None] * depth for n in ("ffn1_norm", "mix_norm", "ffn2_norm", "ffn1_gu", "ffn1_d", "ffn2_gu",
                                             "ffn2_d", "w_out")}
    per_a = {n: [None] * n_a for n in ("gdn_w_in", "gdn_conv", "gdn_A_log", "gdn_dt_bias", "gdn_out_norm")}
    per_b = {"fox_w_in": [None] * (depth - n_a)}
    d_mem_kv = [None] * depth
    fox_grads = []

    def ffn_backward(tag, l, h_in, d_out, gain, kept, wgu, wd):
        parts, dwg, dwu, dwd = _ffn_bwd(f"{tag}_bwd_{l}", h_in, d_out, _row(gain), kept[0], kept[1], wgu, wd, l)
        nh = parts.shape[0]
        (d_in,), (d_gain,) = _rowwise_vjp(
            f"{tag}_norm_bwd_{l}", lambda t, g: (_rms(t, g),), [h_in], [_row(gain)],
            [[V(parts, lead=t) for t in range(nh)]], [f32], add=d_out)
        return d_in, d_gain, jnp.concatenate([dwg, dwu], axis=0), dwd

    for l in reversed(range(depth)):
        rec = saved[l]
        if l == n_a - 1:
            dk_list = [V(g["dk"]) for g in fox_grads]
            dv_list = [V(g["dv"]) for g in fox_grads]
            dc_parts = [_pad_lanes(part.reshape(n_heads, s).T, LANES) for g in fox_grads for part in g["dc"]]
            d_log_f = _cumsum("forget_cumsum_bwd", dc_parts, reverse=True)
            pk, pv, pf = shared["views"]
            (dpk, dpv, dpf), (d_bf,) = _rowwise_vjp(
                "kv_post_bwd", functools.partial(_kv_post, n_heads), [pk, pv, pf], [b_f_row],
                [dk_list, dv_list, [d_log_f]], [bf16, bf16, bf16])
            dp = jnp.concatenate([dpk, dpv, dpf], axis=1)
            d_hn = _matmul("kv_proj_dx", dp, wkv, "nt", f32)
            grads["kv_w"] = _matmul("kv_proj_dw", shared["hn"], dp, "tn", f32)[:, :kv_width]
            (dh,), (d_kvn,) = _rowwise_vjp("kv_norm_bwd", lambda t, g: (_rms(t, g),), [shared["h"]], [_row(kv_norm)],
                                           [[d_hn]], [f32], add=dh)
            grads["kv_norm"] = d_kvn.reshape(-1)
            grads["kv_b_f"] = d_bf[0, :n_heads]

        dh2, per_layer["ffn2_norm"][l], per_layer["ffn2_gu"][l], per_layer["ffn2_d"][l] = ffn_backward(
            "ffn2", l, rec["h2"], dh, ffn2_norm[l], rec["ffn2"], wgu2, wd2)
        d_cat = _matmul(f"out_proj_dx_{l}", dh2, wout[l], "nt", f32)
        per_layer["w_out"][l] = _matmul(f"out_proj_dw_{l}", rec["cat"], dh2, "tn", f32)
        d_main = V(d_cat, cb=0, w=gw)
        d_memo = V(d_cat, cb=gw // mem_w, w=mem_w)
        km, vm = V(mem_kv, cb=2 * l, w=mem_w), V(mem_kv, cb=2 * l + 1, w=mem_w)
        (dqmem,), (dkm, dvm) = _rowwise_vjp(f"mem_attn_bwd_{l}", _mem_attn, [rec["qmem"]], [km, vm], [[d_memo]], [bf16])
        d_mem_kv[l] = jnp.concatenate([dkm, dvm], axis=1)
        if l < n_a:
            proj = rec["proj"]
            z_view = V(proj, cb=3, w=gw)
            (d_o, d_z), (d_onorm,) = _rowwise_vjp(
                f"gdn_post_bwd_{l}", functools.partial(_gdn_post, n_heads, head_dim), [rec["o"], z_view],
                [onorm_rows[l]], [[d_main]], [f32, bf16])
            dq, dk, dv, dgb = _gdn_chunk_bwd(f"gdn_chunk_bwd_{l}", rec["q"], rec["k"], rec["v"], rec["gb"],
                                             rec["states"], d_o, n_heads, head_dim)
            ab_view = V(proj, cb=(4 * gw + mem_w) // LANES, w=LANES)
            (d_yc, d_ab), (d_alog, d_dt) = _rowwise_vjp(
                f"gdn_pre_bwd_{l}", functools.partial(_gdn_pre, n_heads, head_dim), [rec["yc"], ab_view],
                [a_log_rows[l], dt_rows[l]], [[dq], [dk], [dv], [dgb]], [f32, bf16])
            d_qkv, d_conv = _conv_bwd(f"conv_bwd_{l}", V(proj, cb=0, w=3 * gw), d_yc, conv_w[l])
            d_proj = jnp.concatenate([d_qkv, d_z, dqmem, d_ab], axis=1)
            du = _matmul(f"gdn_in_dx_{l}", d_proj, win[l], "nt", f32)
            per_a["gdn_w_in"][l] = unpermute_in(_matmul(f"gdn_in_dw_{l}", rec["u"], d_proj, "tn", f32))
            per_a["gdn_conv"][l] = d_conv
            per_a["gdn_A_log"][l] = d_alog[0, :n_heads]
            per_a["gdn_dt_bias"][l] = d_dt[0, :n_heads]
            per_a["gdn_out_norm"][l] = d_onorm[0]
        else:
            proj = rec["proj"]
            sk, sv, crow = shared["k"], shared["v"], shared["crow"]
            dq, delta, dc_col = _fox_bwd_dq(f"fox_dq_{l}", proj, sk, sv, crow, rec["lse"], rec["cat"], d_cat,
                                            n_heads, head_dim)
            dk, dv, dc_row = _fox_bwd_dkv(f"fox_dkv_{l}", proj, sk, sv, crow, rec["lse"], delta, d_cat,
                                          n_heads, head_dim)
            fox_grads.append(dict(dk=dk, dv=dv, dc=(dc_row, dc_col)))
            d_proj = jnp.concatenate([dq, dqmem], axis=1)
            du = _matmul(f"fox_in_dx_{l}", d_proj, wfox[l - n_a], "nt", f32)
            per_b["fox_w_in"][l - n_a] = _matmul(f"fox_in_dw_{l}", rec["u"], d_proj, "tn", f32)
        (dh1,), (d_mix,) = _rowwise_vjp(f"mix_norm_bwd_{l}", lambda t, g: (_rms(t, g),), [rec["h1"]],
                                        [_row(mix_norm[l])], [[du]], [f32], add=dh2)
        per_layer["mix_norm"][l] = d_mix
        dh, per_layer["ffn1_norm"][l], per_layer["ffn1_gu"][l], per_layer["ffn1_d"][l] = ffn_backward(
            "ffn1", l, rec["h0"], dh1, ffn1_norm[l], rec["ffn1"], wgu1, wd1)

    grad_x = dh[None]

    d_mem_kv_cat = jnp.concatenate(d_mem_kv, axis=1)
    d_wmem_cat = _matmul("mem_kv_dw", mem_n, d_mem_kv_cat, "tn", f32)
    d_mem_n = _matmul("mem_kv_dx", d_mem_kv_cat, wmem_cat, "nt", f32)
    _, (d_memnorm,) = _rowwise_vjp("mem_norm_bwd", lambda t, g: (_rms(t, g),), [mem_tokens], [_row(mem_norm)],
                                   [[d_mem_n]], [None])

    def gu_stack(per):
        return jnp.stack(per, axis=1).astype(bf16)

    stacks = dict(
        ffn1_w_gate_up=gu_stack(per_layer["ffn1_gu"]),
        ffn1_w_down=_rows_to_shards(jnp.stack(per_layer["ffn1_d"])).astype(bf16),
        ffn2_w_gate_up=gu_stack(per_layer["ffn2_gu"]),
        ffn2_w_down=_rows_to_shards(jnp.stack(per_layer["ffn2_d"])).astype(bf16),
        gdn_w_in=_rows_to_shards(jnp.stack(per_a["gdn_w_in"])).astype(bf16),
        gdn_conv=jnp.stack(per_a["gdn_conv"]).reshape(n_a, -1, N_DEV, 3 * gw // N_DEV).transpose(2, 0, 1, 3),
        fox_w_in=_rows_to_shards(jnp.stack(per_b["fox_w_in"])).astype(bf16),
        w_out=_rows_to_shards(jnp.stack(per_layer["w_out"])).astype(bf16),
        mem_w_kv=_rows_to_shards(d_wmem_cat.reshape(d, depth, 2 * mem_w).transpose(1, 0, 2)).astype(bf16),
        kv_w=_rows_to_shards(grads["kv_w"][None])[:, 0].astype(bf16),
    )
    received = dict(zip(big_names, _reduce_exchange([stacks[n] for n in big_names])))

    small_shapes = {n: weights[n].shape for n in small_names}
    pack = _SmallPack(small_shapes)
    small_grads = dict(
        ffn1_norm=jnp.concatenate(per_layer["ffn1_norm"], axis=0), mix_norm=jnp.concatenate(per_layer["mix_norm"], axis=0),
        ffn2_norm=jnp.concatenate(per_layer["ffn2_norm"], axis=0), gdn_A_log=jnp.stack(per_a["gdn_A_log"]),
        gdn_dt_bias=jnp.stack(per_a["gdn_dt_bias"]), gdn_out_norm=jnp.stack(per_a["gdn_out_norm"]),
        mem_norm=d_memnorm.reshape(-1), kv_norm=grads["kv_norm"], kv_b_f=grads["kv_b_f"], final_norm=d_final.reshape(-1))
    (small_parts,) = _all_gather("gather_small_grads", [pack.pack(small_grads)])

    out_g, out_d, out_m, out_v = {}, {}, {}, {}
    for n in big_names:
        shape = weights[n].shape
        c = shape[-1]
        parts = received[n].reshape(N_CHIPS, -1, c)
        res = _adamw(f"adamw_{n}", parts, weights[n].reshape(-1, c), mom_m[n].reshape(-1, c), mom_v[n].reshape(-1, c))
        out_g[n], out_d[n], out_m[n], out_v[n] = [r.reshape(shape) for r in res]
    res = _adamw("adamw_small", small_parts, pack.pack({n: weights[n] for n in small_names}),
                 pack.pack({n: mom_m[n] for n in small_names}), pack.pack({n: mom_v[n] for n in small_names}))
    for dst, packed in zip((out_g, out_d, out_m, out_v), res):
        dst.update(pack.unpack(packed))

    return (loss, grad_x, *[out_g[n] for n in names], *[out_d[n] for n in names],
            *[out_m[n] for n in names], *[out_v[n] for n in names])
```

```python
import functools
import math

import jax
import jax.numpy as jnp
from jax import lax
from jax.experimental import pallas as pl
from jax.experimental.pallas import tpu as pltpu

f32 = jnp.float32
bf16 = jnp.bfloat16
SDS = jax.ShapeDtypeStruct
HIGHEST = lax.Precision.HIGHEST

N_DEV = 8
MEM_HEADS = 4
CHUNK = 64
LANES = 128
EPS = 1e-6
NEG_INF = -1e30
ADAM_LR = 0.001
ADAM_B1 = 0.9
ADAM_B2 = 0.999
ADAM_EPS = 1e-08
ADAM_WD = 0.01
ADAM_STEP = 10

ROW_TILE = 512
MM_TILE = 512
FFN_FWD_TILE = 1024
FFN_BWD_TILE = 512
ATT_TILE = 1024
ATT_HEADS = 3
CUMSUM_TILE = 256
VMEM_LIMIT = 56 * 1024 * 1024

MESH = pl.DeviceIdType.MESH


def _cp(sem=None):
    return pltpu.CompilerParams(dimension_semantics=sem, vmem_limit_bytes=VMEM_LIMIT)


_DIMS = {"nn": (((1,), (0,)), ((), ())), "nt": (((1,), (1,)), ((), ())), "tn": (((0,), (0,)), ((), ()))}


def _dg(a, b, mode):
    return lax.dot_general(a.astype(bf16), b.astype(bf16), _DIMS[mode], preferred_element_type=f32)


@functools.partial(jax.custom_vjp, nondiff_argnums=(2,))
def _mm(a, b, mode):
    return _dg(a, b, mode)


def _mm_fwd(a, b, mode):
    return _dg(a, b, mode), (a, b)


def _mm_bwd(mode, res, ct):
    a, b = res
    if mode == "nn":
        da, db = _dg(ct, b, "nt"), _dg(a, ct, "tn")
    elif mode == "nt":
        da, db = _dg(ct, b, "nn"), _dg(ct, a, "tn")
    else:
        da, db = _dg(b, ct, "nt"), _dg(a, ct, "nn")
    return da.astype(a.dtype), db.astype(b.dtype)


_mm.defvjp(_mm_fwd, _mm_bwd)


def _split_bf16(x):
    hi = x.astype(bf16)
    return hi, (x - hi.astype(f32)).astype(bf16)


def _dgh(a, b, mode="nn"):
    a_hi, a_lo = _split_bf16(a)
    b_hi, b_lo = _split_bf16(b)
    dims = _DIMS[mode]
    return (lax.dot_general(a_hi, b_hi, dims, preferred_element_type=f32)
            + lax.dot_general(a_hi, b_lo, dims, preferred_element_type=f32)
            + lax.dot_general(a_lo, b_hi, dims, preferred_element_type=f32))


@jax.custom_vjp
def _unit_lower_inverses(lows):
    c = lows[0].shape[0]
    ri = lax.broadcasted_iota(jnp.int32, (c, c), 0)
    ci = lax.broadcasted_iota(jnp.int32, (c, c), 1)
    eye = jnp.where(ri == ci, 1.0, 0.0)
    xs = [-low for low in lows]
    rs = [eye + x for x in xs]
    for _ in range(int(math.log2(c)) - 1):
        xs = [_dgh(x, x) for x in xs]
        rs = [r + _dgh(r, x) for r, x in zip(rs, xs)]
    return tuple(rs)


def _uli_fwd(lows):
    ts = _unit_lower_inverses(lows)
    return ts, ts


def _uli_bwd(ts, cts):
    mids = [_dgh(ct, t, "nt") for t, ct in zip(ts, cts)]
    return (tuple(-_dgh(t, m, "tn") for t, m in zip(ts, mids)),)


_unit_lower_inverses.defvjp(_uli_fwd, _uli_bwd)


@functools.partial(jax.custom_vjp, nondiff_argnums=(1,))
def _split_lanes(x, width):
    return tuple(x[:, i * width:(i + 1) * width] for i in range(x.shape[1] // width))


def _split_fwd(x, width):
    return _split_lanes(x, width), None


def _split_bwd(width, _, cts):
    return (jnp.concatenate(list(cts), axis=1),)


_split_lanes.defvjp(_split_fwd, _split_bwd)


def _sigmoid(x):
    return 1.0 / (1.0 + jnp.exp(-x))


def _silu(x):
    return x * _sigmoid(x)


def _softplus(x):
    return jnp.maximum(x, 0.0) + jnp.log1p(jnp.exp(-jnp.abs(x)))


def _rms(x, gain):
    return x * lax.rsqrt(jnp.mean(x * x, axis=-1, keepdims=True) + EPS) * gain


class V:
    def __init__(self, arr, lead=None, cb=0, w=None):
        self.arr, self.lead, self.cb = arr, lead, cb
        self.w = arr.shape[-1] if w is None else w

    @property
    def rows(self):
        return self.arr.shape[-2]

    def spec(self, tile, order=None):
        lead, cb, w = self.lead, self.cb, self.w
        order = order or (lambda i: i)
        if lead is None:
            return pl.BlockSpec((tile, w), lambda i: (order(i), cb))
        return pl.BlockSpec((None, tile, w), lambda i: (lead, order(i), cb))

    def const_spec(self):
        lead, cb, w, r = self.lead, self.cb, self.w, self.rows
        if lead is None:
            return pl.BlockSpec((r, w), lambda i: (0, cb))
        return pl.BlockSpec((None, r, w), lambda i: (lead, 0, cb))


def _v(a):
    return a if isinstance(a, V) else V(a)


def _rowwise(name, fn, rows, consts, outs, tile=None):
    rows = [_v(r) for r in rows]
    consts = [_v(c) for c in consts]
    s = rows[0].rows
    tile = min(tile or ROW_TILE, s)
    nr, nc = len(rows), len(consts)

    def body(*refs):
        vals = [r[...].astype(f32) for r in refs[:nr + nc]]
        res = fn(*vals)
        for o, val in zip(refs[nr + nc:], res):
            o[...] = val.astype(o.dtype)

    return pl.pallas_call(
        body, name=name, grid=(s // tile,),
        in_specs=[r.spec(tile) for r in rows] + [c.const_spec() for c in consts],
        out_specs=[pl.BlockSpec((tile, w), lambda i: (i, 0)) for w, _ in outs],
        out_shape=[SDS((s, w), dt) for w, dt in outs],
        compiler_params=_cp(("parallel",)),
    )(*[r.arr for r in rows], *[c.arr for c in consts])


def _rowwise_vjp(name, fn, rows, consts, cts, d_rows, add=None, tile=None):
    rows = [_v(r) for r in rows]
    consts = [_v(c) for c in consts]
    cts = [[_v(c) for c in group] for group in cts]
    flat_cts = [c for group in cts for c in group]
    s = rows[0].rows
    tile = min(tile or ROW_TILE, s)
    nr, nc, nt = len(rows), len(consts), len(flat_cts)
    want = [k for k, dt in enumerate(d_rows) if dt is not None]
    has_add = add is not None
    add_v = [_v(add)] if has_add else []

    def body(*refs):
        vals = [r[...].astype(f32) for r in refs[:nr + nc]]
        ct_refs = refs[nr + nc:nr + nc + nt]
        pos = nr + nc + nt
        add_ref = refs[pos] if has_add else None
        pos += 1 if has_add else 0
        drow_refs = refs[pos:pos + len(want)]
        dconst_refs = refs[pos + len(want):]
        ctv, at = [], 0
        for group in cts:
            acc = ct_refs[at][...].astype(f32)
            for r in ct_refs[at + 1:at + len(group)]:
                acc = acc + r[...].astype(f32)
            at += len(group)
            ctv.append(acc)
        _, vjp = jax.vjp(fn, *vals)
        grads = vjp(tuple(ctv))
        for o, k in zip(drow_refs, want):
            g = grads[k]
            if has_add and k == want[0]:
                g = g + add_ref[...].astype(f32)
            o[...] = g.astype(o.dtype)

        @pl.when(pl.program_id(0) == 0)
        def _():
            for o in dconst_refs:
                o[...] = jnp.zeros_like(o)

        for o, g in zip(dconst_refs, grads[nr:]):
            o[...] += g

    outs = pl.pallas_call(
        body, name=name, grid=(s // tile,),
        in_specs=[r.spec(tile) for r in rows] + [c.const_spec() for c in consts]
        + [c.spec(tile) for c in flat_cts] + [a.spec(tile) for a in add_v],
        out_specs=[pl.BlockSpec((tile, rows[k].w), lambda i: (i, 0)) for k in want]
        + [pl.BlockSpec((c.rows, c.w), lambda i: (0, 0)) for c in consts],
        out_shape=[SDS((s, rows[k].w), d_rows[k]) for k in want] + [SDS((c.rows, c.w), f32) for c in consts],
        compiler_params=_cp(("arbitrary",)),
    )(*[r.arr for r in rows], *[c.arr for c in consts], *[c.arr for c in flat_cts], *[a.arr for a in add_v])
    return list(outs[:len(want)]), list(outs[len(want):])


def _pick(n, cap):
    best = None
    for t in range(LANES, min(n, cap) + 1, LANES):
        if n % t == 0:
            best = t
    return best or n


def _matmul(name, a, b, mode, out_dtype, add=None, tn_cap=1280):
    has_add = add is not None
    if mode in ("nn", "nt"):
        m, k = a.shape
        n = b.shape[1] if mode == "nn" else b.shape[0]
        tm = min(MM_TILE, m)
        tn = _pick(n, tn_cap) if k * n * 2 > (8 << 20) else n

        def body(*refs):
            a_ref, b_ref = refs[0], refs[1]
            o_ref = refs[-1]
            acc = _dg(a_ref[...], b_ref[...], mode)
            if has_add:
                acc = acc + refs[2][...].astype(f32)
            o_ref[...] = acc.astype(o_ref.dtype)

        b_spec = pl.BlockSpec((k, tn), lambda i, j: (0, j)) if mode == "nn" else pl.BlockSpec((tn, k), lambda i, j: (j, 0))
        in_specs = [pl.BlockSpec((tm, k), lambda i, j: (i, 0)), b_spec]
        args = [a, b]
        if has_add:
            in_specs.append(pl.BlockSpec((tm, tn), lambda i, j: (i, j)))
            args.append(add)
        return pl.pallas_call(
            body, name=name, grid=(m // tm, n // tn), in_specs=in_specs,
            out_specs=pl.BlockSpec((tm, tn), lambda i, j: (i, j)),
            out_shape=SDS((m, n), out_dtype), compiler_params=_cp(("parallel", "parallel")),
        )(*args)
    kk, m = a.shape
    n = b.shape[1]
    tk = min(MM_TILE, kk)
    tn = _pick(n, tn_cap) if m * n * 4 > (7 << 20) else n

    def body_tn(a_ref, b_ref, o_ref):
        @pl.when(pl.program_id(1) == 0)
        def _():
            o_ref[...] = jnp.zeros_like(o_ref)

        o_ref[...] += _dg(a_ref[...], b_ref[...], "tn")

    return pl.pallas_call(
        body_tn, name=name, grid=(n // tn, kk // tk),
        in_specs=[pl.BlockSpec((tk, m), lambda j, k: (k, 0)), pl.BlockSpec((tk, tn), lambda j, k: (k, j))],
        out_specs=pl.BlockSpec((m, tn), lambda j, k: (0, j)),
        out_shape=SDS((m, n), f32), compiler_params=_cp(("parallel", "arbitrary")),
    )(a, b)


def _ffn_fwd(name, h, gain, wgu, wd, layer):
    s, d = h.shape
    hs = wgu.shape[3]
    nh = wgu.shape[0] // 2
    tm = min(FFN_FWD_TILE, s)

    def body(h_ref, g_ref, wg_ref, wu_ref, wd_ref, o_ref, gate_ref, up_ref, n_scr, acc_scr):
        t = pl.program_id(1)

        @pl.when(t == 0)
        def _():
            n_scr[...] = _rms(h_ref[...], g_ref[...]).astype(bf16)
            acc_scr[...] = jnp.zeros_like(acc_scr)

        n = n_scr[...]
        gate = _dg(n, wg_ref[...], "nn")
        up = _dg(n, wu_ref[...], "nn")
        gate_ref[...] = gate.astype(gate_ref.dtype)
        up_ref[...] = up.astype(up_ref.dtype)
        acc_scr[...] += _dg(_silu(gate) * up, wd_ref[...], "nn")

        @pl.when(t == nh - 1)
        def _():
            o_ref[...] = h_ref[...] + 0.5 * acc_scr[...]

    saved_spec = pl.BlockSpec((None, tm, hs), lambda i, t: (t, i, 0))
    return pl.pallas_call(
        body, name=name, grid=(s // tm, nh),
        in_specs=[
            pl.BlockSpec((tm, d), lambda i, t: (i, 0)),
            pl.BlockSpec((1, d), lambda i, t: (0, 0)),
            pl.BlockSpec((None, None, d, hs), lambda i, t: (t, layer, 0, 0)),
            pl.BlockSpec((None, None, d, hs), lambda i, t: (t + nh, layer, 0, 0)),
            pl.BlockSpec((None, hs, d), lambda i, t: (layer, t, 0)),
        ],
        out_specs=[pl.BlockSpec((tm, d), lambda i, t: (i, 0)), saved_spec, saved_spec],
        out_shape=[SDS((s, d), f32), SDS((nh, s, hs), bf16), SDS((nh, s, hs), bf16)],
        scratch_shapes=[pltpu.VMEM((tm, d), bf16), pltpu.VMEM((tm, d), f32)],
        compiler_params=_cp(("parallel", "arbitrary")),
    )(h, gain, wgu, wgu, wd)


def _ffn_bwd(name, h, dout, gain, gate_s, up_s, wgu, wd, layer):
    s, d = h.shape
    hs = wgu.shape[3]
    nh = wgu.shape[0] // 2
    tm = min(FFN_BWD_TILE, s)

    def body(h_ref, do_ref, g_ref, gate_ref, up_ref, wg_ref, wu_ref, wd_ref, dn_ref, dwg_ref, dwu_ref, dwd_ref):
        @pl.when(pl.program_id(1) == 0)
        def _():
            dwg_ref[...] = jnp.zeros_like(dwg_ref)
            dwu_ref[...] = jnp.zeros_like(dwu_ref)
            dwd_ref[...] = jnp.zeros_like(dwd_ref)

        n = _rms(h_ref[...], g_ref[...]).astype(bf16)
        wg, wu, wdn = wg_ref[...], wu_ref[...], wd_ref[...]
        gate = gate_ref[...].astype(f32)
        up = up_ref[...].astype(f32)
        sg = _sigmoid(gate)
        act = gate * sg
        dy = (0.5 * do_ref[...]).astype(bf16)
        da = _dg(dy, wdn, "nt")
        dup = (da * act).astype(bf16)
        dgate = (da * up * (sg * (1.0 + gate * (1.0 - sg)))).astype(bf16)
        dwd_ref[...] += _dg(act * up, dy, "tn")
        dwg_ref[...] += _dg(n, dgate, "tn")
        dwu_ref[...] += _dg(n, dup, "tn")
        dn_ref[...] = (_dg(dgate, wg, "nt") + _dg(dup, wu, "nt")).astype(dn_ref.dtype)

    return pl.pallas_call(
        body, name=name, grid=(nh, s // tm),
        in_specs=[
            pl.BlockSpec((tm, d), lambda t, i: (i, 0)),
            pl.BlockSpec((tm, d), lambda t, i: (i, 0)),
            pl.BlockSpec((1, d), lambda t, i: (0, 0)),
            pl.BlockSpec((None, tm, hs), lambda t, i: (t, i, 0)),
            pl.BlockSpec((None, tm, hs), lambda t, i: (t, i, 0)),
            pl.BlockSpec((None, None, d, hs), lambda t, i: (t, layer, 0, 0)),
            pl.BlockSpec((None, None, d, hs), lambda t, i: (t + nh, layer, 0, 0)),
            pl.BlockSpec((None, hs, d), lambda t, i: (layer, t, 0)),
        ],
        out_specs=[
            pl.BlockSpec((None, tm, d), lambda t, i: (t, i, 0)),
            pl.BlockSpec((None, d, hs), lambda t, i: (t, 0, 0)),
            pl.BlockSpec((None, d, hs), lambda t, i: (t, 0, 0)),
            pl.BlockSpec((hs, d), lambda t, i: (t, 0)),
        ],
        out_shape=[SDS((nh, s, d), bf16), SDS((nh, d, hs), f32), SDS((nh, d, hs), f32), SDS((nh * hs, d), f32)],
        compiler_params=_cp(("parallel", "arbitrary")),
    )(h, dout, gain, gate_s, up_s, wgu, wgu, wd)


def _conv_fwd(name, x, w):
    x = _v(x)
    s, c = x.rows, x.w
    cw = w.shape[0]
    tile = min(ROW_TILE, s)
    cb = x.cb

    def body(x_ref, halo_ref, w_ref, o_ref, buf):
        first = pl.program_id(0) == 0
        buf[0:8, :] = jnp.where(first, 0.0, halo_ref[...])
        buf[8:8 + tile, :] = x_ref[...]
        acc = w_ref[0:1, :] * buf[pl.ds(8 - cw + 1, tile), :]
        for j in range(1, cw):
            acc = acc + w_ref[j:j + 1, :] * buf[pl.ds(8 - cw + 1 + j, tile), :]
        o_ref[...] = acc

    return pl.pallas_call(
        body, name=name, grid=(s // tile,),
        in_specs=[
            pl.BlockSpec((tile, c), lambda i: (i, cb)),
            pl.BlockSpec((8, c), lambda i: (jnp.maximum(i * (tile // 8) - 1, 0), cb)),
            pl.BlockSpec((cw, c), lambda i: (0, 0)),
        ],
        out_specs=pl.BlockSpec((tile, c), lambda i: (i, 0)),
        out_shape=SDS((s, c), f32),
        scratch_shapes=[pltpu.VMEM((tile + 8, c), f32)],
        compiler_params=_cp(("parallel",)),
    )(x.arr, x.arr, w)


def _conv_bwd(name, x, dy, w):
    x = _v(x)
    s, c = x.rows, x.w
    cw = w.shape[0]
    tile = min(ROW_TILE, s)
    n_tiles = s // tile
    cb = x.cb

    def body(x_ref, xh_ref, dy_ref, dyh_ref, w_ref, dx_ref, dw_ref, xbuf, dbuf):
        i = pl.program_id(0)
        xbuf[0:8, :] = jnp.where(i == 0, 0.0, xh_ref[...])
        xbuf[8:8 + tile, :] = x_ref[...]
        dyv = dy_ref[...]
        dbuf[0:tile, :] = dyv
        dbuf[tile:tile + 8, :] = jnp.where(i == n_tiles - 1, 0.0, dyh_ref[...])

        @pl.when(i == 0)
        def _():
            dw_ref[...] = jnp.zeros_like(dw_ref)

        acc = w_ref[0:1, :] * dbuf[pl.ds(cw - 1, tile), :]
        for j in range(1, cw):
            acc = acc + w_ref[j:j + 1, :] * dbuf[pl.ds(cw - 1 - j, tile), :]
        dx_ref[...] = acc.astype(dx_ref.dtype)
        for j in range(cw):
            dw_ref[j:j + 1, :] += jnp.sum(xbuf[pl.ds(8 - cw + 1 + j, tile), :] * dyv, axis=0, keepdims=True)

    return pl.pallas_call(
        body, name=name, grid=(n_tiles,),
        in_specs=[
            pl.BlockSpec((tile, c), lambda i: (i, cb)),
            pl.BlockSpec((8, c), lambda i: (jnp.maximum(i * (tile // 8) - 1, 0), cb)),
            pl.BlockSpec((tile, c), lambda i: (i, 0)),
            pl.BlockSpec((8, c), lambda i: (jnp.minimum((i + 1) * (tile // 8), s // 8 - 1), 0)),
            pl.BlockSpec((cw, c), lambda i: (0, 0)),
        ],
        out_specs=[pl.BlockSpec((tile, c), lambda i: (i, 0)), pl.BlockSpec((cw, c), lambda i: (0, 0))],
        out_shape=[SDS((s, c), bf16), SDS((cw, c), f32)],
        scratch_shapes=[pltpu.VMEM((tile + 8, c), f32), pltpu.VMEM((tile + 8, c), f32)],
        compiler_params=_cp(("arbitrary",)),
    )(x.arr, x.arr, dy, dy, w)


def _gdn_pre(n_heads, head_dim, yc, ab, a_log, dt_bias):
    gw = n_heads * head_dim
    act = _silu(yc)
    parts = _split_lanes(act, head_dim)
    qs = [p * lax.rsqrt(jnp.sum(p * p, axis=-1, keepdims=True) + EPS) * (head_dim ** -0.5) for p in parts[:n_heads]]
    ks = [p * lax.rsqrt(jnp.sum(p * p, axis=-1, keepdims=True) + EPS) for p in parts[n_heads:2 * n_heads]]
    lane = lax.broadcasted_iota(jnp.int32, ab.shape, 1)
    g = -jnp.exp(a_log) * _softplus(ab + dt_bias)
    gb = jnp.where(lane < n_heads, g, jnp.where(lane < 2 * n_heads, _sigmoid(ab), 0.0))
    del gw
    return (jnp.concatenate(qs, axis=1), jnp.concatenate(ks, axis=1),
            jnp.concatenate(list(parts[2 * n_heads:]), axis=1), gb)


def _gdn_post(n_heads, head_dim, o, z, out_norm):
    parts = _split_lanes(o, head_dim)
    normed = jnp.concatenate([_rms(p, out_norm) for p in parts], axis=1)
    return (normed * _silu(z),)


def _gdn_chunk(n_heads, head_dim, q, k, v, gb, *states):
    c = q.shape[0]
    ri = lax.broadcasted_iota(jnp.int32, (c, c), 0)
    ci = lax.broadcasted_iota(jnp.int32, (c, c), 1)
    incl, strict, diag = ri >= ci, ri > ci, ri == ci
    lane = lax.broadcasted_iota(jnp.int32, gb.shape, 1)
    qs, ks, vs = _split_lanes(q, head_dim), _split_lanes(k, head_dim), _split_lanes(v, head_dim)
    heads = range(n_heads)
    g = [jnp.sum(jnp.where(lane == h, gb, 0.0), axis=1, keepdims=True) for h in heads]
    beta = [jnp.sum(jnp.where(lane == n_heads + h, gb, 0.0), axis=1, keepdims=True) for h in heads]
    g_row = [jnp.sum(jnp.where(diag, g[h], 0.0), axis=0, keepdims=True) for h in heads]
    cg_col = [jnp.sum(jnp.where(incl, g_row[h], 0.0), axis=1, keepdims=True) for h in heads]
    cg_row = [jnp.sum(jnp.where(ri <= ci, g[h], 0.0), axis=0, keepdims=True) for h in heads]
    g_last = [jnp.sum(g[h], axis=0, keepdims=True) for h in heads]
    decay = [jnp.where(incl, jnp.exp(jnp.where(incl, cg_col[h] - cg_row[h], 0.0)), 0.0) for h in heads]
    kb = [ks[h] * beta[h] for h in heads]
    lower = [jnp.where(strict, _mm(kb[h], ks[h], "nt") * decay[h], 0.0) for h in heads]
    eye = jnp.where(diag, 1.0, 0.0)
    off_diag = [t - eye for t in _unit_lower_inverses(tuple(lower))]
    e_col = [jnp.exp(cg_col[h]) for h in heads]
    vb = [vs[h] * beta[h] for h in heads]
    kbg = [kb[h] * e_col[h] for h in heads]
    u = [vb[h] + _mm(off_diag[h], vb[h], "nn") for h in heads]
    w = [kbg[h] + _mm(off_diag[h], kbg[h], "nn") for h in heads]
    qk = [jnp.where(incl, _mm(qs[h], ks[h], "nt") * decay[h], 0.0) for h in heads]
    v_new = [u[h] - _mm(w[h], states[h], "nn") for h in heads]
    inter = [_mm(qs[h] * e_col[h], states[h], "nn") for h in heads]
    outs = [inter[h] + _mm(qk[h], v_new[h], "nn") for h in heads]
    k_tail = [ks[h] * jnp.exp(g_last[h] - cg_col[h]) for h in heads]
    new_states = [states[h] * jnp.exp(g_last[h]) + _mm(k_tail[h], v_new[h], "tn") for h in heads]
    return (jnp.concatenate(outs, axis=1), *new_states)


def _gdn_chunk_fwd(name, q, k, v, gb, n_heads, head_dim):
    s, gw = q.shape
    n = s // CHUNK
    fn = functools.partial(_gdn_chunk, n_heads, head_dim)

    def body(q_ref, k_ref, v_ref, gb_ref, o_ref, st_ref, st_scr):
        @pl.when(pl.program_id(0) == 0)
        def _():
            st_scr[...] = jnp.zeros_like(st_scr)

        st_ref[...] = st_scr[...]
        res = fn(q_ref[...], k_ref[...], v_ref[...], gb_ref[...], *[st_scr[h] for h in range(n_heads)])
        o_ref[...] = res[0]
        for h in range(n_heads):
            st_scr[h] = res[1 + h]

    row = lambda w: pl.BlockSpec((CHUNK, w), lambda i: (i, 0))
    return pl.pallas_call(
        body, name=name, grid=(n,),
        in_specs=[row(gw), row(gw), row(gw), row(LANES)],
        out_specs=[row(gw), pl.BlockSpec((None, n_heads, head_dim, head_dim), lambda i: (i, 0, 0, 0))],
        out_shape=[SDS((s, gw), f32), SDS((n, n_heads, head_dim, head_dim), f32)],
        scratch_shapes=[pltpu.VMEM((n_heads, head_dim, head_dim), f32)],
        compiler_params=_cp(("arbitrary",)),
    )(q, k, v, gb)


def _gdn_chunk_bwd(name, q, k, v, gb, states, d_out, n_heads, head_dim):
    s, gw = q.shape
    n = s // CHUNK
    fn = functools.partial(_gdn_chunk, n_heads, head_dim)

    def body(q_ref, k_ref, v_ref, gb_ref, st_ref, do_ref, dq_ref, dk_ref, dv_ref, dgb_ref, dst_scr):
        @pl.when(pl.program_id(0) == 0)
        def _():
            dst_scr[...] = jnp.zeros_like(dst_scr)

        _, vjp = jax.vjp(fn, q_ref[...], k_ref[...], v_ref[...], gb_ref[...], *[st_ref[h] for h in range(n_heads)])
        grads = vjp((do_ref[...].astype(f32), *[dst_scr[h] for h in range(n_heads)]))
        dq_ref[...] = grads[0]
        dk_ref[...] = grads[1]
        dv_ref[...] = grads[2]
        dgb_ref[...] = grads[3]
        for h in range(n_heads):
            dst_scr[h] = grads[4 + h]

    row = lambda w: pl.BlockSpec((CHUNK, w), lambda i: (n - 1 - i, 0))
    return pl.pallas_call(
        body, name=name, grid=(n,),
        in_specs=[row(gw), row(gw), row(gw), row(LANES),
                  pl.BlockSpec((None, n_heads, head_dim, head_dim), lambda i: (n - 1 - i, 0, 0, 0)), row(gw)],
        out_specs=[row(gw), row(gw), row(gw), row(LANES)],
        out_shape=[SDS((s, gw), f32), SDS((s, gw), f32), SDS((s, gw), f32), SDS((s, LANES), f32)],
        scratch_shapes=[pltpu.VMEM((n_heads, head_dim, head_dim), f32)],
        compiler_params=_cp(("arbitrary",)),
    )(q, k, v, gb, states, d_out)


def _mem_attn(qm, km, vm):
    width = qm.shape[1]
    hd = width // MEM_HEADS
    lane = lax.broadcasted_iota(jnp.int32, (1, width), 1)
    out = jnp.zeros_like(qm)
    for h in range(MEM_HEADS):
        msk = jnp.where((lane >= h * hd) & (lane < (h + 1) * hd), 1.0, 0.0)
        logits = _mm(qm * msk, km, "nt") * (hd ** -0.5)
        p = jnp.exp(logits - jnp.max(logits, axis=-1, keepdims=True))
        p = p / jnp.sum(p, axis=-1, keepdims=True)
        out = out + _mm(p, vm, "nn") * msk
    return (out,)


def _kv_post(n_heads, pk, pv, pf, b_f):
    lane = lax.broadcasted_iota(jnp.int32, pf.shape, 1)
    log_f = jnp.where(lane < n_heads, -_softplus(-(pf + b_f)), 0.0)
    return pk, pv, log_f


def _cumsum(name, xs, reverse):
    s, w = xs[0].shape
    tile = min(CUMSUM_TILE, s)
    n = s // tile

    def body(*refs):
        x_refs, o_ref, carry = refs[:-2], refs[-2], refs[-1]

        @pl.when(pl.program_id(0) == 0)
        def _():
            carry[...] = jnp.zeros_like(carry)

        xv = x_refs[0][...]
        for r in x_refs[1:]:
            xv = xv + r[...]
        ri = lax.broadcasted_iota(jnp.int32, (tile, tile), 0)
        ci = lax.broadcasted_iota(jnp.int32, (tile, tile), 1)
        tri = jnp.where((ri <= ci) if reverse else (ri >= ci), 1.0, 0.0).astype(bf16)
        x1 = xv.astype(bf16)
        r1 = xv - x1.astype(f32)
        x2 = r1.astype(bf16)
        x3 = (r1 - x2.astype(f32)).astype(bf16)
        acc = carry[...] + _dg(tri, x1, "nn") + _dg(tri, x2, "nn") + _dg(tri, x3, "nn")
        o_ref[...] = acc
        carry[...] += jnp.sum(xv, axis=0, keepdims=True)

    order = (lambda i: (n - 1 - i, 0)) if reverse else (lambda i: (i, 0))
    return pl.pallas_call(
        body, name=name, grid=(n,),
        in_specs=[pl.BlockSpec((tile, w), order)] * len(xs), out_specs=pl.BlockSpec((tile, w), order),
        out_shape=SDS((s, w), f32), scratch_shapes=[pltpu.VMEM((1, w), f32)],
        compiler_params=_cp(("arbitrary",)),
    )(*xs)


LOG2_E = math.log2(math.e)


def _fox_logits(q_ref, k_ref, cr_ref, hh, head_dim, scale, diagonal):
    sl = slice(hh * head_dim, (hh + 1) * head_dim)
    s = _dg(q_ref[:, sl], k_ref[:, sl], "nt") * (scale * LOG2_E) - cr_ref[hh] * LOG2_E
    if diagonal:
        tq, tk = s.shape
        ok = lax.broadcasted_iota(jnp.int32, (tq, tk), 1) <= lax.broadcasted_iota(jnp.int32, (tq, tk), 0)
        s = jnp.where(ok, s, NEG_INF)
    return s, sl


def _on_causal_tiles(i, j, fn):
    @pl.when(j < i)
    def _():
        fn(False)

    @pl.when(j == i)
    def _():
        fn(True)


def _fox_fwd(name, q, k, v, crow, n_heads, head_dim):
    s = k.shape[0]
    tq = tk = min(ATT_TILE, s)
    nq, nk = s // tq, s // tk
    scale = head_dim ** -0.5
    hpb = ATT_HEADS
    wb = hpb * head_dim

    def body(q_ref, k_ref, v_ref, cr_ref, o_ref, lse_ref, m_scr, l_scr, acc_scr):
        i, j = pl.program_id(1), pl.program_id(2)

        @pl.when(j == 0)
        def _():
            m_scr[...] = jnp.full_like(m_scr, NEG_INF)
            l_scr[...] = jnp.zeros_like(l_scr)
            acc_scr[...] = jnp.zeros_like(acc_scr)

        def step(diagonal):
            heads = range(hpb)
            sl = [slice(hh * head_dim, (hh + 1) * head_dim) for hh in heads]
            sc = [_fox_logits(q_ref, k_ref, cr_ref, hh, head_dim, scale, diagonal)[0] for hh in heads]
            m_old = [m_scr[hh] for hh in heads]
            m_new = [jnp.maximum(m_old[hh], jnp.max(sc[hh], axis=-1, keepdims=True)) for hh in heads]
            p = [jnp.exp2(sc[hh] - m_new[hh]) for hh in heads]
            alpha = [jnp.exp2(m_old[hh] - m_new[hh]) for hh in heads]
            pv = [_dg(p[hh], v_ref[:, sl[hh]], "nn") for hh in heads]
            for hh in heads:
                l_scr[hh] = alpha[hh] * l_scr[hh] + jnp.sum(p[hh], axis=-1, keepdims=True)
                acc_scr[:, sl[hh]] = alpha[hh] * acc_scr[:, sl[hh]] + pv[hh]
                m_scr[hh] = m_new[hh]

        _on_causal_tiles(i, j, step)

        @pl.when(j == nk - 1)
        def _():
            for hh in range(hpb):
                sl = slice(hh * head_dim, (hh + 1) * head_dim)
                o_ref[:, sl] = (acc_scr[:, sl] / l_scr[hh]).astype(o_ref.dtype)
                lse_ref[hh] = m_scr[hh] + jnp.log2(l_scr[hh])

    return pl.pallas_call(
        body, name=name, grid=(n_heads // hpb, nq, nk),
        in_specs=[
            pl.BlockSpec((tq, wb), lambda g, i, j: (i, g)),
            pl.BlockSpec((tk, wb), lambda g, i, j: (jnp.minimum(j, i), g)),
            pl.BlockSpec((tk, wb), lambda g, i, j: (jnp.minimum(j, i), g)),
            pl.BlockSpec((hpb, 1, tk), lambda g, i, j: (g, 0, jnp.minimum(j, i))),
        ],
        out_specs=[pl.BlockSpec((tq, wb), lambda g, i, j: (i, g)),
                   pl.BlockSpec((hpb, tq, 1), lambda g, i, j: (g, i, 0))],
        out_shape=[SDS((s, n_heads * head_dim), bf16), SDS((n_heads, s, 1), f32)],
        scratch_shapes=[pltpu.VMEM((hpb, tq, 1), f32), pltpu.VMEM((hpb, tq, 1), f32), pltpu.VMEM((tq, wb), f32)],
        compiler_params=_cp(("parallel", "parallel", "arbitrary")),
    )(q, k, v, crow)


def _fox_probs(q_ref, k_ref, v_ref, cr_ref, lse_ref, do_ref, hh, head_dim, scale, diagonal):
    sc, sl = _fox_logits(q_ref, k_ref, cr_ref, hh, head_dim, scale, diagonal)
    return jnp.exp2(sc - lse_ref[hh]), _dg(do_ref[:, sl], v_ref[:, sl], "nt"), sl


def _fox_bwd_dq(name, q, k, v, crow, lse, o, do, n_heads, head_dim):
    s = k.shape[0]
    tq = tk = min(ATT_TILE, s)
    nq, nk = s // tq, s // tk
    scale = head_dim ** -0.5
    hpb = ATT_HEADS
    wb = hpb * head_dim

    def body(q_ref, k_ref, v_ref, cr_ref, lse_ref, o_ref, do_ref, dq_ref, delta_ref, dcc_ref, acc_scr):
        i, j = pl.program_id(1), pl.program_id(2)
        heads = range(hpb)

        @pl.when(j == 0)
        def _():
            prod = o_ref[...].astype(f32) * do_ref[...].astype(f32)
            for hh in heads:
                delta_ref[hh] = jnp.sum(prod[:, hh * head_dim:(hh + 1) * head_dim], axis=-1, keepdims=True)
            dcc_ref[...] = jnp.zeros_like(dcc_ref)
            acc_scr[...] = jnp.zeros_like(acc_scr)

        def step(diagonal):
            pd = [_fox_probs(q_ref, k_ref, v_ref, cr_ref, lse_ref, do_ref, hh, head_dim, scale, diagonal) for hh in heads]
            ds = [pd[hh][0] * (pd[hh][1] - delta_ref[hh]) for hh in heads]
            dqs = [_dg(ds[hh], k_ref[:, pd[hh][2]], "nn") for hh in heads]
            for hh in heads:
                dcc_ref[hh] += jnp.sum(ds[hh], axis=-1, keepdims=True)
                acc_scr[:, pd[hh][2]] += dqs[hh]

        _on_causal_tiles(i, j, step)

        @pl.when(j == nk - 1)
        def _():
            dq_ref[...] = (acc_scr[...] * scale).astype(dq_ref.dtype)

    qspec = pl.BlockSpec((tq, wb), lambda g, i, j: (i, g))
    kspec = pl.BlockSpec((tk, wb), lambda g, i, j: (jnp.minimum(j, i), g))
    cspec = pl.BlockSpec((hpb, tq, 1), lambda g, i, j: (g, i, 0))
    return pl.pallas_call(
        body, name=name, grid=(n_heads // hpb, nq, nk),
        in_specs=[qspec, kspec, kspec,
                  pl.BlockSpec((hpb, 1, tk), lambda g, i, j: (g, 0, jnp.minimum(j, i))), cspec, qspec, qspec],
        out_specs=[qspec, cspec, cspec],
        out_shape=[SDS((s, n_heads * head_dim), bf16), SDS((n_heads, s, 1), f32), SDS((n_heads, s, 1), f32)],
        scratch_shapes=[pltpu.VMEM((tq, wb), f32)],
        compiler_params=_cp(("parallel", "parallel", "arbitrary")),
    )(q, k, v, crow, lse, o, do)


def _fox_bwd_dkv(name, q, k, v, crow, lse, delta, do, n_heads, head_dim):
    s = k.shape[0]
    tq = tk = min(ATT_TILE, s)
    nq, nk = s // tq, s // tk
    scale = head_dim ** -0.5
    hpb = ATT_HEADS
    wb = hpb * head_dim

    def body(q_ref, k_ref, v_ref, cr_ref, lse_ref, delta_ref, do_ref, dk_ref, dv_ref, dc_ref):
        j, i = pl.program_id(1), pl.program_id(2)

        @pl.when(i == 0)
        def _():
            dk_ref[...] = jnp.zeros_like(dk_ref)
            dv_ref[...] = jnp.zeros_like(dv_ref)
            dc_ref[...] = jnp.zeros_like(dc_ref)

        def step(diagonal):
            heads = range(hpb)
            pd = [_fox_probs(q_ref, k_ref, v_ref, cr_ref, lse_ref, do_ref, hh, head_dim, scale, diagonal) for hh in heads]
            ds = [pd[hh][0] * (pd[hh][1] - delta_ref[hh]) for hh in heads]
            dvs = [_dg(pd[hh][0], do_ref[:, pd[hh][2]], "tn") for hh in heads]
            dks = [_dg(ds[hh], q_ref[:, pd[hh][2]], "tn") for hh in heads]
            for hh in heads:
                sl = pd[hh][2]
                dv_ref[:, sl] += dvs[hh]
                dk_ref[:, sl] += dks[hh] * scale
                dc_ref[hh] -= jnp.sum(ds[hh], axis=0, keepdims=True)

        _on_causal_tiles(i, j, step)

    qspec = pl.BlockSpec((tq, wb), lambda g, j, i: (jnp.maximum(i, j), g))
    kspec = pl.BlockSpec((tk, wb), lambda g, j, i: (j, g))
    cspec = pl.BlockSpec((hpb, tq, 1), lambda g, j, i: (g, jnp.maximum(i, j), 0))
    rspec = pl.BlockSpec((hpb, 1, tk), lambda g, j, i: (g, 0, j))
    return pl.pallas_call(
        body, name=name, grid=(n_heads // hpb, nk, nq),
        in_specs=[qspec, kspec, kspec, rspec, cspec, cspec, qspec],
        out_specs=[kspec, kspec, rspec],
        out_shape=[SDS((s, n_heads * head_dim), f32), SDS((s, n_heads * head_dim), f32), SDS((n_heads, 1, s), f32)],
        compiler_params=_cp(("parallel", "parallel", "arbitrary")),
    )(q, k, v, crow, lse, delta, do)


def _final_loss(name, h, target, gain):
    s, d = h.shape
    tile = min(ROW_TILE, s)

    def body(h_ref, t_ref, g_ref, loss_ref, dh_ref, dg_ref):
        @pl.when(pl.program_id(0) == 0)
        def _():
            loss_ref[...] = jnp.zeros_like(loss_ref)
            dg_ref[...] = jnp.zeros_like(dg_ref)

        x, g = h_ref[...], g_ref[...]
        rstd = lax.rsqrt(jnp.mean(x * x, axis=-1, keepdims=True) + EPS)
        xhat = x * rstd
        err = xhat * g - t_ref[...]
        row = jnp.sum(err * err, axis=-1, keepdims=True) * (0.5 / d)
        loss_ref[...] += jnp.sum(row, axis=0, keepdims=True)
        dy = err * (1.0 / d)
        dg_ref[...] += jnp.sum(dy * xhat, axis=0, keepdims=True)
        dxhat = dy * g
        dh_ref[...] = rstd * (dxhat - xhat * jnp.mean(dxhat * xhat, axis=-1, keepdims=True))

    return pl.pallas_call(
        body, name=name, grid=(s // tile,),
        in_specs=[pl.BlockSpec((tile, d), lambda i: (i, 0)), pl.BlockSpec((tile, d), lambda i: (i, 0)),
                  pl.BlockSpec((1, d), lambda i: (0, 0))],
        out_specs=[pl.BlockSpec((1, LANES), lambda i: (0, 0)), pl.BlockSpec((tile, d), lambda i: (i, 0)),
                   pl.BlockSpec((1, d), lambda i: (0, 0))],
        out_shape=[SDS((1, LANES), f32), SDS((s, d), f32), SDS((1, d), f32)],
        compiler_params=_cp(("arbitrary",)),
    )(h, target, gain)


def _adamw(name, parts, w, m, v):
    r, c = w.shape
    rb = r
    for cand in (256, 128, 64, 32, 16):
        if r % cand == 0 and cand * c * 4 <= (2 << 20):
            rb = cand
            break
    n_parts = parts.shape[0]

    def body(p_ref, w_ref, m_ref, v_ref, g_out, d_out, m_out, v_out):
        g = p_ref[0].astype(f32)
        for k in range(1, n_parts):
            g = g + p_ref[k].astype(f32)
        m_new = ADAM_B1 * m_ref[...] + (1.0 - ADAM_B1) * g
        v_new = ADAM_B2 * v_ref[...] + (1.0 - ADAM_B2) * (g * g)
        m_hat = m_new / (1.0 - ADAM_B1 ** ADAM_STEP)
        v_hat = v_new / (1.0 - ADAM_B2 ** ADAM_STEP)
        g_out[...] = g
        d_out[...] = -ADAM_LR * (m_hat / (jnp.sqrt(v_hat) + ADAM_EPS) + ADAM_WD * w_ref[...])
        m_out[...] = m_new
        v_out[...] = v_new

    blk = pl.BlockSpec((rb, c), lambda i: (i, 0))
    return pl.pallas_call(
        body, name=name, grid=(r // rb,),
        in_specs=[pl.BlockSpec((n_parts, rb, c), lambda i: (0, i, 0)), blk, blk, blk],
        out_specs=[blk, blk, blk, blk],
        out_shape=[SDS((r, c), f32)] * 4,
        compiler_params=_cp(("parallel",)),
    )(parts, w, m, v)


def _position():
    x, y, c = lax.axis_index("x"), lax.axis_index("y"), lax.axis_index("c")
    return x, y, c


def _all_gather(name, shards):
    n = len(shards)

    def body(*refs):
        ins, outs = refs[:n], refs[n:2 * n]
        send_sems, recv_sems, local_sems = refs[2 * n:]
        x, y, c = _position()
        me, sibling = (x, y, c), (x, y, 1 - c)
        chips = [(1 - x, y), (x, 1 - y), (1 - x, 1 - y)]

        def slot(a, block):
            px, py, pc = block
            return outs[a].at[4 * px + 2 * py + pc]

        def copy(a, k, block, to, src=None):
            return pltpu.make_async_remote_copy(
                src_ref=slot(a, block) if src is None else src, dst_ref=slot(a, block),
                send_sem=send_sems.at[a, k], recv_sem=recv_sems.at[a, k], device_id=to, device_id_type=MESH)

        local = [pltpu.make_async_copy(ins[a], slot(a, me), local_sems.at[a]) for a in range(n)]
        for cp in local:
            cp.start()
        started = []
        for a in range(n):
            first = [copy(a, 0, me, sibling, src=ins[a])]
            first += [copy(a, 1 + j, me, (*chip, c), src=ins[a]) for j, chip in enumerate(chips)]
            for cp in first:
                cp.start()
            started += first
        for a in range(n):
            for j, chip in enumerate(chips):
                copy(a, 1 + j, (*chip, c), me).wait_recv()
                passed = copy(a, 4 + j, (*chip, c), sibling)
                passed.start()
                started.append(passed)
        for a in range(n):
            copy(a, 0, sibling, me).wait_recv()
            for j, chip in enumerate(chips):
                copy(a, 4 + j, (*chip, 1 - c), me).wait_recv()
        for cp in started:
            cp.wait_send()
        for cp in local:
            cp.wait()

    any_spec = pl.BlockSpec(memory_space=pl.ANY)
    outs = pl.pallas_call(
        body, name=name,
        in_specs=[any_spec] * n, out_specs=[any_spec] * n,
        out_shape=[SDS((N_DEV, *a.shape), a.dtype) for a in shards],
        scratch_shapes=[pltpu.SemaphoreType.DMA((n, 7)), pltpu.SemaphoreType.DMA((n, 7)), pltpu.SemaphoreType.DMA((n,))],
    )(*shards)
    return list(outs)


N_CHIPS = N_DEV // 2


def _sibling_exchange(name, arrs):
    n = len(arrs)

    def body(*refs):
        ins, outs = refs[:n], refs[n:2 * n]
        send_sems, recv_sems = refs[2 * n:]
        x, y, c = _position()
        copies = [pltpu.make_async_remote_copy(
            src_ref=ins[a], dst_ref=outs[a], send_sem=send_sems.at[a], recv_sem=recv_sems.at[a],
            device_id=(x, y, 1 - c), device_id_type=MESH) for a in range(n)]
        for cp in copies:
            cp.start()
        for cp in copies:
            cp.wait_recv()
        for cp in copies:
            cp.wait_send()

    any_spec = pl.BlockSpec(memory_space=pl.ANY)
    outs = pl.pallas_call(
        body, name=name,
        in_specs=[any_spec] * n, out_specs=[any_spec] * n, out_shape=[SDS(a.shape, a.dtype) for a in arrs],
        scratch_shapes=[pltpu.SemaphoreType.DMA((n,)), pltpu.SemaphoreType.DMA((n,))],
    )(*arrs)
    return list(outs)


def _chip_exchange(name, parts):
    n = len(parts)

    def body(*refs):
        ins, outs = refs[:n], refs[n:2 * n]
        send_sems, recv_sems, local_sems = refs[2 * n:]
        x, y, c = _position()
        my_chip = 2 * x + y
        local = [pltpu.make_async_copy(ins[a].at[my_chip], outs[a].at[my_chip], local_sems.at[a]) for a in range(n)]
        for cp in local:
            cp.start()
        sends, recvs = [], []
        for a in range(n):
            for k in range(1, N_CHIPS):
                px = (1 - x) if (k >> 1) & 1 else x
                py = (1 - y) if k & 1 else y
                peer_chip = 2 * px + py
                send = pltpu.make_async_remote_copy(
                    src_ref=ins[a].at[peer_chip], dst_ref=outs[a].at[my_chip],
                    send_sem=send_sems.at[a, k - 1], recv_sem=recv_sems.at[a, k - 1],
                    device_id=(px, py, c), device_id_type=MESH)
                send.start()
                sends.append(send)
                recvs.append(pltpu.make_async_remote_copy(
                    src_ref=ins[a].at[peer_chip], dst_ref=outs[a].at[peer_chip],
                    send_sem=send_sems.at[a, k - 1], recv_sem=recv_sems.at[a, k - 1],
                    device_id=(px, py, c), device_id_type=MESH))
        for cp in recvs:
            cp.wait_recv()
        for cp in sends:
            cp.wait_send()
        for cp in local:
            cp.wait()

    any_spec = pl.BlockSpec(memory_space=pl.ANY)
    outs = pl.pallas_call(
        body, name=name,
        in_specs=[any_spec] * n, out_specs=[any_spec] * n,
        out_shape=[SDS(a.shape, a.dtype) for a in parts],
        scratch_shapes=[pltpu.SemaphoreType.DMA((n, N_CHIPS - 1)), pltpu.SemaphoreType.DMA((n, N_CHIPS - 1)),
                        pltpu.SemaphoreType.DMA((n,))],
    )(*parts)
    return list(outs)


def _reduce_exchange(stacks):
    core = lax.axis_index("c")
    by_core = [st.reshape(N_CHIPS, 2, *st.shape[1:]) for st in stacks]
    mine = [lax.dynamic_index_in_dim(v, core, axis=1, keepdims=False) for v in by_core]
    theirs = _sibling_exchange(
        "exchange_sibling", [lax.dynamic_index_in_dim(v, 1 - core, axis=1, keepdims=False) for v in by_core])
    summed = []
    for a, (m, t) in enumerate(zip(mine, theirs)):
        cols = m.shape[-1]
        (both,) = _rowwise(f"exchange_add_{a}", lambda p, q: (p + q,), [m.reshape(-1, cols), t.reshape(-1, cols)], [],
                           [(cols, m.dtype)])
        summed.append(both.reshape(m.shape))
    return _chip_exchange("exchange_chips", summed)


def _rows_from_shards(g):
    n, l, r, c = g.shape
    return g.transpose(1, 0, 2, 3).reshape(l, n * r, c)


def _rows_to_shards(w):
    l, rows, c = w.shape
    return w.reshape(l, N_DEV, rows // N_DEV, c).transpose(1, 0, 2, 3)


def _pad_lanes(a, width):
    return jnp.pad(a, [(0, 0)] * (a.ndim - 1) + [(0, width - a.shape[-1])])


def _row(vec, width=None):
    vec = vec.reshape(1, -1)
    return vec if width is None else _pad_lanes(vec, width)


class _SmallPack:
    def __init__(self, shapes):
        self.shapes, self.offsets, at = shapes, {}, 0
        for name, shape in shapes.items():
            last = shape[-1]
            lead = int(math.prod(shape[:-1]))
            rows = lead * (last // LANES) if last >= LANES else lead
            self.offsets[name] = (at, rows)
            at += rows
        self.rows = -(-at // 8) * 8

    def pack(self, values):
        pieces = []
        for name, shape in self.shapes.items():
            val = values[name].astype(f32)
            if shape[-1] >= LANES:
                pieces.append(val.reshape(-1, LANES))
            else:
                pieces.append(_pad_lanes(val.reshape(-1, shape[-1]), LANES))
        used = sum(p.shape[0] for p in pieces)
        if used < self.rows:
            pieces.append(jnp.zeros((self.rows - used, LANES), f32))
        return jnp.concatenate(pieces, axis=0)

    def unpack(self, packed):
        out = {}
        for name, shape in self.shapes.items():
            at, rows = self.offsets[name]
            blk = packed[at:at + rows]
            out[name] = blk.reshape(shape) if shape[-1] >= LANES else blk[:, :shape[-1]].reshape(shape)
        return out


def kernel(x, mem, ffn1_norm, ffn1_w_gate_up, ffn1_w_down, mix_norm, ffn2_norm, ffn2_w_gate_up, ffn2_w_down, gdn_w_in, gdn_conv, gdn_A_log, gdn_dt_bias, gdn_out_norm, fox_w_in, w_out, mem_norm, mem_w_kv, kv_norm, kv_w, kv_b_f, final_norm, loss_target, m_ffn1_norm, m_ffn1_w_gate_up, m_ffn1_w_down, m_mix_norm, m_ffn2_norm, m_ffn2_w_gate_up, m_ffn2_w_down, m_gdn_w_in, m_gdn_conv, m_gdn_A_log, m_gdn_dt_bias, m_gdn_out_norm, m_fox_w_in, m_w_out, m_mem_norm, m_mem_w_kv, m_kv_norm, m_kv_w, m_kv_b_f, m_final_norm, v_ffn1_norm, v_ffn1_w_gate_up, v_ffn1_w_down, v_mix_norm, v_ffn2_norm, v_ffn2_w_gate_up, v_ffn2_w_down, v_gdn_w_in, v_gdn_conv, v_gdn_A_log, v_gdn_dt_bias, v_gdn_out_norm, v_fox_w_in, v_w_out, v_mem_norm, v_mem_w_kv, v_kv_norm, v_kv_w, v_kv_b_f, v_final_norm):
    weights = dict(ffn1_norm=ffn1_norm, ffn1_w_gate_up=ffn1_w_gate_up, ffn1_w_down=ffn1_w_down, mix_norm=mix_norm,
                   ffn2_norm=ffn2_norm, ffn2_w_gate_up=ffn2_w_gate_up, ffn2_w_down=ffn2_w_down, gdn_w_in=gdn_w_in,
                   gdn_conv=gdn_conv, gdn_A_log=gdn_A_log, gdn_dt_bias=gdn_dt_bias, gdn_out_norm=gdn_out_norm,
                   fox_w_in=fox_w_in, w_out=w_out, mem_norm=mem_norm, mem_w_kv=mem_w_kv, kv_norm=kv_norm, kv_w=kv_w,
                   kv_b_f=kv_b_f, final_norm=final_norm)
    mom_m = dict(ffn1_norm=m_ffn1_norm, ffn1_w_gate_up=m_ffn1_w_gate_up, ffn1_w_down=m_ffn1_w_down, mix_norm=m_mix_norm,
                 ffn2_norm=m_ffn2_norm, ffn2_w_gate_up=m_ffn2_w_gate_up, ffn2_w_down=m_ffn2_w_down, gdn_w_in=m_gdn_w_in,
                 gdn_conv=m_gdn_conv, gdn_A_log=m_gdn_A_log, gdn_dt_bias=m_gdn_dt_bias, gdn_out_norm=m_gdn_out_norm,
                 fox_w_in=m_fox_w_in, w_out=m_w_out, mem_norm=m_mem_norm, mem_w_kv=m_mem_w_kv, kv_norm=m_kv_norm,
                 kv_w=m_kv_w, kv_b_f=m_kv_b_f, final_norm=m_final_norm)
    mom_v = dict(ffn1_norm=v_ffn1_norm, ffn1_w_gate_up=v_ffn1_w_gate_up, ffn1_w_down=v_ffn1_w_down, mix_norm=v_mix_norm,
                 ffn2_norm=v_ffn2_norm, ffn2_w_gate_up=v_ffn2_w_gate_up, ffn2_w_down=v_ffn2_w_down, gdn_w_in=v_gdn_w_in,
                 gdn_conv=v_gdn_conv, gdn_A_log=v_gdn_A_log, gdn_dt_bias=v_gdn_dt_bias, gdn_out_norm=v_gdn_out_norm,
                 fox_w_in=v_fox_w_in, w_out=v_w_out, mem_norm=v_mem_norm, mem_w_kv=v_mem_w_kv, kv_norm=v_kv_norm,
                 kv_w=v_kv_w, kv_b_f=v_kv_b_f, final_norm=v_final_norm)
    names = list(weights)
    small_names = [n for n in names if weights[n].shape == mom_m[n].shape and n in (
        "ffn1_norm", "mix_norm", "ffn2_norm", "gdn_A_log", "gdn_dt_bias", "gdn_out_norm", "mem_norm", "kv_norm",
        "kv_b_f", "final_norm")]
    big_names = [n for n in names if n not in small_names]

    h = x[0]
    target = loss_target[0]
    mem_tokens = mem[0]
    s, d = h.shape
    depth = ffn1_norm.shape[0]
    n_a = gdn_w_in.shape[0]
    n_heads, head_dim = gdn_A_log.shape[1], gdn_out_norm.shape[1]
    gw = n_heads * head_dim
    a_in = gdn_w_in.shape[2]
    mem_w = a_in - 4 * gw - 2 * n_heads
    a_in_pad = 4 * gw + mem_w + LANES
    kv_width = kv_w.shape[1]
    kv_pad = 2 * gw + LANES
    fh = ffn1_w_down.shape[1] * N_DEV

    def permute_in(w):
        ab = w[..., 4 * gw:4 * gw + 2 * n_heads]
        return jnp.concatenate([w[..., :4 * gw], w[..., 4 * gw + 2 * n_heads:], _pad_lanes(ab, LANES)], axis=-1)

    def unpermute_in(w):
        return jnp.concatenate([w[..., :4 * gw], w[..., 4 * gw + mem_w:4 * gw + mem_w + 2 * n_heads],
                                w[..., 4 * gw:4 * gw + mem_w]], axis=-1)

    gathered = _all_gather("gather_weights", [
        ffn1_w_gate_up.astype(bf16), ffn1_w_down.astype(bf16), ffn2_w_gate_up.astype(bf16), ffn2_w_down.astype(bf16),
        permute_in(gdn_w_in).astype(bf16), fox_w_in.astype(bf16), w_out.astype(bf16), mem_w_kv.astype(bf16),
        _pad_lanes(kv_w, kv_pad).astype(bf16)[None], gdn_conv])
    wgu1, wd1_s, wgu2, wd2_s, win_s, wfox_s, wout_s, wmem_s, wkv_s, conv_s = gathered
    wd1, wd2 = _rows_from_shards(wd1_s), _rows_from_shards(wd2_s)
    win, wfox, wout = _rows_from_shards(win_s), _rows_from_shards(wfox_s), _rows_from_shards(wout_s)
    wmem = _rows_from_shards(wmem_s)
    wmem_cat = wmem.transpose(1, 0, 2).reshape(d, depth * 2 * mem_w)
    wkv = _rows_from_shards(wkv_s)[0]
    conv_w = conv_s.transpose(1, 2, 0, 3).reshape(n_a, gdn_conv.shape[1], 3 * gw)

    a_log_rows = [_row(gdn_A_log[l], LANES) for l in range(n_a)]
    dt_rows = [_row(gdn_dt_bias[l], LANES) for l in range(n_a)]
    onorm_rows = [_row(gdn_out_norm[l]) for l in range(n_a)]
    b_f_row = _row(kv_b_f, LANES)

    (mem_n,) = _rowwise("mem_norm", lambda t, g: (_rms(t, g),), [mem_tokens], [_row(mem_norm)], [(d, bf16)])
    mem_kv = _matmul("mem_kv", mem_n, wmem_cat, "nn", f32)

    saved = []
    shared = None
    for l in range(depth):
        rec = {"h0": h}
        h1, gate1, up1 = _ffn_fwd(f"ffn1_fwd_{l}", h, _row(ffn1_norm[l]), wgu1, wd1, l)
        (u,) = _rowwise(f"mix_norm_{l}", lambda t, g: (_rms(t, g),), [h1], [_row(mix_norm[l])], [(d, bf16)])
        rec.update(h1=h1, u=u, ffn1=(gate1, up1))
        if l < n_a:
            proj = _matmul(f"gdn_in_{l}", u, win[l], "nn", f32)
            yc = _conv_fwd(f"conv_fwd_{l}", V(proj, cb=0, w=3 * gw), conv_w[l])
            ab_view = V(proj, cb=(4 * gw + mem_w) // LANES, w=LANES)
            q, k, v, gb = _rowwise(f"gdn_pre_{l}", functools.partial(_gdn_pre, n_heads, head_dim),
                                   [yc, ab_view], [a_log_rows[l], dt_rows[l]],
                                   [(gw, f32), (gw, f32), (gw, f32), (LANES, f32)])
            o, states = _gdn_chunk_fwd(f"gdn_chunk_fwd_{l}", q, k, v, gb, n_heads, head_dim)
            z_view = V(proj, cb=3, w=gw)
            (main,) = _rowwise(f"gdn_post_{l}", functools.partial(_gdn_post, n_heads, head_dim),
                               [o, z_view], [onorm_rows[l]], [(gw, bf16)])
            qmem_view = V(proj, cb=4 * gw // mem_w, w=mem_w)
            rec.update(proj=proj, yc=yc, q=q, k=k, v=v, gb=gb, o=o, states=states)
        else:
            proj = _matmul(f"fox_in_{l}", u, wfox[l - n_a], "nn", bf16)
            sk, sv, crow = shared["k"], shared["v"], shared["crow"]
            main, lse = _fox_fwd(f"fox_fwd_{l}", proj, sk, sv, crow, n_heads, head_dim)
            qmem_view = V(proj, cb=gw // mem_w, w=mem_w)
            rec.update(proj=proj, lse=lse)
        km = V(mem_kv, cb=2 * l, w=mem_w)
        vm = V(mem_kv, cb=2 * l + 1, w=mem_w)
        (mem_out,) = _rowwise(f"mem_attn_{l}", _mem_attn, [qmem_view], [km, vm], [(mem_w, bf16)])
        cat = jnp.concatenate([main, mem_out], axis=1)
        h2 = _matmul(f"out_proj_{l}", cat, wout[l], "nn", f32, add=h1)
        h3, gate2, up2 = _ffn_fwd(f"ffn2_fwd_{l}", h2, _row(ffn2_norm[l]), wgu2, wd2, l)
        rec.update(cat=cat, h2=h2, qmem=qmem_view, ffn2=(gate2, up2))
        saved.append(rec)
        h = h3
        if l == n_a - 1:
            (hn,) = _rowwise("kv_norm", lambda t, g: (_rms(t, g),), [h], [_row(kv_norm)], [(d, bf16)])
            p = _matmul("kv_proj", hn, wkv, "nn", f32)
            pk, pv, pf = V(p, cb=0, w=gw), V(p, cb=1, w=gw), V(p, cb=2 * gw // LANES, w=LANES)
            sk, sv, log_f = _rowwise("kv_post", functools.partial(_kv_post, n_heads), [pk, pv, pf], [b_f_row],
                                     [(gw, bf16), (gw, bf16), (LANES, f32)])
            cum = _cumsum("forget_cumsum", [log_f], reverse=False)
            c_heads = cum[:, :n_heads].T
            shared = dict(k=sk, v=sv, crow=c_heads.reshape(n_heads, 1, s), h=h, hn=hn, p=p, views=(pk, pv, pf))

    loss_part, dh, d_final = _final_loss("final_loss", h, target, _row(final_norm))
    loss = lax.psum(loss_part[0, 0], ("x", "y", "c"))

    grads = {}
    per_layer = {n: [None] * depth for n in ("ffn1_norm", "mix_norm", "ffn2_norm", "ffn1_gu", "ffn1_d", "ffn2_gu",
                                             "ffn2_d", "w_out")}
    per_a = {n: [None] * n_a for n in ("gdn_w_in", "gdn_conv", "gdn_A_log", "gdn_dt_bias", "gdn_out_norm")}
    per_b = {"fox_w_in": [None] * (depth - n_a)}
    d_mem_kv = [None] * depth
    fox_grads = []

    def ffn_backward(tag, l, h_in, d_out, gain, kept, wgu, wd):
        parts, dwg, dwu, dwd = _ffn_bwd(f"{tag}_bwd_{l}", h_in, d_out, _row(gain), kept[0], kept[1], wgu, wd, l)
        nh = parts.shape[0]
        (d_in,), (d_gain,) = _rowwise_vjp(
            f"{tag}_norm_bwd_{l}", lambda t, g: (_rms(t, g),), [h_in], [_row(gain)],
            [[V(parts, lead=t) for t in range(nh)]], [f32], add=d_out)
        return d_in, d_gain, jnp.concatenate([dwg, dwu], axis=0), dwd

    for l in reversed(range(depth)):
        rec = saved[l]
        if l == n_a - 1:
            dk_list = [V(g["dk"]) for g in fox_grads]
            dv_list = [V(g["dv"]) for g in fox_grads]
            dc_parts = [_pad_lanes(part.reshape(n_heads, s).T, LANES) for g in fox_grads for part in g["dc"]]
            d_log_f = _cumsum("forget_cumsum_bwd", dc_parts, reverse=True)
            pk, pv, pf = shared["views"]
            (dpk, dpv, dpf), (d_bf,) = _rowwise_vjp(
                "kv_post_bwd", functools.partial(_kv_post, n_heads), [pk, pv, pf], [b_f_row],
                [dk_list, dv_list, [d_log_f]], [bf16, bf16, bf16])
            dp = jnp.concatenate([dpk, dpv, dpf], axis=1)
            d_hn = _matmul("kv_proj_dx", dp, wkv, "nt", f32)
            grads["kv_w"] = _matmul("kv_proj_dw", shared["hn"], dp, "tn", f32)[:, :kv_width]
            (dh,), (d_kvn,) = _rowwise_vjp("kv_norm_bwd", lambda t, g: (_rms(t, g),), [shared["h"]], [_row(kv_norm)],
                                           [[d_hn]], [f32], add=dh)
            grads["kv_norm"] = d_kvn.reshape(-1)
            grads["kv_b_f"] = d_bf[0, :n_heads]

        dh2, per_layer["ffn2_norm"][l], per_layer["ffn2_gu"][l], per_layer["ffn2_d"][l] = ffn_backward(
            "ffn2", l, rec["h2"], dh, ffn2_norm[l], rec["ffn2"], wgu2, wd2)
        d_cat = _matmul(f"out_proj_dx_{l}", dh2, wout[l], "nt", f32)
        per_layer["w_out"][l] = _matmul(f"out_proj_dw_{l}", rec["cat"], dh2, "tn", f32)
        d_main = V(d_cat, cb=0, w=gw)
        d_memo = V(d_cat, cb=gw // mem_w, w=mem_w)
        km, vm = V(mem_kv, cb=2 * l, w=mem_w), V(mem_kv, cb=2 * l + 1, w=mem_w)
        (dqmem,), (dkm, dvm) = _rowwise_vjp(f"mem_attn_bwd_{l}", _mem_attn, [rec["qmem"]], [km, vm], [[d_memo]], [bf16])
        d_mem_kv[l] = jnp.concatenate([dkm, dvm], axis=1)
        if l < n_a:
            proj = rec["proj"]
            z_view = V(proj, cb=3, w=gw)
            (d_o, d_z), (d_onorm,) = _rowwise_vjp(
                f"gdn_post_bwd_{l}", functools.partial(_gdn_post, n_heads, head_dim), [rec["o"], z_view],
                [onorm_rows[l]], [[d_main]], [f32, bf16])
            dq, dk, dv, dgb = _gdn_chunk_bwd(f"gdn_chunk_bwd_{l}", rec["q"], rec["k"], rec["v"], rec["gb"],
                                             rec["states"], d_o, n_heads, head_dim)
            ab_view = V(proj, cb=(4 * gw + mem_w) // LANES, w=LANES)
            (d_yc, d_ab), (d_alog, d_dt) = _rowwise_vjp(
                f"gdn_pre_bwd_{l}", functools.partial(_gdn_pre, n_heads, head_dim), [rec["yc"], ab_view],
                [a_log_rows[l], dt_rows[l]], [[dq], [dk], [dv], [dgb]], [f32, bf16])
            d_qkv, d_conv = _conv_bwd(f"conv_bwd_{l}", V(proj, cb=0, w=3 * gw), d_yc, conv_w[l])
            d_proj = jnp.concatenate([d_qkv, d_z, dqmem, d_ab], axis=1)
            du = _matmul(f"gdn_in_dx_{l}", d_proj, win[l], "nt", f32)
            per_a["gdn_w_in"][l] = unpermute_in(_matmul(f"gdn_in_dw_{l}", rec["u"], d_proj, "tn", f32))
            per_a["gdn_conv"][l] = d_conv
            per_a["gdn_A_log"][l] = d_alog[0, :n_heads]
            per_a["gdn_dt_bias"][l] = d_dt[0, :n_heads]
            per_a["gdn_out_norm"][l] = d_onorm[0]
        else:
            proj = rec["proj"]
            sk, sv, crow = shared["k"], shared["v"], shared["crow"]
            dq, delta, dc_col = _fox_bwd_dq(f"fox_dq_{l}", proj, sk, sv, crow, rec["lse"], rec["cat"], d_cat,
                                            n_heads, head_dim)
            dk, dv, dc_row = _fox_bwd_dkv(f"fox_dkv_{l}", proj, sk, sv, crow, rec["lse"], delta, d_cat,
                                          n_heads, head_dim)
            fox_grads.append(dict(dk=dk, dv=dv, dc=(dc_row, dc_col)))
            d_proj = jnp.concatenate([dq, dqmem], axis=1)
            du = _matmul(f"fox_in_dx_{l}", d_proj, wfox[l - n_a], "nt", f32)
            per_b["fox_w_in"][l - n_a] = _matmul(f"fox_in_dw_{l}", rec["u"], d_proj, "tn", f32)
        (dh1,), (d_mix,) = _rowwise_vjp(f"mix_norm_bwd_{l}", lambda t, g: (_rms(t, g),), [rec["h1"]],
                                        [_row(mix_norm[l])], [[du]], [f32], add=dh2)
        per_layer["mix_norm"][l] = d_mix
        dh, per_layer["ffn1_norm"][l], per_layer["ffn1_gu"][l], per_layer["ffn1_d"][l] = ffn_backward(
            "ffn1", l, rec["h0"], dh1, ffn1_norm[l], rec["ffn1"], wgu1, wd1)

    grad_x = dh[None]

    d_mem_kv_cat = jnp.concatenate(d_mem_kv, axis=1)
    d_wmem_cat = _matmul("mem_kv_dw", mem_n, d_mem_kv_cat, "tn", f32)
    d_mem_n = _matmul("mem_kv_dx", d_mem_kv_cat, wmem_cat, "nt", f32)
    _, (d_memnorm,) = _rowwise_vjp("mem_norm_bwd", lambda t, g: (_rms(t, g),), [mem_tokens], [_row(mem_norm)],
                                   [[d_mem_n]], [None])

    def gu_stack(per):
        return jnp.stack(per, axis=1).astype(bf16)

    stacks = dict(
        ffn1_w_gate_up=gu_stack(per_layer["ffn1_gu"]),
        ffn1_w_down=_rows_to_shards(jnp.stack(per_layer["ffn1_d"])).astype(bf16),
        ffn2_w_gate_up=gu_stack(per_layer["ffn2_gu"]),
        ffn2_w_down=_rows_to_shards(jnp.stack(per_layer["ffn2_d"])).astype(bf16),
        gdn_w_in=_rows_to_shards(jnp.stack(per_a["gdn_w_in"])).astype(bf16),
        gdn_conv=jnp.stack(per_a["gdn_conv"]).reshape(n_a, -1, N_DEV, 3 * gw // N_DEV).transpose(2, 0, 1, 3),
        fox_w_in=_rows_to_shards(jnp.stack(per_b["fox_w_in"])).astype(bf16),
        w_out=_rows_to_shards(jnp.stack(per_layer["w_out"])).astype(bf16),
        mem_w_kv=_rows_to_shards(d_wmem_cat.reshape(d, depth, 2 * mem_w).transpose(1, 0, 2)).astype(bf16),
        kv_w=_rows_to_shards(grads["kv_w"][None])[:, 0].astype(bf16),
    )
    received = dict(zip(big_names, _reduce_exchange([stacks[n] for n in big_names])))

    small_shapes = {n: weights[n].shape for n in small_names}
    pack = _SmallPack(small_shapes)
    small_grads = dict(
        ffn1_norm=jnp.concatenate(per_layer["ffn1_norm"], axis=0), mix_norm=jnp.concatenate(per_layer["mix_norm"], axis=0),
        ffn2_norm=jnp.concatenate(per_layer["ffn2_norm"], axis=0), gdn_A_log=jnp.stack(per_a["gdn_A_log"]),
        gdn_dt_bias=jnp.stack(per_a["gdn_dt_bias"]), gdn_out_norm=jnp.stack(per_a["gdn_out_norm"]),
        mem_norm=d_memnorm.reshape(-1), kv_norm=grads["kv_norm"], kv_b_f=grads["kv_b_f"], final_norm=d_final.reshape(-1))
    (small_parts,) = _all_gather("gather_small_grads", [pack.pack(small_grads)])

    out_g, out_d, out_m, out_v = {}, {}, {}, {}
    for n in big_names:
        shape = weights[n].shape
        c = shape[-1]
        parts = received[n].reshape(N_CHIPS, -1, c)
        res = _adamw(f"adamw_{n}", parts, weights[n].reshape(-1, c), mom_m[n].reshape(-1, c), mom_v[n].reshape(-1, c))
        out_g[n], out_d[n], out_m[n], out_v[n] = [r.reshape(shape) for r in res]
    res = _adamw("adamw_small", small_parts, pack.pack({n: weights[n] for n in small_names}),
                 pack.pack({n: mom_m[n] for n in small_names}), pack.pack({n: mom_v[n] for n in small_names}))
    for dst, packed in zip((out_g, out_d, out_m, out_v), res):
        dst.update(pack.unpack(packed))

    return (loss, grad_x, *[out_g[n] for n in names], *[out_d[n] for n in names],
            *[out_m[n] for n in names], *[out_v[n] for n in names])
```

```python
import functools
import math

import jax
import jax.numpy as jnp
from jax import lax
from jax.experimental import pallas as pl
from jax.experimental.pallas import tpu as pltpu

f32 = jnp.float32
bf16 = jnp.bfloat16
SDS = jax.ShapeDtypeStruct

N_DEV = 8
MEM_HEADS = 4
CHUNK = 64
LANES = 128
EPS = 1e-6
NEG_INF = -1e30
ADAM_LR = 0.001
ADAM_B1 = 0.9
ADAM_B2 = 0.999
ADAM_EPS = 1e-08
ADAM_WD = 0.01
ADAM_STEP = 10

ROW_TILE = 512
MM_TILE = 512
FFN_FWD_TILE = 1024
FFN_BWD_TILE = 512
ATT_TILE = 1024
ATT_HEADS = 3
CUMSUM_TILE = 256
VMEM_LIMIT = 56 * 1024 * 1024

MESH = pl.DeviceIdType.MESH


def _cp(sem=None):
    return pltpu.CompilerParams(dimension_semantics=sem, vmem_limit_bytes=VMEM_LIMIT)


_DIMS = {"nn": (((1,), (0,)), ((), ())), "nt": (((1,), (1,)), ((), ())), "tn": (((0,), (0,)), ((), ()))}


def _dg(a, b, mode):
    return lax.dot_general(a.astype(bf16), b.astype(bf16), _DIMS[mode], preferred_element_type=f32)


@functools.partial(jax.custom_vjp, nondiff_argnums=(2,))
def _mm(a, b, mode):
    return _dg(a, b, mode)


def _mm_fwd(a, b, mode):
    return _dg(a, b, mode), (a, b)


def _mm_bwd(mode, res, ct):
    a, b = res
    if mode == "nn":
        da, db = _dg(ct, b, "nt"), _dg(a, ct, "tn")
    elif mode == "nt":
        da, db = _dg(ct, b, "nn"), _dg(ct, a, "tn")
    else:
        da, db = _dg(b, ct, "nt"), _dg(a, ct, "nn")
    return da.astype(a.dtype), db.astype(b.dtype)


_mm.defvjp(_mm_fwd, _mm_bwd)


def _split_bf16(x):
    hi = x.astype(bf16)
    return hi, (x - hi.astype(f32)).astype(bf16)


def _dgh(a, b, mode="nn"):
    a_hi, a_lo = _split_bf16(a)
    b_hi, b_lo = _split_bf16(b)
    dims = _DIMS[mode]
    return (lax.dot_general(a_hi, b_hi, dims, preferred_element_type=f32)
            + lax.dot_general(a_hi, b_lo, dims, preferred_element_type=f32)
            + lax.dot_general(a_lo, b_hi, dims, preferred_element_type=f32))


@jax.custom_vjp
def _unit_lower_inverses(lows):
    c = lows[0].shape[0]
    ri = lax.broadcasted_iota(jnp.int32, (c, c), 0)
    ci = lax.broadcasted_iota(jnp.int32, (c, c), 1)
    eye = jnp.where(ri == ci, 1.0, 0.0)
    xs = [-low for low in lows]
    rs = [eye + x for x in xs]
    for _ in range(int(math.log2(c)) - 1):
        xs = [_dgh(x, x) for x in xs]
        rs = [r + _dgh(r, x) for r, x in zip(rs, xs)]
    return tuple(rs)


def _uli_fwd(lows):
    ts = _unit_lower_inverses(lows)
    return ts, ts


def _uli_bwd(ts, cts):
    mids = [_dgh(ct, t, "nt") for t, ct in zip(ts, cts)]
    return (tuple(-_dgh(t, m, "tn") for t, m in zip(ts, mids)),)


_unit_lower_inverses.defvjp(_uli_fwd, _uli_bwd)


@functools.partial(jax.custom_vjp, nondiff_argnums=(1,))
def _split_lanes(x, width):
    return tuple(x[:, i * width:(i + 1) * width] for i in range(x.shape[1] // width))


def _split_fwd(x, width):
    return _split_lanes(x, width), None


def _split_bwd(width, _, cts):
    return (jnp.concatenate(list(cts), axis=1),)


_split_lanes.defvjp(_split_fwd, _split_bwd)


def _sigmoid(x):
    return 1.0 / (1.0 + jnp.exp(-x))


def _silu(x):
    return x * _sigmoid(x)


def _softplus(x):
    return jnp.maximum(x, 0.0) + jnp.log1p(jnp.exp(-jnp.abs(x)))


def _rms(x, gain):
    return x * lax.rsqrt(jnp.mean(x * x, axis=-1, keepdims=True) + EPS) * gain


class V:
    def __init__(self, arr, lead=None, cb=0, w=None):
        self.arr, self.lead, self.cb = arr, lead, cb
        self.w = arr.shape[-1] if w is None else w

    @property
    def rows(self):
        return self.arr.shape[-2]

    def spec(self, tile, order=None):
        lead, cb, w = self.lead, self.cb, self.w
        order = order or (lambda i: i)
        if lead is None:
            return pl.BlockSpec((tile, w), lambda i: (order(i), cb))
        return pl.BlockSpec((None, tile, w), lambda i: (lead, order(i), cb))

    def const_spec(self):
        lead, cb, w, r = self.lead, self.cb, self.w, self.rows
        if lead is None:
            return pl.BlockSpec((r, w), lambda i: (0, cb))
        return pl.BlockSpec((None, r, w), lambda i: (lead, 0, cb))


def _v(a):
    return a if isinstance(a, V) else V(a)


def _rowwise(name, fn, rows, consts, outs, tile=None):
    rows = [_v(r) for r in rows]
    consts = [_v(c) for c in consts]
    s = rows[0].rows
    tile = min(tile or ROW_TILE, s)
    nr, nc = len(rows), len(consts)

    def body(*refs):
        vals = [r[...].astype(f32) for r in refs[:nr + nc]]
        res = fn(*vals)
        for o, val in zip(refs[nr + nc:], res):
            o[...] = val.astype(o.dtype)

    return pl.pallas_call(
        body, name=name, grid=(s // tile,),
        in_specs=[r.spec(tile) for r in rows] + [c.const_spec() for c in consts],
        out_specs=[pl.BlockSpec((tile, w), lambda i: (i, 0)) for w, _ in outs],
        out_shape=[SDS((s, w), dt) for w, dt in outs],
        compiler_params=_cp(("parallel",)),
    )(*[r.arr for r in rows], *[c.arr for c in consts])


def _rowwise_vjp(name, fn, rows, consts, cts, d_rows, add=None, tile=None):
    rows = [_v(r) for r in rows]
    consts = [_v(c) for c in consts]
    cts = [[_v(c) for c in group] for group in cts]
    flat_cts = [c for group in cts for c in group]
    s = rows[0].rows
    tile = min(tile or ROW_TILE, s)
    nr, nc, nt = len(rows), len(consts), len(flat_cts)
    want = [k for k, dt in enumerate(d_rows) if dt is not None]
    has_add = add is not None
    add_v = [_v(add)] if has_add else []

    def body(*refs):
        vals = [r[...].astype(f32) for r in refs[:nr + nc]]
        ct_refs = refs[nr + nc:nr + nc + nt]
        pos = nr + nc + nt
        add_ref = refs[pos] if has_add else None
        pos += 1 if has_add else 0
        drow_refs = refs[pos:pos + len(want)]
        dconst_refs = refs[pos + len(want):]
        ctv, at = [], 0
        for group in cts:
            acc = ct_refs[at][...].astype(f32)
            for r in ct_refs[at + 1:at + len(group)]:
                acc = acc + r[...].astype(f32)
            at += len(group)
            ctv.append(acc)
        _, vjp = jax.vjp(fn, *vals)
        grads = vjp(tuple(ctv))
        for o, k in zip(drow_refs, want):
            g = grads[k]
            if has_add and k == want[0]:
                g = g + add_ref[...].astype(f32)
            o[...] = g.astype(o.dtype)

        @pl.when(pl.program_id(0) == 0)
        def _():
            for o in dconst_refs:
                o[...] = jnp.zeros_like(o)

        for o, g in zip(dconst_refs, grads[nr:]):
            o[...] += g

    outs = pl.pallas_call(
        body, name=name, grid=(s // tile,),
        in_specs=[r.spec(tile) for r in rows] + [c.const_spec() for c in consts]
        + [c.spec(tile) for c in flat_cts] + [a.spec(tile) for a in add_v],
        out_specs=[pl.BlockSpec((tile, rows[k].w), lambda i: (i, 0)) for k in want]
        + [pl.BlockSpec((c.rows, c.w), lambda i: (0, 0)) for c in consts],
        out_shape=[SDS((s, rows[k].w), d_rows[k]) for k in want] + [SDS((c.rows, c.w), f32) for c in consts],
        compiler_params=_cp(("arbitrary",)),
    )(*[r.arr for r in rows], *[c.arr for c in consts], *[c.arr for c in flat_cts], *[a.arr for a in add_v])
    return list(outs[:len(want)]), list(outs[len(want):])


def _pick(n, cap):
    best = None
    for t in range(LANES, min(n, cap) + 1, LANES):
        if n % t == 0:
            best = t
    return best or n


def _matmul(name, a, b, mode, out_dtype, add=None, tn_cap=1280):
    has_add = add is not None
    if mode in ("nn", "nt"):
        m, k = a.shape
        n = b.shape[1] if mode == "nn" else b.shape[0]
        tm = min(MM_TILE, m)
        tn = _pick(n, tn_cap) if k * n * 2 > (8 << 20) else n

        def body(*refs):
            a_ref, b_ref = refs[0], refs[1]
            o_ref = refs[-1]
            acc = _dg(a_ref[...], b_ref[...], mode)
            if has_add:
                acc = acc + refs[2][...].astype(f32)
            o_ref[...] = acc.astype(o_ref.dtype)

        b_spec = pl.BlockSpec((k, tn), lambda i, j: (0, j)) if mode == "nn" else pl.BlockSpec((tn, k), lambda i, j: (j, 0))
        in_specs = [pl.BlockSpec((tm, k), lambda i, j: (i, 0)), b_spec]
        args = [a, b]
        if has_add:
            in_specs.append(pl.BlockSpec((tm, tn), lambda i, j: (i, j)))
            args.append(add)
        return pl.pallas_call(
            body, name=name, grid=(m // tm, n // tn), in_specs=in_specs,
            out_specs=pl.BlockSpec((tm, tn), lambda i, j: (i, j)),
            out_shape=SDS((m, n), out_dtype), compiler_params=_cp(("parallel", "parallel")),
        )(*args)
    kk, m = a.shape
    n = b.shape[1]
    tk = min(MM_TILE, kk)
    tn = _pick(n, tn_cap) if m * n * 4 > (7 << 20) else n

    def body_tn(a_ref, b_ref, o_ref):
        @pl.when(pl.program_id(1) == 0)
        def _():
            o_ref[...] = jnp.zeros_like(o_ref)

        o_ref[...] += _dg(a_ref[...], b_ref[...], "tn")

    return pl.pallas_call(
        body_tn, name=name, grid=(n // tn, kk // tk),
        in_specs=[pl.BlockSpec((tk, m), lambda j, k: (k, 0)), pl.BlockSpec((tk, tn), lambda j, k: (k, j))],
        out_specs=pl.BlockSpec((m, tn), lambda j, k: (0, j)),
        out_shape=SDS((m, n), f32), compiler_params=_cp(("parallel", "arbitrary")),
    )(a, b)


def _ffn_fwd(name, h, gain, wgu, wd, layer):
    s, d = h.shape
    hs = wgu.shape[3]
    nh = wgu.shape[0] // 2
    tm = min(FFN_FWD_TILE, s)

    def body(h_ref, g_ref, wg_ref, wu_ref, wd_ref, o_ref, gate_ref, up_ref, n_scr, acc_scr):
        t = pl.program_id(1)

        @pl.when(t == 0)
        def _():
            n_scr[...] = _rms(h_ref[...], g_ref[...]).astype(bf16)
            acc_scr[...] = jnp.zeros_like(acc_scr)

        n = n_scr[...]
        gate = _dg(n, wg_ref[...], "nn")
        up = _dg(n, wu_ref[...], "nn")
        gate_ref[...] = gate.astype(gate_ref.dtype)
        up_ref[...] = up.astype(up_ref.dtype)
        acc_scr[...] += _dg(_silu(gate) * up, wd_ref[...], "nn")

        @pl.when(t == nh - 1)
        def _():
            o_ref[...] = h_ref[...] + 0.5 * acc_scr[...]

    saved_spec = pl.BlockSpec((None, tm, hs), lambda i, t: (t, i, 0))
    return pl.pallas_call(
        body, name=name, grid=(s // tm, nh),
        in_specs=[
            pl.BlockSpec((tm, d), lambda i, t: (i, 0)),
            pl.BlockSpec((1, d), lambda i, t: (0, 0)),
            pl.BlockSpec((None, None, d, hs), lambda i, t: (t, layer, 0, 0)),
            pl.BlockSpec((None, None, d, hs), lambda i, t: (t + nh, layer, 0, 0)),
            pl.BlockSpec((None, hs, d), lambda i, t: (layer, t, 0)),
        ],
        out_specs=[pl.BlockSpec((tm, d), lambda i, t: (i, 0)), saved_spec, saved_spec],
        out_shape=[SDS((s, d), f32), SDS((nh, s, hs), bf16), SDS((nh, s, hs), bf16)],
        scratch_shapes=[pltpu.VMEM((tm, d), bf16), pltpu.VMEM((tm, d), f32)],
        compiler_params=_cp(("parallel", "arbitrary")),
    )(h, gain, wgu, wgu, wd)


def _ffn_bwd(name, h, dout, gain, gate_s, up_s, wgu, wd, layer):
    s, d = h.shape
    hs = wgu.shape[3]
    nh = wgu.shape[0] // 2
    tm = min(FFN_BWD_TILE, s)

    def body(h_ref, do_ref, g_ref, gate_ref, up_ref, wg_ref, wu_ref, wd_ref, dn_ref, dwg_ref, dwu_ref, dwd_ref):
        @pl.when(pl.program_id(1) == 0)
        def _():
            dwg_ref[...] = jnp.zeros_like(dwg_ref)
            dwu_ref[...] = jnp.zeros_like(dwu_ref)
            dwd_ref[...] = jnp.zeros_like(dwd_ref)

        n = _rms(h_ref[...], g_ref[...]).astype(bf16)
        wg, wu, wdn = wg_ref[...], wu_ref[...], wd_ref[...]
        gate = gate_ref[...].astype(f32)
        up = up_ref[...].astype(f32)
        sg = _sigmoid(gate)
        act = gate * sg
        dy = (0.5 * do_ref[...]).astype(bf16)
        da = _dg(dy, wdn, "nt")
        dup = (da * act).astype(bf16)
        dgate = (da * up * (sg * (1.0 + gate * (1.0 - sg)))).astype(bf16)
        dwd_ref[...] += _dg(act * up, dy, "tn")
        dwg_ref[...] += _dg(n, dgate, "tn")
        dwu_ref[...] += _dg(n, dup, "tn")
        dn_ref[...] = (_dg(dgate, wg, "nt") + _dg(dup, wu, "nt")).astype(dn_ref.dtype)

    return pl.pallas_call(
        body, name=name, grid=(nh, s // tm),
        in_specs=[
            pl.BlockSpec((tm, d), lambda t, i: (i, 0)),
            pl.BlockSpec((tm, d), lambda t, i: (i, 0)),
            pl.BlockSpec((1, d), lambda t, i: (0, 0)),
            pl.BlockSpec((None, tm, hs), lambda t, i: (t, i, 0)),
            pl.BlockSpec((None, tm, hs), lambda t, i: (t, i, 0)),
            pl.BlockSpec((None, None, d, hs), lambda t, i: (t, layer, 0, 0)),
            pl.BlockSpec((None, None, d, hs), lambda t, i: (t + nh, layer, 0, 0)),
            pl.BlockSpec((None, hs, d), lambda t, i: (layer, t, 0)),
        ],
        out_specs=[
            pl.BlockSpec((None, tm, d), lambda t, i: (t, i, 0)),
            pl.BlockSpec((None, d, hs), lambda t, i: (t, 0, 0)),
            pl.BlockSpec((None, d, hs), lambda t, i: (t, 0, 0)),
            pl.BlockSpec((hs, d), lambda t, i: (t, 0)),
        ],
        out_shape=[SDS((nh, s, d), bf16), SDS((nh, d, hs), f32), SDS((nh, d, hs), f32), SDS((nh * hs, d), f32)],
        compiler_params=_cp(("parallel", "arbitrary")),
    )(h, dout, gain, gate_s, up_s, wgu, wgu, wd)


def _conv_fwd(name, x, w):
    x = _v(x)
    s, c = x.rows, x.w
    cw = w.shape[0]
    tile = min(ROW_TILE, s)
    cb = x.cb

    def body(x_ref, halo_ref, w_ref, o_ref, buf):
        first = pl.program_id(0) == 0
        buf[0:8, :] = jnp.where(first, 0.0, halo_ref[...])
        buf[8:8 + tile, :] = x_ref[...]
        acc = w_ref[0:1, :] * buf[pl.ds(8 - cw + 1, tile), :]
        for j in range(1, cw):
            acc = acc + w_ref[j:j + 1, :] * buf[pl.ds(8 - cw + 1 + j, tile), :]
        o_ref[...] = acc

    return pl.pallas_call(
        body, name=name, grid=(s // tile,),
        in_specs=[
            pl.BlockSpec((tile, c), lambda i: (i, cb)),
            pl.BlockSpec((8, c), lambda i: (jnp.maximum(i * (tile // 8) - 1, 0), cb)),
            pl.BlockSpec((cw, c), lambda i: (0, 0)),
        ],
        out_specs=pl.BlockSpec((tile, c), lambda i: (i, 0)),
        out_shape=SDS((s, c), f32),
        scratch_shapes=[pltpu.VMEM((tile + 8, c), f32)],
        compiler_params=_cp(("parallel",)),
    )(x.arr, x.arr, w)


def _conv_bwd(name, x, dy, w):
    x = _v(x)
    s, c = x.rows, x.w
    cw = w.shape[0]
    tile = min(ROW_TILE, s)
    n_tiles = s // tile
    cb = x.cb

    def body(x_ref, xh_ref, dy_ref, dyh_ref, w_ref, dx_ref, dw_ref, xbuf, dbuf):
        i = pl.program_id(0)
        xbuf[0:8, :] = jnp.where(i == 0, 0.0, xh_ref[...])
        xbuf[8:8 + tile, :] = x_ref[...]
        dyv = dy_ref[...]
        dbuf[0:tile, :] = dyv
        dbuf[tile:tile + 8, :] = jnp.where(i == n_tiles - 1, 0.0, dyh_ref[...])

        @pl.when(i == 0)
        def _():
            dw_ref[...] = jnp.zeros_like(dw_ref)

        acc = w_ref[0:1, :] * dbuf[pl.ds(cw - 1, tile), :]
        for j in range(1, cw):
            acc = acc + w_ref[j:j + 1, :] * dbuf[pl.ds(cw - 1 - j, tile), :]
        dx_ref[...] = acc.astype(dx_ref.dtype)
        for j in range(cw):
            dw_ref[j:j + 1, :] += jnp.sum(xbuf[pl.ds(8 - cw + 1 + j, tile), :] * dyv, axis=0, keepdims=True)

    return pl.pallas_call(
        body, name=name, grid=(n_tiles,),
        in_specs=[
            pl.BlockSpec((tile, c), lambda i: (i, cb)),
            pl.BlockSpec((8, c), lambda i: (jnp.maximum(i * (tile // 8) - 1, 0), cb)),
            pl.BlockSpec((tile, c), lambda i: (i, 0)),
            pl.BlockSpec((8, c), lambda i: (jnp.minimum((i + 1) * (tile // 8), s // 8 - 1), 0)),
            pl.BlockSpec((cw, c), lambda i: (0, 0)),
        ],
        out_specs=[pl.BlockSpec((tile, c), lambda i: (i, 0)), pl.BlockSpec((cw, c), lambda i: (0, 0))],
        out_shape=[SDS((s, c), bf16), SDS((cw, c), f32)],
        scratch_shapes=[pltpu.VMEM((tile + 8, c), f32), pltpu.VMEM((tile + 8, c), f32)],
        compiler_params=_cp(("arbitrary",)),
    )(x.arr, x.arr, dy, dy, w)


def _gdn_pre(n_heads, head_dim, yc, ab, a_log, dt_bias):
    gw = n_heads * head_dim
    act = _silu(yc)
    parts = _split_lanes(act, head_dim)
    qs = [p * lax.rsqrt(jnp.sum(p * p, axis=-1, keepdims=True) + EPS) * (head_dim ** -0.5) for p in parts[:n_heads]]
    ks = [p * lax.rsqrt(jnp.sum(p * p, axis=-1, keepdims=True) + EPS) for p in parts[n_heads:2 * n_heads]]
    lane = lax.broadcasted_iota(jnp.int32, ab.shape, 1)
    g = -jnp.exp(a_log) * _softplus(ab + dt_bias)
    gb = jnp.where(lane < n_heads, g, jnp.where(lane < 2 * n_heads, _sigmoid(ab), 0.0))
    del gw
    return (jnp.concatenate(qs, axis=1), jnp.concatenate(ks, axis=1),
            jnp.concatenate(list(parts[2 * n_heads:]), axis=1), gb)


def _gdn_post(n_heads, head_dim, o, z, out_norm):
    parts = _split_lanes(o, head_dim)
    normed = jnp.concatenate([_rms(p, out_norm) for p in parts], axis=1)
    return (normed * _silu(z),)


def _gdn_chunk(n_heads, head_dim, q, k, v, gb, *states):
    c = q.shape[0]
    ri = lax.broadcasted_iota(jnp.int32, (c, c), 0)
    ci = lax.broadcasted_iota(jnp.int32, (c, c), 1)
    incl, strict, diag = ri >= ci, ri > ci, ri == ci
    lane = lax.broadcasted_iota(jnp.int32, gb.shape, 1)
    qs, ks, vs = _split_lanes(q, head_dim), _split_lanes(k, head_dim), _split_lanes(v, head_dim)
    heads = range(n_heads)
    g = [jnp.sum(jnp.where(lane == h, gb, 0.0), axis=1, keepdims=True) for h in heads]
    beta = [jnp.sum(jnp.where(lane == n_heads + h, gb, 0.0), axis=1, keepdims=True) for h in heads]
    g_row = [jnp.sum(jnp.where(diag, g[h], 0.0), axis=0, keepdims=True) for h in heads]
    cg_col = [jnp.sum(jnp.where(incl, g_row[h], 0.0), axis=1, keepdims=True) for h in heads]
    cg_row = [jnp.sum(jnp.where(ri <= ci, g[h], 0.0), axis=0, keepdims=True) for h in heads]
    g_last = [jnp.sum(g[h], axis=0, keepdims=True) for h in heads]
    decay = [jnp.where(incl, jnp.exp(jnp.where(incl, cg_col[h] - cg_row[h], 0.0)), 0.0) for h in heads]
    kb = [ks[h] * beta[h] for h in heads]
    lower = [jnp.where(strict, _mm(kb[h], ks[h], "nt") * decay[h], 0.0) for h in heads]
    eye = jnp.where(diag, 1.0, 0.0)
    off_diag = [t - eye for t in _unit_lower_inverses(tuple(lower))]
    e_col = [jnp.exp(cg_col[h]) for h in heads]
    vb = [vs[h] * beta[h] for h in heads]
    kbg = [kb[h] * e_col[h] for h in heads]
    u = [vb[h] + _mm(off_diag[h], vb[h], "nn") for h in heads]
    w = [kbg[h] + _mm(off_diag[h], kbg[h], "nn") for h in heads]
    qk = [jnp.where(incl, _mm(qs[h], ks[h], "nt") * decay[h], 0.0) for h in heads]
    v_new = [u[h] - _mm(w[h], states[h], "nn") for h in heads]
    inter = [_mm(qs[h] * e_col[h], states[h], "nn") for h in heads]
    outs = [inter[h] + _mm(qk[h], v_new[h], "nn") for h in heads]
    k_tail = [ks[h] * jnp.exp(g_last[h] - cg_col[h]) for h in heads]
    new_states = [states[h] * jnp.exp(g_last[h]) + _mm(k_tail[h], v_new[h], "tn") for h in heads]
    return (jnp.concatenate(outs, axis=1), *new_states)


def _gdn_chunk_fwd(name, q, k, v, gb, n_heads, head_dim):
    s, gw = q.shape
    n = s // CHUNK
    fn = functools.partial(_gdn_chunk, n_heads, head_dim)

    def body(q_ref, k_ref, v_ref, gb_ref, o_ref, st_ref, st_scr):
        @pl.when(pl.program_id(0) == 0)
        def _():
            st_scr[...] = jnp.zeros_like(st_scr)

        st_ref[...] = st_scr[...]
        res = fn(q_ref[...], k_ref[...], v_ref[...], gb_ref[...], *[st_scr[h] for h in range(n_heads)])
        o_ref[...] = res[0]
        for h in range(n_heads):
            st_scr[h] = res[1 + h]

    row = lambda w: pl.BlockSpec((CHUNK, w), lambda i: (i, 0))
    return pl.pallas_call(
        body, name=name, grid=(n,),
        in_specs=[row(gw), row(gw), row(gw), row(LANES)],
        out_specs=[row(gw), pl.BlockSpec((None, n_heads, head_dim, head_dim), lambda i: (i, 0, 0, 0))],
        out_shape=[SDS((s, gw), f32), SDS((n, n_heads, head_dim, head_dim), f32)],
        scratch_shapes=[pltpu.VMEM((n_heads, head_dim, head_dim), f32)],
        compiler_params=_cp(("arbitrary",)),
    )(q, k, v, gb)


def _gdn_chunk_bwd(name, q, k, v, gb, states, d_out, n_heads, head_dim):
    s, gw = q.shape
    n = s // CHUNK
    fn = functools.partial(_gdn_chunk, n_heads, head_dim)

    def body(q_ref, k_ref, v_ref, gb_ref, st_ref, do_ref, dq_ref, dk_ref, dv_ref, dgb_ref, dst_scr):
        @pl.when(pl.program_id(0) == 0)
        def _():
            dst_scr[...] = jnp.zeros_like(dst_scr)

        _, vjp = jax.vjp(fn, q_ref[...], k_ref[...], v_ref[...], gb_ref[...], *[st_ref[h] for h in range(n_heads)])
        grads = vjp((do_ref[...].astype(f32), *[dst_scr[h] for h in range(n_heads)]))
        dq_ref[...] = grads[0]
        dk_ref[...] = grads[1]
        dv_ref[...] = grads[2]
        dgb_ref[...] = grads[3]
        for h in range(n_heads):
            dst_scr[h] = grads[4 + h]

    row = lambda w: pl.BlockSpec((CHUNK, w), lambda i: (n - 1 - i, 0))
    return pl.pallas_call(
        body, name=name, grid=(n,),
        in_specs=[row(gw), row(gw), row(gw), row(LANES),
                  pl.BlockSpec((None, n_heads, head_dim, head_dim), lambda i: (n - 1 - i, 0, 0, 0)), row(gw)],
        out_specs=[row(gw), row(gw), row(gw), row(LANES)],
        out_shape=[SDS((s, gw), f32), SDS((s, gw), f32), SDS((s, gw), f32), SDS((s, LANES), f32)],
        scratch_shapes=[pltpu.VMEM((n_heads, head_dim, head_dim), f32)],
        compiler_params=_cp(("arbitrary",)),
    )(q, k, v, gb, states, d_out)


def _mem_attn(qm, km, vm):
    width = qm.shape[1]
    hd = width // MEM_HEADS
    lane = lax.broadcasted_iota(jnp.int32, (1, width), 1)
    out = jnp.zeros_like(qm)
    for h in range(MEM_HEADS):
        msk = jnp.where((lane >= h * hd) & (lane < (h + 1) * hd), 1.0, 0.0)
        logits = _mm(qm * msk, km, "nt") * (hd ** -0.5)
        p = jnp.exp(logits - jnp.max(logits, axis=-1, keepdims=True))
        p = p / jnp.sum(p, axis=-1, keepdims=True)
        out = out + _mm(p, vm, "nn") * msk
    return (out,)


def _kv_post(n_heads, pk, pv, pf, b_f):
    lane = lax.broadcasted_iota(jnp.int32, pf.shape, 1)
    log_f = jnp.where(lane < n_heads, -_softplus(-(pf + b_f)), 0.0)
    return pk, pv, log_f


def _cumsum(name, xs, reverse, scale=None):
    s, w = xs[0].shape
    tile = min(CUMSUM_TILE, s)
    n = s // tile

    def body(*refs):
        x_refs, o_ref, carry = refs[:-2], refs[-2], refs[-1]

        @pl.when(pl.program_id(0) == 0)
        def _():
            carry[...] = jnp.zeros_like(carry)

        xv = x_refs[0][...]
        for r in x_refs[1:]:
            xv = xv + r[...]
        if scale is not None:
            xv = xv * scale
        ri = lax.broadcasted_iota(jnp.int32, (tile, tile), 0)
        ci = lax.broadcasted_iota(jnp.int32, (tile, tile), 1)
        tri = jnp.where((ri <= ci) if reverse else (ri >= ci), 1.0, 0.0).astype(bf16)
        x1 = xv.astype(bf16)
        r1 = xv - x1.astype(f32)
        x2 = r1.astype(bf16)
        x3 = (r1 - x2.astype(f32)).astype(bf16)
        acc = carry[...] + _dg(tri, x1, "nn") + _dg(tri, x2, "nn") + _dg(tri, x3, "nn")
        o_ref[...] = acc
        carry[...] += jnp.sum(xv, axis=0, keepdims=True)

    order = (lambda i: (n - 1 - i, 0)) if reverse else (lambda i: (i, 0))
    return pl.pallas_call(
        body, name=name, grid=(n,),
        in_specs=[pl.BlockSpec((tile, w), order)] * len(xs), out_specs=pl.BlockSpec((tile, w), order),
        out_shape=SDS((s, w), f32), scratch_shapes=[pltpu.VMEM((1, w), f32)],
        compiler_params=_cp(("arbitrary",)),
    )(*xs)


LOG2_E = math.log2(math.e)


def _fox_logits(q_ref, k_ref, cr_ref, hh, head_dim, scale, diagonal):
    sl = slice(hh * head_dim, (hh + 1) * head_dim)
    s = _dg(q_ref[:, sl], k_ref[:, sl], "nt") * (scale * LOG2_E) - cr_ref[hh]
    if diagonal:
        tq, tk = s.shape
        ok = lax.broadcasted_iota(jnp.int32, (tq, tk), 1) <= lax.broadcasted_iota(jnp.int32, (tq, tk), 0)
        s = jnp.where(ok, s, NEG_INF)
    return s, sl


def _on_causal_tiles(i, j, fn):
    @pl.when(j < i)
    def _():
        fn(False)

    @pl.when(j == i)
    def _():
        fn(True)


def _fox_fwd(name, q, k, v, crow, n_heads, head_dim):
    s = k.shape[0]
    tq = tk = min(ATT_TILE, s)
    nq, nk = s // tq, s // tk
    scale = head_dim ** -0.5
    hpb = ATT_HEADS
    wb = hpb * head_dim

    def body(q_ref, k_ref, v_ref, cr_ref, o_ref, lse_ref, m_scr, l_scr, acc_scr):
        i, j = pl.program_id(1), pl.program_id(2)

        @pl.when(j == 0)
        def _():
            m_scr[...] = jnp.full_like(m_scr, NEG_INF)
            l_scr[...] = jnp.zeros_like(l_scr)
            acc_scr[...] = jnp.zeros_like(acc_scr)

        def step(diagonal):
            heads = range(hpb)
            sl = [slice(hh * head_dim, (hh + 1) * head_dim) for hh in heads]
            sc = [_fox_logits(q_ref, k_ref, cr_ref, hh, head_dim, scale, diagonal)[0] for hh in heads]
            m_old = [m_scr[hh] for hh in heads]
            m_new = [jnp.maximum(m_old[hh], jnp.max(sc[hh], axis=-1, keepdims=True)) for hh in heads]
            p = [jnp.exp2(sc[hh] - m_new[hh]) for hh in heads]
            alpha = [jnp.exp2(m_old[hh] - m_new[hh]) for hh in heads]
            pv = [_dg(p[hh], v_ref[:, sl[hh]], "nn") for hh in heads]
            for hh in heads:
                l_scr[hh] = alpha[hh] * l_scr[hh] + jnp.sum(p[hh], axis=-1, keepdims=True)
                acc_scr[:, sl[hh]] = alpha[hh] * acc_scr[:, sl[hh]] + pv[hh]
                m_scr[hh] = m_new[hh]

        _on_causal_tiles(i, j, step)

        @pl.when(j == nk - 1)
        def _():
            for hh in range(hpb):
                sl = slice(hh * head_dim, (hh + 1) * head_dim)
                o_ref[:, sl] = (acc_scr[:, sl] / l_scr[hh]).astype(o_ref.dtype)
                lse_ref[hh] = m_scr[hh] + jnp.log2(l_scr[hh])

    return pl.pallas_call(
        body, name=name, grid=(n_heads // hpb, nq, nk),
        in_specs=[
            pl.BlockSpec((tq, wb), lambda g, i, j: (i, g)),
            pl.BlockSpec((tk, wb), lambda g, i, j: (jnp.minimum(j, i), g)),
            pl.BlockSpec((tk, wb), lambda g, i, j: (jnp.minimum(j, i), g)),
            pl.BlockSpec((hpb, 1, tk), lambda g, i, j: (g, 0, jnp.minimum(j, i))),
        ],
        out_specs=[pl.BlockSpec((tq, wb), lambda g, i, j: (i, g)),
                   pl.BlockSpec((hpb, tq, 1), lambda g, i, j: (g, i, 0))],
        out_shape=[SDS((s, n_heads * head_dim), bf16), SDS((n_heads, s, 1), f32)],
        scratch_shapes=[pltpu.VMEM((hpb, tq, 1), f32), pltpu.VMEM((hpb, tq, 1), f32), pltpu.VMEM((tq, wb), f32)],
        compiler_params=_cp(("parallel", "parallel", "arbitrary")),
    )(q, k, v, crow)


def _fox_probs(q_ref, k_ref, v_ref, cr_ref, lse_ref, do_ref, hh, head_dim, scale, diagonal):
    sc, sl = _fox_logits(q_ref, k_ref, cr_ref, hh, head_dim, scale, diagonal)
    return jnp.exp2(sc - lse_ref[hh]), _dg(do_ref[:, sl], v_ref[:, sl], "nt"), sl


def _fox_bwd_dq(name, q, k, v, crow, lse, o, do, n_heads, head_dim):
    s = k.shape[0]
    tq = tk = min(ATT_TILE, s)
    nq, nk = s // tq, s // tk
    scale = head_dim ** -0.5
    hpb = ATT_HEADS
    wb = hpb * head_dim

    def body(q_ref, k_ref, v_ref, cr_ref, lse_ref, o_ref, do_ref, dq_ref, delta_ref, dcc_ref, acc_scr):
        i, j = pl.program_id(1), pl.program_id(2)
        heads = range(hpb)

        @pl.when(j == 0)
        def _():
            prod = o_ref[...].astype(f32) * do_ref[...].astype(f32)
            for hh in heads:
                delta_ref[hh] = jnp.sum(prod[:, hh * head_dim:(hh + 1) * head_dim], axis=-1, keepdims=True)
            dcc_ref[...] = jnp.zeros_like(dcc_ref)
            acc_scr[...] = jnp.zeros_like(acc_scr)

        def step(diagonal):
            pd = [_fox_probs(q_ref, k_ref, v_ref, cr_ref, lse_ref, do_ref, hh, head_dim, scale, diagonal) for hh in heads]
            ds = [pd[hh][0] * (pd[hh][1] - delta_ref[hh]) for hh in heads]
            dqs = [_dg(ds[hh], k_ref[:, pd[hh][2]], "nn") for hh in heads]
            for hh in heads:
                dcc_ref[hh] += jnp.sum(ds[hh], axis=-1, keepdims=True)
                acc_scr[:, pd[hh][2]] += dqs[hh]

        _on_causal_tiles(i, j, step)

        @pl.when(j == nk - 1)
        def _():
            dq_ref[...] = (acc_scr[...] * scale).astype(dq_ref.dtype)

    qspec = pl.BlockSpec((tq, wb), lambda g, i, j: (i, g))
    kspec = pl.BlockSpec((tk, wb), lambda g, i, j: (jnp.minimum(j, i), g))
    cspec = pl.BlockSpec((hpb, tq, 1), lambda g, i, j: (g, i, 0))
    return pl.pallas_call(
        body, name=name, grid=(n_heads // hpb, nq, nk),
        in_specs=[qspec, kspec, kspec,
                  pl.BlockSpec((hpb, 1, tk), lambda g, i, j: (g, 0, jnp.minimum(j, i))), cspec, qspec, qspec],
        out_specs=[qspec, cspec, cspec],
        out_shape=[SDS((s, n_heads * head_dim), bf16), SDS((n_heads, s, 1), f32), SDS((n_heads, s, 1), f32)],
        scratch_shapes=[pltpu.VMEM((tq, wb), f32)],
        compiler_params=_cp(("parallel", "parallel", "arbitrary")),
    )(q, k, v, crow, lse, o, do)


def _fox_bwd_dkv(name, q, k, v, crow, lse, delta, do, n_heads, head_dim):
    s = k.shape[0]
    tq = tk = min(ATT_TILE, s)
    nq, nk = s // tq, s // tk
    scale = head_dim ** -0.5
    hpb = ATT_HEADS
    wb = hpb * head_dim

    def body(q_ref, k_ref, v_ref, cr_ref, lse_ref, delta_ref, do_ref, dk_ref, dv_ref, dc_ref):
        j, i = pl.program_id(1), pl.program_id(2)

        @pl.when(i == 0)
        def _():
            dk_ref[...] = jnp.zeros_like(dk_ref)
            dv_ref[...] = jnp.zeros_like(dv_ref)
            dc_ref[...] = jnp.zeros_like(dc_ref)

        def step(diagonal):
            heads = range(hpb)
            pd = [_fox_probs(q_ref, k_ref, v_ref, cr_ref, lse_ref, do_ref, hh, head_dim, scale, diagonal) for hh in heads]
            ds = [pd[hh][0] * (pd[hh][1] - delta_ref[hh]) for hh in heads]
            dvs = [_dg(pd[hh][0], do_ref[:, pd[hh][2]], "tn") for hh in heads]
            dks = [_dg(ds[hh], q_ref[:, pd[hh][2]], "tn") for hh in heads]
            for hh in heads:
                sl = pd[hh][2]
                dv_ref[:, sl] += dvs[hh]
                dk_ref[:, sl] += dks[hh] * scale
                dc_ref[hh] -= jnp.sum(ds[hh], axis=0, keepdims=True)

        _on_causal_tiles(i, j, step)

    qspec = pl.BlockSpec((tq, wb), lambda g, j, i: (jnp.maximum(i, j), g))
    kspec = pl.BlockSpec((tk, wb), lambda g, j, i: (j, g))
    cspec = pl.BlockSpec((hpb, tq, 1), lambda g, j, i: (g, jnp.maximum(i, j), 0))
    rspec = pl.BlockSpec((hpb, 1, tk), lambda g, j, i: (g, 0, j))
    return pl.pallas_call(
        body, name=name, grid=(n_heads // hpb, nk, nq),
        in_specs=[qspec, kspec, kspec, rspec, cspec, cspec, qspec],
        out_specs=[kspec, kspec, rspec],
        out_shape=[SDS((s, n_heads * head_dim), f32), SDS((s, n_heads * head_dim), f32), SDS((n_heads, 1, s), f32)],
        compiler_params=_cp(("parallel", "parallel", "arbitrary")),
    )(q, k, v, crow, lse, delta, do)


def _final_loss(name, h, target, gain):
    s, d = h.shape
    tile = min(ROW_TILE, s)

    def body(h_ref, t_ref, g_ref, loss_ref, dh_ref, dg_ref):
        @pl.when(pl.program_id(0) == 0)
        def _():
            loss_ref[...] = jnp.zeros_like(loss_ref)
            dg_ref[...] = jnp.zeros_like(dg_ref)

        x, g = h_ref[...], g_ref[...]
        rstd = lax.rsqrt(jnp.mean(x * x, axis=-1, keepdims=True) + EPS)
        xhat = x * rstd
        err = xhat * g - t_ref[...]
        row = jnp.sum(err * err, axis=-1, keepdims=True) * (0.5 / d)
        loss_ref[...] += jnp.sum(row, axis=0, keepdims=True)
        dy = err * (1.0 / d)
        dg_ref[...] += jnp.sum(dy * xhat, axis=0, keepdims=True)
        dxhat = dy * g
        dh_ref[...] = rstd * (dxhat - xhat * jnp.mean(dxhat * xhat, axis=-1, keepdims=True))

    return pl.pallas_call(
        body, name=name, grid=(s // tile,),
        in_specs=[pl.BlockSpec((tile, d), lambda i: (i, 0)), pl.BlockSpec((tile, d), lambda i: (i, 0)),
                  pl.BlockSpec((1, d), lambda i: (0, 0))],
        out_specs=[pl.BlockSpec((1, LANES), lambda i: (0, 0)), pl.BlockSpec((tile, d), lambda i: (i, 0)),
                   pl.BlockSpec((1, d), lambda i: (0, 0))],
        out_shape=[SDS((1, LANES), f32), SDS((s, d), f32), SDS((1, d), f32)],
        compiler_params=_cp(("arbitrary",)),
    )(h, target, gain)


def _adamw(name, parts, w, m, v):
    r, c = w.shape
    rb = r
    for cand in (256, 128, 64, 32, 16):
        if r % cand == 0 and cand * c * 4 <= (2 << 20):
            rb = cand
            break
    n_parts = parts.shape[0]

    def body(p_ref, w_ref, m_ref, v_ref, g_out, d_out, m_out, v_out):
        g = p_ref[0].astype(f32)
        for k in range(1, n_parts):
            g = g + p_ref[k].astype(f32)
        m_new = ADAM_B1 * m_ref[...] + (1.0 - ADAM_B1) * g
        v_new = ADAM_B2 * v_ref[...] + (1.0 - ADAM_B2) * (g * g)
        m_hat = m_new / (1.0 - ADAM_B1 ** ADAM_STEP)
        v_hat = v_new / (1.0 - ADAM_B2 ** ADAM_STEP)
        g_out[...] = g
        d_out[...] = -ADAM_LR * (m_hat / (jnp.sqrt(v_hat) + ADAM_EPS) + ADAM_WD * w_ref[...])
        m_out[...] = m_new
        v_out[...] = v_new

    blk = pl.BlockSpec((rb, c), lambda i: (i, 0))
    return pl.pallas_call(
        body, name=name, grid=(r // rb,),
        in_specs=[pl.BlockSpec((n_parts, rb, c), lambda i: (0, i, 0)), blk, blk, blk],
        out_specs=[blk, blk, blk, blk],
        out_shape=[SDS((r, c), f32)] * 4,
        compiler_params=_cp(("parallel",)),
    )(parts, w, m, v)


def _position():
    x, y, c = lax.axis_index("x"), lax.axis_index("y"), lax.axis_index("c")
    return x, y, c


def _all_gather(name, shards):
    n = len(shards)

    def body(*refs):
        ins, outs = refs[:n], refs[n:2 * n]
        send_sems, recv_sems, local_sems = refs[2 * n:]
        x, y, c = _position()
        me, sibling = (x, y, c), (x, y, 1 - c)
        chips = [(1 - x, y), (x, 1 - y), (1 - x, 1 - y)]

        def slot(a, block):
            px, py, pc = block
            return outs[a].at[4 * px + 2 * py + pc]

        def copy(a, k, block, to, src=None):
            return pltpu.make_async_remote_copy(
                src_ref=slot(a, block) if src is None else src, dst_ref=slot(a, block),
                send_sem=send_sems.at[a, k], recv_sem=recv_sems.at[a, k], device_id=to, device_id_type=MESH)

        local = [pltpu.make_async_copy(ins[a], slot(a, me), local_sems.at[a]) for a in range(n)]
        for cp in local:
            cp.start()
        started = []
        for a in range(n):
            first = [copy(a, 0, me, sibling, src=ins[a])]
            first += [copy(a, 1 + j, me, (*chip, c), src=ins[a]) for j, chip in enumerate(chips)]
            for cp in first:
                cp.start()
            started += first
        for a in range(n):
            for j, chip in enumerate(chips):
                copy(a, 1 + j, (*chip, c), me).wait_recv()
                passed = copy(a, 4 + j, (*chip, c), sibling)
                passed.start()
                started.append(passed)
        for a in range(n):
            copy(a, 0, sibling, me).wait_recv()
            for j, chip in enumerate(chips):
                copy(a, 4 + j, (*chip, 1 - c), me).wait_recv()
        for cp in started:
            cp.wait_send()
        for cp in local:
            cp.wait()

    any_spec = pl.BlockSpec(memory_space=pl.ANY)
    outs = pl.pallas_call(
        body, name=name,
        in_specs=[any_spec] * n, out_specs=[any_spec] * n,
        out_shape=[SDS((N_DEV, *a.shape), a.dtype) for a in shards],
        scratch_shapes=[pltpu.SemaphoreType.DMA((n, 7)), pltpu.SemaphoreType.DMA((n, 7)), pltpu.SemaphoreType.DMA((n,))],
    )(*shards)
    return list(outs)


N_CHIPS = N_DEV // 2


def _sibling_exchange(name, arrs):
    n = len(arrs)

    def body(*refs):
        ins, outs = refs[:n], refs[n:2 * n]
        send_sems, recv_sems = refs[2 * n:]
        x, y, c = _position()
        copies = [pltpu.make_async_remote_copy(
            src_ref=ins[a], dst_ref=outs[a], send_sem=send_sems.at[a], recv_sem=recv_sems.at[a],
            device_id=(x, y, 1 - c), device_id_type=MESH) for a in range(n)]
        for cp in copies:
            cp.start()
        for cp in copies:
            cp.wait_recv()
        for cp in copies:
            cp.wait_send()

    any_spec = pl.BlockSpec(memory_space=pl.ANY)
    outs = pl.pallas_call(
        body, name=name,
        in_specs=[any_spec] * n, out_specs=[any_spec] * n, out_shape=[SDS(a.shape, a.dtype) for a in arrs],
        scratch_shapes=[pltpu.SemaphoreType.DMA((n,)), pltpu.SemaphoreType.DMA((n,))],
    )(*arrs)
    return list(outs)


def _chip_exchange(name, parts):
    n = len(parts)

    def body(*refs):
        ins, outs = refs[:n], refs[n:2 * n]
        send_sems, recv_sems, local_sems = refs[2 * n:]
        x, y, c = _position()
        my_chip = 2 * x + y
        local = [pltpu.make_async_copy(ins[a].at[my_chip], outs[a].at[my_chip], local_sems.at[a]) for a in range(n)]
        for cp in local:
            cp.start()
        sends, recvs = [], []
        for a in range(n):
            for k in range(1, N_CHIPS):
                px = (1 - x) if (k >> 1) & 1 else x
                py = (1 - y) if k & 1 else y
                peer_chip = 2 * px + py
                send = pltpu.make_async_remote_copy(
                    src_ref=ins[a].at[peer_chip], dst_ref=outs[a].at[my_chip],
                    send_sem=send_sems.at[a, k - 1], recv_sem=recv_sems.at[a, k - 1],
                    device_id=(px, py, c), device_id_type=MESH)
                send.start()
                sends.append(send)
                recvs.append(pltpu.make_async_remote_copy(
                    src_ref=ins[a].at[peer_chip], dst_ref=outs[a].at[peer_chip],
                    send_sem=send_sems.at[a, k - 1], recv_sem=recv_sems.at[a, k - 1],
                    device_id=(px, py, c), device_id_type=MESH))
        for cp in recvs:
            cp.wait_recv()
        for cp in sends:
            cp.wait_send()
        for cp in local:
            cp.wait()

    any_spec = pl.BlockSpec(memory_space=pl.ANY)
    outs = pl.pallas_call(
        body, name=name,
        in_specs=[any_spec] * n, out_specs=[any_spec] * n,
        out_shape=[SDS(a.shape, a.dtype) for a in parts],
        scratch_shapes=[pltpu.SemaphoreType.DMA((n, N_CHIPS - 1)), pltpu.SemaphoreType.DMA((n, N_CHIPS - 1)),
                        pltpu.SemaphoreType.DMA((n,))],
    )(*parts)
    return list(outs)


def _reduce_exchange(stacks):
    core = lax.axis_index("c")
    by_core = [st.reshape(N_CHIPS, 2, *st.shape[1:]) for st in stacks]
    mine = [lax.dynamic_index_in_dim(v, core, axis=1, keepdims=False) for v in by_core]
    theirs = _sibling_exchange(
        "exchange_sibling", [lax.dynamic_index_in_dim(v, 1 - core, axis=1, keepdims=False) for v in by_core])
    summed = []
    for a, (m, t) in enumerate(zip(mine, theirs)):
        cols = m.shape[-1]
        (both,) = _rowwise(f"exchange_add_{a}", lambda p, q: (p + q,), [m.reshape(-1, cols), t.reshape(-1, cols)], [],
                           [(cols, m.dtype)])
        summed.append(both.reshape(m.shape))
    return _chip_exchange("exchange_chips", summed)


def _rows_from_shards(g):
    n, l, r, c = g.shape
    return g.transpose(1, 0, 2, 3).reshape(l, n * r, c)


def _rows_to_shards(w):
    l, rows, c = w.shape
    return w.reshape(l, N_DEV, rows // N_DEV, c).transpose(1, 0, 2, 3)


def _pad_lanes(a, width):
    return jnp.pad(a, [(0, 0)] * (a.ndim - 1) + [(0, width - a.shape[-1])])


def _row(vec, width=None):
    vec = vec.reshape(1, -1)
    return vec if width is None else _pad_lanes(vec, width)


class _SmallPack:
    def __init__(self, shapes):
        self.shapes, self.offsets, at = shapes, {}, 0
        for name, shape in shapes.items():
            last = shape[-1]
            lead = int(math.prod(shape[:-1]))
            rows = lead * (last // LANES) if last >= LANES else lead
            self.offsets[name] = (at, rows)
            at += rows
        self.rows = -(-at // 8) * 8

    def pack(self, values):
        pieces = []
        for name, shape in self.shapes.items():
            val = values[name].astype(f32)
            if shape[-1] >= LANES:
                pieces.append(val.reshape(-1, LANES))
            else:
                pieces.append(_pad_lanes(val.reshape(-1, shape[-1]), LANES))
        used = sum(p.shape[0] for p in pieces)
        if used < self.rows:
            pieces.append(jnp.zeros((self.rows - used, LANES), f32))
        return jnp.concatenate(pieces, axis=0)

    def unpack(self, packed):
        out = {}
        for name, shape in self.shapes.items():
            at, rows = self.offsets[name]
            blk = packed[at:at + rows]
            out[name] = blk.reshape(shape) if shape[-1] >= LANES else blk[:, :shape[-1]].reshape(shape)
        return out


def kernel(x, mem, ffn1_norm, ffn1_w_gate_up, ffn1_w_down, mix_norm, ffn2_norm, ffn2_w_gate_up, ffn2_w_down, gdn_w_in, gdn_conv, gdn_A_log, gdn_dt_bias, gdn_out_norm, fox_w_in, w_out, mem_norm, mem_w_kv, kv_norm, kv_w, kv_b_f, final_norm, loss_target, m_ffn1_norm, m_ffn1_w_gate_up, m_ffn1_w_down, m_mix_norm, m_ffn2_norm, m_ffn2_w_gate_up, m_ffn2_w_down, m_gdn_w_in, m_gdn_conv, m_gdn_A_log, m_gdn_dt_bias, m_gdn_out_norm, m_fox_w_in, m_w_out, m_mem_norm, m_mem_w_kv, m_kv_norm, m_kv_w, m_kv_b_f, m_final_norm, v_ffn1_norm, v_ffn1_w_gate_up, v_ffn1_w_down, v_mix_norm, v_ffn2_norm, v_ffn2_w_gate_up, v_ffn2_w_down, v_gdn_w_in, v_gdn_conv, v_gdn_A_log, v_gdn_dt_bias, v_gdn_out_norm, v_fox_w_in, v_w_out, v_mem_norm, v_mem_w_kv, v_kv_norm, v_kv_w, v_kv_b_f, v_final_norm):
    weights = dict(ffn1_norm=ffn1_norm, ffn1_w_gate_up=ffn1_w_gate_up, ffn1_w_down=ffn1_w_down, mix_norm=mix_norm,
                   ffn2_norm=ffn2_norm, ffn2_w_gate_up=ffn2_w_gate_up, ffn2_w_down=ffn2_w_down, gdn_w_in=gdn_w_in,
                   gdn_conv=gdn_conv, gdn_A_log=gdn_A_log, gdn_dt_bias=gdn_dt_bias, gdn_out_norm=gdn_out_norm,
                   fox_w_in=fox_w_in, w_out=w_out, mem_norm=mem_norm, mem_w_kv=mem_w_kv, kv_norm=kv_norm, kv_w=kv_w,
                   kv_b_f=kv_b_f, final_norm=final_norm)
    mom_m = dict(ffn1_norm=m_ffn1_norm, ffn1_w_gate_up=m_ffn1_w_gate_up, ffn1_w_down=m_ffn1_w_down, mix_norm=m_mix_norm,
                 ffn2_norm=m_ffn2_norm, ffn2_w_gate_up=m_ffn2_w_gate_up, ffn2_w_down=m_ffn2_w_down, gdn_w_in=m_gdn_w_in,
                 gdn_conv=m_gdn_conv, gdn_A_log=m_gdn_A_log, gdn_dt_bias=m_gdn_dt_bias, gdn_out_norm=m_gdn_out_norm,
                 fox_w_in=m_fox_w_in, w_out=m_w_out, mem_norm=m_mem_norm, mem_w_kv=m_mem_w_kv, kv_norm=m_kv_norm,
                 kv_w=m_kv_w, kv_b_f=m_kv_b_f, final_norm=m_final_norm)
    mom_v = dict(ffn1_norm=v_ffn1_norm, ffn1_w_gate_up=v_ffn1_w_gate_up, ffn1_w_down=v_ffn1_w_down, mix_norm=v_mix_norm,
                 ffn2_norm=v_ffn2_norm, ffn2_w_gate_up=v_ffn2_w_gate_up, ffn2_w_down=v_ffn2_w_down, gdn_w_in=v_gdn_w_in,
                 gdn_conv=v_gdn_conv, gdn_A_log=v_gdn_A_log, gdn_dt_bias=v_gdn_dt_bias, gdn_out_norm=v_gdn_out_norm,
                 fox_w_in=v_fox_w_in, w_out=v_w_out, mem_norm=v_mem_norm, mem_w_kv=v_mem_w_kv, kv_norm=v_kv_norm,
                 kv_w=v_kv_w, kv_b_f=v_kv_b_f, final_norm=v_final_norm)
    names = list(weights)
    small_names = [n for n in names if weights[n].shape == mom_m[n].shape and n in (
        "ffn1_norm", "mix_norm", "ffn2_norm", "gdn_A_log", "gdn_dt_bias", "gdn_out_norm", "mem_norm", "kv_norm",
        "kv_b_f", "final_norm")]
    big_names = [n for n in names if n not in small_names]

    h = x[0]
    target = loss_target[0]
    mem_tokens = mem[0]
    s, d = h.shape
    depth = ffn1_norm.shape[0]
    n_a = gdn_w_in.shape[0]
    n_heads, head_dim = gdn_A_log.shape[1], gdn_out_norm.shape[1]
    gw = n_heads * head_dim
    a_in = gdn_w_in.shape[2]
    mem_w = a_in - 4 * gw - 2 * n_heads
    a_in_pad = 4 * gw + mem_w + LANES
    kv_width = kv_w.shape[1]
    kv_pad = 2 * gw + LANES
    fh = ffn1_w_down.shape[1] * N_DEV

    def permute_in(w):
        ab = w[..., 4 * gw:4 * gw + 2 * n_heads]
        return jnp.concatenate([w[..., :4 * gw], w[..., 4 * gw + 2 * n_heads:], _pad_lanes(ab, LANES)], axis=-1)

    def unpermute_in(w):
        return jnp.concatenate([w[..., :4 * gw], w[..., 4 * gw + mem_w:4 * gw + mem_w + 2 * n_heads],
                                w[..., 4 * gw:4 * gw + mem_w]], axis=-1)

    gathered = _all_gather("gather_weights", [
        ffn1_w_gate_up.astype(bf16), ffn1_w_down.astype(bf16), ffn2_w_gate_up.astype(bf16), ffn2_w_down.astype(bf16),
        permute_in(gdn_w_in).astype(bf16), fox_w_in.astype(bf16), w_out.astype(bf16), mem_w_kv.astype(bf16),
        _pad_lanes(kv_w, kv_pad).astype(bf16)[None], gdn_conv])
    wgu1, wd1_s, wgu2, wd2_s, win_s, wfox_s, wout_s, wmem_s, wkv_s, conv_s = gathered
    wd1, wd2 = _rows_from_shards(wd1_s), _rows_from_shards(wd2_s)
    win, wfox, wout = _rows_from_shards(win_s), _rows_from_shards(wfox_s), _rows_from_shards(wout_s)
    wmem = _rows_from_shards(wmem_s)
    wmem_cat = wmem.transpose(1, 0, 2).reshape(d, depth * 2 * mem_w)
    wkv = _rows_from_shards(wkv_s)[0]
    conv_w = conv_s.transpose(1, 2, 0, 3).reshape(n_a, gdn_conv.shape[1], 3 * gw)

    a_log_rows = [_row(gdn_A_log[l], LANES) for l in range(n_a)]
    dt_rows = [_row(gdn_dt_bias[l], LANES) for l in range(n_a)]
    onorm_rows = [_row(gdn_out_norm[l]) for l in range(n_a)]
    b_f_row = _row(kv_b_f, LANES)

    (mem_n,) = _rowwise("mem_norm", lambda t, g: (_rms(t, g),), [mem_tokens], [_row(mem_norm)], [(d, bf16)])
    mem_kv = _matmul("mem_kv", mem_n, wmem_cat, "nn", f32)

    saved = []
    shared = None
    for l in range(depth):
        rec = {"h0": h}
        h1, gate1, up1 = _ffn_fwd(f"ffn1_fwd_{l}", h, _row(ffn1_norm[l]), wgu1, wd1, l)
        (u,) = _rowwise(f"mix_norm_{l}", lambda t, g: (_rms(t, g),), [h1], [_row(mix_norm[l])], [(d, bf16)])
        rec.update(h1=h1, u=u, ffn1=(gate1, up1))
        if l < n_a:
            proj = _matmul(f"gdn_in_{l}", u, win[l], "nn", f32)
            yc = _conv_fwd(f"conv_fwd_{l}", V(proj, cb=0, w=3 * gw), conv_w[l])
            ab_view = V(proj, cb=(4 * gw + mem_w) // LANES, w=LANES)
            q, k, v, gb = _rowwise(f"gdn_pre_{l}", functools.partial(_gdn_pre, n_heads, head_dim),
                                   [yc, ab_view], [a_log_rows[l], dt_rows[l]],
                                   [(gw, f32), (gw, f32), (gw, f32), (LANES, f32)])
            o, states = _gdn_chunk_fwd(f"gdn_chunk_fwd_{l}", q, k, v, gb, n_heads, head_dim)
            z_view = V(proj, cb=3, w=gw)
            (main,) = _rowwise(f"gdn_post_{l}", functools.partial(_gdn_post, n_heads, head_dim),
                               [o, z_view], [onorm_rows[l]], [(gw, bf16)])
            qmem_view = V(proj, cb=4 * gw // mem_w, w=mem_w)
            rec.update(proj=proj, yc=yc, q=q, k=k, v=v, gb=gb, o=o, states=states)
        else:
            proj = _matmul(f"fox_in_{l}", u, wfox[l - n_a], "nn", bf16)
            sk, sv, crow = shared["k"], shared["v"], shared["crow"]
            main, lse = _fox_fwd(f"fox_fwd_{l}", proj, sk, sv, crow, n_heads, head_dim)
            qmem_view = V(proj, cb=gw // mem_w, w=mem_w)
            rec.update(proj=proj, lse=lse)
        km = V(mem_kv, cb=2 * l, w=mem_w)
        vm = V(mem_kv, cb=2 * l + 1, w=mem_w)
        (mem_out,) = _rowwise(f"mem_attn_{l}", _mem_attn, [qmem_view], [km, vm], [(mem_w, bf16)])
        cat = jnp.concatenate([main, mem_out], axis=1)
        h2 = _matmul(f"out_proj_{l}", cat, wout[l], "nn", f32, add=h1)
        h3, gate2, up2 = _ffn_fwd(f"ffn2_fwd_{l}", h2, _row(ffn2_norm[l]), wgu2, wd2, l)
        rec.update(cat=cat, h2=h2, qmem=qmem_view, ffn2=(gate2, up2))
        saved.append(rec)
        h = h3
        if l == n_a - 1:
            (hn,) = _rowwise("kv_norm", lambda t, g: (_rms(t, g),), [h], [_row(kv_norm)], [(d, bf16)])
            p = _matmul("kv_proj", hn, wkv, "nn", f32)
            pk, pv, pf = V(p, cb=0, w=gw), V(p, cb=1, w=gw), V(p, cb=2 * gw // LANES, w=LANES)
            sk, sv, log_f = _rowwise("kv_post", functools.partial(_kv_post, n_heads), [pk, pv, pf], [b_f_row],
                                     [(gw, bf16), (gw, bf16), (LANES, f32)])
            cum = _cumsum("forget_cumsum", [log_f], reverse=False, scale=LOG2_E)
            c_heads = cum[:, :n_heads].T
            shared = dict(k=sk, v=sv, crow=c_heads.reshape(n_heads, 1, s), h=h, hn=hn, p=p, views=(pk, pv, pf))

    loss_part, dh, d_final = _final_loss("final_loss", h, target, _row(final_norm))
    loss = lax.psum(loss_part[0, 0], ("x", "y", "c"))

    grads = {}
    per_layer = {n: [None] * depth for n in ("ffn1_norm", "mix_norm", "ffn2_norm", "ffn1_gu", "ffn1_d", "ffn2_gu",
                                             "ffn2_d", "w_out")}
    per_a = {n: [None] * n_a for n in ("gdn_w_in", "gdn_conv", "gdn_A_log", "gdn_dt_bias", "gdn_out_norm")}
    per_b = {"fox_w_in": [None] * (depth - n_a)}
    d_mem_kv = [None] * depth
    fox_grads = []

    def ffn_backward(tag, l, h_in, d_out, gain, kept, wgu, wd):
        parts, dwg, dwu, dwd = _ffn_bwd(f"{tag}_bwd_{l}", h_in, d_out, _row(gain), kept[0], kept[1], wgu, wd, l)
        nh = parts.shape[0]
        (d_in,), (d_gain,) = _rowwise_vjp(
            f"{tag}_norm_bwd_{l}", lambda t, g: (_rms(t, g),), [h_in], [_row(gain)],
            [[V(parts, lead=t) for t in range(nh)]], [f32], add=d_out)
        return d_in, d_gain, jnp.concatenate([dwg, dwu], axis=0), dwd

    for l in reversed(range(depth)):
        rec = saved[l]
        if l == n_a - 1:
            dk_list = [V(g["dk"]) for g in fox_grads]
            dv_list = [V(g["dv"]) for g in fox_grads]
            dc_parts = [_pad_lanes(part.reshape(n_heads, s).T, LANES) for g in fox_grads for part in g["dc"]]
            d_log_f = _cumsum("forget_cumsum_bwd", dc_parts, reverse=True)
            pk, pv, pf = shared["views"]
            (dpk, dpv, dpf), (d_bf,) = _rowwise_vjp(
                "kv_post_bwd", functools.partial(_kv_post, n_heads), [pk, pv, pf], [b_f_row],
                [dk_list, dv_list, [d_log_f]], [bf16, bf16, bf16])
            dp = jnp.concatenate([dpk, dpv, dpf], axis=1)
            d_hn = _matmul("kv_proj_dx", dp, wkv, "nt", f32)
            grads["kv_w"] = _matmul("kv_proj_dw", shared["hn"], dp, "tn", f32)[:, :kv_width]
            (dh,), (d_kvn,) = _rowwise_vjp("kv_norm_bwd", lambda t, g: (_rms(t, g),), [shared["h"]], [_row(kv_norm)],
                                           [[d_hn]], [f32], add=dh)
            grads["kv_norm"] = d_kvn.reshape(-1)
            grads["kv_b_f"] = d_bf[0, :n_heads]

        dh2, per_layer["ffn2_norm"][l], per_layer["ffn2_gu"][l], per_layer["ffn2_d"][l] = ffn_backward(
            "ffn2", l, rec["h2"], dh, ffn2_norm[l], rec["ffn2"], wgu2, wd2)
        d_cat = _matmul(f"out_proj_dx_{l}", dh2, wout[l], "nt", f32)
        per_layer["w_out"][l] = _matmul(f"out_proj_dw_{l}", rec["cat"], dh2, "tn", f32)
        d_main = V(d_cat, cb=0, w=gw)
        d_memo = V(d_cat, cb=gw // mem_w, w=mem_w)
        km, vm = V(mem_kv, cb=2 * l, w=mem_w), V(mem_kv, cb=2 * l + 1, w=mem_w)
        (dqmem,), (dkm, dvm) = _rowwise_vjp(f"mem_attn_bwd_{l}", _mem_attn, [rec["qmem"]], [km, vm], [[d_memo]], [bf16])
        d_mem_kv[l] = jnp.concatenate([dkm, dvm], axis=1)
        if l < n_a:
            proj = rec["proj"]
            z_view = V(proj, cb=3, w=gw)
            (d_o, d_z), (d_onorm,) = _rowwise_vjp(
                f"gdn_post_bwd_{l}", functools.partial(_gdn_post, n_heads, head_dim), [rec["o"], z_view],
                [onorm_rows[l]], [[d_main]], [f32, bf16])
            dq, dk, dv, dgb = _gdn_chunk_bwd(f"gdn_chunk_bwd_{l}", rec["q"], rec["k"], rec["v"], rec["gb"],
                                             rec["states"], d_o, n_heads, head_dim)
            ab_view = V(proj, cb=(4 * gw + mem_w) // LANES, w=LANES)
            (d_yc, d_ab), (d_alog, d_dt) = _rowwise_vjp(
                f"gdn_pre_bwd_{l}", functools.partial(_gdn_pre, n_heads, head_dim), [rec["yc"], ab_view],
                [a_log_rows[l], dt_rows[l]], [[dq], [dk], [dv], [dgb]], [f32, bf16])
            d_qkv, d_conv = _conv_bwd(f"conv_bwd_{l}", V(proj, cb=0, w=3 * gw), d_yc, conv_w[l])
            d_proj = jnp.concatenate([d_qkv, d_z, dqmem, d_ab], axis=1)
            du = _matmul(f"gdn_in_dx_{l}", d_proj, win[l], "nt", f32)
            per_a["gdn_w_in"][l] = unpermute_in(_matmul(f"gdn_in_dw_{l}", rec["u"], d_proj, "tn", f32))
            per_a["gdn_conv"][l] = d_conv
            per_a["gdn_A_log"][l] = d_alog[0, :n_heads]
            per_a["gdn_dt_bias"][l] = d_dt[0, :n_heads]
            per_a["gdn_out_norm"][l] = d_onorm[0]
        else:
            proj = rec["proj"]
            sk, sv, crow = shared["k"], shared["v"], shared["crow"]
            dq, delta, dc_col = _fox_bwd_dq(f"fox_dq_{l}", proj, sk, sv, crow, rec["lse"], rec["cat"], d_cat,
                                            n_heads, head_dim)
            dk, dv, dc_row = _fox_bwd_dkv(f"fox_dkv_{l}", proj, sk, sv, crow, rec["lse"], delta, d_cat,
                                          n_heads, head_dim)
            fox_grads.append(dict(dk=dk, dv=dv, dc=(dc_row, dc_col)))
            d_proj = jnp.concatenate([dq, dqmem], axis=1)
            du = _matmul(f"fox_in_dx_{l}", d_proj, wfox[l - n_a], "nt", f32)
            per_b["fox_w_in"][l - n_a] = _matmul(f"fox_in_dw_{l}", rec["u"], d_proj, "tn", f32)
        (dh1,), (d_mix,) = _rowwise_vjp(f"mix_norm_bwd_{l}", lambda t, g: (_rms(t, g),), [rec["h1"]],
                                        [_row(mix_norm[l])], [[du]], [f32], add=dh2)
        per_layer["mix_norm"][l] = d_mix
        dh, per_layer["ffn1_norm"][l], per_layer["ffn1_gu"][l], per_layer["ffn1_d"][l] = ffn_backward(
            "ffn1", l, rec["h0"], dh1, ffn1_norm[l], rec["ffn1"], wgu1, wd1)

    grad_x = dh[None]

    d_mem_kv_cat = jnp.concatenate(d_mem_kv, axis=1)
    d_wmem_cat = _matmul("mem_kv_dw", mem_n, d_mem_kv_cat, "tn", f32)
    d_mem_n = _matmul("mem_kv_dx", d_mem_kv_cat, wmem_cat, "nt", f32)
    _, (d_memnorm,) = _rowwise_vjp("mem_norm_bwd", lambda t, g: (_rms(t, g),), [mem_tokens], [_row(mem_norm)],
                                   [[d_mem_n]], [None])

    def gu_stack(per):
        return jnp.stack(per, axis=1).astype(bf16)

    stacks = dict(
        ffn1_w_gate_up=gu_stack(per_layer["ffn1_gu"]),
        ffn1_w_down=_rows_to_shards(jnp.stack(per_layer["ffn1_d"])).astype(bf16),
        ffn2_w_gate_up=gu_stack(per_layer["ffn2_gu"]),
        ffn2_w_down=_rows_to_shards(jnp.stack(per_layer["ffn2_d"])).astype(bf16),
        gdn_w_in=_rows_to_shards(jnp.stack(per_a["gdn_w_in"])).astype(bf16),
        gdn_conv=jnp.stack(per_a["gdn_conv"]).reshape(n_a, -1, N_DEV, 3 * gw // N_DEV).transpose(2, 0, 1, 3),
        fox_w_in=_rows_to_shards(jnp.stack(per_b["fox_w_in"])).astype(bf16),
        w_out=_rows_to_shards(jnp.stack(per_layer["w_out"])).astype(bf16),
        mem_w_kv=_rows_to_shards(d_wmem_cat.reshape(d, depth, 2 * mem_w).transpose(1, 0, 2)).astype(bf16),
        kv_w=_rows_to_shards(grads["kv_w"][None])[:, 0].astype(bf16),
    )
    received = dict(zip(big_names, _reduce_exchange([stacks[n] for n in big_names])))

    small_shapes = {n: weights[n].shape for n in small_names}
    pack = _SmallPack(small_shapes)
    small_grads = dict(
        ffn1_norm=jnp.concatenate(per_layer["ffn1_norm"], axis=0), mix_norm=jnp.concatenate(per_layer["mix_norm"], axis=0),
        ffn2_norm=jnp.concatenate(per_layer["ffn2_norm"], axis=0), gdn_A_log=jnp.stack(per_a["gdn_A_log"]),
        gdn_dt_bias=jnp.stack(per_a["gdn_dt_bias"]), gdn_out_norm=jnp.stack(per_a["gdn_out_norm"]),
        mem_norm=d_memnorm.reshape(-1), kv_norm=grads["kv_norm"], kv_b_f=grads["kv_b_f"], final_norm=d_final.reshape(-1))
    (small_parts,) = _all_gather("gather_small_grads", [pack.pack(small_grads)])

    out_g, out_d, out_m, out_v = {}, {}, {}, {}
    for n in big_names:
        shape = weights[n].shape
        c = shape[-1]
        parts = received[n].reshape(N_CHIPS, -1, c)
        res = _adamw(f"adamw_{n}", parts, weights[n].reshape(-1, c), mom_m[n].reshape(-1, c), mom_v[n].reshape(-1, c))
        out_g[n], out_d[n], out_m[n], out_v[n] = [r.reshape(shape) for r in res]
    res = _adamw("adamw_small", small_parts, pack.pack({n: weights[n] for n in small_names}),
                 pack.pack({n: mom_m[n] for n in small_names}), pack.pack({n: mom_v[n] for n in small_names}))
    for dst, packed in zip((out_g, out_d, out_m, out_v), res):
        dst.update(pack.unpack(packed))

    return (loss, grad_x, *[out_g[n] for n in names], *[out_d[n] for n in names],
            *[out_m[n] for n in names], *[out_v[n] for n in names])
```

```python
import functools
import math

import jax
import jax.numpy as jnp
from jax import lax
from jax.experimental import pallas as pl
from jax.experimental.pallas import tpu as pltpu

f32 = jnp.float32
bf16 = jnp.bfloat16
SDS = jax.ShapeDtypeStruct

N_DEV = 8
MEM_HEADS = 4
CHUNK = 64
LANES = 128
EPS = 1e-6
NEG_INF = -1e30
ADAM_LR = 0.001
ADAM_B1 = 0.9
ADAM_B2 = 0.999
ADAM_EPS = 1e-08
ADAM_WD = 0.01
ADAM_STEP = 10

ROW_TILE = 512
MM_TILE = 1024
FFN_FWD_TILE = 1024
FFN_BWD_TILE = 512
ATT_TILE = 1024
ATT_HEADS = 3
CUMSUM_TILE = 256
VMEM_LIMIT = 56 * 1024 * 1024

MESH = pl.DeviceIdType.MESH


def _cp(sem=None):
    return pltpu.CompilerParams(dimension_semantics=sem, vmem_limit_bytes=VMEM_LIMIT)


_DIMS = {"nn": (((1,), (0,)), ((), ())), "nt": (((1,), (1,)), ((), ())), "tn": (((0,), (0,)), ((), ()))}


def _dg(a, b, mode):
    return lax.dot_general(a.astype(bf16), b.astype(bf16), _DIMS[mode], preferred_element_type=f32)


@functools.partial(jax.custom_vjp, nondiff_argnums=(2,))
def _mm(a, b, mode):
    return _dg(a, b, mode)


def _mm_fwd(a, b, mode):
    return _dg(a, b, mode), (a, b)


def _mm_bwd(mode, res, ct):
    a, b = res
    if mode == "nn":
        da, db = _dg(ct, b, "nt"), _dg(a, ct, "tn")
    elif mode == "nt":
        da, db = _dg(ct, b, "nn"), _dg(ct, a, "tn")
    else:
        da, db = _dg(b, ct, "nt"), _dg(a, ct, "nn")
    return da.astype(a.dtype), db.astype(b.dtype)


_mm.defvjp(_mm_fwd, _mm_bwd)


def _split_bf16(x):
    hi = x.astype(bf16)
    return hi, (x - hi.astype(f32)).astype(bf16)


def _dgh(a, b, mode="nn"):
    a_hi, a_lo = _split_bf16(a)
    b_hi, b_lo = _split_bf16(b)
    dims = _DIMS[mode]
    return (lax.dot_general(a_hi, b_hi, dims, preferred_element_type=f32)
            + lax.dot_general(a_hi, b_lo, dims, preferred_element_type=f32)
            + lax.dot_general(a_lo, b_hi, dims, preferred_element_type=f32))


@jax.custom_vjp
def _unit_lower_inverses(lows):
    c = lows[0].shape[0]
    ri = lax.broadcasted_iota(jnp.int32, (c, c), 0)
    ci = lax.broadcasted_iota(jnp.int32, (c, c), 1)
    eye = jnp.where(ri == ci, 1.0, 0.0)
    xs = [-low for low in lows]
    rs = [eye + x for x in xs]
    for _ in range(int(math.log2(c)) - 1):
        xs = [_dgh(x, x) for x in xs]
        rs = [r + _dgh(r, x) for r, x in zip(rs, xs)]
    return tuple(rs)


def _uli_fwd(lows):
    ts = _unit_lower_inverses(lows)
    return ts, ts


def _uli_bwd(ts, cts):
    mids = [_dgh(ct, t, "nt") for t, ct in zip(ts, cts)]
    return (tuple(-_dgh(t, m, "tn") for t, m in zip(ts, mids)),)


_unit_lower_inverses.defvjp(_uli_fwd, _uli_bwd)


@functools.partial(jax.custom_vjp, nondiff_argnums=(1,))
def _split_lanes(x, width):
    return tuple(x[:, i * width:(i + 1) * width] for i in range(x.shape[1] // width))


def _split_fwd(x, width):
    return _split_lanes(x, width), None


def _split_bwd(width, _, cts):
    return (jnp.concatenate(list(cts), axis=1),)


_split_lanes.defvjp(_split_fwd, _split_bwd)


def _sigmoid(x):
    return 1.0 / (1.0 + jnp.exp(-x))


def _silu(x):
    return x * _sigmoid(x)


def _softplus(x):
    return jnp.maximum(x, 0.0) + jnp.log1p(jnp.exp(-jnp.abs(x)))


def _rms(x, gain):
    return x * lax.rsqrt(jnp.mean(x * x, axis=-1, keepdims=True) + EPS) * gain


class V:
    def __init__(self, arr, lead=None, cb=0, w=None):
        self.arr, self.lead, self.cb = arr, lead, cb
        self.w = arr.shape[-1] if w is None else w

    @property
    def rows(self):
        return self.arr.shape[-2]

    def spec(self, tile, order=None):
        lead, cb, w = self.lead, self.cb, self.w
        order = order or (lambda i: i)
        if lead is None:
            return pl.BlockSpec((tile, w), lambda i: (order(i), cb))
        return pl.BlockSpec((None, tile, w), lambda i: (lead, order(i), cb))

    def const_spec(self):
        lead, cb, w, r = self.lead, self.cb, self.w, self.rows
        if lead is None:
            return pl.BlockSpec((r, w), lambda i: (0, cb))
        return pl.BlockSpec((None, r, w), lambda i: (lead, 0, cb))


def _v(a):
    return a if isinstance(a, V) else V(a)


def _rowwise(name, fn, rows, consts, outs, tile=None):
    rows = [_v(r) for r in rows]
    consts = [_v(c) for c in consts]
    s = rows[0].rows
    tile = min(tile or ROW_TILE, s)
    nr, nc = len(rows), len(consts)

    def body(*refs):
        vals = [r[...].astype(f32) for r in refs[:nr + nc]]
        res = fn(*vals)
        for o, val in zip(refs[nr + nc:], res):
            o[...] = val.astype(o.dtype)

    return pl.pallas_call(
        body, name=name, grid=(s // tile,),
        in_specs=[r.spec(tile) for r in rows] + [c.const_spec() for c in consts],
        out_specs=[pl.BlockSpec((tile, w), lambda i: (i, 0)) for w, _ in outs],
        out_shape=[SDS((s, w), dt) for w, dt in outs],
        compiler_params=_cp(("parallel",)),
    )(*[r.arr for r in rows], *[c.arr for c in consts])


def _rowwise_vjp(name, fn, rows, consts, cts, d_rows, add=None, tile=None):
    rows = [_v(r) for r in rows]
    consts = [_v(c) for c in consts]
    cts = [[_v(c) for c in group] for group in cts]
    flat_cts = [c for group in cts for c in group]
    s = rows[0].rows
    tile = min(tile or ROW_TILE, s)
    nr, nc, nt = len(rows), len(consts), len(flat_cts)
    want = [k for k, dt in enumerate(d_rows) if dt is not None]
    has_add = add is not None
    add_v = [_v(add)] if has_add else []

    def body(*refs):
        vals = [r[...].astype(f32) for r in refs[:nr + nc]]
        ct_refs = refs[nr + nc:nr + nc + nt]
        pos = nr + nc + nt
        add_ref = refs[pos] if has_add else None
        pos += 1 if has_add else 0
        drow_refs = refs[pos:pos + len(want)]
        dconst_refs = refs[pos + len(want):]
        ctv, at = [], 0
        for group in cts:
            acc = ct_refs[at][...].astype(f32)
            for r in ct_refs[at + 1:at + len(group)]:
                acc = acc + r[...].astype(f32)
            at += len(group)
            ctv.append(acc)
        _, vjp = jax.vjp(fn, *vals)
        grads = vjp(tuple(ctv))
        for o, k in zip(drow_refs, want):
            g = grads[k]
            if has_add and k == want[0]:
                g = g + add_ref[...].astype(f32)
            o[...] = g.astype(o.dtype)

        @pl.when(pl.program_id(0) == 0)
        def _():
            for o in dconst_refs:
                o[...] = jnp.zeros_like(o)

        for o, g in zip(dconst_refs, grads[nr:]):
            o[...] += g

    outs = pl.pallas_call(
        body, name=name, grid=(s // tile,),
        in_specs=[r.spec(tile) for r in rows] + [c.const_spec() for c in consts]
        + [c.spec(tile) for c in flat_cts] + [a.spec(tile) for a in add_v],
        out_specs=[pl.BlockSpec((tile, rows[k].w), lambda i: (i, 0)) for k in want]
        + [pl.BlockSpec((c.rows, c.w), lambda i: (0, 0)) for c in consts],
        out_shape=[SDS((s, rows[k].w), d_rows[k]) for k in want] + [SDS((c.rows, c.w), f32) for c in consts],
        compiler_params=_cp(("arbitrary",)),
    )(*[r.arr for r in rows], *[c.arr for c in consts], *[c.arr for c in flat_cts], *[a.arr for a in add_v])
    return list(outs[:len(want)]), list(outs[len(want):])


def _pick(n, cap):
    best = None
    for t in range(LANES, min(n, cap) + 1, LANES):
        if n % t == 0:
            best = t
    return best or n


def _matmul(name, a, b, mode, out_dtype, add=None, tn_cap=1280):
    has_add = add is not None
    if mode in ("nn", "nt"):
        m, k = a.shape
        n = b.shape[1] if mode == "nn" else b.shape[0]
        tm = min(MM_TILE, m)
        tn = _pick(n, tn_cap) if k * n * 2 > (8 << 20) else n

        def body(*refs):
            a_ref, b_ref = refs[0], refs[1]
            o_ref = refs[-1]
            acc = _dg(a_ref[...], b_ref[...], mode)
            if has_add:
                acc = acc + refs[2][...].astype(f32)
            o_ref[...] = acc.astype(o_ref.dtype)

        b_spec = pl.BlockSpec((k, tn), lambda i, j: (0, j)) if mode == "nn" else pl.BlockSpec((tn, k), lambda i, j: (j, 0))
        in_specs = [pl.BlockSpec((tm, k), lambda i, j: (i, 0)), b_spec]
        args = [a, b]
        if has_add:
            in_specs.append(pl.BlockSpec((tm, tn), lambda i, j: (i, j)))
            args.append(add)
        return pl.pallas_call(
            body, name=name, grid=(m // tm, n // tn), in_specs=in_specs,
            out_specs=pl.BlockSpec((tm, tn), lambda i, j: (i, j)),
            out_shape=SDS((m, n), out_dtype), compiler_params=_cp(("parallel", "parallel")),
        )(*args)
    kk, m = a.shape
    n = b.shape[1]
    tk = min(MM_TILE, kk)
    tn = _pick(n, tn_cap) if m * n * 4 > (7 << 20) else n

    def body_tn(a_ref, b_ref, o_ref):
        @pl.when(pl.program_id(1) == 0)
        def _():
            o_ref[...] = jnp.zeros_like(o_ref)

        o_ref[...] += _dg(a_ref[...], b_ref[...], "tn")

    return pl.pallas_call(
        body_tn, name=name, grid=(n // tn, kk // tk),
        in_specs=[pl.BlockSpec((tk, m), lambda j, k: (k, 0)), pl.BlockSpec((tk, tn), lambda j, k: (k, j))],
        out_specs=pl.BlockSpec((m, tn), lambda j, k: (0, j)),
        out_shape=SDS((m, n), f32), compiler_params=_cp(("parallel", "arbitrary")),
    )(a, b)


def _ffn_fwd(name, h, gain, wgu, wd, layer):
    s, d = h.shape
    hs = wgu.shape[3]
    nh = wgu.shape[0] // 2
    tm = min(FFN_FWD_TILE, s)

    def body(h_ref, g_ref, wg_ref, wu_ref, wd_ref, o_ref, gate_ref, up_ref, n_scr, acc_scr):
        t = pl.program_id(1)

        @pl.when(t == 0)
        def _():
            n_scr[...] = _rms(h_ref[...], g_ref[...]).astype(bf16)
            acc_scr[...] = jnp.zeros_like(acc_scr)

        n = n_scr[...]
        gate = _dg(n, wg_ref[...], "nn")
        up = _dg(n, wu_ref[...], "nn")
        gate_ref[...] = gate.astype(gate_ref.dtype)
        up_ref[...] = up.astype(up_ref.dtype)
        acc_scr[...] += _dg(_silu(gate) * up, wd_ref[...], "nn")

        @pl.when(t == nh - 1)
        def _():
            o_ref[...] = h_ref[...] + 0.5 * acc_scr[...]

    saved_spec = pl.BlockSpec((None, tm, hs), lambda i, t: (t, i, 0))
    return pl.pallas_call(
        body, name=name, grid=(s // tm, nh),
        in_specs=[
            pl.BlockSpec((tm, d), lambda i, t: (i, 0)),
            pl.BlockSpec((1, d), lambda i, t: (0, 0)),
            pl.BlockSpec((None, None, d, hs), lambda i, t: (t, layer, 0, 0)),
            pl.BlockSpec((None, None, d, hs), lambda i, t: (t + nh, layer, 0, 0)),
            pl.BlockSpec((None, hs, d), lambda i, t: (layer, t, 0)),
        ],
        out_specs=[pl.BlockSpec((tm, d), lambda i, t: (i, 0)), saved_spec, saved_spec],
        out_shape=[SDS((s, d), f32), SDS((nh, s, hs), bf16), SDS((nh, s, hs), bf16)],
        scratch_shapes=[pltpu.VMEM((tm, d), bf16), pltpu.VMEM((tm, d), f32)],
        compiler_params=_cp(("parallel", "arbitrary")),
    )(h, gain, wgu, wgu, wd)


def _ffn_bwd(name, h, dout, gain, gate_s, up_s, wgu, wd, layer):
    s, d = h.shape
    hs = wgu.shape[3]
    nh = wgu.shape[0] // 2
    tm = min(FFN_BWD_TILE, s)

    def body(h_ref, do_ref, g_ref, gate_ref, up_ref, wg_ref, wu_ref, wd_ref, dn_ref, dwg_ref, dwu_ref, dwd_ref):
        @pl.when(pl.program_id(1) == 0)
        def _():
            dwg_ref[...] = jnp.zeros_like(dwg_ref)
            dwu_ref[...] = jnp.zeros_like(dwu_ref)
            dwd_ref[...] = jnp.zeros_like(dwd_ref)

        n = _rms(h_ref[...], g_ref[...]).astype(bf16)
        wg, wu, wdn = wg_ref[...], wu_ref[...], wd_ref[...]
        gate = gate_ref[...].astype(f32)
        up = up_ref[...].astype(f32)
        sg = _sigmoid(gate)
        act = gate * sg
        dy = (0.5 * do_ref[...]).astype(bf16)
        da = _dg(dy, wdn, "nt")
        dup = (da * act).astype(bf16)
        dgate = (da * up * (sg * (1.0 + gate * (1.0 - sg)))).astype(bf16)
        dwd_ref[...] += _dg(act * up, dy, "tn")
        dwg_ref[...] += _dg(n, dgate, "tn")
        dwu_ref[...] += _dg(n, dup, "tn")
        dn_ref[...] = (_dg(dgate, wg, "nt") + _dg(dup, wu, "nt")).astype(dn_ref.dtype)

    return pl.pallas_call(
        body, name=name, grid=(nh, s // tm),
        in_specs=[
            pl.BlockSpec((tm, d), lambda t, i: (i, 0)),
            pl.BlockSpec((tm, d), lambda t, i: (i, 0)),
            pl.BlockSpec((1, d), lambda t, i: (0, 0)),
            pl.BlockSpec((None, tm, hs), lambda t, i: (t, i, 0)),
            pl.BlockSpec((None, tm, hs), lambda t, i: (t, i, 0)),
            pl.BlockSpec((None, None, d, hs), lambda t, i: (t, layer, 0, 0)),
            pl.BlockSpec((None, None, d, hs), lambda t, i: (t + nh, layer, 0, 0)),
            pl.BlockSpec((None, hs, d), lambda t, i: (layer, t, 0)),
        ],
        out_specs=[
            pl.BlockSpec((None, tm, d), lambda t, i: (t, i, 0)),
            pl.BlockSpec((None, d, hs), lambda t, i: (t, 0, 0)),
            pl.BlockSpec((None, d, hs), lambda t, i: (t, 0, 0)),
            pl.BlockSpec((hs, d), lambda t, i: (t, 0)),
        ],
        out_shape=[SDS((nh, s, d), bf16), SDS((nh, d, hs), f32), SDS((nh, d, hs), f32), SDS((nh * hs, d), f32)],
        compiler_params=_cp(("parallel", "arbitrary")),
    )(h, dout, gain, gate_s, up_s, wgu, wgu, wd)


def _conv_fwd(name, x, w):
    x = _v(x)
    s, c = x.rows, x.w
    cw = w.shape[0]
    tile = min(ROW_TILE, s)
    cb = x.cb

    def body(x_ref, halo_ref, w_ref, o_ref, buf):
        first = pl.program_id(0) == 0
        buf[0:8, :] = jnp.where(first, 0.0, halo_ref[...])
        buf[8:8 + tile, :] = x_ref[...]
        acc = w_ref[0:1, :] * buf[pl.ds(8 - cw + 1, tile), :]
        for j in range(1, cw):
            acc = acc + w_ref[j:j + 1, :] * buf[pl.ds(8 - cw + 1 + j, tile), :]
        o_ref[...] = acc

    return pl.pallas_call(
        body, name=name, grid=(s // tile,),
        in_specs=[
            pl.BlockSpec((tile, c), lambda i: (i, cb)),
            pl.BlockSpec((8, c), lambda i: (jnp.maximum(i * (tile // 8) - 1, 0), cb)),
            pl.BlockSpec((cw, c), lambda i: (0, 0)),
        ],
        out_specs=pl.BlockSpec((tile, c), lambda i: (i, 0)),
        out_shape=SDS((s, c), f32),
        scratch_shapes=[pltpu.VMEM((tile + 8, c), f32)],
        compiler_params=_cp(("parallel",)),
    )(x.arr, x.arr, w)


def _conv_bwd(name, x, dy, w):
    x = _v(x)
    s, c = x.rows, x.w
    cw = w.shape[0]
    tile = min(ROW_TILE, s)
    n_tiles = s // tile
    cb = x.cb

    def body(x_ref, xh_ref, dy_ref, dyh_ref, w_ref, dx_ref, dw_ref, xbuf, dbuf):
        i = pl.program_id(0)
        xbuf[0:8, :] = jnp.where(i == 0, 0.0, xh_ref[...])
        xbuf[8:8 + tile, :] = x_ref[...]
        dyv = dy_ref[...]
        dbuf[0:tile, :] = dyv
        dbuf[tile:tile + 8, :] = jnp.where(i == n_tiles - 1, 0.0, dyh_ref[...])

        @pl.when(i == 0)
        def _():
            dw_ref[...] = jnp.zeros_like(dw_ref)

        acc = w_ref[0:1, :] * dbuf[pl.ds(cw - 1, tile), :]
        for j in range(1, cw):
            acc = acc + w_ref[j:j + 1, :] * dbuf[pl.ds(cw - 1 - j, tile), :]
        dx_ref[...] = acc.astype(dx_ref.dtype)
        for j in range(cw):
            dw_ref[j:j + 1, :] += jnp.sum(xbuf[pl.ds(8 - cw + 1 + j, tile), :] * dyv, axis=0, keepdims=True)

    return pl.pallas_call(
        body, name=name, grid=(n_tiles,),
        in_specs=[
            pl.BlockSpec((tile, c), lambda i: (i, cb)),
            pl.BlockSpec((8, c), lambda i: (jnp.maximum(i * (tile // 8) - 1, 0), cb)),
            pl.BlockSpec((tile, c), lambda i: (i, 0)),
            pl.BlockSpec((8, c), lambda i: (jnp.minimum((i + 1) * (tile // 8), s // 8 - 1), 0)),
            pl.BlockSpec((cw, c), lambda i: (0, 0)),
        ],
        out_specs=[pl.BlockSpec((tile, c), lambda i: (i, 0)), pl.BlockSpec((cw, c), lambda i: (0, 0))],
        out_shape=[SDS((s, c), bf16), SDS((cw, c), f32)],
        scratch_shapes=[pltpu.VMEM((tile + 8, c), f32), pltpu.VMEM((tile + 8, c), f32)],
        compiler_params=_cp(("arbitrary",)),
    )(x.arr, x.arr, dy, dy, w)


def _gdn_pre(n_heads, head_dim, yc, ab, a_log, dt_bias):
    gw = n_heads * head_dim
    act = _silu(yc)
    parts = _split_lanes(act, head_dim)
    qs = [p * lax.rsqrt(jnp.sum(p * p, axis=-1, keepdims=True) + EPS) * (head_dim ** -0.5) for p in parts[:n_heads]]
    ks = [p * lax.rsqrt(jnp.sum(p * p, axis=-1, keepdims=True) + EPS) for p in parts[n_heads:2 * n_heads]]
    lane = lax.broadcasted_iota(jnp.int32, ab.shape, 1)
    g = -jnp.exp(a_log) * _softplus(ab + dt_bias)
    gb = jnp.where(lane < n_heads, g, jnp.where(lane < 2 * n_heads, _sigmoid(ab), 0.0))
    del gw
    return (jnp.concatenate(qs, axis=1), jnp.concatenate(ks, axis=1),
            jnp.concatenate(list(parts[2 * n_heads:]), axis=1), gb)


def _gdn_post(n_heads, head_dim, o, z, out_norm):
    parts = _split_lanes(o, head_dim)
    normed = jnp.concatenate([_rms(p, out_norm) for p in parts], axis=1)
    return (normed * _silu(z),)


def _gdn_chunk(n_heads, head_dim, q, k, v, gb, *states):
    c = q.shape[0]
    ri = lax.broadcasted_iota(jnp.int32, (c, c), 0)
    ci = lax.broadcasted_iota(jnp.int32, (c, c), 1)
    incl, strict, diag = ri >= ci, ri > ci, ri == ci
    lane = lax.broadcasted_iota(jnp.int32, gb.shape, 1)
    qs, ks, vs = _split_lanes(q, head_dim), _split_lanes(k, head_dim), _split_lanes(v, head_dim)
    heads = range(n_heads)
    g = [jnp.sum(jnp.where(lane == h, gb, 0.0), axis=1, keepdims=True) for h in heads]
    beta = [jnp.sum(jnp.where(lane == n_heads + h, gb, 0.0), axis=1, keepdims=True) for h in heads]
    g_row = [jnp.sum(jnp.where(diag, g[h], 0.0), axis=0, keepdims=True) for h in heads]
    cg_col = [jnp.sum(jnp.where(incl, g_row[h], 0.0), axis=1, keepdims=True) for h in heads]
    cg_row = [jnp.sum(jnp.where(ri <= ci, g[h], 0.0), axis=0, keepdims=True) for h in heads]
    g_last = [jnp.sum(g[h], axis=0, keepdims=True) for h in heads]
    decay = [jnp.where(incl, jnp.exp(jnp.where(incl, cg_col[h] - cg_row[h], 0.0)), 0.0) for h in heads]
    kb = [ks[h] * beta[h] for h in heads]
    lower = [jnp.where(strict, _mm(kb[h], ks[h], "nt") * decay[h], 0.0) for h in heads]
    eye = jnp.where(diag, 1.0, 0.0)
    off_diag = [t - eye for t in _unit_lower_inverses(tuple(lower))]
    e_col = [jnp.exp(cg_col[h]) for h in heads]
    vb = [vs[h] * beta[h] for h in heads]
    kbg = [kb[h] * e_col[h] for h in heads]
    u = [vb[h] + _mm(off_diag[h], vb[h], "nn") for h in heads]
    w = [kbg[h] + _mm(off_diag[h], kbg[h], "nn") for h in heads]
    qk = [jnp.where(incl, _mm(qs[h], ks[h], "nt") * decay[h], 0.0) for h in heads]
    v_new = [u[h] - _mm(w[h], states[h], "nn") for h in heads]
    inter = [_mm(qs[h] * e_col[h], states[h], "nn") for h in heads]
    outs = [inter[h] + _mm(qk[h], v_new[h], "nn") for h in heads]
    k_tail = [ks[h] * jnp.exp(g_last[h] - cg_col[h]) for h in heads]
    new_states = [states[h] * jnp.exp(g_last[h]) + _mm(k_tail[h], v_new[h], "tn") for h in heads]
    return (jnp.concatenate(outs, axis=1), *new_states)


def _gdn_chunk_fwd(name, q, k, v, gb, n_heads, head_dim):
    s, gw = q.shape
    n = s // CHUNK
    fn = functools.partial(_gdn_chunk, n_heads, head_dim)

    def body(q_ref, k_ref, v_ref, gb_ref, o_ref, st_ref, st_scr):
        @pl.when(pl.program_id(0) == 0)
        def _():
            st_scr[...] = jnp.zeros_like(st_scr)

        st_ref[...] = st_scr[...]
        res = fn(q_ref[...], k_ref[...], v_ref[...], gb_ref[...], *[st_scr[h] for h in range(n_heads)])
        o_ref[...] = res[0]
        for h in range(n_heads):
            st_scr[h] = res[1 + h]

    row = lambda w: pl.BlockSpec((CHUNK, w), lambda i: (i, 0))
    return pl.pallas_call(
        body, name=name, grid=(n,),
        in_specs=[row(gw), row(gw), row(gw), row(LANES)],
        out_specs=[row(gw), pl.BlockSpec((None, n_heads, head_dim, head_dim), lambda i: (i, 0, 0, 0))],
        out_shape=[SDS((s, gw), f32), SDS((n, n_heads, head_dim, head_dim), f32)],
        scratch_shapes=[pltpu.VMEM((n_heads, head_dim, head_dim), f32)],
        compiler_params=_cp(("arbitrary",)),
    )(q, k, v, gb)


def _gdn_chunk_bwd(name, q, k, v, gb, states, d_out, n_heads, head_dim):
    s, gw = q.shape
    n = s // CHUNK
    fn = functools.partial(_gdn_chunk, n_heads, head_dim)

    def body(q_ref, k_ref, v_ref, gb_ref, st_ref, do_ref, dq_ref, dk_ref, dv_ref, dgb_ref, dst_scr):
        @pl.when(pl.program_id(0) == 0)
        def _():
            dst_scr[...] = jnp.zeros_like(dst_scr)

        _, vjp = jax.vjp(fn, q_ref[...], k_ref[...], v_ref[...], gb_ref[...], *[st_ref[h] for h in range(n_heads)])
        grads = vjp((do_ref[...].astype(f32), *[dst_scr[h] for h in range(n_heads)]))
        dq_ref[...] = grads[0]
        dk_ref[...] = grads[1]
        dv_ref[...] = grads[2]
        dgb_ref[...] = grads[3]
        for h in range(n_heads):
            dst_scr[h] = grads[4 + h]

    row = lambda w: pl.BlockSpec((CHUNK, w), lambda i: (n - 1 - i, 0))
    return pl.pallas_call(
        body, name=name, grid=(n,),
        in_specs=[row(gw), row(gw), row(gw), row(LANES),
                  pl.BlockSpec((None, n_heads, head_dim, head_dim), lambda i: (n - 1 - i, 0, 0, 0)), row(gw)],
        out_specs=[row(gw), row(gw), row(gw), row(LANES)],
        out_shape=[SDS((s, gw), f32), SDS((s, gw), f32), SDS((s, gw), f32), SDS((s, LANES), f32)],
        scratch_shapes=[pltpu.VMEM((n_heads, head_dim, head_dim), f32)],
        compiler_params=_cp(("arbitrary",)),
    )(q, k, v, gb, states, d_out)


def _mem_attn(qm, km, vm):
    width = qm.shape[1]
    hd = width // MEM_HEADS
    lane = lax.broadcasted_iota(jnp.int32, (1, width), 1)
    out = jnp.zeros_like(qm)
    for h in range(MEM_HEADS):
        msk = jnp.where((lane >= h * hd) & (lane < (h + 1) * hd), 1.0, 0.0)
        logits = _mm(qm * msk, km, "nt") * (hd ** -0.5)
        p = jnp.exp(logits - jnp.max(logits, axis=-1, keepdims=True))
        p = p / jnp.sum(p, axis=-1, keepdims=True)
        out = out + _mm(p, vm, "nn") * msk
    return (out,)


def _kv_post(n_heads, pk, pv, pf, b_f):
    lane = lax.broadcasted_iota(jnp.int32, pf.shape, 1)
    log_f = jnp.where(lane < n_heads, -_softplus(-(pf + b_f)), 0.0)
    return pk, pv, log_f


def _cumsum(name, xs, reverse, scale=None):
    s, w = xs[0].shape
    tile = min(CUMSUM_TILE, s)
    n = s // tile

    def body(*refs):
        x_refs, o_ref, carry = refs[:-2], refs[-2], refs[-1]

        @pl.when(pl.program_id(0) == 0)
        def _():
            carry[...] = jnp.zeros_like(carry)

        xv = x_refs[0][...]
        for r in x_refs[1:]:
            xv = xv + r[...]
        if scale is not None:
            xv = xv * scale
        ri = lax.broadcasted_iota(jnp.int32, (tile, tile), 0)
        ci = lax.broadcasted_iota(jnp.int32, (tile, tile), 1)
        tri = jnp.where((ri <= ci) if reverse else (ri >= ci), 1.0, 0.0).astype(bf16)
        x1 = xv.astype(bf16)
        r1 = xv - x1.astype(f32)
        x2 = r1.astype(bf16)
        x3 = (r1 - x2.astype(f32)).astype(bf16)
        acc = carry[...] + _dg(tri, x1, "nn") + _dg(tri, x2, "nn") + _dg(tri, x3, "nn")
        o_ref[...] = acc
        carry[...] += jnp.sum(xv, axis=0, keepdims=True)

    order = (lambda i: (n - 1 - i, 0)) if reverse else (lambda i: (i, 0))
    return pl.pallas_call(
        body, name=name, grid=(n,),
        in_specs=[pl.BlockSpec((tile, w), order)] * len(xs), out_specs=pl.BlockSpec((tile, w), order),
        out_shape=SDS((s, w), f32), scratch_shapes=[pltpu.VMEM((1, w), f32)],
        compiler_params=_cp(("arbitrary",)),
    )(*xs)


LOG2_E = math.log2(math.e)


def _fox_logits(q_ref, k_ref, cr_ref, hh, head_dim, scale, diagonal):
    sl = slice(hh * head_dim, (hh + 1) * head_dim)
    s = _dg(q_ref[:, sl], k_ref[:, sl], "nt") * (scale * LOG2_E) - cr_ref[hh]
    if diagonal:
        tq, tk = s.shape
        ok = lax.broadcasted_iota(jnp.int32, (tq, tk), 1) <= lax.broadcasted_iota(jnp.int32, (tq, tk), 0)
        s = jnp.where(ok, s, NEG_INF)
    return s, sl


def _on_causal_tiles(i, j, fn):
    @pl.when(j < i)
    def _():
        fn(False)

    @pl.when(j == i)
    def _():
        fn(True)


def _fox_fwd(name, q, k, v, crow, n_heads, head_dim):
    s = k.shape[0]
    tq = tk = min(ATT_TILE, s)
    nq, nk = s // tq, s // tk
    scale = head_dim ** -0.5
    hpb = ATT_HEADS
    wb = hpb * head_dim

    def body(q_ref, k_ref, v_ref, cr_ref, o_ref, lse_ref, m_scr, l_scr, acc_scr):
        i, j = pl.program_id(1), pl.program_id(2)

        @pl.when(j == 0)
        def _():
            m_scr[...] = jnp.full_like(m_scr, NEG_INF)
            l_scr[...] = jnp.zeros_like(l_scr)
            acc_scr[...] = jnp.zeros_like(acc_scr)

        def step(diagonal):
            heads = range(hpb)
            sl = [slice(hh * head_dim, (hh + 1) * head_dim) for hh in heads]
            sc = [_fox_logits(q_ref, k_ref, cr_ref, hh, head_dim, scale, diagonal)[0] for hh in heads]
            m_old = [m_scr[hh] for hh in heads]
            m_new = [jnp.maximum(m_old[hh], jnp.max(sc[hh], axis=-1, keepdims=True)) for hh in heads]
            p = [jnp.exp2(sc[hh] - m_new[hh]) for hh in heads]
            alpha = [jnp.exp2(m_old[hh] - m_new[hh]) for hh in heads]
            pv = [_dg(p[hh], v_ref[:, sl[hh]], "nn") for hh in heads]
            for hh in heads:
                l_scr[hh] = alpha[hh] * l_scr[hh] + jnp.sum(p[hh], axis=-1, keepdims=True)
                acc_scr[:, sl[hh]] = alpha[hh] * acc_scr[:, sl[hh]] + pv[hh]
                m_scr[hh] = m_new[hh]

        _on_causal_tiles(i, j, step)

        @pl.when(j == nk - 1)
        def _():
            for hh in range(hpb):
                sl = slice(hh * head_dim, (hh + 1) * head_dim)
                o_ref[:, sl] = (acc_scr[:, sl] / l_scr[hh]).astype(o_ref.dtype)
                lse_ref[hh] = m_scr[hh] + jnp.log2(l_scr[hh])

    return pl.pallas_call(
        body, name=name, grid=(n_heads // hpb, nq, nk),
        in_specs=[
            pl.BlockSpec((tq, wb), lambda g, i, j: (i, g)),
            pl.BlockSpec((tk, wb), lambda g, i, j: (jnp.minimum(j, i), g)),
            pl.BlockSpec((tk, wb), lambda g, i, j: (jnp.minimum(j, i), g)),
            pl.BlockSpec((hpb, 1, tk), lambda g, i, j: (g, 0, jnp.minimum(j, i))),
        ],
        out_specs=[pl.BlockSpec((tq, wb), lambda g, i, j: (i, g)),
                   pl.BlockSpec((hpb, tq, 1), lambda g, i, j: (g, i, 0))],
        out_shape=[SDS((s, n_heads * head_dim), bf16), SDS((n_heads, s, 1), f32)],
        scratch_shapes=[pltpu.VMEM((hpb, tq, 1), f32), pltpu.VMEM((hpb, tq, 1), f32), pltpu.VMEM((tq, wb), f32)],
        compiler_params=_cp(("parallel", "parallel", "arbitrary")),
    )(q, k, v, crow)


def _fox_probs(q_ref, k_ref, v_ref, cr_ref, lse_ref, do_ref, hh, head_dim, scale, diagonal):
    sc, sl = _fox_logits(q_ref, k_ref, cr_ref, hh, head_dim, scale, diagonal)
    return jnp.exp2(sc - lse_ref[hh]), _dg(do_ref[:, sl], v_ref[:, sl], "nt"), sl


def _fox_bwd_dq(name, q, k, v, crow, lse, o, do, n_heads, head_dim):
    s = k.shape[0]
    tq = tk = min(ATT_TILE, s)
    nq, nk = s // tq, s // tk
    scale = head_dim ** -0.5
    hpb = ATT_HEADS
    wb = hpb * head_dim

    def body(q_ref, k_ref, v_ref, cr_ref, lse_ref, o_ref, do_ref, dq_ref, delta_ref, dcc_ref, acc_scr):
        i, j = pl.program_id(1), pl.program_id(2)
        heads = range(hpb)

        @pl.when(j == 0)
        def _():
            prod = o_ref[...].astype(f32) * do_ref[...].astype(f32)
            for hh in heads:
                delta_ref[hh] = jnp.sum(prod[:, hh * head_dim:(hh + 1) * head_dim], axis=-1, keepdims=True)
            dcc_ref[...] = jnp.zeros_like(dcc_ref)
            acc_scr[...] = jnp.zeros_like(acc_scr)

        def step(diagonal):
            pd = [_fox_probs(q_ref, k_ref, v_ref, cr_ref, lse_ref, do_ref, hh, head_dim, scale, diagonal) for hh in heads]
            ds = [pd[hh][0] * (pd[hh][1] - delta_ref[hh]) for hh in heads]
            dqs = [_dg(ds[hh], k_ref[:, pd[hh][2]], "nn") for hh in heads]
            for hh in heads:
                dcc_ref[hh] += jnp.sum(ds[hh], axis=-1, keepdims=True)
                acc_scr[:, pd[hh][2]] += dqs[hh]

        _on_causal_tiles(i, j, step)

        @pl.when(j == nk - 1)
        def _():
            dq_ref[...] = (acc_scr[...] * scale).astype(dq_ref.dtype)

    qspec = pl.BlockSpec((tq, wb), lambda g, i, j: (i, g))
    kspec = pl.BlockSpec((tk, wb), lambda g, i, j: (jnp.minimum(j, i), g))
    cspec = pl.BlockSpec((hpb, tq, 1), lambda g, i, j: (g, i, 0))
    return pl.pallas_call(
        body, name=name, grid=(n_heads // hpb, nq, nk),
        in_specs=[qspec, kspec, kspec,
                  pl.BlockSpec((hpb, 1, tk), lambda g, i, j: (g, 0, jnp.minimum(j, i))), cspec, qspec, qspec],
        out_specs=[qspec, cspec, cspec],
        out_shape=[SDS((s, n_heads * head_dim), bf16), SDS((n_heads, s, 1), f32), SDS((n_heads, s, 1), f32)],
        scratch_shapes=[pltpu.VMEM((tq, wb), f32)],
        compiler_params=_cp(("parallel", "parallel", "arbitrary")),
    )(q, k, v, crow, lse, o, do)


def _fox_bwd_dkv(name, q, k, v, crow, lse, delta, do, n_heads, head_dim):
    s = k.shape[0]
    tq = tk = min(ATT_TILE, s)
    nq, nk = s // tq, s // tk
    scale = head_dim ** -0.5
    hpb = ATT_HEADS
    wb = hpb * head_dim

    def body(q_ref, k_ref, v_ref, cr_ref, lse_ref, delta_ref, do_ref, dk_ref, dv_ref, dc_ref):
        j, i = pl.program_id(1), pl.program_id(2)

        @pl.when(i == 0)
        def _():
            dk_ref[...] = jnp.zeros_like(dk_ref)
            dv_ref[...] = jnp.zeros_like(dv_ref)
            dc_ref[...] = jnp.zeros_like(dc_ref)

        def step(diagonal):
            heads = range(hpb)
            pd = [_fox_probs(q_ref, k_ref, v_ref, cr_ref, lse_ref, do_ref, hh, head_dim, scale, diagonal) for hh in heads]
            ds = [pd[hh][0] * (pd[hh][1] - delta_ref[hh]) for hh in heads]
            dvs = [_dg(pd[hh][0], do_ref[:, pd[hh][2]], "tn") for hh in heads]
            dks = [_dg(ds[hh], q_ref[:, pd[hh][2]], "tn") for hh in heads]
            for hh in heads:
                sl = pd[hh][2]
                dv_ref[:, sl] += dvs[hh]
                dk_ref[:, sl] += dks[hh] * scale
                dc_ref[hh] -= jnp.sum(ds[hh], axis=0, keepdims=True)

        _on_causal_tiles(i, j, step)

    qspec = pl.BlockSpec((tq, wb), lambda g, j, i: (jnp.maximum(i, j), g))
    kspec = pl.BlockSpec((tk, wb), lambda g, j, i: (j, g))
    cspec = pl.BlockSpec((hpb, tq, 1), lambda g, j, i: (g, jnp.maximum(i, j), 0))
    rspec = pl.BlockSpec((hpb, 1, tk), lambda g, j, i: (g, 0, j))
    return pl.pallas_call(
        body, name=name, grid=(n_heads // hpb, nk, nq),
        in_specs=[qspec, kspec, kspec, rspec, cspec, cspec, qspec],
        out_specs=[kspec, kspec, rspec],
        out_shape=[SDS((s, n_heads * head_dim), f32), SDS((s, n_heads * head_dim), f32), SDS((n_heads, 1, s), f32)],
        compiler_params=_cp(("parallel", "parallel", "arbitrary")),
    )(q, k, v, crow, lse, delta, do)


def _final_loss(name, h, target, gain):
    s, d = h.shape
    tile = min(ROW_TILE, s)

    def body(h_ref, t_ref, g_ref, loss_ref, dh_ref, dg_ref):
        @pl.when(pl.program_id(0) == 0)
        def _():
            loss_ref[...] = jnp.zeros_like(loss_ref)
            dg_ref[...] = jnp.zeros_like(dg_ref)

        x, g = h_ref[...], g_ref[...]
        rstd = lax.rsqrt(jnp.mean(x * x, axis=-1, keepdims=True) + EPS)
        xhat = x * rstd
        err = xhat * g - t_ref[...]
        row = jnp.sum(err * err, axis=-1, keepdims=True) * (0.5 / d)
        loss_ref[...] += jnp.sum(row, axis=0, keepdims=True)
        dy = err * (1.0 / d)
        dg_ref[...] += jnp.sum(dy * xhat, axis=0, keepdims=True)
        dxhat = dy * g
        dh_ref[...] = rstd * (dxhat - xhat * jnp.mean(dxhat * xhat, axis=-1, keepdims=True))

    return pl.pallas_call(
        body, name=name, grid=(s // tile,),
        in_specs=[pl.BlockSpec((tile, d), lambda i: (i, 0)), pl.BlockSpec((tile, d), lambda i: (i, 0)),
                  pl.BlockSpec((1, d), lambda i: (0, 0))],
        out_specs=[pl.BlockSpec((1, LANES), lambda i: (0, 0)), pl.BlockSpec((tile, d), lambda i: (i, 0)),
                   pl.BlockSpec((1, d), lambda i: (0, 0))],
        out_shape=[SDS((1, LANES), f32), SDS((s, d), f32), SDS((1, d), f32)],
        compiler_params=_cp(("arbitrary",)),
    )(h, target, gain)


def _adamw(name, parts, w, m, v):
    r, c = w.shape
    rb = r
    for cand in (256, 128, 64, 32, 16):
        if r % cand == 0 and cand * c * 4 <= (2 << 20):
            rb = cand
            break
    n_parts = parts.shape[0]

    def body(p_ref, w_ref, m_ref, v_ref, g_out, d_out, m_out, v_out):
        g = p_ref[0].astype(f32)
        for k in range(1, n_parts):
            g = g + p_ref[k].astype(f32)
        m_new = ADAM_B1 * m_ref[...] + (1.0 - ADAM_B1) * g
        v_new = ADAM_B2 * v_ref[...] + (1.0 - ADAM_B2) * (g * g)
        m_hat = m_new / (1.0 - ADAM_B1 ** ADAM_STEP)
        v_hat = v_new / (1.0 - ADAM_B2 ** ADAM_STEP)
        g_out[...] = g
        d_out[...] = -ADAM_LR * (m_hat / (jnp.sqrt(v_hat) + ADAM_EPS) + ADAM_WD * w_ref[...])
        m_out[...] = m_new
        v_out[...] = v_new

    blk = pl.BlockSpec((rb, c), lambda i: (i, 0))
    return pl.pallas_call(
        body, name=name, grid=(r // rb,),
        in_specs=[pl.BlockSpec((n_parts, rb, c), lambda i: (0, i, 0)), blk, blk, blk],
        out_specs=[blk, blk, blk, blk],
        out_shape=[SDS((r, c), f32)] * 4,
        compiler_params=_cp(("parallel",)),
    )(parts, w, m, v)


def _position():
    x, y, c = lax.axis_index("x"), lax.axis_index("y"), lax.axis_index("c")
    return x, y, c


def _all_gather(name, shards):
    n = len(shards)

    def body(*refs):
        ins, outs = refs[:n], refs[n:2 * n]
        send_sems, recv_sems, local_sems = refs[2 * n:]
        x, y, c = _position()
        me, sibling = (x, y, c), (x, y, 1 - c)
        chips = [(1 - x, y), (x, 1 - y), (1 - x, 1 - y)]

        def slot(a, block):
            px, py, pc = block
            return outs[a].at[4 * px + 2 * py + pc]

        def copy(a, k, block, to, src=None):
            return pltpu.make_async_remote_copy(
                src_ref=slot(a, block) if src is None else src, dst_ref=slot(a, block),
                send_sem=send_sems.at[a, k], recv_sem=recv_sems.at[a, k], device_id=to, device_id_type=MESH)

        local = [pltpu.make_async_copy(ins[a], slot(a, me), local_sems.at[a]) for a in range(n)]
        for cp in local:
            cp.start()
        started = []
        for a in range(n):
            first = [copy(a, 0, me, sibling, src=ins[a])]
            first += [copy(a, 1 + j, me, (*chip, c), src=ins[a]) for j, chip in enumerate(chips)]
            for cp in first:
                cp.start()
            started += first
        for a in range(n):
            for j, chip in enumerate(chips):
                copy(a, 1 + j, (*chip, c), me).wait_recv()
                passed = copy(a, 4 + j, (*chip, c), sibling)
                passed.start()
                started.append(passed)
        for a in range(n):
            copy(a, 0, sibling, me).wait_recv()
            for j, chip in enumerate(chips):
                copy(a, 4 + j, (*chip, 1 - c), me).wait_recv()
        for cp in started:
            cp.wait_send()
        for cp in local:
            cp.wait()

    any_spec = pl.BlockSpec(memory_space=pl.ANY)
    outs = pl.pallas_call(
        body, name=name,
        in_specs=[any_spec] * n, out_specs=[any_spec] * n,
        out_shape=[SDS((N_DEV, *a.shape), a.dtype) for a in shards],
        scratch_shapes=[pltpu.SemaphoreType.DMA((n, 7)), pltpu.SemaphoreType.DMA((n, 7)), pltpu.SemaphoreType.DMA((n,))],
    )(*shards)
    return list(outs)


N_CHIPS = N_DEV // 2


def _sibling_exchange(name, arrs):
    n = len(arrs)

    def body(*refs):
        ins, outs = refs[:n], refs[n:2 * n]
        send_sems, recv_sems = refs[2 * n:]
        x, y, c = _position()
        copies = [pltpu.make_async_remote_copy(
            src_ref=ins[a], dst_ref=outs[a], send_sem=send_sems.at[a], recv_sem=recv_sems.at[a],
            device_id=(x, y, 1 - c), device_id_type=MESH) for a in range(n)]
        for cp in copies:
            cp.start()
        for cp in copies:
            cp.wait_recv()
        for cp in copies:
            cp.wait_send()

    any_spec = pl.BlockSpec(memory_space=pl.ANY)
    outs = pl.pallas_call(
        body, name=name,
        in_specs=[any_spec] * n, out_specs=[any_spec] * n, out_shape=[SDS(a.shape, a.dtype) for a in arrs],
        scratch_shapes=[pltpu.SemaphoreType.DMA((n,)), pltpu.SemaphoreType.DMA((n,))],
    )(*arrs)
    return list(outs)


def _chip_exchange(name, parts):
    n = len(parts)

    def body(*refs):
        ins, outs = refs[:n], refs[n:2 * n]
        send_sems, recv_sems, local_sems = refs[2 * n:]
        x, y, c = _position()
        my_chip = 2 * x + y
        local = [pltpu.make_async_copy(ins[a].at[my_chip], outs[a].at[my_chip], local_sems.at[a]) for a in range(n)]
        for cp in local:
            cp.start()
        sends, recvs = [], []
        for a in range(n):
            for k in range(1, N_CHIPS):
                px = (1 - x) if (k >> 1) & 1 else x
                py = (1 - y) if k & 1 else y
                peer_chip = 2 * px + py
                send = pltpu.make_async_remote_copy(
                    src_ref=ins[a].at[peer_chip], dst_ref=outs[a].at[my_chip],
                    send_sem=send_sems.at[a, k - 1], recv_sem=recv_sems.at[a, k - 1],
                    device_id=(px, py, c), device_id_type=MESH)
                send.start()
                sends.append(send)
                recvs.append(pltpu.make_async_remote_copy(
                    src_ref=ins[a].at[peer_chip], dst_ref=outs[a].at[peer_chip],
                    send_sem=send_sems.at[a, k - 1], recv_sem=recv_sems.at[a, k - 1],
                    device_id=(px, py, c), device_id_type=MESH))
        for cp in recvs:
            cp.wait_recv()
        for cp in sends:
            cp.wait_send()
        for cp in local:
            cp.wait()

    any_spec = pl.BlockSpec(memory_space=pl.ANY)
    outs = pl.pallas_call(
        body, name=name,
        in_specs=[any_spec] * n, out_specs=[any_spec] * n,
        out_shape=[SDS(a.shape, a.dtype) for a in parts],
        scratch_shapes=[pltpu.SemaphoreType.DMA((n, N_CHIPS - 1)), pltpu.SemaphoreType.DMA((n, N_CHIPS - 1)),
                        pltpu.SemaphoreType.DMA((n,))],
    )(*parts)
    return list(outs)


def _reduce_exchange(stacks):
    core = lax.axis_index("c")
    by_core = [st.reshape(N_CHIPS, 2, *st.shape[1:]) for st in stacks]
    mine = [lax.dynamic_index_in_dim(v, core, axis=1, keepdims=False) for v in by_core]
    theirs = _sibling_exchange(
        "exchange_sibling", [lax.dynamic_index_in_dim(v, 1 - core, axis=1, keepdims=False) for v in by_core])
    summed = []
    for a, (m, t) in enumerate(zip(mine, theirs)):
        cols = m.shape[-1]
        (both,) = _rowwise(f"exchange_add_{a}", lambda p, q: (p + q,), [m.reshape(-1, cols), t.reshape(-1, cols)], [],
                           [(cols, m.dtype)])
        summed.append(both.reshape(m.shape))
    return _chip_exchange("exchange_chips", summed)


def _rows_from_shards(g):
    n, l, r, c = g.shape
    return g.transpose(1, 0, 2, 3).reshape(l, n * r, c)


def _rows_to_shards(w):
    l, rows, c = w.shape
    return w.reshape(l, N_DEV, rows // N_DEV, c).transpose(1, 0, 2, 3)


def _pad_lanes(a, width):
    return jnp.pad(a, [(0, 0)] * (a.ndim - 1) + [(0, width - a.shape[-1])])


def _row(vec, width=None):
    vec = vec.reshape(1, -1)
    return vec if width is None else _pad_lanes(vec, width)


class _SmallPack:
    def __init__(self, shapes):
        self.shapes, self.offsets, at = shapes, {}, 0
        for name, shape in shapes.items():
            last = shape[-1]
            lead = int(math.prod(shape[:-1]))
            rows = lead * (last // LANES) if last >= LANES else lead
            self.offsets[name] = (at, rows)
            at += rows
        self.rows = -(-at // 8) * 8

    def pack(self, values):
        pieces = []
        for name, shape in self.shapes.items():
            val = values[name].astype(f32)
            if shape[-1] >= LANES:
                pieces.append(val.reshape(-1, LANES))
            else:
                pieces.append(_pad_lanes(val.reshape(-1, shape[-1]), LANES))
        used = sum(p.shape[0] for p in pieces)
        if used < self.rows:
            pieces.append(jnp.zeros((self.rows - used, LANES), f32))
        return jnp.concatenate(pieces, axis=0)

    def unpack(self, packed):
        out = {}
        for name, shape in self.shapes.items():
            at, rows = self.offsets[name]
            blk = packed[at:at + rows]
            out[name] = blk.reshape(shape) if shape[-1] >= LANES else blk[:, :shape[-1]].reshape(shape)
        return out


def kernel(x, mem, ffn1_norm, ffn1_w_gate_up, ffn1_w_down, mix_norm, ffn2_norm, ffn2_w_gate_up, ffn2_w_down, gdn_w_in, gdn_conv, gdn_A_log, gdn_dt_bias, gdn_out_norm, fox_w_in, w_out, mem_norm, mem_w_kv, kv_norm, kv_w, kv_b_f, final_norm, loss_target, m_ffn1_norm, m_ffn1_w_gate_up, m_ffn1_w_down, m_mix_norm, m_ffn2_norm, m_ffn2_w_gate_up, m_ffn2_w_down, m_gdn_w_in, m_gdn_conv, m_gdn_A_log, m_gdn_dt_bias, m_gdn_out_norm, m_fox_w_in, m_w_out, m_mem_norm, m_mem_w_kv, m_kv_norm, m_kv_w, m_kv_b_f, m_final_norm, v_ffn1_norm, v_ffn1_w_gate_up, v_ffn1_w_down, v_mix_norm, v_ffn2_norm, v_ffn2_w_gate_up, v_ffn2_w_down, v_gdn_w_in, v_gdn_conv, v_gdn_A_log, v_gdn_dt_bias, v_gdn_out_norm, v_fox_w_in, v_w_out, v_mem_norm, v_mem_w_kv, v_kv_norm, v_kv_w, v_kv_b_f, v_final_norm):
    weights = dict(ffn1_norm=ffn1_norm, ffn1_w_gate_up=ffn1_w_gate_up, ffn1_w_down=ffn1_w_down, mix_norm=mix_norm,
                   ffn2_norm=ffn2_norm, ffn2_w_gate_up=ffn2_w_gate_up, ffn2_w_down=ffn2_w_down, gdn_w_in=gdn_w_in,
                   gdn_conv=gdn_conv, gdn_A_log=gdn_A_log, gdn_dt_bias=gdn_dt_bias, gdn_out_norm=gdn_out_norm,
                   fox_w_in=fox_w_in, w_out=w_out, mem_norm=mem_norm, mem_w_kv=mem_w_kv, kv_norm=kv_norm, kv_w=kv_w,
                   kv_b_f=kv_b_f, final_norm=final_norm)
    mom_m = dict(ffn1_norm=m_ffn1_norm, ffn1_w_gate_up=m_ffn1_w_gate_up, ffn1_w_down=m_ffn1_w_down, mix_norm=m_mix_norm,
                 ffn2_norm=m_ffn2_norm, ffn2_w_gate_up=m_ffn2_w_gate_up, ffn2_w_down=m_ffn2_w_down, gdn_w_in=m_gdn_w_in,
                 gdn_conv=m_gdn_conv, gdn_A_log=m_gdn_A_log, gdn_dt_bias=m_gdn_dt_bias, gdn_out_norm=m_gdn_out_norm,
                 fox_w_in=m_fox_w_in, w_out=m_w_out, mem_norm=m_mem_norm, mem_w_kv=m_mem_w_kv, kv_norm=m_kv_norm,
                 kv_w=m_kv_w, kv_b_f=m_kv_b_f, final_norm=m_final_norm)
    mom_v = dict(ffn1_norm=v_ffn1_norm, ffn1_w_gate_up=v_ffn1_w_gate_up, ffn1_w_down=v_ffn1_w_down, mix_norm=v_mix_norm,
                 ffn2_norm=v_ffn2_norm, ffn2_w_gate_up=v_ffn2_w_gate_up, ffn2_w_down=v_ffn2_w_down, gdn_w_in=v_gdn_w_in,
                 gdn_conv=v_gdn_conv, gdn_A_log=v_gdn_A_log, gdn_dt_bias=v_gdn_dt_bias, gdn_out_norm=v_gdn_out_norm,
                 fox_w_in=v_fox_w_in, w_out=v_w_out, mem_norm=v_mem_norm, mem_w_kv=v_mem_w_kv, kv_norm=v_kv_norm,
                 kv_w=v_kv_w, kv_b_f=v_kv_b_f, final_norm=v_final_norm)
    names = list(weights)
    small_names = [n for n in names if weights[n].shape == mom_m[n].shape and n in (
        "ffn1_norm", "mix_norm", "ffn2_norm", "gdn_A_log", "gdn_dt_bias", "gdn_out_norm", "mem_norm", "kv_norm",
        "kv_b_f", "final_norm")]
    big_names = [n for n in names if n not in small_names]

    h = x[0]
    target = loss_target[0]
    mem_tokens = mem[0]
    s, d = h.shape
    depth = ffn1_norm.shape[0]
    n_a = gdn_w_in.shape[0]
    n_heads, head_dim = gdn_A_log.shape[1], gdn_out_norm.shape[1]
    gw = n_heads * head_dim
    a_in = gdn_w_in.shape[2]
    mem_w = a_in - 4 * gw - 2 * n_heads
    a_in_pad = 4 * gw + mem_w + LANES
    kv_width = kv_w.shape[1]
    kv_pad = 2 * gw + LANES
    fh = ffn1_w_down.shape[1] * N_DEV

    def permute_in(w):
        ab = w[..., 4 * gw:4 * gw + 2 * n_heads]
        return jnp.concatenate([w[..., :4 * gw], w[..., 4 * gw + 2 * n_heads:], _pad_lanes(ab, LANES)], axis=-1)

    def unpermute_in(w):
        return jnp.concatenate([w[..., :4 * gw], w[..., 4 * gw + mem_w:4 * gw + mem_w + 2 * n_heads],
                                w[..., 4 * gw:4 * gw + mem_w]], axis=-1)

    gathered = _all_gather("gather_weights", [
        ffn1_w_gate_up.astype(bf16), ffn1_w_down.astype(bf16), ffn2_w_gate_up.astype(bf16), ffn2_w_down.astype(bf16),
        permute_in(gdn_w_in).astype(bf16), fox_w_in.astype(bf16), w_out.astype(bf16), mem_w_kv.astype(bf16),
        _pad_lanes(kv_w, kv_pad).astype(bf16)[None], gdn_conv])
    wgu1, wd1_s, wgu2, wd2_s, win_s, wfox_s, wout_s, wmem_s, wkv_s, conv_s = gathered
    wd1, wd2 = _rows_from_shards(wd1_s), _rows_from_shards(wd2_s)
    win, wfox, wout = _rows_from_shards(win_s), _rows_from_shards(wfox_s), _rows_from_shards(wout_s)
    wmem = _rows_from_shards(wmem_s)
    wmem_cat = wmem.transpose(1, 0, 2).reshape(d, depth * 2 * mem_w)
    wkv = _rows_from_shards(wkv_s)[0]
    conv_w = conv_s.transpose(1, 2, 0, 3).reshape(n_a, gdn_conv.shape[1], 3 * gw)

    a_log_rows = [_row(gdn_A_log[l], LANES) for l in range(n_a)]
    dt_rows = [_row(gdn_dt_bias[l], LANES) for l in range(n_a)]
    onorm_rows = [_row(gdn_out_norm[l]) for l in range(n_a)]
    b_f_row = _row(kv_b_f, LANES)

    (mem_n,) = _rowwise("mem_norm", lambda t, g: (_rms(t, g),), [mem_tokens], [_row(mem_norm)], [(d, bf16)])
    mem_kv = _matmul("mem_kv", mem_n, wmem_cat, "nn", f32)

    saved = []
    shared = None
    for l in range(depth):
        rec = {"h0": h}
        h1, gate1, up1 = _ffn_fwd(f"ffn1_fwd_{l}", h, _row(ffn1_norm[l]), wgu1, wd1, l)
        (u,) = _rowwise(f"mix_norm_{l}", lambda t, g: (_rms(t, g),), [h1], [_row(mix_norm[l])], [(d, bf16)])
        rec.update(h1=h1, u=u, ffn1=(gate1, up1))
        if l < n_a:
            proj = _matmul(f"gdn_in_{l}", u, win[l], "nn", f32)
            yc = _conv_fwd(f"conv_fwd_{l}", V(proj, cb=0, w=3 * gw), conv_w[l])
            ab_view = V(proj, cb=(4 * gw + mem_w) // LANES, w=LANES)
            q, k, v, gb = _rowwise(f"gdn_pre_{l}", functools.partial(_gdn_pre, n_heads, head_dim),
                                   [yc, ab_view], [a_log_rows[l], dt_rows[l]],
                                   [(gw, f32), (gw, f32), (gw, f32), (LANES, f32)])
            o, states = _gdn_chunk_fwd(f"gdn_chunk_fwd_{l}", q, k, v, gb, n_heads, head_dim)
            z_view = V(proj, cb=3, w=gw)
            (main,) = _rowwise(f"gdn_post_{l}", functools.partial(_gdn_post, n_heads, head_dim),
                               [o, z_view], [onorm_rows[l]], [(gw, bf16)])
            qmem_view = V(proj, cb=4 * gw // mem_w, w=mem_w)
            rec.update(proj=proj, yc=yc, q=q, k=k, v=v, gb=gb, o=o, states=states)
        else:
            proj = _matmul(f"fox_in_{l}", u, wfox[l - n_a], "nn", bf16)
            sk, sv, crow = shared["k"], shared["v"], shared["crow"]
            main, lse = _fox_fwd(f"fox_fwd_{l}", proj, sk, sv, crow, n_heads, head_dim)
            qmem_view = V(proj, cb=gw // mem_w, w=mem_w)
            rec.update(proj=proj, lse=lse)
        km = V(mem_kv, cb=2 * l, w=mem_w)
        vm = V(mem_kv, cb=2 * l + 1, w=mem_w)
        (mem_out,) = _rowwise(f"mem_attn_{l}", _mem_attn, [qmem_view], [km, vm], [(mem_w, bf16)])
        cat = jnp.concatenate([main, mem_out], axis=1)
        h2 = _matmul(f"out_proj_{l}", cat, wout[l], "nn", f32, add=h1)
        h3, gate2, up2 = _ffn_fwd(f"ffn2_fwd_{l}", h2, _row(ffn2_norm[l]), wgu2, wd2, l)
        rec.update(cat=cat, h2=h2, qmem=qmem_view, ffn2=(gate2, up2))
        saved.append(rec)
        h = h3
        if l == n_a - 1:
            (hn,) = _rowwise("kv_norm", lambda t, g: (_rms(t, g),), [h], [_row(kv_norm)], [(d, bf16)])
            p = _matmul("kv_proj", hn, wkv, "nn", f32)
            pk, pv, pf = V(p, cb=0, w=gw), V(p, cb=1, w=gw), V(p, cb=2 * gw // LANES, w=LANES)
            sk, sv, log_f = _rowwise("kv_post", functools.partial(_kv_post, n_heads), [pk, pv, pf], [b_f_row],
                                     [(gw, bf16), (gw, bf16), (LANES, f32)])
            cum = _cumsum("forget_cumsum", [log_f], reverse=False, scale=LOG2_E)
            c_heads = cum[:, :n_heads].T
            shared = dict(k=sk, v=sv, crow=c_heads.reshape(n_heads, 1, s), h=h, hn=hn, p=p, views=(pk, pv, pf))

    loss_part, dh, d_final = _final_loss("final_loss", h, target, _row(final_norm))
    loss = lax.psum(loss_part[0, 0], ("x", "y", "c"))

    grads = {}
    per_layer = {n: [None] * depth for n in ("ffn1_norm", "mix_norm", "ffn2_norm", "ffn1_gu", "ffn1_d", "ffn2_gu",
                                             "ffn2_d", "w_out")}
    per_a = {n: [None] * n_a for n in ("gdn_w_in", "gdn_conv", "gdn_A_log", "gdn_dt_bias", "gdn_out_norm")}
    per_b = {"fox_w_in": [None] * (depth - n_a)}
    d_mem_kv = [None] * depth
    fox_grads = []

    def ffn_backward(tag, l, h_in, d_out, gain, kept, wgu, wd):
        parts, dwg, dwu, dwd = _ffn_bwd(f"{tag}_bwd_{l}", h_in, d_out, _row(gain), kept[0], kept[1], wgu, wd, l)
        nh = parts.shape[0]
        (d_in,), (d_gain,) = _rowwise_vjp(
            f"{tag}_norm_bwd_{l}", lambda t, g: (_rms(t, g),), [h_in], [_row(gain)],
            [[V(parts, lead=t) for t in range(nh)]], [f32], add=d_out)
        return d_in, d_gain, jnp.concatenate([dwg, dwu], axis=0), dwd

    for l in reversed(range(depth)):
        rec = saved[l]
        if l == n_a - 1:
            dk_list = [V(g["dk"]) for g in fox_grads]
            dv_list = [V(g["dv"]) for g in fox_grads]
            dc_parts = [_pad_lanes(part.reshape(n_heads, s).T, LANES) for g in fox_grads for part in g["dc"]]
            d_log_f = _cumsum("forget_cumsum_bwd", dc_parts, reverse=True)
            pk, pv, pf = shared["views"]
            (dpk, dpv, dpf), (d_bf,) = _rowwise_vjp(
                "kv_post_bwd", functools.partial(_kv_post, n_heads), [pk, pv, pf], [b_f_row],
                [dk_list, dv_list, [d_log_f]], [bf16, bf16, bf16])
            dp = jnp.concatenate([dpk, dpv, dpf], axis=1)
            d_hn = _matmul("kv_proj_dx", dp, wkv, "nt", f32)
            grads["kv_w"] = _matmul("kv_proj_dw", shared["hn"], dp, "tn", f32)[:, :kv_width]
            (dh,), (d_kvn,) = _rowwise_vjp("kv_norm_bwd", lambda t, g: (_rms(t, g),), [shared["h"]], [_row(kv_norm)],
                                           [[d_hn]], [f32], add=dh)
            grads["kv_norm"] = d_kvn.reshape(-1)
            grads["kv_b_f"] = d_bf[0, :n_heads]

        dh2, per_layer["ffn2_norm"][l], per_layer["ffn2_gu"][l], per_layer["ffn2_d"][l] = ffn_backward(
            "ffn2", l, rec["h2"], dh, ffn2_norm[l], rec["ffn2"], wgu2, wd2)
        d_cat = _matmul(f"out_proj_dx_{l}", dh2, wout[l], "nt", f32)
        per_layer["w_out"][l] = _matmul(f"out_proj_dw_{l}", rec["cat"], dh2, "tn", f32)
        d_main = V(d_cat, cb=0, w=gw)
        d_memo = V(d_cat, cb=gw // mem_w, w=mem_w)
        km, vm = V(mem_kv, cb=2 * l, w=mem_w), V(mem_kv, cb=2 * l + 1, w=mem_w)
        (dqmem,), (dkm, dvm) = _rowwise_vjp(f"mem_attn_bwd_{l}", _mem_attn, [rec["qmem"]], [km, vm], [[d_memo]], [bf16])
        d_mem_kv[l] = jnp.concatenate([dkm, dvm], axis=1)
        if l < n_a:
            proj = rec["proj"]
            z_view = V(proj, cb=3, w=gw)
            (d_o, d_z), (d_onorm,) = _rowwise_vjp(
                f"gdn_post_bwd_{l}", functools.partial(_gdn_post, n_heads, head_dim), [rec["o"], z_view],
                [onorm_rows[l]], [[d_main]], [f32, bf16])
            dq, dk, dv, dgb = _gdn_chunk_bwd(f"gdn_chunk_bwd_{l}", rec["q"], rec["k"], rec["v"], rec["gb"],
                                             rec["states"], d_o, n_heads, head_dim)
            ab_view = V(proj, cb=(4 * gw + mem_w) // LANES, w=LANES)
            (d_yc, d_ab), (d_alog, d_dt) = _rowwise_vjp(
                f"gdn_pre_bwd_{l}", functools.partial(_gdn_pre, n_heads, head_dim), [rec["yc"], ab_view],
                [a_log_rows[l], dt_rows[l]], [[dq], [dk], [dv], [dgb]], [f32, bf16])
            d_qkv, d_conv = _conv_bwd(f"conv_bwd_{l}", V(proj, cb=0, w=3 * gw), d_yc, conv_w[l])
            d_proj = jnp.concatenate([d_qkv, d_z, dqmem, d_ab], axis=1)
            du = _matmul(f"gdn_in_dx_{l}", d_proj, win[l], "nt", f32)
            per_a["gdn_w_in"][l] = unpermute_in(_matmul(f"gdn_in_dw_{l}", rec["u"], d_proj, "tn", f32))
            per_a["gdn_conv"][l] = d_conv
            per_a["gdn_A_log"][l] = d_alog[0, :n_heads]
            per_a["gdn_dt_bias"][l] = d_dt[0, :n_heads]
            per_a["gdn_out_norm"][l] = d_onorm[0]
        else:
            proj = rec["proj"]
            sk, sv, crow = shared["k"], shared["v"], shared["crow"]
            dq, delta, dc_col = _fox_bwd_dq(f"fox_dq_{l}", proj, sk, sv, crow, rec["lse"], rec["cat"], d_cat,
                                            n_heads, head_dim)
            dk, dv, dc_row = _fox_bwd_dkv(f"fox_dkv_{l}", proj, sk, sv, crow, rec["lse"], delta, d_cat,
                                          n_heads, head_dim)
            fox_grads.append(dict(dk=dk, dv=dv, dc=(dc_row, dc_col)))
            d_proj = jnp.concatenate([dq, dqmem], axis=1)
            du = _matmul(f"fox_in_dx_{l}", d_proj, wfox[l - n_a], "nt", f32)
            per_b["fox_w_in"][l - n_a] = _matmul(f"fox_in_dw_{l}", rec["u"], d_proj, "tn", f32)
        (dh1,), (d_mix,) = _rowwise_vjp(f"mix_norm_bwd_{l}", lambda t, g: (_rms(t, g),), [rec["h1"]],
                                        [_row(mix_norm[l])], [[du]], [f32], add=dh2)
        per_layer["mix_norm"][l] = d_mix
        dh, per_layer["ffn1_norm"][l], per_layer["ffn1_gu"][l], per_layer["ffn1_d"][l] = ffn_backward(
            "ffn1", l, rec["h0"], dh1, ffn1_norm[l], rec["ffn1"], wgu1, wd1)

    grad_x = dh[None]

    d_mem_kv_cat = jnp.concatenate(d_mem_kv, axis=1)
    d_wmem_cat = _matmul("mem_kv_dw", mem_n, d_mem_kv_cat, "tn", f32)
    d_mem_n = _matmul("mem_kv_dx", d_mem_kv_cat, wmem_cat, "nt", f32)
    _, (d_memnorm,) = _rowwise_vjp("mem_norm_bwd", lambda t, g: (_rms(t, g),), [mem_tokens], [_row(mem_norm)],
                                   [[d_mem_n]], [None])

    def gu_stack(per):
        return jnp.stack(per, axis=1).astype(bf16)

    stacks = dict(
        ffn1_w_gate_up=gu_stack(per_layer["ffn1_gu"]),
        ffn1_w_down=_rows_to_shards(jnp.stack(per_layer["ffn1_d"])).astype(bf16),
        ffn2_w_gate_up=gu_stack(per_layer["ffn2_gu"]),
        ffn2_w_down=_rows_to_shards(jnp.stack(per_layer["ffn2_d"])).astype(bf16),
        gdn_w_in=_rows_to_shards(jnp.stack(per_a["gdn_w_in"])).astype(bf16),
        gdn_conv=jnp.stack(per_a["gdn_conv"]).reshape(n_a, -1, N_DEV, 3 * gw // N_DEV).transpose(2, 0, 1, 3),
        fox_w_in=_rows_to_shards(jnp.stack(per_b["fox_w_in"])).astype(bf16),
        w_out=_rows_to_shards(jnp.stack(per_layer["w_out"])).astype(bf16),
        mem_w_kv=_rows_to_shards(d_wmem_cat.reshape(d, depth, 2 * mem_w).transpose(1, 0, 2)).astype(bf16),
        kv_w=_rows_to_shards(grads["kv_w"][None])[:, 0].astype(bf16),
    )
    received = dict(zip(big_names, _reduce_exchange([stacks[n] for n in big_names])))

    small_shapes = {n: weights[n].shape for n in small_names}
    pack = _SmallPack(small_shapes)
    small_grads = dict(
        ffn1_norm=jnp.concatenate(per_layer["ffn1_norm"], axis=0), mix_norm=jnp.concatenate(per_layer["mix_norm"], axis=0),
        ffn2_norm=jnp.concatenate(per_layer["ffn2_norm"], axis=0), gdn_A_log=jnp.stack(per_a["gdn_A_log"]),
        gdn_dt_bias=jnp.stack(per_a["gdn_dt_bias"]), gdn_out_norm=jnp.stack(per_a["gdn_out_norm"]),
        mem_norm=d_memnorm.reshape(-1), kv_norm=grads["kv_norm"], kv_b_f=grads["kv_b_f"], final_norm=d_final.reshape(-1))
    (small_parts,) = _all_gather("gather_small_grads", [pack.pack(small_grads)])

    out_g, out_d, out_m, out_v = {}, {}, {}, {}
    for n in big_names:
        shape = weights[n].shape
        c = shape[-1]
        parts = received[n].reshape(N_CHIPS, -1, c)
        res = _adamw(f"adamw_{n}", parts, weights[n].reshape(-1, c), mom_m[n].reshape(-1, c), mom_v[n].reshape(-1, c))
        out_g[n], out_d[n], out_m[n], out_v[n] = [r.reshape(shape) for r in res]
    res = _adamw("adamw_small", small_parts, pack.pack({n: weights[n] for n in small_names}),
                 pack.pack({n: mom_m[n] for n in small_names}), pack.pack({n: mom_v[n] for n in small_names}))
    for dst, packed in zip((out_g, out_d, out_m, out_v), res):
        dst.update(pack.unpack(packed))

    return (loss, grad_x, *[out_g[n] for n in names], *[out_d[n] for n in names],
            *[out_m[n] for n in names], *[out_v[n] for n in names])
```

```python
import functools
import math

import jax
import jax.numpy as jnp
from jax import lax
from jax.experimental import pallas as pl
from jax.experimental.pallas import tpu as pltpu

f32 = jnp.float32
bf16 = jnp.bfloat16
SDS = jax.ShapeDtypeStruct

N_DEV = 8
MEM_HEADS = 4
CHUNK = 64
LANES = 128
EPS = 1e-6
NEG_INF = -1e30
ADAM_LR = 0.001
ADAM_B1 = 0.9
ADAM_B2 = 0.999
ADAM_EPS = 1e-08
ADAM_WD = 0.01
ADAM_STEP = 10

ROW_TILE = 512
MM_TILE = 1024
NORM_MM_TILE = 512
FFN_FWD_TILE = 1024
FFN_BWD_TILE = 512
ATT_TILE = 1024
ATT_HEADS = 3
CUMSUM_TILE = 256
VMEM_LIMIT = 56 * 1024 * 1024

MESH = pl.DeviceIdType.MESH


def _cp(sem=None):
    return pltpu.CompilerParams(dimension_semantics=sem, vmem_limit_bytes=VMEM_LIMIT)


_DIMS = {"nn": (((1,), (0,)), ((), ())), "nt": (((1,), (1,)), ((), ())), "tn": (((0,), (0,)), ((), ()))}


def _dg(a, b, mode):
    return lax.dot_general(a.astype(bf16), b.astype(bf16), _DIMS[mode], preferred_element_type=f32)


@functools.partial(jax.custom_vjp, nondiff_argnums=(2,))
def _mm(a, b, mode):
    return _dg(a, b, mode)


def _mm_fwd(a, b, mode):
    return _dg(a, b, mode), (a, b)


def _mm_bwd(mode, res, ct):
    a, b = res
    if mode == "nn":
        da, db = _dg(ct, b, "nt"), _dg(a, ct, "tn")
    elif mode == "nt":
        da, db = _dg(ct, b, "nn"), _dg(ct, a, "tn")
    else:
        da, db = _dg(b, ct, "nt"), _dg(a, ct, "nn")
    return da.astype(a.dtype), db.astype(b.dtype)


_mm.defvjp(_mm_fwd, _mm_bwd)


def _split_bf16(x):
    hi = x.astype(bf16)
    return hi, (x - hi.astype(f32)).astype(bf16)


def _dgh(a, b, mode="nn"):
    a_hi, a_lo = _split_bf16(a)
    b_hi, b_lo = _split_bf16(b)
    dims = _DIMS[mode]
    return (lax.dot_general(a_hi, b_hi, dims, preferred_element_type=f32)
            + lax.dot_general(a_hi, b_lo, dims, preferred_element_type=f32)
            + lax.dot_general(a_lo, b_hi, dims, preferred_element_type=f32))


@jax.custom_vjp
def _unit_lower_inverses(lows):
    c = lows[0].shape[0]
    ri = lax.broadcasted_iota(jnp.int32, (c, c), 0)
    ci = lax.broadcasted_iota(jnp.int32, (c, c), 1)
    eye = jnp.where(ri == ci, 1.0, 0.0)
    xs = [-low for low in lows]
    rs = [eye + x for x in xs]
    for _ in range(int(math.log2(c)) - 1):
        xs = [_dgh(x, x) for x in xs]
        rs = [r + _dgh(r, x) for r, x in zip(rs, xs)]
    return tuple(rs)


def _uli_fwd(lows):
    ts = _unit_lower_inverses(lows)
    return ts, ts


def _uli_bwd(ts, cts):
    mids = [_dgh(ct, t, "nt") for t, ct in zip(ts, cts)]
    return (tuple(-_dgh(t, m, "tn") for t, m in zip(ts, mids)),)


_unit_lower_inverses.defvjp(_uli_fwd, _uli_bwd)


@functools.partial(jax.custom_vjp, nondiff_argnums=(1,))
def _split_lanes(x, width):
    return tuple(x[:, i * width:(i + 1) * width] for i in range(x.shape[1] // width))


def _split_fwd(x, width):
    return _split_lanes(x, width), None


def _split_bwd(width, _, cts):
    return (jnp.concatenate(list(cts), axis=1),)


_split_lanes.defvjp(_split_fwd, _split_bwd)


def _sigmoid(x):
    return 1.0 / (1.0 + jnp.exp(-x))


def _silu(x):
    return x * _sigmoid(x)


def _softplus(x):
    return jnp.maximum(x, 0.0) + jnp.log1p(jnp.exp(-jnp.abs(x)))


def _rms(x, gain):
    return x * lax.rsqrt(jnp.mean(x * x, axis=-1, keepdims=True) + EPS) * gain


class V:
    def __init__(self, arr, lead=None, cb=0, w=None):
        self.arr, self.lead, self.cb = arr, lead, cb
        self.w = arr.shape[-1] if w is None else w

    @property
    def rows(self):
        return self.arr.shape[-2]

    def spec(self, tile, order=None):
        lead, cb, w = self.lead, self.cb, self.w
        order = order or (lambda i: i)
        if lead is None:
            return pl.BlockSpec((tile, w), lambda i: (order(i), cb))
        return pl.BlockSpec((None, tile, w), lambda i: (lead, order(i), cb))

    def const_spec(self):
        lead, cb, w, r = self.lead, self.cb, self.w, self.rows
        if lead is None:
            return pl.BlockSpec((r, w), lambda i: (0, cb))
        return pl.BlockSpec((None, r, w), lambda i: (lead, 0, cb))


def _v(a):
    return a if isinstance(a, V) else V(a)


def _rowwise(name, fn, rows, consts, outs, tile=None):
    rows = [_v(r) for r in rows]
    consts = [_v(c) for c in consts]
    s = rows[0].rows
    tile = min(tile or ROW_TILE, s)
    nr, nc = len(rows), len(consts)

    def body(*refs):
        vals = [r[...].astype(f32) for r in refs[:nr + nc]]
        res = fn(*vals)
        for o, val in zip(refs[nr + nc:], res):
            o[...] = val.astype(o.dtype)

    return pl.pallas_call(
        body, name=name, grid=(s // tile,),
        in_specs=[r.spec(tile) for r in rows] + [c.const_spec() for c in consts],
        out_specs=[pl.BlockSpec((tile, w), lambda i: (i, 0)) for w, _ in outs],
        out_shape=[SDS((s, w), dt) for w, dt in outs],
        compiler_params=_cp(("parallel",)),
    )(*[r.arr for r in rows], *[c.arr for c in consts])


def _rowwise_vjp(name, fn, rows, consts, cts, d_rows, add=None, tile=None):
    rows = [_v(r) for r in rows]
    consts = [_v(c) for c in consts]
    cts = [[_v(c) for c in group] for group in cts]
    flat_cts = [c for group in cts for c in group]
    s = rows[0].rows
    tile = min(tile or ROW_TILE, s)
    nr, nc, nt = len(rows), len(consts), len(flat_cts)
    want = [k for k, dt in enumerate(d_rows) if dt is not None]
    has_add = add is not None
    add_v = [_v(add)] if has_add else []

    def body(*refs):
        vals = [r[...].astype(f32) for r in refs[:nr + nc]]
        ct_refs = refs[nr + nc:nr + nc + nt]
        pos = nr + nc + nt
        add_ref = refs[pos] if has_add else None
        pos += 1 if has_add else 0
        drow_refs = refs[pos:pos + len(want)]
        dconst_refs = refs[pos + len(want):]
        ctv, at = [], 0
        for group in cts:
            acc = ct_refs[at][...].astype(f32)
            for r in ct_refs[at + 1:at + len(group)]:
                acc = acc + r[...].astype(f32)
            at += len(group)
            ctv.append(acc)
        _, vjp = jax.vjp(fn, *vals)
        grads = vjp(tuple(ctv))
        for o, k in zip(drow_refs, want):
            g = grads[k]
            if has_add and k == want[0]:
                g = g + add_ref[...].astype(f32)
            o[...] = g.astype(o.dtype)

        @pl.when(pl.program_id(0) == 0)
        def _():
            for o in dconst_refs:
                o[...] = jnp.zeros_like(o)

        for o, g in zip(dconst_refs, grads[nr:]):
            o[...] += g

    outs = pl.pallas_call(
        body, name=name, grid=(s // tile,),
        in_specs=[r.spec(tile) for r in rows] + [c.const_spec() for c in consts]
        + [c.spec(tile) for c in flat_cts] + [a.spec(tile) for a in add_v],
        out_specs=[pl.BlockSpec((tile, rows[k].w), lambda i: (i, 0)) for k in want]
        + [pl.BlockSpec((c.rows, c.w), lambda i: (0, 0)) for c in consts],
        out_shape=[SDS((s, rows[k].w), d_rows[k]) for k in want] + [SDS((c.rows, c.w), f32) for c in consts],
        compiler_params=_cp(("arbitrary",)),
    )(*[r.arr for r in rows], *[c.arr for c in consts], *[c.arr for c in flat_cts], *[a.arr for a in add_v])
    return list(outs[:len(want)]), list(outs[len(want):])


def _pick(n, cap):
    best = None
    for t in range(LANES, min(n, cap) + 1, LANES):
        if n % t == 0:
            best = t
    return best or n


def _norm_matmul(name, h, gain, w, out_dtype):
    m, k = h.shape
    n = w.shape[1]
    tm = min(NORM_MM_TILE, m)

    def body(h_ref, g_ref, w_ref, u_ref, o_ref):
        normed = _rms(h_ref[...], g_ref[...]).astype(bf16)
        u_ref[...] = normed
        o_ref[...] = _dg(normed, w_ref[...], "nn").astype(o_ref.dtype)

    return pl.pallas_call(
        body, name=name, grid=(m // tm,),
        in_specs=[pl.BlockSpec((tm, k), lambda i: (i, 0)), pl.BlockSpec((1, k), lambda i: (0, 0)),
                  pl.BlockSpec((k, n), lambda i: (0, 0))],
        out_specs=[pl.BlockSpec((tm, k), lambda i: (i, 0)), pl.BlockSpec((tm, n), lambda i: (i, 0))],
        out_shape=[SDS((m, k), bf16), SDS((m, n), out_dtype)],
        compiler_params=_cp(("parallel",)),
    )(h, gain, w)


def _matmul(name, a, b, mode, out_dtype, add=None, tn_cap=1280):
    has_add = add is not None
    if mode in ("nn", "nt"):
        m, k = a.shape
        n = b.shape[1] if mode == "nn" else b.shape[0]
        tm = min(MM_TILE, m)
        tn = _pick(n, tn_cap) if k * n * 2 > (8 << 20) else n

        def body(*refs):
            a_ref, b_ref = refs[0], refs[1]
            o_ref = refs[-1]
            acc = _dg(a_ref[...], b_ref[...], mode)
            if has_add:
                acc = acc + refs[2][...].astype(f32)
            o_ref[...] = acc.astype(o_ref.dtype)

        b_spec = pl.BlockSpec((k, tn), lambda i, j: (0, j)) if mode == "nn" else pl.BlockSpec((tn, k), lambda i, j: (j, 0))
        in_specs = [pl.BlockSpec((tm, k), lambda i, j: (i, 0)), b_spec]
        args = [a, b]
        if has_add:
            in_specs.append(pl.BlockSpec((tm, tn), lambda i, j: (i, j)))
            args.append(add)
        return pl.pallas_call(
            body, name=name, grid=(m // tm, n // tn), in_specs=in_specs,
            out_specs=pl.BlockSpec((tm, tn), lambda i, j: (i, j)),
            out_shape=SDS((m, n), out_dtype), compiler_params=_cp(("parallel", "parallel")),
        )(*args)
    kk, m = a.shape
    n = b.shape[1]
    tk = min(MM_TILE, kk)
    tn = _pick(n, tn_cap) if m * n * 4 > (7 << 20) else n

    def body_tn(a_ref, b_ref, o_ref):
        @pl.when(pl.program_id(1) == 0)
        def _():
            o_ref[...] = jnp.zeros_like(o_ref)

        o_ref[...] += _dg(a_ref[...], b_ref[...], "tn")

    return pl.pallas_call(
        body_tn, name=name, grid=(n // tn, kk // tk),
        in_specs=[pl.BlockSpec((tk, m), lambda j, k: (k, 0)), pl.BlockSpec((tk, tn), lambda j, k: (k, j))],
        out_specs=pl.BlockSpec((m, tn), lambda j, k: (0, j)),
        out_shape=SDS((m, n), f32), compiler_params=_cp(("parallel", "arbitrary")),
    )(a, b)


def _ffn_fwd(name, h, gain, wgu, wd, layer):
    s, d = h.shape
    hs = wgu.shape[3]
    nh = wgu.shape[0] // 2
    tm = min(FFN_FWD_TILE, s)

    def body(h_ref, g_ref, wg_ref, wu_ref, wd_ref, o_ref, gate_ref, up_ref, n_scr, acc_scr):
        t = pl.program_id(1)

        @pl.when(t == 0)
        def _():
            n_scr[...] = _rms(h_ref[...], g_ref[...]).astype(bf16)
            acc_scr[...] = jnp.zeros_like(acc_scr)

        n = n_scr[...]
        gate = _dg(n, wg_ref[...], "nn")
        up = _dg(n, wu_ref[...], "nn")
        gate_ref[...] = gate.astype(gate_ref.dtype)
        up_ref[...] = up.astype(up_ref.dtype)
        acc_scr[...] += _dg(_silu(gate) * up, wd_ref[...], "nn")

        @pl.when(t == nh - 1)
        def _():
            o_ref[...] = h_ref[...] + 0.5 * acc_scr[...]

    saved_spec = pl.BlockSpec((None, tm, hs), lambda i, t: (t, i, 0))
    return pl.pallas_call(
        body, name=name, grid=(s // tm, nh),
        in_specs=[
            pl.BlockSpec((tm, d), lambda i, t: (i, 0)),
            pl.BlockSpec((1, d), lambda i, t: (0, 0)),
            pl.BlockSpec((None, None, d, hs), lambda i, t: (t, layer, 0, 0)),
            pl.BlockSpec((None, None, d, hs), lambda i, t: (t + nh, layer, 0, 0)),
            pl.BlockSpec((None, hs, d), lambda i, t: (layer, t, 0)),
        ],
        out_specs=[pl.BlockSpec((tm, d), lambda i, t: (i, 0)), saved_spec, saved_spec],
        out_shape=[SDS((s, d), f32), SDS((nh, s, hs), bf16), SDS((nh, s, hs), bf16)],
        scratch_shapes=[pltpu.VMEM((tm, d), bf16), pltpu.VMEM((tm, d), f32)],
        compiler_params=_cp(("parallel", "arbitrary")),
    )(h, gain, wgu, wgu, wd)


def _ffn_bwd(name, h, dout, gain, gate_s, up_s, wgu, wd, layer):
    s, d = h.shape
    hs = wgu.shape[3]
    nh = wgu.shape[0] // 2
    tm = min(FFN_BWD_TILE, s)

    def body(h_ref, do_ref, g_ref, gate_ref, up_ref, wg_ref, wu_ref, wd_ref, dn_ref, dwg_ref, dwu_ref, dwd_ref):
        @pl.when(pl.program_id(1) == 0)
        def _():
            dwg_ref[...] = jnp.zeros_like(dwg_ref)
            dwu_ref[...] = jnp.zeros_like(dwu_ref)
            dwd_ref[...] = jnp.zeros_like(dwd_ref)

        n = _rms(h_ref[...], g_ref[...]).astype(bf16)
        wg, wu, wdn = wg_ref[...], wu_ref[...], wd_ref[...]
        gate = gate_ref[...].astype(f32)
        up = up_ref[...].astype(f32)
        sg = _sigmoid(gate)
        act = gate * sg
        dy = (0.5 * do_ref[...]).astype(bf16)
        da = _dg(dy, wdn, "nt")
        dup = (da * act).astype(bf16)
        dgate = (da * up * (sg * (1.0 + gate * (1.0 - sg)))).astype(bf16)
        dwd_ref[...] += _dg(act * up, dy, "tn")
        dwg_ref[...] += _dg(n, dgate, "tn")
        dwu_ref[...] += _dg(n, dup, "tn")
        dn_ref[...] = (_dg(dgate, wg, "nt") + _dg(dup, wu, "nt")).astype(dn_ref.dtype)

    return pl.pallas_call(
        body, name=name, grid=(nh, s // tm),
        in_specs=[
            pl.BlockSpec((tm, d), lambda t, i: (i, 0)),
            pl.BlockSpec((tm, d), lambda t, i: (i, 0)),
            pl.BlockSpec((1, d), lambda t, i: (0, 0)),
            pl.BlockSpec((None, tm, hs), lambda t, i: (t, i, 0)),
            pl.BlockSpec((None, tm, hs), lambda t, i: (t, i, 0)),
            pl.BlockSpec((None, None, d, hs), lambda t, i: (t, layer, 0, 0)),
            pl.BlockSpec((None, None, d, hs), lambda t, i: (t + nh, layer, 0, 0)),
            pl.BlockSpec((None, hs, d), lambda t, i: (layer, t, 0)),
        ],
        out_specs=[
            pl.BlockSpec((None, tm, d), lambda t, i: (t, i, 0)),
            pl.BlockSpec((None, d, hs), lambda t, i: (t, 0, 0)),
            pl.BlockSpec((None, d, hs), lambda t, i: (t, 0, 0)),
            pl.BlockSpec((hs, d), lambda t, i: (t, 0)),
        ],
        out_shape=[SDS((nh, s, d), bf16), SDS((nh, d, hs), f32), SDS((nh, d, hs), f32), SDS((nh * hs, d), f32)],
        compiler_params=_cp(("parallel", "arbitrary")),
    )(h, dout, gain, gate_s, up_s, wgu, wgu, wd)


def _conv_fwd(name, x, w):
    x = _v(x)
    s, c = x.rows, x.w
    cw = w.shape[0]
    tile = min(ROW_TILE, s)
    cb = x.cb

    def body(x_ref, halo_ref, w_ref, o_ref, buf):
        first = pl.program_id(0) == 0
        buf[0:8, :] = jnp.where(first, 0.0, halo_ref[...])
        buf[8:8 + tile, :] = x_ref[...]
        acc = w_ref[0:1, :] * buf[pl.ds(8 - cw + 1, tile), :]
        for j in range(1, cw):
            acc = acc + w_ref[j:j + 1, :] * buf[pl.ds(8 - cw + 1 + j, tile), :]
        o_ref[...] = acc

    return pl.pallas_call(
        body, name=name, grid=(s // tile,),
        in_specs=[
            pl.BlockSpec((tile, c), lambda i: (i, cb)),
            pl.BlockSpec((8, c), lambda i: (jnp.maximum(i * (tile // 8) - 1, 0), cb)),
            pl.BlockSpec((cw, c), lambda i: (0, 0)),
        ],
        out_specs=pl.BlockSpec((tile, c), lambda i: (i, 0)),
        out_shape=SDS((s, c), f32),
        scratch_shapes=[pltpu.VMEM((tile + 8, c), f32)],
        compiler_params=_cp(("parallel",)),
    )(x.arr, x.arr, w)


def _conv_bwd(name, x, dy, w):
    x = _v(x)
    s, c = x.rows, x.w
    cw = w.shape[0]
    tile = min(ROW_TILE, s)
    n_tiles = s // tile
    cb = x.cb

    def body(x_ref, xh_ref, dy_ref, dyh_ref, w_ref, dx_ref, dw_ref, xbuf, dbuf):
        i = pl.program_id(0)
        xbuf[0:8, :] = jnp.where(i == 0, 0.0, xh_ref[...])
        xbuf[8:8 + tile, :] = x_ref[...]
        dyv = dy_ref[...]
        dbuf[0:tile, :] = dyv
        dbuf[tile:tile + 8, :] = jnp.where(i == n_tiles - 1, 0.0, dyh_ref[...])

        @pl.when(i == 0)
        def _():
            dw_ref[...] = jnp.zeros_like(dw_ref)

        acc = w_ref[0:1, :] * dbuf[pl.ds(cw - 1, tile), :]
        for j in range(1, cw):
            acc = acc + w_ref[j:j + 1, :] * dbuf[pl.ds(cw - 1 - j, tile), :]
        dx_ref[...] = acc.astype(dx_ref.dtype)
        for j in range(cw):
            dw_ref[j:j + 1, :] += jnp.sum(xbuf[pl.ds(8 - cw + 1 + j, tile), :] * dyv, axis=0, keepdims=True)

    return pl.pallas_call(
        body, name=name, grid=(n_tiles,),
        in_specs=[
            pl.BlockSpec((tile, c), lambda i: (i, cb)),
            pl.BlockSpec((8, c), lambda i: (jnp.maximum(i * (tile // 8) - 1, 0), cb)),
            pl.BlockSpec((tile, c), lambda i: (i, 0)),
            pl.BlockSpec((8, c), lambda i: (jnp.minimum((i + 1) * (tile // 8), s // 8 - 1), 0)),
            pl.BlockSpec((cw, c), lambda i: (0, 0)),
        ],
        out_specs=[pl.BlockSpec((tile, c), lambda i: (i, 0)), pl.BlockSpec((cw, c), lambda i: (0, 0))],
        out_shape=[SDS((s, c), bf16), SDS((cw, c), f32)],
        scratch_shapes=[pltpu.VMEM((tile + 8, c), f32), pltpu.VMEM((tile + 8, c), f32)],
        compiler_params=_cp(("arbitrary",)),
    )(x.arr, x.arr, dy, dy, w)


def _gdn_pre(n_heads, head_dim, yc, ab, a_log, dt_bias):
    gw = n_heads * head_dim
    act = _silu(yc)
    parts = _split_lanes(act, head_dim)
    qs = [p * lax.rsqrt(jnp.sum(p * p, axis=-1, keepdims=True) + EPS) * (head_dim ** -0.5) for p in parts[:n_heads]]
    ks = [p * lax.rsqrt(jnp.sum(p * p, axis=-1, keepdims=True) + EPS) for p in parts[n_heads:2 * n_heads]]
    lane = lax.broadcasted_iota(jnp.int32, ab.shape, 1)
    g = -jnp.exp(a_log) * _softplus(ab + dt_bias)
    gb = jnp.where(lane < n_heads, g, jnp.where(lane < 2 * n_heads, _sigmoid(ab), 0.0))
    del gw
    return (jnp.concatenate(qs, axis=1), jnp.concatenate(ks, axis=1),
            jnp.concatenate(list(parts[2 * n_heads:]), axis=1), gb)


def _gdn_post(n_heads, head_dim, o, z, out_norm):
    parts = _split_lanes(o, head_dim)
    normed = jnp.concatenate([_rms(p, out_norm) for p in parts], axis=1)
    return (normed * _silu(z),)


def _gdn_chunk(n_heads, head_dim, q, k, v, gb, *states):
    c = q.shape[0]
    ri = lax.broadcasted_iota(jnp.int32, (c, c), 0)
    ci = lax.broadcasted_iota(jnp.int32, (c, c), 1)
    incl, strict, diag = ri >= ci, ri > ci, ri == ci
    lane = lax.broadcasted_iota(jnp.int32, gb.shape, 1)
    qs, ks, vs = _split_lanes(q, head_dim), _split_lanes(k, head_dim), _split_lanes(v, head_dim)
    heads = range(n_heads)
    g = [jnp.sum(jnp.where(lane == h, gb, 0.0), axis=1, keepdims=True) for h in heads]
    beta = [jnp.sum(jnp.where(lane == n_heads + h, gb, 0.0), axis=1, keepdims=True) for h in heads]
    g_row = [jnp.sum(jnp.where(diag, g[h], 0.0), axis=0, keepdims=True) for h in heads]
    cg_col = [jnp.sum(jnp.where(incl, g_row[h], 0.0), axis=1, keepdims=True) for h in heads]
    cg_row = [jnp.sum(jnp.where(ri <= ci, g[h], 0.0), axis=0, keepdims=True) for h in heads]
    g_last = [jnp.sum(g[h], axis=0, keepdims=True) for h in heads]
    decay = [jnp.where(incl, jnp.exp(jnp.where(incl, cg_col[h] - cg_row[h], 0.0)), 0.0) for h in heads]
    kb = [ks[h] * beta[h] for h in heads]
    lower = [jnp.where(strict, _mm(kb[h], ks[h], "nt") * decay[h], 0.0) for h in heads]
    eye = jnp.where(diag, 1.0, 0.0)
    off_diag = [t - eye for t in _unit_lower_inverses(tuple(lower))]
    e_col = [jnp.exp(cg_col[h]) for h in heads]
    vb = [vs[h] * beta[h] for h in heads]
    kbg = [kb[h] * e_col[h] for h in heads]
    u = [vb[h] + _mm(off_diag[h], vb[h], "nn") for h in heads]
    w = [kbg[h] + _mm(off_diag[h], kbg[h], "nn") for h in heads]
    qk = [jnp.where(incl, _mm(qs[h], ks[h], "nt") * decay[h], 0.0) for h in heads]
    v_new = [u[h] - _mm(w[h], states[h], "nn") for h in heads]
    inter = [_mm(qs[h] * e_col[h], states[h], "nn") for h in heads]
    outs = [inter[h] + _mm(qk[h], v_new[h], "nn") for h in heads]
    k_tail = [ks[h] * jnp.exp(g_last[h] - cg_col[h]) for h in heads]
    new_states = [states[h] * jnp.exp(g_last[h]) + _mm(k_tail[h], v_new[h], "tn") for h in heads]
    return (jnp.concatenate(outs, axis=1), *new_states)


def _gdn_chunk_fwd(name, q, k, v, gb, n_heads, head_dim):
    s, gw = q.shape
    n = s // CHUNK
    fn = functools.partial(_gdn_chunk, n_heads, head_dim)

    def body(q_ref, k_ref, v_ref, gb_ref, o_ref, st_ref, st_scr):
        @pl.when(pl.program_id(0) == 0)
        def _():
            st_scr[...] = jnp.zeros_like(st_scr)

        st_ref[...] = st_scr[...]
        res = fn(q_ref[...], k_ref[...], v_ref[...], gb_ref[...], *[st_scr[h] for h in range(n_heads)])
        o_ref[...] = res[0]
        for h in range(n_heads):
            st_scr[h] = res[1 + h]

    row = lambda w: pl.BlockSpec((CHUNK, w), lambda i: (i, 0))
    return pl.pallas_call(
        body, name=name, grid=(n,),
        in_specs=[row(gw), row(gw), row(gw), row(LANES)],
        out_specs=[row(gw), pl.BlockSpec((None, n_heads, head_dim, head_dim), lambda i: (i, 0, 0, 0))],
        out_shape=[SDS((s, gw), f32), SDS((n, n_heads, head_dim, head_dim), f32)],
        scratch_shapes=[pltpu.VMEM((n_heads, head_dim, head_dim), f32)],
        compiler_params=_cp(("arbitrary",)),
    )(q, k, v, gb)


def _gdn_chunk_bwd(name, q, k, v, gb, states, d_out, n_heads, head_dim):
    s, gw = q.shape
    n = s // CHUNK
    fn = functools.partial(_gdn_chunk, n_heads, head_dim)

    def body(q_ref, k_ref, v_ref, gb_ref, st_ref, do_ref, dq_ref, dk_ref, dv_ref, dgb_ref, dst_scr):
        @pl.when(pl.program_id(0) == 0)
        def _():
            dst_scr[...] = jnp.zeros_like(dst_scr)

        _, vjp = jax.vjp(fn, q_ref[...], k_ref[...], v_ref[...], gb_ref[...], *[st_ref[h] for h in range(n_heads)])
        grads = vjp((do_ref[...].astype(f32), *[dst_scr[h] for h in range(n_heads)]))
        dq_ref[...] = grads[0]
        dk_ref[...] = grads[1]
        dv_ref[...] = grads[2]
        dgb_ref[...] = grads[3]
        for h in range(n_heads):
            dst_scr[h] = grads[4 + h]

    row = lambda w: pl.BlockSpec((CHUNK, w), lambda i: (n - 1 - i, 0))
    return pl.pallas_call(
        body, name=name, grid=(n,),
        in_specs=[row(gw), row(gw), row(gw), row(LANES),
                  pl.BlockSpec((None, n_heads, head_dim, head_dim), lambda i: (n - 1 - i, 0, 0, 0)), row(gw)],
        out_specs=[row(gw), row(gw), row(gw), row(LANES)],
        out_shape=[SDS((s, gw), f32), SDS((s, gw), f32), SDS((s, gw), f32), SDS((s, LANES), f32)],
        scratch_shapes=[pltpu.VMEM((n_heads, head_dim, head_dim), f32)],
        compiler_params=_cp(("arbitrary",)),
    )(q, k, v, gb, states, d_out)


def _mem_attn(qm, km, vm):
    width = qm.shape[1]
    hd = width // MEM_HEADS
    lane = lax.broadcasted_iota(jnp.int32, (1, width), 1)
    out = jnp.zeros_like(qm)
    for h in range(MEM_HEADS):
        msk = jnp.where((lane >= h * hd) & (lane < (h + 1) * hd), 1.0, 0.0)
        logits = _mm(qm * msk, km, "nt") * (hd ** -0.5)
        p = jnp.exp(logits - jnp.max(logits, axis=-1, keepdims=True))
        p = p / jnp.sum(p, axis=-1, keepdims=True)
        out = out + _mm(p, vm, "nn") * msk
    return (out,)


def _kv_post(n_heads, pk, pv, pf, b_f):
    lane = lax.broadcasted_iota(jnp.int32, pf.shape, 1)
    log_f = jnp.where(lane < n_heads, -_softplus(-(pf + b_f)), 0.0)
    return pk, pv, log_f


def _cumsum(name, xs, reverse, scale=None):
    s, w = xs[0].shape
    tile = min(CUMSUM_TILE, s)
    n = s // tile

    def body(*refs):
        x_refs, o_ref, carry = refs[:-2], refs[-2], refs[-1]

        @pl.when(pl.program_id(0) == 0)
        def _():
            carry[...] = jnp.zeros_like(carry)

        xv = x_refs[0][...]
        for r in x_refs[1:]:
            xv = xv + r[...]
        if scale is not None:
            xv = xv * scale
        ri = lax.broadcasted_iota(jnp.int32, (tile, tile), 0)
        ci = lax.broadcasted_iota(jnp.int32, (tile, tile), 1)
        tri = jnp.where((ri <= ci) if reverse else (ri >= ci), 1.0, 0.0).astype(bf16)
        x1 = xv.astype(bf16)
        r1 = xv - x1.astype(f32)
        x2 = r1.astype(bf16)
        x3 = (r1 - x2.astype(f32)).astype(bf16)
        acc = carry[...] + _dg(tri, x1, "nn") + _dg(tri, x2, "nn") + _dg(tri, x3, "nn")
        o_ref[...] = acc
        carry[...] += jnp.sum(xv, axis=0, keepdims=True)

    order = (lambda i: (n - 1 - i, 0)) if reverse else (lambda i: (i, 0))
    return pl.pallas_call(
        body, name=name, grid=(n,),
        in_specs=[pl.BlockSpec((tile, w), order)] * len(xs), out_specs=pl.BlockSpec((tile, w), order),
        out_shape=SDS((s, w), f32), scratch_shapes=[pltpu.VMEM((1, w), f32)],
        compiler_params=_cp(("arbitrary",)),
    )(*xs)


LOG2_E = math.log2(math.e)


def _fox_logits(q_ref, k_ref, cr_ref, hh, head_dim, scale, diagonal):
    sl = slice(hh * head_dim, (hh + 1) * head_dim)
    s = _dg(q_ref[:, sl], k_ref[:, sl], "nt") * (scale * LOG2_E) - cr_ref[hh]
    if diagonal:
        tq, tk = s.shape
        ok = lax.broadcasted_iota(jnp.int32, (tq, tk), 1) <= lax.broadcasted_iota(jnp.int32, (tq, tk), 0)
        s = jnp.where(ok, s, NEG_INF)
    return s, sl


def _on_causal_tiles(i, j, fn):
    @pl.when(j < i)
    def _():
        fn(False)

    @pl.when(j == i)
    def _():
        fn(True)


def _fox_fwd(name, q, k, v, crow, n_heads, head_dim):
    s = k.shape[0]
    tq = tk = min(ATT_TILE, s)
    nq, nk = s // tq, s // tk
    scale = head_dim ** -0.5
    hpb = ATT_HEADS
    wb = hpb * head_dim

    def body(q_ref, k_ref, v_ref, cr_ref, o_ref, lse_ref, m_scr, l_scr, acc_scr):
        i, j = pl.program_id(1), pl.program_id(2)

        @pl.when(j == 0)
        def _():
            m_scr[...] = jnp.full_like(m_scr, NEG_INF)
            l_scr[...] = jnp.zeros_like(l_scr)
            acc_scr[...] = jnp.zeros_like(acc_scr)

        def step(diagonal):
            heads = range(hpb)
            sl = [slice(hh * head_dim, (hh + 1) * head_dim) for hh in heads]
            sc = [_fox_logits(q_ref, k_ref, cr_ref, hh, head_dim, scale, diagonal)[0] for hh in heads]
            m_old = [m_scr[hh] for hh in heads]
            m_new = [jnp.maximum(m_old[hh], jnp.max(sc[hh], axis=-1, keepdims=True)) for hh in heads]
            p = [jnp.exp2(sc[hh] - m_new[hh]) for hh in heads]
            alpha = [jnp.exp2(m_old[hh] - m_new[hh]) for hh in heads]
            pv = [_dg(p[hh], v_ref[:, sl[hh]], "nn") for hh in heads]
            for hh in heads:
                l_scr[hh] = alpha[hh] * l_scr[hh] + jnp.sum(p[hh], axis=-1, keepdims=True)
                acc_scr[:, sl[hh]] = alpha[hh] * acc_scr[:, sl[hh]] + pv[hh]
                m_scr[hh] = m_new[hh]

        _on_causal_tiles(i, j, step)

        @pl.when(j == nk - 1)
        def _():
            for hh in range(hpb):
                sl = slice(hh * head_dim, (hh + 1) * head_dim)
                o_ref[:, sl] = (acc_scr[:, sl] / l_scr[hh]).astype(o_ref.dtype)
                lse_ref[hh] = m_scr[hh] + jnp.log2(l_scr[hh])

    return pl.pallas_call(
        body, name=name, grid=(n_heads // hpb, nq, nk),
        in_specs=[
            pl.BlockSpec((tq, wb), lambda g, i, j: (i, g)),
            pl.BlockSpec((tk, wb), lambda g, i, j: (jnp.minimum(j, i), g)),
            pl.BlockSpec((tk, wb), lambda g, i, j: (jnp.minimum(j, i), g)),
            pl.BlockSpec((hpb, 1, tk), lambda g, i, j: (g, 0, jnp.minimum(j, i))),
        ],
        out_specs=[pl.BlockSpec((tq, wb), lambda g, i, j: (i, g)),
                   pl.BlockSpec((hpb, tq, 1), lambda g, i, j: (g, i, 0))],
        out_shape=[SDS((s, n_heads * head_dim), bf16), SDS((n_heads, s, 1), f32)],
        scratch_shapes=[pltpu.VMEM((hpb, tq, 1), f32), pltpu.VMEM((hpb, tq, 1), f32), pltpu.VMEM((tq, wb), f32)],
        compiler_params=_cp(("parallel", "parallel", "arbitrary")),
    )(q, k, v, crow)


def _fox_probs(q_ref, k_ref, v_ref, cr_ref, lse_ref, do_ref, hh, head_dim, scale, diagonal):
    sc, sl = _fox_logits(q_ref, k_ref, cr_ref, hh, head_dim, scale, diagonal)
    return jnp.exp2(sc - lse_ref[hh]), _dg(do_ref[:, sl], v_ref[:, sl], "nt"), sl


def _fox_bwd_dq(name, q, k, v, crow, lse, o, do, n_heads, head_dim):
    s = k.shape[0]
    tq = tk = min(ATT_TILE, s)
    nq, nk = s // tq, s // tk
    scale = head_dim ** -0.5
    hpb = ATT_HEADS
    wb = hpb * head_dim

    def body(q_ref, k_ref, v_ref, cr_ref, lse_ref, o_ref, do_ref, dq_ref, delta_ref, dcc_ref, acc_scr):
        i, j = pl.program_id(1), pl.program_id(2)
        heads = range(hpb)

        @pl.when(j == 0)
        def _():
            prod = o_ref[...].astype(f32) * do_ref[...].astype(f32)
            for hh in heads:
                delta_ref[hh] = jnp.sum(prod[:, hh * head_dim:(hh + 1) * head_dim], axis=-1, keepdims=True)
            dcc_ref[...] = jnp.zeros_like(dcc_ref)
            acc_scr[...] = jnp.zeros_like(acc_scr)

        def step(diagonal):
            pd = [_fox_probs(q_ref, k_ref, v_ref, cr_ref, lse_ref, do_ref, hh, head_dim, scale, diagonal) for hh in heads]
            ds = [pd[hh][0] * (pd[hh][1] - delta_ref[hh]) for hh in heads]
            dqs = [_dg(ds[hh], k_ref[:, pd[hh][2]], "nn") for hh in heads]
            for hh in heads:
                dcc_ref[hh] += jnp.sum(ds[hh], axis=-1, keepdims=True)
                acc_scr[:, pd[hh][2]] += dqs[hh]

        _on_causal_tiles(i, j, step)

        @pl.when(j == nk - 1)
        def _():
            dq_ref[...] = (acc_scr[...] * scale).astype(dq_ref.dtype)

    qspec = pl.BlockSpec((tq, wb), lambda g, i, j: (i, g))
    kspec = pl.BlockSpec((tk, wb), lambda g, i, j: (jnp.minimum(j, i), g))
    cspec = pl.BlockSpec((hpb, tq, 1), lambda g, i, j: (g, i, 0))
    return pl.pallas_call(
        body, name=name, grid=(n_heads // hpb, nq, nk),
        in_specs=[qspec, kspec, kspec,
                  pl.BlockSpec((hpb, 1, tk), lambda g, i, j: (g, 0, jnp.minimum(j, i))), cspec, qspec, qspec],
        out_specs=[qspec, cspec, cspec],
        out_shape=[SDS((s, n_heads * head_dim), bf16), SDS((n_heads, s, 1), f32), SDS((n_heads, s, 1), f32)],
        scratch_shapes=[pltpu.VMEM((tq, wb), f32)],
        compiler_params=_cp(("parallel", "parallel", "arbitrary")),
    )(q, k, v, crow, lse, o, do)


def _fox_bwd_dkv(name, q, k, v, crow, lse, delta, do, n_heads, head_dim):
    s = k.shape[0]
    tq = tk = min(ATT_TILE, s)
    nq, nk = s // tq, s // tk
    scale = head_dim ** -0.5
    hpb = ATT_HEADS
    wb = hpb * head_dim

    def body(q_ref, k_ref, v_ref, cr_ref, lse_ref, delta_ref, do_ref, dk_ref, dv_ref, dc_ref):
        j, i = pl.program_id(1), pl.program_id(2)

        @pl.when(i == 0)
        def _():
            dk_ref[...] = jnp.zeros_like(dk_ref)
            dv_ref[...] = jnp.zeros_like(dv_ref)
            dc_ref[...] = jnp.zeros_like(dc_ref)

        def step(diagonal):
            heads = range(hpb)
            pd = [_fox_probs(q_ref, k_ref, v_ref, cr_ref, lse_ref, do_ref, hh, head_dim, scale, diagonal) for hh in heads]
            ds = [pd[hh][0] * (pd[hh][1] - delta_ref[hh]) for hh in heads]
            dvs = [_dg(pd[hh][0], do_ref[:, pd[hh][2]], "tn") for hh in heads]
            dks = [_dg(ds[hh], q_ref[:, pd[hh][2]], "tn") for hh in heads]
            for hh in heads:
                sl = pd[hh][2]
                dv_ref[:, sl] += dvs[hh]
                dk_ref[:, sl] += dks[hh] * scale
                dc_ref[hh] -= jnp.sum(ds[hh], axis=0, keepdims=True)

        _on_causal_tiles(i, j, step)

    qspec = pl.BlockSpec((tq, wb), lambda g, j, i: (jnp.maximum(i, j), g))
    kspec = pl.BlockSpec((tk, wb), lambda g, j, i: (j, g))
    cspec = pl.BlockSpec((hpb, tq, 1), lambda g, j, i: (g, jnp.maximum(i, j), 0))
    rspec = pl.BlockSpec((hpb, 1, tk), lambda g, j, i: (g, 0, j))
    return pl.pallas_call(
        body, name=name, grid=(n_heads // hpb, nk, nq),
        in_specs=[qspec, kspec, kspec, rspec, cspec, cspec, qspec],
        out_specs=[kspec, kspec, rspec],
        out_shape=[SDS((s, n_heads * head_dim), f32), SDS((s, n_heads * head_dim), f32), SDS((n_heads, 1, s), f32)],
        compiler_params=_cp(("parallel", "parallel", "arbitrary")),
    )(q, k, v, crow, lse, delta, do)


def _final_loss(name, h, target, gain):
    s, d = h.shape
    tile = min(ROW_TILE, s)

    def body(h_ref, t_ref, g_ref, loss_ref, dh_ref, dg_ref):
        @pl.when(pl.program_id(0) == 0)
        def _():
            loss_ref[...] = jnp.zeros_like(loss_ref)
            dg_ref[...] = jnp.zeros_like(dg_ref)

        x, g = h_ref[...], g_ref[...]
        rstd = lax.rsqrt(jnp.mean(x * x, axis=-1, keepdims=True) + EPS)
        xhat = x * rstd
        err = xhat * g - t_ref[...]
        row = jnp.sum(err * err, axis=-1, keepdims=True) * (0.5 / d)
        loss_ref[...] += jnp.sum(row, axis=0, keepdims=True)
        dy = err * (1.0 / d)
        dg_ref[...] += jnp.sum(dy * xhat, axis=0, keepdims=True)
        dxhat = dy * g
        dh_ref[...] = rstd * (dxhat - xhat * jnp.mean(dxhat * xhat, axis=-1, keepdims=True))

    return pl.pallas_call(
        body, name=name, grid=(s // tile,),
        in_specs=[pl.BlockSpec((tile, d), lambda i: (i, 0)), pl.BlockSpec((tile, d), lambda i: (i, 0)),
                  pl.BlockSpec((1, d), lambda i: (0, 0))],
        out_specs=[pl.BlockSpec((1, LANES), lambda i: (0, 0)), pl.BlockSpec((tile, d), lambda i: (i, 0)),
                   pl.BlockSpec((1, d), lambda i: (0, 0))],
        out_shape=[SDS((1, LANES), f32), SDS((s, d), f32), SDS((1, d), f32)],
        compiler_params=_cp(("arbitrary",)),
    )(h, target, gain)


def _adamw(name, parts, w, m, v):
    r, c = w.shape
    rb = r
    for cand in (256, 128, 64, 32, 16):
        if r % cand == 0 and cand * c * 4 <= (2 << 20):
            rb = cand
            break
    n_parts = parts.shape[0]

    def body(p_ref, w_ref, m_ref, v_ref, g_out, d_out, m_out, v_out):
        g = p_ref[0].astype(f32)
        for k in range(1, n_parts):
            g = g + p_ref[k].astype(f32)
        m_new = ADAM_B1 * m_ref[...] + (1.0 - ADAM_B1) * g
        v_new = ADAM_B2 * v_ref[...] + (1.0 - ADAM_B2) * (g * g)
        m_hat = m_new / (1.0 - ADAM_B1 ** ADAM_STEP)
        v_hat = v_new / (1.0 - ADAM_B2 ** ADAM_STEP)
        g_out[...] = g
        d_out[...] = -ADAM_LR * (m_hat / (jnp.sqrt(v_hat) + ADAM_EPS) + ADAM_WD * w_ref[...])
        m_out[...] = m_new
        v_out[...] = v_new

    blk = pl.BlockSpec((rb, c), lambda i: (i, 0))
    return pl.pallas_call(
        body, name=name, grid=(r // rb,),
        in_specs=[pl.BlockSpec((n_parts, rb, c), lambda i: (0, i, 0)), blk, blk, blk],
        out_specs=[blk, blk, blk, blk],
        out_shape=[SDS((r, c), f32)] * 4,
        compiler_params=_cp(("parallel",)),
    )(parts, w, m, v)


def _position():
    x, y, c = lax.axis_index("x"), lax.axis_index("y"), lax.axis_index("c")
    return x, y, c


def _all_gather(name, shards):
    n = len(shards)

    def body(*refs):
        ins, outs = refs[:n], refs[n:2 * n]
        send_sems, recv_sems, local_sems = refs[2 * n:]
        x, y, c = _position()
        me, sibling = (x, y, c), (x, y, 1 - c)
        chips = [(1 - x, y), (x, 1 - y), (1 - x, 1 - y)]

        def slot(a, block):
            px, py, pc = block
            return outs[a].at[4 * px + 2 * py + pc]

        def copy(a, k, block, to, src=None):
            return pltpu.make_async_remote_copy(
                src_ref=slot(a, block) if src is None else src, dst_ref=slot(a, block),
                send_sem=send_sems.at[a, k], recv_sem=recv_sems.at[a, k], device_id=to, device_id_type=MESH)

        local = [pltpu.make_async_copy(ins[a], slot(a, me), local_sems.at[a]) for a in range(n)]
        for cp in local:
            cp.start()
        started = []
        for a in range(n):
            first = [copy(a, 0, me, sibling, src=ins[a])]
            first += [copy(a, 1 + j, me, (*chip, c), src=ins[a]) for j, chip in enumerate(chips)]
            for cp in first:
                cp.start()
            started += first
        for a in range(n):
            for j, chip in enumerate(chips):
                copy(a, 1 + j, (*chip, c), me).wait_recv()
                passed = copy(a, 4 + j, (*chip, c), sibling)
                passed.start()
                started.append(passed)
        for a in range(n):
            copy(a, 0, sibling, me).wait_recv()
            for j, chip in enumerate(chips):
                copy(a, 4 + j, (*chip, 1 - c), me).wait_recv()
        for cp in started:
            cp.wait_send()
        for cp in local:
            cp.wait()

    any_spec = pl.BlockSpec(memory_space=pl.ANY)
    outs = pl.pallas_call(
        body, name=name,
        in_specs=[any_spec] * n, out_specs=[any_spec] * n,
        out_shape=[SDS((N_DEV, *a.shape), a.dtype) for a in shards],
        scratch_shapes=[pltpu.SemaphoreType.DMA((n, 7)), pltpu.SemaphoreType.DMA((n, 7)), pltpu.SemaphoreType.DMA((n,))],
    )(*shards)
    return list(outs)


N_CHIPS = N_DEV // 2


def _sibling_exchange(name, arrs):
    n = len(arrs)

    def body(*refs):
        ins, outs = refs[:n], refs[n:2 * n]
        send_sems, recv_sems = refs[2 * n:]
        x, y, c = _position()
        copies = [pltpu.make_async_remote_copy(
            src_ref=ins[a], dst_ref=outs[a], send_sem=send_sems.at[a], recv_sem=recv_sems.at[a],
            device_id=(x, y, 1 - c), device_id_type=MESH) for a in range(n)]
        for cp in copies:
            cp.start()
        for cp in copies:
            cp.wait_recv()
        for cp in copies:
            cp.wait_send()

    any_spec = pl.BlockSpec(memory_space=pl.ANY)
    outs = pl.pallas_call(
        body, name=name,
        in_specs=[any_spec] * n, out_specs=[any_spec] * n, out_shape=[SDS(a.shape, a.dtype) for a in arrs],
        scratch_shapes=[pltpu.SemaphoreType.DMA((n,)), pltpu.SemaphoreType.DMA((n,))],
    )(*arrs)
    return list(outs)


def _chip_exchange(name, parts):
    n = len(parts)

    def body(*refs):
        ins, outs = refs[:n], refs[n:2 * n]
        send_sems, recv_sems, local_sems = refs[2 * n:]
        x, y, c = _position()
        my_chip = 2 * x + y
        local = [pltpu.make_async_copy(ins[a].at[my_chip], outs[a].at[my_chip], local_sems.at[a]) for a in range(n)]
        for cp in local:
            cp.start()
        sends, recvs = [], []
        for a in range(n):
            for k in range(1, N_CHIPS):
                px = (1 - x) if (k >> 1) & 1 else x
                py = (1 - y) if k & 1 else y
                peer_chip = 2 * px + py
                send = pltpu.make_async_remote_copy(
                    src_ref=ins[a].at[peer_chip], dst_ref=outs[a].at[my_chip],
                    send_sem=send_sems.at[a, k - 1], recv_sem=recv_sems.at[a, k - 1],
                    device_id=(px, py, c), device_id_type=MESH)
                send.start()
                sends.append(send)
                recvs.append(pltpu.make_async_remote_copy(
                    src_ref=ins[a].at[peer_chip], dst_ref=outs[a].at[peer_chip],
                    send_sem=send_sems.at[a, k - 1], recv_sem=recv_sems.at[a, k - 1],
                    device_id=(px, py, c), device_id_type=MESH))
        for cp in recvs:
            cp.wait_recv()
        for cp in sends:
            cp.wait_send()
        for cp in local:
            cp.wait()

    any_spec = pl.BlockSpec(memory_space=pl.ANY)
    outs = pl.pallas_call(
        body, name=name,
        in_specs=[any_spec] * n, out_specs=[any_spec] * n,
        out_shape=[SDS(a.shape, a.dtype) for a in parts],
        scratch_shapes=[pltpu.SemaphoreType.DMA((n, N_CHIPS - 1)), pltpu.SemaphoreType.DMA((n, N_CHIPS - 1)),
                        pltpu.SemaphoreType.DMA((n,))],
    )(*parts)
    return list(outs)


def _reduce_exchange(stacks):
    core = lax.axis_index("c")
    by_core = [st.reshape(N_CHIPS, 2, *st.shape[1:]) for st in stacks]
    mine = [lax.dynamic_index_in_dim(v, core, axis=1, keepdims=False) for v in by_core]
    theirs = _sibling_exchange(
        "exchange_sibling", [lax.dynamic_index_in_dim(v, 1 - core, axis=1, keepdims=False) for v in by_core])
    summed = []
    for a, (m, t) in enumerate(zip(mine, theirs)):
        cols = m.shape[-1]
        (both,) = _rowwise(f"exchange_add_{a}", lambda p, q: (p + q,), [m.reshape(-1, cols), t.reshape(-1, cols)], [],
                           [(cols, m.dtype)])
        summed.append(both.reshape(m.shape))
    return _chip_exchange("exchange_chips", summed)


def _rows_from_shards(g):
    n, l, r, c = g.shape
    return g.transpose(1, 0, 2, 3).reshape(l, n * r, c)


def _rows_to_shards(w):
    l, rows, c = w.shape
    return w.reshape(l, N_DEV, rows // N_DEV, c).transpose(1, 0, 2, 3)


def _pad_lanes(a, width):
    return jnp.pad(a, [(0, 0)] * (a.ndim - 1) + [(0, width - a.shape[-1])])


def _row(vec, width=None):
    vec = vec.reshape(1, -1)
    return vec if width is None else _pad_lanes(vec, width)


class _SmallPack:
    def __init__(self, shapes):
        self.shapes, self.offsets, at = shapes, {}, 0
        for name, shape in shapes.items():
            last = shape[-1]
            lead = int(math.prod(shape[:-1]))
            rows = lead * (last // LANES) if last >= LANES else lead
            self.offsets[name] = (at, rows)
            at += rows
        self.rows = -(-at // 8) * 8

    def pack(self, values):
        pieces = []
        for name, shape in self.shapes.items():
            val = values[name].astype(f32)
            if shape[-1] >= LANES:
                pieces.append(val.reshape(-1, LANES))
            else:
                pieces.append(_pad_lanes(val.reshape(-1, shape[-1]), LANES))
        used = sum(p.shape[0] for p in pieces)
        if used < self.rows:
            pieces.append(jnp.zeros((self.rows - used, LANES), f32))
        return jnp.concatenate(pieces, axis=0)

    def unpack(self, packed):
        out = {}
        for name, shape in self.shapes.items():
            at, rows = self.offsets[name]
            blk = packed[at:at + rows]
            out[name] = blk.reshape(shape) if shape[-1] >= LANES else blk[:, :shape[-1]].reshape(shape)
        return out


def kernel(x, mem, ffn1_norm, ffn1_w_gate_up, ffn1_w_down, mix_norm, ffn2_norm, ffn2_w_gate_up, ffn2_w_down, gdn_w_in, gdn_conv, gdn_A_log, gdn_dt_bias, gdn_out_norm, fox_w_in, w_out, mem_norm, mem_w_kv, kv_norm, kv_w, kv_b_f, final_norm, loss_target, m_ffn1_norm, m_ffn1_w_gate_up, m_ffn1_w_down, m_mix_norm, m_ffn2_norm, m_ffn2_w_gate_up, m_ffn2_w_down, m_gdn_w_in, m_gdn_conv, m_gdn_A_log, m_gdn_dt_bias, m_gdn_out_norm, m_fox_w_in, m_w_out, m_mem_norm, m_mem_w_kv, m_kv_norm, m_kv_w, m_kv_b_f, m_final_norm, v_ffn1_norm, v_ffn1_w_gate_up, v_ffn1_w_down, v_mix_norm, v_ffn2_norm, v_ffn2_w_gate_up, v_ffn2_w_down, v_gdn_w_in, v_gdn_conv, v_gdn_A_log, v_gdn_dt_bias, v_gdn_out_norm, v_fox_w_in, v_w_out, v_mem_norm, v_mem_w_kv, v_kv_norm, v_kv_w, v_kv_b_f, v_final_norm):
    weights = dict(ffn1_norm=ffn1_norm, ffn1_w_gate_up=ffn1_w_gate_up, ffn1_w_down=ffn1_w_down, mix_norm=mix_norm,
                   ffn2_norm=ffn2_norm, ffn2_w_gate_up=ffn2_w_gate_up, ffn2_w_down=ffn2_w_down, gdn_w_in=gdn_w_in,
                   gdn_conv=gdn_conv, gdn_A_log=gdn_A_log, gdn_dt_bias=gdn_dt_bias, gdn_out_norm=gdn_out_norm,
                   fox_w_in=fox_w_in, w_out=w_out, mem_norm=mem_norm, mem_w_kv=mem_w_kv, kv_norm=kv_norm, kv_w=kv_w,
                   kv_b_f=kv_b_f, final_norm=final_norm)
    mom_m = dict(ffn1_norm=m_ffn1_norm, ffn1_w_gate_up=m_ffn1_w_gate_up, ffn1_w_down=m_ffn1_w_down, mix_norm=m_mix_norm,
                 ffn2_norm=m_ffn2_norm, ffn2_w_gate_up=m_ffn2_w_gate_up, ffn2_w_down=m_ffn2_w_down, gdn_w_in=m_gdn_w_in,
                 gdn_conv=m_gdn_conv, gdn_A_log=m_gdn_A_log, gdn_dt_bias=m_gdn_dt_bias, gdn_out_norm=m_gdn_out_norm,
                 fox_w_in=m_fox_w_in, w_out=m_w_out, mem_norm=m_mem_norm, mem_w_kv=m_mem_w_kv, kv_norm=m_kv_norm,
                 kv_w=m_kv_w, kv_b_f=m_kv_b_f, final_norm=m_final_norm)
    mom_v = dict(ffn1_norm=v_ffn1_norm, ffn1_w_gate_up=v_ffn1_w_gate_up, ffn1_w_down=v_ffn1_w_down, mix_norm=v_mix_norm,
                 ffn2_norm=v_ffn2_norm, ffn2_w_gate_up=v_ffn2_w_gate_up, ffn2_w_down=v_ffn2_w_down, gdn_w_in=v_gdn_w_in,
                 gdn_conv=v_gdn_conv, gdn_A_log=v_gdn_A_log, gdn_dt_bias=v_gdn_dt_bias, gdn_out_norm=v_gdn_out_norm,
                 fox_w_in=v_fox_w_in, w_out=v_w_out, mem_norm=v_mem_norm, mem_w_kv=v_mem_w_kv, kv_norm=v_kv_norm,
                 kv_w=v_kv_w, kv_b_f=v_kv_b_f, final_norm=v_final_norm)
    names = list(weights)
    small_names = [n for n in names if weights[n].shape == mom_m[n].shape and n in (
        "ffn1_norm", "mix_norm", "ffn2_norm", "gdn_A_log", "gdn_dt_bias", "gdn_out_norm", "mem_norm", "kv_norm",
        "kv_b_f", "final_norm")]
    big_names = [n for n in names if n not in small_names]

    h = x[0]
    target = loss_target[0]
    mem_tokens = mem[0]
    s, d = h.shape
    depth = ffn1_norm.shape[0]
    n_a = gdn_w_in.shape[0]
    n_heads, head_dim = gdn_A_log.shape[1], gdn_out_norm.shape[1]
    gw = n_heads * head_dim
    a_in = gdn_w_in.shape[2]
    mem_w = a_in - 4 * gw - 2 * n_heads
    a_in_pad = 4 * gw + mem_w + LANES
    kv_width = kv_w.shape[1]
    kv_pad = 2 * gw + LANES
    fh = ffn1_w_down.shape[1] * N_DEV

    def permute_in(w):
        ab = w[..., 4 * gw:4 * gw + 2 * n_heads]
        return jnp.concatenate([w[..., :4 * gw], w[..., 4 * gw + 2 * n_heads:], _pad_lanes(ab, LANES)], axis=-1)

    def unpermute_in(w):
        return jnp.concatenate([w[..., :4 * gw], w[..., 4 * gw + mem_w:4 * gw + mem_w + 2 * n_heads],
                                w[..., 4 * gw:4 * gw + mem_w]], axis=-1)

    gathered = _all_gather("gather_weights", [
        ffn1_w_gate_up.astype(bf16), ffn1_w_down.astype(bf16), ffn2_w_gate_up.astype(bf16), ffn2_w_down.astype(bf16),
        permute_in(gdn_w_in).astype(bf16), fox_w_in.astype(bf16), w_out.astype(bf16), mem_w_kv.astype(bf16),
        _pad_lanes(kv_w, kv_pad).astype(bf16)[None], gdn_conv])
    wgu1, wd1_s, wgu2, wd2_s, win_s, wfox_s, wout_s, wmem_s, wkv_s, conv_s = gathered
    wd1, wd2 = _rows_from_shards(wd1_s), _rows_from_shards(wd2_s)
    win, wfox, wout = _rows_from_shards(win_s), _rows_from_shards(wfox_s), _rows_from_shards(wout_s)
    wmem = _rows_from_shards(wmem_s)
    wmem_cat = wmem.transpose(1, 0, 2).reshape(d, depth * 2 * mem_w)
    wkv = _rows_from_shards(wkv_s)[0]
    conv_w = conv_s.transpose(1, 2, 0, 3).reshape(n_a, gdn_conv.shape[1], 3 * gw)

    a_log_rows = [_row(gdn_A_log[l], LANES) for l in range(n_a)]
    dt_rows = [_row(gdn_dt_bias[l], LANES) for l in range(n_a)]
    onorm_rows = [_row(gdn_out_norm[l]) for l in range(n_a)]
    b_f_row = _row(kv_b_f, LANES)

    (mem_n,) = _rowwise("mem_norm", lambda t, g: (_rms(t, g),), [mem_tokens], [_row(mem_norm)], [(d, bf16)])
    mem_kv = _matmul("mem_kv", mem_n, wmem_cat, "nn", f32)

    saved = []
    shared = None
    for l in range(depth):
        rec = {"h0": h}
        h1, gate1, up1 = _ffn_fwd(f"ffn1_fwd_{l}", h, _row(ffn1_norm[l]), wgu1, wd1, l)
        if l < n_a:
            u, proj = _norm_matmul(f"gdn_in_{l}", h1, _row(mix_norm[l]), win[l], f32)
        else:
            u, proj = _norm_matmul(f"fox_in_{l}", h1, _row(mix_norm[l]), wfox[l - n_a], bf16)
        rec.update(h1=h1, u=u, ffn1=(gate1, up1))
        if l < n_a:
            yc = _conv_fwd(f"conv_fwd_{l}", V(proj, cb=0, w=3 * gw), conv_w[l])
            ab_view = V(proj, cb=(4 * gw + mem_w) // LANES, w=LANES)
            q, k, v, gb = _rowwise(f"gdn_pre_{l}", functools.partial(_gdn_pre, n_heads, head_dim),
                                   [yc, ab_view], [a_log_rows[l], dt_rows[l]],
                                   [(gw, f32), (gw, f32), (gw, f32), (LANES, f32)])
            o, states = _gdn_chunk_fwd(f"gdn_chunk_fwd_{l}", q, k, v, gb, n_heads, head_dim)
            z_view = V(proj, cb=3, w=gw)
            (main,) = _rowwise(f"gdn_post_{l}", functools.partial(_gdn_post, n_heads, head_dim),
                               [o, z_view], [onorm_rows[l]], [(gw, bf16)])
            qmem_view = V(proj, cb=4 * gw // mem_w, w=mem_w)
            rec.update(proj=proj, yc=yc, q=q, k=k, v=v, gb=gb, o=o, states=states)
        else:
            sk, sv, crow = shared["k"], shared["v"], shared["crow"]
            main, lse = _fox_fwd(f"fox_fwd_{l}", proj, sk, sv, crow, n_heads, head_dim)
            qmem_view = V(proj, cb=gw // mem_w, w=mem_w)
            rec.update(proj=proj, lse=lse)
        km = V(mem_kv, cb=2 * l, w=mem_w)
        vm = V(mem_kv, cb=2 * l + 1, w=mem_w)
        (mem_out,) = _rowwise(f"mem_attn_{l}", _mem_attn, [qmem_view], [km, vm], [(mem_w, bf16)])
        cat = jnp.concatenate([main, mem_out], axis=1)
        h2 = _matmul(f"out_proj_{l}", cat, wout[l], "nn", f32, add=h1)
        h3, gate2, up2 = _ffn_fwd(f"ffn2_fwd_{l}", h2, _row(ffn2_norm[l]), wgu2, wd2, l)
        rec.update(cat=cat, h2=h2, qmem=qmem_view, ffn2=(gate2, up2))
        saved.append(rec)
        h = h3
        if l == n_a - 1:
            hn, p = _norm_matmul("kv_proj", h, _row(kv_norm), wkv, f32)
            pk, pv, pf = V(p, cb=0, w=gw), V(p, cb=1, w=gw), V(p, cb=2 * gw // LANES, w=LANES)
            sk, sv, log_f = _rowwise("kv_post", functools.partial(_kv_post, n_heads), [pk, pv, pf], [b_f_row],
                                     [(gw, bf16), (gw, bf16), (LANES, f32)])
            cum = _cumsum("forget_cumsum", [log_f], reverse=False, scale=LOG2_E)
            c_heads = cum[:, :n_heads].T
            shared = dict(k=sk, v=sv, crow=c_heads.reshape(n_heads, 1, s), h=h, hn=hn, p=p, views=(pk, pv, pf))

    loss_part, dh, d_final = _final_loss("final_loss", h, target, _row(final_norm))
    loss = lax.psum(loss_part[0, 0], ("x", "y", "c"))

    grads = {}
    per_layer = {n: [None] * depth for n in ("ffn1_norm", "mix_norm", "ffn2_norm", "ffn1_gu", "ffn1_d", "ffn2_gu",
                                             "ffn2_d", "w_out")}
    per_a = {n: [None] * n_a for n in ("gdn_w_in", "gdn_conv", "gdn_A_log", "gdn_dt_bias", "gdn_out_norm")}
    per_b = {"fox_w_in": [None] * (depth - n_a)}
    d_mem_kv = [None] * depth
    fox_grads = []

    def ffn_backward(tag, l, h_in, d_out, gain, kept, wgu, wd):
        parts, dwg, dwu, dwd = _ffn_bwd(f"{tag}_bwd_{l}", h_in, d_out, _row(gain), kept[0], kept[1], wgu, wd, l)
        nh = parts.shape[0]
        (d_in,), (d_gain,) = _rowwise_vjp(
            f"{tag}_norm_bwd_{l}", lambda t, g: (_rms(t, g),), [h_in], [_row(gain)],
            [[V(parts, lead=t) for t in range(nh)]], [f32], add=d_out)
        return d_in, d_gain, jnp.concatenate([dwg, dwu], axis=0), dwd

    for l in reversed(range(depth)):
        rec = saved[l]
        if l == n_a - 1:
            dk_list = [V(g["dk"]) for g in fox_grads]
            dv_list = [V(g["dv"]) for g in fox_grads]
            dc_parts = [_pad_lanes(part.reshape(n_heads, s).T, LANES) for g in fox_grads for part in g["dc"]]
            d_log_f = _cumsum("forget_cumsum_bwd", dc_parts, reverse=True)
            pk, pv, pf = shared["views"]
            (dpk, dpv, dpf), (d_bf,) = _rowwise_vjp(
                "kv_post_bwd", functools.partial(_kv_post, n_heads), [pk, pv, pf], [b_f_row],
                [dk_list, dv_list, [d_log_f]], [bf16, bf16, bf16])
            dp = jnp.concatenate([dpk, dpv, dpf], axis=1)
            d_hn = _matmul("kv_proj_dx", dp, wkv, "nt", f32)
            grads["kv_w"] = _matmul("kv_proj_dw", shared["hn"], dp, "tn", f32)[:, :kv_width]
            (dh,), (d_kvn,) = _rowwise_vjp("kv_norm_bwd", lambda t, g: (_rms(t, g),), [shared["h"]], [_row(kv_norm)],
                                           [[d_hn]], [f32], add=dh)
            grads["kv_norm"] = d_kvn.reshape(-1)
            grads["kv_b_f"] = d_bf[0, :n_heads]

        dh2, per_layer["ffn2_norm"][l], per_layer["ffn2_gu"][l], per_layer["ffn2_d"][l] = ffn_backward(
            "ffn2", l, rec["h2"], dh, ffn2_norm[l], rec["ffn2"], wgu2, wd2)
        d_cat = _matmul(f"out_proj_dx_{l}", dh2, wout[l], "nt", f32)
        per_layer["w_out"][l] = _matmul(f"out_proj_dw_{l}", rec["cat"], dh2, "tn", f32)
        d_main = V(d_cat, cb=0, w=gw)
        d_memo = V(d_cat, cb=gw // mem_w, w=mem_w)
        km, vm = V(mem_kv, cb=2 * l, w=mem_w), V(mem_kv, cb=2 * l + 1, w=mem_w)
        (dqmem,), (dkm, dvm) = _rowwise_vjp(f"mem_attn_bwd_{l}", _mem_attn, [rec["qmem"]], [km, vm], [[d_memo]], [bf16])
        d_mem_kv[l] = jnp.concatenate([dkm, dvm], axis=1)
        if l < n_a:
            proj = rec["proj"]
            z_view = V(proj, cb=3, w=gw)
            (d_o, d_z), (d_onorm,) = _rowwise_vjp(
                f"gdn_post_bwd_{l}", functools.partial(_gdn_post, n_heads, head_dim), [rec["o"], z_view],
                [onorm_rows[l]], [[d_main]], [f32, bf16])
            dq, dk, dv, dgb = _gdn_chunk_bwd(f"gdn_chunk_bwd_{l}", rec["q"], rec["k"], rec["v"], rec["gb"],
                                             rec["states"], d_o, n_heads, head_dim)
            ab_view = V(proj, cb=(4 * gw + mem_w) // LANES, w=LANES)
            (d_yc, d_ab), (d_alog, d_dt) = _rowwise_vjp(
                f"gdn_pre_bwd_{l}", functools.partial(_gdn_pre, n_heads, head_dim), [rec["yc"], ab_view],
                [a_log_rows[l], dt_rows[l]], [[dq], [dk], [dv], [dgb]], [f32, bf16])
            d_qkv, d_conv = _conv_bwd(f"conv_bwd_{l}", V(proj, cb=0, w=3 * gw), d_yc, conv_w[l])
            d_proj = jnp.concatenate([d_qkv, d_z, dqmem, d_ab], axis=1)
            du = _matmul(f"gdn_in_dx_{l}", d_proj, win[l], "nt", f32)
            per_a["gdn_w_in"][l] = unpermute_in(_matmul(f"gdn_in_dw_{l}", rec["u"], d_proj, "tn", f32))
            per_a["gdn_conv"][l] = d_conv
            per_a["gdn_A_log"][l] = d_alog[0, :n_heads]
            per_a["gdn_dt_bias"][l] = d_dt[0, :n_heads]
            per_a["gdn_out_norm"][l] = d_onorm[0]
        else:
            proj = rec["proj"]
            sk, sv, crow = shared["k"], shared["v"], shared["crow"]
            dq, delta, dc_col = _fox_bwd_dq(f"fox_dq_{l}", proj, sk, sv, crow, rec["lse"], rec["cat"], d_cat,
                                            n_heads, head_dim)
            dk, dv, dc_row = _fox_bwd_dkv(f"fox_dkv_{l}", proj, sk, sv, crow, rec["lse"], delta, d_cat,
                                          n_heads, head_dim)
            fox_grads.append(dict(dk=dk, dv=dv, dc=(dc_row, dc_col)))
            d_proj = jnp.concatenate([dq, dqmem], axis=1)
            du = _matmul(f"fox_in_dx_{l}", d_proj, wfox[l - n_a], "nt", f32)
            per_b["fox_w_in"][l - n_a] = _matmul(f"fox_in_dw_{l}", rec["u"], d_proj, "tn", f32)
        (dh1,), (d_mix,) = _rowwise_vjp(f"mix_norm_bwd_{l}", lambda t, g: (_rms(t, g),), [rec["h1"]],
                                        [_row(mix_norm[l])], [[du]], [f32], add=dh2)
        per_layer["mix_norm"][l] = d_mix
        dh, per_layer["ffn1_norm"][l], per_layer["ffn1_gu"][l], per_layer["ffn1_d"][l] = ffn_backward(
            "ffn1", l, rec["h0"], dh1, ffn1_norm[l], rec["ffn1"], wgu1, wd1)

    grad_x = dh[None]

    d_mem_kv_cat = jnp.concatenate(d_mem_kv, axis=1)
    d_wmem_cat = _matmul("mem_kv_dw", mem_n, d_mem_kv_cat, "tn", f32)
    d_mem_n = _matmul("mem_kv_dx", d_mem_kv_cat, wmem_cat, "nt", f32)
    _, (d_memnorm,) = _rowwise_vjp("mem_norm_bwd", lambda t, g: (_rms(t, g),), [mem_tokens], [_row(mem_norm)],
                                   [[d_mem_n]], [None])

    def gu_stack(per):
        return jnp.stack(per, axis=1).astype(bf16)

    stacks = dict(
        ffn1_w_gate_up=gu_stack(per_layer["ffn1_gu"]),
        ffn1_w_down=_rows_to_shards(jnp.stack(per_layer["ffn1_d"])).astype(bf16),
        ffn2_w_gate_up=gu_stack(per_layer["ffn2_gu"]),
        ffn2_w_down=_rows_to_shards(jnp.stack(per_layer["ffn2_d"])).astype(bf16),
        gdn_w_in=_rows_to_shards(jnp.stack(per_a["gdn_w_in"])).astype(bf16),
        gdn_conv=jnp.stack(per_a["gdn_conv"]).reshape(n_a, -1, N_DEV, 3 * gw // N_DEV).transpose(2, 0, 1, 3),
        fox_w_in=_rows_to_shards(jnp.stack(per_b["fox_w_in"])).astype(bf16),
        w_out=_rows_to_shards(jnp.stack(per_layer["w_out"])).astype(bf16),
        mem_w_kv=_rows_to_shards(d_wmem_cat.reshape(d, depth, 2 * mem_w).transpose(1, 0, 2)).astype(bf16),
        kv_w=_rows_to_shards(grads["kv_w"][None])[:, 0].astype(bf16),
    )
    received = dict(zip(big_names, _reduce_exchange([stacks[n] for n in big_names])))

    small_shapes = {n: weights[n].shape for n in small_names}
    pack = _SmallPack(small_shapes)
    small_grads = dict(
        ffn1_norm=jnp.concatenate(per_layer["ffn1_norm"], axis=0), mix_norm=jnp.concatenate(per_layer["mix_norm"], axis=0),
        ffn2_norm=jnp.concatenate(per_layer["ffn2_norm"], axis=0), gdn_A_log=jnp.stack(per_a["gdn_A_log"]),
        gdn_dt_bias=jnp.stack(per_a["gdn_dt_bias"]), gdn_out_norm=jnp.stack(per_a["gdn_out_norm"]),
        mem_norm=d_memnorm.reshape(-1), kv_norm=grads["kv_norm"], kv_b_f=grads["kv_b_f"], final_norm=d_final.reshape(-1))
    (small_parts,) = _all_gather("gather_small_grads", [pack.pack(small_grads)])

    out_g, out_d, out_m, out_v = {}, {}, {}, {}
    for n in big_names:
        shape = weights[n].shape
        c = shape[-1]
        parts = received[n].reshape(N_CHIPS, -1, c)
        res = _adamw(f"adamw_{n}", parts, weights[n].reshape(-1, c), mom_m[n].reshape(-1, c), mom_v[n].reshape(-1, c))
        out_g[n], out_d[n], out_m[n], out_v[n] = [r.reshape(shape) for r in res]
    res = _adamw("adamw_small", small_parts, pack.pack({n: weights[n] for n in small_names}),
                 pack.pack({n: mom_m[n] for n in small_names}), pack.pack({n: mom_v[n] for n in small_names}))
    for dst, packed in zip((out_g, out_d, out_m, out_v), res):
        dst.update(pack.unpack(packed))

    return (loss, grad_x, *[out_g[n] for n in names], *[out_d[n] for n in names],
            *[out_m[n] for n in names], *[out_v[n] for n in names])
```

```python
import functools
import math

import jax
import jax.numpy as jnp
from jax import lax
from jax.experimental import pallas as pl
from jax.experimental.pallas import tpu as pltpu

f32 = jnp.float32
bf16 = jnp.bfloat16
SDS = jax.ShapeDtypeStruct

N_DEV = 8
MEM_HEADS = 4
CHUNK = 64
LANES = 128
EPS = 1e-6
NEG_INF = -1e30
ADAM_LR = 0.001
ADAM_B1 = 0.9
ADAM_B2 = 0.999
ADAM_EPS = 1e-08
ADAM_WD = 0.01
ADAM_STEP = 10

ROW_TILE = 512
MM_TILE = 1024
NORM_MM_TILE = 512
FFN_FWD_TILE = 1024
FFN_BWD_TILE = 512
ATT_TILE = 1024
ATT_HEADS = 3
CUMSUM_TILE = 256
VMEM_LIMIT = 56 * 1024 * 1024

MESH = pl.DeviceIdType.MESH


def _cp(sem=None):
    return pltpu.CompilerParams(dimension_semantics=sem, vmem_limit_bytes=VMEM_LIMIT)


_DIMS = {"nn": (((1,), (0,)), ((), ())), "nt": (((1,), (1,)), ((), ())), "tn": (((0,), (0,)), ((), ()))}


def _dg(a, b, mode):
    return lax.dot_general(a.astype(bf16), b.astype(bf16), _DIMS[mode], preferred_element_type=f32)


@functools.partial(jax.custom_vjp, nondiff_argnums=(2,))
def _mm(a, b, mode):
    return _dg(a, b, mode)


def _mm_fwd(a, b, mode):
    return _dg(a, b, mode), (a, b)


def _mm_bwd(mode, res, ct):
    a, b = res
    if mode == "nn":
        da, db = _dg(ct, b, "nt"), _dg(a, ct, "tn")
    elif mode == "nt":
        da, db = _dg(ct, b, "nn"), _dg(ct, a, "tn")
    else:
        da, db = _dg(b, ct, "nt"), _dg(a, ct, "nn")
    return da.astype(a.dtype), db.astype(b.dtype)


_mm.defvjp(_mm_fwd, _mm_bwd)


def _split_bf16(x):
    hi = x.astype(bf16)
    return hi, (x - hi.astype(f32)).astype(bf16)


def _dgh(a, b, mode="nn"):
    a_hi, a_lo = _split_bf16(a)
    b_hi, b_lo = _split_bf16(b)
    dims = _DIMS[mode]
    return (lax.dot_general(a_hi, b_hi, dims, preferred_element_type=f32)
            + lax.dot_general(a_hi, b_lo, dims, preferred_element_type=f32)
            + lax.dot_general(a_lo, b_hi, dims, preferred_element_type=f32))


@jax.custom_vjp
def _unit_lower_inverses(lows):
    c = lows[0].shape[0]
    ri = lax.broadcasted_iota(jnp.int32, (c, c), 0)
    ci = lax.broadcasted_iota(jnp.int32, (c, c), 1)
    eye = jnp.where(ri == ci, 1.0, 0.0)
    xs = [-low for low in lows]
    rs = [eye + x for x in xs]
    for _ in range(int(math.log2(c)) - 1):
        xs = [_dgh(x, x) for x in xs]
        rs = [r + _dgh(r, x) for r, x in zip(rs, xs)]
    return tuple(rs)


def _uli_fwd(lows):
    ts = _unit_lower_inverses(lows)
    return ts, ts


def _uli_bwd(ts, cts):
    mids = [_dgh(ct, t, "nt") for t, ct in zip(ts, cts)]
    return (tuple(-_dgh(t, m, "tn") for t, m in zip(ts, mids)),)


_unit_lower_inverses.defvjp(_uli_fwd, _uli_bwd)


@functools.partial(jax.custom_vjp, nondiff_argnums=(1,))
def _split_lanes(x, width):
    return tuple(x[:, i * width:(i + 1) * width] for i in range(x.shape[1] // width))


def _split_fwd(x, width):
    return _split_lanes(x, width), None


def _split_bwd(width, _, cts):
    return (jnp.concatenate(list(cts), axis=1),)


_split_lanes.defvjp(_split_fwd, _split_bwd)


def _sigmoid(x):
    return 1.0 / (1.0 + jnp.exp(-x))


def _silu(x):
    return x * _sigmoid(x)


def _softplus(x):
    return jnp.maximum(x, 0.0) + jnp.log1p(jnp.exp(-jnp.abs(x)))


def _rms(x, gain):
    return x * lax.rsqrt(jnp.mean(x * x, axis=-1, keepdims=True) + EPS) * gain


class V:
    def __init__(self, arr, lead=None, cb=0, w=None):
        self.arr, self.lead, self.cb = arr, lead, cb
        self.w = arr.shape[-1] if w is None else w

    @property
    def rows(self):
        return self.arr.shape[-2]

    def spec(self, tile, order=None):
        lead, cb, w = self.lead, self.cb, self.w
        order = order or (lambda i: i)
        if lead is None:
            return pl.BlockSpec((tile, w), lambda i: (order(i), cb))
        return pl.BlockSpec((None, tile, w), lambda i: (lead, order(i), cb))

    def const_spec(self):
        lead, cb, w, r = self.lead, self.cb, self.w, self.rows
        if lead is None:
            return pl.BlockSpec((r, w), lambda i: (0, cb))
        return pl.BlockSpec((None, r, w), lambda i: (lead, 0, cb))


def _v(a):
    return a if isinstance(a, V) else V(a)


def _rowwise(name, fn, rows, consts, outs, tile=None):
    rows = [_v(r) for r in rows]
    consts = [_v(c) for c in consts]
    s = rows[0].rows
    tile = min(tile or ROW_TILE, s)
    nr, nc = len(rows), len(consts)

    def body(*refs):
        vals = [r[...].astype(f32) for r in refs[:nr + nc]]
        res = fn(*vals)
        for o, val in zip(refs[nr + nc:], res):
            o[...] = val.astype(o.dtype)

    return pl.pallas_call(
        body, name=name, grid=(s // tile,),
        in_specs=[r.spec(tile) for r in rows] + [c.const_spec() for c in consts],
        out_specs=[pl.BlockSpec((tile, w), lambda i: (i, 0)) for w, _ in outs],
        out_shape=[SDS((s, w), dt) for w, dt in outs],
        compiler_params=_cp(("parallel",)),
    )(*[r.arr for r in rows], *[c.arr for c in consts])


def _rowwise_vjp(name, fn, rows, consts, cts, d_rows, add=None, tile=None):
    rows = [_v(r) for r in rows]
    consts = [_v(c) for c in consts]
    cts = [[_v(c) for c in group] for group in cts]
    flat_cts = [c for group in cts for c in group]
    s = rows[0].rows
    tile = min(tile or ROW_TILE, s)
    nr, nc, nt = len(rows), len(consts), len(flat_cts)
    want = [k for k, dt in enumerate(d_rows) if dt is not None]
    has_add = add is not None
    add_v = [_v(add)] if has_add else []

    def body(*refs):
        vals = [r[...].astype(f32) for r in refs[:nr + nc]]
        ct_refs = refs[nr + nc:nr + nc + nt]
        pos = nr + nc + nt
        add_ref = refs[pos] if has_add else None
        pos += 1 if has_add else 0
        drow_refs = refs[pos:pos + len(want)]
        dconst_refs = refs[pos + len(want):]
        ctv, at = [], 0
        for group in cts:
            acc = ct_refs[at][...].astype(f32)
            for r in ct_refs[at + 1:at + len(group)]:
                acc = acc + r[...].astype(f32)
            at += len(group)
            ctv.append(acc)
        _, vjp = jax.vjp(fn, *vals)
        grads = vjp(tuple(ctv))
        for o, k in zip(drow_refs, want):
            g = grads[k]
            if has_add and k == want[0]:
                g = g + add_ref[...].astype(f32)
            o[...] = g.astype(o.dtype)

        @pl.when(pl.program_id(0) == 0)
        def _():
            for o in dconst_refs:
                o[...] = jnp.zeros_like(o)

        for o, g in zip(dconst_refs, grads[nr:]):
            o[...] += g

    outs = pl.pallas_call(
        body, name=name, grid=(s // tile,),
        in_specs=[r.spec(tile) for r in rows] + [c.const_spec() for c in consts]
        + [c.spec(tile) for c in flat_cts] + [a.spec(tile) for a in add_v],
        out_specs=[pl.BlockSpec((tile, rows[k].w), lambda i: (i, 0)) for k in want]
        + [pl.BlockSpec((c.rows, c.w), lambda i: (0, 0)) for c in consts],
        out_shape=[SDS((s, rows[k].w), d_rows[k]) for k in want] + [SDS((c.rows, c.w), f32) for c in consts],
        compiler_params=_cp(("arbitrary",)),
    )(*[r.arr for r in rows], *[c.arr for c in consts], *[c.arr for c in flat_cts], *[a.arr for a in add_v])
    return list(outs[:len(want)]), list(outs[len(want):])


def _pick(n, cap):
    best = None
    for t in range(LANES, min(n, cap) + 1, LANES):
        if n % t == 0:
            best = t
    return best or n


def _norm_matmul(name, h, gain, w, out_dtype):
    m, k = h.shape
    n = w.shape[1]
    tm = min(NORM_MM_TILE, m)

    def body(h_ref, g_ref, w_ref, u_ref, o_ref):
        normed = _rms(h_ref[...], g_ref[...]).astype(bf16)
        u_ref[...] = normed
        o_ref[...] = _dg(normed, w_ref[...], "nn").astype(o_ref.dtype)

    return pl.pallas_call(
        body, name=name, grid=(m // tm,),
        in_specs=[pl.BlockSpec((tm, k), lambda i: (i, 0)), pl.BlockSpec((1, k), lambda i: (0, 0)),
                  pl.BlockSpec((k, n), lambda i: (0, 0))],
        out_specs=[pl.BlockSpec((tm, k), lambda i: (i, 0)), pl.BlockSpec((tm, n), lambda i: (i, 0))],
        out_shape=[SDS((m, k), bf16), SDS((m, n), out_dtype)],
        compiler_params=_cp(("parallel",)),
    )(h, gain, w)


def _matmul(name, a, b, mode, out_dtype, add=None, tn_cap=1280):
    has_add = add is not None
    if mode in ("nn", "nt"):
        m, k = a.shape
        n = b.shape[1] if mode == "nn" else b.shape[0]
        tm = min(MM_TILE, m)
        tn = _pick(n, tn_cap) if k * n * 2 > (8 << 20) else n

        def body(*refs):
            a_ref, b_ref = refs[0], refs[1]
            o_ref = refs[-1]
            acc = _dg(a_ref[...], b_ref[...], mode)
            if has_add:
                acc = acc + refs[2][...].astype(f32)
            o_ref[...] = acc.astype(o_ref.dtype)

        b_spec = pl.BlockSpec((k, tn), lambda i, j: (0, j)) if mode == "nn" else pl.BlockSpec((tn, k), lambda i, j: (j, 0))
        in_specs = [pl.BlockSpec((tm, k), lambda i, j: (i, 0)), b_spec]
        args = [a, b]
        if has_add:
            in_specs.append(pl.BlockSpec((tm, tn), lambda i, j: (i, j)))
            args.append(add)
        return pl.pallas_call(
            body, name=name, grid=(m // tm, n // tn), in_specs=in_specs,
            out_specs=pl.BlockSpec((tm, tn), lambda i, j: (i, j)),
            out_shape=SDS((m, n), out_dtype), compiler_params=_cp(("parallel", "parallel")),
        )(*args)
    kk, m = a.shape
    n = b.shape[1]
    tk = min(MM_TILE, kk)
    tn = _pick(n, tn_cap) if m * n * 4 > (7 << 20) else n

    def body_tn(a_ref, b_ref, o_ref):
        @pl.when(pl.program_id(1) == 0)
        def _():
            o_ref[...] = jnp.zeros_like(o_ref)

        o_ref[...] += _dg(a_ref[...], b_ref[...], "tn")

    return pl.pallas_call(
        body_tn, name=name, grid=(n // tn, kk // tk),
        in_specs=[pl.BlockSpec((tk, m), lambda j, k: (k, 0)), pl.BlockSpec((tk, tn), lambda j, k: (k, j))],
        out_specs=pl.BlockSpec((m, tn), lambda j, k: (0, j)),
        out_shape=SDS((m, n), f32), compiler_params=_cp(("parallel", "arbitrary")),
    )(a, b)


def _ffn_fwd(name, h, gain, wgu, wd, layer):
    s, d = h.shape
    hs = wgu.shape[3]
    nh = wgu.shape[0] // 2
    tm = min(FFN_FWD_TILE, s)

    def body(h_ref, g_ref, wg_ref, wu_ref, wd_ref, o_ref, gate_ref, up_ref, n_scr, acc_scr):
        t = pl.program_id(1)

        @pl.when(t == 0)
        def _():
            n_scr[...] = _rms(h_ref[...], g_ref[...]).astype(bf16)
            acc_scr[...] = jnp.zeros_like(acc_scr)

        n = n_scr[...]
        gate = _dg(n, wg_ref[...], "nn")
        up = _dg(n, wu_ref[...], "nn")
        gate_ref[...] = gate.astype(gate_ref.dtype)
        up_ref[...] = up.astype(up_ref.dtype)
        acc_scr[...] += _dg(_silu(gate) * up, wd_ref[...], "nn")

        @pl.when(t == nh - 1)
        def _():
            o_ref[...] = h_ref[...] + 0.5 * acc_scr[...]

    saved_spec = pl.BlockSpec((None, tm, hs), lambda i, t: (t, i, 0))
    return pl.pallas_call(
        body, name=name, grid=(s // tm, nh),
        in_specs=[
            pl.BlockSpec((tm, d), lambda i, t: (i, 0)),
            pl.BlockSpec((1, d), lambda i, t: (0, 0)),
            pl.BlockSpec((None, None, d, hs), lambda i, t: (t, layer, 0, 0)),
            pl.BlockSpec((None, None, d, hs), lambda i, t: (t + nh, layer, 0, 0)),
            pl.BlockSpec((None, hs, d), lambda i, t: (layer, t, 0)),
        ],
        out_specs=[pl.BlockSpec((tm, d), lambda i, t: (i, 0)), saved_spec, saved_spec],
        out_shape=[SDS((s, d), f32), SDS((nh, s, hs), bf16), SDS((nh, s, hs), bf16)],
        scratch_shapes=[pltpu.VMEM((tm, d), bf16), pltpu.VMEM((tm, d), f32)],
        compiler_params=_cp(("parallel", "arbitrary")),
    )(h, gain, wgu, wgu, wd)


def _ffn_bwd(name, h, dout, gain, gate_s, up_s, wgu, wd, layer):
    s, d = h.shape
    hs = wgu.shape[3]
    nh = wgu.shape[0] // 2
    tm = min(FFN_BWD_TILE, s)

    def body(h_ref, do_ref, g_ref, gate_ref, up_ref, wg_ref, wu_ref, wd_ref, dn_ref, dwg_ref, dwu_ref, dwd_ref):
        @pl.when(pl.program_id(1) == 0)
        def _():
            dwg_ref[...] = jnp.zeros_like(dwg_ref)
            dwu_ref[...] = jnp.zeros_like(dwu_ref)
            dwd_ref[...] = jnp.zeros_like(dwd_ref)

        n = _rms(h_ref[...], g_ref[...]).astype(bf16)
        wg, wu, wdn = wg_ref[...], wu_ref[...], wd_ref[...]
        gate = gate_ref[...].astype(f32)
        up = up_ref[...].astype(f32)
        sg = _sigmoid(gate)
        act = gate * sg
        dy = (0.5 * do_ref[...]).astype(bf16)
        da = _dg(dy, wdn, "nt")
        dup = (da * act).astype(bf16)
        dgate = (da * up * (sg * (1.0 + gate * (1.0 - sg)))).astype(bf16)
        dwd_ref[...] += _dg(act * up, dy, "tn")
        dwg_ref[...] += _dg(n, dgate, "tn")
        dwu_ref[...] += _dg(n, dup, "tn")
        dn_ref[...] = (_dg(dgate, wg, "nt") + _dg(dup, wu, "nt")).astype(dn_ref.dtype)

    return pl.pallas_call(
        body, name=name, grid=(nh, s // tm),
        in_specs=[
            pl.BlockSpec((tm, d), lambda t, i: (i, 0)),
            pl.BlockSpec((tm, d), lambda t, i: (i, 0)),
            pl.BlockSpec((1, d), lambda t, i: (0, 0)),
            pl.BlockSpec((None, tm, hs), lambda t, i: (t, i, 0)),
            pl.BlockSpec((None, tm, hs), lambda t, i: (t, i, 0)),
            pl.BlockSpec((None, None, d, hs), lambda t, i: (t, layer, 0, 0)),
            pl.BlockSpec((None, None, d, hs), lambda t, i: (t + nh, layer, 0, 0)),
            pl.BlockSpec((None, hs, d), lambda t, i: (layer, t, 0)),
        ],
        out_specs=[
            pl.BlockSpec((None, tm, d), lambda t, i: (t, i, 0)),
            pl.BlockSpec((None, d, hs), lambda t, i: (t, 0, 0)),
            pl.BlockSpec((None, d, hs), lambda t, i: (t, 0, 0)),
            pl.BlockSpec((hs, d), lambda t, i: (t, 0)),
        ],
        out_shape=[SDS((nh, s, d), bf16), SDS((nh, d, hs), f32), SDS((nh, d, hs), f32), SDS((nh * hs, d), f32)],
        compiler_params=_cp(("parallel", "arbitrary")),
    )(h, dout, gain, gate_s, up_s, wgu, wgu, wd)


def _conv_fwd(name, x, w):
    x = _v(x)
    s, c = x.rows, x.w
    cw = w.shape[0]
    tile = min(ROW_TILE, s)
    cb = x.cb

    def body(x_ref, halo_ref, w_ref, o_ref, buf):
        first = pl.program_id(0) == 0
        buf[0:8, :] = jnp.where(first, 0.0, halo_ref[...])
        buf[8:8 + tile, :] = x_ref[...]
        acc = w_ref[0:1, :] * buf[pl.ds(8 - cw + 1, tile), :]
        for j in range(1, cw):
            acc = acc + w_ref[j:j + 1, :] * buf[pl.ds(8 - cw + 1 + j, tile), :]
        o_ref[...] = acc

    return pl.pallas_call(
        body, name=name, grid=(s // tile,),
        in_specs=[
            pl.BlockSpec((tile, c), lambda i: (i, cb)),
            pl.BlockSpec((8, c), lambda i: (jnp.maximum(i * (tile // 8) - 1, 0), cb)),
            pl.BlockSpec((cw, c), lambda i: (0, 0)),
        ],
        out_specs=pl.BlockSpec((tile, c), lambda i: (i, 0)),
        out_shape=SDS((s, c), f32),
        scratch_shapes=[pltpu.VMEM((tile + 8, c), f32)],
        compiler_params=_cp(("parallel",)),
    )(x.arr, x.arr, w)


def _conv_bwd(name, x, dy, w):
    x = _v(x)
    s, c = x.rows, x.w
    cw = w.shape[0]
    tile = min(ROW_TILE, s)
    n_tiles = s // tile
    cb = x.cb

    def body(x_ref, xh_ref, dy_ref, dyh_ref, w_ref, dx_ref, dw_ref, xbuf, dbuf):
        i = pl.program_id(0)
        xbuf[0:8, :] = jnp.where(i == 0, 0.0, xh_ref[...])
        xbuf[8:8 + tile, :] = x_ref[...]
        dyv = dy_ref[...]
        dbuf[0:tile, :] = dyv
        dbuf[tile:tile + 8, :] = jnp.where(i == n_tiles - 1, 0.0, dyh_ref[...])

        @pl.when(i == 0)
        def _():
            dw_ref[...] = jnp.zeros_like(dw_ref)

        acc = w_ref[0:1, :] * dbuf[pl.ds(cw - 1, tile), :]
        for j in range(1, cw):
            acc = acc + w_ref[j:j + 1, :] * dbuf[pl.ds(cw - 1 - j, tile), :]
        dx_ref[...] = acc.astype(dx_ref.dtype)
        for j in range(cw):
            dw_ref[j:j + 1, :] += jnp.sum(xbuf[pl.ds(8 - cw + 1 + j, tile), :] * dyv, axis=0, keepdims=True)

    return pl.pallas_call(
        body, name=name, grid=(n_tiles,),
        in_specs=[
            pl.BlockSpec((tile, c), lambda i: (i, cb)),
            pl.BlockSpec((8, c), lambda i: (jnp.maximum(i * (tile // 8) - 1, 0), cb)),
            pl.BlockSpec((tile, c), lambda i: (i, 0)),
            pl.BlockSpec((8, c), lambda i: (jnp.minimum((i + 1) * (tile // 8), s // 8 - 1), 0)),
            pl.BlockSpec((cw, c), lambda i: (0, 0)),
        ],
        out_specs=[pl.BlockSpec((tile, c), lambda i: (i, 0)), pl.BlockSpec((cw, c), lambda i: (0, 0))],
        out_shape=[SDS((s, c), bf16), SDS((cw, c), f32)],
        scratch_shapes=[pltpu.VMEM((tile + 8, c), f32), pltpu.VMEM((tile + 8, c), f32)],
        compiler_params=_cp(("arbitrary",)),
    )(x.arr, x.arr, dy, dy, w)


def _gdn_pre(n_heads, head_dim, yc, ab, a_log, dt_bias):
    gw = n_heads * head_dim
    act = _silu(yc)
    parts = _split_lanes(act, head_dim)
    qs = [p * lax.rsqrt(jnp.sum(p * p, axis=-1, keepdims=True) + EPS) * (head_dim ** -0.5) for p in parts[:n_heads]]
    ks = [p * lax.rsqrt(jnp.sum(p * p, axis=-1, keepdims=True) + EPS) for p in parts[n_heads:2 * n_heads]]
    lane = lax.broadcasted_iota(jnp.int32, ab.shape, 1)
    g = -jnp.exp(a_log) * _softplus(ab + dt_bias)
    gb = jnp.where(lane < n_heads, g, jnp.where(lane < 2 * n_heads, _sigmoid(ab), 0.0))
    del gw
    return (jnp.concatenate(qs, axis=1), jnp.concatenate(ks, axis=1),
            jnp.concatenate(list(parts[2 * n_heads:]), axis=1), gb)


def _gdn_post(n_heads, head_dim, o, z, out_norm):
    parts = _split_lanes(o, head_dim)
    normed = jnp.concatenate([_rms(p, out_norm) for p in parts], axis=1)
    return (normed * _silu(z),)


def _gdn_chunk(n_heads, head_dim, q, k, v, gb, *states):
    c = q.shape[0]
    ri = lax.broadcasted_iota(jnp.int32, (c, c), 0)
    ci = lax.broadcasted_iota(jnp.int32, (c, c), 1)
    incl, strict, diag = ri >= ci, ri > ci, ri == ci
    lane = lax.broadcasted_iota(jnp.int32, gb.shape, 1)
    qs, ks, vs = _split_lanes(q, head_dim), _split_lanes(k, head_dim), _split_lanes(v, head_dim)
    heads = range(n_heads)
    g = [jnp.sum(jnp.where(lane == h, gb, 0.0), axis=1, keepdims=True) for h in heads]
    beta = [jnp.sum(jnp.where(lane == n_heads + h, gb, 0.0), axis=1, keepdims=True) for h in heads]
    g_row = [jnp.sum(jnp.where(diag, g[h], 0.0), axis=0, keepdims=True) for h in heads]
    cg_col = [jnp.sum(jnp.where(incl, g_row[h], 0.0), axis=1, keepdims=True) for h in heads]
    cg_row = [jnp.sum(jnp.where(ri <= ci, g[h], 0.0), axis=0, keepdims=True) for h in heads]
    g_last = [jnp.sum(g[h], axis=0, keepdims=True) for h in heads]
    decay = [jnp.where(incl, jnp.exp(jnp.where(incl, cg_col[h] - cg_row[h], 0.0)), 0.0) for h in heads]
    kb = [ks[h] * beta[h] for h in heads]
    lower = [jnp.where(strict, _mm(kb[h], ks[h], "nt") * decay[h], 0.0) for h in heads]
    eye = jnp.where(diag, 1.0, 0.0)
    off_diag = [t - eye for t in _unit_lower_inverses(tuple(lower))]
    e_col = [jnp.exp(cg_col[h]) for h in heads]
    vb = [vs[h] * beta[h] for h in heads]
    kbg = [kb[h] * e_col[h] for h in heads]
    u = [vb[h] + _mm(off_diag[h], vb[h], "nn") for h in heads]
    w = [kbg[h] + _mm(off_diag[h], kbg[h], "nn") for h in heads]
    qk = [jnp.where(incl, _mm(qs[h], ks[h], "nt") * decay[h], 0.0) for h in heads]
    v_new = [u[h] - _mm(w[h], states[h], "nn") for h in heads]
    inter = [_mm(qs[h] * e_col[h], states[h], "nn") for h in heads]
    outs = [inter[h] + _mm(qk[h], v_new[h], "nn") for h in heads]
    k_tail = [ks[h] * jnp.exp(g_last[h] - cg_col[h]) for h in heads]
    new_states = [states[h] * jnp.exp(g_last[h]) + _mm(k_tail[h], v_new[h], "tn") for h in heads]
    return (jnp.concatenate(outs, axis=1), *new_states)


def _gdn_chunk_fwd(name, q, k, v, gb, n_heads, head_dim):
    s, gw = q.shape
    n = s // CHUNK
    fn = functools.partial(_gdn_chunk, n_heads, head_dim)

    def body(q_ref, k_ref, v_ref, gb_ref, o_ref, st_ref, st_scr):
        @pl.when(pl.program_id(0) == 0)
        def _():
            st_scr[...] = jnp.zeros_like(st_scr)

        st_ref[...] = st_scr[...]
        res = fn(q_ref[...], k_ref[...], v_ref[...], gb_ref[...], *[st_scr[h] for h in range(n_heads)])
        o_ref[...] = res[0]
        for h in range(n_heads):
            st_scr[h] = res[1 + h]

    row = lambda w: pl.BlockSpec((CHUNK, w), lambda i: (i, 0))
    return pl.pallas_call(
        body, name=name, grid=(n,),
        in_specs=[row(gw), row(gw), row(gw), row(LANES)],
        out_specs=[row(gw), pl.BlockSpec((None, n_heads, head_dim, head_dim), lambda i: (i, 0, 0, 0))],
        out_shape=[SDS((s, gw), f32), SDS((n, n_heads, head_dim, head_dim), f32)],
        scratch_shapes=[pltpu.VMEM((n_heads, head_dim, head_dim), f32)],
        compiler_params=_cp(("arbitrary",)),
    )(q, k, v, gb)


def _gdn_chunk_bwd(name, q, k, v, gb, states, d_out, n_heads, head_dim):
    s, gw = q.shape
    n = s // CHUNK
    fn = functools.partial(_gdn_chunk, n_heads, head_dim)

    def body(q_ref, k_ref, v_ref, gb_ref, st_ref, do_ref, dq_ref, dk_ref, dv_ref, dgb_ref, dst_scr):
        @pl.when(pl.program_id(0) == 0)
        def _():
            dst_scr[...] = jnp.zeros_like(dst_scr)

        _, vjp = jax.vjp(fn, q_ref[...], k_ref[...], v_ref[...], gb_ref[...], *[st_ref[h] for h in range(n_heads)])
        grads = vjp((do_ref[...].astype(f32), *[dst_scr[h] for h in range(n_heads)]))
        dq_ref[...] = grads[0]
        dk_ref[...] = grads[1]
        dv_ref[...] = grads[2]
        dgb_ref[...] = grads[3]
        for h in range(n_heads):
            dst_scr[h] = grads[4 + h]

    row = lambda w: pl.BlockSpec((CHUNK, w), lambda i: (n - 1 - i, 0))
    return pl.pallas_call(
        body, name=name, grid=(n,),
        in_specs=[row(gw), row(gw), row(gw), row(LANES),
                  pl.BlockSpec((None, n_heads, head_dim, head_dim), lambda i: (n - 1 - i, 0, 0, 0)), row(gw)],
        out_specs=[row(gw), row(gw), row(gw), row(LANES)],
        out_shape=[SDS((s, gw), f32), SDS((s, gw), f32), SDS((s, gw), f32), SDS((s, LANES), f32)],
        scratch_shapes=[pltpu.VMEM((n_heads, head_dim, head_dim), f32)],
        compiler_params=_cp(("arbitrary",)),
    )(q, k, v, gb, states, d_out)


def _mem_attn(qm, km, vm):
    width = qm.shape[1]
    hd = width // MEM_HEADS
    lane = lax.broadcasted_iota(jnp.int32, (1, width), 1)
    out = jnp.zeros_like(qm)
    for h in range(MEM_HEADS):
        msk = jnp.where((lane >= h * hd) & (lane < (h + 1) * hd), 1.0, 0.0)
        logits = _mm(qm * msk, km, "nt") * (hd ** -0.5)
        p = jnp.exp(logits - jnp.max(logits, axis=-1, keepdims=True))
        p = p / jnp.sum(p, axis=-1, keepdims=True)
        out = out + _mm(p, vm, "nn") * msk
    return (out,)


def _kv_post(n_heads, pk, pv, pf, b_f):
    lane = lax.broadcasted_iota(jnp.int32, pf.shape, 1)
    log_f = jnp.where(lane < n_heads, -_softplus(-(pf + b_f)), 0.0)
    return pk, pv, log_f


def _cumsum(name, xs, reverse, scale=None):
    s, w = xs[0].shape
    tile = min(CUMSUM_TILE, s)
    n = s // tile

    def body(*refs):
        x_refs, o_ref, carry = refs[:-2], refs[-2], refs[-1]

        @pl.when(pl.program_id(0) == 0)
        def _():
            carry[...] = jnp.zeros_like(carry)

        xv = x_refs[0][...]
        for r in x_refs[1:]:
            xv = xv + r[...]
        if scale is not None:
            xv = xv * scale
        ri = lax.broadcasted_iota(jnp.int32, (tile, tile), 0)
        ci = lax.broadcasted_iota(jnp.int32, (tile, tile), 1)
        tri = jnp.where((ri <= ci) if reverse else (ri >= ci), 1.0, 0.0).astype(bf16)
        x1 = xv.astype(bf16)
        r1 = xv - x1.astype(f32)
        x2 = r1.astype(bf16)
        x3 = (r1 - x2.astype(f32)).astype(bf16)
        acc = carry[...] + _dg(tri, x1, "nn") + _dg(tri, x2, "nn") + _dg(tri, x3, "nn")
        o_ref[...] = acc
        carry[...] += jnp.sum(xv, axis=0, keepdims=True)

    order = (lambda i: (n - 1 - i, 0)) if reverse else (lambda i: (i, 0))
    return pl.pallas_call(
        body, name=name, grid=(n,),
        in_specs=[pl.BlockSpec((tile, w), order)] * len(xs), out_specs=pl.BlockSpec((tile, w), order),
        out_shape=SDS((s, w), f32), scratch_shapes=[pltpu.VMEM((1, w), f32)],
        compiler_params=_cp(("arbitrary",)),
    )(*xs)


LOG2_E = math.log2(math.e)


def _fox_logits(q_ref, k_ref, cr_ref, hh, head_dim, scale, diagonal):
    sl = slice(hh * head_dim, (hh + 1) * head_dim)
    s = _dg(q_ref[:, sl], k_ref[:, sl], "nt") * (scale * LOG2_E) - cr_ref[hh]
    if diagonal:
        tq, tk = s.shape
        ok = lax.broadcasted_iota(jnp.int32, (tq, tk), 1) <= lax.broadcasted_iota(jnp.int32, (tq, tk), 0)
        s = jnp.where(ok, s, NEG_INF)
    return s, sl


def _on_causal_tiles(i, j, fn):
    @pl.when(j < i)
    def _():
        fn(False)

    @pl.when(j == i)
    def _():
        fn(True)


def _fox_fwd(name, q, k, v, crow, n_heads, head_dim):
    s = k.shape[0]
    tq = tk = min(ATT_TILE, s)
    nq, nk = s // tq, s // tk
    scale = head_dim ** -0.5
    hpb = ATT_HEADS
    wb = hpb * head_dim

    def body(q_ref, k_ref, v_ref, cr_ref, o_ref, lse_ref, m_scr, l_scr, acc_scr):
        i, j = pl.program_id(1), pl.program_id(2)

        @pl.when(j == 0)
        def _():
            m_scr[...] = jnp.full_like(m_scr, NEG_INF)
            l_scr[...] = jnp.zeros_like(l_scr)
            acc_scr[...] = jnp.zeros_like(acc_scr)

        def step(diagonal):
            heads = range(hpb)
            sl = [slice(hh * head_dim, (hh + 1) * head_dim) for hh in heads]
            sc = [_fox_logits(q_ref, k_ref, cr_ref, hh, head_dim, scale, diagonal)[0] for hh in heads]
            m_old = [m_scr[hh] for hh in heads]
            m_new = [jnp.maximum(m_old[hh], jnp.max(sc[hh], axis=-1, keepdims=True)) for hh in heads]
            p = [jnp.exp2(sc[hh] - m_new[hh]) for hh in heads]
            alpha = [jnp.exp2(m_old[hh] - m_new[hh]) for hh in heads]
            pv = [_dg(p[hh], v_ref[:, sl[hh]], "nn") for hh in heads]
            for hh in heads:
                l_scr[hh] = alpha[hh] * l_scr[hh] + jnp.sum(p[hh], axis=-1, keepdims=True)
                acc_scr[:, sl[hh]] = alpha[hh] * acc_scr[:, sl[hh]] + pv[hh]
                m_scr[hh] = m_new[hh]

        _on_causal_tiles(i, j, step)

        @pl.when(j == nk - 1)
        def _():
            for hh in range(hpb):
                sl = slice(hh * head_dim, (hh + 1) * head_dim)
                o_ref[:, sl] = (acc_scr[:, sl] / l_scr[hh]).astype(o_ref.dtype)
                lse_ref[hh] = m_scr[hh] + jnp.log2(l_scr[hh])

    return pl.pallas_call(
        body, name=name, grid=(n_heads // hpb, nq, nk),
        in_specs=[
            pl.BlockSpec((tq, wb), lambda g, i, j: (i, g)),
            pl.BlockSpec((tk, wb), lambda g, i, j: (jnp.minimum(j, i), g)),
            pl.BlockSpec((tk, wb), lambda g, i, j: (jnp.minimum(j, i), g)),
            pl.BlockSpec((hpb, 1, tk), lambda g, i, j: (g, 0, jnp.minimum(j, i))),
        ],
        out_specs=[pl.BlockSpec((tq, wb), lambda g, i, j: (i, g)),
                   pl.BlockSpec((hpb, tq, 1), lambda g, i, j: (g, i, 0))],
        out_shape=[SDS((s, n_heads * head_dim), bf16), SDS((n_heads, s, 1), f32)],
        scratch_shapes=[pltpu.VMEM((hpb, tq, 1), f32), pltpu.VMEM((hpb, tq, 1), f32), pltpu.VMEM((tq, wb), f32)],
        compiler_params=_cp(("parallel", "parallel", "arbitrary")),
    )(q, k, v, crow)


def _fox_probs(q_ref, k_ref, v_ref, cr_ref, lse_ref, do_ref, hh, head_dim, scale, diagonal):
    sc, sl = _fox_logits(q_ref, k_ref, cr_ref, hh, head_dim, scale, diagonal)
    return jnp.exp2(sc - lse_ref[hh]), _dg(do_ref[:, sl], v_ref[:, sl], "nt"), sl


def _fox_bwd_dq(name, q, k, v, crow, lse, o, do, n_heads, head_dim):
    s = k.shape[0]
    tq = tk = min(ATT_TILE, s)
    nq, nk = s // tq, s // tk
    scale = head_dim ** -0.5
    hpb = ATT_HEADS
    wb = hpb * head_dim

    def body(q_ref, k_ref, v_ref, cr_ref, lse_ref, o_ref, do_ref, dq_ref, delta_ref, dcc_ref, acc_scr):
        i, j = pl.program_id(1), pl.program_id(2)
        heads = range(hpb)

        @pl.when(j == 0)
        def _():
            prod = o_ref[...].astype(f32) * do_ref[...].astype(f32)
            for hh in heads:
                delta_ref[hh] = jnp.sum(prod[:, hh * head_dim:(hh + 1) * head_dim], axis=-1, keepdims=True)
            dcc_ref[...] = jnp.zeros_like(dcc_ref)
            acc_scr[...] = jnp.zeros_like(acc_scr)

        def step(diagonal):
            pd = [_fox_probs(q_ref, k_ref, v_ref, cr_ref, lse_ref, do_ref, hh, head_dim, scale, diagonal) for hh in heads]
            ds = [pd[hh][0] * (pd[hh][1] - delta_ref[hh]) for hh in heads]
            dqs = [_dg(ds[hh], k_ref[:, pd[hh][2]], "nn") for hh in heads]
            for hh in heads:
                dcc_ref[hh] += jnp.sum(ds[hh], axis=-1, keepdims=True)
                acc_scr[:, pd[hh][2]] += dqs[hh]

        _on_causal_tiles(i, j, step)

        @pl.when(j == nk - 1)
        def _():
            dq_ref[...] = (acc_scr[...] * scale).astype(dq_ref.dtype)

    qspec = pl.BlockSpec((tq, wb), lambda g, i, j: (i, g))
    kspec = pl.BlockSpec((tk, wb), lambda g, i, j: (jnp.minimum(j, i), g))
    cspec = pl.BlockSpec((hpb, tq, 1), lambda g, i, j: (g, i, 0))
    return pl.pallas_call(
        body, name=name, grid=(n_heads // hpb, nq, nk),
        in_specs=[qspec, kspec, kspec,
                  pl.BlockSpec((hpb, 1, tk), lambda g, i, j: (g, 0, jnp.minimum(j, i))), cspec, qspec, qspec],
        out_specs=[qspec, cspec, cspec],
        out_shape=[SDS((s, n_heads * head_dim), bf16), SDS((n_heads, s, 1), f32), SDS((n_heads, s, 1), f32)],
        scratch_shapes=[pltpu.VMEM((tq, wb), f32)],
        compiler_params=_cp(("parallel", "parallel", "arbitrary")),
    )(q, k, v, crow, lse, o, do)


def _fox_bwd_dkv(name, q, k, v, crow, lse, delta, do, n_heads, head_dim):
    s = k.shape[0]
    tq = tk = min(ATT_TILE, s)
    nq, nk = s // tq, s // tk
    scale = head_dim ** -0.5
    hpb = ATT_HEADS
    wb = hpb * head_dim

    def body(q_ref, k_ref, v_ref, cr_ref, lse_ref, delta_ref, do_ref, dk_ref, dv_ref, dc_ref):
        j, i = pl.program_id(1), pl.program_id(2)

        @pl.when(i == 0)
        def _():
            dk_ref[...] = jnp.zeros_like(dk_ref)
            dv_ref[...] = jnp.zeros_like(dv_ref)
            dc_ref[...] = jnp.zeros_like(dc_ref)

        def step(diagonal):
            heads = range(hpb)
            pd = [_fox_probs(q_ref, k_ref, v_ref, cr_ref, lse_ref, do_ref, hh, head_dim, scale, diagonal) for hh in heads]
            ds = [pd[hh][0] * (pd[hh][1] - delta_ref[hh]) for hh in heads]
            dvs = [_dg(pd[hh][0], do_ref[:, pd[hh][2]], "tn") for hh in heads]
            dks = [_dg(ds[hh], q_ref[:, pd[hh][2]], "tn") for hh in heads]
            for hh in heads:
                sl = pd[hh][2]
                dv_ref[:, sl] += dvs[hh]
                dk_ref[:, sl] += dks[hh] * scale
                dc_ref[hh] -= jnp.sum(ds[hh], axis=0, keepdims=True)

        _on_causal_tiles(i, j, step)

    qspec = pl.BlockSpec((tq, wb), lambda g, j, i: (jnp.maximum(i, j), g))
    kspec = pl.BlockSpec((tk, wb), lambda g, j, i: (j, g))
    cspec = pl.BlockSpec((hpb, tq, 1), lambda g, j, i: (g, jnp.maximum(i, j), 0))
    rspec = pl.BlockSpec((hpb, 1, tk), lambda g, j, i: (g, 0, j))
    return pl.pallas_call(
        body, name=name, grid=(n_heads // hpb, nk, nq),
        in_specs=[qspec, kspec, kspec, rspec, cspec, cspec, qspec],
        out_specs=[kspec, kspec, rspec],
        out_shape=[SDS((s, n_heads * head_dim), f32), SDS((s, n_heads * head_dim), f32), SDS((n_heads, 1, s), f32)],
        compiler_params=_cp(("parallel", "parallel", "arbitrary")),
    )(q, k, v, crow, lse, delta, do)


def _final_loss(name, h, target, gain):
    s, d = h.shape
    tile = min(ROW_TILE, s)

    def body(h_ref, t_ref, g_ref, loss_ref, dh_ref, dg_ref):
        @pl.when(pl.program_id(0) == 0)
        def _():
            loss_ref[...] = jnp.zeros_like(loss_ref)
            dg_ref[...] = jnp.zeros_like(dg_ref)

        x, g = h_ref[...], g_ref[...]
        rstd = lax.rsqrt(jnp.mean(x * x, axis=-1, keepdims=True) + EPS)
        xhat = x * rstd
        err = xhat * g - t_ref[...]
        row = jnp.sum(err * err, axis=-1, keepdims=True) * (0.5 / d)
        loss_ref[...] += jnp.sum(row, axis=0, keepdims=True)
        dy = err * (1.0 / d)
        dg_ref[...] += jnp.sum(dy * xhat, axis=0, keepdims=True)
        dxhat = dy * g
        dh_ref[...] = rstd * (dxhat - xhat * jnp.mean(dxhat * xhat, axis=-1, keepdims=True))

    return pl.pallas_call(
        body, name=name, grid=(s // tile,),
        in_specs=[pl.BlockSpec((tile, d), lambda i: (i, 0)), pl.BlockSpec((tile, d), lambda i: (i, 0)),
                  pl.BlockSpec((1, d), lambda i: (0, 0))],
        out_specs=[pl.BlockSpec((1, LANES), lambda i: (0, 0)), pl.BlockSpec((tile, d), lambda i: (i, 0)),
                   pl.BlockSpec((1, d), lambda i: (0, 0))],
        out_shape=[SDS((1, LANES), f32), SDS((s, d), f32), SDS((1, d), f32)],
        compiler_params=_cp(("arbitrary",)),
    )(h, target, gain)


def _adamw(name, parts, w, m, v):
    r, c = w.shape
    rb = r
    for cand in (256, 128, 64, 32, 16):
        if r % cand == 0 and cand * c * 4 <= (2 << 20):
            rb = cand
            break
    n_parts = parts.shape[0]

    def body(p_ref, w_ref, m_ref, v_ref, g_out, d_out, m_out, v_out):
        g = p_ref[0].astype(f32)
        for k in range(1, n_parts):
            g = g + p_ref[k].astype(f32)
        m_new = ADAM_B1 * m_ref[...] + (1.0 - ADAM_B1) * g
        v_new = ADAM_B2 * v_ref[...] + (1.0 - ADAM_B2) * (g * g)
        m_hat = m_new / (1.0 - ADAM_B1 ** ADAM_STEP)
        v_hat = v_new / (1.0 - ADAM_B2 ** ADAM_STEP)
        g_out[...] = g
        d_out[...] = -ADAM_LR * (m_hat / (jnp.sqrt(v_hat) + ADAM_EPS) + ADAM_WD * w_ref[...])
        m_out[...] = m_new
        v_out[...] = v_new

    blk = pl.BlockSpec((rb, c), lambda i: (i, 0))
    return pl.pallas_call(
        body, name=name, grid=(r // rb,),
        in_specs=[pl.BlockSpec((n_parts, rb, c), lambda i: (0, i, 0)), blk, blk, blk],
        out_specs=[blk, blk, blk, blk],
        out_shape=[SDS((r, c), f32)] * 4,
        compiler_params=_cp(("parallel",)),
    )(parts, w, m, v)


def _position():
    x, y, c = lax.axis_index("x"), lax.axis_index("y"), lax.axis_index("c")
    return x, y, c


def _all_gather(name, shards):
    n = len(shards)

    def body(*refs):
        ins, outs = refs[:n], refs[n:2 * n]
        send_sems, recv_sems, local_sems = refs[2 * n:]
        x, y, c = _position()
        me, sibling = (x, y, c), (x, y, 1 - c)
        chips = [(1 - x, y), (x, 1 - y), (1 - x, 1 - y)]

        def slot(a, block):
            px, py, pc = block
            return outs[a].at[4 * px + 2 * py + pc]

        def copy(a, k, block, to, src=None):
            return pltpu.make_async_remote_copy(
                src_ref=slot(a, block) if src is None else src, dst_ref=slot(a, block),
                send_sem=send_sems.at[a, k], recv_sem=recv_sems.at[a, k], device_id=to, device_id_type=MESH)

        local = [pltpu.make_async_copy(ins[a], slot(a, me), local_sems.at[a]) for a in range(n)]
        for cp in local:
            cp.start()
        started = []
        for a in range(n):
            first = [copy(a, 0, me, sibling, src=ins[a])]
            first += [copy(a, 1 + j, me, (*chip, c), src=ins[a]) for j, chip in enumerate(chips)]
            for cp in first:
                cp.start()
            started += first
        for a in range(n):
            for j, chip in enumerate(chips):
                copy(a, 1 + j, (*chip, c), me).wait_recv()
                passed = copy(a, 4 + j, (*chip, c), sibling)
                passed.start()
                started.append(passed)
        for a in range(n):
            copy(a, 0, sibling, me).wait_recv()
            for j, chip in enumerate(chips):
                copy(a, 4 + j, (*chip, 1 - c), me).wait_recv()
        for cp in started:
            cp.wait_send()
        for cp in local:
            cp.wait()

    any_spec = pl.BlockSpec(memory_space=pl.ANY)
    outs = pl.pallas_call(
        body, name=name,
        in_specs=[any_spec] * n, out_specs=[any_spec] * n,
        out_shape=[SDS((N_DEV, *a.shape), a.dtype) for a in shards],
        scratch_shapes=[pltpu.SemaphoreType.DMA((n, 7)), pltpu.SemaphoreType.DMA((n, 7)), pltpu.SemaphoreType.DMA((n,))],
    )(*shards)
    return list(outs)


N_CHIPS = N_DEV // 2


def _sibling_exchange(name, arrs):
    n = len(arrs)

    def body(*refs):
        ins, outs = refs[:n], refs[n:2 * n]
        send_sems, recv_sems = refs[2 * n:]
        x, y, c = _position()
        copies = [pltpu.make_async_remote_copy(
            src_ref=ins[a], dst_ref=outs[a], send_sem=send_sems.at[a], recv_sem=recv_sems.at[a],
            device_id=(x, y, 1 - c), device_id_type=MESH) for a in range(n)]
        for cp in copies:
            cp.start()
        for cp in copies:
            cp.wait_recv()
        for cp in copies:
            cp.wait_send()

    any_spec = pl.BlockSpec(memory_space=pl.ANY)
    outs = pl.pallas_call(
        body, name=name,
        in_specs=[any_spec] * n, out_specs=[any_spec] * n, out_shape=[SDS(a.shape, a.dtype) for a in arrs],
        scratch_shapes=[pltpu.SemaphoreType.DMA((n,)), pltpu.SemaphoreType.DMA((n,))],
    )(*arrs)
    return list(outs)


def _chip_exchange(name, parts):
    n = len(parts)

    def body(*refs):
        ins, outs = refs[:n], refs[n:2 * n]
        send_sems, recv_sems, local_sems = refs[2 * n:]
        x, y, c = _position()
        my_chip = 2 * x + y
        local = [pltpu.make_async_copy(ins[a].at[my_chip], outs[a].at[my_chip], local_sems.at[a]) for a in range(n)]
        for cp in local:
            cp.start()
        sends, recvs = [], []
        for a in range(n):
            for k in range(1, N_CHIPS):
                px = (1 - x) if (k >> 1) & 1 else x
                py = (1 - y) if k & 1 else y
                peer_chip = 2 * px + py
                send = pltpu.make_async_remote_copy(
                    src_ref=ins[a].at[peer_chip], dst_ref=outs[a].at[my_chip],
                    send_sem=send_sems.at[a, k - 1], recv_sem=recv_sems.at[a, k - 1],
                    device_id=(px, py, c), device_id_type=MESH)
                send.start()
                sends.append(send)
                recvs.append(pltpu.make_async_remote_copy(
                    src_ref=ins[a].at[peer_chip], dst_ref=outs[a].at[peer_chip],
                    send_sem=send_sems.at[a, k - 1], recv_sem=recv_sems.at[a, k - 1],
                    device_id=(px, py, c), device_id_type=MESH))
        for cp in recvs:
            cp.wait_recv()
        for cp in sends:
            cp.wait_send()
        for cp in local:
            cp.wait()

    any_spec = pl.BlockSpec(memory_space=pl.ANY)
    outs = pl.pallas_call(
        body, name=name,
        in_specs=[any_spec] * n, out_specs=[any_spec] * n,
        out_shape=[SDS(a.shape, a.dtype) for a in parts],
        scratch_shapes=[pltpu.SemaphoreType.DMA((n, N_CHIPS - 1)), pltpu.SemaphoreType.DMA((n, N_CHIPS - 1)),
                        pltpu.SemaphoreType.DMA((n,))],
    )(*parts)
    return list(outs)


def _reduce_exchange(stacks):
    core = lax.axis_index("c")
    by_core = [st.reshape(N_CHIPS, 2, *st.shape[1:]) for st in stacks]
    mine = [lax.dynamic_index_in_dim(v, core, axis=1, keepdims=False) for v in by_core]
    theirs = _sibling_exchange(
        "exchange_sibling", [lax.dynamic_index_in_dim(v, 1 - core, axis=1, keepdims=False) for v in by_core])
    summed = []
    for a, (m, t) in enumerate(zip(mine, theirs)):
        cols = m.shape[-1]
        (both,) = _rowwise(f"exchange_add_{a}", lambda p, q: (p + q,), [m.reshape(-1, cols), t.reshape(-1, cols)], [],
                           [(cols, m.dtype)])
        summed.append(both.reshape(m.shape))
    return _chip_exchange("exchange_chips", summed)


def _rows_from_shards(g):
    n, l, r, c = g.shape
    return g.transpose(1, 0, 2, 3).reshape(l, n * r, c)


def _rows_to_shards(w):
    l, rows, c = w.shape
    return w.reshape(l, N_DEV, rows // N_DEV, c).transpose(1, 0, 2, 3)


def _pad_lanes(a, width):
    return jnp.pad(a, [(0, 0)] * (a.ndim - 1) + [(0, width - a.shape[-1])])


def _row(vec, width=None):
    vec = vec.reshape(1, -1)
    return vec if width is None else _pad_lanes(vec, width)


class _SmallPack:
    def __init__(self, shapes):
        self.shapes, self.offsets, at = shapes, {}, 0
        for name, shape in shapes.items():
            last = shape[-1]
            lead = int(math.prod(shape[:-1]))
            rows = lead * (last // LANES) if last >= LANES else lead
            self.offsets[name] = (at, rows)
            at += rows
        self.rows = -(-at // 8) * 8

    def pack(self, values):
        pieces = []
        for name, shape in self.shapes.items():
            val = values[name].astype(f32)
            if shape[-1] >= LANES:
                pieces.append(val.reshape(-1, LANES))
            else:
                pieces.append(_pad_lanes(val.reshape(-1, shape[-1]), LANES))
        used = sum(p.shape[0] for p in pieces)
        if used < self.rows:
            pieces.append(jnp.zeros((self.rows - used, LANES), f32))
        return jnp.concatenate(pieces, axis=0)

    def unpack(self, packed):
        out = {}
        for name, shape in self.shapes.items():
            at, rows = self.offsets[name]
            blk = packed[at:at + rows]
            out[name] = blk.reshape(shape) if shape[-1] >= LANES else blk[:, :shape[-1]].reshape(shape)
        return out


def kernel(x, mem, ffn1_norm, ffn1_w_gate_up, ffn1_w_down, mix_norm, ffn2_norm, ffn2_w_gate_up, ffn2_w_down, gdn_w_in, gdn_conv, gdn_A_log, gdn_dt_bias, gdn_out_norm, fox_w_in, w_out, mem_norm, mem_w_kv, kv_norm, kv_w, kv_b_f, final_norm, loss_target, m_ffn1_norm, m_ffn1_w_gate_up, m_ffn1_w_down, m_mix_norm, m_ffn2_norm, m_ffn2_w_gate_up, m_ffn2_w_down, m_gdn_w_in, m_gdn_conv, m_gdn_A_log, m_gdn_dt_bias, m_gdn_out_norm, m_fox_w_in, m_w_out, m_mem_norm, m_mem_w_kv, m_kv_norm, m_kv_w, m_kv_b_f, m_final_norm, v_ffn1_norm, v_ffn1_w_gate_up, v_ffn1_w_down, v_mix_norm, v_ffn2_norm, v_ffn2_w_gate_up, v_ffn2_w_down, v_gdn_w_in, v_gdn_conv, v_gdn_A_log, v_gdn_dt_bias, v_gdn_out_norm, v_fox_w_in, v_w_out, v_mem_norm, v_mem_w_kv, v_kv_norm, v_kv_w, v_kv_b_f, v_final_norm):
    weights = dict(ffn1_norm=ffn1_norm, ffn1_w_gate_up=ffn1_w_gate_up, ffn1_w_down=ffn1_w_down, mix_norm=mix_norm,
                   ffn2_norm=ffn2_norm, ffn2_w_gate_up=ffn2_w_gate_up, ffn2_w_down=ffn2_w_down, gdn_w_in=gdn_w_in,
                   gdn_conv=gdn_conv, gdn_A_log=gdn_A_log, gdn_dt_bias=gdn_dt_bias, gdn_out_norm=gdn_out_norm,
                   fox_w_in=fox_w_in, w_out=w_out, mem_norm=mem_norm, mem_w_kv=mem_w_kv, kv_norm=kv_norm, kv_w=kv_w,
                   kv_b_f=kv_b_f, final_norm=final_norm)
    mom_m = dict(ffn1_norm=m_ffn1_norm, ffn1_w_gate_up=m_ffn1_w_gate_up, ffn1_w_down=m_ffn1_w_down, mix_norm=m_mix_norm,
                 ffn2_norm=m_ffn2_norm, ffn2_w_gate_up=m_ffn2_w_gate_up, ffn2_w_down=m_ffn2_w_down, gdn_w_in=m_gdn_w_in,
                 gdn_conv=m_gdn_conv, gdn_A_log=m_gdn_A_log, gdn_dt_bias=m_gdn_dt_bias, gdn_out_norm=m_gdn_out_norm,
                 fox_w_in=m_fox_w_in, w_out=m_w_out, mem_norm=m_mem_norm, mem_w_kv=m_mem_w_kv, kv_norm=m_kv_norm,
                 kv_w=m_kv_w, kv_b_f=m_kv_b_f, final_norm=m_final_norm)
    mom_v = dict(ffn1_norm=v_ffn1_norm, ffn1_w_gate_up=v_ffn1_w_gate_up, ffn1_w_down=v_ffn1_w_down, mix_norm=v_mix_norm,
                 ffn2_norm=v_ffn2_norm, ffn2_w_gate_up=v_ffn2_w_gate_up, ffn2_w_down=v_ffn2_w_down, gdn_w_in=v_gdn_w_in,
                 gdn_conv=v_gdn_conv, gdn_A_log=v_gdn_A_log, gdn_dt_bias=v_gdn_dt_bias, gdn_out_norm=v_gdn_out_norm,
                 fox_w_in=v_fox_w_in, w_out=v_w_out, mem_norm=v_mem_norm, mem_w_kv=v_mem_w_kv, kv_norm=v_kv_norm,
                 kv_w=v_kv_w, kv_b_f=v_kv_b_f, final_norm=v_final_norm)
    names = list(weights)
    small_names = [n for n in names if weights[n].shape == mom_m[n].shape and n in (
        "ffn1_norm", "mix_norm", "ffn2_norm", "gdn_A_log", "gdn_dt_bias", "gdn_out_norm", "mem_norm", "kv_norm",
        "kv_b_f", "final_norm")]
    big_names = [n for n in names if n not in small_names]

    h = x[0]
    target = loss_target[0]
    mem_tokens = mem[0]
    s, d = h.shape
    depth = ffn1_norm.shape[0]
    n_a = gdn_w_in.shape[0]
    n_heads, head_dim = gdn_A_log.shape[1], gdn_out_norm.shape[1]
    gw = n_heads * head_dim
    a_in = gdn_w_in.shape[2]
    mem_w = a_in - 4 * gw - 2 * n_heads
    a_in_pad = 4 * gw + mem_w + LANES
    kv_width = kv_w.shape[1]
    kv_pad = 2 * gw + LANES
    fh = ffn1_w_down.shape[1] * N_DEV

    def permute_in(w):
        ab = w[..., 4 * gw:4 * gw + 2 * n_heads]
        return jnp.concatenate([w[..., :4 * gw], w[..., 4 * gw + 2 * n_heads:], _pad_lanes(ab, LANES)], axis=-1)

    def unpermute_in(w):
        return jnp.concatenate([w[..., :4 * gw], w[..., 4 * gw + mem_w:4 * gw + mem_w + 2 * n_heads],
                                w[..., 4 * gw:4 * gw + mem_w]], axis=-1)

    gathered = _all_gather("gather_weights", [
        ffn1_w_gate_up.astype(bf16), ffn1_w_down.astype(bf16), ffn2_w_gate_up.astype(bf16), ffn2_w_down.astype(bf16),
        permute_in(gdn_w_in).astype(bf16), fox_w_in.astype(bf16), w_out.astype(bf16), mem_w_kv.astype(bf16),
        _pad_lanes(kv_w, kv_pad).astype(bf16)[None], gdn_conv])
    wgu1, wd1_s, wgu2, wd2_s, win_s, wfox_s, wout_s, wmem_s, wkv_s, conv_s = gathered
    wd1, wd2 = _rows_from_shards(wd1_s), _rows_from_shards(wd2_s)
    win, wfox, wout = _rows_from_shards(win_s), _rows_from_shards(wfox_s), _rows_from_shards(wout_s)
    wmem = _rows_from_shards(wmem_s)
    wmem_cat = wmem.transpose(1, 0, 2).reshape(d, depth * 2 * mem_w)
    wkv = _rows_from_shards(wkv_s)[0]
    conv_w = conv_s.transpose(1, 2, 0, 3).reshape(n_a, gdn_conv.shape[1], 3 * gw)

    a_log_rows = [_row(gdn_A_log[l], LANES) for l in range(n_a)]
    dt_rows = [_row(gdn_dt_bias[l], LANES) for l in range(n_a)]
    onorm_rows = [_row(gdn_out_norm[l]) for l in range(n_a)]
    b_f_row = _row(kv_b_f, LANES)

    mem_n, mem_kv = _norm_matmul("mem_kv", mem_tokens, _row(mem_norm), wmem_cat, f32)

    saved = []
    shared = None
    for l in range(depth):
        rec = {"h0": h}
        h1, gate1, up1 = _ffn_fwd(f"ffn1_fwd_{l}", h, _row(ffn1_norm[l]), wgu1, wd1, l)
        if l < n_a:
            u, proj = _norm_matmul(f"gdn_in_{l}", h1, _row(mix_norm[l]), win[l], f32)
        else:
            u, proj = _norm_matmul(f"fox_in_{l}", h1, _row(mix_norm[l]), wfox[l - n_a], bf16)
        rec.update(h1=h1, u=u, ffn1=(gate1, up1))
        if l < n_a:
            yc = _conv_fwd(f"conv_fwd_{l}", V(proj, cb=0, w=3 * gw), conv_w[l])
            ab_view = V(proj, cb=(4 * gw + mem_w) // LANES, w=LANES)
            q, k, v, gb = _rowwise(f"gdn_pre_{l}", functools.partial(_gdn_pre, n_heads, head_dim),
                                   [yc, ab_view], [a_log_rows[l], dt_rows[l]],
                                   [(gw, f32), (gw, f32), (gw, f32), (LANES, f32)])
            o, states = _gdn_chunk_fwd(f"gdn_chunk_fwd_{l}", q, k, v, gb, n_heads, head_dim)
            z_view = V(proj, cb=3, w=gw)
            (main,) = _rowwise(f"gdn_post_{l}", functools.partial(_gdn_post, n_heads, head_dim),
                               [o, z_view], [onorm_rows[l]], [(gw, bf16)])
            qmem_view = V(proj, cb=4 * gw // mem_w, w=mem_w)
            rec.update(proj=proj, yc=yc, q=q, k=k, v=v, gb=gb, o=o, states=states)
        else:
            sk, sv, crow = shared["k"], shared["v"], shared["crow"]
            main, lse = _fox_fwd(f"fox_fwd_{l}", proj, sk, sv, crow, n_heads, head_dim)
            qmem_view = V(proj, cb=gw // mem_w, w=mem_w)
            rec.update(proj=proj, lse=lse)
        km = V(mem_kv, cb=2 * l, w=mem_w)
        vm = V(mem_kv, cb=2 * l + 1, w=mem_w)
        (mem_out,) = _rowwise(f"mem_attn_{l}", _mem_attn, [qmem_view], [km, vm], [(mem_w, bf16)])
        cat = jnp.concatenate([main, mem_out], axis=1)
        h2 = _matmul(f"out_proj_{l}", cat, wout[l], "nn", f32, add=h1)
        h3, gate2, up2 = _ffn_fwd(f"ffn2_fwd_{l}", h2, _row(ffn2_norm[l]), wgu2, wd2, l)
        rec.update(cat=cat, h2=h2, qmem=qmem_view, ffn2=(gate2, up2))
        saved.append(rec)
        h = h3
        if l == n_a - 1:
            hn, p = _norm_matmul("kv_proj", h, _row(kv_norm), wkv, f32)
            pk, pv, pf = V(p, cb=0, w=gw), V(p, cb=1, w=gw), V(p, cb=2 * gw // LANES, w=LANES)
            sk, sv, log_f = _rowwise("kv_post", functools.partial(_kv_post, n_heads), [pk, pv, pf], [b_f_row],
                                     [(gw, bf16), (gw, bf16), (LANES, f32)])
            cum = _cumsum("forget_cumsum", [log_f], reverse=False, scale=LOG2_E)
            c_heads = cum[:, :n_heads].T
            shared = dict(k=sk, v=sv, crow=c_heads.reshape(n_heads, 1, s), h=h, hn=hn, p=p, views=(pk, pv, pf))

    loss_part, dh, d_final = _final_loss("final_loss", h, target, _row(final_norm))
    loss = lax.psum(loss_part[0, 0], ("x", "y", "c"))

    grads = {}
    per_layer = {n: [None] * depth for n in ("ffn1_norm", "mix_norm", "ffn2_norm", "ffn1_gu", "ffn1_d", "ffn2_gu",
                                             "ffn2_d", "w_out")}
    per_a = {n: [None] * n_a for n in ("gdn_w_in", "gdn_conv", "gdn_A_log", "gdn_dt_bias", "gdn_out_norm")}
    per_b = {"fox_w_in": [None] * (depth - n_a)}
    d_mem_kv = [None] * depth
    fox_grads = []

    def ffn_backward(tag, l, h_in, d_out, gain, kept, wgu, wd):
        parts, dwg, dwu, dwd = _ffn_bwd(f"{tag}_bwd_{l}", h_in, d_out, _row(gain), kept[0], kept[1], wgu, wd, l)
        nh = parts.shape[0]
        (d_in,), (d_gain,) = _rowwise_vjp(
            f"{tag}_norm_bwd_{l}", lambda t, g: (_rms(t, g),), [h_in], [_row(gain)],
            [[V(parts, lead=t) for t in range(nh)]], [f32], add=d_out)
        return d_in, d_gain, jnp.concatenate([dwg, dwu], axis=0), dwd

    for l in reversed(range(depth)):
        rec = saved[l]
        if l == n_a - 1:
            dk_list = [V(g["dk"]) for g in fox_grads]
            dv_list = [V(g["dv"]) for g in fox_grads]
            dc_parts = [_pad_lanes(part.reshape(n_heads, s).T, LANES) for g in fox_grads for part in g["dc"]]
            d_log_f = _cumsum("forget_cumsum_bwd", dc_parts, reverse=True)
            pk, pv, pf = shared["views"]
            (dpk, dpv, dpf), (d_bf,) = _rowwise_vjp(
                "kv_post_bwd", functools.partial(_kv_post, n_heads), [pk, pv, pf], [b_f_row],
                [dk_list, dv_list, [d_log_f]], [bf16, bf16, bf16])
            dp = jnp.concatenate([dpk, dpv, dpf], axis=1)
            d_hn = _matmul("kv_proj_dx", dp, wkv, "nt", f32)
            grads["kv_w"] = _matmul("kv_proj_dw", shared["hn"], dp, "tn", f32)[:, :kv_width]
            (dh,), (d_kvn,) = _rowwise_vjp("kv_norm_bwd", lambda t, g: (_rms(t, g),), [shared["h"]], [_row(kv_norm)],
                                           [[d_hn]], [f32], add=dh)
            grads["kv_norm"] = d_kvn.reshape(-1)
            grads["kv_b_f"] = d_bf[0, :n_heads]

        dh2, per_layer["ffn2_norm"][l], per_layer["ffn2_gu"][l], per_layer["ffn2_d"][l] = ffn_backward(
            "ffn2", l, rec["h2"], dh, ffn2_norm[l], rec["ffn2"], wgu2, wd2)
        d_cat = _matmul(f"out_proj_dx_{l}", dh2, wout[l], "nt", f32)
        per_layer["w_out"][l] = _matmul(f"out_proj_dw_{l}", rec["cat"], dh2, "tn", f32)
        d_main = V(d_cat, cb=0, w=gw)
        d_memo = V(d_cat, cb=gw // mem_w, w=mem_w)
        km, vm = V(mem_kv, cb=2 * l, w=mem_w), V(mem_kv, cb=2 * l + 1, w=mem_w)
        (dqmem,), (dkm, dvm) = _rowwise_vjp(f"mem_attn_bwd_{l}", _mem_attn, [rec["qmem"]], [km, vm], [[d_memo]], [bf16])
        d_mem_kv[l] = jnp.concatenate([dkm, dvm], axis=1)
        if l < n_a:
            proj = rec["proj"]
            z_view = V(proj, cb=3, w=gw)
            (d_o, d_z), (d_onorm,) = _rowwise_vjp(
                f"gdn_post_bwd_{l}", functools.partial(_gdn_post, n_heads, head_dim), [rec["o"], z_view],
                [onorm_rows[l]], [[d_main]], [f32, bf16])
            dq, dk, dv, dgb = _gdn_chunk_bwd(f"gdn_chunk_bwd_{l}", rec["q"], rec["k"], rec["v"], rec["gb"],
                                             rec["states"], d_o, n_heads, head_dim)
            ab_view = V(proj, cb=(4 * gw + mem_w) // LANES, w=LANES)
            (d_yc, d_ab), (d_alog, d_dt) = _rowwise_vjp(
                f"gdn_pre_bwd_{l}", functools.partial(_gdn_pre, n_heads, head_dim), [rec["yc"], ab_view],
                [a_log_rows[l], dt_rows[l]], [[dq], [dk], [dv], [dgb]], [f32, bf16])
            d_qkv, d_conv = _conv_bwd(f"conv_bwd_{l}", V(proj, cb=0, w=3 * gw), d_yc, conv_w[l])
            d_proj = jnp.concatenate([d_qkv, d_z, dqmem, d_ab], axis=1)
            du = _matmul(f"gdn_in_dx_{l}", d_proj, win[l], "nt", f32)
            per_a["gdn_w_in"][l] = unpermute_in(_matmul(f"gdn_in_dw_{l}", rec["u"], d_proj, "tn", f32))
            per_a["gdn_conv"][l] = d_conv
            per_a["gdn_A_log"][l] = d_alog[0, :n_heads]
            per_a["gdn_dt_bias"][l] = d_dt[0, :n_heads]
            per_a["gdn_out_norm"][l] = d_onorm[0]
        else:
            proj = rec["proj"]
            sk, sv, crow = shared["k"], shared["v"], shared["crow"]
            dq, delta, dc_col = _fox_bwd_dq(f"fox_dq_{l}", proj, sk, sv, crow, rec["lse"], rec["cat"], d_cat,
                                            n_heads, head_dim)
            dk, dv, dc_row = _fox_bwd_dkv(f"fox_dkv_{l}", proj, sk, sv, crow, rec["lse"], delta, d_cat,
                                          n_heads, head_dim)
            fox_grads.append(dict(dk=dk, dv=dv, dc=(dc_row, dc_col)))
            d_proj = jnp.concatenate([dq, dqmem], axis=1)
            du = _matmul(f"fox_in_dx_{l}", d_proj, wfox[l - n_a], "nt", f32)
            per_b["fox_w_in"][l - n_a] = _matmul(f"fox_in_dw_{l}", rec["u"], d_proj, "tn", f32)
        (dh1,), (d_mix,) = _rowwise_vjp(f"mix_norm_bwd_{l}", lambda t, g: (_rms(t, g),), [rec["h1"]],
                                        [_row(mix_norm[l])], [[du]], [f32], add=dh2)
        per_layer["mix_norm"][l] = d_mix
        dh, per_layer["ffn1_norm"][l], per_layer["ffn1_gu"][l], per_layer["ffn1_d"][l] = ffn_backward(
            "ffn1", l, rec["h0"], dh1, ffn1_norm[l], rec["ffn1"], wgu1, wd1)

    grad_x = dh[None]

    d_mem_kv_cat = jnp.concatenate(d_mem_kv, axis=1)
    d_wmem_cat = _matmul("mem_kv_dw", mem_n, d_mem_kv_cat, "tn", f32)
    d_mem_n = _matmul("mem_kv_dx", d_mem_kv_cat, wmem_cat, "nt", f32)
    _, (d_memnorm,) = _rowwise_vjp("mem_norm_bwd", lambda t, g: (_rms(t, g),), [mem_tokens], [_row(mem_norm)],
                                   [[d_mem_n]], [None])

    def gu_stack(per):
        return jnp.stack(per, axis=1).astype(bf16)

    stacks = dict(
        ffn1_w_gate_up=gu_stack(per_layer["ffn1_gu"]),
        ffn1_w_down=_rows_to_shards(jnp.stack(per_layer["ffn1_d"])).astype(bf16),
        ffn2_w_gate_up=gu_stack(per_layer["ffn2_gu"]),
        ffn2_w_down=_rows_to_shards(jnp.stack(per_layer["ffn2_d"])).astype(bf16),
        gdn_w_in=_rows_to_shards(jnp.stack(per_a["gdn_w_in"])).astype(bf16),
        gdn_conv=jnp.stack(per_a["gdn_conv"]).reshape(n_a, -1, N_DEV, 3 * gw // N_DEV).transpose(2, 0, 1, 3),
        fox_w_in=_rows_to_shards(jnp.stack(per_b["fox_w_in"])).astype(bf16),
        w_out=_rows_to_shards(jnp.stack(per_layer["w_out"])).astype(bf16),
        mem_w_kv=_rows_to_shards(d_wmem_cat.reshape(d, depth, 2 * mem_w).transpose(1, 0, 2)).astype(bf16),
        kv_w=_rows_to_shards(grads["kv_w"][None])[:, 0].astype(bf16),
    )
    received = dict(zip(big_names, _reduce_exchange([stacks[n] for n in big_names])))

    small_shapes = {n: weights[n].shape for n in small_names}
    pack = _SmallPack(small_shapes)
    small_grads = dict(
        ffn1_norm=jnp.concatenate(per_layer["ffn1_norm"], axis=0), mix_norm=jnp.concatenate(per_layer["mix_norm"], axis=0),
        ffn2_norm=jnp.concatenate(per_layer["ffn2_norm"], axis=0), gdn_A_log=jnp.stack(per_a["gdn_A_log"]),
        gdn_dt_bias=jnp.stack(per_a["gdn_dt_bias"]), gdn_out_norm=jnp.stack(per_a["gdn_out_norm"]),
        mem_norm=d_memnorm.reshape(-1), kv_norm=grads["kv_norm"], kv_b_f=grads["kv_b_f"], final_norm=d_final.reshape(-1))
    (small_parts,) = _all_gather("gather_small_grads", [pack.pack(small_grads)])

    out_g, out_d, out_m, out_v = {}, {}, {}, {}
    for n in big_names:
        shape = weights[n].shape
        c = shape[-1]
        parts = received[n].reshape(N_CHIPS, -1, c)
        res = _adamw(f"adamw_{n}", parts, weights[n].reshape(-1, c), mom_m[n].reshape(-1, c), mom_v[n].reshape(-1, c))
        out_g[n], out_d[n], out_m[n], out_v[n] = [r.reshape(shape) for r in res]
    res = _adamw("adamw_small", small_parts, pack.pack({n: weights[n] for n in small_names}),
                 pack.pack({n: mom_m[n] for n in small_names}), pack.pack({n: mom_v[n] for n in small_names}))
    for dst, packed in zip((out_g, out_d, out_m, out_v), res):
        dst.update(pack.unpack(packed))

    return (loss, grad_x, *[out_g[n] for n in names], *[out_d[n] for n in names],
            *[out_m[n] for n in names], *[out_v[n] for n in names])
```
